```python
import math
import jax, jax.numpy as jnp
from jax import lax
import numpy as np

D_MODEL = 1024
BATCH = 8
SEQ = 8192
DEPTH = 1

HEAD_DIM = D_MODEL // 16
N_ATTN_HEADS = 12
N_GMLP_GROUPS = 4
GMLP_DIM = HEAD_DIM
ATTN_WIDTH = N_ATTN_HEADS * HEAD_DIM
GMLP_WIDTH = N_GMLP_GROUPS * GMLP_DIM
MIX_WIDTH = ATTN_WIDTH + GMLP_WIDTH
IN_WIDTH = 3 * ATTN_WIDTH + 2 * GMLP_WIDTH
CHUNK = 128
BLOCK = 128
DILATED_PATTERNS = ((128, 1), (512, 4), (2048, 16))
D_FF = 4 * D_MODEL
EPS = 1e-6

kernel_name = "hymba_gmlp_longnet_alibi_block"


def alibi_slopes(n):
    def pow2_slopes(m):
        start = 2.0 ** (-8.0 / m)
        return [start ** (i + 1) for i in range(m)]
    if math.log2(n).is_integer():
        s = pow2_slopes(n)
    else:
        c = 2 ** int(math.floor(math.log2(n)))
        s = pow2_slopes(c) + pow2_slopes(2 * c)[0::2][: n - c]
    return np.asarray(s, dtype=np.float32)


def rms_norm(x, g):
    xf = x.astype(jnp.float32)
    y = xf * lax.rsqrt(jnp.mean(xf * xf, axis=-1, keepdims=True) + EPS)
    return (y * g.astype(jnp.float32)).astype(x.dtype)


def layer_norm(x, g, b):
    xf = x.astype(jnp.float32)
    mu = jnp.mean(xf, axis=-1, keepdims=True)
    var = jnp.mean(jnp.square(xf - mu), axis=-1, keepdims=True)
    y = (xf - mu) * lax.rsqrt(var + EPS)
    return (y * g.astype(jnp.float32) + b.astype(jnp.float32)).astype(x.dtype)


def chunked_spatial_gating(u, z, ln_g, ln_b, w_s, b_s):
    B, S, G, C = u.shape
    u = jax.nn.gelu(u)
    z = layer_norm(jax.nn.gelu(z), ln_g, ln_b)
    zc = z.reshape(B, S // CHUNK, CHUNK, G, C)
    causal = jnp.tril(jnp.ones((CHUNK, CHUNK), dtype=w_s.dtype))
    ws = w_s * causal[None]
    mixed = jnp.einsum('gts,bnsgc->bntgc', ws, zc) + b_s.T[None, None, :, :, None]
    return u * mixed.reshape(B, S, G, C)


def dilated_window_attention(q, k, v, slopes, window, dilation):
    B, S, H, Dh = q.shape
    span = BLOCK * dilation
    S_pad = -(-S // span) * span
    pad = S_pad - S
    L = S_pad // dilation
    nb = L // BLOCK

    def to_sub(t):
        t = jnp.pad(t.astype(jnp.float32), ((0, 0), (0, pad), (0, 0), (0, 0)))
        t = t.reshape(B, L, dilation, H, Dh).transpose(0, 2, 3, 1, 4)
        return t.reshape(B, dilation, H, nb, BLOCK, Dh)

    qs, ks, vs = to_sub(q), to_sub(k), to_sub(v)
    blk_pad = ((0, 0), (0, 0), (0, 0), (1, 0), (0, 0), (0, 0))
    kb = jnp.concatenate([jnp.pad(ks, blk_pad)[:, :, :, :-1], ks], axis=4)
    vb = jnp.concatenate([jnp.pad(vs, blk_pad)[:, :, :, :-1], vs], axis=4)

    scores = jnp.einsum('brhnqd,brhnkd->brhnqk', qs, kb)
    qi = jnp.arange(BLOCK)[:, None]
    kj = jnp.arange(2 * BLOCK)[None, :]
    steps = qi + BLOCK - kj
    band = (steps >= 0) & (steps <= window // dilation)
    blk = jnp.arange(nb)[:, None, None]
    valid = band[None] & ~((blk == 0) & (kj[None] < BLOCK))
    alibi = -slopes[:, None, None] * (steps * dilation).astype(jnp.float32)[None]
    scores = scores + alibi[None, None, :, None]
    scores = jnp.where(valid[None, None, None], scores, -jnp.inf)

    m = jnp.max(scores, axis=-1, keepdims=True)
    p = jnp.exp(scores - m)
    l = jnp.sum(p, axis=-1, keepdims=True)
    o = jnp.einsum('brhnqk,brhnkd->brhnqd', p, vb) / l
    lse = (m + jnp.log(l))[..., 0]

    o = o.reshape(B, dilation, H, L, Dh).transpose(0, 3, 1, 2, 4).reshape(B, S_pad, H, Dh)[:, :S]
    lse = lse.reshape(B, dilation, H, L).transpose(0, 3, 1, 2).reshape(B, S_pad, H)[:, :S]
    return o, lse


def mixture_of_dilations(q, k, v, slopes):
    outs, lses = [], []
    for window, dilation in DILATED_PATTERNS:
        o, lse = dilated_window_attention(q, k, v, slopes, window, dilation)
        outs.append(o)
        lses.append(lse)
    w = jax.nn.softmax(jnp.stack(lses, axis=0), axis=0)
    return jnp.sum(w[..., None] * jnp.stack(outs, axis=0), axis=0)


def _fwd_setup_inputs(seed: int = 0) -> dict:
    key = jax.random.key(seed)
    ks = jax.random.split(key, 16)
    f32 = jnp.float32
    nrm = lambda k, shape, scale: jax.random.normal(k, shape, f32) * scale
    G, C = N_GMLP_GROUPS, GMLP_DIM
    return {
        "x": nrm(ks[0], (BATCH, SEQ, D_MODEL), 1.0),
        "norm1_g": 1.0 + nrm(ks[1], (DEPTH, D_MODEL), 0.02),
        "w_in": nrm(ks[2], (DEPTH, D_MODEL, IN_WIDTH), D_MODEL ** -0.5),
        "sgu_ln_g": 1.0 + nrm(ks[3], (DEPTH, G, C), 0.02),
        "sgu_ln_b": nrm(ks[4], (DEPTH, G, C), 0.02),
        "sgu_w": nrm(ks[5], (DEPTH, G, CHUNK, CHUNK), CHUNK ** -0.5),
        "sgu_b": 1.0 + nrm(ks[6], (DEPTH, G, CHUNK), 0.02),
        "attn_out_g": 1.0 + nrm(ks[7], (DEPTH, ATTN_WIDTH), 0.02),
        "gmlp_out_g": 1.0 + nrm(ks[8], (DEPTH, GMLP_WIDTH), 0.02),
        "w_out": nrm(ks[9], (DEPTH, MIX_WIDTH, D_MODEL), MIX_WIDTH ** -0.5),
        "norm2_g": 1.0 + nrm(ks[10], (DEPTH, D_MODEL), 0.02),
        "w_ff1": nrm(ks[11], (DEPTH, D_MODEL, D_FF), D_MODEL ** -0.5),
        "w_ff2": nrm(ks[12], (DEPTH, D_FF, D_MODEL), D_FF ** -0.5),
        "final_norm_g": 1.0 + nrm(ks[13], (D_MODEL,), 0.02),
    }


def _fwd_reference(x, norm1_g, w_in, sgu_ln_g, sgu_ln_b, sgu_w, sgu_b, attn_out_g, gmlp_out_g,
              w_out, norm2_g, w_ff1, w_ff2, final_norm_g):
    B, S, _ = x.shape
    slopes = jnp.asarray(alibi_slopes(N_ATTN_HEADS), dtype=jnp.float32)
    scale = HEAD_DIM ** -0.5
    A, Gw = ATTN_WIDTH, GMLP_WIDTH
    h = x
    for l in range(DEPTH):
        hn = rms_norm(h, norm1_g[l])
        proj = hn @ w_in[l]
        q = (proj[..., :A] * scale).reshape(B, S, N_ATTN_HEADS, HEAD_DIM)
        k = proj[..., A:2 * A].reshape(B, S, N_ATTN_HEADS, HEAD_DIM)
        v = proj[..., 2 * A:3 * A].reshape(B, S, N_ATTN_HEADS, HEAD_DIM)
        u = proj[..., 3 * A:3 * A + Gw].reshape(B, S, N_GMLP_GROUPS, GMLP_DIM)
        z = proj[..., 3 * A + Gw:].reshape(B, S, N_GMLP_GROUPS, GMLP_DIM)

        attn = mixture_of_dilations(q, k, v, slopes).astype(h.dtype).reshape(B, S, A)
        gmlp = chunked_spatial_gating(u, z, sgu_ln_g[l], sgu_ln_b[l], sgu_w[l], sgu_b[l]).reshape(B, S, Gw)

        mixed = jnp.concatenate([rms_norm(attn, attn_out_g[l]), rms_norm(gmlp, gmlp_out_g[l])], axis=-1)
        h = h + mixed @ w_out[l]

        hn = rms_norm(h, norm2_g[l])
        h = h + jnp.square(jax.nn.relu(hn @ w_ff1[l])) @ w_ff2[l]
    return rms_norm(h, final_norm_g)


import jax as _jax
import jax.numpy as _jnp

TWIN_FORMAT = 'train_step'
FWD_PARAMS = ['x', 'norm1_g', 'w_in', 'sgu_ln_g', 'sgu_ln_b', 'sgu_w', 'sgu_b', 'attn_out_g', 'gmlp_out_g', 'w_out', 'norm2_g', 'w_ff1', 'w_ff2', 'final_norm_g']
TWIN_WEIGHTS = ['norm1_g', 'w_in', 'sgu_ln_g', 'sgu_ln_b', 'sgu_w', 'sgu_b', 'attn_out_g', 'gmlp_out_g', 'w_out', 'norm2_g', 'w_ff1', 'w_ff2', 'final_norm_g']
TWIN_DIFF_INPUT = 'x'
TWIN_INPUTS = ['x', 'norm1_g', 'w_in', 'sgu_ln_g', 'sgu_ln_b', 'sgu_w', 'sgu_b', 'attn_out_g', 'gmlp_out_g', 'w_out', 'norm2_g', 'w_ff1', 'w_ff2', 'final_norm_g', 'loss_target', 'm_norm1_g', 'm_w_in', 'm_sgu_ln_g', 'm_sgu_ln_b', 'm_sgu_w', 'm_sgu_b', 'm_attn_out_g', 'm_gmlp_out_g', 'm_w_out', 'm_norm2_g', 'm_w_ff1', 'm_w_ff2', 'm_final_norm_g', 'v_norm1_g', 'v_w_in', 'v_sgu_ln_g', 'v_sgu_ln_b', 'v_sgu_w', 'v_sgu_b', 'v_attn_out_g', 'v_gmlp_out_g', 'v_w_out', 'v_norm2_g', 'v_w_ff1', 'v_w_ff2', 'v_final_norm_g']
TWIN_OUTPUTS = ['loss', 'grad_x', 'grad_norm1_g', 'grad_w_in', 'grad_sgu_ln_g', 'grad_sgu_ln_b', 'grad_sgu_w', 'grad_sgu_b', 'grad_attn_out_g', 'grad_gmlp_out_g', 'grad_w_out', 'grad_norm2_g', 'grad_w_ff1', 'grad_w_ff2', 'grad_final_norm_g', 'delta_norm1_g', 'delta_w_in', 'delta_sgu_ln_g', 'delta_sgu_ln_b', 'delta_sgu_w', 'delta_sgu_b', 'delta_attn_out_g', 'delta_gmlp_out_g', 'delta_w_out', 'delta_norm2_g', 'delta_w_ff1', 'delta_w_ff2', 'delta_final_norm_g', 'new_m_norm1_g', 'new_m_w_in', 'new_m_sgu_ln_g', 'new_m_sgu_ln_b', 'new_m_sgu_w', 'new_m_sgu_b', 'new_m_attn_out_g', 'new_m_gmlp_out_g', 'new_m_w_out', 'new_m_norm2_g', 'new_m_w_ff1', 'new_m_w_ff2', 'new_m_final_norm_g', 'new_v_norm1_g', 'new_v_w_in', 'new_v_sgu_ln_g', 'new_v_sgu_ln_b', 'new_v_sgu_w', 'new_v_sgu_b', 'new_v_attn_out_g', 'new_v_gmlp_out_g', 'new_v_w_out', 'new_v_norm2_g', 'new_v_w_ff1', 'new_v_w_ff2', 'new_v_final_norm_g']
TWIN_LEAF_KINDS = {'loss': 'loss', 'grad_x': 'grad_x', 'grad_norm1_g': 'grad_w', 'grad_w_in': 'grad_w', 'grad_sgu_ln_g': 'grad_w', 'grad_sgu_ln_b': 'grad_w', 'grad_sgu_w': 'grad_w', 'grad_sgu_b': 'grad_w', 'grad_attn_out_g': 'grad_w', 'grad_gmlp_out_g': 'grad_w', 'grad_w_out': 'grad_w', 'grad_norm2_g': 'grad_w', 'grad_w_ff1': 'grad_w', 'grad_w_ff2': 'grad_w', 'grad_final_norm_g': 'grad_w', 'delta_norm1_g': 'delta_w', 'delta_w_in': 'delta_w', 'delta_sgu_ln_g': 'delta_w', 'delta_sgu_ln_b': 'delta_w', 'delta_sgu_w': 'delta_w', 'delta_sgu_b': 'delta_w', 'delta_attn_out_g': 'delta_w', 'delta_gmlp_out_g': 'delta_w', 'delta_w_out': 'delta_w', 'delta_norm2_g': 'delta_w', 'delta_w_ff1': 'delta_w', 'delta_w_ff2': 'delta_w', 'delta_final_norm_g': 'delta_w', 'new_m_norm1_g': 'new_m', 'new_m_w_in': 'new_m', 'new_m_sgu_ln_g': 'new_m', 'new_m_sgu_ln_b': 'new_m', 'new_m_sgu_w': 'new_m', 'new_m_sgu_b': 'new_m', 'new_m_attn_out_g': 'new_m', 'new_m_gmlp_out_g': 'new_m', 'new_m_w_out': 'new_m', 'new_m_norm2_g': 'new_m', 'new_m_w_ff1': 'new_m', 'new_m_w_ff2': 'new_m', 'new_m_final_norm_g': 'new_m', 'new_v_norm1_g': 'new_v', 'new_v_w_in': 'new_v', 'new_v_sgu_ln_g': 'new_v', 'new_v_sgu_ln_b': 'new_v', 'new_v_sgu_w': 'new_v', 'new_v_sgu_b': 'new_v', 'new_v_attn_out_g': 'new_v', 'new_v_gmlp_out_g': 'new_v', 'new_v_w_out': 'new_v', 'new_v_norm2_g': 'new_v', 'new_v_w_ff1': 'new_v', 'new_v_w_ff2': 'new_v', 'new_v_final_norm_g': 'new_v'}


def _forward(args):
    return _fwd_reference(*[args[k] for k in FWD_PARAMS])


def _output_shape():
    def fwd():
        inp = _fwd_setup_inputs(0)
        return _fwd_reference(*[inp[k] for k in FWD_PARAMS])
    out = _jax.eval_shape(fwd)
    return out.shape, out.dtype

N_MICROBATCH = 1
ADAM_LR = 0.001
ADAM_B1 = 0.9
ADAM_B2 = 0.999
ADAM_EPS = 1e-08
ADAM_WD = 0.01
ADAM_STEP = 10
PER_EXAMPLE_BATCH_AXIS = {'x': 0, 'loss_target': 0}
SHARED_INPUTS = []
_WEIGHT_DTYPES = {'norm1_g': _jnp.float32, 'w_in': _jnp.float32, 'sgu_ln_g': _jnp.float32, 'sgu_ln_b': _jnp.float32, 'sgu_w': _jnp.float32, 'sgu_b': _jnp.float32, 'attn_out_g': _jnp.float32, 'gmlp_out_g': _jnp.float32, 'w_out': _jnp.float32, 'norm2_g': _jnp.float32, 'w_ff1': _jnp.float32, 'w_ff2': _jnp.float32, 'final_norm_g': _jnp.float32}
MOMENT_SCALE = {'norm1_g': 2.389549e-01, 'w_in': 1.499615e-01, 'sgu_ln_g': 1.094274e-01, 'sgu_ln_b': 1.150662e-01, 'sgu_w': 7.398225e-02, 'sgu_b': 1.007879e-01, 'attn_out_g': 1.992511e-01, 'gmlp_out_g': 1.848619e-01, 'w_out': 1.967079e-01, 'norm2_g': 1.940266e-01, 'w_ff1': 9.622692e-02, 'w_ff2': 2.068376e-01, 'final_norm_g': 6.449329e+01}


def _to_microbatches(a, axis):
    t = _jnp.moveaxis(a, axis, 0)
    t = t.reshape((N_MICROBATCH, t.shape[0] // N_MICROBATCH) + t.shape[1:])
    return _jnp.moveaxis(t, 1, axis + 1)


def setup_inputs(seed: int = 0) -> dict:
    inp = _fwd_setup_inputs(seed)
    key = _jax.random.fold_in(_jax.random.key(seed), 7919)
    shape, _ = _output_shape()
    out = dict(inp)
    out["loss_target"] = _jax.random.normal(_jax.random.fold_in(key, 0), shape, _jnp.float32)
    for i, name in enumerate(TWIN_WEIGHTS):
        w = inp[name].astype(_jnp.float32)
        if MOMENT_SCALE is None:
            s = _jnp.sqrt(_jnp.mean(_jnp.square(w)) + 1e-30)
        else:
            s = MOMENT_SCALE[name]
        km, kv = _jax.random.split(_jax.random.fold_in(key, i + 1))
        out[name] = w
        out["m_" + name] = s * _jax.random.normal(km, w.shape, _jnp.float32)
        out["v_" + name] = (s * s) * _jax.random.uniform(kv, w.shape, _jnp.float32, 0.5, 1.5)
    if N_MICROBATCH > 1:
        for name, axis in PER_EXAMPLE_BATCH_AXIS.items():
            out[name] = _to_microbatches(out[name], axis)
    return {'x': out['x'], 'norm1_g': out['norm1_g'], 'w_in': out['w_in'], 'sgu_ln_g': out['sgu_ln_g'], 'sgu_ln_b': out['sgu_ln_b'], 'sgu_w': out['sgu_w'], 'sgu_b': out['sgu_b'], 'attn_out_g': out['attn_out_g'], 'gmlp_out_g': out['gmlp_out_g'], 'w_out': out['w_out'], 'norm2_g': out['norm2_g'], 'w_ff1': out['w_ff1'], 'w_ff2': out['w_ff2'], 'final_norm_g': out['final_norm_g'], 'loss_target': out['loss_target'], 'm_norm1_g': out['m_norm1_g'], 'm_w_in': out['m_w_in'], 'm_sgu_ln_g': out['m_sgu_ln_g'], 'm_sgu_ln_b': out['m_sgu_ln_b'], 'm_sgu_w': out['m_sgu_w'], 'm_sgu_b': out['m_sgu_b'], 'm_attn_out_g': out['m_attn_out_g'], 'm_gmlp_out_g': out['m_gmlp_out_g'], 'm_w_out': out['m_w_out'], 'm_norm2_g': out['m_norm2_g'], 'm_w_ff1': out['m_w_ff1'], 'm_w_ff2': out['m_w_ff2'], 'm_final_norm_g': out['m_final_norm_g'], 'v_norm1_g': out['v_norm1_g'], 'v_w_in': out['v_w_in'], 'v_sgu_ln_g': out['v_sgu_ln_g'], 'v_sgu_ln_b': out['v_sgu_ln_b'], 'v_sgu_w': out['v_sgu_w'], 'v_sgu_b': out['v_sgu_b'], 'v_attn_out_g': out['v_attn_out_g'], 'v_gmlp_out_g': out['v_gmlp_out_g'], 'v_w_out': out['v_w_out'], 'v_norm2_g': out['v_norm2_g'], 'v_w_ff1': out['v_w_ff1'], 'v_w_ff2': out['v_w_ff2'], 'v_final_norm_g': out['v_final_norm_g']}


def _loss(weights, diff, rest, loss_target):
    with _jax.named_scope("forward"):
        args = {**rest, TWIN_DIFF_INPUT: diff, **{k: w.astype(_WEIGHT_DTYPES[k]) for k, w in weights.items()}}
        y = _forward(args)
    with _jax.named_scope("loss_head"):
        err = _jnp.square(y.astype(_jnp.float32) - loss_target)
        return 0.5 * _jnp.sum(_jnp.mean(err, axis=-1)) if err.ndim else 0.5 * err


def _adamw(w, g, m, v):
    m = ADAM_B1 * m + (1.0 - ADAM_B1) * g
    v = ADAM_B2 * v + (1.0 - ADAM_B2) * _jnp.square(g)
    m_hat = m / (1.0 - ADAM_B1 ** ADAM_STEP)
    v_hat = v / (1.0 - ADAM_B2 ** ADAM_STEP)
    delta = -ADAM_LR * (m_hat / (_jnp.sqrt(v_hat) + ADAM_EPS) + ADAM_WD * w)
    return delta, m, v


def reference(x, norm1_g, w_in, sgu_ln_g, sgu_ln_b, sgu_w, sgu_b, attn_out_g, gmlp_out_g, w_out, norm2_g, w_ff1, w_ff2, final_norm_g, loss_target, m_norm1_g, m_w_in, m_sgu_ln_g, m_sgu_ln_b, m_sgu_w, m_sgu_b, m_attn_out_g, m_gmlp_out_g, m_w_out, m_norm2_g, m_w_ff1, m_w_ff2, m_final_norm_g, v_norm1_g, v_w_in, v_sgu_ln_g, v_sgu_ln_b, v_sgu_w, v_sgu_b, v_attn_out_g, v_gmlp_out_g, v_w_out, v_norm2_g, v_w_ff1, v_w_ff2, v_final_norm_g):
    given = dict(x=x, norm1_g=norm1_g, w_in=w_in, sgu_ln_g=sgu_ln_g, sgu_ln_b=sgu_ln_b, sgu_w=sgu_w, sgu_b=sgu_b, attn_out_g=attn_out_g, gmlp_out_g=gmlp_out_g, w_out=w_out, norm2_g=norm2_g, w_ff1=w_ff1, w_ff2=w_ff2, final_norm_g=final_norm_g, loss_target=loss_target, m_norm1_g=m_norm1_g, m_w_in=m_w_in, m_sgu_ln_g=m_sgu_ln_g, m_sgu_ln_b=m_sgu_ln_b, m_sgu_w=m_sgu_w, m_sgu_b=m_sgu_b, m_attn_out_g=m_attn_out_g, m_gmlp_out_g=m_gmlp_out_g, m_w_out=m_w_out, m_norm2_g=m_norm2_g, m_w_ff1=m_w_ff1, m_w_ff2=m_w_ff2, m_final_norm_g=m_final_norm_g, v_norm1_g=v_norm1_g, v_w_in=v_w_in, v_sgu_ln_g=v_sgu_ln_g, v_sgu_ln_b=v_sgu_ln_b, v_sgu_w=v_sgu_w, v_sgu_b=v_sgu_b, v_attn_out_g=v_attn_out_g, v_gmlp_out_g=v_gmlp_out_g, v_w_out=v_w_out, v_norm2_g=v_norm2_g, v_w_ff1=v_w_ff1, v_w_ff2=v_w_ff2, v_final_norm_g=v_final_norm_g)
    weights = {n: given[n] for n in TWIN_WEIGHTS}
    shared = {n: given[n] for n in SHARED_INPUTS}
    per_example = {n: given[n] for n in ['x']}
    grad_fn = _jax.value_and_grad(_loss, argnums=(0, 1))

    def one_microbatch(ex, loss_target):
        ex = dict(ex)
        diff = ex.pop(TWIN_DIFF_INPUT)
        return grad_fn(weights, diff, {**shared, **ex}, loss_target)

    if N_MICROBATCH == 1:
        loss, (grad_w, grad_x) = one_microbatch(per_example, given["loss_target"])
    else:
        def body(carry, xs):
            loss_sum, grad_sum = carry
            l_k, (gw_k, gx_k) = one_microbatch(xs[0], xs[1])
            with _jax.named_scope("update"):
                return (loss_sum + l_k, _jax.tree.map(_jnp.add, grad_sum, gw_k)), gx_k

        init = (_jnp.zeros((), _jnp.float32), _jax.tree.map(_jnp.zeros_like, weights))
        (loss, grad_w), grad_x = _jax.lax.scan(body, init, (per_example, given["loss_target"]))
    with _jax.named_scope("update"):
        delta_w, new_m, new_v = {}, {}, {}
        for n in TWIN_WEIGHTS:
            delta_w[n], new_m[n], new_v[n] = _adamw(weights[n], grad_w[n], given["m_" + n], given["v_" + n])
    return (loss, grad_x, *[grad_w[n] for n in TWIN_WEIGHTS], *[delta_w[n] for n in TWIN_WEIGHTS],
            *[new_m[n] for n in TWIN_WEIGHTS], *[new_v[n] for n in TWIN_WEIGHTS])
```

```python
import functools
import math

import numpy as np
import jax
import jax.numpy as jnp
from jax import lax
from jax.experimental import pallas as pl
from jax.experimental.pallas import tpu as pltpu

F32 = jnp.float32
BF16 = jnp.bfloat16

D_MODEL = 1024
HEAD_DIM = 64
N_HEADS = 12
ATTN_W = N_HEADS * HEAD_DIM
N_GROUPS = 4
GMLP_W = N_GROUPS * HEAD_DIM
IN_W = 3 * ATTN_W + 2 * GMLP_W
D_FF = 4 * D_MODEL
CHUNK = 128
DILATIONS = (1, 4, 16)
EPS = 1e-6
Q_SCALE = HEAD_DIM ** -0.5
NEG = -1e30

ADAM_LR, ADAM_B1, ADAM_B2, ADAM_EPS, ADAM_WD, ADAM_STEP = 0.001, 0.9, 0.999, 1e-08, 0.01, 10

N_DEV = 8
LANES = 128
VMEM_LIMIT = 56 << 20
PACK_ROWS = (IN_W + D_MODEL + D_FF + D_FF) * D_MODEL // N_DEV // D_MODEL
SMALL_ROWS = 552

MESH = pl.DeviceIdType.MESH


def _alibi_slopes(n):
    def pow2(m):
        start = 2.0 ** (-8.0 / m)
        return [start ** (i + 1) for i in range(m)]
    c = 2 ** int(math.floor(math.log2(n)))
    s = pow2(n) if c == n else pow2(c) + pow2(2 * c)[0::2][: n - c]
    return np.asarray(s, dtype=np.float32)


SLOPES = _alibi_slopes(N_HEADS)


def _params(sem=None):
    kw = dict(vmem_limit_bytes=VMEM_LIMIT)
    if sem is not None:
        kw["dimension_semantics"] = sem
    return pltpu.CompilerParams(**kw)


def _rows(tm, n):
    return pl.BlockSpec((tm, n), lambda i: (i, 0))


def _resident(shape):
    return pl.BlockSpec(shape, lambda *_: (0,) * len(shape), pipeline_mode=pl.Buffered(1))


def _rms(x):
    r = lax.rsqrt(jnp.mean(x * x, axis=-1, keepdims=True) + EPS)
    return x * r, r


def _rms_bwd(n, r, g, dy):
    dn = dy * g
    return r * (dn - n * jnp.mean(dn * n, axis=-1, keepdims=True))


def _accum_rows(acc_ref, v):
    acc_ref[...] += jnp.broadcast_to(jnp.sum(v, axis=0, keepdims=True), acc_ref.shape)


_G0 = math.sqrt(2.0 / math.pi)
_G1 = 0.044715


def _gelu(x):
    t = jnp.tanh(_G0 * (x + _G1 * (x * x * x)))
    return x * (0.5 * (1.0 + t)), t


def _gelu_grad(x, t):
    return 0.5 * (1.0 + t) + 0.5 * x * (1.0 - t * t) * (_G0 * (1.0 + 3.0 * _G1 * x * x))


NT = (((1,), (1,)), ((), ()))
TN = (((0,), (0,)), ((), ()))


def _dot(a, b, dims=None):
    if dims is None:
        return jnp.dot(a, b, preferred_element_type=F32)
    return lax.dot_general(a, b, dims, preferred_element_type=F32)


def _proj_fwd(x, g1, w_in):
    T = x.shape[0]
    tm = 256

    def body(x_ref, g_ref, w_ref, hn_ref, q_ref, k_ref, v_ref, u_ref, z_ref):
        n, _ = _rms(x_ref[...])
        hn = (n * g_ref[...]).astype(BF16)
        hn_ref[...] = hn
        a = ATTN_W
        q_ref[...] = (_dot(hn, w_ref[:, 0:a]) * Q_SCALE).astype(BF16)
        k_ref[...] = _dot(hn, w_ref[:, a:2 * a]).astype(BF16)
        v_ref[...] = _dot(hn, w_ref[:, 2 * a:3 * a]).astype(BF16)
        u_ref[...] = _dot(hn, w_ref[:, 3 * a:3 * a + GMLP_W])
        z_ref[...] = _dot(hn, w_ref[:, 3 * a + GMLP_W:])

    sds = jax.ShapeDtypeStruct
    return pl.pallas_call(
        body, name="proj_fwd", grid=(T // tm,),
        in_specs=[_rows(tm, D_MODEL), _resident((1, D_MODEL)), _resident((D_MODEL, IN_W))],
        out_specs=[_rows(tm, D_MODEL), _rows(tm, ATTN_W), _rows(tm, ATTN_W), _rows(tm, ATTN_W),
                   _rows(tm, GMLP_W), _rows(tm, GMLP_W)],
        out_shape=[sds((T, D_MODEL), BF16), sds((T, ATTN_W), BF16), sds((T, ATTN_W), BF16),
                   sds((T, ATTN_W), BF16), sds((T, GMLP_W), F32), sds((T, GMLP_W), F32)],
        compiler_params=_params(("parallel",)),
    )(x, g1, w_in)


def _attn_geometry(T, d):
    L = T // d
    tq = min(512, L)
    return L, tq, tq // CHUNK, L // tq


def _slope_rows(d):
    row = np.tile(np.repeat(SLOPES * d, HEAD_DIM), d)
    return jnp.asarray(np.broadcast_to(row[None], (8, row.shape[0])), F32)


def _band():
    qi = lax.broadcasted_iota(jnp.int32, (CHUNK, 2 * CHUNK), 0)
    kj = lax.broadcasted_iota(jnp.int32, (CHUNK, 2 * CHUNK), 1)
    steps = qi + CHUNK - kj
    return kj, (steps >= 0) & (steps <= CHUNK), steps.astype(F32)


def _attn_specs(tq, nbq):
    cur = pl.BlockSpec((tq, LANES), lambda c, t: (t, c))
    prev = pl.BlockSpec((CHUNK, LANES), lambda c, t: (jnp.maximum(t * nbq - 1, 0), c))
    slope = pl.BlockSpec((8, LANES), lambda c, t: (0, c))
    return cur, prev, slope


def _kv_window(b, t, kc_ref, kp_ref, vc_ref, vp_ref, kj, band):
    if b == 0:
        kcat = jnp.concatenate([kp_ref[...], kc_ref[0:CHUNK, :]], axis=0)
        vcat = jnp.concatenate([vp_ref[...], vc_ref[0:CHUNK, :]], axis=0)
        valid = band & ((kj + jnp.where(t > 0, CHUNK, 0)) >= CHUNK)
    else:
        kcat = kc_ref[(b - 1) * CHUNK:(b + 1) * CHUNK, :]
        vcat = vc_ref[(b - 1) * CHUNK:(b + 1) * CHUNK, :]
        valid = band
    return kcat, vcat, valid


def _attn_fwd(q, k, v, d):
    T = q.shape[0]
    L, tq, nbq, nt = _attn_geometry(T, d)
    qv, kv, vv = (a.reshape(L, d * ATTN_W) for a in (q, k, v))

    def body(sl_ref, q_ref, kc_ref, kp_ref, vc_ref, vp_ref, o_ref, l_ref):
        t = pl.program_id(1)
        kj, band, stepsf = _band()
        head0 = lax.broadcasted_iota(jnp.int32, (CHUNK, LANES), 1) < HEAD_DIM
        sl = sl_ref[0:1, :]
        for b in range(nbq):
            kcat, vcat, valid = _kv_window(b, t, kc_ref, kp_ref, vc_ref, vp_ref, kj, band)
            rows = slice(b * CHUNK, (b + 1) * CHUNK)
            qb = q_ref[rows, :]
            outs, lses = [], []
            for h in range(2):
                hm = head0 if h == 0 else jnp.logical_not(head0)
                qm = jnp.where(hm, qb, jnp.zeros_like(qb))
                s = _dot(qm, kcat, NT)
                s = jnp.where(valid, s - sl[:, h * HEAD_DIM:h * HEAD_DIM + 1] * stepsf, NEG)
                m = jnp.max(s, axis=-1, keepdims=True)
                p = jnp.exp(s - m)
                l = jnp.sum(p, axis=-1, keepdims=True)
                outs.append(_dot(p.astype(BF16), vcat) / l)
                lses.append(jnp.broadcast_to(m + jnp.log(l), (CHUNK, LANES)))
            o_ref[rows, :] = jnp.where(head0, outs[0], outs[1])
            l_ref[rows, :] = jnp.where(head0, lses[0], lses[1])

    cur, prev, slope = _attn_specs(tq, nbq)
    o, lse = pl.pallas_call(
        body, name=f"attn_fwd_d{d}", grid=(d * (ATTN_W // LANES), nt),
        in_specs=[slope, cur, cur, prev, cur, prev],
        out_specs=[cur, cur],
        out_shape=[jax.ShapeDtypeStruct((L, d * ATTN_W), F32)] * 2,
        compiler_params=_params(("parallel", "arbitrary")),
    )(_slope_rows(d), qv, kv, kv, vv, vv)
    return o.reshape(T, ATTN_W), lse.reshape(T, ATTN_W)


def _attn_combine(outs, lses):
    T = outs[0].shape[0]
    tm = 512

    def body(o1, l1, o2, l2, o3, l3, attn_ref, lse_ref):
        a, b, c = l1[...], l2[...], l3[...]
        m = jnp.maximum(jnp.maximum(a, b), c)
        ea, eb, ec = jnp.exp(a - m), jnp.exp(b - m), jnp.exp(c - m)
        tot = ea + eb + ec
        attn_ref[...] = (ea * o1[...] + eb * o2[...] + ec * o3[...]) / tot
        lse_ref[...] = m + jnp.log(tot)

    spec = _rows(tm, ATTN_W)
    return pl.pallas_call(
        body, name="attn_combine", grid=(T // tm,),
        in_specs=[spec] * 6, out_specs=[spec, spec],
        out_shape=[jax.ShapeDtypeStruct((T, ATTN_W), F32)] * 2,
        compiler_params=_params(("parallel",)),
    )(outs[0], lses[0], outs[1], lses[1], outs[2], lses[2])


def _group_mean(v, grp):
    out = jnp.zeros_like(v)
    for g in range(N_GROUPS):
        mk = grp == g
        s = jnp.sum(jnp.where(mk, v, 0.0), axis=-1, keepdims=True) * (1.0 / HEAD_DIM)
        out = jnp.where(mk, s, out)
    return out


def _gmlp_core(uu, zz, lg, lb, ws, sb_ref, grp):
    ug, tu = _gelu(uu)
    zg, tz = _gelu(zz)
    zc = zg - _group_mean(zg, grp)
    rstd = lax.rsqrt(_group_mean(zc * zc, grp) + EPS)
    xhat = zc * rstd
    zn16 = (xhat * lg + lb).astype(BF16)
    mixed = jnp.zeros_like(uu)
    for g in range(N_GROUPS):
        mixed = jnp.where(grp == g, _dot(ws[g], zn16) + sb_ref[:, g:g + 1], mixed)
    return ug, tu, tz, xhat, rstd, zn16, mixed


def _causal_ws(w_ref):
    ti = lax.broadcasted_iota(jnp.int32, (CHUNK, CHUNK), 0)
    si = lax.broadcasted_iota(jnp.int32, (CHUNK, CHUNK), 1)
    causal = si <= ti
    return causal, [jnp.where(causal, w_ref[g], 0.0).astype(BF16) for g in range(N_GROUPS)]


def _gmlp_fwd(u, z, ln_g, ln_b, sgu_w, sgu_bt):
    T = u.shape[0]
    tg = 512

    def body(u_ref, z_ref, g_ref, b_ref, w_ref, sb_ref, out_ref):
        grp = lax.broadcasted_iota(jnp.int32, (CHUNK, GMLP_W), 1) // HEAD_DIM
        _, ws = _causal_ws(w_ref)
        for ci in range(tg // CHUNK):
            rows = slice(ci * CHUNK, (ci + 1) * CHUNK)
            ug, _, _, _, _, _, mixed = _gmlp_core(u_ref[rows, :], z_ref[rows, :], g_ref[...], b_ref[...],
                                                  ws, sb_ref, grp)
            out_ref[rows, :] = ug * mixed

    return pl.pallas_call(
        body, name="gmlp_fwd", grid=(T // tg,),
        in_specs=[_rows(tg, GMLP_W), _rows(tg, GMLP_W), _resident((1, GMLP_W)), _resident((1, GMLP_W)),
                  _resident((N_GROUPS, CHUNK, CHUNK)), _resident((CHUNK, N_GROUPS))],
        out_specs=_rows(tg, GMLP_W),
        out_shape=jax.ShapeDtypeStruct((T, GMLP_W), F32),
        compiler_params=_params(("parallel",)),
    )(u, z, ln_g, ln_b, sgu_w, sgu_bt)


def _out_fwd(attn, gm, ga, gg, w_out, x, g2):
    T = x.shape[0]
    tm = 256

    def body(a_ref, m_ref, ga_ref, gg_ref, w_ref, x_ref, g2_ref, mix_ref, h1_ref, hn2_ref):
        an, _ = _rms(a_ref[...])
        gn, _ = _rms(m_ref[...])
        an = (an * ga_ref[...]).astype(BF16)
        gn = (gn * gg_ref[...]).astype(BF16)
        mix_ref[:, 0:ATTN_W] = an
        mix_ref[:, ATTN_W:] = gn
        h1 = x_ref[...] + _dot(an, w_ref[0:ATTN_W, :]) + _dot(gn, w_ref[ATTN_W:, :])
        h1_ref[...] = h1
        n2, _ = _rms(h1)
        hn2_ref[...] = (n2 * g2_ref[...]).astype(BF16)

    sds = jax.ShapeDtypeStruct
    return pl.pallas_call(
        body, name="out_fwd", grid=(T // tm,),
        in_specs=[_rows(tm, ATTN_W), _rows(tm, GMLP_W), _resident((1, ATTN_W)), _resident((1, GMLP_W)),
                  _resident((D_MODEL, D_MODEL)), _rows(tm, D_MODEL), _resident((1, D_MODEL))],
        out_specs=[_rows(tm, D_MODEL)] * 3,
        out_shape=[sds((T, D_MODEL), BF16), sds((T, D_MODEL), F32), sds((T, D_MODEL), BF16)],
        compiler_params=_params(("parallel",)),
    )(attn, gm, ga, gg, w_out, x, g2)


FF_CHUNK = 512


def _ffn_fwd(hn2, h1, w1, w2, gf, tgt):
    T = h1.shape[0]
    tm = 256

    def body(hn_ref, h1_ref, w1_ref, w2_ref, gf_ref, t_ref, a_ref, dhf_ref, dhb_ref, loss_ref, dgf_ref):
        i = pl.program_id(0)

        @pl.when(i == 0)
        def _():
            loss_ref[...] = jnp.zeros_like(loss_ref)
            dgf_ref[...] = jnp.zeros_like(dgf_ref)

        hn = hn_ref[...]
        acc = h1_ref[...]
        for j in range(D_FF // FF_CHUNK):
            cols = slice(j * FF_CHUNK, (j + 1) * FF_CHUNK)
            a = _dot(hn, w1_ref[:, cols])
            a_ref[:, cols] = a
            act = jnp.square(jnp.maximum(a, 0.0)).astype(BF16)
            acc = acc + _dot(act, w2_ref[cols, :])
        n3, r3 = _rms(acc)
        gf_row = gf_ref[...]
        e = n3 * gf_row - t_ref[...]
        loss_ref[...] += 0.5 * jnp.sum(jnp.mean(e * e, axis=-1, keepdims=True))
        dy = e * (1.0 / D_MODEL)
        _accum_rows(dgf_ref, dy * n3)
        dh2 = _rms_bwd(n3, r3, gf_row, dy)
        dhf_ref[...] = dh2
        dhb_ref[...] = dh2.astype(BF16)

    sds = jax.ShapeDtypeStruct
    acc_spec = lambda n: pl.BlockSpec((8, n), lambda i: (0, 0))
    return pl.pallas_call(
        body, name="ffn_fwd", grid=(T // tm,),
        in_specs=[_rows(tm, D_MODEL), _rows(tm, D_MODEL), _resident((D_MODEL, D_FF)), _resident((D_FF, D_MODEL)),
                  _resident((1, D_MODEL)), _rows(tm, D_MODEL)],
        out_specs=[_rows(tm, D_FF), _rows(tm, D_MODEL), _rows(tm, D_MODEL), acc_spec(LANES), acc_spec(D_MODEL)],
        out_shape=[sds((T, D_FF), F32), sds((T, D_MODEL), F32), sds((T, D_MODEL), BF16),
                   sds((8, LANES), F32), sds((8, D_MODEL), F32)],
        compiler_params=_params(("arbitrary",)),
    )(hn2, h1, w1, w2, gf, tgt)


def _ffn_bwd(dh2b, dh2f, a, h1, g2, w2t, w1t):
    T = h1.shape[0]
    tm = 256

    def body(db_ref, df_ref, a_ref, h1_ref, g2_ref, w2t_ref, w1t_ref, da_ref, act_ref, d1f_ref, d1b_ref, dg_ref):
        @pl.when(pl.program_id(0) == 0)
        def _():
            dg_ref[...] = jnp.zeros_like(dg_ref)

        db = db_ref[...]
        acc = jnp.zeros((tm, D_MODEL), F32)
        for j in range(D_FF // FF_CHUNK):
            cols = slice(j * FF_CHUNK, (j + 1) * FF_CHUNK)
            r = jnp.maximum(a_ref[:, cols], 0.0)
            da = (_dot(db, w2t_ref[:, cols]) * (2.0 * r)).astype(BF16)
            da_ref[:, cols] = da
            act_ref[:, cols] = (r * r).astype(BF16)
            acc = acc + _dot(da, w1t_ref[cols, :])
        n2, r2 = _rms(h1_ref[...])
        _accum_rows(dg_ref, acc * n2)
        dh1 = df_ref[...] + _rms_bwd(n2, r2, g2_ref[...], acc)
        d1f_ref[...] = dh1
        d1b_ref[...] = dh1.astype(BF16)

    sds = jax.ShapeDtypeStruct
    return pl.pallas_call(
        body, name="ffn_bwd", grid=(T // tm,),
        in_specs=[_rows(tm, D_MODEL), _rows(tm, D_MODEL), _rows(tm, D_FF), _rows(tm, D_MODEL),
                  _resident((1, D_MODEL)), _resident((D_MODEL, D_FF)), _resident((D_FF, D_MODEL))],
        out_specs=[_rows(tm, D_FF), _rows(tm, D_FF), _rows(tm, D_MODEL), _rows(tm, D_MODEL),
                   pl.BlockSpec((8, D_MODEL), lambda i: (0, 0))],
        out_shape=[sds((T, D_FF), BF16), sds((T, D_FF), BF16), sds((T, D_MODEL), F32), sds((T, D_MODEL), BF16),
                   sds((8, D_MODEL), F32)],
        compiler_params=_params(("arbitrary",)),
    )(dh2b, dh2f, a, h1, g2, w2t, w1t)


def _out_bwd(dh1b, w_out_t, attn, gm, ga, gg):
    T = attn.shape[0]
    tm = 256

    def body(d_ref, w_ref, a_ref, m_ref, ga_ref, gg_ref, da_ref, dm_ref, dga_ref, dgg_ref):
        @pl.when(pl.program_id(0) == 0)
        def _():
            dga_ref[...] = jnp.zeros_like(dga_ref)
            dgg_ref[...] = jnp.zeros_like(dgg_ref)

        d = d_ref[...]
        dan = _dot(d, w_ref[:, 0:ATTN_W])
        dgn = _dot(d, w_ref[:, ATTN_W:])
        na, ra = _rms(a_ref[...])
        ng, rg = _rms(m_ref[...])
        _accum_rows(dga_ref, dan * na)
        _accum_rows(dgg_ref, dgn * ng)
        da_ref[...] = _rms_bwd(na, ra, ga_ref[...], dan)
        dm_ref[...] = _rms_bwd(ng, rg, gg_ref[...], dgn)

    sds = jax.ShapeDtypeStruct
    return pl.pallas_call(
        body, name="out_bwd", grid=(T // tm,),
        in_specs=[_rows(tm, D_MODEL), _resident((D_MODEL, D_MODEL)), _rows(tm, ATTN_W), _rows(tm, GMLP_W),
                  _resident((1, ATTN_W)), _resident((1, GMLP_W))],
        out_specs=[_rows(tm, ATTN_W), _rows(tm, GMLP_W), pl.BlockSpec((8, ATTN_W), lambda i: (0, 0)),
                   pl.BlockSpec((8, GMLP_W), lambda i: (0, 0))],
        out_shape=[sds((T, ATTN_W), F32), sds((T, GMLP_W), F32), sds((8, ATTN_W), F32), sds((8, GMLP_W), F32)],
        compiler_params=_params(("arbitrary",)),
    )(dh1b, w_out_t, attn, gm, ga, gg)


def _gmlp_bwd(u, z, dgm, ln_g, ln_b, sgu_w, sgu_bt):
    T = u.shape[0]
    tg = 512
    nsteps = T // tg

    def body(u_ref, z_ref, d_ref, g_ref, b_ref, w_ref, sb_ref, du_ref, dz_ref, dlg_ref, dlb_ref, dw_ref, dsb_ref):
        i = pl.program_id(0)

        @pl.when(i == 0)
        def _():
            for ref in (dlg_ref, dlb_ref, dw_ref, dsb_ref):
                ref[...] = jnp.zeros_like(ref)

        grp = lax.broadcasted_iota(jnp.int32, (CHUNK, GMLP_W), 1) // HEAD_DIM
        lane = lax.broadcasted_iota(jnp.int32, (CHUNK, LANES), 1)
        causal, ws = _causal_ws(w_ref)
        lg = g_ref[...]
        for ci in range(tg // CHUNK):
            rows = slice(ci * CHUNK, (ci + 1) * CHUNK)
            uu, zz = u_ref[rows, :], z_ref[rows, :]
            ug, tu, tz, xhat, rstd, zn16, mixed = _gmlp_core(uu, zz, lg, b_ref[...], ws, sb_ref, grp)
            dgm_c = d_ref[rows, :]
            dmx = dgm_c * ug
            du_ref[rows, :] = dgm_c * mixed * _gelu_grad(uu, tu)
            dmx16 = dmx.astype(BF16)
            dzn = jnp.zeros_like(dmx)
            dsb = jnp.zeros((CHUNK, LANES), F32)
            for g in range(N_GROUPS):
                mk = grp == g
                dzn = jnp.where(mk, _dot(ws[g], dmx16, TN), dzn)
                dw_ref[g] += _dot(jnp.where(mk, dmx16, jnp.zeros_like(dmx16)), zn16, NT)
                dsb = jnp.where(lane == g, jnp.sum(jnp.where(mk, dmx, 0.0), axis=-1, keepdims=True), dsb)
            dsb_ref[...] += dsb
            _accum_rows(dlg_ref, dzn * xhat)
            _accum_rows(dlb_ref, dzn)
            dxh = dzn * lg
            dzg = rstd * (dxh - _group_mean(dxh, grp) - xhat * _group_mean(dxh * xhat, grp))
            dz_ref[rows, :] = dzg * _gelu_grad(zz, tz)

        @pl.when(i == nsteps - 1)
        def _():
            for g in range(N_GROUPS):
                dw_ref[g] = jnp.where(causal, dw_ref[g], 0.0)

    sds = jax.ShapeDtypeStruct
    return pl.pallas_call(
        body, name="gmlp_bwd", grid=(nsteps,),
        in_specs=[_rows(tg, GMLP_W)] * 3 + [_resident((1, GMLP_W)), _resident((1, GMLP_W)),
                                              _resident((N_GROUPS, CHUNK, CHUNK)), _resident((CHUNK, N_GROUPS))],
        out_specs=[_rows(tg, GMLP_W), _rows(tg, GMLP_W), pl.BlockSpec((8, GMLP_W), lambda i: (0, 0)),
                   pl.BlockSpec((8, GMLP_W), lambda i: (0, 0)),
                   pl.BlockSpec((N_GROUPS, CHUNK, CHUNK), lambda i: (0, 0, 0)),
                   pl.BlockSpec((CHUNK, LANES), lambda i: (0, 0))],
        out_shape=[sds((T, GMLP_W), F32), sds((T, GMLP_W), F32), sds((8, GMLP_W), F32), sds((8, GMLP_W), F32),
                   sds((N_GROUPS, CHUNK, CHUNK), F32), sds((CHUNK, LANES), F32)],
        compiler_params=_params(("arbitrary",)),
    )(u, z, dgm, ln_g, ln_b, sgu_w, sgu_bt)


def _attn_bwd(q, k, v, dattn, attn, lse, d):
    T = q.shape[0]
    L, tq, nbq, nt = _attn_geometry(T, d)
    view = lambda a: a.reshape(L, d * ATTN_W)

    def body(sl_ref, q_ref, kc_ref, kp_ref, vc_ref, vp_ref, do_ref, o_ref, lse_ref, dq_ref, dk_ref, dv_ref):
        t = pl.program_id(1)

        @pl.when(t == 0)
        def _():
            dk_ref[...] = jnp.zeros_like(dk_ref)
            dv_ref[...] = jnp.zeros_like(dv_ref)

        kj, band, stepsf = _band()
        head0 = lax.broadcasted_iota(jnp.int32, (CHUNK, LANES), 1) < HEAD_DIM
        head0_kv = lax.broadcasted_iota(jnp.int32, (2 * CHUNK, LANES), 1) < HEAD_DIM
        sl = sl_ref[0:1, :]
        for b in range(nbq):
            kcat, vcat, valid = _kv_window(b, t, kc_ref, kp_ref, vc_ref, vp_ref, kj, band)
            rows = slice(b * CHUNK, (b + 1) * CHUNK)
            qb = q_ref[rows, :]
            dob = do_ref[rows, :]
            dd = dob * o_ref[rows, :]
            lseb = lse_ref[rows, :]
            dob16 = dob.astype(BF16)
            parts = []
            for h in range(2):
                hm = head0 if h == 0 else jnp.logical_not(head0)
                qm = jnp.where(hm, qb, jnp.zeros_like(qb))
                dom = jnp.where(hm, dob16, jnp.zeros_like(dob16))
                delta = jnp.sum(jnp.where(hm, dd, 0.0), axis=-1, keepdims=True)
                s = _dot(qm, kcat, NT)
                s = jnp.where(valid, s - sl[:, h * HEAD_DIM:h * HEAD_DIM + 1] * stepsf, NEG)
                p = jnp.exp(s - lseb[:, h * HEAD_DIM:h * HEAD_DIM + 1])
                ds = (p * (_dot(dom, vcat, NT) - delta)).astype(BF16)
                parts.append((_dot(ds, kcat), _dot(ds, qb, TN), _dot(p.astype(BF16), dob16, TN)))
            dq_ref[rows, :] = jnp.where(head0, parts[0][0], parts[1][0])
            ck = jnp.where(head0_kv, parts[0][1], parts[1][1])
            cv = jnp.where(head0_kv, parts[0][2], parts[1][2])
            row0 = pl.multiple_of((t * nbq + b) * CHUNK, CHUNK)
            dk_ref[pl.ds(row0, CHUNK), :] += ck[CHUNK:, :]
            dv_ref[pl.ds(row0, CHUNK), :] += cv[CHUNK:, :]

            def add_prev(ck=ck, cv=cv, row0=row0):
                prev0 = pl.multiple_of(row0 - CHUNK, CHUNK)
                dk_ref[pl.ds(prev0, CHUNK), :] += ck[:CHUNK, :]
                dv_ref[pl.ds(prev0, CHUNK), :] += cv[:CHUNK, :]

            if b == 0:
                pl.when(t > 0)(add_prev)
            else:
                add_prev()

    cur, prev, slope = _attn_specs(tq, nbq)
    whole = pl.BlockSpec((L, LANES), lambda c, t: (0, c))
    sds = jax.ShapeDtypeStruct((L, d * ATTN_W), F32)
    dq, dk, dv = pl.pallas_call(
        body, name=f"attn_bwd_d{d}", grid=(d * (ATTN_W // LANES), nt),
        in_specs=[slope, cur, cur, prev, cur, prev, cur, cur, cur],
        out_specs=[cur, whole, whole],
        out_shape=[sds, sds, sds],
        compiler_params=_params(("parallel", "arbitrary")),
    )(_slope_rows(d), view(q), view(k), view(k), view(v), view(v), view(dattn), view(attn), view(lse))
    return tuple(a.reshape(T, ATTN_W) for a in (dq, dk, dv))


def _dproj_assemble(dqs, dks, dvs, du, dz):
    T = du.shape[0]
    tm = 512

    def body(q1, q2, q3, k1, k2, k3, v1, v2, v3, u_ref, z_ref, out_ref):
        a = ATTN_W
        out_ref[:, 0:a] = ((q1[...] + q2[...] + q3[...]) * Q_SCALE).astype(BF16)
        out_ref[:, a:2 * a] = (k1[...] + k2[...] + k3[...]).astype(BF16)
        out_ref[:, 2 * a:3 * a] = (v1[...] + v2[...] + v3[...]).astype(BF16)
        out_ref[:, 3 * a:3 * a + GMLP_W] = u_ref[...].astype(BF16)
        out_ref[:, 3 * a + GMLP_W:] = z_ref[...].astype(BF16)

    return pl.pallas_call(
        body, name="dproj_assemble", grid=(T // tm,),
        in_specs=[_rows(tm, ATTN_W)] * 9 + [_rows(tm, GMLP_W)] * 2,
        out_specs=_rows(tm, IN_W),
        out_shape=jax.ShapeDtypeStruct((T, IN_W), BF16),
        compiler_params=_params(("parallel",)),
    )(*dqs, *dks, *dvs, du, dz)


def _proj_bwd(dproj, w_in_t, x, g1, dh1):
    T = x.shape[0]
    tm = 256

    def body(d_ref, w_ref, x_ref, g_ref, r_ref, dx_ref, dg_ref):
        @pl.when(pl.program_id(0) == 0)
        def _():
            dg_ref[...] = jnp.zeros_like(dg_ref)

        dhn = _dot(d_ref[...], w_ref[...])
        n1, r1 = _rms(x_ref[...])
        _accum_rows(dg_ref, dhn * n1)
        dx_ref[...] = r_ref[...] + _rms_bwd(n1, r1, g_ref[...], dhn)

    return pl.pallas_call(
        body, name="proj_bwd", grid=(T // tm,),
        in_specs=[_rows(tm, IN_W), _resident((IN_W, D_MODEL)), _rows(tm, D_MODEL), _resident((1, D_MODEL)),
                  _rows(tm, D_MODEL)],
        out_specs=[_rows(tm, D_MODEL), pl.BlockSpec((8, D_MODEL), lambda i: (0, 0))],
        out_shape=[jax.ShapeDtypeStruct((T, D_MODEL), F32), jax.ShapeDtypeStruct((8, D_MODEL), F32)],
        compiler_params=_params(("arbitrary",)),
    )(dproj, w_in_t, x, g1, dh1)


def _dw(a, b, name, tka, tnb):
    T, ka = a.shape
    nb = b.shape[1]
    tt = min(1024, T)

    def body(a_ref, b_ref, o_ref):
        @pl.when(pl.program_id(2) == 0)
        def _():
            o_ref[...] = jnp.zeros_like(o_ref)

        o_ref[...] += _dot(a_ref[...], b_ref[...], TN)

    return pl.pallas_call(
        body, name=name, grid=(ka // tka, nb // tnb, T // tt),
        in_specs=[pl.BlockSpec((tt, tka), lambda i, j, s: (s, i)), pl.BlockSpec((tt, tnb), lambda i, j, s: (s, j))],
        out_specs=pl.BlockSpec((tka, tnb), lambda i, j, s: (i, j)),
        out_shape=jax.ShapeDtypeStruct((ka, nb), F32),
        compiler_params=_params(("parallel", "parallel", "arbitrary")),
    )(a, b)


def _adamw(w, m, v, parts, name, tr):
    R, C = w.shape
    P = parts.shape[0]

    def body(w_ref, m_ref, v_ref, p_ref, g_ref, d_ref, m2_ref, v2_ref):
        g = p_ref[0].astype(F32)
        for i in range(1, P):
            g = g + p_ref[i].astype(F32)
        m2 = ADAM_B1 * m_ref[...] + (1.0 - ADAM_B1) * g
        v2 = ADAM_B2 * v_ref[...] + (1.0 - ADAM_B2) * jnp.square(g)
        m_hat = m2 / (1.0 - ADAM_B1 ** ADAM_STEP)
        v_hat = v2 / (1.0 - ADAM_B2 ** ADAM_STEP)
        g_ref[...] = g
        d_ref[...] = -ADAM_LR * (m_hat / (jnp.sqrt(v_hat) + ADAM_EPS) + ADAM_WD * w_ref[...])
        m2_ref[...] = m2
        v2_ref[...] = v2

    spec = _rows(tr, C)
    return pl.pallas_call(
        body, name=name, grid=(R // tr,),
        in_specs=[spec, spec, spec, pl.BlockSpec((P, tr, C), lambda i: (0, i, 0))],
        out_specs=[spec] * 4,
        out_shape=[jax.ShapeDtypeStruct((R, C), F32)] * 4,
        compiler_params=_params(("parallel",)),
    )(w, m, v, parts)


def _add_pairs(a, b):
    n, R, C = a.shape
    tr = R // 4

    def body(a_ref, b_ref, o_ref):
        o_ref[...] = a_ref[...] + b_ref[...]

    spec = pl.BlockSpec((1, tr, C), lambda i, j: (i, j, 0))
    return pl.pallas_call(
        body, name="grad_pair_sum", grid=(n, R // tr),
        in_specs=[spec, spec], out_specs=spec,
        out_shape=jax.ShapeDtypeStruct(a.shape, F32),
        compiler_params=_params(("parallel", "parallel")),
    )(a, b)


_HBM = pl.BlockSpec(memory_space=pltpu.HBM)


def _place():
    return lax.axis_index("x"), lax.axis_index("y"), lax.axis_index("c")


def _all_gather(xs, name):
    def body(x_ref, out_ref, send_sems, recv_sems, local_sem):
        x, y, c = _place()
        me, sibling = (x, y, c), (x, y, 1 - c)
        chips = [(1 - x, y), (x, 1 - y), (1 - x, 1 - y)]

        def slot(px, py, pc):
            return out_ref.at[4 * px + 2 * py + pc]

        def copy(k, block, to, src=None):
            return pltpu.make_async_remote_copy(
                src_ref=slot(*block) if src is None else src, dst_ref=slot(*block),
                send_sem=send_sems.at[k], recv_sem=recv_sems.at[k], device_id=to, device_id_type=MESH)

        mine = pltpu.make_async_copy(x_ref, slot(*me), local_sem)
        mine.start()
        first = [copy(0, me, sibling, src=x_ref)]
        first += [copy(1 + j, me, (*chip, c), src=x_ref) for j, chip in enumerate(chips)]
        for cp in first:
            cp.start()
        passed = [copy(4 + j, (*chip, c), sibling) for j, chip in enumerate(chips)]
        for j, chip in enumerate(chips):
            copy(1 + j, (*chip, c), me).wait_recv()
            passed[j].start()
        copy(0, sibling, me).wait_recv()
        for j, chip in enumerate(chips):
            copy(4 + j, (*chip, 1 - c), me).wait_recv()
        for cp in first + passed:
            cp.wait_send()
        mine.wait()

    return pl.pallas_call(
        body, name=name,
        out_shape=jax.ShapeDtypeStruct((N_DEV,) + xs.shape, xs.dtype),
        in_specs=[_HBM], out_specs=_HBM,
        scratch_shapes=[pltpu.SemaphoreType.DMA((7,)), pltpu.SemaphoreType.DMA((7,)), pltpu.SemaphoreType.DMA],
    )(xs)


def _sibling_exchange(send):
    def body(s_ref, r_ref, send_sem, recv_sem):
        x, y, c = _place()
        cp = pltpu.make_async_remote_copy(src_ref=s_ref, dst_ref=r_ref, send_sem=send_sem, recv_sem=recv_sem,
                                          device_id=(x, y, 1 - c), device_id_type=MESH)
        cp.start()
        cp.wait()

    return pl.pallas_call(
        body, name="grad_sibling_exchange",
        out_shape=jax.ShapeDtypeStruct(send.shape, send.dtype),
        in_specs=[_HBM], out_specs=_HBM,
        scratch_shapes=[pltpu.SemaphoreType.DMA, pltpu.SemaphoreType.DMA],
    )(send)


def _chip_exchange(p):
    def body(p_ref, r_ref, send_sems, recv_sems, local_sem):
        x, y, c = _place()
        my_chip = 2 * x + y
        chips = [(1 - x, y), (x, 1 - y), (1 - x, 1 - y)]
        mine = pltpu.make_async_copy(p_ref.at[my_chip], r_ref.at[my_chip], local_sem)
        mine.start()
        sends = []
        for k, (px, py) in enumerate(chips):
            cp = pltpu.make_async_remote_copy(
                src_ref=p_ref.at[2 * px + py], dst_ref=r_ref.at[my_chip],
                send_sem=send_sems.at[k], recv_sem=recv_sems.at[k], device_id=(px, py, c), device_id_type=MESH)
            cp.start()
            sends.append(cp)
        for k, (px, py) in enumerate(chips):
            pltpu.make_async_remote_copy(
                src_ref=p_ref.at[my_chip], dst_ref=r_ref.at[2 * px + py],
                send_sem=send_sems.at[k], recv_sem=recv_sems.at[k], device_id=(px, py, c),
                device_id_type=MESH).wait_recv()
        for cp in sends:
            cp.wait_send()
        mine.wait()

    return pl.pallas_call(
        body, name="grad_chip_exchange",
        out_shape=jax.ShapeDtypeStruct(p.shape, p.dtype),
        in_specs=[_HBM], out_specs=_HBM,
        scratch_shapes=[pltpu.SemaphoreType.DMA((3,)), pltpu.SemaphoreType.DMA((3,)), pltpu.SemaphoreType.DMA],
    )(p)


def _local_step(x, tgt, small, w_in, w_out, w_ff1, w_ff2):
    g1, g2, gf = small["norm1_g"], small["norm2_g"], small["final_norm_g"].reshape(1, D_MODEL)
    ga, gg = small["attn_out_g"], small["gmlp_out_g"]
    ln_g = small["sgu_ln_g"].reshape(1, GMLP_W)
    ln_b = small["sgu_ln_b"].reshape(1, GMLP_W)
    sgu_w = small["sgu_w"][0]
    sgu_bt = small["sgu_b"][0].T

    hn1, q, k, v, u, z = _proj_fwd(x, g1, w_in)
    outs, lses = zip(*[_attn_fwd(q, k, v, d) for d in DILATIONS])
    attn, lse = _attn_combine(outs, lses)
    gm = _gmlp_fwd(u, z, ln_g, ln_b, sgu_w, sgu_bt)
    mixed, h1, hn2 = _out_fwd(attn, gm, ga, gg, w_out, x, g2)
    a, dh2f, dh2b, loss8, dgf8 = _ffn_fwd(hn2, h1, w_ff1, w_ff2, gf, tgt)

    da, act, dh1f, dh1b, dg2 = _ffn_bwd(dh2b, dh2f, a, h1, g2, w_ff2.T, w_ff1.T)
    dw_ff2 = _dw(act, dh2b, "dw_ff2", 512, 1024)
    dw_ff1 = _dw(hn2, da, "dw_ff1", 512, 1024)
    dattn, dgm, dga, dgg = _out_bwd(dh1b, w_out.T, attn, gm, ga, gg)
    dw_out = _dw(mixed, dh1b, "dw_out", 512, 1024)
    du, dz, dlg, dlb, dsw, dsb = _gmlp_bwd(u, z, dgm, ln_g, ln_b, sgu_w, sgu_bt)
    dqs, dks, dvs = zip(*[_attn_bwd(q, k, v, dattn, attn, lse, d) for d in DILATIONS])
    dproj = _dproj_assemble(dqs, dks, dvs, du, dz)
    dw_in = _dw(hn1, dproj, "dw_in", 512, IN_W // 2)
    dx, dg1 = _proj_bwd(dproj, w_in.T, x, g1, dh1f)

    small_grads = dict(
        norm1_g=dg1[0], sgu_ln_g=dlg[0], sgu_ln_b=dlb[0], sgu_w=dsw, sgu_b=dsb[:, :N_GROUPS].T,
        attn_out_g=dga[0], gmlp_out_g=dgg[0], norm2_g=dg2[0], final_norm_g=dgf8[0])
    return loss8[0, 0], dx, (dw_in, dw_out, dw_ff1, dw_ff2), small_grads


SMALL_NAMES = ("norm1_g", "sgu_ln_g", "sgu_ln_b", "sgu_w", "sgu_b", "attn_out_g", "gmlp_out_g", "norm2_g",
               "final_norm_g")
WEIGHT_ORDER = ("norm1_g", "w_in", "sgu_ln_g", "sgu_ln_b", "sgu_w", "sgu_b", "attn_out_g", "gmlp_out_g", "w_out",
                "norm2_g", "w_ff1", "w_ff2", "final_norm_g")


def _pack_small(d):
    return jnp.concatenate([d[n].reshape(-1, LANES) for n in SMALL_NAMES], axis=0)


def _unpack_small(p, like):
    out, r = {}, 0
    for n in SMALL_NAMES:
        rows = like[n].size // LANES
        out[n] = p[r:r + rows].reshape(like[n].shape)
        r += rows
    return out


def _pack_big_shards(w_in, w_out, w_ff1, w_ff2):
    return jnp.concatenate([w_in.reshape(-1, D_MODEL), w_out.reshape(-1, D_MODEL), w_ff1.reshape(-1, D_MODEL),
                            w_ff2.reshape(-1, D_MODEL)], axis=0)


_R_IN, _R_OUT, _R_FF = IN_W // N_DEV, D_MODEL // N_DEV, D_FF // N_DEV
_O1, _O2, _O3 = _R_IN, _R_IN + _R_OUT, _R_IN + _R_OUT + _R_FF


def _unpack_big_shards(p):
    return dict(w_in=p[0:_O1].reshape(1, D_MODEL, _R_IN), w_out=p[_O1:_O2].reshape(1, _R_OUT, D_MODEL),
                w_ff1=p[_O2:_O3].reshape(1, D_MODEL, _R_FF), w_ff2=p[_O3:].reshape(1, _R_FF, D_MODEL))


def _unpack_gathered(g):
    w_in = g[:, 0:_O1].reshape(N_DEV, D_MODEL, _R_IN).transpose(1, 0, 2).reshape(D_MODEL, IN_W)
    w_out = g[:, _O1:_O2].reshape(D_MODEL, D_MODEL)
    w_ff1 = g[:, _O2:_O3].reshape(N_DEV, D_MODEL, _R_FF).transpose(1, 0, 2).reshape(D_MODEL, D_FF)
    w_ff2 = g[:, _O3:].reshape(D_FF, D_MODEL)
    return w_in, w_out, w_ff1, w_ff2


def _pack_big_grads(dw_in, dw_out, dw_ff1, dw_ff2):
    col = lambda w, n: w.reshape(D_MODEL, 4, 2, n).transpose(2, 1, 0, 3).reshape(2, 4, n, D_MODEL)
    row = lambda w, n: w.reshape(4, 2, n, D_MODEL).transpose(1, 0, 2, 3)
    return jnp.concatenate([col(dw_in, _R_IN), row(dw_out, _R_OUT), col(dw_ff1, _R_FF), row(dw_ff2, _R_FF)], axis=2)


def kernel(x, norm1_g, w_in, sgu_ln_g, sgu_ln_b, sgu_w, sgu_b, attn_out_g, gmlp_out_g, w_out, norm2_g, w_ff1, w_ff2, final_norm_g, loss_target, m_norm1_g, m_w_in, m_sgu_ln_g, m_sgu_ln_b, m_sgu_w, m_sgu_b, m_attn_out_g, m_gmlp_out_g, m_w_out, m_norm2_g, m_w_ff1, m_w_ff2, m_final_norm_g, v_norm1_g, v_w_in, v_sgu_ln_g, v_sgu_ln_b, v_sgu_w, v_sgu_b, v_attn_out_g, v_gmlp_out_g, v_w_out, v_norm2_g, v_w_ff1, v_w_ff2, v_final_norm_g):
    w = dict(norm1_g=norm1_g, w_in=w_in, sgu_ln_g=sgu_ln_g, sgu_ln_b=sgu_ln_b, sgu_w=sgu_w, sgu_b=sgu_b,
             attn_out_g=attn_out_g, gmlp_out_g=gmlp_out_g, w_out=w_out, norm2_g=norm2_g, w_ff1=w_ff1, w_ff2=w_ff2,
             final_norm_g=final_norm_g)
    m = dict(norm1_g=m_norm1_g, w_in=m_w_in, sgu_ln_g=m_sgu_ln_g, sgu_ln_b=m_sgu_ln_b, sgu_w=m_sgu_w, sgu_b=m_sgu_b,
             attn_out_g=m_attn_out_g, gmlp_out_g=m_gmlp_out_g, w_out=m_w_out, norm2_g=m_norm2_g, w_ff1=m_w_ff1,
             w_ff2=m_w_ff2, final_norm_g=m_final_norm_g)
    v = dict(norm1_g=v_norm1_g, w_in=v_w_in, sgu_ln_g=v_sgu_ln_g, sgu_ln_b=v_sgu_ln_b, sgu_w=v_sgu_w, sgu_b=v_sgu_b,
             attn_out_g=v_attn_out_g, gmlp_out_g=v_gmlp_out_g, w_out=v_w_out, norm2_g=v_norm2_g, w_ff1=v_w_ff1,
             w_ff2=v_w_ff2, final_norm_g=v_final_norm_g)
    big = ("w_in", "w_out", "w_ff1", "w_ff2")
    core = lax.axis_index("c")

    shards = _pack_big_shards(*[w[n] for n in big])
    gathered = _all_gather(shards.astype(BF16), "weight_all_gather")
    full = _unpack_gathered(gathered)

    loss, dx, big_grads, small_grads = _local_step(x[0], loss_target[0], {n: w[n] for n in SMALL_NAMES}, *full)
    loss = lax.psum(loss, ("x", "y", "c"))

    packed = _pack_big_grads(*big_grads)
    keep = lax.dynamic_index_in_dim(packed, core, 0, keepdims=False)
    send = lax.dynamic_index_in_dim(packed, 1 - core, 0, keepdims=False)
    chip_sum = _add_pairs(keep, _sibling_exchange(send))
    parts = _chip_exchange(chip_sum)
    gb, db, mb, vb = _adamw(shards, _pack_big_shards(*[m[n] for n in big]), _pack_big_shards(*[v[n] for n in big]),
                            parts, "adamw_large", PACK_ROWS // 4)

    small_parts = _all_gather(_pack_small(small_grads), "small_grad_all_gather")
    gs, ds, ms, vs = _adamw(_pack_small({n: w[n] for n in SMALL_NAMES}), _pack_small({n: m[n] for n in SMALL_NAMES}),
                            _pack_small({n: v[n] for n in SMALL_NAMES}), small_parts, "adamw_small", SMALL_ROWS)

    outs = []
    for pb, ps in ((gb, gs), (db, ds), (mb, ms), (vb, vs)):
        d = dict(_unpack_big_shards(pb))
        d.update(_unpack_small(ps, w))
        outs.extend(d[n] for n in WEIGHT_ORDER)
    return (loss, dx[None], *outs)
```

```python
import functools
import math

import numpy as np
import jax
import jax.numpy as jnp
from jax import lax
from jax.experimental import pallas as pl
from jax.experimental.pallas import tpu as pltpu

F32 = jnp.float32
BF16 = jnp.bfloat16

D_MODEL = 1024
HEAD_DIM = 64
N_HEADS = 12
ATTN_W = N_HEADS * HEAD_DIM
N_GROUPS = 4
GMLP_W = N_GROUPS * HEAD_DIM
IN_W = 3 * ATTN_W + 2 * GMLP_W
D_FF = 4 * D_MODEL
CHUNK = 128
DILATIONS = (1, 4, 16)
EPS = 1e-6
Q_SCALE = HEAD_DIM ** -0.5
NEG = -1e30

ADAM_LR, ADAM_B1, ADAM_B2, ADAM_EPS, ADAM_WD, ADAM_STEP = 0.001, 0.9, 0.999, 1e-08, 0.01, 10

N_DEV = 8
LANES = 128
VMEM_LIMIT = 56 << 20
PACK_ROWS = (IN_W + D_MODEL + D_FF + D_FF) * D_MODEL // N_DEV // D_MODEL
SMALL_ROWS = 552

MESH = pl.DeviceIdType.MESH


def _alibi_slopes(n):
    def pow2(m):
        start = 2.0 ** (-8.0 / m)
        return [start ** (i + 1) for i in range(m)]
    c = 2 ** int(math.floor(math.log2(n)))
    s = pow2(n) if c == n else pow2(c) + pow2(2 * c)[0::2][: n - c]
    return np.asarray(s, dtype=np.float32)


SLOPES = _alibi_slopes(N_HEADS)


def _params(sem=None):
    kw = dict(vmem_limit_bytes=VMEM_LIMIT)
    if sem is not None:
        kw["dimension_semantics"] = sem
    return pltpu.CompilerParams(**kw)


def _rows(tm, n):
    return pl.BlockSpec((tm, n), lambda i: (i, 0))


def _resident(shape):
    return pl.BlockSpec(shape, lambda *_: (0,) * len(shape), pipeline_mode=pl.Buffered(1))


def _rms(x):
    r = lax.rsqrt(jnp.mean(x * x, axis=-1, keepdims=True) + EPS)
    return x * r, r


def _rms_bwd(n, r, g, dy):
    dn = dy * g
    return r * (dn - n * jnp.mean(dn * n, axis=-1, keepdims=True))


def _accum_rows(acc_ref, v):
    acc_ref[...] += jnp.broadcast_to(jnp.sum(v, axis=0, keepdims=True), acc_ref.shape)


_G0 = math.sqrt(2.0 / math.pi)
_G1 = 0.044715


def _gelu(x):
    t = jnp.tanh(_G0 * (x + _G1 * (x * x * x)))
    return x * (0.5 * (1.0 + t)), t


def _gelu_grad(x, t):
    return 0.5 * (1.0 + t) + 0.5 * x * (1.0 - t * t) * (_G0 * (1.0 + 3.0 * _G1 * x * x))


NT = (((1,), (1,)), ((), ()))
TN = (((0,), (0,)), ((), ()))


def _dot(a, b, dims=None):
    if dims is None:
        return jnp.dot(a, b, preferred_element_type=F32)
    return lax.dot_general(a, b, dims, preferred_element_type=F32)


def _proj_fwd(x, g1, w_in):
    T = x.shape[0]
    tm = 256

    def body(x_ref, g_ref, w_ref, hn_ref, q_ref, k_ref, v_ref, u_ref, z_ref):
        n, _ = _rms(x_ref[...])
        hn = (n * g_ref[...]).astype(BF16)
        hn_ref[...] = hn
        a = ATTN_W
        q_ref[...] = _dot(hn, w_ref[:, 0:a]) * Q_SCALE
        k_ref[...] = _dot(hn, w_ref[:, a:2 * a])
        v_ref[...] = _dot(hn, w_ref[:, 2 * a:3 * a])
        u_ref[...] = _dot(hn, w_ref[:, 3 * a:3 * a + GMLP_W])
        z_ref[...] = _dot(hn, w_ref[:, 3 * a + GMLP_W:])

    sds = jax.ShapeDtypeStruct
    return pl.pallas_call(
        body, name="proj_fwd", grid=(T // tm,),
        in_specs=[_rows(tm, D_MODEL), _resident((1, D_MODEL)), _resident((D_MODEL, IN_W))],
        out_specs=[_rows(tm, D_MODEL), _rows(tm, ATTN_W), _rows(tm, ATTN_W), _rows(tm, ATTN_W),
                   _rows(tm, GMLP_W), _rows(tm, GMLP_W)],
        out_shape=[sds((T, D_MODEL), BF16), sds((T, ATTN_W), F32), sds((T, ATTN_W), F32),
                   sds((T, ATTN_W), F32), sds((T, GMLP_W), F32), sds((T, GMLP_W), F32)],
        compiler_params=_params(("parallel",)),
    )(x, g1, w_in)


ATT_TILE = 2048
ATT_BLOCKS = ATT_TILE // CHUNK


def _slope_table():
    row = np.repeat(SLOPES, HEAD_DIM)
    return jnp.asarray(np.broadcast_to(row[None], (8, ATTN_W)), F32)


def _stacked_consts(sl_ref):
    shape = (2 * CHUNK, 2 * CHUNK)
    row = lax.broadcasted_iota(jnp.int32, shape, 0)
    kj = lax.broadcasted_iota(jnp.int32, shape, 1)
    steps = (row & (CHUNK - 1)) + CHUNK - kj
    band = (steps >= 0) & (steps <= CHUNK)
    sl = sl_ref[0:1, :]
    upper = lax.broadcasted_iota(jnp.int32, (2 * CHUNK, 1), 0) < CHUNK
    slope2 = jnp.where(upper, sl[:, 0:1], sl[:, HEAD_DIM:HEAD_DIM + 1])
    return kj, band, slope2 * steps.astype(F32)


def _block_rows(j, d):
    if d == 1:
        start = j * CHUNK
        return j, start, pl.ds(start, CHUNK)
    r, b = j % d, j // d
    start = r + (d * CHUNK) * b
    return b, start, pl.ds(start, CHUNK, stride=d)


def _prev_rows(j, d):
    _, start, _ = _block_rows(j, d)
    if d == 1:
        return pl.ds(start - CHUNK, CHUNK), pl.ds(ATT_TILE - CHUNK, CHUNK)
    last = j % d + (d * CHUNK) * (ATT_BLOCKS // d - 1)
    return pl.ds(start - d * CHUNK, CHUNK, stride=d), pl.ds(last, CHUNK, stride=d)


def _kv_block(j, d, rows, kc_ref, kp_ref, vc_ref, vp_ref):
    b = j // d
    here, before = _prev_rows(j, d)
    if ATT_BLOCKS // d == 1:
        kp, vp = kp_ref[before, :], vp_ref[before, :]
    else:
        src_k, src_v, src_rows = (kp_ref, vp_ref, before) if b == 0 else (kc_ref, vc_ref, here)
        kp, vp = src_k[src_rows, :], src_v[src_rows, :]
    kcat = jnp.concatenate([kp, kc_ref[rows, :]], axis=0).astype(BF16)
    vcat = jnp.concatenate([vp, vc_ref[rows, :]], axis=0).astype(BF16)
    return kcat, vcat


def _stack_heads(xb, head0):
    zero = jnp.zeros_like(xb)
    return jnp.concatenate([jnp.where(head0, xb, zero), jnp.where(head0, zero, xb)], axis=0).astype(BF16)


def _unstack_heads(x2, head0):
    return jnp.where(head0, x2[:CHUNK, :], x2[CHUNK:, :])


def _scores(q2, kcat, bias, kj, first):
    s = _dot(q2, kcat, NT) + bias
    return jnp.where(kj < jnp.where(first, CHUNK, 0), NEG, s)


def _attn_tile_specs(nt, lag):
    clamp = (lambda t: jnp.minimum(t, nt - 1)) if lag else (lambda t: t)
    cur = pl.BlockSpec((ATT_TILE, LANES), lambda c, t: (clamp(t), c))
    prev = pl.BlockSpec((ATT_TILE, LANES), lambda c, t: (jnp.maximum(clamp(t) - 1, 0), c))
    slope = pl.BlockSpec((8, LANES), lambda c, t: (0, c))
    return cur, prev, slope


def _attn_fwd(q, k, v):
    T = q.shape[0]
    nt = T // ATT_TILE

    def body(sl_ref, q_ref, kc_ref, kp_ref, vc_ref, vp_ref, attn_ref, lse_ref, *acc):
        o_acc, l_acc = acc[:3], acc[3:]
        t = pl.program_id(1)
        kj, band, base = _stacked_consts(sl_ref)
        head0 = lax.broadcasted_iota(jnp.int32, (CHUNK, LANES), 1) < HEAD_DIM
        for pi, d in enumerate(DILATIONS):
            bias = jnp.where(band, -(float(d) * base), NEG)

            def block(j, carry, d=d, pi=pi, bias=bias):
                b, _, rows = _block_rows(j, d)
                kcat, vcat = _kv_block(j, d, rows, kc_ref, kp_ref, vc_ref, vp_ref)
                q2 = _stack_heads(q_ref[rows, :], head0)
                s = _scores(q2, kcat, bias, kj, (t == 0) & (b == 0))
                m = jnp.max(s, axis=-1, keepdims=True)
                p = jnp.exp(s - m)
                l = jnp.sum(p, axis=-1, keepdims=True)
                o_acc[pi][rows, :] = _unstack_heads(_dot(p.astype(BF16), vcat) / l, head0)
                l_acc[pi][rows, :] = _unstack_heads(jnp.broadcast_to(m + jnp.log(l), (2 * CHUNK, LANES)), head0)
                return carry

            for j in range(ATT_BLOCKS):
                block(j, 0)

        for ci in range(ATT_TILE // 256):
            rows = slice(ci * 256, (ci + 1) * 256)
            a, b, c = l_acc[0][rows, :], l_acc[1][rows, :], l_acc[2][rows, :]
            m = jnp.maximum(jnp.maximum(a, b), c)
            ea, eb, ec = jnp.exp(a - m), jnp.exp(b - m), jnp.exp(c - m)
            tot = ea + eb + ec
            attn_ref[rows, :] = (ea * o_acc[0][rows, :] + eb * o_acc[1][rows, :] + ec * o_acc[2][rows, :]) / tot
            lse_ref[rows, :] = m + jnp.log(tot)

    cur, prev, slope = _attn_tile_specs(nt, lag=False)
    return pl.pallas_call(
        body, name="attn_fwd", grid=(ATTN_W // LANES, nt),
        in_specs=[slope, cur, cur, prev, cur, prev],
        out_specs=[cur, cur],
        out_shape=[jax.ShapeDtypeStruct((T, ATTN_W), F32)] * 2,
        scratch_shapes=[pltpu.VMEM((ATT_TILE, LANES), F32)] * 6,
        compiler_params=_params(("parallel", "arbitrary")),
    )(_slope_table(), q, k, k, v, v)


def _group_mean(v, grp):
    out = jnp.zeros_like(v)
    for g in range(N_GROUPS):
        mk = grp == g
        s = jnp.sum(jnp.where(mk, v, 0.0), axis=-1, keepdims=True) * (1.0 / HEAD_DIM)
        out = jnp.where(mk, s, out)
    return out


def _gmlp_core(uu, zz, lg, lb, ws, sb_ref, grp):
    ug, tu = _gelu(uu)
    zg, tz = _gelu(zz)
    zc = zg - _group_mean(zg, grp)
    rstd = lax.rsqrt(_group_mean(zc * zc, grp) + EPS)
    xhat = zc * rstd
    zn16 = (xhat * lg + lb).astype(BF16)
    mixed = jnp.zeros_like(uu)
    for g in range(N_GROUPS):
        mixed = jnp.where(grp == g, _dot(ws[g], zn16) + sb_ref[:, g:g + 1], mixed)
    return ug, tu, tz, xhat, rstd, zn16, mixed


def _causal_ws(w_ref):
    ti = lax.broadcasted_iota(jnp.int32, (CHUNK, CHUNK), 0)
    si = lax.broadcasted_iota(jnp.int32, (CHUNK, CHUNK), 1)
    causal = si <= ti
    return causal, [jnp.where(causal, w_ref[g], 0.0).astype(BF16) for g in range(N_GROUPS)]


def _gmlp_fwd(u, z, ln_g, ln_b, sgu_w, sgu_bt):
    T = u.shape[0]
    tg = 512

    def body(u_ref, z_ref, g_ref, b_ref, w_ref, sb_ref, out_ref):
        grp = lax.broadcasted_iota(jnp.int32, (CHUNK, GMLP_W), 1) // HEAD_DIM
        _, ws = _causal_ws(w_ref)
        for ci in range(tg // CHUNK):
            rows = slice(ci * CHUNK, (ci + 1) * CHUNK)
            ug, _, _, _, _, _, mixed = _gmlp_core(u_ref[rows, :], z_ref[rows, :], g_ref[...], b_ref[...],
                                                  ws, sb_ref, grp)
            out_ref[rows, :] = ug * mixed

    return pl.pallas_call(
        body, name="gmlp_fwd", grid=(T // tg,),
        in_specs=[_rows(tg, GMLP_W), _rows(tg, GMLP_W), _resident((1, GMLP_W)), _resident((1, GMLP_W)),
                  _resident((N_GROUPS, CHUNK, CHUNK)), _resident((CHUNK, N_GROUPS))],
        out_specs=_rows(tg, GMLP_W),
        out_shape=jax.ShapeDtypeStruct((T, GMLP_W), F32),
        compiler_params=_params(("parallel",)),
    )(u, z, ln_g, ln_b, sgu_w, sgu_bt)


def _out_fwd(attn, gm, ga, gg, w_out, x, g2):
    T = x.shape[0]
    tm = 256

    def body(a_ref, m_ref, ga_ref, gg_ref, w_ref, x_ref, g2_ref, mix_ref, h1_ref, hn2_ref):
        an, _ = _rms(a_ref[...])
        gn, _ = _rms(m_ref[...])
        an = (an * ga_ref[...]).astype(BF16)
        gn = (gn * gg_ref[...]).astype(BF16)
        mix_ref[:, 0:ATTN_W] = an
        mix_ref[:, ATTN_W:] = gn
        h1 = x_ref[...] + _dot(an, w_ref[0:ATTN_W, :]) + _dot(gn, w_ref[ATTN_W:, :])
        h1_ref[...] = h1
        n2, _ = _rms(h1)
        hn2_ref[...] = (n2 * g2_ref[...]).astype(BF16)

    sds = jax.ShapeDtypeStruct
    return pl.pallas_call(
        body, name="out_fwd", grid=(T // tm,),
        in_specs=[_rows(tm, ATTN_W), _rows(tm, GMLP_W), _resident((1, ATTN_W)), _resident((1, GMLP_W)),
                  _resident((D_MODEL, D_MODEL)), _rows(tm, D_MODEL), _resident((1, D_MODEL))],
        out_specs=[_rows(tm, D_MODEL)] * 3,
        out_shape=[sds((T, D_MODEL), BF16), sds((T, D_MODEL), F32), sds((T, D_MODEL), BF16)],
        compiler_params=_params(("parallel",)),
    )(attn, gm, ga, gg, w_out, x, g2)


FF_CHUNK = 512


def _ffn_fwd(hn2, h1, w1, w2, gf, tgt):
    T = h1.shape[0]
    tm = 256

    def body(hn_ref, h1_ref, w1_ref, w2_ref, gf_ref, t_ref, a_ref, dhf_ref, dhb_ref, loss_ref, dgf_ref):
        i = pl.program_id(0)

        @pl.when(i == 0)
        def _():
            loss_ref[...] = jnp.zeros_like(loss_ref)
            dgf_ref[...] = jnp.zeros_like(dgf_ref)

        hn = hn_ref[...]
        acc = h1_ref[...]
        for j in range(D_FF // FF_CHUNK):
            cols = slice(j * FF_CHUNK, (j + 1) * FF_CHUNK)
            a = _dot(hn, w1_ref[:, cols])
            a_ref[:, cols] = a
            act = jnp.square(jnp.maximum(a, 0.0)).astype(BF16)
            acc = acc + _dot(act, w2_ref[cols, :])
        n3, r3 = _rms(acc)
        gf_row = gf_ref[...]
        e = n3 * gf_row - t_ref[...]
        loss_ref[...] += 0.5 * jnp.sum(jnp.mean(e * e, axis=-1, keepdims=True))
        dy = e * (1.0 / D_MODEL)
        _accum_rows(dgf_ref, dy * n3)
        dh2 = _rms_bwd(n3, r3, gf_row, dy)
        dhf_ref[...] = dh2
        dhb_ref[...] = dh2.astype(BF16)

    sds = jax.ShapeDtypeStruct
    acc_spec = lambda n: pl.BlockSpec((8, n), lambda i: (0, 0))
    return pl.pallas_call(
        body, name="ffn_fwd", grid=(T // tm,),
        in_specs=[_rows(tm, D_MODEL), _rows(tm, D_MODEL), _resident((D_MODEL, D_FF)), _resident((D_FF, D_MODEL)),
                  _resident((1, D_MODEL)), _rows(tm, D_MODEL)],
        out_specs=[_rows(tm, D_FF), _rows(tm, D_MODEL), _rows(tm, D_MODEL), acc_spec(LANES), acc_spec(D_MODEL)],
        out_shape=[sds((T, D_FF), F32), sds((T, D_MODEL), F32), sds((T, D_MODEL), BF16),
                   sds((8, LANES), F32), sds((8, D_MODEL), F32)],
        compiler_params=_params(("arbitrary",)),
    )(hn2, h1, w1, w2, gf, tgt)


def _ffn_bwd(dh2b, dh2f, a, h1, g2, w2t, w1t):
    T = h1.shape[0]
    tm = 256

    def body(db_ref, df_ref, a_ref, h1_ref, g2_ref, w2t_ref, w1t_ref, da_ref, act_ref, d1f_ref, d1b_ref, dg_ref):
        @pl.when(pl.program_id(0) == 0)
        def _():
            dg_ref[...] = jnp.zeros_like(dg_ref)

        db = db_ref[...]
        acc = jnp.zeros((tm, D_MODEL), F32)
        for j in range(D_FF // FF_CHUNK):
            cols = slice(j * FF_CHUNK, (j + 1) * FF_CHUNK)
            r = jnp.maximum(a_ref[:, cols], 0.0)
            da = (_dot(db, w2t_ref[:, cols]) * (2.0 * r)).astype(BF16)
            da_ref[:, cols] = da
            act_ref[:, cols] = (r * r).astype(BF16)
            acc = acc + _dot(da, w1t_ref[cols, :])
        n2, r2 = _rms(h1_ref[...])
        _accum_rows(dg_ref, acc * n2)
        dh1 = df_ref[...] + _rms_bwd(n2, r2, g2_ref[...], acc)
        d1f_ref[...] = dh1
        d1b_ref[...] = dh1.astype(BF16)

    sds = jax.ShapeDtypeStruct
    return pl.pallas_call(
        body, name="ffn_bwd", grid=(T // tm,),
        in_specs=[_rows(tm, D_MODEL), _rows(tm, D_MODEL), _rows(tm, D_FF), _rows(tm, D_MODEL),
                  _resident((1, D_MODEL)), _resident((D_MODEL, D_FF)), _resident((D_FF, D_MODEL))],
        out_specs=[_rows(tm, D_FF), _rows(tm, D_FF), _rows(tm, D_MODEL), _rows(tm, D_MODEL),
                   pl.BlockSpec((8, D_MODEL), lambda i: (0, 0))],
        out_shape=[sds((T, D_FF), BF16), sds((T, D_FF), BF16), sds((T, D_MODEL), F32), sds((T, D_MODEL), BF16),
                   sds((8, D_MODEL), F32)],
        compiler_params=_params(("arbitrary",)),
    )(dh2b, dh2f, a, h1, g2, w2t, w1t)


def _out_bwd(dh1b, w_out_t, attn, gm, ga, gg):
    T = attn.shape[0]
    tm = 256

    def body(d_ref, w_ref, a_ref, m_ref, ga_ref, gg_ref, da_ref, dm_ref, dga_ref, dgg_ref):
        @pl.when(pl.program_id(0) == 0)
        def _():
            dga_ref[...] = jnp.zeros_like(dga_ref)
            dgg_ref[...] = jnp.zeros_like(dgg_ref)

        d = d_ref[...]
        dan = _dot(d, w_ref[:, 0:ATTN_W])
        dgn = _dot(d, w_ref[:, ATTN_W:])
        na, ra = _rms(a_ref[...])
        ng, rg = _rms(m_ref[...])
        _accum_rows(dga_ref, dan * na)
        _accum_rows(dgg_ref, dgn * ng)
        da_ref[...] = _rms_bwd(na, ra, ga_ref[...], dan)
        dm_ref[...] = _rms_bwd(ng, rg, gg_ref[...], dgn)

    sds = jax.ShapeDtypeStruct
    return pl.pallas_call(
        body, name="out_bwd", grid=(T // tm,),
        in_specs=[_rows(tm, D_MODEL), _resident((D_MODEL, D_MODEL)), _rows(tm, ATTN_W), _rows(tm, GMLP_W),
                  _resident((1, ATTN_W)), _resident((1, GMLP_W))],
        out_specs=[_rows(tm, ATTN_W), _rows(tm, GMLP_W), pl.BlockSpec((8, ATTN_W), lambda i: (0, 0)),
                   pl.BlockSpec((8, GMLP_W), lambda i: (0, 0))],
        out_shape=[sds((T, ATTN_W), F32), sds((T, GMLP_W), F32), sds((8, ATTN_W), F32), sds((8, GMLP_W), F32)],
        compiler_params=_params(("arbitrary",)),
    )(dh1b, w_out_t, attn, gm, ga, gg)


def _gmlp_bwd(u, z, dgm, ln_g, ln_b, sgu_w, sgu_bt):
    T = u.shape[0]
    tg = 512
    nsteps = T // tg

    def body(u_ref, z_ref, d_ref, g_ref, b_ref, w_ref, sb_ref, du_ref, dz_ref, dlg_ref, dlb_ref, dw_ref, dsb_ref):
        i = pl.program_id(0)

        @pl.when(i == 0)
        def _():
            for ref in (dlg_ref, dlb_ref, dw_ref, dsb_ref):
                ref[...] = jnp.zeros_like(ref)

        grp = lax.broadcasted_iota(jnp.int32, (CHUNK, GMLP_W), 1) // HEAD_DIM
        lane = lax.broadcasted_iota(jnp.int32, (CHUNK, LANES), 1)
        causal, ws = _causal_ws(w_ref)
        lg = g_ref[...]
        for ci in range(tg // CHUNK):
            rows = slice(ci * CHUNK, (ci + 1) * CHUNK)
            uu, zz = u_ref[rows, :], z_ref[rows, :]
            ug, tu, tz, xhat, rstd, zn16, mixed = _gmlp_core(uu, zz, lg, b_ref[...], ws, sb_ref, grp)
            dgm_c = d_ref[rows, :]
            dmx = dgm_c * ug
            du_ref[rows, :] = dgm_c * mixed * _gelu_grad(uu, tu)
            dmx16 = dmx.astype(BF16)
            dzn = jnp.zeros_like(dmx)
            dsb = jnp.zeros((CHUNK, LANES), F32)
            for g in range(N_GROUPS):
                mk = grp == g
                dzn = jnp.where(mk, _dot(ws[g], dmx16, TN), dzn)
                dw_ref[g] += _dot(jnp.where(mk, dmx16, jnp.zeros_like(dmx16)), zn16, NT)
                dsb = jnp.where(lane == g, jnp.sum(jnp.where(mk, dmx, 0.0), axis=-1, keepdims=True), dsb)
            dsb_ref[...] += dsb
            _accum_rows(dlg_ref, dzn * xhat)
            _accum_rows(dlb_ref, dzn)
            dxh = dzn * lg
            dzg = rstd * (dxh - _group_mean(dxh, grp) - xhat * _group_mean(dxh * xhat, grp))
            dz_ref[rows, :] = dzg * _gelu_grad(zz, tz)

        @pl.when(i == nsteps - 1)
        def _():
            for g in range(N_GROUPS):
                dw_ref[g] = jnp.where(causal, dw_ref[g], 0.0)

    sds = jax.ShapeDtypeStruct
    return pl.pallas_call(
        body, name="gmlp_bwd", grid=(nsteps,),
        in_specs=[_rows(tg, GMLP_W)] * 3 + [_resident((1, GMLP_W)), _resident((1, GMLP_W)),
                                              _resident((N_GROUPS, CHUNK, CHUNK)), _resident((CHUNK, N_GROUPS))],
        out_specs=[_rows(tg, GMLP_W), _rows(tg, GMLP_W), pl.BlockSpec((8, GMLP_W), lambda i: (0, 0)),
                   pl.BlockSpec((8, GMLP_W), lambda i: (0, 0)),
                   pl.BlockSpec((N_GROUPS, CHUNK, CHUNK), lambda i: (0, 0, 0)),
                   pl.BlockSpec((CHUNK, LANES), lambda i: (0, 0))],
        out_shape=[sds((T, GMLP_W), F32), sds((T, GMLP_W), F32), sds((8, GMLP_W), F32), sds((8, GMLP_W), F32),
                   sds((N_GROUPS, CHUNK, CHUNK), F32), sds((CHUNK, LANES), F32)],
        compiler_params=_params(("arbitrary",)),
    )(u, z, dgm, ln_g, ln_b, sgu_w, sgu_bt)


def _attn_bwd(q, k, v, dattn, attn, lse):
    T = q.shape[0]
    nt = T // ATT_TILE

    def body(sl_ref, q_ref, kc_ref, kp_ref, vc_ref, vp_ref, do_ref, o_ref, lse_ref, dq_ref, dk_ref, dv_ref,
             dk_acc, dv_acc, delta_s):
        t = pl.program_id(1)
        s_cur = t % 2
        s_prev = 1 - s_cur

        @pl.when(t == 0)
        def _():
            dk_acc[...] = jnp.zeros_like(dk_acc)
            dv_acc[...] = jnp.zeros_like(dv_acc)

        @pl.when(t < nt)
        def _():
            kj, band, base = _stacked_consts(sl_ref)
            head0 = lax.broadcasted_iota(jnp.int32, (CHUNK, LANES), 1) < HEAD_DIM
            for ci in range(ATT_TILE // 256):
                rows = slice(ci * 256, (ci + 1) * 256)
                h0 = lax.broadcasted_iota(jnp.int32, (256, LANES), 1) < HEAD_DIM
                dd = do_ref[rows, :] * o_ref[rows, :]
                d0 = jnp.sum(jnp.where(h0, dd, 0.0), axis=-1, keepdims=True)
                d1 = jnp.sum(jnp.where(h0, 0.0, dd), axis=-1, keepdims=True)
                delta_s[rows, :] = jnp.where(h0, d0, d1)
            dq_ref[...] = jnp.zeros_like(dq_ref)

            def column(xb):
                return jnp.concatenate([xb[:, 0:1], xb[:, HEAD_DIM:HEAD_DIM + 1]], axis=0)

            for d in DILATIONS:
                bias = jnp.where(band, -(float(d) * base), NEG)

                def block(j, carry, d=d, bias=bias):
                    b, _, rows = _block_rows(j, d)
                    kcat, vcat = _kv_block(j, d, rows, kc_ref, kp_ref, vc_ref, vp_ref)
                    q2 = _stack_heads(q_ref[rows, :], head0)
                    do2 = _stack_heads(do_ref[rows, :], head0)
                    s = _scores(q2, kcat, bias, kj, (t == 0) & (b == 0))
                    p = jnp.exp(s - column(lse_ref[rows, :]))
                    ds = (p * (_dot(do2, vcat, NT) - column(delta_s[rows, :]))).astype(BF16)
                    dq_ref[rows, :] += _unstack_heads(_dot(ds, kcat), head0)
                    ck = _dot(ds, q2, TN)
                    cv = _dot(p.astype(BF16), do2, TN)
                    dk_acc[s_cur, rows, :] += ck[CHUNK:, :]
                    dv_acc[s_cur, rows, :] += cv[CHUNK:, :]
                    here, before = _prev_rows(j, d)
                    if ATT_BLOCKS // d == 1:
                        dk_acc[s_prev, before, :] += ck[:CHUNK, :]
                        dv_acc[s_prev, before, :] += cv[:CHUNK, :]
                    else:
                        slot, dst = (s_prev, before) if b == 0 else (s_cur, here)
                        dk_acc[slot, dst, :] += ck[:CHUNK, :]
                        dv_acc[slot, dst, :] += cv[:CHUNK, :]
                    return carry

                for j in range(ATT_BLOCKS):
                    block(j, 0)

        dk_ref[...] = dk_acc[s_prev]
        dv_ref[...] = dv_acc[s_prev]
        dk_acc[s_prev] = jnp.zeros((ATT_TILE, LANES), F32)
        dv_acc[s_prev] = jnp.zeros((ATT_TILE, LANES), F32)

    cur, prev, slope = _attn_tile_specs(nt, lag=True)
    late = pl.BlockSpec((ATT_TILE, LANES), lambda c, t: (jnp.maximum(t - 1, 0), c))
    sds = jax.ShapeDtypeStruct((T, ATTN_W), F32)
    return pl.pallas_call(
        body, name="attn_bwd", grid=(ATTN_W // LANES, nt + 1),
        in_specs=[slope, cur, cur, prev, cur, prev, cur, cur, cur],
        out_specs=[cur, late, late],
        out_shape=[sds, sds, sds],
        scratch_shapes=[pltpu.VMEM((2, ATT_TILE, LANES), F32), pltpu.VMEM((2, ATT_TILE, LANES), F32),
                        pltpu.VMEM((ATT_TILE, LANES), F32)],
        compiler_params=_params(("parallel", "arbitrary")),
    )(_slope_table(), q, k, k, v, v, dattn, attn, lse)


def _dproj_assemble(dq, dk, dv, du, dz):
    T = du.shape[0]
    tm = 512

    def body(q_ref, k_ref, v_ref, u_ref, z_ref, out_ref):
        a = ATTN_W
        out_ref[:, 0:a] = (q_ref[...] * Q_SCALE).astype(BF16)
        out_ref[:, a:2 * a] = k_ref[...].astype(BF16)
        out_ref[:, 2 * a:3 * a] = v_ref[...].astype(BF16)
        out_ref[:, 3 * a:3 * a + GMLP_W] = u_ref[...].astype(BF16)
        out_ref[:, 3 * a + GMLP_W:] = z_ref[...].astype(BF16)

    return pl.pallas_call(
        body, name="dproj_assemble", grid=(T // tm,),
        in_specs=[_rows(tm, ATTN_W)] * 3 + [_rows(tm, GMLP_W)] * 2,
        out_specs=_rows(tm, IN_W),
        out_shape=jax.ShapeDtypeStruct((T, IN_W), BF16),
        compiler_params=_params(("parallel",)),
    )(dq, dk, dv, du, dz)


def _proj_bwd(dproj, w_in_t, x, g1, dh1):
    T = x.shape[0]
    tm = 256

    def body(d_ref, w_ref, x_ref, g_ref, r_ref, dx_ref, dg_ref):
        @pl.when(pl.program_id(0) == 0)
        def _():
            dg_ref[...] = jnp.zeros_like(dg_ref)

        dhn = _dot(d_ref[...], w_ref[...])
        n1, r1 = _rms(x_ref[...])
        _accum_rows(dg_ref, dhn * n1)
        dx_ref[...] = r_ref[...] + _rms_bwd(n1, r1, g_ref[...], dhn)

    return pl.pallas_call(
        body, name="proj_bwd", grid=(T // tm,),
        in_specs=[_rows(tm, IN_W), _resident((IN_W, D_MODEL)), _rows(tm, D_MODEL), _resident((1, D_MODEL)),
                  _rows(tm, D_MODEL)],
        out_specs=[_rows(tm, D_MODEL), pl.BlockSpec((8, D_MODEL), lambda i: (0, 0))],
        out_shape=[jax.ShapeDtypeStruct((T, D_MODEL), F32), jax.ShapeDtypeStruct((8, D_MODEL), F32)],
        compiler_params=_params(("arbitrary",)),
    )(dproj, w_in_t, x, g1, dh1)


def _dw(a, b, name, tka, tnb):
    T, ka = a.shape
    nb = b.shape[1]
    tt = min(1024, T)

    def body(a_ref, b_ref, o_ref):
        @pl.when(pl.program_id(2) == 0)
        def _():
            o_ref[...] = jnp.zeros_like(o_ref)

        o_ref[...] += _dot(a_ref[...], b_ref[...], TN)

    return pl.pallas_call(
        body, name=name, grid=(ka // tka, nb // tnb, T // tt),
        in_specs=[pl.BlockSpec((tt, tka), lambda i, j, s: (s, i)), pl.BlockSpec((tt, tnb), lambda i, j, s: (s, j))],
        out_specs=pl.BlockSpec((tka, tnb), lambda i, j, s: (i, j)),
        out_shape=jax.ShapeDtypeStruct((ka, nb), F32),
        compiler_params=_params(("parallel", "parallel", "arbitrary")),
    )(a, b)


def _adamw(w, m, v, parts, name, tr):
    R, C = w.shape
    P = parts.shape[0]

    def body(w_ref, m_ref, v_ref, p_ref, g_ref, d_ref, m2_ref, v2_ref):
        g = p_ref[0].astype(F32)
        for i in range(1, P):
            g = g + p_ref[i].astype(F32)
        m2 = ADAM_B1 * m_ref[...] + (1.0 - ADAM_B1) * g
        v2 = ADAM_B2 * v_ref[...] + (1.0 - ADAM_B2) * jnp.square(g)
        m_hat = m2 / (1.0 - ADAM_B1 ** ADAM_STEP)
        v_hat = v2 / (1.0 - ADAM_B2 ** ADAM_STEP)
        g_ref[...] = g
        d_ref[...] = -ADAM_LR * (m_hat / (jnp.sqrt(v_hat) + ADAM_EPS) + ADAM_WD * w_ref[...])
        m2_ref[...] = m2
        v2_ref[...] = v2

    spec = _rows(tr, C)
    return pl.pallas_call(
        body, name=name, grid=(R // tr,),
        in_specs=[spec, spec, spec, pl.BlockSpec((P, tr, C), lambda i: (0, i, 0))],
        out_specs=[spec] * 4,
        out_shape=[jax.ShapeDtypeStruct((R, C), F32)] * 4,
        compiler_params=_params(("parallel",)),
    )(w, m, v, parts)


def _add_pairs(a, b):
    n, R, C = a.shape
    tr = R // 4

    def body(a_ref, b_ref, o_ref):
        o_ref[...] = a_ref[...] + b_ref[...]

    spec = pl.BlockSpec((1, tr, C), lambda i, j: (i, j, 0))
    return pl.pallas_call(
        body, name="grad_pair_sum", grid=(n, R // tr),
        in_specs=[spec, spec], out_specs=spec,
        out_shape=jax.ShapeDtypeStruct(a.shape, F32),
        compiler_params=_params(("parallel", "parallel")),
    )(a, b)


_HBM = pl.BlockSpec(memory_space=pltpu.HBM)


def _place():
    return lax.axis_index("x"), lax.axis_index("y"), lax.axis_index("c")


def _all_gather(xs, name):
    def body(x_ref, out_ref, send_sems, recv_sems, local_sem):
        x, y, c = _place()
        me, sibling = (x, y, c), (x, y, 1 - c)
        chips = [(1 - x, y), (x, 1 - y), (1 - x, 1 - y)]

        def slot(px, py, pc):
            return out_ref.at[4 * px + 2 * py + pc]

        def copy(k, block, to, src=None):
            return pltpu.make_async_remote_copy(
                src_ref=slot(*block) if src is None else src, dst_ref=slot(*block),
                send_sem=send_sems.at[k], recv_sem=recv_sems.at[k], device_id=to, device_id_type=MESH)

        mine = pltpu.make_async_copy(x_ref, slot(*me), local_sem)
        mine.start()
        first = [copy(0, me, sibling, src=x_ref)]
        first += [copy(1 + j, me, (*chip, c), src=x_ref) for j, chip in enumerate(chips)]
        for cp in first:
            cp.start()
        passed = [copy(4 + j, (*chip, c), sibling) for j, chip in enumerate(chips)]
        for j, chip in enumerate(chips):
            copy(1 + j, (*chip, c), me).wait_recv()
            passed[j].start()
        copy(0, sibling, me).wait_recv()
        for j, chip in enumerate(chips):
            copy(4 + j, (*chip, 1 - c), me).wait_recv()
        for cp in first + passed:
            cp.wait_send()
        mine.wait()

    return pl.pallas_call(
        body, name=name,
        out_shape=jax.ShapeDtypeStruct((N_DEV,) + xs.shape, xs.dtype),
        in_specs=[_HBM], out_specs=_HBM,
        scratch_shapes=[pltpu.SemaphoreType.DMA((7,)), pltpu.SemaphoreType.DMA((7,)), pltpu.SemaphoreType.DMA],
    )(xs)


def _sibling_exchange(send):
    def body(s_ref, r_ref, send_sem, recv_sem):
        x, y, c = _place()
        cp = pltpu.make_async_remote_copy(src_ref=s_ref, dst_ref=r_ref, send_sem=send_sem, recv_sem=recv_sem,
                                          device_id=(x, y, 1 - c), device_id_type=MESH)
        cp.start()
        cp.wait()

    return pl.pallas_call(
        body, name="grad_sibling_exchange",
        out_shape=jax.ShapeDtypeStruct(send.shape, send.dtype),
        in_specs=[_HBM], out_specs=_HBM,
        scratch_shapes=[pltpu.SemaphoreType.DMA, pltpu.SemaphoreType.DMA],
    )(send)


def _chip_exchange(p):
    def body(p_ref, r_ref, send_sems, recv_sems, local_sem):
        x, y, c = _place()
        my_chip = 2 * x + y
        chips = [(1 - x, y), (x, 1 - y), (1 - x, 1 - y)]
        mine = pltpu.make_async_copy(p_ref.at[my_chip], r_ref.at[my_chip], local_sem)
        mine.start()
        sends = []
        for k, (px, py) in enumerate(chips):
            cp = pltpu.make_async_remote_copy(
                src_ref=p_ref.at[2 * px + py], dst_ref=r_ref.at[my_chip],
                send_sem=send_sems.at[k], recv_sem=recv_sems.at[k], device_id=(px, py, c), device_id_type=MESH)
            cp.start()
            sends.append(cp)
        for k, (px, py) in enumerate(chips):
            pltpu.make_async_remote_copy(
                src_ref=p_ref.at[my_chip], dst_ref=r_ref.at[2 * px + py],
                send_sem=send_sems.at[k], recv_sem=recv_sems.at[k], device_id=(px, py, c),
                device_id_type=MESH).wait_recv()
        for cp in sends:
            cp.wait_send()
        mine.wait()

    return pl.pallas_call(
        body, name="grad_chip_exchange",
        out_shape=jax.ShapeDtypeStruct(p.shape, p.dtype),
        in_specs=[_HBM], out_specs=_HBM,
        scratch_shapes=[pltpu.SemaphoreType.DMA((3,)), pltpu.SemaphoreType.DMA((3,)), pltpu.SemaphoreType.DMA],
    )(p)


def _local_step(x, tgt, small, w_in, w_out, w_ff1, w_ff2):
    g1, g2, gf = small["norm1_g"], small["norm2_g"], small["final_norm_g"].reshape(1, D_MODEL)
    ga, gg = small["attn_out_g"], small["gmlp_out_g"]
    ln_g = small["sgu_ln_g"].reshape(1, GMLP_W)
    ln_b = small["sgu_ln_b"].reshape(1, GMLP_W)
    sgu_w = small["sgu_w"][0]
    sgu_bt = small["sgu_b"][0].T

    hn1, q, k, v, u, z = _proj_fwd(x, g1, w_in)
    attn, lse = _attn_fwd(q, k, v)
    gm = _gmlp_fwd(u, z, ln_g, ln_b, sgu_w, sgu_bt)
    mixed, h1, hn2 = _out_fwd(attn, gm, ga, gg, w_out, x, g2)
    a, dh2f, dh2b, loss8, dgf8 = _ffn_fwd(hn2, h1, w_ff1, w_ff2, gf, tgt)

    da, act, dh1f, dh1b, dg2 = _ffn_bwd(dh2b, dh2f, a, h1, g2, w_ff2.T, w_ff1.T)
    dw_ff2 = _dw(act, dh2b, "dw_ff2", 512, 1024)
    dw_ff1 = _dw(hn2, da, "dw_ff1", 512, 1024)
    dattn, dgm, dga, dgg = _out_bwd(dh1b, w_out.T, attn, gm, ga, gg)
    dw_out = _dw(mixed, dh1b, "dw_out", 512, 1024)
    du, dz, dlg, dlb, dsw, dsb = _gmlp_bwd(u, z, dgm, ln_g, ln_b, sgu_w, sgu_bt)
    dq, dk, dv = _attn_bwd(q, k, v, dattn, attn, lse)
    dproj = _dproj_assemble(dq, dk, dv, du, dz)
    dw_in = _dw(hn1, dproj, "dw_in", 512, IN_W // 2)
    dx, dg1 = _proj_bwd(dproj, w_in.T, x, g1, dh1f)

    small_grads = dict(
        norm1_g=dg1[0], sgu_ln_g=dlg[0], sgu_ln_b=dlb[0], sgu_w=dsw, sgu_b=dsb[:, :N_GROUPS].T,
        attn_out_g=dga[0], gmlp_out_g=dgg[0], norm2_g=dg2[0], final_norm_g=dgf8[0])
    return loss8[0, 0], dx, (dw_in, dw_out, dw_ff1, dw_ff2), small_grads


SMALL_NAMES = ("norm1_g", "sgu_ln_g", "sgu_ln_b", "sgu_w", "sgu_b", "attn_out_g", "gmlp_out_g", "norm2_g",
               "final_norm_g")
WEIGHT_ORDER = ("norm1_g", "w_in", "sgu_ln_g", "sgu_ln_b", "sgu_w", "sgu_b", "attn_out_g", "gmlp_out_g", "w_out",
                "norm2_g", "w_ff1", "w_ff2", "final_norm_g")


def _pack_small(d):
    return jnp.concatenate([d[n].reshape(-1, LANES) for n in SMALL_NAMES], axis=0)


def _unpack_small(p, like):
    out, r = {}, 0
    for n in SMALL_NAMES:
        rows = like[n].size // LANES
        out[n] = p[r:r + rows].reshape(like[n].shape)
        r += rows
    return out


def _pack_big_shards(w_in, w_out, w_ff1, w_ff2):
    return jnp.concatenate([w_in.reshape(-1, D_MODEL), w_out.reshape(-1, D_MODEL), w_ff1.reshape(-1, D_MODEL),
                            w_ff2.reshape(-1, D_MODEL)], axis=0)


_R_IN, _R_OUT, _R_FF = IN_W // N_DEV, D_MODEL // N_DEV, D_FF // N_DEV
_O1, _O2, _O3 = _R_IN, _R_IN + _R_OUT, _R_IN + _R_OUT + _R_FF


def _unpack_big_shards(p):
    return dict(w_in=p[0:_O1].reshape(1, D_MODEL, _R_IN), w_out=p[_O1:_O2].reshape(1, _R_OUT, D_MODEL),
                w_ff1=p[_O2:_O3].reshape(1, D_MODEL, _R_FF), w_ff2=p[_O3:].reshape(1, _R_FF, D_MODEL))


def _unpack_gathered(g):
    w_in = g[:, 0:_O1].reshape(N_DEV, D_MODEL, _R_IN).transpose(1, 0, 2).reshape(D_MODEL, IN_W)
    w_out = g[:, _O1:_O2].reshape(D_MODEL, D_MODEL)
    w_ff1 = g[:, _O2:_O3].reshape(N_DEV, D_MODEL, _R_FF).transpose(1, 0, 2).reshape(D_MODEL, D_FF)
    w_ff2 = g[:, _O3:].reshape(D_FF, D_MODEL)
    return w_in, w_out, w_ff1, w_ff2


def _pack_big_grads(dw_in, dw_out, dw_ff1, dw_ff2):
    col = lambda w, n: w.reshape(D_MODEL, 4, 2, n).transpose(2, 1, 0, 3).reshape(2, 4, n, D_MODEL)
    row = lambda w, n: w.reshape(4, 2, n, D_MODEL).transpose(1, 0, 2, 3)
    return jnp.concatenate([col(dw_in, _R_IN), row(dw_out, _R_OUT), col(dw_ff1, _R_FF), row(dw_ff2, _R_FF)], axis=2)


def kernel(x, norm1_g, w_in, sgu_ln_g, sgu_ln_b, sgu_w, sgu_b, attn_out_g, gmlp_out_g, w_out, norm2_g, w_ff1, w_ff2, final_norm_g, loss_target, m_norm1_g, m_w_in, m_sgu_ln_g, m_sgu_ln_b, m_sgu_w, m_sgu_b, m_attn_out_g, m_gmlp_out_g, m_w_out, m_norm2_g, m_w_ff1, m_w_ff2, m_final_norm_g, v_norm1_g, v_w_in, v_sgu_ln_g, v_sgu_ln_b, v_sgu_w, v_sgu_b, v_attn_out_g, v_gmlp_out_g, v_w_out, v_norm2_g, v_w_ff1, v_w_ff2, v_final_norm_g):
    w = dict(norm1_g=norm1_g, w_in=w_in, sgu_ln_g=sgu_ln_g, sgu_ln_b=sgu_ln_b, sgu_w=sgu_w, sgu_b=sgu_b,
             attn_out_g=attn_out_g, gmlp_out_g=gmlp_out_g, w_out=w_out, norm2_g=norm2_g, w_ff1=w_ff1, w_ff2=w_ff2,
             final_norm_g=final_norm_g)
    m = dict(norm1_g=m_norm1_g, w_in=m_w_in, sgu_ln_g=m_sgu_ln_g, sgu_ln_b=m_sgu_ln_b, sgu_w=m_sgu_w, sgu_b=m_sgu_b,
             attn_out_g=m_attn_out_g, gmlp_out_g=m_gmlp_out_g, w_out=m_w_out, norm2_g=m_norm2_g, w_ff1=m_w_ff1,
             w_ff2=m_w_ff2, final_norm_g=m_final_norm_g)
    v = dict(norm1_g=v_norm1_g, w_in=v_w_in, sgu_ln_g=v_sgu_ln_g, sgu_ln_b=v_sgu_ln_b, sgu_w=v_sgu_w, sgu_b=v_sgu_b,
             attn_out_g=v_attn_out_g, gmlp_out_g=v_gmlp_out_g, w_out=v_w_out, norm2_g=v_norm2_g, w_ff1=v_w_ff1,
             w_ff2=v_w_ff2, final_norm_g=v_final_norm_g)
    big = ("w_in", "w_out", "w_ff1", "w_ff2")
    core = lax.axis_index("c")

    shards = _pack_big_shards(*[w[n] for n in big])
    gathered = _all_gather(shards.astype(BF16), "weight_all_gather")
    full = _unpack_gathered(gathered)

    loss, dx, big_grads, small_grads = _local_step(x[0], loss_target[0], {n: w[n] for n in SMALL_NAMES}, *full)
    loss = lax.psum(loss, ("x", "y", "c"))

    packed = _pack_big_grads(*big_grads)
    keep = lax.dynamic_index_in_dim(packed, core, 0, keepdims=False)
    send = lax.dynamic_index_in_dim(packed, 1 - core, 0, keepdims=False)
    chip_sum = _add_pairs(keep, _sibling_exchange(send))
    parts = _chip_exchange(chip_sum)
    gb, db, mb, vb = _adamw(shards, _pack_big_shards(*[m[n] for n in big]), _pack_big_shards(*[v[n] for n in big]),
                            parts, "adamw_large", PACK_ROWS // 4)

    small_parts = _all_gather(_pack_small(small_grads), "small_grad_all_gather")
    gs, ds, ms, vs = _adamw(_pack_small({n: w[n] for n in SMALL_NAMES}), _pack_small({n: m[n] for n in SMALL_NAMES}),
                            _pack_small({n: v[n] for n in SMALL_NAMES}), small_parts, "adamw_small", SMALL_ROWS)

    outs = []
    for pb, ps in ((gb, gs), (db, ds), (mb, ms), (vb, vs)):
        d = dict(_unpack_big_shards(pb))
        d.update(_unpack_small(ps, w))
        outs.extend(d[n] for n in WEIGHT_ORDER)
    return (loss, dx[None], *outs)
```

```python
import functools
import math

import numpy as np
import jax
import jax.numpy as jnp
from jax import lax
from jax.experimental import pallas as pl
from jax.experimental.pallas import tpu as pltpu

F32 = jnp.float32
BF16 = jnp.bfloat16

D_MODEL = 1024
HEAD_DIM = 64
N_HEADS = 12
ATTN_W = N_HEADS * HEAD_DIM
N_GROUPS = 4
GMLP_W = N_GROUPS * HEAD_DIM
IN_W = 3 * ATTN_W + 2 * GMLP_W
D_FF = 4 * D_MODEL
CHUNK = 128
DILATIONS = (1, 4, 16)
EPS = 1e-6
Q_SCALE = HEAD_DIM ** -0.5
NEG = -1e30

ADAM_LR, ADAM_B1, ADAM_B2, ADAM_EPS, ADAM_WD, ADAM_STEP = 0.001, 0.9, 0.999, 1e-08, 0.01, 10

N_DEV = 8
LANES = 128
VMEM_LIMIT = 56 << 20
PACK_ROWS = (IN_W + D_MODEL + D_FF + D_FF) * D_MODEL // N_DEV // D_MODEL
SMALL_ROWS = 552

MESH = pl.DeviceIdType.MESH


def _alibi_slopes(n):
    def pow2(m):
        start = 2.0 ** (-8.0 / m)
        return [start ** (i + 1) for i in range(m)]
    c = 2 ** int(math.floor(math.log2(n)))
    s = pow2(n) if c == n else pow2(c) + pow2(2 * c)[0::2][: n - c]
    return np.asarray(s, dtype=np.float32)


SLOPES = _alibi_slopes(N_HEADS)


def _params(sem=None):
    kw = dict(vmem_limit_bytes=VMEM_LIMIT)
    if sem is not None:
        kw["dimension_semantics"] = sem
    return pltpu.CompilerParams(**kw)


def _rows(tm, n):
    return pl.BlockSpec((tm, n), lambda i: (i, 0))


def _resident(shape):
    return pl.BlockSpec(shape, lambda *_: (0,) * len(shape), pipeline_mode=pl.Buffered(1))


def _rms(x):
    r = lax.rsqrt(jnp.mean(x * x, axis=-1, keepdims=True) + EPS)
    return x * r, r


def _rms_bwd(n, r, g, dy):
    dn = dy * g
    return r * (dn - n * jnp.mean(dn * n, axis=-1, keepdims=True))


def _accum_rows(acc_ref, v):
    acc_ref[...] += jnp.broadcast_to(jnp.sum(v, axis=0, keepdims=True), acc_ref.shape)


_G0 = math.sqrt(2.0 / math.pi)
_G1 = 0.044715


def _gelu(x):
    t = jnp.tanh(_G0 * (x + _G1 * (x * x * x)))
    return x * (0.5 * (1.0 + t)), t


def _gelu_grad(x, t):
    return 0.5 * (1.0 + t) + 0.5 * x * (1.0 - t * t) * (_G0 * (1.0 + 3.0 * _G1 * x * x))


NT = (((1,), (1,)), ((), ()))
TN = (((0,), (0,)), ((), ()))


def _dot(a, b, dims=None):
    if dims is None:
        return jnp.dot(a, b, preferred_element_type=F32)
    return lax.dot_general(a, b, dims, preferred_element_type=F32)


def _proj_fwd(x, g1, w_in_t):
    T = x.shape[0]
    tm = 256

    def body(x_ref, g_ref, w_ref, hn_ref, q_ref, k_ref, v_ref, u_ref, z_ref):
        n, _ = _rms(x_ref[...])
        hn = (n * g_ref[...]).astype(BF16)
        hn_ref[...] = hn
        a = ATTN_W
        q_ref[...] = _dot(hn, w_ref[0:a, :], NT) * Q_SCALE
        k_ref[...] = _dot(hn, w_ref[a:2 * a, :], NT)
        v_ref[...] = _dot(hn, w_ref[2 * a:3 * a, :], NT)
        u_ref[...] = _dot(hn, w_ref[3 * a:3 * a + GMLP_W, :], NT)
        z_ref[...] = _dot(hn, w_ref[3 * a + GMLP_W:, :], NT)

    sds = jax.ShapeDtypeStruct
    return pl.pallas_call(
        body, name="proj_fwd", grid=(T // tm,),
        in_specs=[_rows(tm, D_MODEL), _resident((1, D_MODEL)), _resident((IN_W, D_MODEL))],
        out_specs=[_rows(tm, D_MODEL), _rows(tm, ATTN_W), _rows(tm, ATTN_W), _rows(tm, ATTN_W),
                   _rows(tm, GMLP_W), _rows(tm, GMLP_W)],
        out_shape=[sds((T, D_MODEL), BF16), sds((T, ATTN_W), F32), sds((T, ATTN_W), F32),
                   sds((T, ATTN_W), F32), sds((T, GMLP_W), F32), sds((T, GMLP_W), F32)],
        compiler_params=_params(("parallel",)),
    )(x, g1, w_in_t)


ATT_TILE = 2048
ATT_BLOCKS = ATT_TILE // CHUNK


def _slope_table():
    row = np.repeat(SLOPES, HEAD_DIM)
    return jnp.asarray(np.broadcast_to(row[None], (8, ATTN_W)), F32)


def _stacked_consts(sl_ref):
    shape = (2 * CHUNK, 2 * CHUNK)
    row = lax.broadcasted_iota(jnp.int32, shape, 0)
    kj = lax.broadcasted_iota(jnp.int32, shape, 1)
    steps = (row & (CHUNK - 1)) + CHUNK - kj
    band = (steps >= 0) & (steps <= CHUNK)
    sl = sl_ref[0:1, :]
    upper = lax.broadcasted_iota(jnp.int32, (2 * CHUNK, 1), 0) < CHUNK
    slope2 = jnp.where(upper, sl[:, 0:1], sl[:, HEAD_DIM:HEAD_DIM + 1])
    return kj, band, slope2 * steps.astype(F32)


def _block_rows(j, d):
    if d == 1:
        start = j * CHUNK
        return j, start, pl.ds(start, CHUNK)
    r, b = j % d, j // d
    start = r + (d * CHUNK) * b
    return b, start, pl.ds(start, CHUNK, stride=d)


def _prev_rows(j, d):
    _, start, _ = _block_rows(j, d)
    if d == 1:
        return pl.ds(start - CHUNK, CHUNK), pl.ds(ATT_TILE - CHUNK, CHUNK)
    last = j % d + (d * CHUNK) * (ATT_BLOCKS // d - 1)
    return pl.ds(start - d * CHUNK, CHUNK, stride=d), pl.ds(last, CHUNK, stride=d)


def _kv_block(j, d, rows, kc_ref, kp_ref, vc_ref, vp_ref):
    b = j // d
    here, before = _prev_rows(j, d)
    if ATT_BLOCKS // d == 1:
        kp, vp = kp_ref[before, :], vp_ref[before, :]
    else:
        src_k, src_v, src_rows = (kp_ref, vp_ref, before) if b == 0 else (kc_ref, vc_ref, here)
        kp, vp = src_k[src_rows, :], src_v[src_rows, :]
    kcat = jnp.concatenate([kp, kc_ref[rows, :]], axis=0).astype(BF16)
    vcat = jnp.concatenate([vp, vc_ref[rows, :]], axis=0).astype(BF16)
    return kcat, vcat


def _stack_heads(xb, head0):
    zero = jnp.zeros_like(xb)
    return jnp.concatenate([jnp.where(head0, xb, zero), jnp.where(head0, zero, xb)], axis=0).astype(BF16)


def _unstack_heads(x2, head0):
    return jnp.where(head0, x2[:CHUNK, :], x2[CHUNK:, :])


def _scores(q2, kcat, bias, kj, first):
    s = _dot(q2, kcat, NT) + bias
    return jnp.where(kj < jnp.where(first, CHUNK, 0), NEG, s)


def _attn_tile_specs(nt, lag):
    clamp = (lambda t: jnp.minimum(t, nt - 1)) if lag else (lambda t: t)
    cur = pl.BlockSpec((ATT_TILE, LANES), lambda c, t: (clamp(t), c))
    prev = pl.BlockSpec((ATT_TILE, LANES), lambda c, t: (jnp.maximum(clamp(t) - 1, 0), c))
    slope = pl.BlockSpec((8, LANES), lambda c, t: (0, c))
    return cur, prev, slope


def _attn_fwd(q, k, v, shards=()):
    T = q.shape[0]
    nt = T // ATT_TILE
    ns = len(shards)
    steps = (ATTN_W // LANES) * nt

    def body(sl_ref, q_ref, kc_ref, kp_ref, vc_ref, vp_ref, *rest):
        x_refs, rest = rest[:ns], rest[ns:]
        attn_ref, lse_ref = rest[:2]
        g_refs, rest = rest[2:2 + ns], rest[2 + ns:]
        o_acc, l_acc = rest[:3], rest[3:6]
        t = pl.program_id(1)
        if ns:
            step = pl.program_id(0) * nt + t
            start, forward, finish = _gather_phases(x_refs, g_refs, *rest[6:])
            pl.when(step == 0)(start)
            pl.when(step == steps // 2)(forward)
        kj, band, base = _stacked_consts(sl_ref)
        head0 = lax.broadcasted_iota(jnp.int32, (CHUNK, LANES), 1) < HEAD_DIM
        for pi, d in enumerate(DILATIONS):
            bias = jnp.where(band, -(float(d) * base), NEG)

            def block(j, carry, d=d, pi=pi, bias=bias):
                b, _, rows = _block_rows(j, d)
                kcat, vcat = _kv_block(j, d, rows, kc_ref, kp_ref, vc_ref, vp_ref)
                q2 = _stack_heads(q_ref[rows, :], head0)
                s = _scores(q2, kcat, bias, kj, (t == 0) & (b == 0))
                m = jnp.max(s, axis=-1, keepdims=True)
                p = jnp.exp(s - m)
                l = jnp.sum(p, axis=-1, keepdims=True)
                o_acc[pi][rows, :] = _unstack_heads(_dot(p.astype(BF16), vcat) / l, head0)
                l_acc[pi][rows, :] = _unstack_heads(jnp.broadcast_to(m + jnp.log(l), (2 * CHUNK, LANES)), head0)
                return carry

            for j in range(ATT_BLOCKS):
                block(j, 0)

        for ci in range(ATT_TILE // 256):
            rows = slice(ci * 256, (ci + 1) * 256)
            a, b, c = l_acc[0][rows, :], l_acc[1][rows, :], l_acc[2][rows, :]
            m = jnp.maximum(jnp.maximum(a, b), c)
            ea, eb, ec = jnp.exp(a - m), jnp.exp(b - m), jnp.exp(c - m)
            tot = ea + eb + ec
            attn_ref[rows, :] = (ea * o_acc[0][rows, :] + eb * o_acc[1][rows, :] + ec * o_acc[2][rows, :]) / tot
            lse_ref[rows, :] = m + jnp.log(tot)

        if ns:
            pl.when(step == steps - 1)(finish)

    cur, prev, slope = _attn_tile_specs(nt, lag=False)
    outs = pl.pallas_call(
        body, name="attn_fwd", grid=(ATTN_W // LANES, nt),
        in_specs=[slope, cur, cur, prev, cur, prev] + [_HBM] * ns,
        out_specs=[cur, cur] + [_HBM] * ns,
        out_shape=[jax.ShapeDtypeStruct((T, ATTN_W), F32)] * 2 + [_gathered_shape(s) for s in shards],
        scratch_shapes=[pltpu.VMEM((ATT_TILE, LANES), F32)] * 6 + (_gather_sems(ns) if ns else []),
        compiler_params=_params(("arbitrary", "arbitrary")),
    )(_slope_table(), q, k, k, v, v, *shards)
    return outs[0], outs[1], tuple(outs[2:])


def _group_mean(v, grp):
    out = jnp.zeros_like(v)
    for g in range(N_GROUPS):
        mk = grp == g
        s = jnp.sum(jnp.where(mk, v, 0.0), axis=-1, keepdims=True) * (1.0 / HEAD_DIM)
        out = jnp.where(mk, s, out)
    return out


def _gmlp_core(uu, zz, lg, lb, ws, sb_ref, grp):
    ug, tu = _gelu(uu)
    zg, tz = _gelu(zz)
    zc = zg - _group_mean(zg, grp)
    rstd = lax.rsqrt(_group_mean(zc * zc, grp) + EPS)
    xhat = zc * rstd
    zn16 = (xhat * lg + lb).astype(BF16)
    mixed = jnp.zeros_like(uu)
    for g in range(N_GROUPS):
        mixed = jnp.where(grp == g, _dot(ws[g], zn16) + sb_ref[:, g:g + 1], mixed)
    return ug, tu, tz, xhat, rstd, zn16, mixed


def _causal_ws(w_ref):
    ti = lax.broadcasted_iota(jnp.int32, (CHUNK, CHUNK), 0)
    si = lax.broadcasted_iota(jnp.int32, (CHUNK, CHUNK), 1)
    causal = si <= ti
    return causal, [jnp.where(causal, w_ref[g], 0.0).astype(BF16) for g in range(N_GROUPS)]


def _gmlp_fwd(u, z, ln_g, ln_b, sgu_w, sgu_bt):
    T = u.shape[0]
    tg = 512

    def body(u_ref, z_ref, g_ref, b_ref, w_ref, sb_ref, out_ref):
        grp = lax.broadcasted_iota(jnp.int32, (CHUNK, GMLP_W), 1) // HEAD_DIM
        _, ws = _causal_ws(w_ref)
        for ci in range(tg // CHUNK):
            rows = slice(ci * CHUNK, (ci + 1) * CHUNK)
            ug, _, _, _, _, _, mixed = _gmlp_core(u_ref[rows, :], z_ref[rows, :], g_ref[...], b_ref[...],
                                                  ws, sb_ref, grp)
            out_ref[rows, :] = ug * mixed

    return pl.pallas_call(
        body, name="gmlp_fwd", grid=(T // tg,),
        in_specs=[_rows(tg, GMLP_W), _rows(tg, GMLP_W), _resident((1, GMLP_W)), _resident((1, GMLP_W)),
                  _resident((N_GROUPS, CHUNK, CHUNK)), _resident((CHUNK, N_GROUPS))],
        out_specs=_rows(tg, GMLP_W),
        out_shape=jax.ShapeDtypeStruct((T, GMLP_W), F32),
        compiler_params=_params(("parallel",)),
    )(u, z, ln_g, ln_b, sgu_w, sgu_bt)


def _out_fwd(attn, gm, ga, gg, w_out, x, g2):
    T = x.shape[0]
    tm = 256

    def body(a_ref, m_ref, ga_ref, gg_ref, w_ref, x_ref, g2_ref, mix_ref, h1_ref, hn2_ref):
        an, _ = _rms(a_ref[...])
        gn, _ = _rms(m_ref[...])
        an = (an * ga_ref[...]).astype(BF16)
        gn = (gn * gg_ref[...]).astype(BF16)
        mix_ref[:, 0:ATTN_W] = an
        mix_ref[:, ATTN_W:] = gn
        h1 = x_ref[...] + _dot(an, w_ref[0:ATTN_W, :]) + _dot(gn, w_ref[ATTN_W:, :])
        h1_ref[...] = h1
        n2, _ = _rms(h1)
        hn2_ref[...] = (n2 * g2_ref[...]).astype(BF16)

    sds = jax.ShapeDtypeStruct
    return pl.pallas_call(
        body, name="out_fwd", grid=(T // tm,),
        in_specs=[_rows(tm, ATTN_W), _rows(tm, GMLP_W), _resident((1, ATTN_W)), _resident((1, GMLP_W)),
                  _resident((D_MODEL, D_MODEL)), _rows(tm, D_MODEL), _resident((1, D_MODEL))],
        out_specs=[_rows(tm, D_MODEL)] * 3,
        out_shape=[sds((T, D_MODEL), BF16), sds((T, D_MODEL), F32), sds((T, D_MODEL), BF16)],
        compiler_params=_params(("parallel",)),
    )(attn, gm, ga, gg, w_out, x, g2)


FF_CHUNK = 512


def _ffn_fwd(hn2, h1, w1t, w2, gf, tgt):
    T = h1.shape[0]
    tm = 256

    def body(hn_ref, h1_ref, w1_ref, w2_ref, gf_ref, t_ref, a_ref, dhf_ref, dhb_ref, loss_ref, dgf_ref):
        i = pl.program_id(0)

        @pl.when(i == 0)
        def _():
            loss_ref[...] = jnp.zeros_like(loss_ref)
            dgf_ref[...] = jnp.zeros_like(dgf_ref)

        hn = hn_ref[...]
        acc = h1_ref[...]
        for j in range(D_FF // FF_CHUNK):
            cols = slice(j * FF_CHUNK, (j + 1) * FF_CHUNK)
            a = _dot(hn, w1_ref[cols, :], NT)
            a_ref[:, cols] = a
            act = jnp.square(jnp.maximum(a, 0.0)).astype(BF16)
            acc = acc + _dot(act, w2_ref[cols, :])
        n3, r3 = _rms(acc)
        gf_row = gf_ref[...]
        e = n3 * gf_row - t_ref[...]
        loss_ref[...] += 0.5 * jnp.sum(jnp.mean(e * e, axis=-1, keepdims=True))
        dy = e * (1.0 / D_MODEL)
        _accum_rows(dgf_ref, dy * n3)
        dh2 = _rms_bwd(n3, r3, gf_row, dy)
        dhf_ref[...] = dh2
        dhb_ref[...] = dh2.astype(BF16)

    sds = jax.ShapeDtypeStruct
    acc_spec = lambda n: pl.BlockSpec((8, n), lambda i: (0, 0))
    return pl.pallas_call(
        body, name="ffn_fwd", grid=(T // tm,),
        in_specs=[_rows(tm, D_MODEL), _rows(tm, D_MODEL), _resident((D_FF, D_MODEL)), _resident((D_FF, D_MODEL)),
                  _resident((1, D_MODEL)), _rows(tm, D_MODEL)],
        out_specs=[_rows(tm, D_FF), _rows(tm, D_MODEL), _rows(tm, D_MODEL), acc_spec(LANES), acc_spec(D_MODEL)],
        out_shape=[sds((T, D_FF), F32), sds((T, D_MODEL), F32), sds((T, D_MODEL), BF16),
                   sds((8, LANES), F32), sds((8, D_MODEL), F32)],
        compiler_params=_params(("arbitrary",)),
    )(hn2, h1, w1t, w2, gf, tgt)


def _ffn_bwd(dh2b, dh2f, a, h1, g2, w2, w1t):
    T = h1.shape[0]
    tm = 256

    def body(db_ref, df_ref, a_ref, h1_ref, g2_ref, w2_ref, w1t_ref, da_ref, act_ref, d1f_ref, d1b_ref, dg_ref):
        @pl.when(pl.program_id(0) == 0)
        def _():
            dg_ref[...] = jnp.zeros_like(dg_ref)

        db = db_ref[...]
        acc = jnp.zeros((tm, D_MODEL), F32)
        for j in range(D_FF // FF_CHUNK):
            cols = slice(j * FF_CHUNK, (j + 1) * FF_CHUNK)
            r = jnp.maximum(a_ref[:, cols], 0.0)
            da = (_dot(db, w2_ref[cols, :], NT) * (2.0 * r)).astype(BF16)
            da_ref[:, cols] = da
            act_ref[:, cols] = (r * r).astype(BF16)
            acc = acc + _dot(da, w1t_ref[cols, :])
        n2, r2 = _rms(h1_ref[...])
        _accum_rows(dg_ref, acc * n2)
        dh1 = df_ref[...] + _rms_bwd(n2, r2, g2_ref[...], acc)
        d1f_ref[...] = dh1
        d1b_ref[...] = dh1.astype(BF16)

    sds = jax.ShapeDtypeStruct
    return pl.pallas_call(
        body, name="ffn_bwd", grid=(T // tm,),
        in_specs=[_rows(tm, D_MODEL), _rows(tm, D_MODEL), _rows(tm, D_FF), _rows(tm, D_MODEL),
                  _resident((1, D_MODEL)), _resident((D_FF, D_MODEL)), _resident((D_FF, D_MODEL))],
        out_specs=[_rows(tm, D_FF), _rows(tm, D_FF), _rows(tm, D_MODEL), _rows(tm, D_MODEL),
                   pl.BlockSpec((8, D_MODEL), lambda i: (0, 0))],
        out_shape=[sds((T, D_FF), BF16), sds((T, D_FF), BF16), sds((T, D_MODEL), F32), sds((T, D_MODEL), BF16),
                   sds((8, D_MODEL), F32)],
        compiler_params=_params(("arbitrary",)),
    )(dh2b, dh2f, a, h1, g2, w2, w1t)


def _out_bwd(dh1b, w_out, attn, gm, ga, gg):
    T = attn.shape[0]
    tm = 256

    def body(d_ref, w_ref, a_ref, m_ref, ga_ref, gg_ref, da_ref, dm_ref, dga_ref, dgg_ref):
        @pl.when(pl.program_id(0) == 0)
        def _():
            dga_ref[...] = jnp.zeros_like(dga_ref)
            dgg_ref[...] = jnp.zeros_like(dgg_ref)

        d = d_ref[...]
        dan = _dot(d, w_ref[0:ATTN_W, :], NT)
        dgn = _dot(d, w_ref[ATTN_W:, :], NT)
        na, ra = _rms(a_ref[...])
        ng, rg = _rms(m_ref[...])
        _accum_rows(dga_ref, dan * na)
        _accum_rows(dgg_ref, dgn * ng)
        da_ref[...] = _rms_bwd(na, ra, ga_ref[...], dan)
        dm_ref[...] = _rms_bwd(ng, rg, gg_ref[...], dgn)

    sds = jax.ShapeDtypeStruct
    return pl.pallas_call(
        body, name="out_bwd", grid=(T // tm,),
        in_specs=[_rows(tm, D_MODEL), _resident((D_MODEL, D_MODEL)), _rows(tm, ATTN_W), _rows(tm, GMLP_W),
                  _resident((1, ATTN_W)), _resident((1, GMLP_W))],
        out_specs=[_rows(tm, ATTN_W), _rows(tm, GMLP_W), pl.BlockSpec((8, ATTN_W), lambda i: (0, 0)),
                   pl.BlockSpec((8, GMLP_W), lambda i: (0, 0))],
        out_shape=[sds((T, ATTN_W), F32), sds((T, GMLP_W), F32), sds((8, ATTN_W), F32), sds((8, GMLP_W), F32)],
        compiler_params=_params(("arbitrary",)),
    )(dh1b, w_out, attn, gm, ga, gg)


def _gmlp_bwd(u, z, dgm, ln_g, ln_b, sgu_w, sgu_bt):
    T = u.shape[0]
    tg = 512
    nsteps = T // tg

    def body(u_ref, z_ref, d_ref, g_ref, b_ref, w_ref, sb_ref, du_ref, dz_ref, dlg_ref, dlb_ref, dw_ref, dsb_ref):
        i = pl.program_id(0)

        @pl.when(i == 0)
        def _():
            for ref in (dlg_ref, dlb_ref, dw_ref, dsb_ref):
                ref[...] = jnp.zeros_like(ref)

        grp = lax.broadcasted_iota(jnp.int32, (CHUNK, GMLP_W), 1) // HEAD_DIM
        lane = lax.broadcasted_iota(jnp.int32, (CHUNK, LANES), 1)
        causal, ws = _causal_ws(w_ref)
        lg = g_ref[...]
        for ci in range(tg // CHUNK):
            rows = slice(ci * CHUNK, (ci + 1) * CHUNK)
            uu, zz = u_ref[rows, :], z_ref[rows, :]
            ug, tu, tz, xhat, rstd, zn16, mixed = _gmlp_core(uu, zz, lg, b_ref[...], ws, sb_ref, grp)
            dgm_c = d_ref[rows, :]
            dmx = dgm_c * ug
            du_ref[rows, :] = dgm_c * mixed * _gelu_grad(uu, tu)
            dmx16 = dmx.astype(BF16)
            dzn = jnp.zeros_like(dmx)
            dsb = jnp.zeros((CHUNK, LANES), F32)
            for g in range(N_GROUPS):
                mk = grp == g
                dzn = jnp.where(mk, _dot(ws[g], dmx16, TN), dzn)
                dw_ref[g] += _dot(jnp.where(mk, dmx16, jnp.zeros_like(dmx16)), zn16, NT)
                dsb = jnp.where(lane == g, jnp.sum(jnp.where(mk, dmx, 0.0), axis=-1, keepdims=True), dsb)
            dsb_ref[...] += dsb
            _accum_rows(dlg_ref, dzn * xhat)
            _accum_rows(dlb_ref, dzn)
            dxh = dzn * lg
            dzg = rstd * (dxh - _group_mean(dxh, grp) - xhat * _group_mean(dxh * xhat, grp))
            dz_ref[rows, :] = dzg * _gelu_grad(zz, tz)

        @pl.when(i == nsteps - 1)
        def _():
            for g in range(N_GROUPS):
                dw_ref[g] = jnp.where(causal, dw_ref[g], 0.0)

    sds = jax.ShapeDtypeStruct
    return pl.pallas_call(
        body, name="gmlp_bwd", grid=(nsteps,),
        in_specs=[_rows(tg, GMLP_W)] * 3 + [_resident((1, GMLP_W)), _resident((1, GMLP_W)),
                                              _resident((N_GROUPS, CHUNK, CHUNK)), _resident((CHUNK, N_GROUPS))],
        out_specs=[_rows(tg, GMLP_W), _rows(tg, GMLP_W), pl.BlockSpec((8, GMLP_W), lambda i: (0, 0)),
                   pl.BlockSpec((8, GMLP_W), lambda i: (0, 0)),
                   pl.BlockSpec((N_GROUPS, CHUNK, CHUNK), lambda i: (0, 0, 0)),
                   pl.BlockSpec((CHUNK, LANES), lambda i: (0, 0))],
        out_shape=[sds((T, GMLP_W), F32), sds((T, GMLP_W), F32), sds((8, GMLP_W), F32), sds((8, GMLP_W), F32),
                   sds((N_GROUPS, CHUNK, CHUNK), F32), sds((CHUNK, LANES), F32)],
        compiler_params=_params(("arbitrary",)),
    )(u, z, dgm, ln_g, ln_b, sgu_w, sgu_bt)


def _attn_bwd(q, k, v, dattn, attn, lse, chip_sums=()):
    T = q.shape[0]
    nt = T // ATT_TILE
    ns = len(chip_sums)
    steps = (ATTN_W // LANES) * (nt + 1)

    def body(sl_ref, q_ref, kc_ref, kp_ref, vc_ref, vp_ref, do_ref, o_ref, lse_ref, *rest):
        p_refs, rest = rest[:ns], rest[ns:]
        dq_ref, dk_ref, dv_ref = rest[:3]
        r_refs, rest = rest[3:3 + ns], rest[3 + ns:]
        dk_acc, dv_acc, delta_s = rest[:3]
        t = pl.program_id(1)
        if ns:
            step = pl.program_id(0) * (nt + 1) + t
            start, finish = _chip_exchange_phases(p_refs, r_refs, *rest[3:])
            pl.when(step == 0)(start)
        s_cur = t % 2
        s_prev = 1 - s_cur

        @pl.when(t == 0)
        def _():
            dk_acc[...] = jnp.zeros_like(dk_acc)
            dv_acc[...] = jnp.zeros_like(dv_acc)

        @pl.when(t < nt)
        def _():
            kj, band, base = _stacked_consts(sl_ref)
            head0 = lax.broadcasted_iota(jnp.int32, (CHUNK, LANES), 1) < HEAD_DIM
            for ci in range(ATT_TILE // 256):
                rows = slice(ci * 256, (ci + 1) * 256)
                h0 = lax.broadcasted_iota(jnp.int32, (256, LANES), 1) < HEAD_DIM
                dd = do_ref[rows, :] * o_ref[rows, :]
                d0 = jnp.sum(jnp.where(h0, dd, 0.0), axis=-1, keepdims=True)
                d1 = jnp.sum(jnp.where(h0, 0.0, dd), axis=-1, keepdims=True)
                delta_s[rows, :] = jnp.where(h0, d0, d1)
            dq_ref[...] = jnp.zeros_like(dq_ref)

            def column(xb):
                return jnp.concatenate([xb[:, 0:1], xb[:, HEAD_DIM:HEAD_DIM + 1]], axis=0)

            for d in DILATIONS:
                bias = jnp.where(band, -(float(d) * base), NEG)

                def block(j, carry, d=d, bias=bias):
                    b, _, rows = _block_rows(j, d)
                    kcat, vcat = _kv_block(j, d, rows, kc_ref, kp_ref, vc_ref, vp_ref)
                    q2 = _stack_heads(q_ref[rows, :], head0)
                    do2 = _stack_heads(do_ref[rows, :], head0)
                    s = _scores(q2, kcat, bias, kj, (t == 0) & (b == 0))
                    p = jnp.exp(s - column(lse_ref[rows, :]))
                    ds = (p * (_dot(do2, vcat, NT) - column(delta_s[rows, :]))).astype(BF16)
                    dq_ref[rows, :] += _unstack_heads(_dot(ds, kcat), head0)
                    ck = _dot(ds, q2, TN)
                    cv = _dot(p.astype(BF16), do2, TN)
                    dk_acc[s_cur, rows, :] += ck[CHUNK:, :]
                    dv_acc[s_cur, rows, :] += cv[CHUNK:, :]
                    here, before = _prev_rows(j, d)
                    if ATT_BLOCKS // d == 1:
                        dk_acc[s_prev, before, :] += ck[:CHUNK, :]
                        dv_acc[s_prev, before, :] += cv[:CHUNK, :]
                    else:
                        slot, dst = (s_prev, before) if b == 0 else (s_cur, here)
                        dk_acc[slot, dst, :] += ck[:CHUNK, :]
                        dv_acc[slot, dst, :] += cv[:CHUNK, :]
                    return carry

                for j in range(ATT_BLOCKS):
                    block(j, 0)

        dk_ref[...] = dk_acc[s_prev]
        dv_ref[...] = dv_acc[s_prev]
        dk_acc[s_prev] = jnp.zeros((ATT_TILE, LANES), F32)
        dv_acc[s_prev] = jnp.zeros((ATT_TILE, LANES), F32)

        if ns:
            pl.when(step == steps - 1)(finish)

    cur, prev, slope = _attn_tile_specs(nt, lag=True)
    late = pl.BlockSpec((ATT_TILE, LANES), lambda c, t: (jnp.maximum(t - 1, 0), c))
    sds = jax.ShapeDtypeStruct((T, ATTN_W), F32)
    outs = pl.pallas_call(
        body, name="attn_bwd", grid=(ATTN_W // LANES, nt + 1),
        in_specs=[slope, cur, cur, prev, cur, prev, cur, cur, cur] + [_HBM] * ns,
        out_specs=[cur, late, late] + [_HBM] * ns,
        out_shape=[sds, sds, sds] + [jax.ShapeDtypeStruct(p.shape, p.dtype) for p in chip_sums],
        scratch_shapes=[pltpu.VMEM((2, ATT_TILE, LANES), F32), pltpu.VMEM((2, ATT_TILE, LANES), F32),
                        pltpu.VMEM((ATT_TILE, LANES), F32)] + (_chip_exchange_sems(ns) if ns else []),
        compiler_params=_params(("arbitrary", "arbitrary")),
    )(_slope_table(), q, k, k, v, v, dattn, attn, lse, *chip_sums)
    return outs[0], outs[1], outs[2], tuple(outs[3:])


def _dproj_assemble(dq, dk, dv, du, dz):
    T = du.shape[0]
    tm = 512

    def body(q_ref, k_ref, v_ref, u_ref, z_ref, out_ref):
        a = ATTN_W
        out_ref[:, 0:a] = (q_ref[...] * Q_SCALE).astype(BF16)
        out_ref[:, a:2 * a] = k_ref[...].astype(BF16)
        out_ref[:, 2 * a:3 * a] = v_ref[...].astype(BF16)
        out_ref[:, 3 * a:3 * a + GMLP_W] = u_ref[...].astype(BF16)
        out_ref[:, 3 * a + GMLP_W:] = z_ref[...].astype(BF16)

    return pl.pallas_call(
        body, name="dproj_assemble", grid=(T // tm,),
        in_specs=[_rows(tm, ATTN_W)] * 3 + [_rows(tm, GMLP_W)] * 2,
        out_specs=_rows(tm, IN_W),
        out_shape=jax.ShapeDtypeStruct((T, IN_W), BF16),
        compiler_params=_params(("parallel",)),
    )(dq, dk, dv, du, dz)


def _proj_bwd(dproj, w_in_t, x, g1, dh1):
    T = x.shape[0]
    tm = 256

    def body(d_ref, w_ref, x_ref, g_ref, r_ref, dx_ref, dg_ref):
        @pl.when(pl.program_id(0) == 0)
        def _():
            dg_ref[...] = jnp.zeros_like(dg_ref)

        dhn = _dot(d_ref[...], w_ref[...])
        n1, r1 = _rms(x_ref[...])
        _accum_rows(dg_ref, dhn * n1)
        dx_ref[...] = r_ref[...] + _rms_bwd(n1, r1, g_ref[...], dhn)

    return pl.pallas_call(
        body, name="proj_bwd", grid=(T // tm,),
        in_specs=[_rows(tm, IN_W), _resident((IN_W, D_MODEL)), _rows(tm, D_MODEL), _resident((1, D_MODEL)),
                  _rows(tm, D_MODEL)],
        out_specs=[_rows(tm, D_MODEL), pl.BlockSpec((8, D_MODEL), lambda i: (0, 0))],
        out_shape=[jax.ShapeDtypeStruct((T, D_MODEL), F32), jax.ShapeDtypeStruct((8, D_MODEL), F32)],
        compiler_params=_params(("arbitrary",)),
    )(dproj, w_in_t, x, g1, dh1)


def _dw(a, b, name, tka, tnb):
    T, ka = a.shape
    nb = b.shape[1]
    tt = min(1024, T)

    def body(a_ref, b_ref, o_ref):
        @pl.when(pl.program_id(2) == 0)
        def _():
            o_ref[...] = jnp.zeros_like(o_ref)

        o_ref[...] += _dot(a_ref[...], b_ref[...], TN)

    return pl.pallas_call(
        body, name=name, grid=(ka // tka, nb // tnb, T // tt),
        in_specs=[pl.BlockSpec((tt, tka), lambda i, j, s: (s, i)), pl.BlockSpec((tt, tnb), lambda i, j, s: (s, j))],
        out_specs=pl.BlockSpec((tka, tnb), lambda i, j, s: (i, j)),
        out_shape=jax.ShapeDtypeStruct((ka, nb), F32),
        compiler_params=_params(("parallel", "parallel", "arbitrary")),
    )(a, b)


def _adamw(w, m, v, parts, name, tr, transposed=False):
    R, C = w.shape
    P = parts.shape[0]

    def body(w_ref, m_ref, v_ref, p_ref, g_ref, d_ref, m2_ref, v2_ref):
        g = p_ref[0].astype(F32)
        for i in range(1, P):
            g = g + p_ref[i].astype(F32)
        if transposed:
            g = g.T
        m2 = ADAM_B1 * m_ref[...] + (1.0 - ADAM_B1) * g
        v2 = ADAM_B2 * v_ref[...] + (1.0 - ADAM_B2) * jnp.square(g)
        m_hat = m2 / (1.0 - ADAM_B1 ** ADAM_STEP)
        v_hat = v2 / (1.0 - ADAM_B2 ** ADAM_STEP)
        g_ref[...] = g
        d_ref[...] = -ADAM_LR * (m_hat / (jnp.sqrt(v_hat) + ADAM_EPS) + ADAM_WD * w_ref[...])
        m2_ref[...] = m2
        v2_ref[...] = v2

    spec = _rows(tr, C)
    part_spec = (pl.BlockSpec((P, C, tr), lambda i: (0, 0, i)) if transposed
                 else pl.BlockSpec((P, tr, C), lambda i: (0, i, 0)))
    return pl.pallas_call(
        body, name=name, grid=(R // tr,),
        in_specs=[spec, spec, spec, part_spec],
        out_specs=[spec] * 4,
        out_shape=[jax.ShapeDtypeStruct((R, C), F32)] * 4,
        compiler_params=_params(("parallel",)),
    )(w, m, v, parts)


def _pair_sum(core, grad, recv, name):
    _, _, n, C = grad.shape
    tr = n // 2

    def body(c_ref, a_ref, b_ref, o_ref):
        o_ref[...] = a_ref[...] + b_ref[...]

    spec = pl.BlockSpec((1, tr, C), lambda i, j, c_ref: (i, j, 0))
    return pl.pallas_call(
        body, name=name,
        grid_spec=pltpu.PrefetchScalarGridSpec(
            num_scalar_prefetch=1, grid=(4, n // tr),
            in_specs=[pl.BlockSpec((1, None, tr, C), lambda i, j, c_ref: (i, c_ref[0], j, 0)), spec],
            out_specs=spec),
        out_shape=jax.ShapeDtypeStruct(recv.shape, F32),
        compiler_params=_params(("parallel", "parallel")),
    )(core.reshape(1), grad, recv)


_HBM = pl.BlockSpec(memory_space=pltpu.HBM)


def _place():
    return lax.axis_index("x"), lax.axis_index("y"), lax.axis_index("c")


def _gathered_shape(shard):
    return jax.ShapeDtypeStruct((N_DEV,) + shard.shape, shard.dtype)


def _gather_sems(n):
    return [pltpu.SemaphoreType.DMA((7, n)), pltpu.SemaphoreType.DMA((7, n)), pltpu.SemaphoreType.DMA((n,))]


def _gather_phases(x_refs, out_refs, send_sems, recv_sems, local_sems):
    x, y, c = _place()
    me, sibling = (x, y, c), (x, y, 1 - c)
    chips = [(1 - x, y), (x, 1 - y), (1 - x, 1 - y)]
    arrays = range(len(x_refs))

    def slot(i, px, py, pc):
        return out_refs[i].at[4 * px + 2 * py + pc]

    def copy(i, k, block, to, own=False):
        return pltpu.make_async_remote_copy(
            src_ref=x_refs[i] if own else slot(i, *block), dst_ref=slot(i, *block),
            send_sem=send_sems.at[k, i], recv_sem=recv_sems.at[k, i], device_id=to, device_id_type=MESH)

    def mine(i):
        return pltpu.make_async_copy(x_refs[i], slot(i, *me), local_sems.at[i])

    def start():
        for i in arrays:
            mine(i).start()
            copy(i, 0, me, sibling, own=True).start()
            for j, chip in enumerate(chips):
                copy(i, 1 + j, me, (*chip, c), own=True).start()

    def forward():
        for i in arrays:
            for j, chip in enumerate(chips):
                copy(i, 1 + j, (*chip, c), me).wait_recv()
                copy(i, 4 + j, (*chip, c), sibling).start()

    def finish():
        for i in arrays:
            copy(i, 0, sibling, me).wait_recv()
            copy(i, 0, me, sibling, own=True).wait_send()
            for j, chip in enumerate(chips):
                copy(i, 4 + j, (*chip, 1 - c), me).wait_recv()
                copy(i, 1 + j, me, (*chip, c), own=True).wait_send()
                copy(i, 4 + j, (*chip, c), sibling).wait_send()
            mine(i).wait()

    return start, forward, finish


def _all_gather(shards, name):
    n = len(shards)

    def body(*refs):
        start, forward, finish = _gather_phases(refs[:n], refs[n:2 * n], *refs[2 * n:])
        start()
        forward()
        finish()

    return pl.pallas_call(
        body, name=name,
        out_shape=[_gathered_shape(s) for s in shards],
        in_specs=[_HBM] * n, out_specs=[_HBM] * n,
        scratch_shapes=_gather_sems(n),
    )(*shards)


def _sibling_exchange(grads, name):
    n = len(grads)

    def body(*refs):
        g_refs, r_refs, send_sems, recv_sems = refs[:n], refs[n:2 * n], refs[2 * n], refs[2 * n + 1]
        x, y, c = _place()
        copies = [pltpu.make_async_remote_copy(
            src_ref=g_refs[i].at[:, 1 - c], dst_ref=r_refs[i], send_sem=send_sems.at[i], recv_sem=recv_sems.at[i],
            device_id=(x, y, 1 - c), device_id_type=MESH) for i in range(n)]
        for cp in copies:
            cp.start()
        for cp in copies:
            cp.wait()

    return pl.pallas_call(
        body, name=name,
        out_shape=[jax.ShapeDtypeStruct((g.shape[0],) + g.shape[2:], g.dtype) for g in grads],
        in_specs=[_HBM] * n, out_specs=[_HBM] * n,
        scratch_shapes=[pltpu.SemaphoreType.DMA((n,)), pltpu.SemaphoreType.DMA((n,))],
    )(*grads)


def _chip_exchange_sems(n):
    return [pltpu.SemaphoreType.DMA((3, n)), pltpu.SemaphoreType.DMA((3, n)), pltpu.SemaphoreType.DMA((n,))]


def _chip_exchange_phases(p_refs, r_refs, send_sems, recv_sems, local_sems):
    x, y, c = _place()
    my_chip = 2 * x + y
    chips = [(1 - x, y), (x, 1 - y), (1 - x, 1 - y)]
    arrays = range(len(p_refs))

    def mine(i):
        return pltpu.make_async_copy(p_refs[i].at[my_chip], r_refs[i].at[my_chip], local_sems.at[i])

    def copy(i, k, src_chip, dst_chip):
        px, py = chips[k]
        return pltpu.make_async_remote_copy(
            src_ref=p_refs[i].at[src_chip], dst_ref=r_refs[i].at[dst_chip],
            send_sem=send_sems.at[k, i], recv_sem=recv_sems.at[k, i], device_id=(px, py, c), device_id_type=MESH)

    def start():
        for i in arrays:
            mine(i).start()
            for k, (px, py) in enumerate(chips):
                copy(i, k, 2 * px + py, my_chip).start()

    def finish():
        for i in arrays:
            for k, (px, py) in enumerate(chips):
                copy(i, k, my_chip, 2 * px + py).wait_recv()
                copy(i, k, 2 * px + py, my_chip).wait_send()
            mine(i).wait()

    return start, finish


def _chip_exchange(chip_sums, name):
    n = len(chip_sums)

    def body(*refs):
        start, finish = _chip_exchange_phases(refs[:n], refs[n:2 * n], *refs[2 * n:])
        start()
        finish()

    return pl.pallas_call(
        body, name=name,
        out_shape=[jax.ShapeDtypeStruct(p.shape, p.dtype) for p in chip_sums],
        in_specs=[_HBM] * n, out_specs=[_HBM] * n,
        scratch_shapes=_chip_exchange_sems(n),
    )(*chip_sums)


_R_IN, _R_OUT, _R_FF = IN_W // N_DEV, D_MODEL // N_DEV, D_FF // N_DEV


def _by_owner(g):
    return g.reshape(4, 2, g.shape[0] // N_DEV, D_MODEL)


def _local_step(x, tgt, small, w_in_t, rest, core=None):
    exchange = core is not None
    g1, g2, gf = small["norm1_g"], small["norm2_g"], small["final_norm_g"].reshape(1, D_MODEL)
    ga, gg = small["attn_out_g"], small["gmlp_out_g"]
    ln_g = small["sgu_ln_g"].reshape(1, GMLP_W)
    ln_b = small["sgu_ln_b"].reshape(1, GMLP_W)
    sgu_w = small["sgu_w"][0]
    sgu_bt = small["sgu_b"][0].T

    hn1, q, k, v, u, z = _proj_fwd(x, g1, w_in_t)
    attn, lse, gathered = _attn_fwd(q, k, v, shards=rest if exchange else ())
    w_out, w_ff1_t, w_ff2 = [g.reshape(-1, D_MODEL) for g in gathered] if exchange else rest
    gm = _gmlp_fwd(u, z, ln_g, ln_b, sgu_w, sgu_bt)
    mixed, h1, hn2 = _out_fwd(attn, gm, ga, gg, w_out, x, g2)
    a, dh2f, dh2b, loss8, dgf8 = _ffn_fwd(hn2, h1, w_ff1_t, w_ff2, gf, tgt)

    da, act, dh1f, dh1b, dg2 = _ffn_bwd(dh2b, dh2f, a, h1, g2, w_ff2, w_ff1_t)
    dw_ff2 = _dw(act, dh2b, "dw_ff2", 512, 1024)
    dw_ff1_t = _dw(da, hn2, "dw_ff1", 512, 1024)
    dattn, dgm, dga, dgg = _out_bwd(dh1b, w_out, attn, gm, ga, gg)
    dw_out = _dw(mixed, dh1b, "dw_out", 512, 1024)
    early = [dw_out, dw_ff1_t, dw_ff2]
    if exchange:
        early = [_by_owner(g) for g in early]
        got = _sibling_exchange(early, "grad_sibling_exchange_early")
        early = [_pair_sum(core, g, r, f"grad_pair_sum_{i}") for i, (g, r) in enumerate(zip(early, got))]
    du, dz, dlg, dlb, dsw, dsb = _gmlp_bwd(u, z, dgm, ln_g, ln_b, sgu_w, sgu_bt)
    dq, dk, dv, arrived = _attn_bwd(q, k, v, dattn, attn, lse, chip_sums=early if exchange else ())
    dproj = _dproj_assemble(dq, dk, dv, du, dz)
    dw_in_t = _dw(dproj, hn1, "dw_in", IN_W // 2, 1024)
    dx, dg1 = _proj_bwd(dproj, w_in_t, x, g1, dh1f)
    if exchange:
        late = _by_owner(dw_in_t)
        got, = _sibling_exchange([late], "grad_sibling_exchange_late")
        dw_in_t, = _chip_exchange([_pair_sum(core, late, got, "grad_pair_sum_in")], "grad_chip_exchange_late")
        early = arrived

    small_grads = dict(
        norm1_g=dg1[0], sgu_ln_g=dlg[0], sgu_ln_b=dlb[0], sgu_w=dsw, sgu_b=dsb[:, :N_GROUPS].T,
        attn_out_g=dga[0], gmlp_out_g=dgg[0], norm2_g=dg2[0], final_norm_g=dgf8[0])
    return loss8[0, 0], dx, (dw_in_t, *early), small_grads


SMALL_NAMES = ("norm1_g", "sgu_ln_g", "sgu_ln_b", "sgu_w", "sgu_b", "attn_out_g", "gmlp_out_g", "norm2_g",
               "final_norm_g")
WEIGHT_ORDER = ("norm1_g", "w_in", "sgu_ln_g", "sgu_ln_b", "sgu_w", "sgu_b", "attn_out_g", "gmlp_out_g", "w_out",
                "norm2_g", "w_ff1", "w_ff2", "final_norm_g")


def _pack_small(d):
    return jnp.concatenate([d[n].reshape(-1, LANES) for n in SMALL_NAMES], axis=0)


def _unpack_small(p, like):
    out, r = {}, 0
    for n in SMALL_NAMES:
        rows = like[n].size // LANES
        out[n] = p[r:r + rows].reshape(like[n].shape)
        r += rows
    return out


def kernel(x, norm1_g, w_in, sgu_ln_g, sgu_ln_b, sgu_w, sgu_b, attn_out_g, gmlp_out_g, w_out, norm2_g, w_ff1, w_ff2, final_norm_g, loss_target, m_norm1_g, m_w_in, m_sgu_ln_g, m_sgu_ln_b, m_sgu_w, m_sgu_b, m_attn_out_g, m_gmlp_out_g, m_w_out, m_norm2_g, m_w_ff1, m_w_ff2, m_final_norm_g, v_norm1_g, v_w_in, v_sgu_ln_g, v_sgu_ln_b, v_sgu_w, v_sgu_b, v_attn_out_g, v_gmlp_out_g, v_w_out, v_norm2_g, v_w_ff1, v_w_ff2, v_final_norm_g):
    w = dict(norm1_g=norm1_g, w_in=w_in, sgu_ln_g=sgu_ln_g, sgu_ln_b=sgu_ln_b, sgu_w=sgu_w, sgu_b=sgu_b,
             attn_out_g=attn_out_g, gmlp_out_g=gmlp_out_g, w_out=w_out, norm2_g=norm2_g, w_ff1=w_ff1, w_ff2=w_ff2,
             final_norm_g=final_norm_g)
    m = dict(norm1_g=m_norm1_g, w_in=m_w_in, sgu_ln_g=m_sgu_ln_g, sgu_ln_b=m_sgu_ln_b, sgu_w=m_sgu_w, sgu_b=m_sgu_b,
             attn_out_g=m_attn_out_g, gmlp_out_g=m_gmlp_out_g, w_out=m_w_out, norm2_g=m_norm2_g, w_ff1=m_w_ff1,
             w_ff2=m_w_ff2, final_norm_g=m_final_norm_g)
    v = dict(norm1_g=v_norm1_g, w_in=v_w_in, sgu_ln_g=v_sgu_ln_g, sgu_ln_b=v_sgu_ln_b, sgu_w=v_sgu_w, sgu_b=v_sgu_b,
             attn_out_g=v_attn_out_g, gmlp_out_g=v_gmlp_out_g, w_out=v_w_out, norm2_g=v_norm2_g, w_ff1=v_w_ff1,
             w_ff2=v_w_ff2, final_norm_g=v_final_norm_g)
    big = ("w_in", "w_out", "w_ff1", "w_ff2")
    core = lax.axis_index("c")

    w_in_t, = _all_gather([w_in[0].T.astype(BF16)], "w_in_all_gather")
    rest = (w_out[0].astype(BF16), w_ff1[0].T.astype(BF16), w_ff2[0].astype(BF16))
    loss, dx, parts, small_grads = _local_step(x[0], loss_target[0], {n: w[n] for n in SMALL_NAMES},
                                               w_in_t.reshape(IN_W, D_MODEL), rest, core=core)
    loss = lax.psum(loss, ("x", "y", "c"))

    new = {}
    for n, p, transposed, tr in zip(big, parts, (True, False, True, False), (128, 128, 128, 256)):
        new[n] = [a[None] for a in _adamw(w[n][0], m[n][0], v[n][0], p, "adamw_" + n, tr, transposed)]

    small_parts, = _all_gather([_pack_small(small_grads)], "small_grad_all_gather")
    packed = _adamw(_pack_small({n: w[n] for n in SMALL_NAMES}), _pack_small({n: m[n] for n in SMALL_NAMES}),
                    _pack_small({n: v[n] for n in SMALL_NAMES}), small_parts, "adamw_small", SMALL_ROWS)

    outs = []
    for i, ps in enumerate(packed):
        d = {n: new[n][i] for n in big}
        d.update(_unpack_small(ps, w))
        outs.extend(d[n] for n in WEIGHT_ORDER)
    return (loss, dx[None], *outs)
```

```python
import functools
import math

import numpy as np
import jax
import jax.numpy as jnp
from jax import lax
from jax.experimental import pallas as pl
from jax.experimental.pallas import tpu as pltpu

F32 = jnp.float32
BF16 = jnp.bfloat16

D_MODEL = 1024
HEAD_DIM = 64
N_HEADS = 12
ATTN_W = N_HEADS * HEAD_DIM
N_GROUPS = 4
GMLP_W = N_GROUPS * HEAD_DIM
IN_W = 3 * ATTN_W + 2 * GMLP_W
D_FF = 4 * D_MODEL
CHUNK = 128
DILATIONS = (1, 4, 16)
EPS = 1e-6
Q_SCALE = HEAD_DIM ** -0.5
NEG = -1e30

ADAM_LR, ADAM_B1, ADAM_B2, ADAM_EPS, ADAM_WD, ADAM_STEP = 0.001, 0.9, 0.999, 1e-08, 0.01, 10

N_DEV = 8
LANES = 128
VMEM_LIMIT = 56 << 20
SMALL_ROWS = 552

TM_PROJ = 512
TM_FFN = 512
FF_CHUNK = 512
DW_TILE = (512, 1024, 4096)
DW_TILE_IN = (IN_W // 2, 1024, 2048)

MESH = pl.DeviceIdType.MESH


def _alibi_slopes(n):
    def pow2(m):
        start = 2.0 ** (-8.0 / m)
        return [start ** (i + 1) for i in range(m)]
    c = 2 ** int(math.floor(math.log2(n)))
    s = pow2(n) if c == n else pow2(c) + pow2(2 * c)[0::2][: n - c]
    return np.asarray(s, dtype=np.float32)


SLOPES = _alibi_slopes(N_HEADS)


def _params(sem=None):
    kw = dict(vmem_limit_bytes=VMEM_LIMIT)
    if sem is not None:
        kw["dimension_semantics"] = sem
    return pltpu.CompilerParams(**kw)


def _rows(tm, n):
    return pl.BlockSpec((tm, n), lambda i: (i, 0))


def _resident(shape):
    return pl.BlockSpec(shape, lambda *_: (0,) * len(shape), pipeline_mode=pl.Buffered(1))


def _rms(x):
    r = lax.rsqrt(jnp.mean(x * x, axis=-1, keepdims=True) + EPS)
    return x * r, r


def _rms_bwd(n, r, g, dy):
    dn = dy * g
    return r * (dn - n * jnp.mean(dn * n, axis=-1, keepdims=True))


def _accum_rows(acc_ref, v):
    acc_ref[...] += jnp.broadcast_to(jnp.sum(v, axis=0, keepdims=True), acc_ref.shape)


_G0 = math.sqrt(2.0 / math.pi)
_G1 = 0.044715


def _gelu(x):
    t = jnp.tanh(_G0 * (x + _G1 * (x * x * x)))
    return x * (0.5 * (1.0 + t)), t


def _gelu_grad(x, t):
    return 0.5 * (1.0 + t) + 0.5 * x * (1.0 - t * t) * (_G0 * (1.0 + 3.0 * _G1 * x * x))


NT = (((1,), (1,)), ((), ()))
TN = (((0,), (0,)), ((), ()))


def _dot(a, b, dims=None):
    if dims is None:
        return jnp.dot(a, b, preferred_element_type=F32)
    return lax.dot_general(a, b, dims, preferred_element_type=F32)


def _proj_fwd(x, g1, w_in_t):
    T = x.shape[0]
    tm = TM_PROJ

    def body(x_ref, g_ref, w_ref, hn_ref, q_ref, k_ref, v_ref, u_ref, z_ref):
        n, _ = _rms(x_ref[...])
        hn = (n * g_ref[...]).astype(BF16)
        hn_ref[...] = hn
        a = ATTN_W
        q_ref[...] = _dot(hn, w_ref[0:a, :], NT) * Q_SCALE
        k_ref[...] = _dot(hn, w_ref[a:2 * a, :], NT)
        v_ref[...] = _dot(hn, w_ref[2 * a:3 * a, :], NT)
        u_ref[...] = _dot(hn, w_ref[3 * a:3 * a + GMLP_W, :], NT)
        z_ref[...] = _dot(hn, w_ref[3 * a + GMLP_W:, :], NT)

    sds = jax.ShapeDtypeStruct
    return pl.pallas_call(
        body, name="proj_fwd", grid=(T // tm,),
        in_specs=[_rows(tm, D_MODEL), _resident((1, D_MODEL)), _resident((IN_W, D_MODEL))],
        out_specs=[_rows(tm, D_MODEL), _rows(tm, ATTN_W), _rows(tm, ATTN_W), _rows(tm, ATTN_W),
                   _rows(tm, GMLP_W), _rows(tm, GMLP_W)],
        out_shape=[sds((T, D_MODEL), BF16), sds((T, ATTN_W), F32), sds((T, ATTN_W), F32),
                   sds((T, ATTN_W), F32), sds((T, GMLP_W), F32), sds((T, GMLP_W), F32)],
        compiler_params=_params(("parallel",)),
    )(x, g1, w_in_t)


ATT_TILE = 2048
ATT_BLOCKS = ATT_TILE // CHUNK


def _slope_table():
    row = np.repeat(SLOPES, HEAD_DIM)
    return jnp.asarray(np.broadcast_to(row[None], (8, ATTN_W)), F32)


def _stacked_consts(sl_ref):
    shape = (2 * CHUNK, 2 * CHUNK)
    row = lax.broadcasted_iota(jnp.int32, shape, 0)
    kj = lax.broadcasted_iota(jnp.int32, shape, 1)
    steps = (row & (CHUNK - 1)) + CHUNK - kj
    band = (steps >= 0) & (steps <= CHUNK)
    sl = sl_ref[0:1, :]
    upper = lax.broadcasted_iota(jnp.int32, (2 * CHUNK, 1), 0) < CHUNK
    slope2 = jnp.where(upper, sl[:, 0:1], sl[:, HEAD_DIM:HEAD_DIM + 1])
    return kj, band, slope2 * steps.astype(F32)


def _block_rows(j, d):
    if d == 1:
        start = j * CHUNK
        return j, start, pl.ds(start, CHUNK)
    r, b = j % d, j // d
    start = r + (d * CHUNK) * b
    return b, start, pl.ds(start, CHUNK, stride=d)


def _prev_rows(j, d):
    _, start, _ = _block_rows(j, d)
    if d == 1:
        return pl.ds(start - CHUNK, CHUNK), pl.ds(ATT_TILE - CHUNK, CHUNK)
    last = j % d + (d * CHUNK) * (ATT_BLOCKS // d - 1)
    return pl.ds(start - d * CHUNK, CHUNK, stride=d), pl.ds(last, CHUNK, stride=d)


def _kv_block(j, d, rows, kc_ref, kp_ref, vc_ref, vp_ref):
    b = j // d
    here, before = _prev_rows(j, d)
    if ATT_BLOCKS // d == 1:
        kp, vp = kp_ref[before, :], vp_ref[before, :]
    else:
        src_k, src_v, src_rows = (kp_ref, vp_ref, before) if b == 0 else (kc_ref, vc_ref, here)
        kp, vp = src_k[src_rows, :], src_v[src_rows, :]
    kcat = jnp.concatenate([kp, kc_ref[rows, :]], axis=0).astype(BF16)
    vcat = jnp.concatenate([vp, vc_ref[rows, :]], axis=0).astype(BF16)
    return kcat, vcat


def _stack_heads(xb, head0):
    zero = jnp.zeros_like(xb)
    return jnp.concatenate([jnp.where(head0, xb, zero), jnp.where(head0, zero, xb)], axis=0).astype(BF16)


def _unstack_heads(x2, head0):
    return jnp.where(head0, x2[:CHUNK, :], x2[CHUNK:, :])


def _scores(q2, kcat, bias, kj, first):
    s = _dot(q2, kcat, NT) + bias
    return jnp.where(kj < jnp.where(first, CHUNK, 0), NEG, s)


def _attn_tile_specs(nt, lag):
    clamp = (lambda t: jnp.minimum(t, nt - 1)) if lag else (lambda t: t)
    cur = pl.BlockSpec((ATT_TILE, LANES), lambda c, t: (clamp(t), c))
    prev = pl.BlockSpec((ATT_TILE, LANES), lambda c, t: (jnp.maximum(clamp(t) - 1, 0), c))
    slope = pl.BlockSpec((8, LANES), lambda c, t: (0, c))
    return cur, prev, slope


def _attn_fwd(q, k, v, shards=()):
    T = q.shape[0]
    nt = T // ATT_TILE
    ns = len(shards)
    steps = (ATTN_W // LANES) * nt

    def body(sl_ref, q_ref, kc_ref, kp_ref, vc_ref, vp_ref, *rest):
        x_refs, rest = rest[:ns], rest[ns:]
        attn_ref, lse_ref = rest[:2]
        g_refs, rest = rest[2:2 + ns], rest[2 + ns:]
        o_acc, l_acc = rest[:3], rest[3:6]
        t = pl.program_id(1)
        if ns:
            step = pl.program_id(0) * nt + t
            start, forward, finish = _gather_phases(x_refs, g_refs, *rest[6:])
            pl.when(step == 0)(start)
            pl.when(step == steps // 2)(forward)
        kj, band, base = _stacked_consts(sl_ref)
        head0 = lax.broadcasted_iota(jnp.int32, (CHUNK, LANES), 1) < HEAD_DIM
        for pi, d in enumerate(DILATIONS):
            bias = jnp.where(band, -(float(d) * base), NEG)

            def block(j, carry, d=d, pi=pi, bias=bias):
                b, _, rows = _block_rows(j, d)
                kcat, vcat = _kv_block(j, d, rows, kc_ref, kp_ref, vc_ref, vp_ref)
                q2 = _stack_heads(q_ref[rows, :], head0)
                s = _scores(q2, kcat, bias, kj, (t == 0) & (b == 0))
                m = jnp.max(s, axis=-1, keepdims=True)
                p = jnp.exp(s - m)
                l = jnp.sum(p, axis=-1, keepdims=True)
                o_acc[pi][rows, :] = _unstack_heads(_dot(p.astype(BF16), vcat) / l, head0)
                l_acc[pi][rows, :] = _unstack_heads(jnp.broadcast_to(m + jnp.log(l), (2 * CHUNK, LANES)), head0)
                return carry

            for j in range(ATT_BLOCKS):
                block(j, 0)

        for ci in range(ATT_TILE // 256):
            rows = slice(ci * 256, (ci + 1) * 256)
            a, b, c = l_acc[0][rows, :], l_acc[1][rows, :], l_acc[2][rows, :]
            m = jnp.maximum(jnp.maximum(a, b), c)
            ea, eb, ec = jnp.exp(a - m), jnp.exp(b - m), jnp.exp(c - m)
            tot = ea + eb + ec
            attn_ref[rows, :] = (ea * o_acc[0][rows, :] + eb * o_acc[1][rows, :] + ec * o_acc[2][rows, :]) / tot
            lse_ref[rows, :] = m + jnp.log(tot)

        if ns:
            pl.when(step == steps - 1)(finish)

    cur, prev, slope = _attn_tile_specs(nt, lag=False)
    outs = pl.pallas_call(
        body, name="attn_fwd", grid=(ATTN_W // LANES, nt),
        in_specs=[slope, cur, cur, prev, cur, prev] + [_HBM] * ns,
        out_specs=[cur, cur] + [_HBM] * ns,
        out_shape=[jax.ShapeDtypeStruct((T, ATTN_W), F32)] * 2 + [_gathered_shape(s) for s in shards],
        scratch_shapes=[pltpu.VMEM((ATT_TILE, LANES), F32)] * 6 + (_gather_sems(ns) if ns else []),
        compiler_params=_params(("arbitrary", "arbitrary")),
    )(_slope_table(), q, k, k, v, v, *shards)
    return outs[0], outs[1], tuple(outs[2:])


def _group_mean(v, grp):
    out = jnp.zeros_like(v)
    for g in range(N_GROUPS):
        mk = grp == g
        s = jnp.sum(jnp.where(mk, v, 0.0), axis=-1, keepdims=True) * (1.0 / HEAD_DIM)
        out = jnp.where(mk, s, out)
    return out


def _gmlp_core(uu, zz, lg, lb, ws, sb_ref, grp):
    ug, tu = _gelu(uu)
    zg, tz = _gelu(zz)
    zc = zg - _group_mean(zg, grp)
    rstd = lax.rsqrt(_group_mean(zc * zc, grp) + EPS)
    xhat = zc * rstd
    zn16 = (xhat * lg + lb).astype(BF16)
    mixed = jnp.zeros_like(uu)
    for g in range(N_GROUPS):
        mixed = jnp.where(grp == g, _dot(ws[g], zn16) + sb_ref[:, g:g + 1], mixed)
    return ug, tu, tz, xhat, rstd, zn16, mixed


def _causal_ws(w_ref):
    ti = lax.broadcasted_iota(jnp.int32, (CHUNK, CHUNK), 0)
    si = lax.broadcasted_iota(jnp.int32, (CHUNK, CHUNK), 1)
    causal = si <= ti
    return causal, [jnp.where(causal, w_ref[g], 0.0).astype(BF16) for g in range(N_GROUPS)]


def _gmlp_fwd(u, z, ln_g, ln_b, sgu_w, sgu_bt):
    T = u.shape[0]
    tg = 512

    def body(u_ref, z_ref, g_ref, b_ref, w_ref, sb_ref, out_ref):
        grp = lax.broadcasted_iota(jnp.int32, (CHUNK, GMLP_W), 1) // HEAD_DIM
        _, ws = _causal_ws(w_ref)
        for ci in range(tg // CHUNK):
            rows = slice(ci * CHUNK, (ci + 1) * CHUNK)
            ug, _, _, _, _, _, mixed = _gmlp_core(u_ref[rows, :], z_ref[rows, :], g_ref[...], b_ref[...],
                                                  ws, sb_ref, grp)
            out_ref[rows, :] = ug * mixed

    return pl.pallas_call(
        body, name="gmlp_fwd", grid=(T // tg,),
        in_specs=[_rows(tg, GMLP_W), _rows(tg, GMLP_W), _resident((1, GMLP_W)), _resident((1, GMLP_W)),
                  _resident((N_GROUPS, CHUNK, CHUNK)), _resident((CHUNK, N_GROUPS))],
        out_specs=_rows(tg, GMLP_W),
        out_shape=jax.ShapeDtypeStruct((T, GMLP_W), F32),
        compiler_params=_params(("parallel",)),
    )(u, z, ln_g, ln_b, sgu_w, sgu_bt)


def _out_fwd(attn, gm, ga, gg, w_out, x, g2):
    T = x.shape[0]
    tm = TM_PROJ

    def body(a_ref, m_ref, ga_ref, gg_ref, w_ref, x_ref, g2_ref, mix_ref, h1_ref, hn2_ref):
        an, _ = _rms(a_ref[...])
        gn, _ = _rms(m_ref[...])
        an = (an * ga_ref[...]).astype(BF16)
        gn = (gn * gg_ref[...]).astype(BF16)
        mix_ref[:, 0:ATTN_W] = an
        mix_ref[:, ATTN_W:] = gn
        h1 = x_ref[...] + _dot(an, w_ref[0:ATTN_W, :]) + _dot(gn, w_ref[ATTN_W:, :])
        h1_ref[...] = h1
        n2, _ = _rms(h1)
        hn2_ref[...] = (n2 * g2_ref[...]).astype(BF16)

    sds = jax.ShapeDtypeStruct
    return pl.pallas_call(
        body, name="out_fwd", grid=(T // tm,),
        in_specs=[_rows(tm, ATTN_W), _rows(tm, GMLP_W), _resident((1, ATTN_W)), _resident((1, GMLP_W)),
                  _resident((D_MODEL, D_MODEL)), _rows(tm, D_MODEL), _resident((1, D_MODEL))],
        out_specs=[_rows(tm, D_MODEL)] * 3,
        out_shape=[sds((T, D_MODEL), BF16), sds((T, D_MODEL), F32), sds((T, D_MODEL), BF16)],
        compiler_params=_params(("parallel",)),
    )(attn, gm, ga, gg, w_out, x, g2)


def _ffn_fwd(hn2, h1, w1t, w2, gf, tgt):
    T = h1.shape[0]
    tm = TM_FFN

    def body(hn_ref, h1_ref, w1_ref, w2_ref, gf_ref, t_ref, r_ref, dhf_ref, dhb_ref, loss_ref, dgf_ref):
        i = pl.program_id(0)

        @pl.when(i == 0)
        def _():
            loss_ref[...] = jnp.zeros_like(loss_ref)
            dgf_ref[...] = jnp.zeros_like(dgf_ref)

        hn = hn_ref[...]
        acc = h1_ref[...]
        for j in range(D_FF // FF_CHUNK):
            cols = slice(j * FF_CHUNK, (j + 1) * FF_CHUNK)
            r = jnp.maximum(_dot(hn, w1_ref[cols, :], NT), 0.0)
            r_ref[:, cols] = r.astype(BF16)
            act = jnp.square(r).astype(BF16)
            acc = acc + _dot(act, w2_ref[cols, :])
        n3, r3 = _rms(acc)
        gf_row = gf_ref[...]
        e = n3 * gf_row - t_ref[...]
        loss_ref[...] += 0.5 * jnp.sum(jnp.mean(e * e, axis=-1, keepdims=True))
        dy = e * (1.0 / D_MODEL)
        _accum_rows(dgf_ref, dy * n3)
        dh2 = _rms_bwd(n3, r3, gf_row, dy)
        dhf_ref[...] = dh2
        dhb_ref[...] = dh2.astype(BF16)

    sds = jax.ShapeDtypeStruct
    acc_spec = lambda n: pl.BlockSpec((8, n), lambda i: (0, 0))
    return pl.pallas_call(
        body, name="ffn_fwd", grid=(T // tm,),
        in_specs=[_rows(tm, D_MODEL), _rows(tm, D_MODEL), _resident((D_FF, D_MODEL)), _resident((D_FF, D_MODEL)),
                  _resident((1, D_MODEL)), _rows(tm, D_MODEL)],
        out_specs=[_rows(tm, D_FF), _rows(tm, D_MODEL), _rows(tm, D_MODEL), acc_spec(LANES), acc_spec(D_MODEL)],
        out_shape=[sds((T, D_FF), BF16), sds((T, D_MODEL), F32), sds((T, D_MODEL), BF16),
                   sds((8, LANES), F32), sds((8, D_MODEL), F32)],
        compiler_params=_params(("arbitrary",)),
    )(hn2, h1, w1t, w2, gf, tgt)


def _ffn_bwd(dh2b, dh2f, relu, h1, g2, w2, w1t):
    T = h1.shape[0]
    tm = TM_FFN

    def body(db_ref, df_ref, r_ref, h1_ref, g2_ref, w2_ref, w1t_ref, da_ref, d1f_ref, d1b_ref, dg_ref):
        @pl.when(pl.program_id(0) == 0)
        def _():
            dg_ref[...] = jnp.zeros_like(dg_ref)

        db = db_ref[...]
        acc = jnp.zeros((tm, D_MODEL), F32)
        for j in range(D_FF // FF_CHUNK):
            cols = slice(j * FF_CHUNK, (j + 1) * FF_CHUNK)
            da = (_dot(db, w2_ref[cols, :], NT) * (2.0 * r_ref[:, cols].astype(F32))).astype(BF16)
            da_ref[:, cols] = da
            acc = acc + _dot(da, w1t_ref[cols, :])
        n2, r2 = _rms(h1_ref[...])
        _accum_rows(dg_ref, acc * n2)
        dh1 = df_ref[...] + _rms_bwd(n2, r2, g2_ref[...], acc)
        d1f_ref[...] = dh1
        d1b_ref[...] = dh1.astype(BF16)

    sds = jax.ShapeDtypeStruct
    return pl.pallas_call(
        body, name="ffn_bwd", grid=(T // tm,),
        in_specs=[_rows(tm, D_MODEL), _rows(tm, D_MODEL), _rows(tm, D_FF), _rows(tm, D_MODEL),
                  _resident((1, D_MODEL)), _resident((D_FF, D_MODEL)), _resident((D_FF, D_MODEL))],
        out_specs=[_rows(tm, D_FF), _rows(tm, D_MODEL), _rows(tm, D_MODEL),
                   pl.BlockSpec((8, D_MODEL), lambda i: (0, 0))],
        out_shape=[sds((T, D_FF), BF16), sds((T, D_MODEL), F32), sds((T, D_MODEL), BF16), sds((8, D_MODEL), F32)],
        compiler_params=_params(("arbitrary",)),
    )(dh2b, dh2f, relu, h1, g2, w2, w1t)


def _out_bwd(dh1b, w_out, attn, gm, ga, gg):
    T = attn.shape[0]
    tm = TM_PROJ

    def body(d_ref, w_ref, a_ref, m_ref, ga_ref, gg_ref, da_ref, dm_ref, dga_ref, dgg_ref):
        @pl.when(pl.program_id(0) == 0)
        def _():
            dga_ref[...] = jnp.zeros_like(dga_ref)
            dgg_ref[...] = jnp.zeros_like(dgg_ref)

        d = d_ref[...]
        dan = _dot(d, w_ref[0:ATTN_W, :], NT)
        dgn = _dot(d, w_ref[ATTN_W:, :], NT)
        na, ra = _rms(a_ref[...])
        ng, rg = _rms(m_ref[...])
        _accum_rows(dga_ref, dan * na)
        _accum_rows(dgg_ref, dgn * ng)
        da_ref[...] = _rms_bwd(na, ra, ga_ref[...], dan)
        dm_ref[...] = _rms_bwd(ng, rg, gg_ref[...], dgn)

    sds = jax.ShapeDtypeStruct
    return pl.pallas_call(
        body, name="out_bwd", grid=(T // tm,),
        in_specs=[_rows(tm, D_MODEL), _resident((D_MODEL, D_MODEL)), _rows(tm, ATTN_W), _rows(tm, GMLP_W),
                  _resident((1, ATTN_W)), _resident((1, GMLP_W))],
        out_specs=[_rows(tm, ATTN_W), _rows(tm, GMLP_W), pl.BlockSpec((8, ATTN_W), lambda i: (0, 0)),
                   pl.BlockSpec((8, GMLP_W), lambda i: (0, 0))],
        out_shape=[sds((T, ATTN_W), F32), sds((T, GMLP_W), F32), sds((8, ATTN_W), F32), sds((8, GMLP_W), F32)],
        compiler_params=_params(("arbitrary",)),
    )(dh1b, w_out, attn, gm, ga, gg)


def _gmlp_bwd(u, z, dgm, ln_g, ln_b, sgu_w, sgu_bt):
    T = u.shape[0]
    tg = 512
    nsteps = T // tg

    def body(u_ref, z_ref, d_ref, g_ref, b_ref, w_ref, sb_ref, du_ref, dz_ref, dlg_ref, dlb_ref, dw_ref, dsb_ref):
        i = pl.program_id(0)

        @pl.when(i == 0)
        def _():
            for ref in (dlg_ref, dlb_ref, dw_ref, dsb_ref):
                ref[...] = jnp.zeros_like(ref)

        grp = lax.broadcasted_iota(jnp.int32, (CHUNK, GMLP_W), 1) // HEAD_DIM
        lane = lax.broadcasted_iota(jnp.int32, (CHUNK, LANES), 1)
        causal, ws = _causal_ws(w_ref)
        lg = g_ref[...]
        for ci in range(tg // CHUNK):
            rows = slice(ci * CHUNK, (ci + 1) * CHUNK)
            uu, zz = u_ref[rows, :], z_ref[rows, :]
            ug, tu, tz, xhat, rstd, zn16, mixed = _gmlp_core(uu, zz, lg, b_ref[...], ws, sb_ref, grp)
            dgm_c = d_ref[rows, :]
            dmx = dgm_c * ug
            du_ref[rows, :] = dgm_c * mixed * _gelu_grad(uu, tu)
            dmx16 = dmx.astype(BF16)
            dzn = jnp.zeros_like(dmx)
            dsb = jnp.zeros((CHUNK, LANES), F32)
            for g in range(N_GROUPS):
                mk = grp == g
                dzn = jnp.where(mk, _dot(ws[g], dmx16, TN), dzn)
                dw_ref[g] += _dot(jnp.where(mk, dmx16, jnp.zeros_like(dmx16)), zn16, NT)
                dsb = jnp.where(lane == g, jnp.sum(jnp.where(mk, dmx, 0.0), axis=-1, keepdims=True), dsb)
            dsb_ref[...] += dsb
            _accum_rows(dlg_ref, dzn * xhat)
            _accum_rows(dlb_ref, dzn)
            dxh = dzn * lg
            dzg = rstd * (dxh - _group_mean(dxh, grp) - xhat * _group_mean(dxh * xhat, grp))
            dz_ref[rows, :] = dzg * _gelu_grad(zz, tz)

        @pl.when(i == nsteps - 1)
        def _():
            for g in range(N_GROUPS):
                dw_ref[g] = jnp.where(causal, dw_ref[g], 0.0)

    sds = jax.ShapeDtypeStruct
    return pl.pallas_call(
        body, name="gmlp_bwd", grid=(nsteps,),
        in_specs=[_rows(tg, GMLP_W)] * 3 + [_resident((1, GMLP_W)), _resident((1, GMLP_W)),
                                              _resident((N_GROUPS, CHUNK, CHUNK)), _resident((CHUNK, N_GROUPS))],
        out_specs=[_rows(tg, GMLP_W), _rows(tg, GMLP_W), pl.BlockSpec((8, GMLP_W), lambda i: (0, 0)),
                   pl.BlockSpec((8, GMLP_W), lambda i: (0, 0)),
                   pl.BlockSpec((N_GROUPS, CHUNK, CHUNK), lambda i: (0, 0, 0)),
                   pl.BlockSpec((CHUNK, LANES), lambda i: (0, 0))],
        out_shape=[sds((T, GMLP_W), F32), sds((T, GMLP_W), F32), sds((8, GMLP_W), F32), sds((8, GMLP_W), F32),
                   sds((N_GROUPS, CHUNK, CHUNK), F32), sds((CHUNK, LANES), F32)],
        compiler_params=_params(("arbitrary",)),
    )(u, z, dgm, ln_g, ln_b, sgu_w, sgu_bt)


def _attn_bwd(q, k, v, dattn, attn, lse, chip_sums=()):
    T = q.shape[0]
    nt = T // ATT_TILE
    ns = len(chip_sums)
    steps = (ATTN_W // LANES) * (nt + 1)

    def body(sl_ref, q_ref, kc_ref, kp_ref, vc_ref, vp_ref, do_ref, o_ref, lse_ref, *rest):
        p_refs, rest = rest[:ns], rest[ns:]
        dq_ref, dk_ref, dv_ref = rest[:3]
        r_refs, rest = rest[3:3 + ns], rest[3 + ns:]
        dk_acc, dv_acc, delta_s = rest[:3]
        t = pl.program_id(1)
        if ns:
            step = pl.program_id(0) * (nt + 1) + t
            start, finish = _chip_exchange_phases(p_refs, r_refs, *rest[3:])
            pl.when(step == 0)(start)
        s_cur = t % 2
        s_prev = 1 - s_cur

        @pl.when(t == 0)
        def _():
            dk_acc[...] = jnp.zeros_like(dk_acc)
            dv_acc[...] = jnp.zeros_like(dv_acc)

        @pl.when(t < nt)
        def _():
            kj, band, base = _stacked_consts(sl_ref)
            head0 = lax.broadcasted_iota(jnp.int32, (CHUNK, LANES), 1) < HEAD_DIM
            for ci in range(ATT_TILE // 256):
                rows = slice(ci * 256, (ci + 1) * 256)
                h0 = lax.broadcasted_iota(jnp.int32, (256, LANES), 1) < HEAD_DIM
                dd = do_ref[rows, :] * o_ref[rows, :]
                d0 = jnp.sum(jnp.where(h0, dd, 0.0), axis=-1, keepdims=True)
                d1 = jnp.sum(jnp.where(h0, 0.0, dd), axis=-1, keepdims=True)
                delta_s[rows, :] = jnp.where(h0, d0, d1)
            dq_ref[...] = jnp.zeros_like(dq_ref)

            def column(xb):
                return jnp.concatenate([xb[:, 0:1], xb[:, HEAD_DIM:HEAD_DIM + 1]], axis=0)

            for d in DILATIONS:
                bias = jnp.where(band, -(float(d) * base), NEG)

                def block(j, carry, d=d, bias=bias):
                    b, _, rows = _block_rows(j, d)
                    kcat, vcat = _kv_block(j, d, rows, kc_ref, kp_ref, vc_ref, vp_ref)
                    q2 = _stack_heads(q_ref[rows, :], head0)
                    do2 = _stack_heads(do_ref[rows, :], head0)
                    s = _scores(q2, kcat, bias, kj, (t == 0) & (b == 0))
                    p = jnp.exp(s - column(lse_ref[rows, :]))
                    ds = (p * (_dot(do2, vcat, NT) - column(delta_s[rows, :]))).astype(BF16)
                    dq_ref[rows, :] += _unstack_heads(_dot(ds, kcat), head0)
                    ck = _dot(ds, q2, TN)
                    cv = _dot(p.astype(BF16), do2, TN)
                    dk_acc[s_cur, rows, :] += ck[CHUNK:, :]
                    dv_acc[s_cur, rows, :] += cv[CHUNK:, :]
                    here, before = _prev_rows(j, d)
                    if ATT_BLOCKS // d == 1:
                        dk_acc[s_prev, before, :] += ck[:CHUNK, :]
                        dv_acc[s_prev, before, :] += cv[:CHUNK, :]
                    else:
                        slot, dst = (s_prev, before) if b == 0 else (s_cur, here)
                        dk_acc[slot, dst, :] += ck[:CHUNK, :]
                        dv_acc[slot, dst, :] += cv[:CHUNK, :]
                    return carry

                for j in range(ATT_BLOCKS):
                    block(j, 0)

        dk_ref[...] = dk_acc[s_prev]
        dv_ref[...] = dv_acc[s_prev]
        dk_acc[s_prev] = jnp.zeros((ATT_TILE, LANES), F32)
        dv_acc[s_prev] = jnp.zeros((ATT_TILE, LANES), F32)

        if ns:
            pl.when(step == steps - 1)(finish)

    cur, prev, slope = _attn_tile_specs(nt, lag=True)
    late = pl.BlockSpec((ATT_TILE, LANES), lambda c, t: (jnp.maximum(t - 1, 0), c))
    sds = jax.ShapeDtypeStruct((T, ATTN_W), F32)
    outs = pl.pallas_call(
        body, name="attn_bwd", grid=(ATTN_W // LANES, nt + 1),
        in_specs=[slope, cur, cur, prev, cur, prev, cur, cur, cur] + [_HBM] * ns,
        out_specs=[cur, late, late] + [_HBM] * ns,
        out_shape=[sds, sds, sds] + [jax.ShapeDtypeStruct(p.shape, p.dtype) for p in chip_sums],
        scratch_shapes=[pltpu.VMEM((2, ATT_TILE, LANES), F32), pltpu.VMEM((2, ATT_TILE, LANES), F32),
                        pltpu.VMEM((ATT_TILE, LANES), F32)] + (_chip_exchange_sems(ns) if ns else []),
        compiler_params=_params(("arbitrary", "arbitrary")),
    )(_slope_table(), q, k, k, v, v, dattn, attn, lse, *chip_sums)
    return outs[0], outs[1], outs[2], tuple(outs[3:])


def _dproj_assemble(dq, dk, dv, du, dz):
    T = du.shape[0]
    tm = 512

    def body(q_ref, k_ref, v_ref, u_ref, z_ref, out_ref):
        a = ATTN_W
        out_ref[:, 0:a] = (q_ref[...] * Q_SCALE).astype(BF16)
        out_ref[:, a:2 * a] = k_ref[...].astype(BF16)
        out_ref[:, 2 * a:3 * a] = v_ref[...].astype(BF16)
        out_ref[:, 3 * a:3 * a + GMLP_W] = u_ref[...].astype(BF16)
        out_ref[:, 3 * a + GMLP_W:] = z_ref[...].astype(BF16)

    return pl.pallas_call(
        body, name="dproj_assemble", grid=(T // tm,),
        in_specs=[_rows(tm, ATTN_W)] * 3 + [_rows(tm, GMLP_W)] * 2,
        out_specs=_rows(tm, IN_W),
        out_shape=jax.ShapeDtypeStruct((T, IN_W), BF16),
        compiler_params=_params(("parallel",)),
    )(dq, dk, dv, du, dz)


def _proj_bwd(dproj, w_in_t, x, g1, dh1):
    T = x.shape[0]
    tm = TM_PROJ

    def body(d_ref, w_ref, x_ref, g_ref, r_ref, dx_ref, dg_ref):
        @pl.when(pl.program_id(0) == 0)
        def _():
            dg_ref[...] = jnp.zeros_like(dg_ref)

        dhn = _dot(d_ref[...], w_ref[...])
        n1, r1 = _rms(x_ref[...])
        _accum_rows(dg_ref, dhn * n1)
        dx_ref[...] = r_ref[...] + _rms_bwd(n1, r1, g_ref[...], dhn)

    return pl.pallas_call(
        body, name="proj_bwd", grid=(T // tm,),
        in_specs=[_rows(tm, IN_W), _resident((IN_W, D_MODEL)), _rows(tm, D_MODEL), _resident((1, D_MODEL)),
                  _rows(tm, D_MODEL)],
        out_specs=[_rows(tm, D_MODEL), pl.BlockSpec((8, D_MODEL), lambda i: (0, 0))],
        out_shape=[jax.ShapeDtypeStruct((T, D_MODEL), F32), jax.ShapeDtypeStruct((8, D_MODEL), F32)],
        compiler_params=_params(("arbitrary",)),
    )(dproj, w_in_t, x, g1, dh1)


def _dw(a, b, name, tile, square_a=False):
    T, ka = a.shape
    nb = b.shape[1]
    tka, tnb, tt = tile
    tt = min(tt, T)

    def body(a_ref, b_ref, o_ref):
        @pl.when(pl.program_id(2) == 0)
        def _():
            o_ref[...] = jnp.zeros_like(o_ref)

        a_tile = a_ref[...]
        if square_a:
            a_tile = jnp.square(a_tile.astype(F32)).astype(BF16)
        o_ref[...] += _dot(a_tile, b_ref[...], TN)

    return pl.pallas_call(
        body, name=name, grid=(ka // tka, nb // tnb, T // tt),
        in_specs=[pl.BlockSpec((tt, tka), lambda i, j, s: (s, i)), pl.BlockSpec((tt, tnb), lambda i, j, s: (s, j))],
        out_specs=pl.BlockSpec((tka, tnb), lambda i, j, s: (i, j)),
        out_shape=jax.ShapeDtypeStruct((ka, nb), F32),
        compiler_params=_params(("parallel", "parallel", "arbitrary")),
    )(a, b)


def _adamw(w, m, v, parts, name, tr, transposed=False):
    R, C = w.shape
    P = parts.shape[0]

    def body(w_ref, m_ref, v_ref, p_ref, g_ref, d_ref, m2_ref, v2_ref):
        g = p_ref[0].astype(F32)
        for i in range(1, P):
            g = g + p_ref[i].astype(F32)
        if transposed:
            g = g.T
        m2 = ADAM_B1 * m_ref[...] + (1.0 - ADAM_B1) * g
        v2 = ADAM_B2 * v_ref[...] + (1.0 - ADAM_B2) * jnp.square(g)
        m_hat = m2 / (1.0 - ADAM_B1 ** ADAM_STEP)
        v_hat = v2 / (1.0 - ADAM_B2 ** ADAM_STEP)
        g_ref[...] = g
        d_ref[...] = -ADAM_LR * (m_hat / (jnp.sqrt(v_hat) + ADAM_EPS) + ADAM_WD * w_ref[...])
        m2_ref[...] = m2
        v2_ref[...] = v2

    spec = _rows(tr, C)
    part_spec = (pl.BlockSpec((P, C, tr), lambda i: (0, 0, i)) if transposed
                 else pl.BlockSpec((P, tr, C), lambda i: (0, i, 0)))
    return pl.pallas_call(
        body, name=name, grid=(R // tr,),
        in_specs=[spec, spec, spec, part_spec],
        out_specs=[spec] * 4,
        out_shape=[jax.ShapeDtypeStruct((R, C), F32)] * 4,
        compiler_params=_params(("parallel",)),
    )(w, m, v, parts)


def _pair_sum(core, grad, recv, name):
    _, _, n, C = grad.shape
    tr = n // 2

    def body(c_ref, a_ref, b_ref, o_ref):
        o_ref[...] = a_ref[...] + b_ref[...]

    spec = pl.BlockSpec((1, tr, C), lambda i, j, c_ref: (i, j, 0))
    return pl.pallas_call(
        body, name=name,
        grid_spec=pltpu.PrefetchScalarGridSpec(
            num_scalar_prefetch=1, grid=(4, n // tr),
            in_specs=[pl.BlockSpec((1, None, tr, C), lambda i, j, c_ref: (i, c_ref[0], j, 0)), spec],
            out_specs=spec),
        out_shape=jax.ShapeDtypeStruct(recv.shape, F32),
        compiler_params=_params(("parallel", "parallel")),
    )(core.reshape(1), grad, recv)


_HBM = pl.BlockSpec(memory_space=pltpu.HBM)


def _place():
    return lax.axis_index("x"), lax.axis_index("y"), lax.axis_index("c")


def _gathered_shape(shard):
    return jax.ShapeDtypeStruct((N_DEV,) + shard.shape, shard.dtype)


def _gather_sems(n):
    return [pltpu.SemaphoreType.DMA((7, n)), pltpu.SemaphoreType.DMA((7, n)), pltpu.SemaphoreType.DMA((n,))]


def _gather_phases(x_refs, out_refs, send_sems, recv_sems, local_sems):
    x, y, c = _place()
    me, sibling = (x, y, c), (x, y, 1 - c)
    chips = [(1 - x, y), (x, 1 - y), (1 - x, 1 - y)]
    arrays = range(len(x_refs))

    def slot(i, px, py, pc):
        return out_refs[i].at[4 * px + 2 * py + pc]

    def copy(i, k, block, to, own=False):
        return pltpu.make_async_remote_copy(
            src_ref=x_refs[i] if own else slot(i, *block), dst_ref=slot(i, *block),
            send_sem=send_sems.at[k, i], recv_sem=recv_sems.at[k, i], device_id=to, device_id_type=MESH)

    def mine(i):
        return pltpu.make_async_copy(x_refs[i], slot(i, *me), local_sems.at[i])

    def start():
        for i in arrays:
            mine(i).start()
            copy(i, 0, me, sibling, own=True).start()
            for j, chip in enumerate(chips):
                copy(i, 1 + j, me, (*chip, c), own=True).start()

    def forward():
        for i in arrays:
            for j, chip in enumerate(chips):
                copy(i, 1 + j, (*chip, c), me).wait_recv()
                copy(i, 4 + j, (*chip, c), sibling).start()

    def finish():
        for i in arrays:
            copy(i, 0, sibling, me).wait_recv()
            copy(i, 0, me, sibling, own=True).wait_send()
            for j, chip in enumerate(chips):
                copy(i, 4 + j, (*chip, 1 - c), me).wait_recv()
                copy(i, 1 + j, me, (*chip, c), own=True).wait_send()
                copy(i, 4 + j, (*chip, c), sibling).wait_send()
            mine(i).wait()

    return start, forward, finish


def _all_gather(shards, name):
    n = len(shards)

    def body(*refs):
        start, forward, finish = _gather_phases(refs[:n], refs[n:2 * n], *refs[2 * n:])
        start()
        forward()
        finish()

    return pl.pallas_call(
        body, name=name,
        out_shape=[_gathered_shape(s) for s in shards],
        in_specs=[_HBM] * n, out_specs=[_HBM] * n,
        scratch_shapes=_gather_sems(n),
    )(*shards)


def _sibling_exchange(grads, name):
    n = len(grads)

    def body(*refs):
        g_refs, r_refs, send_sems, recv_sems = refs[:n], refs[n:2 * n], refs[2 * n], refs[2 * n + 1]
        x, y, c = _place()
        copies = [pltpu.make_async_remote_copy(
            src_ref=g_refs[i].at[:, 1 - c], dst_ref=r_refs[i], send_sem=send_sems.at[i], recv_sem=recv_sems.at[i],
            device_id=(x, y, 1 - c), device_id_type=MESH) for i in range(n)]
        for cp in copies:
            cp.start()
        for cp in copies:
            cp.wait()

    return pl.pallas_call(
        body, name=name,
        out_shape=[jax.ShapeDtypeStruct((g.shape[0],) + g.shape[2:], g.dtype) for g in grads],
        in_specs=[_HBM] * n, out_specs=[_HBM] * n,
        scratch_shapes=[pltpu.SemaphoreType.DMA((n,)), pltpu.SemaphoreType.DMA((n,))],
    )(*grads)


def _chip_exchange_sems(n):
    return [pltpu.SemaphoreType.DMA((3, n)), pltpu.SemaphoreType.DMA((3, n)), pltpu.SemaphoreType.DMA((n,))]


def _chip_exchange_phases(p_refs, r_refs, send_sems, recv_sems, local_sems):
    x, y, c = _place()
    my_chip = 2 * x + y
    chips = [(1 - x, y), (x, 1 - y), (1 - x, 1 - y)]
    arrays = range(len(p_refs))

    def mine(i):
        return pltpu.make_async_copy(p_refs[i].at[my_chip], r_refs[i].at[my_chip], local_sems.at[i])

    def copy(i, k, src_chip, dst_chip):
        px, py = chips[k]
        return pltpu.make_async_remote_copy(
            src_ref=p_refs[i].at[src_chip], dst_ref=r_refs[i].at[dst_chip],
            send_sem=send_sems.at[k, i], recv_sem=recv_sems.at[k, i], device_id=(px, py, c), device_id_type=MESH)

    def start():
        for i in arrays:
            mine(i).start()
            for k, (px, py) in enumerate(chips):
                copy(i, k, 2 * px + py, my_chip).start()

    def finish():
        for i in arrays:
            for k, (px, py) in enumerate(chips):
                copy(i, k, my_chip, 2 * px + py).wait_recv()
                copy(i, k, 2 * px + py, my_chip).wait_send()
            mine(i).wait()

    return start, finish


def _chip_exchange(chip_sums, name):
    n = len(chip_sums)

    def body(*refs):
        start, finish = _chip_exchange_phases(refs[:n], refs[n:2 * n], *refs[2 * n:])
        start()
        finish()

    return pl.pallas_call(
        body, name=name,
        out_shape=[jax.ShapeDtypeStruct(p.shape, p.dtype) for p in chip_sums],
        in_specs=[_HBM] * n, out_specs=[_HBM] * n,
        scratch_shapes=_chip_exchange_sems(n),
    )(*chip_sums)


_R_IN, _R_OUT, _R_FF = IN_W // N_DEV, D_MODEL // N_DEV, D_FF // N_DEV


def _by_owner(g):
    return g.reshape(4, 2, g.shape[0] // N_DEV, D_MODEL)


def _local_step(x, tgt, small, w_in_t, rest, core=None):
    exchange = core is not None
    g1, g2, gf = small["norm1_g"], small["norm2_g"], small["final_norm_g"].reshape(1, D_MODEL)
    ga, gg = small["attn_out_g"], small["gmlp_out_g"]
    ln_g = small["sgu_ln_g"].reshape(1, GMLP_W)
    ln_b = small["sgu_ln_b"].reshape(1, GMLP_W)
    sgu_w = small["sgu_w"][0]
    sgu_bt = small["sgu_b"][0].T

    hn1, q, k, v, u, z = _proj_fwd(x, g1, w_in_t)
    attn, lse, gathered = _attn_fwd(q, k, v, shards=rest if exchange else ())
    w_out, w_ff1_t, w_ff2 = [g.reshape(-1, D_MODEL) for g in gathered] if exchange else rest
    gm = _gmlp_fwd(u, z, ln_g, ln_b, sgu_w, sgu_bt)
    mixed, h1, hn2 = _out_fwd(attn, gm, ga, gg, w_out, x, g2)
    relu, dh2f, dh2b, loss8, dgf8 = _ffn_fwd(hn2, h1, w_ff1_t, w_ff2, gf, tgt)

    da, dh1f, dh1b, dg2 = _ffn_bwd(dh2b, dh2f, relu, h1, g2, w_ff2, w_ff1_t)
    dw_ff2 = _dw(relu, dh2b, "dw_ff2", DW_TILE, square_a=True)
    dw_ff1_t = _dw(da, hn2, "dw_ff1", DW_TILE)
    dattn, dgm, dga, dgg = _out_bwd(dh1b, w_out, attn, gm, ga, gg)
    dw_out = _dw(mixed, dh1b, "dw_out", DW_TILE)
    early = [dw_out, dw_ff1_t, dw_ff2]
    if exchange:
        early = [_by_owner(g) for g in early]
        got = _sibling_exchange(early, "grad_sibling_exchange_early")
        early = [_pair_sum(core, g, r, f"grad_pair_sum_{i}") for i, (g, r) in enumerate(zip(early, got))]
    du, dz, dlg, dlb, dsw, dsb = _gmlp_bwd(u, z, dgm, ln_g, ln_b, sgu_w, sgu_bt)
    dq, dk, dv, arrived = _attn_bwd(q, k, v, dattn, attn, lse, chip_sums=early if exchange else ())
    dproj = _dproj_assemble(dq, dk, dv, du, dz)
    dw_in_t = _dw(dproj, hn1, "dw_in", DW_TILE_IN)
    dx, dg1 = _proj_bwd(dproj, w_in_t, x, g1, dh1f)
    if exchange:
        late = _by_owner(dw_in_t)
        got, = _sibling_exchange([late], "grad_sibling_exchange_late")
        dw_in_t, = _chip_exchange([_pair_sum(core, late, got, "grad_pair_sum_in")], "grad_chip_exchange_late")
        early = arrived

    small_grads = dict(
        norm1_g=dg1[0], sgu_ln_g=dlg[0], sgu_ln_b=dlb[0], sgu_w=dsw, sgu_b=dsb[:, :N_GROUPS].T,
        attn_out_g=dga[0], gmlp_out_g=dgg[0], norm2_g=dg2[0], final_norm_g=dgf8[0])
    return loss8[0, 0], dx, (dw_in_t, *early), small_grads


SMALL_NAMES = ("norm1_g", "sgu_ln_g", "sgu_ln_b", "sgu_w", "sgu_b", "attn_out_g", "gmlp_out_g", "norm2_g",
               "final_norm_g")
WEIGHT_ORDER = ("norm1_g", "w_in", "sgu_ln_g", "sgu_ln_b", "sgu_w", "sgu_b", "attn_out_g", "gmlp_out_g", "w_out",
                "norm2_g", "w_ff1", "w_ff2", "final_norm_g")


def _pack_small(d):
    return jnp.concatenate([d[n].reshape(-1, LANES) for n in SMALL_NAMES], axis=0)


def _unpack_small(p, like):
    out, r = {}, 0
    for n in SMALL_NAMES:
        rows = like[n].size // LANES
        out[n] = p[r:r + rows].reshape(like[n].shape)
        r += rows
    return out


def kernel(x, norm1_g, w_in, sgu_ln_g, sgu_ln_b, sgu_w, sgu_b, attn_out_g, gmlp_out_g, w_out, norm2_g, w_ff1, w_ff2, final_norm_g, loss_target, m_norm1_g, m_w_in, m_sgu_ln_g, m_sgu_ln_b, m_sgu_w, m_sgu_b, m_attn_out_g, m_gmlp_out_g, m_w_out, m_norm2_g, m_w_ff1, m_w_ff2, m_final_norm_g, v_norm1_g, v_w_in, v_sgu_ln_g, v_sgu_ln_b, v_sgu_w, v_sgu_b, v_attn_out_g, v_gmlp_out_g, v_w_out, v_norm2_g, v_w_ff1, v_w_ff2, v_final_norm_g):
    w = dict(norm1_g=norm1_g, w_in=w_in, sgu_ln_g=sgu_ln_g, sgu_ln_b=sgu_ln_b, sgu_w=sgu_w, sgu_b=sgu_b,
             attn_out_g=attn_out_g, gmlp_out_g=gmlp_out_g, w_out=w_out, norm2_g=norm2_g, w_ff1=w_ff1, w_ff2=w_ff2,
             final_norm_g=final_norm_g)
    m = dict(norm1_g=m_norm1_g, w_in=m_w_in, sgu_ln_g=m_sgu_ln_g, sgu_ln_b=m_sgu_ln_b, sgu_w=m_sgu_w, sgu_b=m_sgu_b,
             attn_out_g=m_attn_out_g, gmlp_out_g=m_gmlp_out_g, w_out=m_w_out, norm2_g=m_norm2_g, w_ff1=m_w_ff1,
             w_ff2=m_w_ff2, final_norm_g=m_final_norm_g)
    v = dict(norm1_g=v_norm1_g, w_in=v_w_in, sgu_ln_g=v_sgu_ln_g, sgu_ln_b=v_sgu_ln_b, sgu_w=v_sgu_w, sgu_b=v_sgu_b,
             attn_out_g=v_attn_out_g, gmlp_out_g=v_gmlp_out_g, w_out=v_w_out, norm2_g=v_norm2_g, w_ff1=v_w_ff1,
             w_ff2=v_w_ff2, final_norm_g=v_final_norm_g)
    big = ("w_in", "w_out", "w_ff1", "w_ff2")
    core = lax.axis_index("c")

    w_in_t, = _all_gather([w_in[0].T.astype(BF16)], "w_in_all_gather")
    rest = (w_out[0].astype(BF16), w_ff1[0].T.astype(BF16), w_ff2[0].astype(BF16))
    loss, dx, parts, small_grads = _local_step(x[0], loss_target[0], {n: w[n] for n in SMALL_NAMES},
                                               w_in_t.reshape(IN_W, D_MODEL), rest, core=core)
    loss = lax.psum(loss, ("x", "y", "c"))

    new = {}
    for n, p, transposed, tr in zip(big, parts, (True, False, True, False), (128, 128, 128, 256)):
        new[n] = [a[None] for a in _adamw(w[n][0], m[n][0], v[n][0], p, "adamw_" + n, tr, transposed)]

    small_parts, = _all_gather([_pack_small(small_grads)], "small_grad_all_gather")
    packed = _adamw(_pack_small({n: w[n] for n in SMALL_NAMES}), _pack_small({n: m[n] for n in SMALL_NAMES}),
                    _pack_small({n: v[n] for n in SMALL_NAMES}), small_parts, "adamw_small", SMALL_ROWS)

    outs = []
    for i, ps in enumerate(packed):
        d = {n: new[n][i] for n in big}
        d.update(_unpack_small(ps, w))
        outs.extend(d[n] for n in WEIGHT_ORDER)
    return (loss, dx[None], *outs)
```

```python
import functools
import math

import numpy as np
import jax
import jax.numpy as jnp
from jax import lax
from jax.experimental import pallas as pl
from jax.experimental.pallas import tpu as pltpu

F32 = jnp.float32
BF16 = jnp.bfloat16

D_MODEL = 1024
HEAD_DIM = 64
N_HEADS = 12
ATTN_W = N_HEADS * HEAD_DIM
N_GROUPS = 4
GMLP_W = N_GROUPS * HEAD_DIM
IN_W = 3 * ATTN_W + 2 * GMLP_W
D_FF = 4 * D_MODEL
CHUNK = 128
DILATIONS = (1, 4, 16)
EPS = 1e-6
Q_SCALE = HEAD_DIM ** -0.5
NEG = -1e30

ADAM_LR, ADAM_B1, ADAM_B2, ADAM_EPS, ADAM_WD, ADAM_STEP = 0.001, 0.9, 0.999, 1e-08, 0.01, 10

N_DEV = 8
LANES = 128
VMEM_LIMIT = 56 << 20
SMALL_ROWS = 552

TM_PROJ = 512
TM_FFN = 512
FF_CHUNK = 512
DW_TILE = (512, 1024, 4096)
DW_TILE_IN = (IN_W // 2, 1024, 2048)

MESH = pl.DeviceIdType.MESH


def _alibi_slopes(n):
    def pow2(m):
        start = 2.0 ** (-8.0 / m)
        return [start ** (i + 1) for i in range(m)]
    c = 2 ** int(math.floor(math.log2(n)))
    s = pow2(n) if c == n else pow2(c) + pow2(2 * c)[0::2][: n - c]
    return np.asarray(s, dtype=np.float32)


SLOPES = _alibi_slopes(N_HEADS)


def _params(sem=None):
    kw = dict(vmem_limit_bytes=VMEM_LIMIT)
    if sem is not None:
        kw["dimension_semantics"] = sem
    return pltpu.CompilerParams(**kw)


def _rows(tm, n):
    return pl.BlockSpec((tm, n), lambda i: (i, 0))


def _resident(shape):
    return pl.BlockSpec(shape, lambda *_: (0,) * len(shape), pipeline_mode=pl.Buffered(1))


def _rms(x):
    r = lax.rsqrt(jnp.mean(x * x, axis=-1, keepdims=True) + EPS)
    return x * r, r


def _rms_bwd(n, r, g, dy):
    dn = dy * g
    return r * (dn - n * jnp.mean(dn * n, axis=-1, keepdims=True))


def _accum_rows(acc_ref, v):
    acc_ref[...] += jnp.broadcast_to(jnp.sum(v, axis=0, keepdims=True), acc_ref.shape)


_G0 = math.sqrt(2.0 / math.pi)
_G1 = 0.044715


def _gelu(x):
    t = jnp.tanh(_G0 * (x + _G1 * (x * x * x)))
    return x * (0.5 * (1.0 + t)), t


def _gelu_grad(x, t):
    return 0.5 * (1.0 + t) + 0.5 * x * (1.0 - t * t) * (_G0 * (1.0 + 3.0 * _G1 * x * x))


NT = (((1,), (1,)), ((), ()))
TN = (((0,), (0,)), ((), ()))


def _dot(a, b, dims=None):
    if dims is None:
        return jnp.dot(a, b, preferred_element_type=F32)
    return lax.dot_general(a, b, dims, preferred_element_type=F32)


def _proj_fwd(x, g1, w_in_t):
    T = x.shape[0]
    tm = TM_PROJ

    def body(x_ref, g_ref, w_ref, hn_ref, q_ref, k_ref, v_ref, u_ref, z_ref):
        n, _ = _rms(x_ref[...])
        hn = (n * g_ref[...]).astype(BF16)
        hn_ref[...] = hn
        a = ATTN_W
        q_ref[...] = _dot(hn, w_ref[0:a, :], NT) * Q_SCALE
        k_ref[...] = _dot(hn, w_ref[a:2 * a, :], NT)
        v_ref[...] = _dot(hn, w_ref[2 * a:3 * a, :], NT)
        u_ref[...] = _dot(hn, w_ref[3 * a:3 * a + GMLP_W, :], NT)
        z_ref[...] = _dot(hn, w_ref[3 * a + GMLP_W:, :], NT)

    sds = jax.ShapeDtypeStruct
    return pl.pallas_call(
        body, name="proj_fwd", grid=(T // tm,),
        in_specs=[_rows(tm, D_MODEL), _resident((1, D_MODEL)), _resident((IN_W, D_MODEL))],
        out_specs=[_rows(tm, D_MODEL), _rows(tm, ATTN_W), _rows(tm, ATTN_W), _rows(tm, ATTN_W),
                   _rows(tm, GMLP_W), _rows(tm, GMLP_W)],
        out_shape=[sds((T, D_MODEL), BF16), sds((T, ATTN_W), F32), sds((T, ATTN_W), F32),
                   sds((T, ATTN_W), F32), sds((T, GMLP_W), F32), sds((T, GMLP_W), F32)],
        compiler_params=_params(("parallel",)),
    )(x, g1, w_in_t)


ATT_TILE = 2048
ATT_BLOCKS = ATT_TILE // CHUNK


def _slope_table():
    row = np.repeat(SLOPES, HEAD_DIM)
    return jnp.asarray(np.broadcast_to(row[None], (8, ATTN_W)), F32)


def _stacked_consts(sl_ref):
    shape = (2 * CHUNK, 2 * CHUNK)
    row = lax.broadcasted_iota(jnp.int32, shape, 0)
    kj = lax.broadcasted_iota(jnp.int32, shape, 1)
    steps = (row & (CHUNK - 1)) + CHUNK - kj
    band = (steps >= 0) & (steps <= CHUNK)
    sl = sl_ref[0:1, :]
    upper = lax.broadcasted_iota(jnp.int32, (2 * CHUNK, 1), 0) < CHUNK
    slope2 = jnp.where(upper, sl[:, 0:1], sl[:, HEAD_DIM:HEAD_DIM + 1])
    return kj, band, slope2 * steps.astype(F32)


def _block_rows(j, d):
    if d == 1:
        start = j * CHUNK
        return j, start, pl.ds(start, CHUNK)
    r, b = j % d, j // d
    start = r + (d * CHUNK) * b
    return b, start, pl.ds(start, CHUNK, stride=d)


def _prev_rows(j, d):
    _, start, _ = _block_rows(j, d)
    if d == 1:
        return pl.ds(start - CHUNK, CHUNK), pl.ds(ATT_TILE - CHUNK, CHUNK)
    last = j % d + (d * CHUNK) * (ATT_BLOCKS // d - 1)
    return pl.ds(start - d * CHUNK, CHUNK, stride=d), pl.ds(last, CHUNK, stride=d)


def _kv_block(j, d, rows, kc_ref, kp_ref, vc_ref, vp_ref):
    b = j // d
    here, before = _prev_rows(j, d)
    if ATT_BLOCKS // d == 1:
        kp, vp = kp_ref[before, :], vp_ref[before, :]
    else:
        src_k, src_v, src_rows = (kp_ref, vp_ref, before) if b == 0 else (kc_ref, vc_ref, here)
        kp, vp = src_k[src_rows, :], src_v[src_rows, :]
    kcat = jnp.concatenate([kp, kc_ref[rows, :]], axis=0).astype(BF16)
    vcat = jnp.concatenate([vp, vc_ref[rows, :]], axis=0).astype(BF16)
    return kcat, vcat


def _stack_heads(xb, head0):
    zero = jnp.zeros_like(xb)
    return jnp.concatenate([jnp.where(head0, xb, zero), jnp.where(head0, zero, xb)], axis=0).astype(BF16)


def _unstack_heads(x2, head0):
    return jnp.where(head0, x2[:CHUNK, :], x2[CHUNK:, :])


def _scores(q2, kcat, bias, kj, first):
    s = _dot(q2, kcat, NT) + bias
    return jnp.where(kj < jnp.where(first, CHUNK, 0), NEG, s)


def _attn_tile_specs(nt, lag):
    clamp = (lambda t: jnp.minimum(t, nt - 1)) if lag else (lambda t: t)
    cur = pl.BlockSpec((ATT_TILE, LANES), lambda c, t: (clamp(t), c))
    prev = pl.BlockSpec((ATT_TILE, LANES), lambda c, t: (jnp.maximum(clamp(t) - 1, 0), c))
    slope = pl.BlockSpec((8, LANES), lambda c, t: (0, c))
    return cur, prev, slope


def _attn_fwd(q, k, v, shards=()):
    T = q.shape[0]
    nt = T // ATT_TILE
    ns = len(shards)
    steps = (ATTN_W // LANES) * nt

    def body(sl_ref, q_ref, kc_ref, kp_ref, vc_ref, vp_ref, *rest):
        x_refs, rest = rest[:ns], rest[ns:]
        attn_ref, lse_ref = rest[:2]
        g_refs, rest = rest[2:2 + ns], rest[2 + ns:]
        o_acc, l_acc = rest[:3], rest[3:6]
        t = pl.program_id(1)
        if ns:
            step = pl.program_id(0) * nt + t
            start, forward, finish = _gather_phases(x_refs, g_refs, *rest[6:])
            pl.when(step == 0)(start)
            pl.when(step == steps // 2)(forward)
        kj, band, base = _stacked_consts(sl_ref)
        head0 = lax.broadcasted_iota(jnp.int32, (CHUNK, LANES), 1) < HEAD_DIM
        for pi, d in enumerate(DILATIONS):
            bias = jnp.where(band, -(float(d) * base), NEG)

            def block(j, carry, d=d, pi=pi, bias=bias):
                b, _, rows = _block_rows(j, d)
                kcat, vcat = _kv_block(j, d, rows, kc_ref, kp_ref, vc_ref, vp_ref)
                q2 = _stack_heads(q_ref[rows, :], head0)
                s = _scores(q2, kcat, bias, kj, (t == 0) & (b == 0))
                m = jnp.max(s, axis=-1, keepdims=True)
                p = jnp.exp(s - m)
                l = jnp.sum(p, axis=-1, keepdims=True)
                o_acc[pi][rows, :] = _unstack_heads(_dot(p.astype(BF16), vcat) / l, head0)
                l_acc[pi][rows, :] = _unstack_heads(jnp.broadcast_to(m + jnp.log(l), (2 * CHUNK, LANES)), head0)
                return carry

            for j in range(ATT_BLOCKS):
                block(j, 0)

        for ci in range(ATT_TILE // 256):
            rows = slice(ci * 256, (ci + 1) * 256)
            a, b, c = l_acc[0][rows, :], l_acc[1][rows, :], l_acc[2][rows, :]
            m = jnp.maximum(jnp.maximum(a, b), c)
            ea, eb, ec = jnp.exp(a - m), jnp.exp(b - m), jnp.exp(c - m)
            tot = ea + eb + ec
            attn_ref[rows, :] = (ea * o_acc[0][rows, :] + eb * o_acc[1][rows, :] + ec * o_acc[2][rows, :]) / tot
            lse_ref[rows, :] = m + jnp.log(tot)

        if ns:
            pl.when(step == steps - 1)(finish)

    cur, prev, slope = _attn_tile_specs(nt, lag=False)
    outs = pl.pallas_call(
        body, name="attn_fwd", grid=(ATTN_W // LANES, nt),
        in_specs=[slope, cur, cur, prev, cur, prev] + [_HBM] * ns,
        out_specs=[cur, cur] + [_HBM] * ns,
        out_shape=[jax.ShapeDtypeStruct((T, ATTN_W), F32)] * 2 + [_gathered_shape(s) for s in shards],
        scratch_shapes=[pltpu.VMEM((ATT_TILE, LANES), F32)] * 6 + (_gather_sems(ns) if ns else []),
        compiler_params=_params(("arbitrary", "arbitrary")),
    )(_slope_table(), q, k, k, v, v, *shards)
    return outs[0], outs[1], tuple(outs[2:])


def _group_mean(v, grp):
    out = jnp.zeros_like(v)
    for g in range(N_GROUPS):
        mk = grp == g
        s = jnp.sum(jnp.where(mk, v, 0.0), axis=-1, keepdims=True) * (1.0 / HEAD_DIM)
        out = jnp.where(mk, s, out)
    return out


def _gmlp_core(uu, zz, lg, lb, ws, sb_ref, grp):
    ug, tu = _gelu(uu)
    zg, tz = _gelu(zz)
    zc = zg - _group_mean(zg, grp)
    rstd = lax.rsqrt(_group_mean(zc * zc, grp) + EPS)
    xhat = zc * rstd
    zn16 = (xhat * lg + lb).astype(BF16)
    mixed = jnp.zeros_like(uu)
    for g in range(N_GROUPS):
        mixed = jnp.where(grp == g, _dot(ws[g], zn16) + sb_ref[:, g:g + 1], mixed)
    return ug, tu, tz, xhat, rstd, zn16, mixed


def _causal_ws(w_ref):
    ti = lax.broadcasted_iota(jnp.int32, (CHUNK, CHUNK), 0)
    si = lax.broadcasted_iota(jnp.int32, (CHUNK, CHUNK), 1)
    causal = si <= ti
    return causal, [jnp.where(causal, w_ref[g], 0.0).astype(BF16) for g in range(N_GROUPS)]


def _gmlp_fwd(u, z, ln_g, ln_b, sgu_w, sgu_bt):
    T = u.shape[0]
    tg = 512

    def body(u_ref, z_ref, g_ref, b_ref, w_ref, sb_ref, out_ref):
        grp = lax.broadcasted_iota(jnp.int32, (CHUNK, GMLP_W), 1) // HEAD_DIM
        _, ws = _causal_ws(w_ref)
        for ci in range(tg // CHUNK):
            rows = slice(ci * CHUNK, (ci + 1) * CHUNK)
            ug, _, _, _, _, _, mixed = _gmlp_core(u_ref[rows, :], z_ref[rows, :], g_ref[...], b_ref[...],
                                                  ws, sb_ref, grp)
            out_ref[rows, :] = ug * mixed

    return pl.pallas_call(
        body, name="gmlp_fwd", grid=(T // tg,),
        in_specs=[_rows(tg, GMLP_W), _rows(tg, GMLP_W), _resident((1, GMLP_W)), _resident((1, GMLP_W)),
                  _resident((N_GROUPS, CHUNK, CHUNK)), _resident((CHUNK, N_GROUPS))],
        out_specs=_rows(tg, GMLP_W),
        out_shape=jax.ShapeDtypeStruct((T, GMLP_W), F32),
        compiler_params=_params(("parallel",)),
    )(u, z, ln_g, ln_b, sgu_w, sgu_bt)


def _out_fwd(attn, gm, ga, gg, w_out, x, g2):
    T = x.shape[0]
    tm = TM_PROJ

    def body(a_ref, m_ref, ga_ref, gg_ref, w_ref, x_ref, g2_ref, mix_ref, h1_ref, hn2_ref):
        an, _ = _rms(a_ref[...])
        gn, _ = _rms(m_ref[...])
        an = (an * ga_ref[...]).astype(BF16)
        gn = (gn * gg_ref[...]).astype(BF16)
        mix_ref[:, 0:ATTN_W] = an
        mix_ref[:, ATTN_W:] = gn
        h1 = x_ref[...] + _dot(an, w_ref[0:ATTN_W, :]) + _dot(gn, w_ref[ATTN_W:, :])
        h1_ref[...] = h1
        n2, _ = _rms(h1)
        hn2_ref[...] = (n2 * g2_ref[...]).astype(BF16)

    sds = jax.ShapeDtypeStruct
    return pl.pallas_call(
        body, name="out_fwd", grid=(T // tm,),
        in_specs=[_rows(tm, ATTN_W), _rows(tm, GMLP_W), _resident((1, ATTN_W)), _resident((1, GMLP_W)),
                  _resident((D_MODEL, D_MODEL)), _rows(tm, D_MODEL), _resident((1, D_MODEL))],
        out_specs=[_rows(tm, D_MODEL)] * 3,
        out_shape=[sds((T, D_MODEL), BF16), sds((T, D_MODEL), F32), sds((T, D_MODEL), BF16)],
        compiler_params=_params(("parallel",)),
    )(attn, gm, ga, gg, w_out, x, g2)


def _ffn_fwd(hn2, h1, w1t, w2, gf, tgt):
    T = h1.shape[0]
    tm = TM_FFN

    def body(hn_ref, h1_ref, w1_ref, w2_ref, gf_ref, t_ref, r_ref, dhf_ref, dhb_ref, loss_ref, dgf_ref):
        i = pl.program_id(0)

        @pl.when(i == 0)
        def _():
            loss_ref[...] = jnp.zeros_like(loss_ref)
            dgf_ref[...] = jnp.zeros_like(dgf_ref)

        hn = hn_ref[...]
        acc = h1_ref[...]
        for j in range(D_FF // FF_CHUNK):
            cols = slice(j * FF_CHUNK, (j + 1) * FF_CHUNK)
            r = jnp.maximum(_dot(hn, w1_ref[cols, :], NT), 0.0)
            r_ref[:, cols] = r.astype(BF16)
            act = jnp.square(r).astype(BF16)
            acc = acc + _dot(act, w2_ref[cols, :])
        n3, r3 = _rms(acc)
        gf_row = gf_ref[...]
        e = n3 * gf_row - t_ref[...]
        loss_ref[...] += 0.5 * jnp.sum(jnp.mean(e * e, axis=-1, keepdims=True))
        dy = e * (1.0 / D_MODEL)
        _accum_rows(dgf_ref, dy * n3)
        dh2 = _rms_bwd(n3, r3, gf_row, dy)
        dhf_ref[...] = dh2
        dhb_ref[...] = dh2.astype(BF16)

    sds = jax.ShapeDtypeStruct
    acc_spec = lambda n: pl.BlockSpec((8, n), lambda i: (0, 0))
    return pl.pallas_call(
        body, name="ffn_fwd", grid=(T // tm,),
        in_specs=[_rows(tm, D_MODEL), _rows(tm, D_MODEL), _resident((D_FF, D_MODEL)), _resident((D_FF, D_MODEL)),
                  _resident((1, D_MODEL)), _rows(tm, D_MODEL)],
        out_specs=[_rows(tm, D_FF), _rows(tm, D_MODEL), _rows(tm, D_MODEL), acc_spec(LANES), acc_spec(D_MODEL)],
        out_shape=[sds((T, D_FF), BF16), sds((T, D_MODEL), F32), sds((T, D_MODEL), BF16),
                   sds((8, LANES), F32), sds((8, D_MODEL), F32)],
        compiler_params=_params(("arbitrary",)),
    )(hn2, h1, w1t, w2, gf, tgt)


def _ffn_bwd(dh2b, dh2f, relu, h1, g2, w2, w1t):
    T = h1.shape[0]
    tm = TM_FFN

    def body(db_ref, df_ref, r_ref, h1_ref, g2_ref, w2_ref, w1t_ref, da_ref, d1f_ref, d1b_ref, dg_ref):
        @pl.when(pl.program_id(0) == 0)
        def _():
            dg_ref[...] = jnp.zeros_like(dg_ref)

        db = db_ref[...]
        acc = jnp.zeros((tm, D_MODEL), F32)
        for j in range(D_FF // FF_CHUNK):
            cols = slice(j * FF_CHUNK, (j + 1) * FF_CHUNK)
            da = (_dot(db, w2_ref[cols, :], NT) * (2.0 * r_ref[:, cols].astype(F32))).astype(BF16)
            da_ref[:, cols] = da
            acc = acc + _dot(da, w1t_ref[cols, :])
        n2, r2 = _rms(h1_ref[...])
        _accum_rows(dg_ref, acc * n2)
        dh1 = df_ref[...] + _rms_bwd(n2, r2, g2_ref[...], acc)
        d1f_ref[...] = dh1
        d1b_ref[...] = dh1.astype(BF16)

    sds = jax.ShapeDtypeStruct
    return pl.pallas_call(
        body, name="ffn_bwd", grid=(T // tm,),
        in_specs=[_rows(tm, D_MODEL), _rows(tm, D_MODEL), _rows(tm, D_FF), _rows(tm, D_MODEL),
                  _resident((1, D_MODEL)), _resident((D_FF, D_MODEL)), _resident((D_FF, D_MODEL))],
        out_specs=[_rows(tm, D_FF), _rows(tm, D_MODEL), _rows(tm, D_MODEL),
                   pl.BlockSpec((8, D_MODEL), lambda i: (0, 0))],
        out_shape=[sds((T, D_FF), BF16), sds((T, D_MODEL), F32), sds((T, D_MODEL), BF16), sds((8, D_MODEL), F32)],
        compiler_params=_params(("arbitrary",)),
    )(dh2b, dh2f, relu, h1, g2, w2, w1t)


def _out_bwd(dh1b, w_out, attn, gm, ga, gg):
    T = attn.shape[0]
    tm = TM_PROJ

    def body(d_ref, w_ref, a_ref, m_ref, ga_ref, gg_ref, da_ref, dm_ref, dga_ref, dgg_ref):
        @pl.when(pl.program_id(0) == 0)
        def _():
            dga_ref[...] = jnp.zeros_like(dga_ref)
            dgg_ref[...] = jnp.zeros_like(dgg_ref)

        d = d_ref[...]
        dan = _dot(d, w_ref[0:ATTN_W, :], NT)
        dgn = _dot(d, w_ref[ATTN_W:, :], NT)
        na, ra = _rms(a_ref[...])
        ng, rg = _rms(m_ref[...])
        _accum_rows(dga_ref, dan * na)
        _accum_rows(dgg_ref, dgn * ng)
        da_ref[...] = _rms_bwd(na, ra, ga_ref[...], dan)
        dm_ref[...] = _rms_bwd(ng, rg, gg_ref[...], dgn)

    sds = jax.ShapeDtypeStruct
    return pl.pallas_call(
        body, name="out_bwd", grid=(T // tm,),
        in_specs=[_rows(tm, D_MODEL), _resident((D_MODEL, D_MODEL)), _rows(tm, ATTN_W), _rows(tm, GMLP_W),
                  _resident((1, ATTN_W)), _resident((1, GMLP_W))],
        out_specs=[_rows(tm, ATTN_W), _rows(tm, GMLP_W), pl.BlockSpec((8, ATTN_W), lambda i: (0, 0)),
                   pl.BlockSpec((8, GMLP_W), lambda i: (0, 0))],
        out_shape=[sds((T, ATTN_W), F32), sds((T, GMLP_W), F32), sds((8, ATTN_W), F32), sds((8, GMLP_W), F32)],
        compiler_params=_params(("arbitrary",)),
    )(dh1b, w_out, attn, gm, ga, gg)


def _gmlp_bwd(u, z, dgm, ln_g, ln_b, sgu_w, sgu_bt):
    T = u.shape[0]
    tg = 512
    nsteps = T // tg

    def body(u_ref, z_ref, d_ref, g_ref, b_ref, w_ref, sb_ref, du_ref, dz_ref, dlg_ref, dlb_ref, dw_ref, dsb_ref):
        i = pl.program_id(0)

        @pl.when(i == 0)
        def _():
            for ref in (dlg_ref, dlb_ref, dw_ref, dsb_ref):
                ref[...] = jnp.zeros_like(ref)

        grp = lax.broadcasted_iota(jnp.int32, (CHUNK, GMLP_W), 1) // HEAD_DIM
        lane = lax.broadcasted_iota(jnp.int32, (CHUNK, LANES), 1)
        causal, ws = _causal_ws(w_ref)
        lg = g_ref[...]
        for ci in range(tg // CHUNK):
            rows = slice(ci * CHUNK, (ci + 1) * CHUNK)
            uu, zz = u_ref[rows, :], z_ref[rows, :]
            ug, tu, tz, xhat, rstd, zn16, mixed = _gmlp_core(uu, zz, lg, b_ref[...], ws, sb_ref, grp)
            dgm_c = d_ref[rows, :]
            dmx = dgm_c * ug
            du_ref[rows, :] = dgm_c * mixed * _gelu_grad(uu, tu)
            dmx16 = dmx.astype(BF16)
            dzn = jnp.zeros_like(dmx)
            dsb = jnp.zeros((CHUNK, LANES), F32)
            for g in range(N_GROUPS):
                mk = grp == g
                dzn = jnp.where(mk, _dot(ws[g], dmx16, TN), dzn)
                dw_ref[g] += _dot(jnp.where(mk, dmx16, jnp.zeros_like(dmx16)), zn16, NT)
                dsb = jnp.where(lane == g, jnp.sum(jnp.where(mk, dmx, 0.0), axis=-1, keepdims=True), dsb)
            dsb_ref[...] += dsb
            _accum_rows(dlg_ref, dzn * xhat)
            _accum_rows(dlb_ref, dzn)
            dxh = dzn * lg
            dzg = rstd * (dxh - _group_mean(dxh, grp) - xhat * _group_mean(dxh * xhat, grp))
            dz_ref[rows, :] = dzg * _gelu_grad(zz, tz)

        @pl.when(i == nsteps - 1)
        def _():
            for g in range(N_GROUPS):
                dw_ref[g] = jnp.where(causal, dw_ref[g], 0.0)

    sds = jax.ShapeDtypeStruct
    return pl.pallas_call(
        body, name="gmlp_bwd", grid=(nsteps,),
        in_specs=[_rows(tg, GMLP_W)] * 3 + [_resident((1, GMLP_W)), _resident((1, GMLP_W)),
                                              _resident((N_GROUPS, CHUNK, CHUNK)), _resident((CHUNK, N_GROUPS))],
        out_specs=[_rows(tg, GMLP_W), _rows(tg, GMLP_W), pl.BlockSpec((8, GMLP_W), lambda i: (0, 0)),
                   pl.BlockSpec((8, GMLP_W), lambda i: (0, 0)),
                   pl.BlockSpec((N_GROUPS, CHUNK, CHUNK), lambda i: (0, 0, 0)),
                   pl.BlockSpec((CHUNK, LANES), lambda i: (0, 0))],
        out_shape=[sds((T, GMLP_W), F32), sds((T, GMLP_W), F32), sds((8, GMLP_W), F32), sds((8, GMLP_W), F32),
                   sds((N_GROUPS, CHUNK, CHUNK), F32), sds((CHUNK, LANES), F32)],
        compiler_params=_params(("arbitrary",)),
    )(u, z, dgm, ln_g, ln_b, sgu_w, sgu_bt)


def _attn_bwd(q, k, v, dattn, attn, lse, owner_grads=()):
    T = q.shape[0]
    nt = T // ATT_TILE
    ns = len(owner_grads)
    steps = (ATTN_W // LANES) * (nt + 1)

    def body(sl_ref, q_ref, kc_ref, kp_ref, vc_ref, vp_ref, do_ref, o_ref, lse_ref, *rest):
        p_refs, rest = rest[:ns], rest[ns:]
        dq_ref, dk_ref, dv_ref = rest[:3]
        r_refs, rest = rest[3:3 + ns], rest[3 + ns:]
        dk_acc, dv_acc, delta_s = rest[:3]
        t = pl.program_id(1)
        if ns:
            step = pl.program_id(0) * (nt + 1) + t
            start, finish = _owner_exchange_phases(p_refs, r_refs, *rest[3:])
            pl.when(step == 0)(start)
        s_cur = t % 2
        s_prev = 1 - s_cur

        @pl.when(t == 0)
        def _():
            dk_acc[...] = jnp.zeros_like(dk_acc)
            dv_acc[...] = jnp.zeros_like(dv_acc)

        @pl.when(t < nt)
        def _():
            kj, band, base = _stacked_consts(sl_ref)
            head0 = lax.broadcasted_iota(jnp.int32, (CHUNK, LANES), 1) < HEAD_DIM
            for ci in range(ATT_TILE // 256):
                rows = slice(ci * 256, (ci + 1) * 256)
                h0 = lax.broadcasted_iota(jnp.int32, (256, LANES), 1) < HEAD_DIM
                dd = do_ref[rows, :] * o_ref[rows, :]
                d0 = jnp.sum(jnp.where(h0, dd, 0.0), axis=-1, keepdims=True)
                d1 = jnp.sum(jnp.where(h0, 0.0, dd), axis=-1, keepdims=True)
                delta_s[rows, :] = jnp.where(h0, d0, d1)
            dq_ref[...] = jnp.zeros_like(dq_ref)

            def column(xb):
                return jnp.concatenate([xb[:, 0:1], xb[:, HEAD_DIM:HEAD_DIM + 1]], axis=0)

            for d in DILATIONS:
                bias = jnp.where(band, -(float(d) * base), NEG)

                def block(j, carry, d=d, bias=bias):
                    b, _, rows = _block_rows(j, d)
                    kcat, vcat = _kv_block(j, d, rows, kc_ref, kp_ref, vc_ref, vp_ref)
                    q2 = _stack_heads(q_ref[rows, :], head0)
                    do2 = _stack_heads(do_ref[rows, :], head0)
                    s = _scores(q2, kcat, bias, kj, (t == 0) & (b == 0))
                    p = jnp.exp(s - column(lse_ref[rows, :]))
                    ds = (p * (_dot(do2, vcat, NT) - column(delta_s[rows, :]))).astype(BF16)
                    dq_ref[rows, :] += _unstack_heads(_dot(ds, kcat), head0)
                    ck = _dot(ds, q2, TN)
                    cv = _dot(p.astype(BF16), do2, TN)
                    dk_acc[s_cur, rows, :] += ck[CHUNK:, :]
                    dv_acc[s_cur, rows, :] += cv[CHUNK:, :]
                    here, before = _prev_rows(j, d)
                    if ATT_BLOCKS // d == 1:
                        dk_acc[s_prev, before, :] += ck[:CHUNK, :]
                        dv_acc[s_prev, before, :] += cv[:CHUNK, :]
                    else:
                        slot, dst = (s_prev, before) if b == 0 else (s_cur, here)
                        dk_acc[slot, dst, :] += ck[:CHUNK, :]
                        dv_acc[slot, dst, :] += cv[:CHUNK, :]
                    return carry

                for j in range(ATT_BLOCKS):
                    block(j, 0)

        dk_ref[...] = dk_acc[s_prev]
        dv_ref[...] = dv_acc[s_prev]
        dk_acc[s_prev] = jnp.zeros((ATT_TILE, LANES), F32)
        dv_acc[s_prev] = jnp.zeros((ATT_TILE, LANES), F32)

        if ns:
            pl.when(step == steps - 1)(finish)

    cur, prev, slope = _attn_tile_specs(nt, lag=True)
    late = pl.BlockSpec((ATT_TILE, LANES), lambda c, t: (jnp.maximum(t - 1, 0), c))
    sds = jax.ShapeDtypeStruct((T, ATTN_W), F32)
    outs = pl.pallas_call(
        body, name="attn_bwd", grid=(ATTN_W // LANES, nt + 1),
        in_specs=[slope, cur, cur, prev, cur, prev, cur, cur, cur] + [_HBM] * ns,
        out_specs=[cur, late, late] + [_HBM] * ns,
        out_shape=[sds, sds, sds] + [jax.ShapeDtypeStruct(p.shape, p.dtype) for p in owner_grads],
        scratch_shapes=[pltpu.VMEM((2, ATT_TILE, LANES), F32), pltpu.VMEM((2, ATT_TILE, LANES), F32),
                        pltpu.VMEM((ATT_TILE, LANES), F32)] + (_owner_exchange_sems(ns) if ns else []),
        compiler_params=_params(("arbitrary", "arbitrary")),
    )(_slope_table(), q, k, k, v, v, dattn, attn, lse, *owner_grads)
    return outs[0], outs[1], outs[2], tuple(outs[3:])


def _dproj_assemble(dq, dk, dv, du, dz):
    T = du.shape[0]
    tm = 512

    def body(q_ref, k_ref, v_ref, u_ref, z_ref, out_ref):
        a = ATTN_W
        out_ref[:, 0:a] = (q_ref[...] * Q_SCALE).astype(BF16)
        out_ref[:, a:2 * a] = k_ref[...].astype(BF16)
        out_ref[:, 2 * a:3 * a] = v_ref[...].astype(BF16)
        out_ref[:, 3 * a:3 * a + GMLP_W] = u_ref[...].astype(BF16)
        out_ref[:, 3 * a + GMLP_W:] = z_ref[...].astype(BF16)

    return pl.pallas_call(
        body, name="dproj_assemble", grid=(T // tm,),
        in_specs=[_rows(tm, ATTN_W)] * 3 + [_rows(tm, GMLP_W)] * 2,
        out_specs=_rows(tm, IN_W),
        out_shape=jax.ShapeDtypeStruct((T, IN_W), BF16),
        compiler_params=_params(("parallel",)),
    )(dq, dk, dv, du, dz)


def _proj_bwd(dproj, w_in_t, x, g1, dh1, chip_sums=()):
    T = x.shape[0]
    tm = TM_PROJ
    ns = len(chip_sums)
    steps = T // tm

    def body(d_ref, w_ref, x_ref, g_ref, r_ref, *rest):
        p_refs, rest = rest[:ns], rest[ns:]
        dx_ref, dg_ref = rest[:2]
        r_refs, sems = rest[2:2 + ns], rest[2 + ns:]
        step = pl.program_id(0)
        if ns:
            start, finish = _chip_exchange_phases(p_refs, r_refs, *sems)
            pl.when(step == 0)(start)

        @pl.when(step == 0)
        def _():
            dg_ref[...] = jnp.zeros_like(dg_ref)

        dhn = _dot(d_ref[...], w_ref[...])
        n1, r1 = _rms(x_ref[...])
        _accum_rows(dg_ref, dhn * n1)
        dx_ref[...] = r_ref[...] + _rms_bwd(n1, r1, g_ref[...], dhn)
        if ns:
            pl.when(step == steps - 1)(finish)

    outs = pl.pallas_call(
        body, name="proj_bwd", grid=(steps,),
        in_specs=[_rows(tm, IN_W), _resident((IN_W, D_MODEL)), _rows(tm, D_MODEL), _resident((1, D_MODEL)),
                  _rows(tm, D_MODEL)] + [_HBM] * ns,
        out_specs=[_rows(tm, D_MODEL), pl.BlockSpec((8, D_MODEL), lambda i: (0, 0))] + [_HBM] * ns,
        out_shape=[jax.ShapeDtypeStruct((T, D_MODEL), F32), jax.ShapeDtypeStruct((8, D_MODEL), F32)]
        + [jax.ShapeDtypeStruct(p.shape, p.dtype) for p in chip_sums],
        scratch_shapes=_chip_exchange_sems(ns) if ns else [],
        compiler_params=_params(("arbitrary",)),
    )(dproj, w_in_t, x, g1, dh1, *chip_sums)
    return outs[0], outs[1], tuple(outs[2:])


def _dw(a, b, name, tile, square_a=False, out_dtype=F32):
    T, ka = a.shape
    nb = b.shape[1]
    tka, tnb, tt = tile
    tt = min(tt, T)
    last = T // tt - 1

    def body(a_ref, b_ref, o_ref, *scratch):
        acc_ref = scratch[0] if scratch else o_ref
        s = pl.program_id(2)

        @pl.when(s == 0)
        def _():
            acc_ref[...] = jnp.zeros_like(acc_ref)

        a_tile = a_ref[...]
        if square_a:
            a_tile = jnp.square(a_tile.astype(F32)).astype(BF16)
        acc_ref[...] += _dot(a_tile, b_ref[...], TN)
        if scratch:
            @pl.when(s == last)
            def _():
                o_ref[...] = acc_ref[...].astype(out_dtype)

    return pl.pallas_call(
        body, name=name, grid=(ka // tka, nb // tnb, T // tt),
        in_specs=[pl.BlockSpec((tt, tka), lambda i, j, s: (s, i)), pl.BlockSpec((tt, tnb), lambda i, j, s: (s, j))],
        out_specs=pl.BlockSpec((tka, tnb), lambda i, j, s: (i, j)),
        out_shape=jax.ShapeDtypeStruct((ka, nb), out_dtype),
        scratch_shapes=[] if out_dtype == F32 else [pltpu.VMEM((tka, tnb), F32)],
        compiler_params=_params(("parallel", "parallel", "arbitrary")),
    )(a, b)


def _adamw(w, m, v, parts, name, tr, transposed=False):
    R, C = w.shape
    P = parts.shape[0]

    def body(w_ref, m_ref, v_ref, p_ref, g_ref, d_ref, m2_ref, v2_ref):
        g = p_ref[0].astype(F32)
        for i in range(1, P):
            g = g + p_ref[i].astype(F32)
        if transposed:
            g = g.T
        m2 = ADAM_B1 * m_ref[...] + (1.0 - ADAM_B1) * g
        v2 = ADAM_B2 * v_ref[...] + (1.0 - ADAM_B2) * jnp.square(g)
        m_hat = m2 / (1.0 - ADAM_B1 ** ADAM_STEP)
        v_hat = v2 / (1.0 - ADAM_B2 ** ADAM_STEP)
        g_ref[...] = g
        d_ref[...] = -ADAM_LR * (m_hat / (jnp.sqrt(v_hat) + ADAM_EPS) + ADAM_WD * w_ref[...])
        m2_ref[...] = m2
        v2_ref[...] = v2

    spec = _rows(tr, C)
    part_spec = (pl.BlockSpec((P, C, tr), lambda i: (0, 0, i)) if transposed
                 else pl.BlockSpec((P, tr, C), lambda i: (0, i, 0)))
    return pl.pallas_call(
        body, name=name, grid=(R // tr,),
        in_specs=[spec, spec, spec, part_spec],
        out_specs=[spec] * 4,
        out_shape=[jax.ShapeDtypeStruct((R, C), F32)] * 4,
        compiler_params=_params(("parallel",)),
    )(w, m, v, parts)


def _pair_sum(core, grad, recv, name):
    _, _, n, C = grad.shape
    tr = n // 2

    def body(c_ref, a_ref, b_ref, o_ref):
        o_ref[...] = a_ref[...] + b_ref[...]

    spec = pl.BlockSpec((1, tr, C), lambda i, j, c_ref: (i, j, 0))
    return pl.pallas_call(
        body, name=name,
        grid_spec=pltpu.PrefetchScalarGridSpec(
            num_scalar_prefetch=1, grid=(4, n // tr),
            in_specs=[pl.BlockSpec((1, None, tr, C), lambda i, j, c_ref: (i, c_ref[0], j, 0)), spec],
            out_specs=spec),
        out_shape=jax.ShapeDtypeStruct(recv.shape, F32),
        compiler_params=_params(("parallel", "parallel")),
    )(core.reshape(1), grad, recv)


_HBM = pl.BlockSpec(memory_space=pltpu.HBM)


def _place():
    return lax.axis_index("x"), lax.axis_index("y"), lax.axis_index("c")


def _gathered_shape(shard):
    return jax.ShapeDtypeStruct((N_DEV,) + shard.shape, shard.dtype)


def _gather_sems(n):
    return [pltpu.SemaphoreType.DMA((7, n)), pltpu.SemaphoreType.DMA((7, n)), pltpu.SemaphoreType.DMA((n,))]


def _gather_phases(x_refs, out_refs, send_sems, recv_sems, local_sems):
    x, y, c = _place()
    me, sibling = (x, y, c), (x, y, 1 - c)
    chips = [(1 - x, y), (x, 1 - y), (1 - x, 1 - y)]
    arrays = range(len(x_refs))

    def slot(i, px, py, pc):
        return out_refs[i].at[4 * px + 2 * py + pc]

    def copy(i, k, block, to, own=False):
        return pltpu.make_async_remote_copy(
            src_ref=x_refs[i] if own else slot(i, *block), dst_ref=slot(i, *block),
            send_sem=send_sems.at[k, i], recv_sem=recv_sems.at[k, i], device_id=to, device_id_type=MESH)

    def mine(i):
        return pltpu.make_async_copy(x_refs[i], slot(i, *me), local_sems.at[i])

    def start():
        for i in arrays:
            mine(i).start()
            copy(i, 0, me, sibling, own=True).start()
            for j, chip in enumerate(chips):
                copy(i, 1 + j, me, (*chip, c), own=True).start()

    def forward():
        for i in arrays:
            for j, chip in enumerate(chips):
                copy(i, 1 + j, (*chip, c), me).wait_recv()
                copy(i, 4 + j, (*chip, c), sibling).start()

    def finish():
        for i in arrays:
            copy(i, 0, sibling, me).wait_recv()
            copy(i, 0, me, sibling, own=True).wait_send()
            for j, chip in enumerate(chips):
                copy(i, 4 + j, (*chip, 1 - c), me).wait_recv()
                copy(i, 1 + j, me, (*chip, c), own=True).wait_send()
                copy(i, 4 + j, (*chip, c), sibling).wait_send()
            mine(i).wait()

    return start, forward, finish


def _all_gather(shards, name):
    n = len(shards)

    def body(*refs):
        start, forward, finish = _gather_phases(refs[:n], refs[n:2 * n], *refs[2 * n:])
        start()
        forward()
        finish()

    return pl.pallas_call(
        body, name=name,
        out_shape=[_gathered_shape(s) for s in shards],
        in_specs=[_HBM] * n, out_specs=[_HBM] * n,
        scratch_shapes=_gather_sems(n),
    )(*shards)


def _sibling_exchange(grads, name):
    n = len(grads)

    def body(*refs):
        g_refs, r_refs, send_sems, recv_sems = refs[:n], refs[n:2 * n], refs[2 * n], refs[2 * n + 1]
        x, y, c = _place()
        copies = [pltpu.make_async_remote_copy(
            src_ref=g_refs[i].at[:, 1 - c], dst_ref=r_refs[i], send_sem=send_sems.at[i], recv_sem=recv_sems.at[i],
            device_id=(x, y, 1 - c), device_id_type=MESH) for i in range(n)]
        for cp in copies:
            cp.start()
        for cp in copies:
            cp.wait()

    return pl.pallas_call(
        body, name=name,
        out_shape=[jax.ShapeDtypeStruct((g.shape[0],) + g.shape[2:], g.dtype) for g in grads],
        in_specs=[_HBM] * n, out_specs=[_HBM] * n,
        scratch_shapes=[pltpu.SemaphoreType.DMA((n,)), pltpu.SemaphoreType.DMA((n,))],
    )(*grads)


def _owner_exchange_sems(n):
    return [pltpu.SemaphoreType.DMA((7, n)), pltpu.SemaphoreType.DMA((7, n)), pltpu.SemaphoreType.DMA((n,))]


def _owner_exchange_phases(g_refs, r_refs, send_sems, recv_sems, local_sems):
    x, y, c = _place()
    me = 4 * x + 2 * y + c
    flip = lambda v, bit: 1 - v if bit else v
    peers = [(flip(x, k & 4), flip(y, k & 2), flip(c, k & 1)) for k in range(1, N_DEV)]
    arrays = range(len(g_refs))

    def mine(i):
        return pltpu.make_async_copy(g_refs[i].at[me], r_refs[i].at[me], local_sems.at[i])

    def copy(i, k, src_slot, dst_slot):
        return pltpu.make_async_remote_copy(
            src_ref=g_refs[i].at[src_slot], dst_ref=r_refs[i].at[dst_slot],
            send_sem=send_sems.at[k, i], recv_sem=recv_sems.at[k, i], device_id=peers[k], device_id_type=MESH)

    def start():
        for i in arrays:
            mine(i).start()
            for k, (px, py, pc) in enumerate(peers):
                copy(i, k, 4 * px + 2 * py + pc, me).start()

    def finish():
        for i in arrays:
            for k, (px, py, pc) in enumerate(peers):
                copy(i, k, me, 4 * px + 2 * py + pc).wait_recv()
                copy(i, k, 4 * px + 2 * py + pc, me).wait_send()
            mine(i).wait()

    return start, finish


def _chip_exchange_sems(n):
    return [pltpu.SemaphoreType.DMA((3, n)), pltpu.SemaphoreType.DMA((3, n)), pltpu.SemaphoreType.DMA((n,))]


def _chip_exchange_phases(p_refs, r_refs, send_sems, recv_sems, local_sems):
    x, y, c = _place()
    my_chip = 2 * x + y
    chips = [(1 - x, y), (x, 1 - y), (1 - x, 1 - y)]
    arrays = range(len(p_refs))

    def mine(i):
        return pltpu.make_async_copy(p_refs[i].at[my_chip], r_refs[i].at[my_chip], local_sems.at[i])

    def copy(i, k, src_chip, dst_chip):
        px, py = chips[k]
        return pltpu.make_async_remote_copy(
            src_ref=p_refs[i].at[src_chip], dst_ref=r_refs[i].at[dst_chip],
            send_sem=send_sems.at[k, i], recv_sem=recv_sems.at[k, i], device_id=(px, py, c), device_id_type=MESH)

    def start():
        for i in arrays:
            mine(i).start()
            for k, (px, py) in enumerate(chips):
                copy(i, k, 2 * px + py, my_chip).start()

    def finish():
        for i in arrays:
            for k, (px, py) in enumerate(chips):
                copy(i, k, my_chip, 2 * px + py).wait_recv()
                copy(i, k, 2 * px + py, my_chip).wait_send()
            mine(i).wait()

    return start, finish


_R_IN, _R_OUT, _R_FF = IN_W // N_DEV, D_MODEL // N_DEV, D_FF // N_DEV


def _by_owner(g):
    return g.reshape(4, 2, g.shape[0] // N_DEV, D_MODEL)


def _local_step(x, tgt, small, w_in_t, rest, core=None):
    exchange = core is not None
    g1, g2, gf = small["norm1_g"], small["norm2_g"], small["final_norm_g"].reshape(1, D_MODEL)
    ga, gg = small["attn_out_g"], small["gmlp_out_g"]
    ln_g = small["sgu_ln_g"].reshape(1, GMLP_W)
    ln_b = small["sgu_ln_b"].reshape(1, GMLP_W)
    sgu_w = small["sgu_w"][0]
    sgu_bt = small["sgu_b"][0].T

    hn1, q, k, v, u, z = _proj_fwd(x, g1, w_in_t)
    attn, lse, gathered = _attn_fwd(q, k, v, shards=rest if exchange else ())
    w_out, w_ff1_t, w_ff2 = [g.reshape(-1, D_MODEL) for g in gathered] if exchange else rest
    gm = _gmlp_fwd(u, z, ln_g, ln_b, sgu_w, sgu_bt)
    mixed, h1, hn2 = _out_fwd(attn, gm, ga, gg, w_out, x, g2)
    relu, dh2f, dh2b, loss8, dgf8 = _ffn_fwd(hn2, h1, w_ff1_t, w_ff2, gf, tgt)

    da, dh1f, dh1b, dg2 = _ffn_bwd(dh2b, dh2f, relu, h1, g2, w_ff2, w_ff1_t)
    wire = BF16 if exchange else F32
    dw_ff2 = _dw(relu, dh2b, "dw_ff2", DW_TILE, square_a=True, out_dtype=wire)
    dw_ff1_t = _dw(da, hn2, "dw_ff1", DW_TILE, out_dtype=wire)
    dattn, dgm, dga, dgg = _out_bwd(dh1b, w_out, attn, gm, ga, gg)
    dw_out = _dw(mixed, dh1b, "dw_out", DW_TILE, out_dtype=wire)
    early = [dw_out, dw_ff1_t, dw_ff2]
    if exchange:
        early = [g.reshape(N_DEV, -1, D_MODEL) for g in early]
    du, dz, dlg, dlb, dsw, dsb = _gmlp_bwd(u, z, dgm, ln_g, ln_b, sgu_w, sgu_bt)
    dq, dk, dv, arrived = _attn_bwd(q, k, v, dattn, attn, lse, owner_grads=early if exchange else ())
    dproj = _dproj_assemble(dq, dk, dv, du, dz)
    dw_in_t = _dw(dproj, hn1, "dw_in", DW_TILE_IN)
    late = ()
    if exchange:
        by_owner = _by_owner(dw_in_t)
        got, = _sibling_exchange([by_owner], "grad_sibling_exchange")
        late = (_pair_sum(core, by_owner, got, "grad_pair_sum"),)
    dx, dg1, late = _proj_bwd(dproj, w_in_t, x, g1, dh1f, chip_sums=late)
    if exchange:
        dw_in_t, early = late[0], arrived

    small_grads = dict(
        norm1_g=dg1[0], sgu_ln_g=dlg[0], sgu_ln_b=dlb[0], sgu_w=dsw, sgu_b=dsb[:, :N_GROUPS].T,
        attn_out_g=dga[0], gmlp_out_g=dgg[0], norm2_g=dg2[0], final_norm_g=dgf8[0])
    return loss8[0, 0], dx, (dw_in_t, *early), small_grads


SMALL_NAMES = ("norm1_g", "sgu_ln_g", "sgu_ln_b", "sgu_w", "sgu_b", "attn_out_g", "gmlp_out_g", "norm2_g",
               "final_norm_g")
WEIGHT_ORDER = ("norm1_g", "w_in", "sgu_ln_g", "sgu_ln_b", "sgu_w", "sgu_b", "attn_out_g", "gmlp_out_g", "w_out",
                "norm2_g", "w_ff1", "w_ff2", "final_norm_g")


def _pack_small(d):
    return jnp.concatenate([d[n].reshape(-1, LANES) for n in SMALL_NAMES], axis=0)


def _unpack_small(p, like):
    out, r = {}, 0
    for n in SMALL_NAMES:
        rows = like[n].size // LANES
        out[n] = p[r:r + rows].reshape(like[n].shape)
        r += rows
    return out


def kernel(x, norm1_g, w_in, sgu_ln_g, sgu_ln_b, sgu_w, sgu_b, attn_out_g, gmlp_out_g, w_out, norm2_g, w_ff1, w_ff2, final_norm_g, loss_target, m_norm1_g, m_w_in, m_sgu_ln_g, m_sgu_ln_b, m_sgu_w, m_sgu_b, m_attn_out_g, m_gmlp_out_g, m_w_out, m_norm2_g, m_w_ff1, m_w_ff2, m_final_norm_g, v_norm1_g, v_w_in, v_sgu_ln_g, v_sgu_ln_b, v_sgu_w, v_sgu_b, v_attn_out_g, v_gmlp_out_g, v_w_out, v_norm2_g, v_w_ff1, v_w_ff2, v_final_norm_g):
    w = dict(norm1_g=norm1_g, w_in=w_in, sgu_ln_g=sgu_ln_g, sgu_ln_b=sgu_ln_b, sgu_w=sgu_w, sgu_b=sgu_b,
             attn_out_g=attn_out_g, gmlp_out_g=gmlp_out_g, w_out=w_out, norm2_g=norm2_g, w_ff1=w_ff1, w_ff2=w_ff2,
             final_norm_g=final_norm_g)
    m = dict(norm1_g=m_norm1_g, w_in=m_w_in, sgu_ln_g=m_sgu_ln_g, sgu_ln_b=m_sgu_ln_b, sgu_w=m_sgu_w, sgu_b=m_sgu_b,
             attn_out_g=m_attn_out_g, gmlp_out_g=m_gmlp_out_g, w_out=m_w_out, norm2_g=m_norm2_g, w_ff1=m_w_ff1,
             w_ff2=m_w_ff2, final_norm_g=m_final_norm_g)
    v = dict(norm1_g=v_norm1_g, w_in=v_w_in, sgu_ln_g=v_sgu_ln_g, sgu_ln_b=v_sgu_ln_b, sgu_w=v_sgu_w, sgu_b=v_sgu_b,
             attn_out_g=v_attn_out_g, gmlp_out_g=v_gmlp_out_g, w_out=v_w_out, norm2_g=v_norm2_g, w_ff1=v_w_ff1,
             w_ff2=v_w_ff2, final_norm_g=v_final_norm_g)
    big = ("w_in", "w_out", "w_ff1", "w_ff2")
    core = lax.axis_index("c")

    w_in_t, = _all_gather([w_in[0].T.astype(BF16)], "w_in_all_gather")
    rest = (w_out[0].astype(BF16), w_ff1[0].T.astype(BF16), w_ff2[0].astype(BF16))
    loss, dx, parts, small_grads = _local_step(x[0], loss_target[0], {n: w[n] for n in SMALL_NAMES},
                                               w_in_t.reshape(IN_W, D_MODEL), rest, core=core)
    loss = lax.psum(loss, ("x", "y", "c"))

    new = {}
    for n, p, transposed, tr in zip(big, parts, (True, False, True, False), (128, 128, 128, 256)):
        new[n] = [a[None] for a in _adamw(w[n][0], m[n][0], v[n][0], p, "adamw_" + n, tr, transposed)]

    small_parts, = _all_gather([_pack_small(small_grads)], "small_grad_all_gather")
    packed = _adamw(_pack_small({n: w[n] for n in SMALL_NAMES}), _pack_small({n: m[n] for n in SMALL_NAMES}),
                    _pack_small({n: v[n] for n in SMALL_NAMES}), small_parts, "adamw_small", SMALL_ROWS)

    outs = []
    for i, ps in enumerate(packed):
        d = {n: new[n][i] for n in big}
        d.update(_unpack_small(ps, w))
        outs.extend(d[n] for n in WEIGHT_ORDER)
    return (loss, dx[None], *outs)
```

```python
import functools
import math

import numpy as np
import jax
import jax.numpy as jnp
from jax import lax
from jax.experimental import pallas as pl
from jax.experimental.pallas import tpu as pltpu

F32 = jnp.float32
BF16 = jnp.bfloat16

D_MODEL = 1024
HEAD_DIM = 64
N_HEADS = 12
ATTN_W = N_HEADS * HEAD_DIM
N_GROUPS = 4
GMLP_W = N_GROUPS * HEAD_DIM
IN_W = 3 * ATTN_W + 2 * GMLP_W
D_FF = 4 * D_MODEL
CHUNK = 128
DILATIONS = (1, 4, 16)
EPS = 1e-6
Q_SCALE = HEAD_DIM ** -0.5
NEG = -1e30

ADAM_LR, ADAM_B1, ADAM_B2, ADAM_EPS, ADAM_WD, ADAM_STEP = 0.001, 0.9, 0.999, 1e-08, 0.01, 10

N_DEV = 8
LANES = 128
VMEM_LIMIT = 56 << 20
SMALL_ROWS = 552

TM_PROJ = 512
TM_FFN = 512
FF_CHUNK = 512
DW_TILE = (512, 1024, 4096)
DW_TILE_IN = (IN_W // 2, 1024, 2048)

MESH = pl.DeviceIdType.MESH


def _alibi_slopes(n):
    def pow2(m):
        start = 2.0 ** (-8.0 / m)
        return [start ** (i + 1) for i in range(m)]
    c = 2 ** int(math.floor(math.log2(n)))
    s = pow2(n) if c == n else pow2(c) + pow2(2 * c)[0::2][: n - c]
    return np.asarray(s, dtype=np.float32)


SLOPES = _alibi_slopes(N_HEADS)


def _params(sem=None):
    kw = dict(vmem_limit_bytes=VMEM_LIMIT)
    if sem is not None:
        kw["dimension_semantics"] = sem
    return pltpu.CompilerParams(**kw)


def _rows(tm, n):
    return pl.BlockSpec((tm, n), lambda i: (i, 0))


def _resident(shape):
    return pl.BlockSpec(shape, lambda *_: (0,) * len(shape), pipeline_mode=pl.Buffered(1))


def _rms(x):
    r = lax.rsqrt(jnp.mean(x * x, axis=-1, keepdims=True) + EPS)
    return x * r, r


def _rms_bwd(n, r, g, dy):
    dn = dy * g
    return r * (dn - n * jnp.mean(dn * n, axis=-1, keepdims=True))


def _accum_rows(acc_ref, v):
    acc_ref[...] += jnp.broadcast_to(jnp.sum(v, axis=0, keepdims=True), acc_ref.shape)


_G0 = math.sqrt(2.0 / math.pi)
_G1 = 0.044715


def _gelu(x):
    t = jnp.tanh(_G0 * (x + _G1 * (x * x * x)))
    return x * (0.5 * (1.0 + t)), t


def _gelu_grad(x, t):
    return 0.5 * (1.0 + t) + 0.5 * x * (1.0 - t * t) * (_G0 * (1.0 + 3.0 * _G1 * x * x))


NT = (((1,), (1,)), ((), ()))
TN = (((0,), (0,)), ((), ()))


def _dot(a, b, dims=None):
    if dims is None:
        return jnp.dot(a, b, preferred_element_type=F32)
    return lax.dot_general(a, b, dims, preferred_element_type=F32)


def _proj_fwd(x, g1, w_in_t):
    T = x.shape[0]
    tm = TM_PROJ

    def body(x_ref, g_ref, w_ref, hn_ref, q_ref, k_ref, v_ref, u_ref, z_ref):
        n, _ = _rms(x_ref[...])
        hn = (n * g_ref[...]).astype(BF16)
        hn_ref[...] = hn
        a = ATTN_W
        q_ref[...] = _dot(hn, w_ref[0:a, :], NT) * Q_SCALE
        k_ref[...] = _dot(hn, w_ref[a:2 * a, :], NT)
        v_ref[...] = _dot(hn, w_ref[2 * a:3 * a, :], NT)
        u_ref[...] = _dot(hn, w_ref[3 * a:3 * a + GMLP_W, :], NT)
        z_ref[...] = _dot(hn, w_ref[3 * a + GMLP_W:, :], NT)

    sds = jax.ShapeDtypeStruct
    return pl.pallas_call(
        body, name="proj_fwd", grid=(T // tm,),
        in_specs=[_rows(tm, D_MODEL), _resident((1, D_MODEL)), _resident((IN_W, D_MODEL))],
        out_specs=[_rows(tm, D_MODEL), _rows(tm, ATTN_W), _rows(tm, ATTN_W), _rows(tm, ATTN_W),
                   _rows(tm, GMLP_W), _rows(tm, GMLP_W)],
        out_shape=[sds((T, D_MODEL), BF16), sds((T, ATTN_W), F32), sds((T, ATTN_W), F32),
                   sds((T, ATTN_W), F32), sds((T, GMLP_W), F32), sds((T, GMLP_W), F32)],
        compiler_params=_params(("parallel",)),
    )(x, g1, w_in_t)


ATT_TILE = 2048
ATT_BLOCKS = ATT_TILE // CHUNK


def _slope_table():
    row = np.repeat(SLOPES, HEAD_DIM)
    return jnp.asarray(np.broadcast_to(row[None], (8, ATTN_W)), F32)


def _stacked_consts(sl_ref):
    shape = (2 * CHUNK, 2 * CHUNK)
    row = lax.broadcasted_iota(jnp.int32, shape, 0)
    kj = lax.broadcasted_iota(jnp.int32, shape, 1)
    steps = (row & (CHUNK - 1)) + CHUNK - kj
    band = (steps >= 0) & (steps <= CHUNK)
    sl = sl_ref[0:1, :]
    upper = lax.broadcasted_iota(jnp.int32, (2 * CHUNK, 1), 0) < CHUNK
    slope2 = jnp.where(upper, sl[:, 0:1], sl[:, HEAD_DIM:HEAD_DIM + 1])
    return kj, band, slope2 * steps.astype(F32)


def _residue_view(a):
    return a.reshape(a.shape[0] // ATT_BLOCKS, ATT_BLOCKS, a.shape[1])


def _tile_copies(hbm, buf, sem, hp, t, to_hbm=False):
    rows = pl.ds(pl.multiple_of(t * CHUNK, CHUNK), CHUNK)
    lanes = pl.ds(pl.multiple_of(hp * LANES, LANES), LANES)
    pairs = [(hbm.at[rows, r, lanes], buf.at[r]) for r in range(ATT_BLOCKS)]
    return [pltpu.make_async_copy(v, h, sem) if to_hbm else pltpu.make_async_copy(h, v, sem) for h, v in pairs]


def _wait_tile(buf, sem):
    pltpu.make_async_copy(buf, buf, sem).wait()


def _residue_rows(d, j):
    if d == 16:
        return [(j, 0, CHUNK)]
    if d == 4:
        return [(j % 4 + 4 * m, 32 * (j // 4), 32) for m in range(4)]
    return [(r, 8 * j, 8) for r in range(ATT_BLOCKS)]


def _block_order(p, d):
    if d == 16:
        return p
    if d == 4:
        return 4 * (p & 31) + (p >> 5)
    return 16 * (p & 7) + (p >> 3)


def _first_in_tile(d, j):
    return _residue_rows(d, j)[0][1] == 0


def _rm_block(buf, d, j):
    return jnp.concatenate([buf[r, lo:lo + n, :] for r, lo, n in _residue_rows(d, j)], axis=0)


def _rm_block_before(buf, buf_before, d, j):
    if _first_in_tile(d, j):
        return jnp.concatenate([buf_before[r, CHUNK - n:CHUNK, :] for r, _, n in _residue_rows(d, j)], axis=0)
    return jnp.concatenate([buf[r, lo - n:lo, :] for r, lo, n in _residue_rows(d, j)], axis=0)


def _rm_store(buf, d, j, val):
    at = 0
    for r, lo, n in _residue_rows(d, j):
        buf[r, lo:lo + n, :] = val[at:at + n, :]
        at += n


def _residue_consts(sl_ref, d):
    shape = (2 * CHUNK, 2 * CHUNK)
    row = lax.broadcasted_iota(jnp.int32, shape, 0)
    col = lax.broadcasted_iota(jnp.int32, shape, 1)
    steps = _block_order(row & (CHUNK - 1), d) + CHUNK - (_block_order(col & (CHUNK - 1), d) + (col & CHUNK))
    band = (steps >= 0) & (steps <= CHUNK)
    sl = sl_ref[0:1, :]
    upper = lax.broadcasted_iota(jnp.int32, (2 * CHUNK, 1), 0) < CHUNK
    slope2 = jnp.where(upper, sl[:, 0:1], sl[:, HEAD_DIM:HEAD_DIM + 1])
    return col, jnp.where(band, -(float(d) * slope2 * steps.astype(F32)), NEG)


def _block_rows(j, d):
    if d == 1:
        start = j * CHUNK
        return j, start, pl.ds(start, CHUNK)
    r, b = j % d, j // d
    start = r + (d * CHUNK) * b
    return b, start, pl.ds(start, CHUNK, stride=d)


def _prev_rows(j, d):
    _, start, _ = _block_rows(j, d)
    if d == 1:
        return pl.ds(start - CHUNK, CHUNK), pl.ds(ATT_TILE - CHUNK, CHUNK)
    last = j % d + (d * CHUNK) * (ATT_BLOCKS // d - 1)
    return pl.ds(start - d * CHUNK, CHUNK, stride=d), pl.ds(last, CHUNK, stride=d)


def _kv_block(j, d, rows, kc_ref, kp_ref, vc_ref, vp_ref):
    b = j // d
    here, before = _prev_rows(j, d)
    if ATT_BLOCKS // d == 1:
        kp, vp = kp_ref[before, :], vp_ref[before, :]
    else:
        src_k, src_v, src_rows = (kp_ref, vp_ref, before) if b == 0 else (kc_ref, vc_ref, here)
        kp, vp = src_k[src_rows, :], src_v[src_rows, :]
    kcat = jnp.concatenate([kp, kc_ref[rows, :]], axis=0).astype(BF16)
    vcat = jnp.concatenate([vp, vc_ref[rows, :]], axis=0).astype(BF16)
    return kcat, vcat


def _stack_heads(xb, head0):
    zero = jnp.zeros_like(xb)
    return jnp.concatenate([jnp.where(head0, xb, zero), jnp.where(head0, zero, xb)], axis=0).astype(BF16)


def _unstack_heads(x2, head0):
    return jnp.where(head0, x2[:CHUNK, :], x2[CHUNK:, :])


def _scores(q2, kcat, bias, kj, first):
    s = _dot(q2, kcat, NT) + bias
    return jnp.where(kj < jnp.where(first, CHUNK, 0), NEG, s)


def _attn_tile_specs(nt, lag):
    clamp = (lambda t: jnp.minimum(t, nt - 1)) if lag else (lambda t: t)
    cur = pl.BlockSpec((ATT_TILE, LANES), lambda c, t: (clamp(t), c))
    prev = pl.BlockSpec((ATT_TILE, LANES), lambda c, t: (jnp.maximum(clamp(t) - 1, 0), c))
    slope = pl.BlockSpec((8, LANES), lambda c, t: (0, c))
    return cur, prev, slope


def _attn_fwd(q, k, v, shards=()):
    T = q.shape[0]
    nt = T // ATT_TILE
    ns = len(shards)
    steps = (ATTN_W // LANES) * nt

    def body(sl_ref, q_hbm, k_hbm, v_hbm, *rest):
        x_refs, rest = rest[:ns], rest[ns:]
        attn_hbm, lse_hbm = rest[:2]
        g_refs, rest = rest[2:2 + ns], rest[2 + ns:]
        qbuf, kbuf, vbuf, obuf, lbuf = rest[:5]
        o_acc, l_acc = rest[5:8], rest[8:11]
        sem_q, sem_k, sem_v, sem_o, sem_l = rest[11:16]
        hp, t = pl.program_id(0), pl.program_id(1)
        step = hp * nt + t
        two, three = step % 2, step % 3
        before, after = (step + 2) % 3, (step + 1) % 3
        if ns:
            start, forward, finish = _gather_phases(x_refs, g_refs, *rest[16:])
            pl.when(step == 0)(start)
            pl.when(step == steps // 2)(forward)

        def fetch(hp_, t_, two_, three_):
            for cp in (_tile_copies(q_hbm, qbuf.at[two_], sem_q.at[two_], hp_, t_)
                       + _tile_copies(k_hbm, kbuf.at[three_], sem_k.at[three_], hp_, t_)
                       + _tile_copies(v_hbm, vbuf.at[three_], sem_v.at[three_], hp_, t_)):
                cp.start()

        @pl.when(step == 0)
        def _():
            kbuf[2] = jnp.zeros((ATT_BLOCKS, CHUNK, LANES), F32)
            vbuf[2] = jnp.zeros((ATT_BLOCKS, CHUNK, LANES), F32)
            fetch(0, 0, 0, 0)

        @pl.when(step + 1 < steps)
        def _():
            fetch((step + 1) // nt, (step + 1) % nt, 1 - two, after)

        _wait_tile(qbuf.at[two], sem_q.at[two])
        _wait_tile(kbuf.at[three], sem_k.at[three])
        _wait_tile(vbuf.at[three], sem_v.at[three])

        @pl.when(step >= 2)
        def _():
            _wait_tile(obuf.at[two], sem_o.at[two])
            _wait_tile(lbuf.at[two], sem_l.at[two])

        q_t, k_t, v_t = qbuf.at[two], kbuf.at[three], vbuf.at[three]
        k_b, v_b = kbuf.at[before], vbuf.at[before]
        head0 = lax.broadcasted_iota(jnp.int32, (CHUNK, LANES), 1) < HEAD_DIM
        for pi, d in enumerate(DILATIONS):
            col, bias = _residue_consts(sl_ref, d)
            for j in range(ATT_BLOCKS):
                kcat = jnp.concatenate([_rm_block_before(k_t, k_b, d, j), _rm_block(k_t, d, j)], axis=0).astype(BF16)
                vcat = jnp.concatenate([_rm_block_before(v_t, v_b, d, j), _rm_block(v_t, d, j)], axis=0).astype(BF16)
                q2 = _stack_heads(_rm_block(q_t, d, j), head0)
                s = _scores(q2, kcat, bias, col, (t == 0) if _first_in_tile(d, j) else False)
                m = jnp.max(s, axis=-1, keepdims=True)
                p = jnp.exp(s - m)
                l = jnp.sum(p, axis=-1, keepdims=True)
                _rm_store(o_acc[pi], d, j, _unstack_heads(_dot(p.astype(BF16), vcat) / l, head0))
                _rm_store(l_acc[pi], d, j,
                          _unstack_heads(jnp.broadcast_to(m + jnp.log(l), (2 * CHUNK, LANES)), head0))

        for r in range(ATT_BLOCKS):
            a, b, c = l_acc[0][r], l_acc[1][r], l_acc[2][r]
            m = jnp.maximum(jnp.maximum(a, b), c)
            ea, eb, ec = jnp.exp(a - m), jnp.exp(b - m), jnp.exp(c - m)
            tot = ea + eb + ec
            obuf[two, r] = (ea * o_acc[0][r] + eb * o_acc[1][r] + ec * o_acc[2][r]) / tot
            lbuf[two, r] = m + jnp.log(tot)

        for cp in (_tile_copies(attn_hbm, obuf.at[two], sem_o.at[two], hp, t, to_hbm=True)
                   + _tile_copies(lse_hbm, lbuf.at[two], sem_l.at[two], hp, t, to_hbm=True)):
            cp.start()

        @pl.when(step == steps - 1)
        def _():
            for slot in (two, 1 - two)[:min(steps, 2)]:
                _wait_tile(obuf.at[slot], sem_o.at[slot])
                _wait_tile(lbuf.at[slot], sem_l.at[slot])

        if ns:
            pl.when(step == steps - 1)(finish)

    tile = lambda n: pltpu.VMEM((n, ATT_BLOCKS, CHUNK, LANES), F32)
    dma = lambda n: pltpu.SemaphoreType.DMA((n,))
    view = jax.ShapeDtypeStruct((T // ATT_BLOCKS, ATT_BLOCKS, ATTN_W), F32)
    outs = pl.pallas_call(
        body, name="attn_fwd", grid=(ATTN_W // LANES, nt),
        in_specs=[pl.BlockSpec((8, LANES), lambda c, t: (0, c))] + [_HBM] * (3 + ns),
        out_specs=[_HBM] * (2 + ns),
        out_shape=[view, view] + [_gathered_shape(s) for s in shards],
        scratch_shapes=[tile(2), tile(3), tile(3), tile(2), tile(2)] + [pltpu.VMEM((ATT_BLOCKS, CHUNK, LANES), F32)] * 6
        + [dma(2), dma(3), dma(3), dma(2), dma(2)] + (_gather_sems(ns) if ns else []),
        compiler_params=_params(("arbitrary", "arbitrary")),
    )(_slope_table(), _residue_view(q), _residue_view(k), _residue_view(v), *shards)
    return outs[0].reshape(T, ATTN_W), outs[1].reshape(T, ATTN_W), tuple(outs[2:])


def _group_mean(v, grp):
    out = jnp.zeros_like(v)
    for g in range(N_GROUPS):
        mk = grp == g
        s = jnp.sum(jnp.where(mk, v, 0.0), axis=-1, keepdims=True) * (1.0 / HEAD_DIM)
        out = jnp.where(mk, s, out)
    return out


def _gmlp_core(uu, zz, lg, lb, ws, sb_ref, grp):
    ug, tu = _gelu(uu)
    zg, tz = _gelu(zz)
    zc = zg - _group_mean(zg, grp)
    rstd = lax.rsqrt(_group_mean(zc * zc, grp) + EPS)
    xhat = zc * rstd
    zn16 = (xhat * lg + lb).astype(BF16)
    mixed = jnp.zeros_like(uu)
    for g in range(N_GROUPS):
        mixed = jnp.where(grp == g, _dot(ws[g], zn16) + sb_ref[:, g:g + 1], mixed)
    return ug, tu, tz, xhat, rstd, zn16, mixed


def _causal_ws(w_ref):
    ti = lax.broadcasted_iota(jnp.int32, (CHUNK, CHUNK), 0)
    si = lax.broadcasted_iota(jnp.int32, (CHUNK, CHUNK), 1)
    causal = si <= ti
    return causal, [jnp.where(causal, w_ref[g], 0.0).astype(BF16) for g in range(N_GROUPS)]


def _gmlp_fwd(u, z, ln_g, ln_b, sgu_w, sgu_bt):
    T = u.shape[0]
    tg = 512

    def body(u_ref, z_ref, g_ref, b_ref, w_ref, sb_ref, out_ref):
        grp = lax.broadcasted_iota(jnp.int32, (CHUNK, GMLP_W), 1) // HEAD_DIM
        _, ws = _causal_ws(w_ref)
        for ci in range(tg // CHUNK):
            rows = slice(ci * CHUNK, (ci + 1) * CHUNK)
            ug, _, _, _, _, _, mixed = _gmlp_core(u_ref[rows, :], z_ref[rows, :], g_ref[...], b_ref[...],
                                                  ws, sb_ref, grp)
            out_ref[rows, :] = ug * mixed

    return pl.pallas_call(
        body, name="gmlp_fwd", grid=(T // tg,),
        in_specs=[_rows(tg, GMLP_W), _rows(tg, GMLP_W), _resident((1, GMLP_W)), _resident((1, GMLP_W)),
                  _resident((N_GROUPS, CHUNK, CHUNK)), _resident((CHUNK, N_GROUPS))],
        out_specs=_rows(tg, GMLP_W),
        out_shape=jax.ShapeDtypeStruct((T, GMLP_W), F32),
        compiler_params=_params(("parallel",)),
    )(u, z, ln_g, ln_b, sgu_w, sgu_bt)


def _out_fwd(attn, gm, ga, gg, w_out, x, g2):
    T = x.shape[0]
    tm = TM_PROJ

    def body(a_ref, m_ref, ga_ref, gg_ref, w_ref, x_ref, g2_ref, mix_ref, h1_ref, hn2_ref):
        an, _ = _rms(a_ref[...])
        gn, _ = _rms(m_ref[...])
        an = (an * ga_ref[...]).astype(BF16)
        gn = (gn * gg_ref[...]).astype(BF16)
        mix_ref[:, 0:ATTN_W] = an
        mix_ref[:, ATTN_W:] = gn
        h1 = x_ref[...] + _dot(an, w_ref[0:ATTN_W, :]) + _dot(gn, w_ref[ATTN_W:, :])
        h1_ref[...] = h1
        n2, _ = _rms(h1)
        hn2_ref[...] = (n2 * g2_ref[...]).astype(BF16)

    sds = jax.ShapeDtypeStruct
    return pl.pallas_call(
        body, name="out_fwd", grid=(T // tm,),
        in_specs=[_rows(tm, ATTN_W), _rows(tm, GMLP_W), _resident((1, ATTN_W)), _resident((1, GMLP_W)),
                  _resident((D_MODEL, D_MODEL)), _rows(tm, D_MODEL), _resident((1, D_MODEL))],
        out_specs=[_rows(tm, D_MODEL)] * 3,
        out_shape=[sds((T, D_MODEL), BF16), sds((T, D_MODEL), F32), sds((T, D_MODEL), BF16)],
        compiler_params=_params(("parallel",)),
    )(attn, gm, ga, gg, w_out, x, g2)


def _ffn_fwd(hn2, h1, w1t, w2, gf, tgt):
    T = h1.shape[0]
    tm = TM_FFN

    def body(hn_ref, h1_ref, w1_ref, w2_ref, gf_ref, t_ref, r_ref, dhf_ref, dhb_ref, loss_ref, dgf_ref):
        i = pl.program_id(0)

        @pl.when(i == 0)
        def _():
            loss_ref[...] = jnp.zeros_like(loss_ref)
            dgf_ref[...] = jnp.zeros_like(dgf_ref)

        hn = hn_ref[...]
        acc = h1_ref[...]
        for j in range(D_FF // FF_CHUNK):
            cols = slice(j * FF_CHUNK, (j + 1) * FF_CHUNK)
            r = jnp.maximum(_dot(hn, w1_ref[cols, :], NT), 0.0)
            r_ref[:, cols] = r.astype(BF16)
            act = jnp.square(r).astype(BF16)
            acc = acc + _dot(act, w2_ref[cols, :])
        n3, r3 = _rms(acc)
        gf_row = gf_ref[...]
        e = n3 * gf_row - t_ref[...]
        loss_ref[...] += 0.5 * jnp.sum(jnp.mean(e * e, axis=-1, keepdims=True))
        dy = e * (1.0 / D_MODEL)
        _accum_rows(dgf_ref, dy * n3)
        dh2 = _rms_bwd(n3, r3, gf_row, dy)
        dhf_ref[...] = dh2
        dhb_ref[...] = dh2.astype(BF16)

    sds = jax.ShapeDtypeStruct
    acc_spec = lambda n: pl.BlockSpec((8, n), lambda i: (0, 0))
    return pl.pallas_call(
        body, name="ffn_fwd", grid=(T // tm,),
        in_specs=[_rows(tm, D_MODEL), _rows(tm, D_MODEL), _resident((D_FF, D_MODEL)), _resident((D_FF, D_MODEL)),
                  _resident((1, D_MODEL)), _rows(tm, D_MODEL)],
        out_specs=[_rows(tm, D_FF), _rows(tm, D_MODEL), _rows(tm, D_MODEL), acc_spec(LANES), acc_spec(D_MODEL)],
        out_shape=[sds((T, D_FF), BF16), sds((T, D_MODEL), F32), sds((T, D_MODEL), BF16),
                   sds((8, LANES), F32), sds((8, D_MODEL), F32)],
        compiler_params=_params(("arbitrary",)),
    )(hn2, h1, w1t, w2, gf, tgt)


def _ffn_bwd(dh2b, dh2f, relu, h1, g2, w2, w1t):
    T = h1.shape[0]
    tm = TM_FFN

    def body(db_ref, df_ref, r_ref, h1_ref, g2_ref, w2_ref, w1t_ref, da_ref, d1f_ref, d1b_ref, dg_ref):
        @pl.when(pl.program_id(0) == 0)
        def _():
            dg_ref[...] = jnp.zeros_like(dg_ref)

        db = db_ref[...]
        acc = jnp.zeros((tm, D_MODEL), F32)
        for j in range(D_FF // FF_CHUNK):
            cols = slice(j * FF_CHUNK, (j + 1) * FF_CHUNK)
            da = (_dot(db, w2_ref[cols, :], NT) * (2.0 * r_ref[:, cols].astype(F32))).astype(BF16)
            da_ref[:, cols] = da
            acc = acc + _dot(da, w1t_ref[cols, :])
        n2, r2 = _rms(h1_ref[...])
        _accum_rows(dg_ref, acc * n2)
        dh1 = df_ref[...] + _rms_bwd(n2, r2, g2_ref[...], acc)
        d1f_ref[...] = dh1
        d1b_ref[...] = dh1.astype(BF16)

    sds = jax.ShapeDtypeStruct
    return pl.pallas_call(
        body, name="ffn_bwd", grid=(T // tm,),
        in_specs=[_rows(tm, D_MODEL), _rows(tm, D_MODEL), _rows(tm, D_FF), _rows(tm, D_MODEL),
                  _resident((1, D_MODEL)), _resident((D_FF, D_MODEL)), _resident((D_FF, D_MODEL))],
        out_specs=[_rows(tm, D_FF), _rows(tm, D_MODEL), _rows(tm, D_MODEL),
                   pl.BlockSpec((8, D_MODEL), lambda i: (0, 0))],
        out_shape=[sds((T, D_FF), BF16), sds((T, D_MODEL), F32), sds((T, D_MODEL), BF16), sds((8, D_MODEL), F32)],
        compiler_params=_params(("arbitrary",)),
    )(dh2b, dh2f, relu, h1, g2, w2, w1t)


def _out_bwd(dh1b, w_out, attn, gm, ga, gg):
    T = attn.shape[0]
    tm = TM_PROJ

    def body(d_ref, w_ref, a_ref, m_ref, ga_ref, gg_ref, da_ref, dm_ref, dga_ref, dgg_ref):
        @pl.when(pl.program_id(0) == 0)
        def _():
            dga_ref[...] = jnp.zeros_like(dga_ref)
            dgg_ref[...] = jnp.zeros_like(dgg_ref)

        d = d_ref[...]
        dan = _dot(d, w_ref[0:ATTN_W, :], NT)
        dgn = _dot(d, w_ref[ATTN_W:, :], NT)
        na, ra = _rms(a_ref[...])
        ng, rg = _rms(m_ref[...])
        _accum_rows(dga_ref, dan * na)
        _accum_rows(dgg_ref, dgn * ng)
        da_ref[...] = _rms_bwd(na, ra, ga_ref[...], dan)
        dm_ref[...] = _rms_bwd(ng, rg, gg_ref[...], dgn)

    sds = jax.ShapeDtypeStruct
    return pl.pallas_call(
        body, name="out_bwd", grid=(T // tm,),
        in_specs=[_rows(tm, D_MODEL), _resident((D_MODEL, D_MODEL)), _rows(tm, ATTN_W), _rows(tm, GMLP_W),
                  _resident((1, ATTN_W)), _resident((1, GMLP_W))],
        out_specs=[_rows(tm, ATTN_W), _rows(tm, GMLP_W), pl.BlockSpec((8, ATTN_W), lambda i: (0, 0)),
                   pl.BlockSpec((8, GMLP_W), lambda i: (0, 0))],
        out_shape=[sds((T, ATTN_W), F32), sds((T, GMLP_W), F32), sds((8, ATTN_W), F32), sds((8, GMLP_W), F32)],
        compiler_params=_params(("arbitrary",)),
    )(dh1b, w_out, attn, gm, ga, gg)


def _gmlp_bwd(u, z, dgm, ln_g, ln_b, sgu_w, sgu_bt):
    T = u.shape[0]
    tg = 512
    nsteps = T // tg

    def body(u_ref, z_ref, d_ref, g_ref, b_ref, w_ref, sb_ref, du_ref, dz_ref, dlg_ref, dlb_ref, dw_ref, dsb_ref):
        i = pl.program_id(0)

        @pl.when(i == 0)
        def _():
            for ref in (dlg_ref, dlb_ref, dw_ref, dsb_ref):
                ref[...] = jnp.zeros_like(ref)

        grp = lax.broadcasted_iota(jnp.int32, (CHUNK, GMLP_W), 1) // HEAD_DIM
        lane = lax.broadcasted_iota(jnp.int32, (CHUNK, LANES), 1)
        causal, ws = _causal_ws(w_ref)
        lg = g_ref[...]
        for ci in range(tg // CHUNK):
            rows = slice(ci * CHUNK, (ci + 1) * CHUNK)
            uu, zz = u_ref[rows, :], z_ref[rows, :]
            ug, tu, tz, xhat, rstd, zn16, mixed = _gmlp_core(uu, zz, lg, b_ref[...], ws, sb_ref, grp)
            dgm_c = d_ref[rows, :]
            dmx = dgm_c * ug
            du_ref[rows, :] = dgm_c * mixed * _gelu_grad(uu, tu)
            dmx16 = dmx.astype(BF16)
            dzn = jnp.zeros_like(dmx)
            dsb = jnp.zeros((CHUNK, LANES), F32)
            for g in range(N_GROUPS):
                mk = grp == g
                dzn = jnp.where(mk, _dot(ws[g], dmx16, TN), dzn)
                dw_ref[g] += _dot(jnp.where(mk, dmx16, jnp.zeros_like(dmx16)), zn16, NT)
                dsb = jnp.where(lane == g, jnp.sum(jnp.where(mk, dmx, 0.0), axis=-1, keepdims=True), dsb)
            dsb_ref[...] += dsb
            _accum_rows(dlg_ref, dzn * xhat)
            _accum_rows(dlb_ref, dzn)
            dxh = dzn * lg
            dzg = rstd * (dxh - _group_mean(dxh, grp) - xhat * _group_mean(dxh * xhat, grp))
            dz_ref[rows, :] = dzg * _gelu_grad(zz, tz)

        @pl.when(i == nsteps - 1)
        def _():
            for g in range(N_GROUPS):
                dw_ref[g] = jnp.where(causal, dw_ref[g], 0.0)

    sds = jax.ShapeDtypeStruct
    return pl.pallas_call(
        body, name="gmlp_bwd", grid=(nsteps,),
        in_specs=[_rows(tg, GMLP_W)] * 3 + [_resident((1, GMLP_W)), _resident((1, GMLP_W)),
                                              _resident((N_GROUPS, CHUNK, CHUNK)), _resident((CHUNK, N_GROUPS))],
        out_specs=[_rows(tg, GMLP_W), _rows(tg, GMLP_W), pl.BlockSpec((8, GMLP_W), lambda i: (0, 0)),
                   pl.BlockSpec((8, GMLP_W), lambda i: (0, 0)),
                   pl.BlockSpec((N_GROUPS, CHUNK, CHUNK), lambda i: (0, 0, 0)),
                   pl.BlockSpec((CHUNK, LANES), lambda i: (0, 0))],
        out_shape=[sds((T, GMLP_W), F32), sds((T, GMLP_W), F32), sds((8, GMLP_W), F32), sds((8, GMLP_W), F32),
                   sds((N_GROUPS, CHUNK, CHUNK), F32), sds((CHUNK, LANES), F32)],
        compiler_params=_params(("arbitrary",)),
    )(u, z, dgm, ln_g, ln_b, sgu_w, sgu_bt)


def _attn_bwd(q, k, v, dattn, attn, lse, owner_grads=()):
    T = q.shape[0]
    nt = T // ATT_TILE
    ns = len(owner_grads)
    steps = (ATTN_W // LANES) * (nt + 1)

    def body(sl_ref, q_ref, kc_ref, kp_ref, vc_ref, vp_ref, do_ref, o_ref, lse_ref, *rest):
        p_refs, rest = rest[:ns], rest[ns:]
        dq_ref, dk_ref, dv_ref = rest[:3]
        r_refs, rest = rest[3:3 + ns], rest[3 + ns:]
        dk_acc, dv_acc, delta_s = rest[:3]
        t = pl.program_id(1)
        if ns:
            step = pl.program_id(0) * (nt + 1) + t
            start, finish = _owner_exchange_phases(p_refs, r_refs, *rest[3:])
            pl.when(step == 0)(start)
        s_cur = t % 2
        s_prev = 1 - s_cur

        @pl.when(t == 0)
        def _():
            dk_acc[...] = jnp.zeros_like(dk_acc)
            dv_acc[...] = jnp.zeros_like(dv_acc)

        @pl.when(t < nt)
        def _():
            kj, band, base = _stacked_consts(sl_ref)
            head0 = lax.broadcasted_iota(jnp.int32, (CHUNK, LANES), 1) < HEAD_DIM
            for ci in range(ATT_TILE // 256):
                rows = slice(ci * 256, (ci + 1) * 256)
                h0 = lax.broadcasted_iota(jnp.int32, (256, LANES), 1) < HEAD_DIM
                dd = do_ref[rows, :] * o_ref[rows, :]
                d0 = jnp.sum(jnp.where(h0, dd, 0.0), axis=-1, keepdims=True)
                d1 = jnp.sum(jnp.where(h0, 0.0, dd), axis=-1, keepdims=True)
                delta_s[rows, :] = jnp.where(h0, d0, d1)
            dq_ref[...] = jnp.zeros_like(dq_ref)

            def column(xb):
                return jnp.concatenate([xb[:, 0:1], xb[:, HEAD_DIM:HEAD_DIM + 1]], axis=0)

            for d in DILATIONS:
                bias = jnp.where(band, -(float(d) * base), NEG)

                def block(j, carry, d=d, bias=bias):
                    b, _, rows = _block_rows(j, d)
                    kcat, vcat = _kv_block(j, d, rows, kc_ref, kp_ref, vc_ref, vp_ref)
                    q2 = _stack_heads(q_ref[rows, :], head0)
                    do2 = _stack_heads(do_ref[rows, :], head0)
                    s = _scores(q2, kcat, bias, kj, (t == 0) & (b == 0))
                    p = jnp.exp(s - column(lse_ref[rows, :]))
                    ds = (p * (_dot(do2, vcat, NT) - column(delta_s[rows, :]))).astype(BF16)
                    dq_ref[rows, :] += _unstack_heads(_dot(ds, kcat), head0)
                    ck = _dot(ds, q2, TN)
                    cv = _dot(p.astype(BF16), do2, TN)
                    dk_acc[s_cur, rows, :] += ck[CHUNK:, :]
                    dv_acc[s_cur, rows, :] += cv[CHUNK:, :]
                    here, before = _prev_rows(j, d)
                    if ATT_BLOCKS // d == 1:
                        dk_acc[s_prev, before, :] += ck[:CHUNK, :]
                        dv_acc[s_prev, before, :] += cv[:CHUNK, :]
                    else:
                        slot, dst = (s_prev, before) if b == 0 else (s_cur, here)
                        dk_acc[slot, dst, :] += ck[:CHUNK, :]
                        dv_acc[slot, dst, :] += cv[:CHUNK, :]
                    return carry

                for j in range(ATT_BLOCKS):
                    block(j, 0)

        dk_ref[...] = dk_acc[s_prev]
        dv_ref[...] = dv_acc[s_prev]
        dk_acc[s_prev] = jnp.zeros((ATT_TILE, LANES), F32)
        dv_acc[s_prev] = jnp.zeros((ATT_TILE, LANES), F32)

        if ns:
            pl.when(step == steps - 1)(finish)

    cur, prev, slope = _attn_tile_specs(nt, lag=True)
    late = pl.BlockSpec((ATT_TILE, LANES), lambda c, t: (jnp.maximum(t - 1, 0), c))
    sds = jax.ShapeDtypeStruct((T, ATTN_W), F32)
    outs = pl.pallas_call(
        body, name="attn_bwd", grid=(ATTN_W // LANES, nt + 1),
        in_specs=[slope, cur, cur, prev, cur, prev, cur, cur, cur] + [_HBM] * ns,
        out_specs=[cur, late, late] + [_HBM] * ns,
        out_shape=[sds, sds, sds] + [jax.ShapeDtypeStruct(p.shape, p.dtype) for p in owner_grads],
        scratch_shapes=[pltpu.VMEM((2, ATT_TILE, LANES), F32), pltpu.VMEM((2, ATT_TILE, LANES), F32),
                        pltpu.VMEM((ATT_TILE, LANES), F32)] + (_owner_exchange_sems(ns) if ns else []),
        compiler_params=_params(("arbitrary", "arbitrary")),
    )(_slope_table(), q, k, k, v, v, dattn, attn, lse, *owner_grads)
    return outs[0], outs[1], outs[2], tuple(outs[3:])


def _dproj_assemble(dq, dk, dv, du, dz):
    T = du.shape[0]
    tm = 512

    def body(q_ref, k_ref, v_ref, u_ref, z_ref, out_ref):
        a = ATTN_W
        out_ref[:, 0:a] = (q_ref[...] * Q_SCALE).astype(BF16)
        out_ref[:, a:2 * a] = k_ref[...].astype(BF16)
        out_ref[:, 2 * a:3 * a] = v_ref[...].astype(BF16)
        out_ref[:, 3 * a:3 * a + GMLP_W] = u_ref[...].astype(BF16)
        out_ref[:, 3 * a + GMLP_W:] = z_ref[...].astype(BF16)

    return pl.pallas_call(
        body, name="dproj_assemble", grid=(T // tm,),
        in_specs=[_rows(tm, ATTN_W)] * 3 + [_rows(tm, GMLP_W)] * 2,
        out_specs=_rows(tm, IN_W),
        out_shape=jax.ShapeDtypeStruct((T, IN_W), BF16),
        compiler_params=_params(("parallel",)),
    )(dq, dk, dv, du, dz)


def _proj_bwd(dproj, w_in_t, x, g1, dh1, chip_sums=()):
    T = x.shape[0]
    tm = TM_PROJ
    ns = len(chip_sums)
    steps = T // tm

    def body(d_ref, w_ref, x_ref, g_ref, r_ref, *rest):
        p_refs, rest = rest[:ns], rest[ns:]
        dx_ref, dg_ref = rest[:2]
        r_refs, sems = rest[2:2 + ns], rest[2 + ns:]
        step = pl.program_id(0)
        if ns:
            start, finish = _chip_exchange_phases(p_refs, r_refs, *sems)
            pl.when(step == 0)(start)

        @pl.when(step == 0)
        def _():
            dg_ref[...] = jnp.zeros_like(dg_ref)

        dhn = _dot(d_ref[...], w_ref[...])
        n1, r1 = _rms(x_ref[...])
        _accum_rows(dg_ref, dhn * n1)
        dx_ref[...] = r_ref[...] + _rms_bwd(n1, r1, g_ref[...], dhn)
        if ns:
            pl.when(step == steps - 1)(finish)

    outs = pl.pallas_call(
        body, name="proj_bwd", grid=(steps,),
        in_specs=[_rows(tm, IN_W), _resident((IN_W, D_MODEL)), _rows(tm, D_MODEL), _resident((1, D_MODEL)),
                  _rows(tm, D_MODEL)] + [_HBM] * ns,
        out_specs=[_rows(tm, D_MODEL), pl.BlockSpec((8, D_MODEL), lambda i: (0, 0))] + [_HBM] * ns,
        out_shape=[jax.ShapeDtypeStruct((T, D_MODEL), F32), jax.ShapeDtypeStruct((8, D_MODEL), F32)]
        + [jax.ShapeDtypeStruct(p.shape, p.dtype) for p in chip_sums],
        scratch_shapes=_chip_exchange_sems(ns) if ns else [],
        compiler_params=_params(("arbitrary",)),
    )(dproj, w_in_t, x, g1, dh1, *chip_sums)
    return outs[0], outs[1], tuple(outs[2:])


def _dw(a, b, name, tile, square_a=False, out_dtype=F32):
    T, ka = a.shape
    nb = b.shape[1]
    tka, tnb, tt = tile
    tt = min(tt, T)
    last = T // tt - 1

    def body(a_ref, b_ref, o_ref, *scratch):
        acc_ref = scratch[0] if scratch else o_ref
        s = pl.program_id(2)

        @pl.when(s == 0)
        def _():
            acc_ref[...] = jnp.zeros_like(acc_ref)

        a_tile = a_ref[...]
        if square_a:
            a_tile = jnp.square(a_tile.astype(F32)).astype(BF16)
        acc_ref[...] += _dot(a_tile, b_ref[...], TN)
        if scratch:
            @pl.when(s == last)
            def _():
                o_ref[...] = acc_ref[...].astype(out_dtype)

    return pl.pallas_call(
        body, name=name, grid=(ka // tka, nb // tnb, T // tt),
        in_specs=[pl.BlockSpec((tt, tka), lambda i, j, s: (s, i)), pl.BlockSpec((tt, tnb), lambda i, j, s: (s, j))],
        out_specs=pl.BlockSpec((tka, tnb), lambda i, j, s: (i, j)),
        out_shape=jax.ShapeDtypeStruct((ka, nb), out_dtype),
        scratch_shapes=[] if out_dtype == F32 else [pltpu.VMEM((tka, tnb), F32)],
        compiler_params=_params(("parallel", "parallel", "arbitrary")),
    )(a, b)


def _adamw(w, m, v, parts, name, tr, transposed=False):
    R, C = w.shape
    P = parts.shape[0]

    def body(w_ref, m_ref, v_ref, p_ref, g_ref, d_ref, m2_ref, v2_ref):
        g = p_ref[0].astype(F32)
        for i in range(1, P):
            g = g + p_ref[i].astype(F32)
        if transposed:
            g = g.T
        m2 = ADAM_B1 * m_ref[...] + (1.0 - ADAM_B1) * g
        v2 = ADAM_B2 * v_ref[...] + (1.0 - ADAM_B2) * jnp.square(g)
        m_hat = m2 / (1.0 - ADAM_B1 ** ADAM_STEP)
        v_hat = v2 / (1.0 - ADAM_B2 ** ADAM_STEP)
        g_ref[...] = g
        d_ref[...] = -ADAM_LR * (m_hat / (jnp.sqrt(v_hat) + ADAM_EPS) + ADAM_WD * w_ref[...])
        m2_ref[...] = m2
        v2_ref[...] = v2

    spec = _rows(tr, C)
    part_spec = (pl.BlockSpec((P, C, tr), lambda i: (0, 0, i)) if transposed
                 else pl.BlockSpec((P, tr, C), lambda i: (0, i, 0)))
    return pl.pallas_call(
        body, name=name, grid=(R // tr,),
        in_specs=[spec, spec, spec, part_spec],
        out_specs=[spec] * 4,
        out_shape=[jax.ShapeDtypeStruct((R, C), F32)] * 4,
        compiler_params=_params(("parallel",)),
    )(w, m, v, parts)


def _pair_sum(core, grad, recv, name):
    _, _, n, C = grad.shape
    tr = n // 2

    def body(c_ref, a_ref, b_ref, o_ref):
        o_ref[...] = a_ref[...] + b_ref[...]

    spec = pl.BlockSpec((1, tr, C), lambda i, j, c_ref: (i, j, 0))
    return pl.pallas_call(
        body, name=name,
        grid_spec=pltpu.PrefetchScalarGridSpec(
            num_scalar_prefetch=1, grid=(4, n // tr),
            in_specs=[pl.BlockSpec((1, None, tr, C), lambda i, j, c_ref: (i, c_ref[0], j, 0)), spec],
            out_specs=spec),
        out_shape=jax.ShapeDtypeStruct(recv.shape, F32),
        compiler_params=_params(("parallel", "parallel")),
    )(core.reshape(1), grad, recv)


_HBM = pl.BlockSpec(memory_space=pltpu.HBM)


def _place():
    return lax.axis_index("x"), lax.axis_index("y"), lax.axis_index("c")


def _gathered_shape(shard):
    return jax.ShapeDtypeStruct((N_DEV,) + shard.shape, shard.dtype)


def _gather_sems(n):
    return [pltpu.SemaphoreType.DMA((7, n)), pltpu.SemaphoreType.DMA((7, n)), pltpu.SemaphoreType.DMA((n,))]


def _gather_phases(x_refs, out_refs, send_sems, recv_sems, local_sems):
    x, y, c = _place()
    me, sibling = (x, y, c), (x, y, 1 - c)
    chips = [(1 - x, y), (x, 1 - y), (1 - x, 1 - y)]
    arrays = range(len(x_refs))

    def slot(i, px, py, pc):
        return out_refs[i].at[4 * px + 2 * py + pc]

    def copy(i, k, block, to, own=False):
        return pltpu.make_async_remote_copy(
            src_ref=x_refs[i] if own else slot(i, *block), dst_ref=slot(i, *block),
            send_sem=send_sems.at[k, i], recv_sem=recv_sems.at[k, i], device_id=to, device_id_type=MESH)

    def mine(i):
        return pltpu.make_async_copy(x_refs[i], slot(i, *me), local_sems.at[i])

    def start():
        for i in arrays:
            mine(i).start()
            copy(i, 0, me, sibling, own=True).start()
            for j, chip in enumerate(chips):
                copy(i, 1 + j, me, (*chip, c), own=True).start()

    def forward():
        for i in arrays:
            for j, chip in enumerate(chips):
                copy(i, 1 + j, (*chip, c), me).wait_recv()
                copy(i, 4 + j, (*chip, c), sibling).start()

    def finish():
        for i in arrays:
            copy(i, 0, sibling, me).wait_recv()
            copy(i, 0, me, sibling, own=True).wait_send()
            for j, chip in enumerate(chips):
                copy(i, 4 + j, (*chip, 1 - c), me).wait_recv()
                copy(i, 1 + j, me, (*chip, c), own=True).wait_send()
                copy(i, 4 + j, (*chip, c), sibling).wait_send()
            mine(i).wait()

    return start, forward, finish


def _all_gather(shards, name):
    n = len(shards)

    def body(*refs):
        start, forward, finish = _gather_phases(refs[:n], refs[n:2 * n], *refs[2 * n:])
        start()
        forward()
        finish()

    return pl.pallas_call(
        body, name=name,
        out_shape=[_gathered_shape(s) for s in shards],
        in_specs=[_HBM] * n, out_specs=[_HBM] * n,
        scratch_shapes=_gather_sems(n),
    )(*shards)


def _sibling_exchange(grads, name):
    n = len(grads)

    def body(*refs):
        g_refs, r_refs, send_sems, recv_sems = refs[:n], refs[n:2 * n], refs[2 * n], refs[2 * n + 1]
        x, y, c = _place()
        copies = [pltpu.make_async_remote_copy(
            src_ref=g_refs[i].at[:, 1 - c], dst_ref=r_refs[i], send_sem=send_sems.at[i], recv_sem=recv_sems.at[i],
            device_id=(x, y, 1 - c), device_id_type=MESH) for i in range(n)]
        for cp in copies:
            cp.start()
        for cp in copies:
            cp.wait()

    return pl.pallas_call(
        body, name=name,
        out_shape=[jax.ShapeDtypeStruct((g.shape[0],) + g.shape[2:], g.dtype) for g in grads],
        in_specs=[_HBM] * n, out_specs=[_HBM] * n,
        scratch_shapes=[pltpu.SemaphoreType.DMA((n,)), pltpu.SemaphoreType.DMA((n,))],
    )(*grads)


def _owner_exchange_sems(n):
    return [pltpu.SemaphoreType.DMA((7, n)), pltpu.SemaphoreType.DMA((7, n)), pltpu.SemaphoreType.DMA((n,))]


def _owner_exchange_phases(g_refs, r_refs, send_sems, recv_sems, local_sems):
    x, y, c = _place()
    me = 4 * x + 2 * y + c
    flip = lambda v, bit: 1 - v if bit else v
    peers = [(flip(x, k & 4), flip(y, k & 2), flip(c, k & 1)) for k in range(1, N_DEV)]
    arrays = range(len(g_refs))

    def mine(i):
        return pltpu.make_async_copy(g_refs[i].at[me], r_refs[i].at[me], local_sems.at[i])

    def copy(i, k, src_slot, dst_slot):
        return pltpu.make_async_remote_copy(
            src_ref=g_refs[i].at[src_slot], dst_ref=r_refs[i].at[dst_slot],
            send_sem=send_sems.at[k, i], recv_sem=recv_sems.at[k, i], device_id=peers[k], device_id_type=MESH)

    def start():
        for i in arrays:
            mine(i).start()
            for k, (px, py, pc) in enumerate(peers):
                copy(i, k, 4 * px + 2 * py + pc, me).start()

    def finish():
        for i in arrays:
            for k, (px, py, pc) in enumerate(peers):
                copy(i, k, me, 4 * px + 2 * py + pc).wait_recv()
                copy(i, k, 4 * px + 2 * py + pc, me).wait_send()
            mine(i).wait()

    return start, finish


def _chip_exchange_sems(n):
    return [pltpu.SemaphoreType.DMA((3, n)), pltpu.SemaphoreType.DMA((3, n)), pltpu.SemaphoreType.DMA((n,))]


def _chip_exchange_phases(p_refs, r_refs, send_sems, recv_sems, local_sems):
    x, y, c = _place()
    my_chip = 2 * x + y
    chips = [(1 - x, y), (x, 1 - y), (1 - x, 1 - y)]
    arrays = range(len(p_refs))

    def mine(i):
        return pltpu.make_async_copy(p_refs[i].at[my_chip], r_refs[i].at[my_chip], local_sems.at[i])

    def copy(i, k, src_chip, dst_chip):
        px, py = chips[k]
        return pltpu.make_async_remote_copy(
            src_ref=p_refs[i].at[src_chip], dst_ref=r_refs[i].at[dst_chip],
            send_sem=send_sems.at[k, i], recv_sem=recv_sems.at[k, i], device_id=(px, py, c), device_id_type=MESH)

    def start():
        for i in arrays:
            mine(i).start()
            for k, (px, py) in enumerate(chips):
                copy(i, k, 2 * px + py, my_chip).start()

    def finish():
        for i in arrays:
            for k, (px, py) in enumerate(chips):
                copy(i, k, my_chip, 2 * px + py).wait_recv()
                copy(i, k, 2 * px + py, my_chip).wait_send()
            mine(i).wait()

    return start, finish


_R_IN, _R_OUT, _R_FF = IN_W // N_DEV, D_MODEL // N_DEV, D_FF // N_DEV


def _by_owner(g):
    return g.reshape(4, 2, g.shape[0] // N_DEV, D_MODEL)


def _local_step(x, tgt, small, w_in_t, rest, core=None):
    exchange = core is not None
    g1, g2, gf = small["norm1_g"], small["norm2_g"], small["final_norm_g"].reshape(1, D_MODEL)
    ga, gg = small["attn_out_g"], small["gmlp_out_g"]
    ln_g = small["sgu_ln_g"].reshape(1, GMLP_W)
    ln_b = small["sgu_ln_b"].reshape(1, GMLP_W)
    sgu_w = small["sgu_w"][0]
    sgu_bt = small["sgu_b"][0].T

    hn1, q, k, v, u, z = _proj_fwd(x, g1, w_in_t)
    attn, lse, gathered = _attn_fwd(q, k, v, shards=rest if exchange else ())
    w_out, w_ff1_t, w_ff2 = [g.reshape(-1, D_MODEL) for g in gathered] if exchange else rest
    gm = _gmlp_fwd(u, z, ln_g, ln_b, sgu_w, sgu_bt)
    mixed, h1, hn2 = _out_fwd(attn, gm, ga, gg, w_out, x, g2)
    relu, dh2f, dh2b, loss8, dgf8 = _ffn_fwd(hn2, h1, w_ff1_t, w_ff2, gf, tgt)

    da, dh1f, dh1b, dg2 = _ffn_bwd(dh2b, dh2f, relu, h1, g2, w_ff2, w_ff1_t)
    wire = BF16 if exchange else F32
    dw_ff2 = _dw(relu, dh2b, "dw_ff2", DW_TILE, square_a=True, out_dtype=wire)
    dw_ff1_t = _dw(da, hn2, "dw_ff1", DW_TILE, out_dtype=wire)
    dattn, dgm, dga, dgg = _out_bwd(dh1b, w_out, attn, gm, ga, gg)
    dw_out = _dw(mixed, dh1b, "dw_out", DW_TILE, out_dtype=wire)
    early = [dw_out, dw_ff1_t, dw_ff2]
    if exchange:
        early = [g.reshape(N_DEV, -1, D_MODEL) for g in early]
    du, dz, dlg, dlb, dsw, dsb = _gmlp_bwd(u, z, dgm, ln_g, ln_b, sgu_w, sgu_bt)
    dq, dk, dv, arrived = _attn_bwd(q, k, v, dattn, attn, lse, owner_grads=early if exchange else ())
    dproj = _dproj_assemble(dq, dk, dv, du, dz)
    dw_in_t = _dw(dproj, hn1, "dw_in", DW_TILE_IN)
    late = ()
    if exchange:
        by_owner = _by_owner(dw_in_t)
        got, = _sibling_exchange([by_owner], "grad_sibling_exchange")
        late = (_pair_sum(core, by_owner, got, "grad_pair_sum"),)
    dx, dg1, late = _proj_bwd(dproj, w_in_t, x, g1, dh1f, chip_sums=late)
    if exchange:
        dw_in_t, early = late[0], arrived

    small_grads = dict(
        norm1_g=dg1[0], sgu_ln_g=dlg[0], sgu_ln_b=dlb[0], sgu_w=dsw, sgu_b=dsb[:, :N_GROUPS].T,
        attn_out_g=dga[0], gmlp_out_g=dgg[0], norm2_g=dg2[0], final_norm_g=dgf8[0])
    return loss8[0, 0], dx, (dw_in_t, *early), small_grads


SMALL_NAMES = ("norm1_g", "sgu_ln_g", "sgu_ln_b", "sgu_w", "sgu_b", "attn_out_g", "gmlp_out_g", "norm2_g",
               "final_norm_g")
WEIGHT_ORDER = ("norm1_g", "w_in", "sgu_ln_g", "sgu_ln_b", "sgu_w", "sgu_b", "attn_out_g", "gmlp_out_g", "w_out",
                "norm2_g", "w_ff1", "w_ff2", "final_norm_g")


def _pack_small(d):
    return jnp.concatenate([d[n].reshape(-1, LANES) for n in SMALL_NAMES], axis=0)


def _unpack_small(p, like):
    out, r = {}, 0
    for n in SMALL_NAMES:
        rows = like[n].size // LANES
        out[n] = p[r:r + rows].reshape(like[n].shape)
        r += rows
    return out


def kernel(x, norm1_g, w_in, sgu_ln_g, sgu_ln_b, sgu_w, sgu_b, attn_out_g, gmlp_out_g, w_out, norm2_g, w_ff1, w_ff2, final_norm_g, loss_target, m_norm1_g, m_w_in, m_sgu_ln_g, m_sgu_ln_b, m_sgu_w, m_sgu_b, m_attn_out_g, m_gmlp_out_g, m_w_out, m_norm2_g, m_w_ff1, m_w_ff2, m_final_norm_g, v_norm1_g, v_w_in, v_sgu_ln_g, v_sgu_ln_b, v_sgu_w, v_sgu_b, v_attn_out_g, v_gmlp_out_g, v_w_out, v_norm2_g, v_w_ff1, v_w_ff2, v_final_norm_g):
    w = dict(norm1_g=norm1_g, w_in=w_in, sgu_ln_g=sgu_ln_g, sgu_ln_b=sgu_ln_b, sgu_w=sgu_w, sgu_b=sgu_b,
             attn_out_g=attn_out_g, gmlp_out_g=gmlp_out_g, w_out=w_out, norm2_g=norm2_g, w_ff1=w_ff1, w_ff2=w_ff2,
             final_norm_g=final_norm_g)
    m = dict(norm1_g=m_norm1_g, w_in=m_w_in, sgu_ln_g=m_sgu_ln_g, sgu_ln_b=m_sgu_ln_b, sgu_w=m_sgu_w, sgu_b=m_sgu_b,
             attn_out_g=m_attn_out_g, gmlp_out_g=m_gmlp_out_g, w_out=m_w_out, norm2_g=m_norm2_g, w_ff1=m_w_ff1,
             w_ff2=m_w_ff2, final_norm_g=m_final_norm_g)
    v = dict(norm1_g=v_norm1_g, w_in=v_w_in, sgu_ln_g=v_sgu_ln_g, sgu_ln_b=v_sgu_ln_b, sgu_w=v_sgu_w, sgu_b=v_sgu_b,
             attn_out_g=v_attn_out_g, gmlp_out_g=v_gmlp_out_g, w_out=v_w_out, norm2_g=v_norm2_g, w_ff1=v_w_ff1,
             w_ff2=v_w_ff2, final_norm_g=v_final_norm_g)
    big = ("w_in", "w_out", "w_ff1", "w_ff2")
    core = lax.axis_index("c")

    w_in_t, = _all_gather([w_in[0].T.astype(BF16)], "w_in_all_gather")
    rest = (w_out[0].astype(BF16), w_ff1[0].T.astype(BF16), w_ff2[0].astype(BF16))
    loss, dx, parts, small_grads = _local_step(x[0], loss_target[0], {n: w[n] for n in SMALL_NAMES},
                                               w_in_t.reshape(IN_W, D_MODEL), rest, core=core)
    loss = lax.psum(loss, ("x", "y", "c"))

    new = {}
    for n, p, transposed, tr in zip(big, parts, (True, False, True, False), (128, 128, 128, 256)):
        new[n] = [a[None] for a in _adamw(w[n][0], m[n][0], v[n][0], p, "adamw_" + n, tr, transposed)]

    small_parts, = _all_gather([_pack_small(small_grads)], "small_grad_all_gather")
    packed = _adamw(_pack_small({n: w[n] for n in SMALL_NAMES}), _pack_small({n: m[n] for n in SMALL_NAMES}),
                    _pack_small({n: v[n] for n in SMALL_NAMES}), small_parts, "adamw_small", SMALL_ROWS)

    outs = []
    for i, ps in enumerate(packed):
        d = {n: new[n][i] for n in big}
        d.update(_unpack_small(ps, w))
        outs.extend(d[n] for n in WEIGHT_ORDER)
    return (loss, dx[None], *outs)
```

```python
import functools
import math

import numpy as np
import jax
import jax.numpy as jnp
from jax import lax
from jax.experimental import pallas as pl
from jax.experimental.pallas import tpu as pltpu

F32 = jnp.float32
BF16 = jnp.bfloat16

D_MODEL = 1024
HEAD_DIM = 64
N_HEADS = 12
ATTN_W = N_HEADS * HEAD_DIM
N_GROUPS = 4
GMLP_W = N_GROUPS * HEAD_DIM
IN_W = 3 * ATTN_W + 2 * GMLP_W
D_FF = 4 * D_MODEL
CHUNK = 128
DILATIONS = (1, 4, 16)
EPS = 1e-6
Q_SCALE = HEAD_DIM ** -0.5
NEG = -1e30

ADAM_LR, ADAM_B1, ADAM_B2, ADAM_EPS, ADAM_WD, ADAM_STEP = 0.001, 0.9, 0.999, 1e-08, 0.01, 10

N_DEV = 8
LANES = 128
VMEM_LIMIT = 56 << 20
SMALL_ROWS = 552

TM_PROJ = 512
TM_FFN = 512
FF_CHUNK = 512
DW_TILE = (512, 1024, 4096)
DW_TILE_IN = (IN_W // 2, 1024, 2048)

MESH = pl.DeviceIdType.MESH


def _alibi_slopes(n):
    def pow2(m):
        start = 2.0 ** (-8.0 / m)
        return [start ** (i + 1) for i in range(m)]
    c = 2 ** int(math.floor(math.log2(n)))
    s = pow2(n) if c == n else pow2(c) + pow2(2 * c)[0::2][: n - c]
    return np.asarray(s, dtype=np.float32)


SLOPES = _alibi_slopes(N_HEADS)


def _params(sem=None):
    kw = dict(vmem_limit_bytes=VMEM_LIMIT)
    if sem is not None:
        kw["dimension_semantics"] = sem
    return pltpu.CompilerParams(**kw)


def _rows(tm, n):
    return pl.BlockSpec((tm, n), lambda i: (i, 0))


def _resident(shape):
    return pl.BlockSpec(shape, lambda *_: (0,) * len(shape), pipeline_mode=pl.Buffered(1))


def _rms(x):
    r = lax.rsqrt(jnp.mean(x * x, axis=-1, keepdims=True) + EPS)
    return x * r, r


def _rms_bwd(n, r, g, dy):
    dn = dy * g
    return r * (dn - n * jnp.mean(dn * n, axis=-1, keepdims=True))


def _accum_rows(acc_ref, v):
    acc_ref[...] += jnp.broadcast_to(jnp.sum(v, axis=0, keepdims=True), acc_ref.shape)


_G0 = math.sqrt(2.0 / math.pi)
_G1 = 0.044715


def _gelu(x):
    t = jnp.tanh(_G0 * (x + _G1 * (x * x * x)))
    return x * (0.5 * (1.0 + t)), t


def _gelu_grad(x, t):
    return 0.5 * (1.0 + t) + 0.5 * x * (1.0 - t * t) * (_G0 * (1.0 + 3.0 * _G1 * x * x))


NT = (((1,), (1,)), ((), ()))
TN = (((0,), (0,)), ((), ()))


def _dot(a, b, dims=None):
    if dims is None:
        return jnp.dot(a, b, preferred_element_type=F32)
    return lax.dot_general(a, b, dims, preferred_element_type=F32)


def _proj_fwd(x, g1, w_in_t):
    T = x.shape[0]
    tm = TM_PROJ

    def body(x_ref, g_ref, w_ref, hn_ref, q_ref, k_ref, v_ref, u_ref, z_ref):
        n, _ = _rms(x_ref[...])
        hn = (n * g_ref[...]).astype(BF16)
        hn_ref[...] = hn
        a = ATTN_W
        q_ref[...] = _dot(hn, w_ref[0:a, :], NT) * Q_SCALE
        k_ref[...] = _dot(hn, w_ref[a:2 * a, :], NT)
        v_ref[...] = _dot(hn, w_ref[2 * a:3 * a, :], NT)
        u_ref[...] = _dot(hn, w_ref[3 * a:3 * a + GMLP_W, :], NT)
        z_ref[...] = _dot(hn, w_ref[3 * a + GMLP_W:, :], NT)

    sds = jax.ShapeDtypeStruct
    return pl.pallas_call(
        body, name="proj_fwd", grid=(T // tm,),
        in_specs=[_rows(tm, D_MODEL), _resident((1, D_MODEL)), _resident((IN_W, D_MODEL))],
        out_specs=[_rows(tm, D_MODEL), _rows(tm, ATTN_W), _rows(tm, ATTN_W), _rows(tm, ATTN_W),
                   _rows(tm, GMLP_W), _rows(tm, GMLP_W)],
        out_shape=[sds((T, D_MODEL), BF16), sds((T, ATTN_W), F32), sds((T, ATTN_W), F32),
                   sds((T, ATTN_W), F32), sds((T, GMLP_W), F32), sds((T, GMLP_W), F32)],
        compiler_params=_params(("parallel",)),
    )(x, g1, w_in_t)


ATT_TILE = 2048
ATT_BLOCKS = ATT_TILE // CHUNK


def _slope_table():
    row = np.repeat(SLOPES, HEAD_DIM)
    return jnp.asarray(np.broadcast_to(row[None], (8, ATTN_W)), F32)


def _stacked_consts(sl_ref):
    shape = (2 * CHUNK, 2 * CHUNK)
    row = lax.broadcasted_iota(jnp.int32, shape, 0)
    kj = lax.broadcasted_iota(jnp.int32, shape, 1)
    steps = (row & (CHUNK - 1)) + CHUNK - kj
    band = (steps >= 0) & (steps <= CHUNK)
    sl = sl_ref[0:1, :]
    upper = lax.broadcasted_iota(jnp.int32, (2 * CHUNK, 1), 0) < CHUNK
    slope2 = jnp.where(upper, sl[:, 0:1], sl[:, HEAD_DIM:HEAD_DIM + 1])
    return kj, band, slope2 * steps.astype(F32)


def _residue_view(a):
    return a.reshape(a.shape[0] // ATT_BLOCKS, ATT_BLOCKS, a.shape[1])


def _tile_copies(hbm, buf, sem, hp, t, to_hbm=False):
    rows = pl.ds(pl.multiple_of(t * CHUNK, CHUNK), CHUNK)
    lanes = pl.ds(pl.multiple_of(hp * LANES, LANES), LANES)
    pairs = [(hbm.at[rows, r, lanes], buf.at[r]) for r in range(ATT_BLOCKS)]
    return [pltpu.make_async_copy(v, h, sem) if to_hbm else pltpu.make_async_copy(h, v, sem) for h, v in pairs]


def _wait_tile(buf, sem):
    pltpu.make_async_copy(buf, buf, sem).wait()


def _residue_rows(d, j):
    if d == 16:
        return [(j, 0, CHUNK)]
    if d == 4:
        return [(j % 4 + 4 * m, 32 * (j // 4), 32) for m in range(4)]
    return [(r, 8 * j, 8) for r in range(ATT_BLOCKS)]


def _block_order(p, d):
    if d == 16:
        return p
    if d == 4:
        return 4 * (p & 31) + (p >> 5)
    return 16 * (p & 7) + (p >> 3)


def _first_in_tile(d, j):
    return _residue_rows(d, j)[0][1] == 0


def _rm_block(buf, d, j):
    return jnp.concatenate([buf[r, lo:lo + n, :] for r, lo, n in _residue_rows(d, j)], axis=0)


def _rm_block_before(buf, buf_before, d, j):
    if _first_in_tile(d, j):
        return jnp.concatenate([buf_before[r, CHUNK - n:CHUNK, :] for r, _, n in _residue_rows(d, j)], axis=0)
    return jnp.concatenate([buf[r, lo - n:lo, :] for r, lo, n in _residue_rows(d, j)], axis=0)


def _rm_store(buf, d, j, val):
    at = 0
    for r, lo, n in _residue_rows(d, j):
        buf[r, lo:lo + n, :] = val[at:at + n, :]
        at += n


def _rm_add(buf, rows, val):
    at = 0
    for r, lo, n in rows:
        buf[r, lo:lo + n, :] += val[at:at + n, :]
        at += n


def _residue_bias(sl_ref, d):
    shape = (2 * CHUNK, 2 * CHUNK)
    row = lax.broadcasted_iota(jnp.int32, shape, 0)
    col = lax.broadcasted_iota(jnp.int32, shape, 1)
    steps = _block_order(row & (CHUNK - 1), d) + CHUNK - (_block_order(col & (CHUNK - 1), d) + (col & CHUNK))
    band = (steps >= 0) & (steps <= CHUNK)
    sl = sl_ref[0:1, :]
    upper = lax.broadcasted_iota(jnp.int32, (2 * CHUNK, 1), 0) < CHUNK
    slope2 = jnp.where(upper, sl[:, 0:1], sl[:, HEAD_DIM:HEAD_DIM + 1])
    return jnp.where(band, -(float(d) * slope2 * steps.astype(F32)), NEG)


def _block_rows(j, d):
    if d == 1:
        start = j * CHUNK
        return j, start, pl.ds(start, CHUNK)
    r, b = j % d, j // d
    start = r + (d * CHUNK) * b
    return b, start, pl.ds(start, CHUNK, stride=d)


def _prev_rows(j, d):
    _, start, _ = _block_rows(j, d)
    if d == 1:
        return pl.ds(start - CHUNK, CHUNK), pl.ds(ATT_TILE - CHUNK, CHUNK)
    last = j % d + (d * CHUNK) * (ATT_BLOCKS // d - 1)
    return pl.ds(start - d * CHUNK, CHUNK, stride=d), pl.ds(last, CHUNK, stride=d)


def _kv_block(j, d, rows, kc_ref, kp_ref, vc_ref, vp_ref):
    b = j // d
    here, before = _prev_rows(j, d)
    if ATT_BLOCKS // d == 1:
        kp, vp = kp_ref[before, :], vp_ref[before, :]
    else:
        src_k, src_v, src_rows = (kp_ref, vp_ref, before) if b == 0 else (kc_ref, vc_ref, here)
        kp, vp = src_k[src_rows, :], src_v[src_rows, :]
    kcat = jnp.concatenate([kp, kc_ref[rows, :]], axis=0).astype(BF16)
    vcat = jnp.concatenate([vp, vc_ref[rows, :]], axis=0).astype(BF16)
    return kcat, vcat


def _stack_heads(xb, head0):
    zero = jnp.zeros_like(xb)
    return jnp.concatenate([jnp.where(head0, xb, zero), jnp.where(head0, zero, xb)], axis=0).astype(BF16)


def _unstack_heads(x2, head0):
    return jnp.where(head0, x2[:CHUNK, :], x2[CHUNK:, :])


def _scores(q2, kcat, bias, kj, first):
    s = _dot(q2, kcat, NT) + bias
    return jnp.where(kj < jnp.where(first, CHUNK, 0), NEG, s)


def _attn_tile_specs(nt, lag):
    clamp = (lambda t: jnp.minimum(t, nt - 1)) if lag else (lambda t: t)
    cur = pl.BlockSpec((ATT_TILE, LANES), lambda c, t: (clamp(t), c))
    prev = pl.BlockSpec((ATT_TILE, LANES), lambda c, t: (jnp.maximum(clamp(t) - 1, 0), c))
    slope = pl.BlockSpec((8, LANES), lambda c, t: (0, c))
    return cur, prev, slope


def _attn_fwd(q, k, v, shards=()):
    T = q.shape[0]
    nt = T // ATT_TILE
    ns = len(shards)
    steps = (ATTN_W // LANES) * nt

    def body(sl_ref, q_hbm, k_hbm, v_hbm, *rest):
        x_refs, rest = rest[:ns], rest[ns:]
        attn_hbm, lse_hbm = rest[:2]
        g_refs, rest = rest[2:2 + ns], rest[2 + ns:]
        qbuf, kbuf, vbuf, obuf, lbuf = rest[:5]
        o_acc, l_acc = rest[5:8], rest[8:11]
        sem_q, sem_k, sem_v, sem_o, sem_l = rest[11:16]
        hp, t = pl.program_id(0), pl.program_id(1)
        step = hp * nt + t
        two, three = step % 2, step % 3
        before, after = (step + 2) % 3, (step + 1) % 3
        if ns:
            start, forward, finish = _gather_phases(x_refs, g_refs, *rest[16:])
            pl.when(step == 0)(start)
            pl.when(step == steps // 2)(forward)

        def fetch(hp_, t_, two_, three_):
            for cp in (_tile_copies(q_hbm, qbuf.at[two_], sem_q.at[two_], hp_, t_)
                       + _tile_copies(k_hbm, kbuf.at[three_], sem_k.at[three_], hp_, t_)
                       + _tile_copies(v_hbm, vbuf.at[three_], sem_v.at[three_], hp_, t_)):
                cp.start()

        @pl.when(step == 0)
        def _():
            kbuf[2] = jnp.zeros((ATT_BLOCKS, CHUNK, LANES), F32)
            vbuf[2] = jnp.zeros((ATT_BLOCKS, CHUNK, LANES), F32)
            fetch(0, 0, 0, 0)

        @pl.when(step + 1 < steps)
        def _():
            fetch((step + 1) // nt, (step + 1) % nt, 1 - two, after)

        _wait_tile(qbuf.at[two], sem_q.at[two])
        _wait_tile(kbuf.at[three], sem_k.at[three])
        _wait_tile(vbuf.at[three], sem_v.at[three])

        @pl.when(step >= 2)
        def _():
            _wait_tile(obuf.at[two], sem_o.at[two])
            _wait_tile(lbuf.at[two], sem_l.at[two])

        q_t, k_t, v_t = qbuf.at[two], kbuf.at[three], vbuf.at[three]
        k_b, v_b = kbuf.at[before], vbuf.at[before]
        head0 = lax.broadcasted_iota(jnp.int32, (CHUNK, LANES), 1) < HEAD_DIM
        no_key_before = jnp.where(lax.broadcasted_iota(jnp.int32, (2 * CHUNK, 2 * CHUNK), 1) < CHUNK, NEG, 0.0)
        for pi, d in enumerate(DILATIONS):
            bias = _residue_bias(sl_ref, d)
            for j in range(ATT_BLOCKS):
                kcat = jnp.concatenate([_rm_block_before(k_t, k_b, d, j), _rm_block(k_t, d, j)], axis=0).astype(BF16)
                vcat = jnp.concatenate([_rm_block_before(v_t, v_b, d, j), _rm_block(v_t, d, j)], axis=0).astype(BF16)
                s = _dot(_stack_heads(_rm_block(q_t, d, j), head0), kcat, NT) + bias
                if _first_in_tile(d, j):
                    s = s + jnp.where(t == 0, 1.0, 0.0) * no_key_before
                m = jnp.max(s, axis=-1, keepdims=True)
                p = jnp.exp(s - m)
                l = jnp.sum(p, axis=-1, keepdims=True)
                _rm_store(o_acc[pi], d, j, _unstack_heads(_dot(p.astype(BF16), vcat) * (1.0 / l), head0))
                _rm_store(l_acc[pi], d, j,
                          _unstack_heads(jnp.broadcast_to(m + jnp.log(l), (2 * CHUNK, LANES)), head0))

        for r in range(ATT_BLOCKS):
            a, b, c = l_acc[0][r], l_acc[1][r], l_acc[2][r]
            m = jnp.maximum(jnp.maximum(a, b), c)
            ea, eb, ec = jnp.exp(a - m), jnp.exp(b - m), jnp.exp(c - m)
            tot = ea + eb + ec
            obuf[two, r] = (ea * o_acc[0][r] + eb * o_acc[1][r] + ec * o_acc[2][r]) / tot
            lbuf[two, r] = m + jnp.log(tot)

        for cp in (_tile_copies(attn_hbm, obuf.at[two], sem_o.at[two], hp, t, to_hbm=True)
                   + _tile_copies(lse_hbm, lbuf.at[two], sem_l.at[two], hp, t, to_hbm=True)):
            cp.start()

        @pl.when(step == steps - 1)
        def _():
            for slot in (two, 1 - two)[:min(steps, 2)]:
                _wait_tile(obuf.at[slot], sem_o.at[slot])
                _wait_tile(lbuf.at[slot], sem_l.at[slot])

        if ns:
            pl.when(step == steps - 1)(finish)

    tile = lambda n: pltpu.VMEM((n, ATT_BLOCKS, CHUNK, LANES), F32)
    dma = lambda n: pltpu.SemaphoreType.DMA((n,))
    view = jax.ShapeDtypeStruct((T // ATT_BLOCKS, ATT_BLOCKS, ATTN_W), F32)
    outs = pl.pallas_call(
        body, name="attn_fwd", grid=(ATTN_W // LANES, nt),
        in_specs=[pl.BlockSpec((8, LANES), lambda c, t: (0, c))] + [_HBM] * (3 + ns),
        out_specs=[_HBM] * (2 + ns),
        out_shape=[view, view] + [_gathered_shape(s) for s in shards],
        scratch_shapes=[tile(2), tile(3), tile(3), tile(2), tile(2)] + [pltpu.VMEM((ATT_BLOCKS, CHUNK, LANES), F32)] * 6
        + [dma(2), dma(3), dma(3), dma(2), dma(2)] + (_gather_sems(ns) if ns else []),
        compiler_params=_params(("arbitrary", "arbitrary")),
    )(_slope_table(), _residue_view(q), _residue_view(k), _residue_view(v), *shards)
    return outs[0].reshape(T, ATTN_W), outs[1].reshape(T, ATTN_W), tuple(outs[2:])


def _group_mean(v, grp):
    out = jnp.zeros_like(v)
    for g in range(N_GROUPS):
        mk = grp == g
        s = jnp.sum(jnp.where(mk, v, 0.0), axis=-1, keepdims=True) * (1.0 / HEAD_DIM)
        out = jnp.where(mk, s, out)
    return out


def _gmlp_core(uu, zz, lg, lb, ws, sb_ref, grp):
    ug, tu = _gelu(uu)
    zg, tz = _gelu(zz)
    zc = zg - _group_mean(zg, grp)
    rstd = lax.rsqrt(_group_mean(zc * zc, grp) + EPS)
    xhat = zc * rstd
    zn16 = (xhat * lg + lb).astype(BF16)
    mixed = jnp.zeros_like(uu)
    for g in range(N_GROUPS):
        mixed = jnp.where(grp == g, _dot(ws[g], zn16) + sb_ref[:, g:g + 1], mixed)
    return ug, tu, tz, xhat, rstd, zn16, mixed


def _causal_ws(w_ref):
    ti = lax.broadcasted_iota(jnp.int32, (CHUNK, CHUNK), 0)
    si = lax.broadcasted_iota(jnp.int32, (CHUNK, CHUNK), 1)
    causal = si <= ti
    return causal, [jnp.where(causal, w_ref[g], 0.0).astype(BF16) for g in range(N_GROUPS)]


def _gmlp_fwd(u, z, ln_g, ln_b, sgu_w, sgu_bt):
    T = u.shape[0]
    tg = 512

    def body(u_ref, z_ref, g_ref, b_ref, w_ref, sb_ref, out_ref):
        grp = lax.broadcasted_iota(jnp.int32, (CHUNK, GMLP_W), 1) // HEAD_DIM
        _, ws = _causal_ws(w_ref)
        for ci in range(tg // CHUNK):
            rows = slice(ci * CHUNK, (ci + 1) * CHUNK)
            ug, _, _, _, _, _, mixed = _gmlp_core(u_ref[rows, :], z_ref[rows, :], g_ref[...], b_ref[...],
                                                  ws, sb_ref, grp)
            out_ref[rows, :] = ug * mixed

    return pl.pallas_call(
        body, name="gmlp_fwd", grid=(T // tg,),
        in_specs=[_rows(tg, GMLP_W), _rows(tg, GMLP_W), _resident((1, GMLP_W)), _resident((1, GMLP_W)),
                  _resident((N_GROUPS, CHUNK, CHUNK)), _resident((CHUNK, N_GROUPS))],
        out_specs=_rows(tg, GMLP_W),
        out_shape=jax.ShapeDtypeStruct((T, GMLP_W), F32),
        compiler_params=_params(("parallel",)),
    )(u, z, ln_g, ln_b, sgu_w, sgu_bt)


def _out_fwd(attn, gm, ga, gg, w_out, x, g2):
    T = x.shape[0]
    tm = TM_PROJ

    def body(a_ref, m_ref, ga_ref, gg_ref, w_ref, x_ref, g2_ref, mix_ref, h1_ref, hn2_ref):
        an, _ = _rms(a_ref[...])
        gn, _ = _rms(m_ref[...])
        an = (an * ga_ref[...]).astype(BF16)
        gn = (gn * gg_ref[...]).astype(BF16)
        mix_ref[:, 0:ATTN_W] = an
        mix_ref[:, ATTN_W:] = gn
        h1 = x_ref[...] + _dot(an, w_ref[0:ATTN_W, :]) + _dot(gn, w_ref[ATTN_W:, :])
        h1_ref[...] = h1
        n2, _ = _rms(h1)
        hn2_ref[...] = (n2 * g2_ref[...]).astype(BF16)

    sds = jax.ShapeDtypeStruct
    return pl.pallas_call(
        body, name="out_fwd", grid=(T // tm,),
        in_specs=[_rows(tm, ATTN_W), _rows(tm, GMLP_W), _resident((1, ATTN_W)), _resident((1, GMLP_W)),
                  _resident((D_MODEL, D_MODEL)), _rows(tm, D_MODEL), _resident((1, D_MODEL))],
        out_specs=[_rows(tm, D_MODEL)] * 3,
        out_shape=[sds((T, D_MODEL), BF16), sds((T, D_MODEL), F32), sds((T, D_MODEL), BF16)],
        compiler_params=_params(("parallel",)),
    )(attn, gm, ga, gg, w_out, x, g2)


def _ffn_fwd(hn2, h1, w1t, w2, gf, tgt):
    T = h1.shape[0]
    tm = TM_FFN

    def body(hn_ref, h1_ref, w1_ref, w2_ref, gf_ref, t_ref, r_ref, dhf_ref, dhb_ref, loss_ref, dgf_ref):
        i = pl.program_id(0)

        @pl.when(i == 0)
        def _():
            loss_ref[...] = jnp.zeros_like(loss_ref)
            dgf_ref[...] = jnp.zeros_like(dgf_ref)

        hn = hn_ref[...]
        acc = h1_ref[...]
        for j in range(D_FF // FF_CHUNK):
            cols = slice(j * FF_CHUNK, (j + 1) * FF_CHUNK)
            r = jnp.maximum(_dot(hn, w1_ref[cols, :], NT), 0.0)
            r_ref[:, cols] = r.astype(BF16)
            act = jnp.square(r).astype(BF16)
            acc = acc + _dot(act, w2_ref[cols, :])
        n3, r3 = _rms(acc)
        gf_row = gf_ref[...]
        e = n3 * gf_row - t_ref[...]
        loss_ref[...] += 0.5 * jnp.sum(jnp.mean(e * e, axis=-1, keepdims=True))
        dy = e * (1.0 / D_MODEL)
        _accum_rows(dgf_ref, dy * n3)
        dh2 = _rms_bwd(n3, r3, gf_row, dy)
        dhf_ref[...] = dh2
        dhb_ref[...] = dh2.astype(BF16)

    sds = jax.ShapeDtypeStruct
    acc_spec = lambda n: pl.BlockSpec((8, n), lambda i: (0, 0))
    return pl.pallas_call(
        body, name="ffn_fwd", grid=(T // tm,),
        in_specs=[_rows(tm, D_MODEL), _rows(tm, D_MODEL), _resident((D_FF, D_MODEL)), _resident((D_FF, D_MODEL)),
                  _resident((1, D_MODEL)), _rows(tm, D_MODEL)],
        out_specs=[_rows(tm, D_FF), _rows(tm, D_MODEL), _rows(tm, D_MODEL), acc_spec(LANES), acc_spec(D_MODEL)],
        out_shape=[sds((T, D_FF), BF16), sds((T, D_MODEL), F32), sds((T, D_MODEL), BF16),
                   sds((8, LANES), F32), sds((8, D_MODEL), F32)],
        compiler_params=_params(("arbitrary",)),
    )(hn2, h1, w1t, w2, gf, tgt)


def _ffn_bwd(dh2b, dh2f, relu, h1, g2, w2, w1t):
    T = h1.shape[0]
    tm = TM_FFN

    def body(db_ref, df_ref, r_ref, h1_ref, g2_ref, w2_ref, w1t_ref, da_ref, d1f_ref, d1b_ref, dg_ref):
        @pl.when(pl.program_id(0) == 0)
        def _():
            dg_ref[...] = jnp.zeros_like(dg_ref)

        db = db_ref[...]
        acc = jnp.zeros((tm, D_MODEL), F32)
        for j in range(D_FF // FF_CHUNK):
            cols = slice(j * FF_CHUNK, (j + 1) * FF_CHUNK)
            da = (_dot(db, w2_ref[cols, :], NT) * (2.0 * r_ref[:, cols].astype(F32))).astype(BF16)
            da_ref[:, cols] = da
            acc = acc + _dot(da, w1t_ref[cols, :])
        n2, r2 = _rms(h1_ref[...])
        _accum_rows(dg_ref, acc * n2)
        dh1 = df_ref[...] + _rms_bwd(n2, r2, g2_ref[...], acc)
        d1f_ref[...] = dh1
        d1b_ref[...] = dh1.astype(BF16)

    sds = jax.ShapeDtypeStruct
    return pl.pallas_call(
        body, name="ffn_bwd", grid=(T // tm,),
        in_specs=[_rows(tm, D_MODEL), _rows(tm, D_MODEL), _rows(tm, D_FF), _rows(tm, D_MODEL),
                  _resident((1, D_MODEL)), _resident((D_FF, D_MODEL)), _resident((D_FF, D_MODEL))],
        out_specs=[_rows(tm, D_FF), _rows(tm, D_MODEL), _rows(tm, D_MODEL),
                   pl.BlockSpec((8, D_MODEL), lambda i: (0, 0))],
        out_shape=[sds((T, D_FF), BF16), sds((T, D_MODEL), F32), sds((T, D_MODEL), BF16), sds((8, D_MODEL), F32)],
        compiler_params=_params(("arbitrary",)),
    )(dh2b, dh2f, relu, h1, g2, w2, w1t)


def _out_bwd(dh1b, w_out, attn, gm, ga, gg):
    T = attn.shape[0]
    tm = TM_PROJ

    def body(d_ref, w_ref, a_ref, m_ref, ga_ref, gg_ref, da_ref, dm_ref, dga_ref, dgg_ref):
        @pl.when(pl.program_id(0) == 0)
        def _():
            dga_ref[...] = jnp.zeros_like(dga_ref)
            dgg_ref[...] = jnp.zeros_like(dgg_ref)

        d = d_ref[...]
        dan = _dot(d, w_ref[0:ATTN_W, :], NT)
        dgn = _dot(d, w_ref[ATTN_W:, :], NT)
        na, ra = _rms(a_ref[...])
        ng, rg = _rms(m_ref[...])
        _accum_rows(dga_ref, dan * na)
        _accum_rows(dgg_ref, dgn * ng)
        da_ref[...] = _rms_bwd(na, ra, ga_ref[...], dan)
        dm_ref[...] = _rms_bwd(ng, rg, gg_ref[...], dgn)

    sds = jax.ShapeDtypeStruct
    return pl.pallas_call(
        body, name="out_bwd", grid=(T // tm,),
        in_specs=[_rows(tm, D_MODEL), _resident((D_MODEL, D_MODEL)), _rows(tm, ATTN_W), _rows(tm, GMLP_W),
                  _resident((1, ATTN_W)), _resident((1, GMLP_W))],
        out_specs=[_rows(tm, ATTN_W), _rows(tm, GMLP_W), pl.BlockSpec((8, ATTN_W), lambda i: (0, 0)),
                   pl.BlockSpec((8, GMLP_W), lambda i: (0, 0))],
        out_shape=[sds((T, ATTN_W), F32), sds((T, GMLP_W), F32), sds((8, ATTN_W), F32), sds((8, GMLP_W), F32)],
        compiler_params=_params(("arbitrary",)),
    )(dh1b, w_out, attn, gm, ga, gg)


def _gmlp_bwd(u, z, dgm, ln_g, ln_b, sgu_w, sgu_bt):
    T = u.shape[0]
    tg = 512
    nsteps = T // tg

    def body(u_ref, z_ref, d_ref, g_ref, b_ref, w_ref, sb_ref, du_ref, dz_ref, dlg_ref, dlb_ref, dw_ref, dsb_ref):
        i = pl.program_id(0)

        @pl.when(i == 0)
        def _():
            for ref in (dlg_ref, dlb_ref, dw_ref, dsb_ref):
                ref[...] = jnp.zeros_like(ref)

        grp = lax.broadcasted_iota(jnp.int32, (CHUNK, GMLP_W), 1) // HEAD_DIM
        lane = lax.broadcasted_iota(jnp.int32, (CHUNK, LANES), 1)
        causal, ws = _causal_ws(w_ref)
        lg = g_ref[...]
        for ci in range(tg // CHUNK):
            rows = slice(ci * CHUNK, (ci + 1) * CHUNK)
            uu, zz = u_ref[rows, :], z_ref[rows, :]
            ug, tu, tz, xhat, rstd, zn16, mixed = _gmlp_core(uu, zz, lg, b_ref[...], ws, sb_ref, grp)
            dgm_c = d_ref[rows, :]
            dmx = dgm_c * ug
            du_ref[rows, :] = dgm_c * mixed * _gelu_grad(uu, tu)
            dmx16 = dmx.astype(BF16)
            dzn = jnp.zeros_like(dmx)
            dsb = jnp.zeros((CHUNK, LANES), F32)
            for g in range(N_GROUPS):
                mk = grp == g
                dzn = jnp.where(mk, _dot(ws[g], dmx16, TN), dzn)
                dw_ref[g] += _dot(jnp.where(mk, dmx16, jnp.zeros_like(dmx16)), zn16, NT)
                dsb = jnp.where(lane == g, jnp.sum(jnp.where(mk, dmx, 0.0), axis=-1, keepdims=True), dsb)
            dsb_ref[...] += dsb
            _accum_rows(dlg_ref, dzn * xhat)
            _accum_rows(dlb_ref, dzn)
            dxh = dzn * lg
            dzg = rstd * (dxh - _group_mean(dxh, grp) - xhat * _group_mean(dxh * xhat, grp))
            dz_ref[rows, :] = dzg * _gelu_grad(zz, tz)

        @pl.when(i == nsteps - 1)
        def _():
            for g in range(N_GROUPS):
                dw_ref[g] = jnp.where(causal, dw_ref[g], 0.0)

    sds = jax.ShapeDtypeStruct
    return pl.pallas_call(
        body, name="gmlp_bwd", grid=(nsteps,),
        in_specs=[_rows(tg, GMLP_W)] * 3 + [_resident((1, GMLP_W)), _resident((1, GMLP_W)),
                                              _resident((N_GROUPS, CHUNK, CHUNK)), _resident((CHUNK, N_GROUPS))],
        out_specs=[_rows(tg, GMLP_W), _rows(tg, GMLP_W), pl.BlockSpec((8, GMLP_W), lambda i: (0, 0)),
                   pl.BlockSpec((8, GMLP_W), lambda i: (0, 0)),
                   pl.BlockSpec((N_GROUPS, CHUNK, CHUNK), lambda i: (0, 0, 0)),
                   pl.BlockSpec((CHUNK, LANES), lambda i: (0, 0))],
        out_shape=[sds((T, GMLP_W), F32), sds((T, GMLP_W), F32), sds((8, GMLP_W), F32), sds((8, GMLP_W), F32),
                   sds((N_GROUPS, CHUNK, CHUNK), F32), sds((CHUNK, LANES), F32)],
        compiler_params=_params(("arbitrary",)),
    )(u, z, dgm, ln_g, ln_b, sgu_w, sgu_bt)


def _attn_bwd_strided(q, k, v, dattn, attn, lse, owner_grads=()):
    T = q.shape[0]
    nt = T // ATT_TILE
    ns = len(owner_grads)
    steps = (ATTN_W // LANES) * (nt + 1)

    def body(sl_ref, q_ref, kc_ref, kp_ref, vc_ref, vp_ref, do_ref, o_ref, lse_ref, *rest):
        p_refs, rest = rest[:ns], rest[ns:]
        dq_ref, dk_ref, dv_ref = rest[:3]
        r_refs, rest = rest[3:3 + ns], rest[3 + ns:]
        dk_acc, dv_acc, delta_s = rest[:3]
        t = pl.program_id(1)
        if ns:
            step = pl.program_id(0) * (nt + 1) + t
            start, finish = _owner_exchange_phases(p_refs, r_refs, *rest[3:])
            pl.when(step == 0)(start)
        s_cur = t % 2
        s_prev = 1 - s_cur

        @pl.when(t == 0)
        def _():
            dk_acc[...] = jnp.zeros_like(dk_acc)
            dv_acc[...] = jnp.zeros_like(dv_acc)

        @pl.when(t < nt)
        def _():
            kj, band, base = _stacked_consts(sl_ref)
            head0 = lax.broadcasted_iota(jnp.int32, (CHUNK, LANES), 1) < HEAD_DIM
            for ci in range(ATT_TILE // 256):
                rows = slice(ci * 256, (ci + 1) * 256)
                h0 = lax.broadcasted_iota(jnp.int32, (256, LANES), 1) < HEAD_DIM
                dd = do_ref[rows, :] * o_ref[rows, :]
                d0 = jnp.sum(jnp.where(h0, dd, 0.0), axis=-1, keepdims=True)
                d1 = jnp.sum(jnp.where(h0, 0.0, dd), axis=-1, keepdims=True)
                delta_s[rows, :] = jnp.where(h0, d0, d1)
            dq_ref[...] = jnp.zeros_like(dq_ref)

            def column(xb):
                return jnp.concatenate([xb[:, 0:1], xb[:, HEAD_DIM:HEAD_DIM + 1]], axis=0)

            for d in DILATIONS:
                bias = jnp.where(band, -(float(d) * base), NEG)

                def block(j, carry, d=d, bias=bias):
                    b, _, rows = _block_rows(j, d)
                    kcat, vcat = _kv_block(j, d, rows, kc_ref, kp_ref, vc_ref, vp_ref)
                    q2 = _stack_heads(q_ref[rows, :], head0)
                    do2 = _stack_heads(do_ref[rows, :], head0)
                    s = _scores(q2, kcat, bias, kj, (t == 0) & (b == 0))
                    p = jnp.exp(s - column(lse_ref[rows, :]))
                    ds = (p * (_dot(do2, vcat, NT) - column(delta_s[rows, :]))).astype(BF16)
                    dq_ref[rows, :] += _unstack_heads(_dot(ds, kcat), head0)
                    ck = _dot(ds, q2, TN)
                    cv = _dot(p.astype(BF16), do2, TN)
                    dk_acc[s_cur, rows, :] += ck[CHUNK:, :]
                    dv_acc[s_cur, rows, :] += cv[CHUNK:, :]
                    here, before = _prev_rows(j, d)
                    if ATT_BLOCKS // d == 1:
                        dk_acc[s_prev, before, :] += ck[:CHUNK, :]
                        dv_acc[s_prev, before, :] += cv[:CHUNK, :]
                    else:
                        slot, dst = (s_prev, before) if b == 0 else (s_cur, here)
                        dk_acc[slot, dst, :] += ck[:CHUNK, :]
                        dv_acc[slot, dst, :] += cv[:CHUNK, :]
                    return carry

                for j in range(ATT_BLOCKS):
                    block(j, 0)

        dk_ref[...] = dk_acc[s_prev]
        dv_ref[...] = dv_acc[s_prev]
        dk_acc[s_prev] = jnp.zeros((ATT_TILE, LANES), F32)
        dv_acc[s_prev] = jnp.zeros((ATT_TILE, LANES), F32)

        if ns:
            pl.when(step == steps - 1)(finish)

    cur, prev, slope = _attn_tile_specs(nt, lag=True)
    late = pl.BlockSpec((ATT_TILE, LANES), lambda c, t: (jnp.maximum(t - 1, 0), c))
    sds = jax.ShapeDtypeStruct((T, ATTN_W), F32)
    outs = pl.pallas_call(
        body, name="attn_bwd", grid=(ATTN_W // LANES, nt + 1),
        in_specs=[slope, cur, cur, prev, cur, prev, cur, cur, cur] + [_HBM] * ns,
        out_specs=[cur, late, late] + [_HBM] * ns,
        out_shape=[sds, sds, sds] + [jax.ShapeDtypeStruct(p.shape, p.dtype) for p in owner_grads],
        scratch_shapes=[pltpu.VMEM((2, ATT_TILE, LANES), F32), pltpu.VMEM((2, ATT_TILE, LANES), F32),
                        pltpu.VMEM((ATT_TILE, LANES), F32)] + (_owner_exchange_sems(ns) if ns else []),
        compiler_params=_params(("arbitrary", "arbitrary")),
    )(_slope_table(), q, k, k, v, v, dattn, attn, lse, *owner_grads)
    return outs[0], outs[1], outs[2], tuple(outs[3:])


def _attn_bwd(q, k, v, dattn, attn, lse, owner_grads=()):
    T = q.shape[0]
    nt = T // ATT_TILE
    ns = len(owner_grads)
    steps = (ATTN_W // LANES) * nt

    def body(sl_ref, q_hbm, k_hbm, v_hbm, do_hbm, o_hbm, lse_hbm, *rest):
        p_refs, rest = rest[:ns], rest[ns:]
        dq_hbm, dk_hbm, dv_hbm = rest[:3]
        r_refs, rest = rest[3:3 + ns], rest[3 + ns:]
        qbuf, dobuf, obuf, lbuf, kbuf, vbuf, dqbuf, dkbuf, dvbuf, delta_s = rest[:10]
        sem_q, sem_do, sem_o, sem_l, sem_k, sem_v, sem_dq, sem_dk, sem_dv = rest[10:19]
        hp, t = pl.program_id(0), pl.program_id(1)
        step = hp * nt + t
        two, three = step % 2, step % 3
        before, after = (step + 2) % 3, (step + 1) % 3
        if ns:
            start, finish = _owner_exchange_phases(p_refs, r_refs, *rest[19:])
            pl.when(step == 0)(start)

        def fetch(hp_, t_, two_, three_):
            for hbm, buf, sem, slot in ((q_hbm, qbuf, sem_q, two_), (do_hbm, dobuf, sem_do, two_),
                                        (o_hbm, obuf, sem_o, two_), (lse_hbm, lbuf, sem_l, two_),
                                        (k_hbm, kbuf, sem_k, three_), (v_hbm, vbuf, sem_v, three_)):
                for cp in _tile_copies(hbm, buf.at[slot], sem.at[slot], hp_, t_):
                    cp.start()

        @pl.when(step == 0)
        def _():
            kbuf[2] = jnp.zeros((ATT_BLOCKS, CHUNK, LANES), F32)
            vbuf[2] = jnp.zeros((ATT_BLOCKS, CHUNK, LANES), F32)
            dkbuf[3] = jnp.zeros((ATT_BLOCKS, CHUNK, LANES), F32)
            dvbuf[3] = jnp.zeros((ATT_BLOCKS, CHUNK, LANES), F32)
            fetch(0, 0, 0, 0)

        @pl.when(step + 1 < steps)
        def _():
            fetch((step + 1) // nt, (step + 1) % nt, 1 - two, after)

        for buf, sem in ((qbuf, sem_q), (dobuf, sem_do), (obuf, sem_o), (lbuf, sem_l)):
            _wait_tile(buf.at[two], sem.at[two])
        _wait_tile(kbuf.at[three], sem_k.at[three])
        _wait_tile(vbuf.at[three], sem_v.at[three])

        @pl.when(step >= 2)
        def _():
            _wait_tile(dqbuf.at[two], sem_dq.at[two])

        @pl.when(step >= 3)
        def _():
            _wait_tile(dkbuf.at[three], sem_dk.at[three])
            _wait_tile(dvbuf.at[three], sem_dv.at[three])

        zero_tile = jnp.zeros((ATT_BLOCKS, CHUNK, LANES), F32)
        dqbuf[two] = zero_tile
        dkbuf[three] = zero_tile
        dvbuf[three] = zero_tile

        q_t, do_t, l_t, k_t, v_t = qbuf.at[two], dobuf.at[two], lbuf.at[two], kbuf.at[three], vbuf.at[three]
        k_b, v_b = kbuf.at[before], vbuf.at[before]
        dq_t, dk_t, dv_t = dqbuf.at[two], dkbuf.at[three], dvbuf.at[three]
        dk_b, dv_b = dkbuf.at[before], dvbuf.at[before]
        sink = jnp.where(t > 0, before, 3)
        dk_sink, dv_sink = dkbuf.at[sink], dvbuf.at[sink]
        head0 = lax.broadcasted_iota(jnp.int32, (CHUNK, LANES), 1) < HEAD_DIM
        for r in range(ATT_BLOCKS):
            dd = dobuf[two, r] * obuf[two, r]
            d0 = jnp.sum(jnp.where(head0, dd, 0.0), axis=-1, keepdims=True)
            d1 = jnp.sum(jnp.where(head0, 0.0, dd), axis=-1, keepdims=True)
            delta_s[r] = jnp.where(head0, d0, d1)

        def column(xb):
            return jnp.concatenate([xb[:, 0:1], xb[:, HEAD_DIM:HEAD_DIM + 1]], axis=0)

        no_key_before = jnp.where(lax.broadcasted_iota(jnp.int32, (2 * CHUNK, 2 * CHUNK), 1) < CHUNK, NEG, 0.0)
        for d in DILATIONS:
            bias = _residue_bias(sl_ref, d)
            for j in range(ATT_BLOCKS):
                kcat = jnp.concatenate([_rm_block_before(k_t, k_b, d, j), _rm_block(k_t, d, j)], axis=0).astype(BF16)
                vcat = jnp.concatenate([_rm_block_before(v_t, v_b, d, j), _rm_block(v_t, d, j)], axis=0).astype(BF16)
                q2 = _stack_heads(_rm_block(q_t, d, j), head0)
                do2 = _stack_heads(_rm_block(do_t, d, j), head0)
                s = _dot(q2, kcat, NT) + bias
                if _first_in_tile(d, j):
                    s = s + jnp.where(t == 0, 1.0, 0.0) * no_key_before
                p = jnp.exp(s - column(_rm_block(l_t, d, j)))
                ds = (p * (_dot(do2, vcat, NT) - column(_rm_block(delta_s, d, j)))).astype(BF16)
                _rm_add(dq_t, _residue_rows(d, j), _unstack_heads(_dot(ds, kcat), head0))
                ck = _dot(ds, q2, TN)
                cv = _dot(p.astype(BF16), do2, TN)
                _rm_add(dk_t, _residue_rows(d, j), ck[CHUNK:, :])
                _rm_add(dv_t, _residue_rows(d, j), cv[CHUNK:, :])
                if _first_in_tile(d, j):
                    rows = [(r, CHUNK - n, n) for r, _, n in _residue_rows(d, j)]
                    _rm_add(dk_sink, rows, ck[:CHUNK, :])
                    _rm_add(dv_sink, rows, cv[:CHUNK, :])
                else:
                    rows = [(r, lo - n, n) for r, lo, n in _residue_rows(d, j)]
                    _rm_add(dk_t, rows, ck[:CHUNK, :])
                    _rm_add(dv_t, rows, cv[:CHUNK, :])

        for cp in _tile_copies(dq_hbm, dq_t, sem_dq.at[two], hp, t, to_hbm=True):
            cp.start()

        @pl.when(t > 0)
        def _():
            for cp in (_tile_copies(dk_hbm, dk_b, sem_dk.at[before], hp, t - 1, to_hbm=True)
                       + _tile_copies(dv_hbm, dv_b, sem_dv.at[before], hp, t - 1, to_hbm=True)):
                cp.start()

        @pl.when(t == nt - 1)
        def _():
            for cp in (_tile_copies(dk_hbm, dk_t, sem_dk.at[three], hp, t, to_hbm=True)
                       + _tile_copies(dv_hbm, dv_t, sem_dv.at[three], hp, t, to_hbm=True)):
                cp.start()

        @pl.when(step == steps - 1)
        def _():
            for slot in range(2):
                _wait_tile(dqbuf.at[slot], sem_dq.at[slot])
            for slot in range(3):
                _wait_tile(dkbuf.at[slot], sem_dk.at[slot])
                _wait_tile(dvbuf.at[slot], sem_dv.at[slot])

        if ns:
            pl.when(step == steps - 1)(finish)

    tile = lambda n: pltpu.VMEM((n, ATT_BLOCKS, CHUNK, LANES), F32)
    dma = lambda n: pltpu.SemaphoreType.DMA((n,))
    view = jax.ShapeDtypeStruct((T // ATT_BLOCKS, ATT_BLOCKS, ATTN_W), F32)
    outs = pl.pallas_call(
        body, name="attn_bwd", grid=(ATTN_W // LANES, nt),
        in_specs=[pl.BlockSpec((8, LANES), lambda c, t: (0, c))] + [_HBM] * (6 + ns),
        out_specs=[_HBM] * (3 + ns),
        out_shape=[view] * 3 + [jax.ShapeDtypeStruct(p.shape, p.dtype) for p in owner_grads],
        scratch_shapes=[tile(2), tile(2), tile(2), tile(2), tile(3), tile(3), tile(2), tile(4), tile(4),
                        pltpu.VMEM((ATT_BLOCKS, CHUNK, LANES), F32)]
        + [dma(2), dma(2), dma(2), dma(2), dma(3), dma(3), dma(2), dma(3), dma(3)]
        + (_owner_exchange_sems(ns) if ns else []),
        compiler_params=_params(("arbitrary", "arbitrary")),
    )(_slope_table(), *[_residue_view(a) for a in (q, k, v, dattn, attn, lse)], *owner_grads)
    return (*[o.reshape(T, ATTN_W) for o in outs[:3]], tuple(outs[3:]))


def _dproj_assemble(dq, dk, dv, du, dz):
    T = du.shape[0]
    tm = 512

    def body(q_ref, k_ref, v_ref, u_ref, z_ref, out_ref):
        a = ATTN_W
        out_ref[:, 0:a] = (q_ref[...] * Q_SCALE).astype(BF16)
        out_ref[:, a:2 * a] = k_ref[...].astype(BF16)
        out_ref[:, 2 * a:3 * a] = v_ref[...].astype(BF16)
        out_ref[:, 3 * a:3 * a + GMLP_W] = u_ref[...].astype(BF16)
        out_ref[:, 3 * a + GMLP_W:] = z_ref[...].astype(BF16)

    return pl.pallas_call(
        body, name="dproj_assemble", grid=(T // tm,),
        in_specs=[_rows(tm, ATTN_W)] * 3 + [_rows(tm, GMLP_W)] * 2,
        out_specs=_rows(tm, IN_W),
        out_shape=jax.ShapeDtypeStruct((T, IN_W), BF16),
        compiler_params=_params(("parallel",)),
    )(dq, dk, dv, du, dz)


def _proj_bwd(dproj, w_in_t, x, g1, dh1, chip_sums=()):
    T = x.shape[0]
    tm = TM_PROJ
    ns = len(chip_sums)
    steps = T // tm

    def body(d_ref, w_ref, x_ref, g_ref, r_ref, *rest):
        p_refs, rest = rest[:ns], rest[ns:]
        dx_ref, dg_ref = rest[:2]
        r_refs, sems = rest[2:2 + ns], rest[2 + ns:]
        step = pl.program_id(0)
        if ns:
            start, finish = _chip_exchange_phases(p_refs, r_refs, *sems)
            pl.when(step == 0)(start)

        @pl.when(step == 0)
        def _():
            dg_ref[...] = jnp.zeros_like(dg_ref)

        dhn = _dot(d_ref[...], w_ref[...])
        n1, r1 = _rms(x_ref[...])
        _accum_rows(dg_ref, dhn * n1)
        dx_ref[...] = r_ref[...] + _rms_bwd(n1, r1, g_ref[...], dhn)
        if ns:
            pl.when(step == steps - 1)(finish)

    outs = pl.pallas_call(
        body, name="proj_bwd", grid=(steps,),
        in_specs=[_rows(tm, IN_W), _resident((IN_W, D_MODEL)), _rows(tm, D_MODEL), _resident((1, D_MODEL)),
                  _rows(tm, D_MODEL)] + [_HBM] * ns,
        out_specs=[_rows(tm, D_MODEL), pl.BlockSpec((8, D_MODEL), lambda i: (0, 0))] + [_HBM] * ns,
        out_shape=[jax.ShapeDtypeStruct((T, D_MODEL), F32), jax.ShapeDtypeStruct((8, D_MODEL), F32)]
        + [jax.ShapeDtypeStruct(p.shape, p.dtype) for p in chip_sums],
        scratch_shapes=_chip_exchange_sems(ns) if ns else [],
        compiler_params=_params(("arbitrary",)),
    )(dproj, w_in_t, x, g1, dh1, *chip_sums)
    return outs[0], outs[1], tuple(outs[2:])


def _dw(a, b, name, tile, square_a=False, out_dtype=F32):
    T, ka = a.shape
    nb = b.shape[1]
    tka, tnb, tt = tile
    tt = min(tt, T)
    last = T // tt - 1

    def body(a_ref, b_ref, o_ref, *scratch):
        acc_ref = scratch[0] if scratch else o_ref
        s = pl.program_id(2)

        @pl.when(s == 0)
        def _():
            acc_ref[...] = jnp.zeros_like(acc_ref)

        a_tile = a_ref[...]
        if square_a:
            a_tile = jnp.square(a_tile.astype(F32)).astype(BF16)
        acc_ref[...] += _dot(a_tile, b_ref[...], TN)
        if scratch:
            @pl.when(s == last)
            def _():
                o_ref[...] = acc_ref[...].astype(out_dtype)

    return pl.pallas_call(
        body, name=name, grid=(ka // tka, nb // tnb, T // tt),
        in_specs=[pl.BlockSpec((tt, tka), lambda i, j, s: (s, i)), pl.BlockSpec((tt, tnb), lambda i, j, s: (s, j))],
        out_specs=pl.BlockSpec((tka, tnb), lambda i, j, s: (i, j)),
        out_shape=jax.ShapeDtypeStruct((ka, nb), out_dtype),
        scratch_shapes=[] if out_dtype == F32 else [pltpu.VMEM((tka, tnb), F32)],
        compiler_params=_params(("parallel", "parallel", "arbitrary")),
    )(a, b)


def _adamw(w, m, v, parts, name, tr, transposed=False):
    R, C = w.shape
    P = parts.shape[0]

    def body(w_ref, m_ref, v_ref, p_ref, g_ref, d_ref, m2_ref, v2_ref):
        g = p_ref[0].astype(F32)
        for i in range(1, P):
            g = g + p_ref[i].astype(F32)
        if transposed:
            g = g.T
        m2 = ADAM_B1 * m_ref[...] + (1.0 - ADAM_B1) * g
        v2 = ADAM_B2 * v_ref[...] + (1.0 - ADAM_B2) * jnp.square(g)
        m_hat = m2 / (1.0 - ADAM_B1 ** ADAM_STEP)
        v_hat = v2 / (1.0 - ADAM_B2 ** ADAM_STEP)
        g_ref[...] = g
        d_ref[...] = -ADAM_LR * (m_hat / (jnp.sqrt(v_hat) + ADAM_EPS) + ADAM_WD * w_ref[...])
        m2_ref[...] = m2
        v2_ref[...] = v2

    spec = _rows(tr, C)
    part_spec = (pl.BlockSpec((P, C, tr), lambda i: (0, 0, i)) if transposed
                 else pl.BlockSpec((P, tr, C), lambda i: (0, i, 0)))
    return pl.pallas_call(
        body, name=name, grid=(R // tr,),
        in_specs=[spec, spec, spec, part_spec],
        out_specs=[spec] * 4,
        out_shape=[jax.ShapeDtypeStruct((R, C), F32)] * 4,
        compiler_params=_params(("parallel",)),
    )(w, m, v, parts)


def _pair_sum(core, grad, recv, name):
    _, _, n, C = grad.shape
    tr = n // 2

    def body(c_ref, a_ref, b_ref, o_ref):
        o_ref[...] = a_ref[...] + b_ref[...]

    spec = pl.BlockSpec((1, tr, C), lambda i, j, c_ref: (i, j, 0))
    return pl.pallas_call(
        body, name=name,
        grid_spec=pltpu.PrefetchScalarGridSpec(
            num_scalar_prefetch=1, grid=(4, n // tr),
            in_specs=[pl.BlockSpec((1, None, tr, C), lambda i, j, c_ref: (i, c_ref[0], j, 0)), spec],
            out_specs=spec),
        out_shape=jax.ShapeDtypeStruct(recv.shape, F32),
        compiler_params=_params(("parallel", "parallel")),
    )(core.reshape(1), grad, recv)


_HBM = pl.BlockSpec(memory_space=pltpu.HBM)


def _place():
    return lax.axis_index("x"), lax.axis_index("y"), lax.axis_index("c")


def _gathered_shape(shard):
    return jax.ShapeDtypeStruct((N_DEV,) + shard.shape, shard.dtype)


def _gather_sems(n):
    return [pltpu.SemaphoreType.DMA((7, n)), pltpu.SemaphoreType.DMA((7, n)), pltpu.SemaphoreType.DMA((n,))]


def _gather_phases(x_refs, out_refs, send_sems, recv_sems, local_sems):
    x, y, c = _place()
    me, sibling = (x, y, c), (x, y, 1 - c)
    chips = [(1 - x, y), (x, 1 - y), (1 - x, 1 - y)]
    arrays = range(len(x_refs))

    def slot(i, px, py, pc):
        return out_refs[i].at[4 * px + 2 * py + pc]

    def copy(i, k, block, to, own=False):
        return pltpu.make_async_remote_copy(
            src_ref=x_refs[i] if own else slot(i, *block), dst_ref=slot(i, *block),
            send_sem=send_sems.at[k, i], recv_sem=recv_sems.at[k, i], device_id=to, device_id_type=MESH)

    def mine(i):
        return pltpu.make_async_copy(x_refs[i], slot(i, *me), local_sems.at[i])

    def start():
        for i in arrays:
            mine(i).start()
            copy(i, 0, me, sibling, own=True).start()
            for j, chip in enumerate(chips):
                copy(i, 1 + j, me, (*chip, c), own=True).start()

    def forward():
        for i in arrays:
            for j, chip in enumerate(chips):
                copy(i, 1 + j, (*chip, c), me).wait_recv()
                copy(i, 4 + j, (*chip, c), sibling).start()

    def finish():
        for i in arrays:
            copy(i, 0, sibling, me).wait_recv()
            copy(i, 0, me, sibling, own=True).wait_send()
            for j, chip in enumerate(chips):
                copy(i, 4 + j, (*chip, 1 - c), me).wait_recv()
                copy(i, 1 + j, me, (*chip, c), own=True).wait_send()
                copy(i, 4 + j, (*chip, c), sibling).wait_send()
            mine(i).wait()

    return start, forward, finish


def _all_gather(shards, name):
    n = len(shards)

    def body(*refs):
        start, forward, finish = _gather_phases(refs[:n], refs[n:2 * n], *refs[2 * n:])
        start()
        forward()
        finish()

    return pl.pallas_call(
        body, name=name,
        out_shape=[_gathered_shape(s) for s in shards],
        in_specs=[_HBM] * n, out_specs=[_HBM] * n,
        scratch_shapes=_gather_sems(n),
    )(*shards)


def _sibling_exchange(grads, name):
    n = len(grads)

    def body(*refs):
        g_refs, r_refs, send_sems, recv_sems = refs[:n], refs[n:2 * n], refs[2 * n], refs[2 * n + 1]
        x, y, c = _place()
        copies = [pltpu.make_async_remote_copy(
            src_ref=g_refs[i].at[:, 1 - c], dst_ref=r_refs[i], send_sem=send_sems.at[i], recv_sem=recv_sems.at[i],
            device_id=(x, y, 1 - c), device_id_type=MESH) for i in range(n)]
        for cp in copies:
            cp.start()
        for cp in copies:
            cp.wait()

    return pl.pallas_call(
        body, name=name,
        out_shape=[jax.ShapeDtypeStruct((g.shape[0],) + g.shape[2:], g.dtype) for g in grads],
        in_specs=[_HBM] * n, out_specs=[_HBM] * n,
        scratch_shapes=[pltpu.SemaphoreType.DMA((n,)), pltpu.SemaphoreType.DMA((n,))],
    )(*grads)


def _owner_exchange_sems(n):
    return [pltpu.SemaphoreType.DMA((7, n)), pltpu.SemaphoreType.DMA((7, n)), pltpu.SemaphoreType.DMA((n,))]


def _owner_exchange_phases(g_refs, r_refs, send_sems, recv_sems, local_sems):
    x, y, c = _place()
    me = 4 * x + 2 * y + c
    flip = lambda v, bit: 1 - v if bit else v
    peers = [(flip(x, k & 4), flip(y, k & 2), flip(c, k & 1)) for k in range(1, N_DEV)]
    arrays = range(len(g_refs))

    def mine(i):
        return pltpu.make_async_copy(g_refs[i].at[me], r_refs[i].at[me], local_sems.at[i])

    def copy(i, k, src_slot, dst_slot):
        return pltpu.make_async_remote_copy(
            src_ref=g_refs[i].at[src_slot], dst_ref=r_refs[i].at[dst_slot],
            send_sem=send_sems.at[k, i], recv_sem=recv_sems.at[k, i], device_id=peers[k], device_id_type=MESH)

    def start():
        for i in arrays:
            mine(i).start()
            for k, (px, py, pc) in enumerate(peers):
                copy(i, k, 4 * px + 2 * py + pc, me).start()

    def finish():
        for i in arrays:
            for k, (px, py, pc) in enumerate(peers):
                copy(i, k, me, 4 * px + 2 * py + pc).wait_recv()
                copy(i, k, 4 * px + 2 * py + pc, me).wait_send()
            mine(i).wait()

    return start, finish


def _chip_exchange_sems(n):
    return [pltpu.SemaphoreType.DMA((3, n)), pltpu.SemaphoreType.DMA((3, n)), pltpu.SemaphoreType.DMA((n,))]


def _chip_exchange_phases(p_refs, r_refs, send_sems, recv_sems, local_sems):
    x, y, c = _place()
    my_chip = 2 * x + y
    chips = [(1 - x, y), (x, 1 - y), (1 - x, 1 - y)]
    arrays = range(len(p_refs))

    def mine(i):
        return pltpu.make_async_copy(p_refs[i].at[my_chip], r_refs[i].at[my_chip], local_sems.at[i])

    def copy(i, k, src_chip, dst_chip):
        px, py = chips[k]
        return pltpu.make_async_remote_copy(
            src_ref=p_refs[i].at[src_chip], dst_ref=r_refs[i].at[dst_chip],
            send_sem=send_sems.at[k, i], recv_sem=recv_sems.at[k, i], device_id=(px, py, c), device_id_type=MESH)

    def start():
        for i in arrays:
            mine(i).start()
            for k, (px, py) in enumerate(chips):
                copy(i, k, 2 * px + py, my_chip).start()

    def finish():
        for i in arrays:
            for k, (px, py) in enumerate(chips):
                copy(i, k, my_chip, 2 * px + py).wait_recv()
                copy(i, k, 2 * px + py, my_chip).wait_send()
            mine(i).wait()

    return start, finish


_R_IN, _R_OUT, _R_FF = IN_W // N_DEV, D_MODEL // N_DEV, D_FF // N_DEV


def _by_owner(g):
    return g.reshape(4, 2, g.shape[0] // N_DEV, D_MODEL)


def _local_step(x, tgt, small, w_in_t, rest, core=None):
    exchange = core is not None
    g1, g2, gf = small["norm1_g"], small["norm2_g"], small["final_norm_g"].reshape(1, D_MODEL)
    ga, gg = small["attn_out_g"], small["gmlp_out_g"]
    ln_g = small["sgu_ln_g"].reshape(1, GMLP_W)
    ln_b = small["sgu_ln_b"].reshape(1, GMLP_W)
    sgu_w = small["sgu_w"][0]
    sgu_bt = small["sgu_b"][0].T

    hn1, q, k, v, u, z = _proj_fwd(x, g1, w_in_t)
    attn, lse, gathered = _attn_fwd(q, k, v, shards=rest if exchange else ())
    w_out, w_ff1_t, w_ff2 = [g.reshape(-1, D_MODEL) for g in gathered] if exchange else rest
    gm = _gmlp_fwd(u, z, ln_g, ln_b, sgu_w, sgu_bt)
    mixed, h1, hn2 = _out_fwd(attn, gm, ga, gg, w_out, x, g2)
    relu, dh2f, dh2b, loss8, dgf8 = _ffn_fwd(hn2, h1, w_ff1_t, w_ff2, gf, tgt)

    da, dh1f, dh1b, dg2 = _ffn_bwd(dh2b, dh2f, relu, h1, g2, w_ff2, w_ff1_t)
    wire = BF16 if exchange else F32
    dw_ff2 = _dw(relu, dh2b, "dw_ff2", DW_TILE, square_a=True, out_dtype=wire)
    dw_ff1_t = _dw(da, hn2, "dw_ff1", DW_TILE, out_dtype=wire)
    dattn, dgm, dga, dgg = _out_bwd(dh1b, w_out, attn, gm, ga, gg)
    dw_out = _dw(mixed, dh1b, "dw_out", DW_TILE, out_dtype=wire)
    early = [dw_out, dw_ff1_t, dw_ff2]
    if exchange:
        early = [g.reshape(N_DEV, -1, D_MODEL) for g in early]
    du, dz, dlg, dlb, dsw, dsb = _gmlp_bwd(u, z, dgm, ln_g, ln_b, sgu_w, sgu_bt)
    dq, dk, dv, arrived = _attn_bwd(q, k, v, dattn, attn, lse, owner_grads=early if exchange else ())
    dproj = _dproj_assemble(dq, dk, dv, du, dz)
    dw_in_t = _dw(dproj, hn1, "dw_in", DW_TILE_IN)
    late = ()
    if exchange:
        by_owner = _by_owner(dw_in_t)
        got, = _sibling_exchange([by_owner], "grad_sibling_exchange")
        late = (_pair_sum(core, by_owner, got, "grad_pair_sum"),)
    dx, dg1, late = _proj_bwd(dproj, w_in_t, x, g1, dh1f, chip_sums=late)
    if exchange:
        dw_in_t, early = late[0], arrived

    small_grads = dict(
        norm1_g=dg1[0], sgu_ln_g=dlg[0], sgu_ln_b=dlb[0], sgu_w=dsw, sgu_b=dsb[:, :N_GROUPS].T,
        attn_out_g=dga[0], gmlp_out_g=dgg[0], norm2_g=dg2[0], final_norm_g=dgf8[0])
    return loss8[0, 0], dx, (dw_in_t, *early), small_grads


SMALL_NAMES = ("norm1_g", "sgu_ln_g", "sgu_ln_b", "sgu_w", "sgu_b", "attn_out_g", "gmlp_out_g", "norm2_g",
               "final_norm_g")
WEIGHT_ORDER = ("norm1_g", "w_in", "sgu_ln_g", "sgu_ln_b", "sgu_w", "sgu_b", "attn_out_g", "gmlp_out_g", "w_out",
                "norm2_g", "w_ff1", "w_ff2", "final_norm_g")


def _pack_small(d):
    return jnp.concatenate([d[n].reshape(-1, LANES) for n in SMALL_NAMES], axis=0)


def _unpack_small(p, like):
    out, r = {}, 0
    for n in SMALL_NAMES:
        rows = like[n].size // LANES
        out[n] = p[r:r + rows].reshape(like[n].shape)
        r += rows
    return out


def kernel(x, norm1_g, w_in, sgu_ln_g, sgu_ln_b, sgu_w, sgu_b, attn_out_g, gmlp_out_g, w_out, norm2_g, w_ff1, w_ff2, final_norm_g, loss_target, m_norm1_g, m_w_in, m_sgu_ln_g, m_sgu_ln_b, m_sgu_w, m_sgu_b, m_attn_out_g, m_gmlp_out_g, m_w_out, m_norm2_g, m_w_ff1, m_w_ff2, m_final_norm_g, v_norm1_g, v_w_in, v_sgu_ln_g, v_sgu_ln_b, v_sgu_w, v_sgu_b, v_attn_out_g, v_gmlp_out_g, v_w_out, v_norm2_g, v_w_ff1, v_w_ff2, v_final_norm_g):
    w = dict(norm1_g=norm1_g, w_in=w_in, sgu_ln_g=sgu_ln_g, sgu_ln_b=sgu_ln_b, sgu_w=sgu_w, sgu_b=sgu_b,
             attn_out_g=attn_out_g, gmlp_out_g=gmlp_out_g, w_out=w_out, norm2_g=norm2_g, w_ff1=w_ff1, w_ff2=w_ff2,
             final_norm_g=final_norm_g)
    m = dict(norm1_g=m_norm1_g, w_in=m_w_in, sgu_ln_g=m_sgu_ln_g, sgu_ln_b=m_sgu_ln_b, sgu_w=m_sgu_w, sgu_b=m_sgu_b,
             attn_out_g=m_attn_out_g, gmlp_out_g=m_gmlp_out_g, w_out=m_w_out, norm2_g=m_norm2_g, w_ff1=m_w_ff1,
             w_ff2=m_w_ff2, final_norm_g=m_final_norm_g)
    v = dict(norm1_g=v_norm1_g, w_in=v_w_in, sgu_ln_g=v_sgu_ln_g, sgu_ln_b=v_sgu_ln_b, sgu_w=v_sgu_w, sgu_b=v_sgu_b,
             attn_out_g=v_attn_out_g, gmlp_out_g=v_gmlp_out_g, w_out=v_w_out, norm2_g=v_norm2_g, w_ff1=v_w_ff1,
             w_ff2=v_w_ff2, final_norm_g=v_final_norm_g)
    big = ("w_in", "w_out", "w_ff1", "w_ff2")
    core = lax.axis_index("c")

    w_in_t, = _all_gather([w_in[0].T.astype(BF16)], "w_in_all_gather")
    rest = (w_out[0].astype(BF16), w_ff1[0].T.astype(BF16), w_ff2[0].astype(BF16))
    loss, dx, parts, small_grads = _local_step(x[0], loss_target[0], {n: w[n] for n in SMALL_NAMES},
                                               w_in_t.reshape(IN_W, D_MODEL), rest, core=core)
    loss = lax.psum(loss, ("x", "y", "c"))

    new = {}
    for n, p, transposed, tr in zip(big, parts, (True, False, True, False), (128, 128, 128, 256)):
        new[n] = [a[None] for a in _adamw(w[n][0], m[n][0], v[n][0], p, "adamw_" + n, tr, transposed)]

    small_parts, = _all_gather([_pack_small(small_grads)], "small_grad_all_gather")
    packed = _adamw(_pack_small({n: w[n] for n in SMALL_NAMES}), _pack_small({n: m[n] for n in SMALL_NAMES}),
                    _pack_small({n: v[n] for n in SMALL_NAMES}), small_parts, "adamw_small", SMALL_ROWS)

    outs = []
    for i, ps in enumerate(packed):
        d = {n: new[n][i] for n in big}
        d.update(_unpack_small(ps, w))
        outs.extend(d[n] for n in WEIGHT_ORDER)
    return (loss, dx[None], *outs)
```

```python
import functools
import math

import numpy as np
import jax
import jax.numpy as jnp
from jax import lax
from jax.experimental import pallas as pl
from jax.experimental.pallas import tpu as pltpu

F32 = jnp.float32
BF16 = jnp.bfloat16

D_MODEL = 1024
HEAD_DIM = 64
N_HEADS = 12
ATTN_W = N_HEADS * HEAD_DIM
N_GROUPS = 4
GMLP_W = N_GROUPS * HEAD_DIM
IN_W = 3 * ATTN_W + 2 * GMLP_W
D_FF = 4 * D_MODEL
CHUNK = 128
DILATIONS = (1, 4, 16)
EPS = 1e-6
Q_SCALE = HEAD_DIM ** -0.5
NEG = -1e30

ADAM_LR, ADAM_B1, ADAM_B2, ADAM_EPS, ADAM_WD, ADAM_STEP = 0.001, 0.9, 0.999, 1e-08, 0.01, 10

N_DEV = 8
LANES = 128
VMEM_LIMIT = 56 << 20

TM_PROJ = 512
TM_FFN = 512
FF_CHUNK = 512
DW_TILE = (512, 1024, 4096)
DW_TILE_QKV = (3 * ATTN_W // 2, 1024, 2048)
DW_TILE_UZ = (GMLP_W, 1024, 4096)

MESH = pl.DeviceIdType.MESH


def _alibi_slopes(n):
    def pow2(m):
        start = 2.0 ** (-8.0 / m)
        return [start ** (i + 1) for i in range(m)]
    c = 2 ** int(math.floor(math.log2(n)))
    s = pow2(n) if c == n else pow2(c) + pow2(2 * c)[0::2][: n - c]
    return np.asarray(s, dtype=np.float32)


SLOPES = _alibi_slopes(N_HEADS)


def _params(sem=None):
    kw = dict(vmem_limit_bytes=VMEM_LIMIT)
    if sem is not None:
        kw["dimension_semantics"] = sem
    return pltpu.CompilerParams(**kw)


def _rows(tm, n):
    return pl.BlockSpec((tm, n), lambda i: (i, 0))


def _resident(shape):
    return pl.BlockSpec(shape, lambda *_: (0,) * len(shape), pipeline_mode=pl.Buffered(1))


def _rms(x):
    r = lax.rsqrt(jnp.mean(x * x, axis=-1, keepdims=True) + EPS)
    return x * r, r


def _rms_bwd(n, r, g, dy):
    dn = dy * g
    return r * (dn - n * jnp.mean(dn * n, axis=-1, keepdims=True))


def _accum_rows(acc_ref, v):
    acc_ref[...] += jnp.broadcast_to(jnp.sum(v, axis=0, keepdims=True), acc_ref.shape)


_G0 = math.sqrt(2.0 / math.pi)
_G1 = 0.044715


def _gelu(x):
    t = jnp.tanh(_G0 * (x + _G1 * (x * x * x)))
    return x * (0.5 * (1.0 + t)), t


def _gelu_grad(x, t):
    return 0.5 * (1.0 + t) + 0.5 * x * (1.0 - t * t) * (_G0 * (1.0 + 3.0 * _G1 * x * x))


NT = (((1,), (1,)), ((), ()))
TN = (((0,), (0,)), ((), ()))


def _dot(a, b, dims=None):
    if dims is None:
        return jnp.dot(a, b, preferred_element_type=F32)
    return lax.dot_general(a, b, dims, preferred_element_type=F32)


def _proj_fwd(x, g1, w_in_t):
    T = x.shape[0]
    tm = TM_PROJ

    def body(x_ref, g_ref, w_ref, hn_ref, q_ref, k_ref, v_ref, u_ref, z_ref):
        n, _ = _rms(x_ref[...])
        hn = (n * g_ref[...]).astype(BF16)
        hn_ref[...] = hn
        a = ATTN_W
        q_ref[...] = _dot(hn, w_ref[0:a, :], NT) * Q_SCALE
        k_ref[...] = _dot(hn, w_ref[a:2 * a, :], NT)
        v_ref[...] = _dot(hn, w_ref[2 * a:3 * a, :], NT)
        u_ref[...] = _dot(hn, w_ref[3 * a:3 * a + GMLP_W, :], NT)
        z_ref[...] = _dot(hn, w_ref[3 * a + GMLP_W:, :], NT)

    sds = jax.ShapeDtypeStruct
    return pl.pallas_call(
        body, name="proj_fwd", grid=(T // tm,),
        in_specs=[_rows(tm, D_MODEL), _resident((1, D_MODEL)), _resident((IN_W, D_MODEL))],
        out_specs=[_rows(tm, D_MODEL), _rows(tm, ATTN_W), _rows(tm, ATTN_W), _rows(tm, ATTN_W),
                   _rows(tm, GMLP_W), _rows(tm, GMLP_W)],
        out_shape=[sds((T, D_MODEL), BF16), sds((T, ATTN_W), F32), sds((T, ATTN_W), F32),
                   sds((T, ATTN_W), F32), sds((T, GMLP_W), F32), sds((T, GMLP_W), F32)],
        compiler_params=_params(("parallel",)),
    )(x, g1, w_in_t)


ATT_TILE = 2048
ATT_BLOCKS = ATT_TILE // CHUNK
SM_BLOCKS = 4


def _slope_table():
    row = np.repeat(SLOPES, HEAD_DIM)
    return jnp.asarray(np.broadcast_to(row[None], (8, ATTN_W)), F32)


def _residue_view(a):
    return a.reshape(a.shape[0] // ATT_BLOCKS, ATT_BLOCKS, a.shape[1])


def _tile_copies(hbm, buf, sem, hp, t, to_hbm=False, lane0=0):
    rows = pl.ds(pl.multiple_of(t * CHUNK, CHUNK), CHUNK)
    lanes = pl.ds(pl.multiple_of(lane0 + hp * LANES, LANES), LANES)
    pairs = [(hbm.at[rows, r, lanes], buf.at[r]) for r in range(ATT_BLOCKS)]
    return [pltpu.make_async_copy(v, h, sem) if to_hbm else pltpu.make_async_copy(h, v, sem) for h, v in pairs]


def _wait_tile(buf, sem):
    pltpu.make_async_copy(buf, buf, sem).wait()


def _residue_rows(d, j):
    if d == 16:
        return [(j, 0, CHUNK)]
    if d == 4:
        return [(j % 4 + 4 * m, 32 * (j // 4), 32) for m in range(4)]
    return [(r, 8 * j, 8) for r in range(ATT_BLOCKS)]


def _block_order(p, d):
    if d == 16:
        return p
    if d == 4:
        return 4 * (p & 31) + (p >> 5)
    return 16 * (p & 7) + (p >> 3)


def _first_in_tile(d, j):
    return _residue_rows(d, j)[0][1] == 0


def _rm_block(buf, d, j):
    return jnp.concatenate([buf[r, lo:lo + n, :] for r, lo, n in _residue_rows(d, j)], axis=0)


def _rm_block_before(buf, buf_before, d, j):
    if _first_in_tile(d, j):
        return jnp.concatenate([buf_before[r, CHUNK - n:CHUNK, :] for r, _, n in _residue_rows(d, j)], axis=0)
    return jnp.concatenate([buf[r, lo - n:lo, :] for r, lo, n in _residue_rows(d, j)], axis=0)


def _rm_store(buf, d, j, val):
    at = 0
    for r, lo, n in _residue_rows(d, j):
        buf[r, lo:lo + n, :] = val[at:at + n, :]
        at += n


def _rm_add(buf, rows, val):
    at = 0
    for r, lo, n in rows:
        buf[r, lo:lo + n, :] += val[at:at + n, :]
        at += n


def _residue_bias(sl_ref, d):
    shape = (2 * CHUNK, 2 * CHUNK)
    row = lax.broadcasted_iota(jnp.int32, shape, 0)
    col = lax.broadcasted_iota(jnp.int32, shape, 1)
    steps = _block_order(row & (CHUNK - 1), d) + CHUNK - (_block_order(col & (CHUNK - 1), d) + (col & CHUNK))
    band = (steps >= 0) & (steps <= CHUNK)
    sl = sl_ref[0:1, :]
    upper = lax.broadcasted_iota(jnp.int32, (2 * CHUNK, 1), 0) < CHUNK
    slope2 = jnp.where(upper, sl[:, 0:1], sl[:, HEAD_DIM:HEAD_DIM + 1])
    return jnp.where(band, -(float(d) * slope2 * steps.astype(F32)), NEG)


def _stack_heads(xb, head0):
    zero = jnp.zeros_like(xb)
    return jnp.concatenate([jnp.where(head0, xb, zero), jnp.where(head0, zero, xb)], axis=0).astype(BF16)


def _unstack_heads(x2, head0):
    return jnp.where(head0, x2[:CHUNK, :], x2[CHUNK:, :])


def _attn_fwd(q, k, v, shards=()):
    T = q.shape[0]
    nt = T // ATT_TILE
    ns = len(shards)
    steps = (ATTN_W // LANES) * nt

    def body(sl_ref, q_hbm, k_hbm, v_hbm, *rest):
        x_refs, rest = rest[:ns], rest[ns:]
        attn_hbm, lse_hbm = rest[:2]
        g_refs, rest = rest[2:2 + ns], rest[2 + ns:]
        qbuf, kbuf, vbuf, obuf, lbuf = rest[:5]
        o_acc, l_acc = rest[5:8], rest[8:11]
        sem_q, sem_k, sem_v, sem_o, sem_l = rest[11:16]
        hp, t = pl.program_id(0), pl.program_id(1)
        step = hp * nt + t
        two, three = step % 2, step % 3
        before, after = (step + 2) % 3, (step + 1) % 3
        if ns:
            start, forward, finish = _gather_phases(x_refs, g_refs, *rest[16:])
            pl.when(step == 0)(start)
            pl.when(step == steps // 2)(forward)

        def fetch(hp_, t_, two_, three_):
            for cp in (_tile_copies(q_hbm, qbuf.at[two_], sem_q.at[two_], hp_, t_)
                       + _tile_copies(k_hbm, kbuf.at[three_], sem_k.at[three_], hp_, t_)
                       + _tile_copies(v_hbm, vbuf.at[three_], sem_v.at[three_], hp_, t_)):
                cp.start()

        @pl.when(step == 0)
        def _():
            kbuf[2] = jnp.zeros((ATT_BLOCKS, CHUNK, LANES), F32)
            vbuf[2] = jnp.zeros((ATT_BLOCKS, CHUNK, LANES), F32)
            fetch(0, 0, 0, 0)

        @pl.when(step + 1 < steps)
        def _():
            fetch((step + 1) // nt, (step + 1) % nt, 1 - two, after)

        _wait_tile(qbuf.at[two], sem_q.at[two])
        _wait_tile(kbuf.at[three], sem_k.at[three])
        _wait_tile(vbuf.at[three], sem_v.at[three])

        @pl.when(step >= 2)
        def _():
            _wait_tile(obuf.at[two], sem_o.at[two])
            _wait_tile(lbuf.at[two], sem_l.at[two])

        q_t, k_t, v_t = qbuf.at[two], kbuf.at[three], vbuf.at[three]
        k_b, v_b = kbuf.at[before], vbuf.at[before]
        head0 = lax.broadcasted_iota(jnp.int32, (CHUNK, LANES), 1) < HEAD_DIM
        no_key_before = jnp.where(lax.broadcasted_iota(jnp.int32, (2 * CHUNK, 2 * CHUNK), 1) < CHUNK, NEG, 0.0)
        for pi, d in enumerate(DILATIONS):
            bias = _residue_bias(sl_ref, d)

            def scores(j, d=d, bias=bias):
                kcat = jnp.concatenate([_rm_block_before(k_t, k_b, d, j), _rm_block(k_t, d, j)], axis=0).astype(BF16)
                vcat = jnp.concatenate([_rm_block_before(v_t, v_b, d, j), _rm_block(v_t, d, j)], axis=0).astype(BF16)
                s = _dot(_stack_heads(_rm_block(q_t, d, j), head0), kcat, NT) + bias
                if _first_in_tile(d, j):
                    s = s + jnp.where(t == 0, 1.0, 0.0) * no_key_before
                return s, vcat

            def output(j, p, vcat, scale, lse, d=d, pi=pi):
                _rm_store(o_acc[pi], d, j, _unstack_heads(_dot(p, vcat) * scale, head0))
                _rm_store(l_acc[pi], d, j, _unstack_heads(jnp.broadcast_to(lse, (2 * CHUNK, LANES)), head0))

            for j0 in range(0, ATT_BLOCKS, SM_BLOCKS):
                group = [scores(j) for j in range(j0, j0 + SM_BLOCKS)]
                s = jnp.concatenate([g[0] for g in group], axis=0)
                m = jnp.max(s, axis=-1, keepdims=True)
                p = jnp.exp(s - m)
                l = jnp.sum(p, axis=-1, keepdims=True)
                p, scale, lse = p.astype(BF16), 1.0 / l, m + jnp.log(l)
                for i, (_, vcat) in enumerate(group):
                    rows = slice(i * 2 * CHUNK, (i + 1) * 2 * CHUNK)
                    output(j0 + i, p[rows, :], vcat, scale[rows, :], lse[rows, :])

        for r in range(ATT_BLOCKS):
            a, b, c = l_acc[0][r], l_acc[1][r], l_acc[2][r]
            m = jnp.maximum(jnp.maximum(a, b), c)
            ea, eb, ec = jnp.exp(a - m), jnp.exp(b - m), jnp.exp(c - m)
            tot = ea + eb + ec
            obuf[two, r] = (ea * o_acc[0][r] + eb * o_acc[1][r] + ec * o_acc[2][r]) / tot
            lbuf[two, r] = m + jnp.log(tot)

        for cp in (_tile_copies(attn_hbm, obuf.at[two], sem_o.at[two], hp, t, to_hbm=True)
                   + _tile_copies(lse_hbm, lbuf.at[two], sem_l.at[two], hp, t, to_hbm=True)):
            cp.start()

        @pl.when(step == steps - 1)
        def _():
            for slot in (two, 1 - two)[:min(steps, 2)]:
                _wait_tile(obuf.at[slot], sem_o.at[slot])
                _wait_tile(lbuf.at[slot], sem_l.at[slot])

        if ns:
            pl.when(step == steps - 1)(finish)

    tile = lambda n: pltpu.VMEM((n, ATT_BLOCKS, CHUNK, LANES), F32)
    dma = lambda n: pltpu.SemaphoreType.DMA((n,))
    view = jax.ShapeDtypeStruct((T // ATT_BLOCKS, ATT_BLOCKS, ATTN_W), F32)
    outs = pl.pallas_call(
        body, name="attn_fwd", grid=(ATTN_W // LANES, nt),
        in_specs=[pl.BlockSpec((8, LANES), lambda c, t: (0, c))] + [_HBM] * (3 + ns),
        out_specs=[_HBM] * (2 + ns),
        out_shape=[view, view] + [_gathered_shape(s) for s in shards],
        scratch_shapes=[tile(2), tile(3), tile(3), tile(2), tile(2)] + [pltpu.VMEM((ATT_BLOCKS, CHUNK, LANES), F32)] * 6
        + [dma(2), dma(3), dma(3), dma(2), dma(2)] + (_gather_sems(ns) if ns else []),
        compiler_params=_params(("arbitrary", "arbitrary")),
    )(_slope_table(), _residue_view(q), _residue_view(k), _residue_view(v), *shards)
    return outs[0].reshape(T, ATTN_W), outs[1].reshape(T, ATTN_W), tuple(outs[2:])


def _group_mean(v, grp):
    out = jnp.zeros_like(v)
    for g in range(N_GROUPS):
        mk = grp == g
        s = jnp.sum(jnp.where(mk, v, 0.0), axis=-1, keepdims=True) * (1.0 / HEAD_DIM)
        out = jnp.where(mk, s, out)
    return out


def _gmlp_core(uu, zz, lg, lb, ws, sb_ref, grp):
    ug, tu = _gelu(uu)
    zg, tz = _gelu(zz)
    zc = zg - _group_mean(zg, grp)
    rstd = lax.rsqrt(_group_mean(zc * zc, grp) + EPS)
    xhat = zc * rstd
    zn16 = (xhat * lg + lb).astype(BF16)
    mixed = jnp.zeros_like(uu)
    for g in range(N_GROUPS):
        mixed = jnp.where(grp == g, _dot(ws[g], zn16) + sb_ref[:, g:g + 1], mixed)
    return ug, tu, tz, xhat, rstd, zn16, mixed


def _causal_ws(w_ref):
    ti = lax.broadcasted_iota(jnp.int32, (CHUNK, CHUNK), 0)
    si = lax.broadcasted_iota(jnp.int32, (CHUNK, CHUNK), 1)
    causal = si <= ti
    return causal, [jnp.where(causal, w_ref[g], 0.0).astype(BF16) for g in range(N_GROUPS)]


def _gmlp_fwd(u, z, ln_g, ln_b, sgu_w, sgu_bt):
    T = u.shape[0]
    tg = 512

    def body(u_ref, z_ref, g_ref, b_ref, w_ref, sb_ref, out_ref):
        grp = lax.broadcasted_iota(jnp.int32, (CHUNK, GMLP_W), 1) // HEAD_DIM
        _, ws = _causal_ws(w_ref)
        for ci in range(tg // CHUNK):
            rows = slice(ci * CHUNK, (ci + 1) * CHUNK)
            ug, _, _, _, _, _, mixed = _gmlp_core(u_ref[rows, :], z_ref[rows, :], g_ref[...], b_ref[...],
                                                  ws, sb_ref, grp)
            out_ref[rows, :] = ug * mixed

    return pl.pallas_call(
        body, name="gmlp_fwd", grid=(T // tg,),
        in_specs=[_rows(tg, GMLP_W), _rows(tg, GMLP_W), _resident((1, GMLP_W)), _resident((1, GMLP_W)),
                  _resident((N_GROUPS, CHUNK, CHUNK)), _resident((CHUNK, N_GROUPS))],
        out_specs=_rows(tg, GMLP_W),
        out_shape=jax.ShapeDtypeStruct((T, GMLP_W), F32),
        compiler_params=_params(("parallel",)),
    )(u, z, ln_g, ln_b, sgu_w, sgu_bt)


def _out_fwd(attn, gm, ga, gg, w_out, x, g2):
    T = x.shape[0]
    tm = TM_PROJ

    def body(a_ref, m_ref, ga_ref, gg_ref, w_ref, x_ref, g2_ref, mix_ref, h1_ref, hn2_ref):
        an, _ = _rms(a_ref[...])
        gn, _ = _rms(m_ref[...])
        an = (an * ga_ref[...]).astype(BF16)
        gn = (gn * gg_ref[...]).astype(BF16)
        mix_ref[:, 0:ATTN_W] = an
        mix_ref[:, ATTN_W:] = gn
        h1 = x_ref[...] + _dot(an, w_ref[0:ATTN_W, :]) + _dot(gn, w_ref[ATTN_W:, :])
        h1_ref[...] = h1
        n2, _ = _rms(h1)
        hn2_ref[...] = (n2 * g2_ref[...]).astype(BF16)

    sds = jax.ShapeDtypeStruct
    return pl.pallas_call(
        body, name="out_fwd", grid=(T // tm,),
        in_specs=[_rows(tm, ATTN_W), _rows(tm, GMLP_W), _resident((1, ATTN_W)), _resident((1, GMLP_W)),
                  _resident((D_MODEL, D_MODEL)), _rows(tm, D_MODEL), _resident((1, D_MODEL))],
        out_specs=[_rows(tm, D_MODEL)] * 3,
        out_shape=[sds((T, D_MODEL), BF16), sds((T, D_MODEL), F32), sds((T, D_MODEL), BF16)],
        compiler_params=_params(("parallel",)),
    )(attn, gm, ga, gg, w_out, x, g2)


def _ffn_fwd(hn2, h1, w1t, w2, gf, tgt):
    T = h1.shape[0]
    tm = TM_FFN

    def body(hn_ref, h1_ref, w1_ref, w2_ref, gf_ref, t_ref, r_ref, dhf_ref, dhb_ref, loss_ref, dgf_ref):
        i = pl.program_id(0)

        @pl.when(i == 0)
        def _():
            loss_ref[...] = jnp.zeros_like(loss_ref)
            dgf_ref[...] = jnp.zeros_like(dgf_ref)

        hn = hn_ref[...]
        acc = h1_ref[...]
        for j in range(D_FF // FF_CHUNK):
            cols = slice(j * FF_CHUNK, (j + 1) * FF_CHUNK)
            r = jnp.maximum(_dot(hn, w1_ref[cols, :], NT), 0.0)
            r_ref[:, cols] = r.astype(BF16)
            act = jnp.square(r).astype(BF16)
            acc = acc + _dot(act, w2_ref[cols, :])
        n3, r3 = _rms(acc)
        gf_row = gf_ref[...]
        e = n3 * gf_row - t_ref[...]
        loss_ref[...] += 0.5 * jnp.sum(jnp.mean(e * e, axis=-1, keepdims=True))
        dy = e * (1.0 / D_MODEL)
        _accum_rows(dgf_ref, dy * n3)
        dh2 = _rms_bwd(n3, r3, gf_row, dy)
        dhf_ref[...] = dh2
        dhb_ref[...] = dh2.astype(BF16)

    sds = jax.ShapeDtypeStruct
    acc_spec = lambda n: pl.BlockSpec((8, n), lambda i: (0, 0))
    return pl.pallas_call(
        body, name="ffn_fwd", grid=(T // tm,),
        in_specs=[_rows(tm, D_MODEL), _rows(tm, D_MODEL), _resident((D_FF, D_MODEL)), _resident((D_FF, D_MODEL)),
                  _resident((1, D_MODEL)), _rows(tm, D_MODEL)],
        out_specs=[_rows(tm, D_FF), _rows(tm, D_MODEL), _rows(tm, D_MODEL), acc_spec(LANES), acc_spec(D_MODEL)],
        out_shape=[sds((T, D_FF), BF16), sds((T, D_MODEL), F32), sds((T, D_MODEL), BF16),
                   sds((8, LANES), F32), sds((8, D_MODEL), F32)],
        compiler_params=_params(("arbitrary",)),
    )(hn2, h1, w1t, w2, gf, tgt)


def _ffn_bwd(dh2b, dh2f, relu, h1, g2, w2, w1t):
    T = h1.shape[0]
    tm = TM_FFN

    def body(db_ref, df_ref, r_ref, h1_ref, g2_ref, w2_ref, w1t_ref, da_ref, d1f_ref, d1b_ref, dg_ref):
        @pl.when(pl.program_id(0) == 0)
        def _():
            dg_ref[...] = jnp.zeros_like(dg_ref)

        db = db_ref[...]
        acc = jnp.zeros((tm, D_MODEL), F32)
        for j in range(D_FF // FF_CHUNK):
            cols = slice(j * FF_CHUNK, (j + 1) * FF_CHUNK)
            da = (_dot(db, w2_ref[cols, :], NT) * (2.0 * r_ref[:, cols].astype(F32))).astype(BF16)
            da_ref[:, cols] = da
            acc = acc + _dot(da, w1t_ref[cols, :])
        n2, r2 = _rms(h1_ref[...])
        _accum_rows(dg_ref, acc * n2)
        dh1 = df_ref[...] + _rms_bwd(n2, r2, g2_ref[...], acc)
        d1f_ref[...] = dh1
        d1b_ref[...] = dh1.astype(BF16)

    sds = jax.ShapeDtypeStruct
    return pl.pallas_call(
        body, name="ffn_bwd", grid=(T // tm,),
        in_specs=[_rows(tm, D_MODEL), _rows(tm, D_MODEL), _rows(tm, D_FF), _rows(tm, D_MODEL),
                  _resident((1, D_MODEL)), _resident((D_FF, D_MODEL)), _resident((D_FF, D_MODEL))],
        out_specs=[_rows(tm, D_FF), _rows(tm, D_MODEL), _rows(tm, D_MODEL),
                   pl.BlockSpec((8, D_MODEL), lambda i: (0, 0))],
        out_shape=[sds((T, D_FF), BF16), sds((T, D_MODEL), F32), sds((T, D_MODEL), BF16), sds((8, D_MODEL), F32)],
        compiler_params=_params(("arbitrary",)),
    )(dh2b, dh2f, relu, h1, g2, w2, w1t)


def _out_bwd(dh1b, w_out, attn, gm, ga, gg):
    T = attn.shape[0]
    tm = TM_PROJ

    def body(d_ref, w_ref, a_ref, m_ref, ga_ref, gg_ref, da_ref, dm_ref, dga_ref, dgg_ref):
        @pl.when(pl.program_id(0) == 0)
        def _():
            dga_ref[...] = jnp.zeros_like(dga_ref)
            dgg_ref[...] = jnp.zeros_like(dgg_ref)

        d = d_ref[...]
        dan = _dot(d, w_ref[0:ATTN_W, :], NT)
        dgn = _dot(d, w_ref[ATTN_W:, :], NT)
        na, ra = _rms(a_ref[...])
        ng, rg = _rms(m_ref[...])
        _accum_rows(dga_ref, dan * na)
        _accum_rows(dgg_ref, dgn * ng)
        da_ref[...] = _rms_bwd(na, ra, ga_ref[...], dan)
        dm_ref[...] = _rms_bwd(ng, rg, gg_ref[...], dgn)

    sds = jax.ShapeDtypeStruct
    return pl.pallas_call(
        body, name="out_bwd", grid=(T // tm,),
        in_specs=[_rows(tm, D_MODEL), _resident((D_MODEL, D_MODEL)), _rows(tm, ATTN_W), _rows(tm, GMLP_W),
                  _resident((1, ATTN_W)), _resident((1, GMLP_W))],
        out_specs=[_rows(tm, ATTN_W), _rows(tm, GMLP_W), pl.BlockSpec((8, ATTN_W), lambda i: (0, 0)),
                   pl.BlockSpec((8, GMLP_W), lambda i: (0, 0))],
        out_shape=[sds((T, ATTN_W), F32), sds((T, GMLP_W), F32), sds((8, ATTN_W), F32), sds((8, GMLP_W), F32)],
        compiler_params=_params(("arbitrary",)),
    )(dh1b, w_out, attn, gm, ga, gg)


def _gmlp_bwd(u, z, dgm, ln_g, ln_b, sgu_w, sgu_bt):
    T = u.shape[0]
    tg = 512
    nsteps = T // tg

    def body(u_ref, z_ref, d_ref, g_ref, b_ref, w_ref, sb_ref, duz_ref, dlg_ref, dlb_ref, dw_ref, dsb_ref):
        i = pl.program_id(0)

        @pl.when(i == 0)
        def _():
            for ref in (dlg_ref, dlb_ref, dw_ref, dsb_ref):
                ref[...] = jnp.zeros_like(ref)

        grp = lax.broadcasted_iota(jnp.int32, (CHUNK, GMLP_W), 1) // HEAD_DIM
        lane = lax.broadcasted_iota(jnp.int32, (CHUNK, LANES), 1)
        causal, ws = _causal_ws(w_ref)
        lg = g_ref[...]
        for ci in range(tg // CHUNK):
            rows = slice(ci * CHUNK, (ci + 1) * CHUNK)
            uu, zz = u_ref[rows, :], z_ref[rows, :]
            ug, tu, tz, xhat, rstd, zn16, mixed = _gmlp_core(uu, zz, lg, b_ref[...], ws, sb_ref, grp)
            dgm_c = d_ref[rows, :]
            dmx = dgm_c * ug
            duz_ref[rows, 0:GMLP_W] = dgm_c * mixed * _gelu_grad(uu, tu)
            dmx16 = dmx.astype(BF16)
            dzn = jnp.zeros_like(dmx)
            dsb = jnp.zeros((CHUNK, LANES), F32)
            for g in range(N_GROUPS):
                mk = grp == g
                dzn = jnp.where(mk, _dot(ws[g], dmx16, TN), dzn)
                dw_ref[g] += _dot(jnp.where(mk, dmx16, jnp.zeros_like(dmx16)), zn16, NT)
                dsb = jnp.where(lane == g, jnp.sum(jnp.where(mk, dmx, 0.0), axis=-1, keepdims=True), dsb)
            dsb_ref[...] += dsb
            _accum_rows(dlg_ref, dzn * xhat)
            _accum_rows(dlb_ref, dzn)
            dxh = dzn * lg
            dzg = rstd * (dxh - _group_mean(dxh, grp) - xhat * _group_mean(dxh * xhat, grp))
            duz_ref[rows, GMLP_W:] = dzg * _gelu_grad(zz, tz)

        @pl.when(i == nsteps - 1)
        def _():
            for g in range(N_GROUPS):
                dw_ref[g] = jnp.where(causal, dw_ref[g], 0.0)

    sds = jax.ShapeDtypeStruct
    return pl.pallas_call(
        body, name="gmlp_bwd", grid=(nsteps,),
        in_specs=[_rows(tg, GMLP_W)] * 3 + [_resident((1, GMLP_W)), _resident((1, GMLP_W)),
                                              _resident((N_GROUPS, CHUNK, CHUNK)), _resident((CHUNK, N_GROUPS))],
        out_specs=[_rows(tg, 2 * GMLP_W), pl.BlockSpec((8, GMLP_W), lambda i: (0, 0)),
                   pl.BlockSpec((8, GMLP_W), lambda i: (0, 0)),
                   pl.BlockSpec((N_GROUPS, CHUNK, CHUNK), lambda i: (0, 0, 0)),
                   pl.BlockSpec((CHUNK, LANES), lambda i: (0, 0))],
        out_shape=[sds((T, 2 * GMLP_W), F32), sds((8, GMLP_W), F32), sds((8, GMLP_W), F32),
                   sds((N_GROUPS, CHUNK, CHUNK), F32), sds((CHUNK, LANES), F32)],
        compiler_params=_params(("arbitrary",)),
    )(u, z, dgm, ln_g, ln_b, sgu_w, sgu_bt)


def _attn_bwd(q, k, v, dattn, attn, lse, owner_grads=()):
    T = q.shape[0]
    nt = T // ATT_TILE
    ns = len(owner_grads)
    steps = (ATTN_W // LANES) * nt

    def body(sl_ref, q_hbm, k_hbm, v_hbm, do_hbm, o_hbm, lse_hbm, *rest):
        p_refs, rest = rest[:ns], rest[ns:]
        dq_hbm = dk_hbm = dv_hbm = rest[0]
        r_refs, rest = rest[1:1 + ns], rest[1 + ns:]
        qbuf, dobuf, obuf, lbuf, kbuf, vbuf, dqbuf, dkbuf, dvbuf, delta_s = rest[:10]
        sem_q, sem_do, sem_o, sem_l, sem_k, sem_v, sem_dq, sem_dk, sem_dv = rest[10:19]
        hp, t = pl.program_id(0), pl.program_id(1)
        step = hp * nt + t
        two, three = step % 2, step % 3
        before, after = (step + 2) % 3, (step + 1) % 3
        if ns:
            start, finish = _owner_exchange_phases(p_refs, r_refs, *rest[19:])
            pl.when(step == 0)(start)

        def fetch(hp_, t_, two_, three_):
            for hbm, buf, sem, slot in ((q_hbm, qbuf, sem_q, two_), (do_hbm, dobuf, sem_do, two_),
                                        (o_hbm, obuf, sem_o, two_), (lse_hbm, lbuf, sem_l, two_),
                                        (k_hbm, kbuf, sem_k, three_), (v_hbm, vbuf, sem_v, three_)):
                for cp in _tile_copies(hbm, buf.at[slot], sem.at[slot], hp_, t_):
                    cp.start()

        @pl.when(step == 0)
        def _():
            kbuf[2] = jnp.zeros((ATT_BLOCKS, CHUNK, LANES), F32)
            vbuf[2] = jnp.zeros((ATT_BLOCKS, CHUNK, LANES), F32)
            dkbuf[3] = jnp.zeros((ATT_BLOCKS, CHUNK, LANES), F32)
            dvbuf[3] = jnp.zeros((ATT_BLOCKS, CHUNK, LANES), F32)
            fetch(0, 0, 0, 0)

        @pl.when(step + 1 < steps)
        def _():
            fetch((step + 1) // nt, (step + 1) % nt, 1 - two, after)

        for buf, sem in ((qbuf, sem_q), (dobuf, sem_do), (obuf, sem_o), (lbuf, sem_l)):
            _wait_tile(buf.at[two], sem.at[two])
        _wait_tile(kbuf.at[three], sem_k.at[three])
        _wait_tile(vbuf.at[three], sem_v.at[three])

        @pl.when(step >= 2)
        def _():
            _wait_tile(dqbuf.at[two], sem_dq.at[two])

        @pl.when(step >= 3)
        def _():
            _wait_tile(dkbuf.at[three], sem_dk.at[three])
            _wait_tile(dvbuf.at[three], sem_dv.at[three])

        zero_tile = jnp.zeros((ATT_BLOCKS, CHUNK, LANES), F32)
        dqbuf[two] = zero_tile
        dkbuf[three] = zero_tile
        dvbuf[three] = zero_tile

        q_t, do_t, l_t, k_t, v_t = qbuf.at[two], dobuf.at[two], lbuf.at[two], kbuf.at[three], vbuf.at[three]
        k_b, v_b = kbuf.at[before], vbuf.at[before]
        dq_t, dk_t, dv_t = dqbuf.at[two], dkbuf.at[three], dvbuf.at[three]
        dk_b, dv_b = dkbuf.at[before], dvbuf.at[before]
        sink = jnp.where(t > 0, before, 3)
        dk_sink, dv_sink = dkbuf.at[sink], dvbuf.at[sink]
        head0 = lax.broadcasted_iota(jnp.int32, (CHUNK, LANES), 1) < HEAD_DIM
        for r in range(ATT_BLOCKS):
            dd = dobuf[two, r] * obuf[two, r]
            d0 = jnp.sum(jnp.where(head0, dd, 0.0), axis=-1, keepdims=True)
            d1 = jnp.sum(jnp.where(head0, 0.0, dd), axis=-1, keepdims=True)
            delta_s[r] = jnp.where(head0, d0, d1)

        def column(xb):
            return jnp.concatenate([xb[:, 0:1], xb[:, HEAD_DIM:HEAD_DIM + 1]], axis=0)

        no_key_before = jnp.where(lax.broadcasted_iota(jnp.int32, (2 * CHUNK, 2 * CHUNK), 1) < CHUNK, NEG, 0.0)
        for d in DILATIONS:
            bias = _residue_bias(sl_ref, d)
            for j in range(ATT_BLOCKS):
                kcat = jnp.concatenate([_rm_block_before(k_t, k_b, d, j), _rm_block(k_t, d, j)], axis=0).astype(BF16)
                vcat = jnp.concatenate([_rm_block_before(v_t, v_b, d, j), _rm_block(v_t, d, j)], axis=0).astype(BF16)
                q2 = _stack_heads(_rm_block(q_t, d, j), head0)
                do2 = _stack_heads(_rm_block(do_t, d, j), head0)
                s = _dot(q2, kcat, NT) + bias
                if _first_in_tile(d, j):
                    s = s + jnp.where(t == 0, 1.0, 0.0) * no_key_before
                p = jnp.exp(s - column(_rm_block(l_t, d, j)))
                ds = (p * (_dot(do2, vcat, NT) - column(_rm_block(delta_s, d, j)))).astype(BF16)
                _rm_add(dq_t, _residue_rows(d, j), _unstack_heads(_dot(ds, kcat), head0))
                ck = _dot(ds, q2, TN)
                cv = _dot(p.astype(BF16), do2, TN)
                _rm_add(dk_t, _residue_rows(d, j), ck[CHUNK:, :])
                _rm_add(dv_t, _residue_rows(d, j), cv[CHUNK:, :])
                if _first_in_tile(d, j):
                    rows = [(r, CHUNK - n, n) for r, _, n in _residue_rows(d, j)]
                    _rm_add(dk_sink, rows, ck[:CHUNK, :])
                    _rm_add(dv_sink, rows, cv[:CHUNK, :])
                else:
                    rows = [(r, lo - n, n) for r, lo, n in _residue_rows(d, j)]
                    _rm_add(dk_t, rows, ck[:CHUNK, :])
                    _rm_add(dv_t, rows, cv[:CHUNK, :])

        for r in range(ATT_BLOCKS):
            dqbuf[two, r] = dqbuf[two, r] * Q_SCALE
        for cp in _tile_copies(dq_hbm, dq_t, sem_dq.at[two], hp, t, to_hbm=True):
            cp.start()

        @pl.when(t > 0)
        def _():
            for cp in (_tile_copies(dk_hbm, dk_b, sem_dk.at[before], hp, t - 1, to_hbm=True, lane0=ATTN_W)
                       + _tile_copies(dv_hbm, dv_b, sem_dv.at[before], hp, t - 1, to_hbm=True, lane0=2 * ATTN_W)):
                cp.start()

        @pl.when(t == nt - 1)
        def _():
            for cp in (_tile_copies(dk_hbm, dk_t, sem_dk.at[three], hp, t, to_hbm=True, lane0=ATTN_W)
                       + _tile_copies(dv_hbm, dv_t, sem_dv.at[three], hp, t, to_hbm=True, lane0=2 * ATTN_W)):
                cp.start()

        @pl.when(step == steps - 1)
        def _():
            for slot in range(2):
                _wait_tile(dqbuf.at[slot], sem_dq.at[slot])
            for slot in range(3):
                _wait_tile(dkbuf.at[slot], sem_dk.at[slot])
                _wait_tile(dvbuf.at[slot], sem_dv.at[slot])

        if ns:
            pl.when(step == steps - 1)(finish)

    tile = lambda n: pltpu.VMEM((n, ATT_BLOCKS, CHUNK, LANES), F32)
    dma = lambda n: pltpu.SemaphoreType.DMA((n,))
    view = jax.ShapeDtypeStruct((T // ATT_BLOCKS, ATT_BLOCKS, ATTN_W), F32)
    outs = pl.pallas_call(
        body, name="attn_bwd", grid=(ATTN_W // LANES, nt),
        in_specs=[pl.BlockSpec((8, LANES), lambda c, t: (0, c))] + [_HBM] * (6 + ns),
        out_specs=[_HBM] * (1 + ns),
        out_shape=[jax.ShapeDtypeStruct((T // ATT_BLOCKS, ATT_BLOCKS, 3 * ATTN_W), F32)]
        + [jax.ShapeDtypeStruct(p.shape, p.dtype) for p in owner_grads],
        scratch_shapes=[tile(2), tile(2), tile(2), tile(2), tile(3), tile(3), tile(2), tile(4), tile(4),
                        pltpu.VMEM((ATT_BLOCKS, CHUNK, LANES), F32)]
        + [dma(2), dma(2), dma(2), dma(2), dma(3), dma(3), dma(2), dma(3), dma(3)]
        + (_owner_exchange_sems(ns) if ns else []),
        compiler_params=_params(("arbitrary", "arbitrary")),
    )(_slope_table(), *[_residue_view(a) for a in (q, k, v, dattn, attn, lse)], *owner_grads)
    return outs[0].reshape(T, 3 * ATTN_W), tuple(outs[1:])


def _proj_bwd(dqkv, duz, w_in_t, x, g1, dh1, chip_sums=()):
    T = x.shape[0]
    tm = TM_PROJ
    ns = len(chip_sums)
    steps = T // tm
    n_qkv = dqkv.shape[1]

    def body(d_ref, duz_ref, w_ref, x_ref, g_ref, r_ref, *rest):
        p_refs, rest = rest[:ns], rest[ns:]
        dx_ref, dg_ref = rest[:2]
        r_refs, sems = rest[2:2 + ns], rest[2 + ns:]
        step = pl.program_id(0)
        if ns:
            start, finish = _chip_exchange_phases(p_refs, r_refs, *sems)
            pl.when(step == 0)(start)

        @pl.when(step == 0)
        def _():
            dg_ref[...] = jnp.zeros_like(dg_ref)

        dhn = (_dot(d_ref[...].astype(BF16), w_ref[0:n_qkv, :])
               + _dot(duz_ref[...].astype(BF16), w_ref[n_qkv:, :]))
        n1, r1 = _rms(x_ref[...])
        _accum_rows(dg_ref, dhn * n1)
        dx_ref[...] = r_ref[...] + _rms_bwd(n1, r1, g_ref[...], dhn)
        if ns:
            pl.when(step == steps - 1)(finish)

    outs = pl.pallas_call(
        body, name="proj_bwd", grid=(steps,),
        in_specs=[_rows(tm, n_qkv), _rows(tm, IN_W - n_qkv), _resident((IN_W, D_MODEL)), _rows(tm, D_MODEL),
                  _resident((1, D_MODEL)), _rows(tm, D_MODEL)] + [_HBM] * ns,
        out_specs=[_rows(tm, D_MODEL), pl.BlockSpec((8, D_MODEL), lambda i: (0, 0))] + [_HBM] * ns,
        out_shape=[jax.ShapeDtypeStruct((T, D_MODEL), F32), jax.ShapeDtypeStruct((8, D_MODEL), F32)]
        + [jax.ShapeDtypeStruct(p.shape, p.dtype) for p in chip_sums],
        scratch_shapes=_chip_exchange_sems(ns) if ns else [],
        compiler_params=_params(("arbitrary",)),
    )(dqkv, duz, w_in_t, x, g1, dh1, *chip_sums)
    return outs[0], outs[1], tuple(outs[2:])


def _dw(a, b, name, tile, square_a=False, out_dtype=F32, rows=None, row0=0, into=None):
    T, ka = a.shape
    nb = b.shape[1]
    tka, tnb, tt = tile
    tt = min(tt, T)
    last = T // tt - 1
    block0 = row0 // tka

    def body(a_ref, b_ref, *refs):
        refs = refs[1:] if into is not None else refs
        o_ref = refs[0]
        acc_ref = refs[1] if len(refs) > 1 else o_ref
        s = pl.program_id(2)

        @pl.when(s == 0)
        def _():
            acc_ref[...] = jnp.zeros_like(acc_ref)

        a_tile = a_ref[...]
        if square_a:
            a_tile = jnp.square(a_tile.astype(F32))
        acc_ref[...] += _dot(a_tile.astype(BF16), b_ref[...], TN)
        if acc_ref is not o_ref:
            @pl.when(s == last)
            def _():
                o_ref[...] = acc_ref[...].astype(out_dtype)

    return pl.pallas_call(
        body, name=name, grid=(ka // tka, nb // tnb, T // tt),
        in_specs=[pl.BlockSpec((tt, tka), lambda i, j, s: (s, i)), pl.BlockSpec((tt, tnb), lambda i, j, s: (s, j))]
        + ([_HBM] if into is not None else []),
        out_specs=pl.BlockSpec((tka, tnb), lambda i, j, s: (block0 + i, j)),
        out_shape=jax.ShapeDtypeStruct((rows or ka, nb), out_dtype),
        scratch_shapes=[] if out_dtype == F32 else [pltpu.VMEM((tka, tnb), F32)],
        input_output_aliases={2: 0} if into is not None else {},
        compiler_params=_params(("parallel", "parallel", "arbitrary")),
    )(a, b, *([into] if into is not None else []))


def _adamw(w, m, v, parts, name, tr, transposed=False):
    R, C = w.shape
    P = parts.shape[0]

    def body(w_ref, m_ref, v_ref, p_ref, g_ref, d_ref, m2_ref, v2_ref):
        g = p_ref[0].astype(F32)
        for i in range(1, P):
            g = g + p_ref[i].astype(F32)
        if transposed:
            g = g.T
        m2 = ADAM_B1 * m_ref[...] + (1.0 - ADAM_B1) * g
        v2 = ADAM_B2 * v_ref[...] + (1.0 - ADAM_B2) * jnp.square(g)
        m_hat = m2 / (1.0 - ADAM_B1 ** ADAM_STEP)
        v_hat = v2 / (1.0 - ADAM_B2 ** ADAM_STEP)
        g_ref[...] = g
        d_ref[...] = -ADAM_LR * (m_hat / (jnp.sqrt(v_hat) + ADAM_EPS) + ADAM_WD * w_ref[...])
        m2_ref[...] = m2
        v2_ref[...] = v2

    spec = _rows(tr, C)
    part_spec = (pl.BlockSpec((P, C, tr), lambda i: (0, 0, i)) if transposed
                 else pl.BlockSpec((P, tr, C), lambda i: (0, i, 0)))
    return pl.pallas_call(
        body, name=name, grid=(R // tr,),
        in_specs=[spec, spec, spec, part_spec],
        out_specs=[spec] * 4,
        out_shape=[jax.ShapeDtypeStruct((R, C), F32)] * 4,
        compiler_params=_params(("parallel",)),
    )(w, m, v, parts)


def _pair_sum(core, grad, recv, name):
    _, _, n, C = grad.shape
    tr = n // 2

    def body(c_ref, a_ref, b_ref, o_ref):
        o_ref[...] = a_ref[...] + b_ref[...]

    spec = pl.BlockSpec((1, tr, C), lambda i, j, c_ref: (i, j, 0))
    return pl.pallas_call(
        body, name=name,
        grid_spec=pltpu.PrefetchScalarGridSpec(
            num_scalar_prefetch=1, grid=(4, n // tr),
            in_specs=[pl.BlockSpec((1, None, tr, C), lambda i, j, c_ref: (i, c_ref[0], j, 0)), spec],
            out_specs=spec),
        out_shape=jax.ShapeDtypeStruct(recv.shape, F32),
        compiler_params=_params(("parallel", "parallel")),
    )(core.reshape(1), grad, recv)


_HBM = pl.BlockSpec(memory_space=pltpu.HBM)


def _place():
    return lax.axis_index("x"), lax.axis_index("y"), lax.axis_index("c")


def _gathered_shape(shard):
    return jax.ShapeDtypeStruct((N_DEV,) + shard.shape, shard.dtype)


def _gather_sems(n):
    return [pltpu.SemaphoreType.DMA((7, n)), pltpu.SemaphoreType.DMA((7, n)), pltpu.SemaphoreType.DMA((n,))]


def _gather_phases(x_refs, out_refs, send_sems, recv_sems, local_sems):
    x, y, c = _place()
    me, sibling = (x, y, c), (x, y, 1 - c)
    chips = [(1 - x, y), (x, 1 - y), (1 - x, 1 - y)]
    arrays = range(len(x_refs))

    def slot(i, px, py, pc):
        return out_refs[i].at[4 * px + 2 * py + pc]

    def copy(i, k, block, to, own=False):
        return pltpu.make_async_remote_copy(
            src_ref=x_refs[i] if own else slot(i, *block), dst_ref=slot(i, *block),
            send_sem=send_sems.at[k, i], recv_sem=recv_sems.at[k, i], device_id=to, device_id_type=MESH)

    def mine(i):
        return pltpu.make_async_copy(x_refs[i], slot(i, *me), local_sems.at[i])

    def start():
        for i in arrays:
            mine(i).start()
            copy(i, 0, me, sibling, own=True).start()
            for j, chip in enumerate(chips):
                copy(i, 1 + j, me, (*chip, c), own=True).start()

    def forward():
        for i in arrays:
            for j, chip in enumerate(chips):
                copy(i, 1 + j, (*chip, c), me).wait_recv()
                copy(i, 4 + j, (*chip, c), sibling).start()

    def finish():
        for i in arrays:
            copy(i, 0, sibling, me).wait_recv()
            copy(i, 0, me, sibling, own=True).wait_send()
            for j, chip in enumerate(chips):
                copy(i, 4 + j, (*chip, 1 - c), me).wait_recv()
                copy(i, 1 + j, me, (*chip, c), own=True).wait_send()
                copy(i, 4 + j, (*chip, c), sibling).wait_send()
            mine(i).wait()

    return start, forward, finish


def _all_gather(shards, name):
    n = len(shards)

    def body(*refs):
        start, forward, finish = _gather_phases(refs[:n], refs[n:2 * n], *refs[2 * n:])
        start()
        forward()
        finish()

    return pl.pallas_call(
        body, name=name,
        out_shape=[_gathered_shape(s) for s in shards],
        in_specs=[_HBM] * n, out_specs=[_HBM] * n,
        scratch_shapes=_gather_sems(n),
    )(*shards)


def _sibling_exchange(grads, name):
    n = len(grads)

    def body(*refs):
        g_refs, r_refs, send_sems, recv_sems = refs[:n], refs[n:2 * n], refs[2 * n], refs[2 * n + 1]
        x, y, c = _place()
        copies = [pltpu.make_async_remote_copy(
            src_ref=g_refs[i].at[:, 1 - c], dst_ref=r_refs[i], send_sem=send_sems.at[i], recv_sem=recv_sems.at[i],
            device_id=(x, y, 1 - c), device_id_type=MESH) for i in range(n)]
        for cp in copies:
            cp.start()
        for cp in copies:
            cp.wait()

    return pl.pallas_call(
        body, name=name,
        out_shape=[jax.ShapeDtypeStruct((g.shape[0],) + g.shape[2:], g.dtype) for g in grads],
        in_specs=[_HBM] * n, out_specs=[_HBM] * n,
        scratch_shapes=[pltpu.SemaphoreType.DMA((n,)), pltpu.SemaphoreType.DMA((n,))],
    )(*grads)


def _owner_exchange_sems(n):
    return [pltpu.SemaphoreType.DMA((7, n)), pltpu.SemaphoreType.DMA((7, n)), pltpu.SemaphoreType.DMA((n,))]


def _owner_exchange_phases(g_refs, r_refs, send_sems, recv_sems, local_sems):
    x, y, c = _place()
    me = 4 * x + 2 * y + c
    flip = lambda v, bit: 1 - v if bit else v
    peers = [(flip(x, k & 4), flip(y, k & 2), flip(c, k & 1)) for k in range(1, N_DEV)]
    arrays = range(len(g_refs))

    def mine(i):
        return pltpu.make_async_copy(g_refs[i].at[me], r_refs[i].at[me], local_sems.at[i])

    def copy(i, k, src_slot, dst_slot):
        return pltpu.make_async_remote_copy(
            src_ref=g_refs[i].at[src_slot], dst_ref=r_refs[i].at[dst_slot],
            send_sem=send_sems.at[k, i], recv_sem=recv_sems.at[k, i], device_id=peers[k], device_id_type=MESH)

    def start():
        for i in arrays:
            mine(i).start()
            for k, (px, py, pc) in enumerate(peers):
                copy(i, k, 4 * px + 2 * py + pc, me).start()

    def finish():
        for i in arrays:
            for k, (px, py, pc) in enumerate(peers):
                copy(i, k, me, 4 * px + 2 * py + pc).wait_recv()
                copy(i, k, 4 * px + 2 * py + pc, me).wait_send()
            mine(i).wait()

    return start, finish


def _chip_exchange_sems(n):
    return [pltpu.SemaphoreType.DMA((3, n)), pltpu.SemaphoreType.DMA((3, n)), pltpu.SemaphoreType.DMA((n,))]


def _chip_exchange_phases(p_refs, r_refs, send_sems, recv_sems, local_sems):
    x, y, c = _place()
    my_chip = 2 * x + y
    chips = [(1 - x, y), (x, 1 - y), (1 - x, 1 - y)]
    arrays = range(len(p_refs))

    def mine(i):
        return pltpu.make_async_copy(p_refs[i].at[my_chip], r_refs[i].at[my_chip], local_sems.at[i])

    def copy(i, k, src_chip, dst_chip):
        px, py = chips[k]
        return pltpu.make_async_remote_copy(
            src_ref=p_refs[i].at[src_chip], dst_ref=r_refs[i].at[dst_chip],
            send_sem=send_sems.at[k, i], recv_sem=recv_sems.at[k, i], device_id=(px, py, c), device_id_type=MESH)

    def start():
        for i in arrays:
            mine(i).start()
            for k, (px, py) in enumerate(chips):
                copy(i, k, 2 * px + py, my_chip).start()

    def finish():
        for i in arrays:
            for k, (px, py) in enumerate(chips):
                copy(i, k, my_chip, 2 * px + py).wait_recv()
                copy(i, k, 2 * px + py, my_chip).wait_send()
            mine(i).wait()

    return start, finish


_R_IN, _R_OUT, _R_FF = IN_W // N_DEV, D_MODEL // N_DEV, D_FF // N_DEV


def _by_owner(g):
    return g.reshape(4, 2, g.shape[0] // N_DEV, D_MODEL)


def _local_step(x, tgt, small, w_in_t, rest, core=None):
    exchange = core is not None
    g1, g2, gf = small["norm1_g"], small["norm2_g"], small["final_norm_g"].reshape(1, D_MODEL)
    ga, gg = small["attn_out_g"], small["gmlp_out_g"]
    ln_g = small["sgu_ln_g"].reshape(1, GMLP_W)
    ln_b = small["sgu_ln_b"].reshape(1, GMLP_W)
    sgu_w = small["sgu_w"][0]
    sgu_bt = small["sgu_b"][0].T

    hn1, q, k, v, u, z = _proj_fwd(x, g1, w_in_t)
    attn, lse, gathered = _attn_fwd(q, k, v, shards=rest if exchange else ())
    w_out, w_ff1_t, w_ff2 = [g.reshape(-1, D_MODEL) for g in gathered] if exchange else rest
    gm = _gmlp_fwd(u, z, ln_g, ln_b, sgu_w, sgu_bt)
    mixed, h1, hn2 = _out_fwd(attn, gm, ga, gg, w_out, x, g2)
    relu, dh2f, dh2b, loss8, dgf8 = _ffn_fwd(hn2, h1, w_ff1_t, w_ff2, gf, tgt)

    da, dh1f, dh1b, dg2 = _ffn_bwd(dh2b, dh2f, relu, h1, g2, w_ff2, w_ff1_t)
    wire = BF16 if exchange else F32
    dw_ff2 = _dw(relu, dh2b, "dw_ff2", DW_TILE, square_a=True, out_dtype=wire)
    dw_ff1_t = _dw(da, hn2, "dw_ff1", DW_TILE, out_dtype=wire)
    dattn, dgm, dga, dgg = _out_bwd(dh1b, w_out, attn, gm, ga, gg)
    dw_out = _dw(mixed, dh1b, "dw_out", DW_TILE, out_dtype=wire)
    early = [dw_out, dw_ff1_t, dw_ff2]
    if exchange:
        early = [g.reshape(N_DEV, -1, D_MODEL) for g in early]
    duz, dlg, dlb, dsw, dsb = _gmlp_bwd(u, z, dgm, ln_g, ln_b, sgu_w, sgu_bt)
    dqkv, arrived = _attn_bwd(q, k, v, dattn, attn, lse, owner_grads=early if exchange else ())
    dw_in_t = _dw(dqkv, hn1, "dw_in_qkv", DW_TILE_QKV, rows=IN_W)
    dw_in_t = _dw(duz, hn1, "dw_in_uz", DW_TILE_UZ, rows=IN_W, row0=3 * ATTN_W, into=dw_in_t)
    late = ()
    if exchange:
        by_owner = _by_owner(dw_in_t)
        got, = _sibling_exchange([by_owner], "grad_sibling_exchange")
        late = (_pair_sum(core, by_owner, got, "grad_pair_sum"),)
    dx, dg1, late = _proj_bwd(dqkv, duz, w_in_t, x, g1, dh1f, chip_sums=late)
    if exchange:
        dw_in_t, early = late[0], arrived

    small_grads = dict(
        norm1_g=dg1[0], sgu_ln_g=dlg[0], sgu_ln_b=dlb[0], sgu_w=dsw, sgu_b=dsb[:, :N_GROUPS].T,
        attn_out_g=dga[0], gmlp_out_g=dgg[0], norm2_g=dg2[0], final_norm_g=dgf8[0])
    return loss8[0, 0], dx, (dw_in_t, *early), small_grads


SMALL_NAMES = ("norm1_g", "sgu_ln_g", "sgu_ln_b", "sgu_w", "sgu_b", "attn_out_g", "gmlp_out_g", "norm2_g",
               "final_norm_g")
WEIGHT_ORDER = ("norm1_g", "w_in", "sgu_ln_g", "sgu_ln_b", "sgu_w", "sgu_b", "attn_out_g", "gmlp_out_g", "w_out",
                "norm2_g", "w_ff1", "w_ff2", "final_norm_g")


TINY_NAMES = tuple(n for n in SMALL_NAMES if n != "sgu_w")
TINY_ROWS = 48


def _pack_tiny(d, last):
    return jnp.concatenate([d[n].reshape(-1, LANES) for n in TINY_NAMES] + [last], axis=0)


def _unpack_tiny(p, like):
    out, r = {}, 0
    for n in TINY_NAMES:
        rows = like[n].size // LANES
        out[n] = p[r:r + rows].reshape(like[n].shape)
        r += rows
    return out


def kernel(x, norm1_g, w_in, sgu_ln_g, sgu_ln_b, sgu_w, sgu_b, attn_out_g, gmlp_out_g, w_out, norm2_g, w_ff1, w_ff2, final_norm_g, loss_target, m_norm1_g, m_w_in, m_sgu_ln_g, m_sgu_ln_b, m_sgu_w, m_sgu_b, m_attn_out_g, m_gmlp_out_g, m_w_out, m_norm2_g, m_w_ff1, m_w_ff2, m_final_norm_g, v_norm1_g, v_w_in, v_sgu_ln_g, v_sgu_ln_b, v_sgu_w, v_sgu_b, v_attn_out_g, v_gmlp_out_g, v_w_out, v_norm2_g, v_w_ff1, v_w_ff2, v_final_norm_g):
    w = dict(norm1_g=norm1_g, w_in=w_in, sgu_ln_g=sgu_ln_g, sgu_ln_b=sgu_ln_b, sgu_w=sgu_w, sgu_b=sgu_b,
             attn_out_g=attn_out_g, gmlp_out_g=gmlp_out_g, w_out=w_out, norm2_g=norm2_g, w_ff1=w_ff1, w_ff2=w_ff2,
             final_norm_g=final_norm_g)
    m = dict(norm1_g=m_norm1_g, w_in=m_w_in, sgu_ln_g=m_sgu_ln_g, sgu_ln_b=m_sgu_ln_b, sgu_w=m_sgu_w, sgu_b=m_sgu_b,
             attn_out_g=m_attn_out_g, gmlp_out_g=m_gmlp_out_g, w_out=m_w_out, norm2_g=m_norm2_g, w_ff1=m_w_ff1,
             w_ff2=m_w_ff2, final_norm_g=m_final_norm_g)
    v = dict(norm1_g=v_norm1_g, w_in=v_w_in, sgu_ln_g=v_sgu_ln_g, sgu_ln_b=v_sgu_ln_b, sgu_w=v_sgu_w, sgu_b=v_sgu_b,
             attn_out_g=v_attn_out_g, gmlp_out_g=v_gmlp_out_g, w_out=v_w_out, norm2_g=v_norm2_g, w_ff1=v_w_ff1,
             w_ff2=v_w_ff2, final_norm_g=v_final_norm_g)
    big = ("w_in", "w_out", "w_ff1", "w_ff2")
    core = lax.axis_index("c")

    w_in_t, = _all_gather([w_in[0].T.astype(BF16)], "w_in_all_gather")
    rest = (w_out[0].astype(BF16), w_ff1[0].T.astype(BF16), w_ff2[0].astype(BF16))
    loss, dx, parts, small_grads = _local_step(x[0], loss_target[0], {n: w[n] for n in SMALL_NAMES},
                                               w_in_t.reshape(IN_W, D_MODEL), rest, core=core)

    new = {}
    for n, p, transposed, tr in zip(big, parts, (True, False, True, False), (128, 128, 128, 256)):
        new[n] = [a[None] for a in _adamw(w[n][0], m[n][0], v[n][0], p, "adamw_" + n, tr, transposed)]

    flat = lambda a: a.reshape(-1, LANES)
    tiny_parts, sgu_parts = _all_gather(
        [_pack_tiny(small_grads, jnp.full((8, LANES), loss, F32)), flat(small_grads["sgu_w"])], "small_grad_all_gather")
    pad = jnp.ones((8, LANES), F32)
    tiny = _adamw(_pack_tiny(w, pad), _pack_tiny(m, pad), _pack_tiny(v, pad), tiny_parts, "adamw_tiny", TINY_ROWS)
    sgu = _adamw(flat(sgu_w), flat(m_sgu_w), flat(v_sgu_w), sgu_parts, "adamw_sgu_w", 512)
    loss = tiny[0][TINY_ROWS - 8, 0]

    outs = []
    for i in range(4):
        d = {n: new[n][i] for n in big}
        d.update(_unpack_tiny(tiny[i], w))
        d["sgu_w"] = sgu[i].reshape(sgu_w.shape)
        outs.extend(d[n] for n in WEIGHT_ORDER)
    return (loss, dx[None], *outs)
```

```python
import functools
import math

import numpy as np
import jax
import jax.numpy as jnp
from jax import lax
from jax.experimental import pallas as pl
from jax.experimental.pallas import tpu as pltpu

F32 = jnp.float32
BF16 = jnp.bfloat16

D_MODEL = 1024
HEAD_DIM = 64
N_HEADS = 12
ATTN_W = N_HEADS * HEAD_DIM
N_GROUPS = 4
GMLP_W = N_GROUPS * HEAD_DIM
IN_W = 3 * ATTN_W + 2 * GMLP_W
D_FF = 4 * D_MODEL
CHUNK = 128
DILATIONS = (1, 4, 16)
EPS = 1e-6
Q_SCALE = HEAD_DIM ** -0.5
NEG = -1e30

ADAM_LR, ADAM_B1, ADAM_B2, ADAM_EPS, ADAM_WD, ADAM_STEP = 0.001, 0.9, 0.999, 1e-08, 0.01, 10

N_DEV = 8
LANES = 128
VMEM_LIMIT = 56 << 20

TM_PROJ = 512
TM_FFN = 512
FF_CHUNK = 512
TM_GMLP = 1024
DW_TILE = (512, 1024, 4096)
DW_TILE_IN = (IN_W // 2, 1024, 2048)

MESH = pl.DeviceIdType.MESH


def _alibi_slopes(n):
    def pow2(m):
        start = 2.0 ** (-8.0 / m)
        return [start ** (i + 1) for i in range(m)]
    c = 2 ** int(math.floor(math.log2(n)))
    s = pow2(n) if c == n else pow2(c) + pow2(2 * c)[0::2][: n - c]
    return np.asarray(s, dtype=np.float32)


SLOPES = _alibi_slopes(N_HEADS)


def _params(sem=None):
    kw = dict(vmem_limit_bytes=VMEM_LIMIT)
    if sem is not None:
        kw["dimension_semantics"] = sem
    return pltpu.CompilerParams(**kw)


def _rows(tm, n):
    return pl.BlockSpec((tm, n), lambda i: (i, 0))


def _resident(shape):
    return pl.BlockSpec(shape, lambda *_: (0,) * len(shape), pipeline_mode=pl.Buffered(1))


def _rms(x):
    r = lax.rsqrt(jnp.mean(x * x, axis=-1, keepdims=True) + EPS)
    return x * r, r


def _rms_bwd(n, r, g, dy):
    dn = dy * g
    return r * (dn - n * jnp.mean(dn * n, axis=-1, keepdims=True))


def _accum_rows(acc_ref, v):
    acc_ref[...] += jnp.broadcast_to(jnp.sum(v, axis=0, keepdims=True), acc_ref.shape)


_G0 = math.sqrt(2.0 / math.pi)
_G1 = 0.044715


def _gelu(x):
    t = jnp.tanh(_G0 * (x + _G1 * (x * x * x)))
    return x * (0.5 * (1.0 + t)), t


def _gelu_grad(x, t):
    return 0.5 * (1.0 + t) + 0.5 * x * (1.0 - t * t) * (_G0 * (1.0 + 3.0 * _G1 * x * x))


NT = (((1,), (1,)), ((), ()))
TN = (((0,), (0,)), ((), ()))


def _dot(a, b, dims=None):
    if dims is None:
        return jnp.dot(a, b, preferred_element_type=F32)
    return lax.dot_general(a, b, dims, preferred_element_type=F32)


def _proj_fwd(x, g1, w_in_t):
    T = x.shape[0]
    tm = TM_PROJ

    def body(x_ref, g_ref, w_ref, hn_ref, q_ref, k_ref, v_ref, u_ref, z_ref):
        n, _ = _rms(x_ref[...])
        hn = (n * g_ref[...]).astype(BF16)
        hn_ref[...] = hn
        a = ATTN_W
        q_ref[...] = _dot(hn, w_ref[0:a, :], NT) * Q_SCALE
        k_ref[...] = _dot(hn, w_ref[a:2 * a, :], NT)
        v_ref[...] = _dot(hn, w_ref[2 * a:3 * a, :], NT)
        u_ref[...] = _dot(hn, w_ref[3 * a:3 * a + GMLP_W, :], NT)
        z_ref[...] = _dot(hn, w_ref[3 * a + GMLP_W:, :], NT)

    sds = jax.ShapeDtypeStruct
    return pl.pallas_call(
        body, name="proj_fwd", grid=(T // tm,),
        in_specs=[_rows(tm, D_MODEL), _resident((1, D_MODEL)), _resident((IN_W, D_MODEL))],
        out_specs=[_rows(tm, D_MODEL), _rows(tm, ATTN_W), _rows(tm, ATTN_W), _rows(tm, ATTN_W),
                   _rows(tm, GMLP_W), _rows(tm, GMLP_W)],
        out_shape=[sds((T, D_MODEL), BF16), sds((T, ATTN_W), F32), sds((T, ATTN_W), F32),
                   sds((T, ATTN_W), F32), sds((T, GMLP_W), F32), sds((T, GMLP_W), F32)],
        compiler_params=_params(("parallel",)),
    )(x, g1, w_in_t)


ATT_TILE = 2048
ATT_BLOCKS = ATT_TILE // CHUNK
SM_BLOCKS = 4


def _slope_table():
    row = np.repeat(SLOPES, HEAD_DIM)
    return jnp.asarray(np.broadcast_to(row[None], (8, ATTN_W)), F32)


def _residue_view(a):
    return a.reshape(a.shape[0] // ATT_BLOCKS, ATT_BLOCKS, a.shape[1])


def _tile_copies(hbm, buf, sem, hp, t, to_hbm=False, lane0=0):
    rows = pl.ds(pl.multiple_of(t * CHUNK, CHUNK), CHUNK)
    lanes = pl.ds(pl.multiple_of(lane0 + hp * LANES, LANES), LANES)
    pairs = [(hbm.at[rows, r, lanes], buf.at[r]) for r in range(ATT_BLOCKS)]
    return [pltpu.make_async_copy(v, h, sem) if to_hbm else pltpu.make_async_copy(h, v, sem) for h, v in pairs]


def _wait_tile(buf, sem):
    pltpu.make_async_copy(buf, buf, sem).wait()


def _residue_rows(d, j):
    if d == 16:
        return [(j, 0, CHUNK)]
    if d == 4:
        return [(j % 4 + 4 * m, 32 * (j // 4), 32) for m in range(4)]
    return [(r, 8 * j, 8) for r in range(ATT_BLOCKS)]


def _block_order(p, d):
    if d == 16:
        return p
    if d == 4:
        return 4 * (p & 31) + (p >> 5)
    return 16 * (p & 7) + (p >> 3)


def _first_in_tile(d, j):
    return _residue_rows(d, j)[0][1] == 0


def _rm_block(buf, d, j):
    return jnp.concatenate([buf[r, lo:lo + n, :] for r, lo, n in _residue_rows(d, j)], axis=0)


def _rm_block_before(buf, buf_before, d, j):
    if _first_in_tile(d, j):
        return jnp.concatenate([buf_before[r, CHUNK - n:CHUNK, :] for r, _, n in _residue_rows(d, j)], axis=0)
    return jnp.concatenate([buf[r, lo - n:lo, :] for r, lo, n in _residue_rows(d, j)], axis=0)


def _rm_store(buf, d, j, val):
    at = 0
    for r, lo, n in _residue_rows(d, j):
        buf[r, lo:lo + n, :] = val[at:at + n, :]
        at += n


def _rm_add(buf, rows, val):
    at = 0
    for r, lo, n in rows:
        buf[r, lo:lo + n, :] += val[at:at + n, :]
        at += n


def _residue_bias(sl_ref, d):
    shape = (2 * CHUNK, 2 * CHUNK)
    row = lax.broadcasted_iota(jnp.int32, shape, 0)
    col = lax.broadcasted_iota(jnp.int32, shape, 1)
    steps = _block_order(row & (CHUNK - 1), d) + CHUNK - (_block_order(col & (CHUNK - 1), d) + (col & CHUNK))
    band = (steps >= 0) & (steps <= CHUNK)
    sl = sl_ref[0:1, :]
    upper = lax.broadcasted_iota(jnp.int32, (2 * CHUNK, 1), 0) < CHUNK
    slope2 = jnp.where(upper, sl[:, 0:1], sl[:, HEAD_DIM:HEAD_DIM + 1])
    return jnp.where(band, -(float(d) * slope2 * steps.astype(F32)), NEG)


def _stack_heads(xb, head0):
    zero = jnp.zeros_like(xb)
    return jnp.concatenate([jnp.where(head0, xb, zero), jnp.where(head0, zero, xb)], axis=0).astype(BF16)


def _unstack_heads(x2, head0):
    return jnp.where(head0, x2[:CHUNK, :], x2[CHUNK:, :])


def _attn_fwd(q, k, v, shards=()):
    T = q.shape[0]
    nt = T // ATT_TILE
    ns = len(shards)
    steps = (ATTN_W // LANES) * nt

    def body(sl_ref, q_hbm, k_hbm, v_hbm, *rest):
        x_refs, rest = rest[:ns], rest[ns:]
        attn_hbm, lse_hbm = rest[:2]
        g_refs, rest = rest[2:2 + ns], rest[2 + ns:]
        qbuf, kbuf, vbuf, obuf, lbuf = rest[:5]
        o_acc, l_acc = rest[5:8], rest[8:11]
        sem_q, sem_k, sem_v, sem_o, sem_l = rest[11:16]
        hp, t = pl.program_id(0), pl.program_id(1)
        step = hp * nt + t
        two, three = step % 2, step % 3
        before, after = (step + 2) % 3, (step + 1) % 3
        if ns:
            start, forward, finish = _gather_phases(x_refs, g_refs, *rest[16:])
            pl.when(step == 0)(start)
            pl.when(step == steps // 2)(forward)

        def fetch(hp_, t_, two_, three_):
            for cp in (_tile_copies(q_hbm, qbuf.at[two_], sem_q.at[two_], hp_, t_)
                       + _tile_copies(k_hbm, kbuf.at[three_], sem_k.at[three_], hp_, t_)
                       + _tile_copies(v_hbm, vbuf.at[three_], sem_v.at[three_], hp_, t_)):
                cp.start()

        @pl.when(step == 0)
        def _():
            kbuf[2] = jnp.zeros((ATT_BLOCKS, CHUNK, LANES), F32)
            vbuf[2] = jnp.zeros((ATT_BLOCKS, CHUNK, LANES), F32)
            fetch(0, 0, 0, 0)

        @pl.when(step + 1 < steps)
        def _():
            fetch((step + 1) // nt, (step + 1) % nt, 1 - two, after)

        _wait_tile(qbuf.at[two], sem_q.at[two])
        _wait_tile(kbuf.at[three], sem_k.at[three])
        _wait_tile(vbuf.at[three], sem_v.at[three])

        @pl.when(step >= 2)
        def _():
            _wait_tile(obuf.at[two], sem_o.at[two])
            _wait_tile(lbuf.at[two], sem_l.at[two])

        q_t, k_t, v_t = qbuf.at[two], kbuf.at[three], vbuf.at[three]
        k_b, v_b = kbuf.at[before], vbuf.at[before]
        head0 = lax.broadcasted_iota(jnp.int32, (CHUNK, LANES), 1) < HEAD_DIM
        no_key_before = jnp.where(lax.broadcasted_iota(jnp.int32, (2 * CHUNK, 2 * CHUNK), 1) < CHUNK, NEG, 0.0)
        for pi, d in enumerate(DILATIONS):
            bias = _residue_bias(sl_ref, d)

            def scores(j, d=d, bias=bias):
                kcat = jnp.concatenate([_rm_block_before(k_t, k_b, d, j), _rm_block(k_t, d, j)], axis=0).astype(BF16)
                vcat = jnp.concatenate([_rm_block_before(v_t, v_b, d, j), _rm_block(v_t, d, j)], axis=0).astype(BF16)
                s = _dot(_stack_heads(_rm_block(q_t, d, j), head0), kcat, NT) + bias
                if _first_in_tile(d, j):
                    s = s + jnp.where(t == 0, 1.0, 0.0) * no_key_before
                return s, vcat

            def output(j, p, vcat, scale, lse, d=d, pi=pi):
                _rm_store(o_acc[pi], d, j, _unstack_heads(_dot(p, vcat) * scale, head0))
                _rm_store(l_acc[pi], d, j, _unstack_heads(jnp.broadcast_to(lse, (2 * CHUNK, LANES)), head0))

            for j0 in range(0, ATT_BLOCKS, SM_BLOCKS):
                group = [scores(j) for j in range(j0, j0 + SM_BLOCKS)]
                s = jnp.concatenate([g[0] for g in group], axis=0)
                m = jnp.max(s, axis=-1, keepdims=True)
                p = jnp.exp(s - m)
                l = jnp.sum(p, axis=-1, keepdims=True)
                p, scale, lse = p.astype(BF16), 1.0 / l, m + jnp.log(l)
                for i, (_, vcat) in enumerate(group):
                    rows = slice(i * 2 * CHUNK, (i + 1) * 2 * CHUNK)
                    output(j0 + i, p[rows, :], vcat, scale[rows, :], lse[rows, :])

        for r in range(ATT_BLOCKS):
            a, b, c = l_acc[0][r], l_acc[1][r], l_acc[2][r]
            m = jnp.maximum(jnp.maximum(a, b), c)
            ea, eb, ec = jnp.exp(a - m), jnp.exp(b - m), jnp.exp(c - m)
            tot = ea + eb + ec
            obuf[two, r] = (ea * o_acc[0][r] + eb * o_acc[1][r] + ec * o_acc[2][r]) / tot
            lbuf[two, r] = m + jnp.log(tot)

        for cp in (_tile_copies(attn_hbm, obuf.at[two], sem_o.at[two], hp, t, to_hbm=True)
                   + _tile_copies(lse_hbm, lbuf.at[two], sem_l.at[two], hp, t, to_hbm=True)):
            cp.start()

        @pl.when(step == steps - 1)
        def _():
            for slot in (two, 1 - two)[:min(steps, 2)]:
                _wait_tile(obuf.at[slot], sem_o.at[slot])
                _wait_tile(lbuf.at[slot], sem_l.at[slot])

        if ns:
            pl.when(step == steps - 1)(finish)

    tile = lambda n: pltpu.VMEM((n, ATT_BLOCKS, CHUNK, LANES), F32)
    dma = lambda n: pltpu.SemaphoreType.DMA((n,))
    view = jax.ShapeDtypeStruct((T // ATT_BLOCKS, ATT_BLOCKS, ATTN_W), F32)
    outs = pl.pallas_call(
        body, name="attn_fwd", grid=(ATTN_W // LANES, nt),
        in_specs=[pl.BlockSpec((8, LANES), lambda c, t: (0, c))] + [_HBM] * (3 + ns),
        out_specs=[_HBM] * (2 + ns),
        out_shape=[view, view] + [_gathered_shape(s) for s in shards],
        scratch_shapes=[tile(2), tile(3), tile(3), tile(2), tile(2)] + [pltpu.VMEM((ATT_BLOCKS, CHUNK, LANES), F32)] * 6
        + [dma(2), dma(3), dma(3), dma(2), dma(2)] + (_gather_sems(ns) if ns else []),
        compiler_params=_params(("arbitrary", "arbitrary")),
    )(_slope_table(), _residue_view(q), _residue_view(k), _residue_view(v), *shards)
    return outs[0].reshape(T, ATTN_W), outs[1].reshape(T, ATTN_W), tuple(outs[2:])


def _group_mean(v, grp):
    out = jnp.zeros_like(v)
    for g in range(N_GROUPS):
        mk = grp == g
        s = jnp.sum(jnp.where(mk, v, 0.0), axis=-1, keepdims=True) * (1.0 / HEAD_DIM)
        out = jnp.where(mk, s, out)
    return out


def _gmlp_core(uu, zz, lg, lb, ws, sb_ref, grp):
    ug, tu = _gelu(uu)
    zg, tz = _gelu(zz)
    zc = zg - _group_mean(zg, grp)
    rstd = lax.rsqrt(_group_mean(zc * zc, grp) + EPS)
    xhat = zc * rstd
    zn16 = (xhat * lg + lb).astype(BF16)
    mixed = []
    for ci in range(uu.shape[0] // CHUNK):
        rows = slice(ci * CHUNK, (ci + 1) * CHUNK)
        m = jnp.zeros((CHUNK, GMLP_W), F32)
        for g in range(N_GROUPS):
            m = jnp.where(grp[:CHUNK] == g, _dot(ws[g], zn16[rows, :]) + sb_ref[:, g:g + 1], m)
        mixed.append(m)
    return ug, tu, tz, xhat, rstd, zn16, jnp.concatenate(mixed, axis=0)


def _causal_ws(w_ref):
    ti = lax.broadcasted_iota(jnp.int32, (CHUNK, CHUNK), 0)
    si = lax.broadcasted_iota(jnp.int32, (CHUNK, CHUNK), 1)
    causal = si <= ti
    return causal, [jnp.where(causal, w_ref[g], 0.0).astype(BF16) for g in range(N_GROUPS)]


def _gmlp_fwd(u, z, ln_g, ln_b, sgu_w, sgu_bt):
    T = u.shape[0]
    tg = TM_GMLP

    def body(u_ref, z_ref, g_ref, b_ref, w_ref, sb_ref, out_ref):
        grp = lax.broadcasted_iota(jnp.int32, (tg, GMLP_W), 1) // HEAD_DIM
        _, ws = _causal_ws(w_ref)
        ug, _, _, _, _, _, mixed = _gmlp_core(u_ref[...], z_ref[...], g_ref[...], b_ref[...], ws, sb_ref, grp)
        out_ref[...] = ug * mixed

    return pl.pallas_call(
        body, name="gmlp_fwd", grid=(T // tg,),
        in_specs=[_rows(tg, GMLP_W), _rows(tg, GMLP_W), _resident((1, GMLP_W)), _resident((1, GMLP_W)),
                  _resident((N_GROUPS, CHUNK, CHUNK)), _resident((CHUNK, N_GROUPS))],
        out_specs=_rows(tg, GMLP_W),
        out_shape=jax.ShapeDtypeStruct((T, GMLP_W), F32),
        compiler_params=_params(("parallel",)),
    )(u, z, ln_g, ln_b, sgu_w, sgu_bt)


def _out_fwd(attn, gm, ga, gg, w_out, x, g2):
    T = x.shape[0]
    tm = TM_PROJ

    def body(a_ref, m_ref, ga_ref, gg_ref, w_ref, x_ref, g2_ref, mix_ref, h1_ref, hn2_ref):
        an, _ = _rms(a_ref[...])
        gn, _ = _rms(m_ref[...])
        an = (an * ga_ref[...]).astype(BF16)
        gn = (gn * gg_ref[...]).astype(BF16)
        mix_ref[:, 0:ATTN_W] = an
        mix_ref[:, ATTN_W:] = gn
        h1 = x_ref[...] + _dot(an, w_ref[0:ATTN_W, :]) + _dot(gn, w_ref[ATTN_W:, :])
        h1_ref[...] = h1
        n2, _ = _rms(h1)
        hn2_ref[...] = (n2 * g2_ref[...]).astype(BF16)

    sds = jax.ShapeDtypeStruct
    return pl.pallas_call(
        body, name="out_fwd", grid=(T // tm,),
        in_specs=[_rows(tm, ATTN_W), _rows(tm, GMLP_W), _resident((1, ATTN_W)), _resident((1, GMLP_W)),
                  _resident((D_MODEL, D_MODEL)), _rows(tm, D_MODEL), _resident((1, D_MODEL))],
        out_specs=[_rows(tm, D_MODEL)] * 3,
        out_shape=[sds((T, D_MODEL), BF16), sds((T, D_MODEL), F32), sds((T, D_MODEL), BF16)],
        compiler_params=_params(("parallel",)),
    )(attn, gm, ga, gg, w_out, x, g2)


def _ffn_fwd(hn2, h1, w1t, w2, gf, tgt):
    T = h1.shape[0]
    tm = TM_FFN

    def body(hn_ref, h1_ref, w1_ref, w2_ref, gf_ref, t_ref, r_ref, dhf_ref, dhb_ref, loss_ref, dgf_ref):
        i = pl.program_id(0)

        @pl.when(i == 0)
        def _():
            loss_ref[...] = jnp.zeros_like(loss_ref)
            dgf_ref[...] = jnp.zeros_like(dgf_ref)

        hn = hn_ref[...]
        acc = h1_ref[...]
        for j in range(D_FF // FF_CHUNK):
            cols = slice(j * FF_CHUNK, (j + 1) * FF_CHUNK)
            r = jnp.maximum(_dot(hn, w1_ref[cols, :], NT), 0.0)
            r_ref[:, cols] = r.astype(BF16)
            act = jnp.square(r).astype(BF16)
            acc = acc + _dot(act, w2_ref[cols, :])
        n3, r3 = _rms(acc)
        gf_row = gf_ref[...]
        e = n3 * gf_row - t_ref[...]
        loss_ref[...] += 0.5 * jnp.sum(jnp.mean(e * e, axis=-1, keepdims=True))
        dy = e * (1.0 / D_MODEL)
        _accum_rows(dgf_ref, dy * n3)
        dh2 = _rms_bwd(n3, r3, gf_row, dy)
        dhf_ref[...] = dh2
        dhb_ref[...] = dh2.astype(BF16)

    sds = jax.ShapeDtypeStruct
    acc_spec = lambda n: pl.BlockSpec((8, n), lambda i: (0, 0))
    return pl.pallas_call(
        body, name="ffn_fwd", grid=(T // tm,),
        in_specs=[_rows(tm, D_MODEL), _rows(tm, D_MODEL), _resident((D_FF, D_MODEL)), _resident((D_FF, D_MODEL)),
                  _resident((1, D_MODEL)), _rows(tm, D_MODEL)],
        out_specs=[_rows(tm, D_FF), _rows(tm, D_MODEL), _rows(tm, D_MODEL), acc_spec(LANES), acc_spec(D_MODEL)],
        out_shape=[sds((T, D_FF), BF16), sds((T, D_MODEL), F32), sds((T, D_MODEL), BF16),
                   sds((8, LANES), F32), sds((8, D_MODEL), F32)],
        compiler_params=_params(("arbitrary",)),
    )(hn2, h1, w1t, w2, gf, tgt)


def _ffn_bwd(dh2b, dh2f, relu, h1, g2, w2, w1t):
    T = h1.shape[0]
    tm = TM_FFN

    def body(db_ref, df_ref, r_ref, h1_ref, g2_ref, w2_ref, w1t_ref, da_ref, d1f_ref, d1b_ref, dg_ref):
        @pl.when(pl.program_id(0) == 0)
        def _():
            dg_ref[...] = jnp.zeros_like(dg_ref)

        db = db_ref[...]
        acc = jnp.zeros((tm, D_MODEL), F32)
        for j in range(D_FF // FF_CHUNK):
            cols = slice(j * FF_CHUNK, (j + 1) * FF_CHUNK)
            da = (_dot(db, w2_ref[cols, :], NT) * (2.0 * r_ref[:, cols].astype(F32))).astype(BF16)
            da_ref[:, cols] = da
            acc = acc + _dot(da, w1t_ref[cols, :])
        n2, r2 = _rms(h1_ref[...])
        _accum_rows(dg_ref, acc * n2)
        dh1 = df_ref[...] + _rms_bwd(n2, r2, g2_ref[...], acc)
        d1f_ref[...] = dh1
        d1b_ref[...] = dh1.astype(BF16)

    sds = jax.ShapeDtypeStruct
    return pl.pallas_call(
        body, name="ffn_bwd", grid=(T // tm,),
        in_specs=[_rows(tm, D_MODEL), _rows(tm, D_MODEL), _rows(tm, D_FF), _rows(tm, D_MODEL),
                  _resident((1, D_MODEL)), _resident((D_FF, D_MODEL)), _resident((D_FF, D_MODEL))],
        out_specs=[_rows(tm, D_FF), _rows(tm, D_MODEL), _rows(tm, D_MODEL),
                   pl.BlockSpec((8, D_MODEL), lambda i: (0, 0))],
        out_shape=[sds((T, D_FF), BF16), sds((T, D_MODEL), F32), sds((T, D_MODEL), BF16), sds((8, D_MODEL), F32)],
        compiler_params=_params(("arbitrary",)),
    )(dh2b, dh2f, relu, h1, g2, w2, w1t)


def _out_bwd(dh1b, w_out, attn, gm, ga, gg):
    T = attn.shape[0]
    tm = TM_PROJ

    def body(d_ref, w_ref, a_ref, m_ref, ga_ref, gg_ref, da_ref, dm_ref, dga_ref, dgg_ref):
        @pl.when(pl.program_id(0) == 0)
        def _():
            dga_ref[...] = jnp.zeros_like(dga_ref)
            dgg_ref[...] = jnp.zeros_like(dgg_ref)

        d = d_ref[...]
        dan = _dot(d, w_ref[0:ATTN_W, :], NT)
        dgn = _dot(d, w_ref[ATTN_W:, :], NT)
        na, ra = _rms(a_ref[...])
        ng, rg = _rms(m_ref[...])
        _accum_rows(dga_ref, dan * na)
        _accum_rows(dgg_ref, dgn * ng)
        da_ref[...] = _rms_bwd(na, ra, ga_ref[...], dan)
        dm_ref[...] = _rms_bwd(ng, rg, gg_ref[...], dgn)

    sds = jax.ShapeDtypeStruct
    return pl.pallas_call(
        body, name="out_bwd", grid=(T // tm,),
        in_specs=[_rows(tm, D_MODEL), _resident((D_MODEL, D_MODEL)), _rows(tm, ATTN_W), _rows(tm, GMLP_W),
                  _resident((1, ATTN_W)), _resident((1, GMLP_W))],
        out_specs=[_rows(tm, ATTN_W), _rows(tm, GMLP_W), pl.BlockSpec((8, ATTN_W), lambda i: (0, 0)),
                   pl.BlockSpec((8, GMLP_W), lambda i: (0, 0))],
        out_shape=[sds((T, ATTN_W), F32), sds((T, GMLP_W), F32), sds((8, ATTN_W), F32), sds((8, GMLP_W), F32)],
        compiler_params=_params(("arbitrary",)),
    )(dh1b, w_out, attn, gm, ga, gg)


def _gmlp_bwd(u, z, dgm, ln_g, ln_b, sgu_w, sgu_bt):
    T = u.shape[0]
    tg = TM_GMLP
    nsteps = T // tg

    def body(u_ref, z_ref, d_ref, g_ref, b_ref, w_ref, sb_ref, duz_ref, dlg_ref, dlb_ref, dw_ref, dsb_ref):
        i = pl.program_id(0)

        @pl.when(i == 0)
        def _():
            for ref in (dlg_ref, dlb_ref, dw_ref, dsb_ref):
                ref[...] = jnp.zeros_like(ref)

        grp = lax.broadcasted_iota(jnp.int32, (tg, GMLP_W), 1) // HEAD_DIM
        lane = lax.broadcasted_iota(jnp.int32, (CHUNK, LANES), 1)
        causal, ws = _causal_ws(w_ref)
        lg = g_ref[...]
        uu, zz, dgm = u_ref[...], z_ref[...], d_ref[...]
        ug, tu, tz, xhat, rstd, zn16, mixed = _gmlp_core(uu, zz, lg, b_ref[...], ws, sb_ref, grp)
        dmx = dgm * ug
        duz_ref[:, 0:GMLP_W] = dgm * mixed * _gelu_grad(uu, tu)
        dmx16 = dmx.astype(BF16)
        dzn = []
        for ci in range(tg // CHUNK):
            rows = slice(ci * CHUNK, (ci + 1) * CHUNK)
            dmx_c, d = dmx16[rows, :], jnp.zeros((CHUNK, GMLP_W), F32)
            for g in range(N_GROUPS):
                mk = grp[:CHUNK] == g
                d = jnp.where(mk, _dot(ws[g], dmx_c, TN), d)
                dw_ref[g] += _dot(jnp.where(mk, dmx_c, jnp.zeros_like(dmx_c)), zn16[rows, :], NT)
            dzn.append(d)
        dzn = jnp.concatenate(dzn, axis=0)
        dsb = jnp.zeros((CHUNK, LANES), F32)
        for g in range(N_GROUPS):
            per_token = jnp.sum(jnp.where(grp == g, dmx, 0.0), axis=-1, keepdims=True)
            by_position = sum(per_token[ci * CHUNK:(ci + 1) * CHUNK] for ci in range(tg // CHUNK))
            dsb = jnp.where(lane == g, by_position, dsb)
        dsb_ref[...] += dsb
        _accum_rows(dlg_ref, dzn * xhat)
        _accum_rows(dlb_ref, dzn)
        dxh = dzn * lg
        dzg = rstd * (dxh - _group_mean(dxh, grp) - xhat * _group_mean(dxh * xhat, grp))
        duz_ref[:, GMLP_W:] = dzg * _gelu_grad(zz, tz)

        @pl.when(i == nsteps - 1)
        def _():
            for g in range(N_GROUPS):
                dw_ref[g] = jnp.where(causal, dw_ref[g], 0.0)

    sds = jax.ShapeDtypeStruct
    return pl.pallas_call(
        body, name="gmlp_bwd", grid=(nsteps,),
        in_specs=[_rows(tg, GMLP_W)] * 3 + [_resident((1, GMLP_W)), _resident((1, GMLP_W)),
                                              _resident((N_GROUPS, CHUNK, CHUNK)), _resident((CHUNK, N_GROUPS))],
        out_specs=[_rows(tg, 2 * GMLP_W), pl.BlockSpec((8, GMLP_W), lambda i: (0, 0)),
                   pl.BlockSpec((8, GMLP_W), lambda i: (0, 0)),
                   pl.BlockSpec((N_GROUPS, CHUNK, CHUNK), lambda i: (0, 0, 0)),
                   pl.BlockSpec((CHUNK, LANES), lambda i: (0, 0))],
        out_shape=[sds((T, 2 * GMLP_W), F32), sds((8, GMLP_W), F32), sds((8, GMLP_W), F32),
                   sds((N_GROUPS, CHUNK, CHUNK), F32), sds((CHUNK, LANES), F32)],
        compiler_params=_params(("arbitrary",)),
    )(u, z, dgm, ln_g, ln_b, sgu_w, sgu_bt)


def _attn_bwd(q, k, v, dattn, attn, lse, duz, owner_grads=()):
    T = q.shape[0]
    nt = T // ATT_TILE
    ns = len(owner_grads)
    steps = (ATTN_W // LANES) * nt

    def body(sl_ref, q_hbm, k_hbm, v_hbm, do_hbm, o_hbm, lse_hbm, duz_hbm, *rest):
        p_refs, rest = rest[:ns], rest[ns:]
        dq_hbm = dk_hbm = dv_hbm = rest[0]
        r_refs, rest = rest[1:1 + ns], rest[1 + ns:]
        qbuf, dobuf, obuf, lbuf, kbuf, vbuf, dqbuf, dkbuf, dvbuf, delta_s = rest[:10]
        sem_q, sem_do, sem_o, sem_l, sem_k, sem_v, sem_dq, sem_dk, sem_dv, sem_uz = rest[10:20]
        hp, t = pl.program_id(0), pl.program_id(1)
        step = hp * nt + t
        two, three = step % 2, step % 3
        before, after = (step + 2) % 3, (step + 1) % 3
        uz_copy = pltpu.make_async_copy(duz_hbm, dq_hbm.at[:, :, pl.ds(3 * ATTN_W, 2 * GMLP_W)], sem_uz)
        pl.when(step == 0)(uz_copy.start)
        if ns:
            start, finish = _owner_exchange_phases(p_refs, r_refs, *rest[20:])
            pl.when(step == 0)(start)

        def fetch(hp_, t_, two_, three_):
            for hbm, buf, sem, slot in ((q_hbm, qbuf, sem_q, two_), (do_hbm, dobuf, sem_do, two_),
                                        (o_hbm, obuf, sem_o, two_), (lse_hbm, lbuf, sem_l, two_),
                                        (k_hbm, kbuf, sem_k, three_), (v_hbm, vbuf, sem_v, three_)):
                for cp in _tile_copies(hbm, buf.at[slot], sem.at[slot], hp_, t_):
                    cp.start()

        @pl.when(step == 0)
        def _():
            kbuf[2] = jnp.zeros((ATT_BLOCKS, CHUNK, LANES), F32)
            vbuf[2] = jnp.zeros((ATT_BLOCKS, CHUNK, LANES), F32)
            dkbuf[3] = jnp.zeros((ATT_BLOCKS, CHUNK, LANES), F32)
            dvbuf[3] = jnp.zeros((ATT_BLOCKS, CHUNK, LANES), F32)
            fetch(0, 0, 0, 0)

        @pl.when(step + 1 < steps)
        def _():
            fetch((step + 1) // nt, (step + 1) % nt, 1 - two, after)

        for buf, sem in ((qbuf, sem_q), (dobuf, sem_do), (obuf, sem_o), (lbuf, sem_l)):
            _wait_tile(buf.at[two], sem.at[two])
        _wait_tile(kbuf.at[three], sem_k.at[three])
        _wait_tile(vbuf.at[three], sem_v.at[three])

        @pl.when(step >= 2)
        def _():
            _wait_tile(dqbuf.at[two], sem_dq.at[two])

        @pl.when(step >= 3)
        def _():
            _wait_tile(dkbuf.at[three], sem_dk.at[three])
            _wait_tile(dvbuf.at[three], sem_dv.at[three])

        zero_tile = jnp.zeros((ATT_BLOCKS, CHUNK, LANES), F32)
        dqbuf[two] = zero_tile
        dkbuf[three] = zero_tile
        dvbuf[three] = zero_tile

        q_t, do_t, l_t, k_t, v_t = qbuf.at[two], dobuf.at[two], lbuf.at[two], kbuf.at[three], vbuf.at[three]
        k_b, v_b = kbuf.at[before], vbuf.at[before]
        dq_t, dk_t, dv_t = dqbuf.at[two], dkbuf.at[three], dvbuf.at[three]
        dk_b, dv_b = dkbuf.at[before], dvbuf.at[before]
        sink = jnp.where(t > 0, before, 3)
        dk_sink, dv_sink = dkbuf.at[sink], dvbuf.at[sink]
        head0 = lax.broadcasted_iota(jnp.int32, (CHUNK, LANES), 1) < HEAD_DIM
        for r in range(ATT_BLOCKS):
            dd = dobuf[two, r] * obuf[two, r]
            d0 = jnp.sum(jnp.where(head0, dd, 0.0), axis=-1, keepdims=True)
            d1 = jnp.sum(jnp.where(head0, 0.0, dd), axis=-1, keepdims=True)
            delta_s[r] = jnp.where(head0, d0, d1)

        def column(xb):
            return jnp.concatenate([xb[:, 0:1], xb[:, HEAD_DIM:HEAD_DIM + 1]], axis=0)

        no_key_before = jnp.where(lax.broadcasted_iota(jnp.int32, (2 * CHUNK, 2 * CHUNK), 1) < CHUNK, NEG, 0.0)
        for d in DILATIONS:
            bias = _residue_bias(sl_ref, d)
            for j in range(ATT_BLOCKS):
                kcat = jnp.concatenate([_rm_block_before(k_t, k_b, d, j), _rm_block(k_t, d, j)], axis=0).astype(BF16)
                vcat = jnp.concatenate([_rm_block_before(v_t, v_b, d, j), _rm_block(v_t, d, j)], axis=0).astype(BF16)
                q2 = _stack_heads(_rm_block(q_t, d, j), head0)
                do2 = _stack_heads(_rm_block(do_t, d, j), head0)
                s = _dot(q2, kcat, NT) + bias
                if _first_in_tile(d, j):
                    s = s + jnp.where(t == 0, 1.0, 0.0) * no_key_before
                p = jnp.exp(s - column(_rm_block(l_t, d, j)))
                ds = (p * (_dot(do2, vcat, NT) - column(_rm_block(delta_s, d, j)))).astype(BF16)
                _rm_add(dq_t, _residue_rows(d, j), _unstack_heads(_dot(ds, kcat), head0))
                ck = _dot(ds, q2, TN)
                cv = _dot(p.astype(BF16), do2, TN)
                _rm_add(dk_t, _residue_rows(d, j), ck[CHUNK:, :])
                _rm_add(dv_t, _residue_rows(d, j), cv[CHUNK:, :])
                if _first_in_tile(d, j):
                    rows = [(r, CHUNK - n, n) for r, _, n in _residue_rows(d, j)]
                    _rm_add(dk_sink, rows, ck[:CHUNK, :])
                    _rm_add(dv_sink, rows, cv[:CHUNK, :])
                else:
                    rows = [(r, lo - n, n) for r, lo, n in _residue_rows(d, j)]
                    _rm_add(dk_t, rows, ck[:CHUNK, :])
                    _rm_add(dv_t, rows, cv[:CHUNK, :])

        for r in range(ATT_BLOCKS):
            dqbuf[two, r] = dqbuf[two, r] * Q_SCALE
        for cp in _tile_copies(dq_hbm, dq_t, sem_dq.at[two], hp, t, to_hbm=True):
            cp.start()

        @pl.when(t > 0)
        def _():
            for cp in (_tile_copies(dk_hbm, dk_b, sem_dk.at[before], hp, t - 1, to_hbm=True, lane0=ATTN_W)
                       + _tile_copies(dv_hbm, dv_b, sem_dv.at[before], hp, t - 1, to_hbm=True, lane0=2 * ATTN_W)):
                cp.start()

        @pl.when(t == nt - 1)
        def _():
            for cp in (_tile_copies(dk_hbm, dk_t, sem_dk.at[three], hp, t, to_hbm=True, lane0=ATTN_W)
                       + _tile_copies(dv_hbm, dv_t, sem_dv.at[three], hp, t, to_hbm=True, lane0=2 * ATTN_W)):
                cp.start()

        @pl.when(step == steps - 1)
        def _():
            for slot in range(2):
                _wait_tile(dqbuf.at[slot], sem_dq.at[slot])
            for slot in range(3):
                _wait_tile(dkbuf.at[slot], sem_dk.at[slot])
                _wait_tile(dvbuf.at[slot], sem_dv.at[slot])
            uz_copy.wait()

        if ns:
            pl.when(step == steps - 1)(finish)

    tile = lambda n: pltpu.VMEM((n, ATT_BLOCKS, CHUNK, LANES), F32)
    dma = lambda n: pltpu.SemaphoreType.DMA((n,))
    view = jax.ShapeDtypeStruct((T // ATT_BLOCKS, ATT_BLOCKS, ATTN_W), F32)
    outs = pl.pallas_call(
        body, name="attn_bwd", grid=(ATTN_W // LANES, nt),
        in_specs=[pl.BlockSpec((8, LANES), lambda c, t: (0, c))] + [_HBM] * (7 + ns),
        out_specs=[_HBM] * (1 + ns),
        out_shape=[jax.ShapeDtypeStruct((T // ATT_BLOCKS, ATT_BLOCKS, IN_W), F32)]
        + [jax.ShapeDtypeStruct(p.shape, p.dtype) for p in owner_grads],
        scratch_shapes=[tile(2), tile(2), tile(2), tile(2), tile(3), tile(3), tile(2), tile(4), tile(4),
                        pltpu.VMEM((ATT_BLOCKS, CHUNK, LANES), F32)]
        + [dma(2), dma(2), dma(2), dma(2), dma(3), dma(3), dma(2), dma(3), dma(3), pltpu.SemaphoreType.DMA]
        + (_owner_exchange_sems(ns) if ns else []),
        compiler_params=_params(("arbitrary", "arbitrary")),
    )(_slope_table(), *[_residue_view(a) for a in (q, k, v, dattn, attn, lse, duz)], *owner_grads)
    return outs[0].reshape(T, IN_W), tuple(outs[1:])


def _proj_bwd(dproj, w_in_t, x, g1, dh1, chip_sums=()):
    T = x.shape[0]
    tm = TM_PROJ
    ns = len(chip_sums)
    steps = T // tm

    def body(d_ref, w_ref, x_ref, g_ref, r_ref, *rest):
        p_refs, rest = rest[:ns], rest[ns:]
        dx_ref, dg_ref = rest[:2]
        r_refs, sems = rest[2:2 + ns], rest[2 + ns:]
        step = pl.program_id(0)
        if ns:
            start, finish = _chip_exchange_phases(p_refs, r_refs, *sems)
            pl.when(step == 0)(start)

        @pl.when(step == 0)
        def _():
            dg_ref[...] = jnp.zeros_like(dg_ref)

        dhn = _dot(d_ref[...].astype(BF16), w_ref[...])
        n1, r1 = _rms(x_ref[...])
        _accum_rows(dg_ref, dhn * n1)
        dx_ref[...] = r_ref[...] + _rms_bwd(n1, r1, g_ref[...], dhn)
        if ns:
            pl.when(step == steps - 1)(finish)

    outs = pl.pallas_call(
        body, name="proj_bwd", grid=(steps,),
        in_specs=[_rows(tm, IN_W), _resident((IN_W, D_MODEL)), _rows(tm, D_MODEL), _resident((1, D_MODEL)),
                  _rows(tm, D_MODEL)] + [_HBM] * ns,
        out_specs=[_rows(tm, D_MODEL), pl.BlockSpec((8, D_MODEL), lambda i: (0, 0))] + [_HBM] * ns,
        out_shape=[jax.ShapeDtypeStruct((T, D_MODEL), F32), jax.ShapeDtypeStruct((8, D_MODEL), F32)]
        + [jax.ShapeDtypeStruct(p.shape, p.dtype) for p in chip_sums],
        scratch_shapes=_chip_exchange_sems(ns) if ns else [],
        compiler_params=_params(("arbitrary",)),
    )(dproj, w_in_t, x, g1, dh1, *chip_sums)
    return outs[0], outs[1], tuple(outs[2:])


def _dw(a, b, name, tile, square_a=False, out_dtype=F32):
    T, ka = a.shape
    nb = b.shape[1]
    tka, tnb, tt = tile
    tt = min(tt, T)
    last = T // tt - 1

    def body(a_ref, b_ref, *refs):
        o_ref = refs[0]
        acc_ref = refs[1] if len(refs) > 1 else o_ref
        s = pl.program_id(2)

        @pl.when(s == 0)
        def _():
            acc_ref[...] = jnp.zeros_like(acc_ref)

        a_tile = a_ref[...]
        if square_a:
            a_tile = jnp.square(a_tile.astype(F32))
        acc_ref[...] += _dot(a_tile.astype(BF16), b_ref[...], TN)
        if acc_ref is not o_ref:
            @pl.when(s == last)
            def _():
                o_ref[...] = acc_ref[...].astype(out_dtype)

    return pl.pallas_call(
        body, name=name, grid=(ka // tka, nb // tnb, T // tt),
        in_specs=[pl.BlockSpec((tt, tka), lambda i, j, s: (s, i)), pl.BlockSpec((tt, tnb), lambda i, j, s: (s, j))],
        out_specs=pl.BlockSpec((tka, tnb), lambda i, j, s: (i, j)),
        out_shape=jax.ShapeDtypeStruct((ka, nb), out_dtype),
        scratch_shapes=[] if out_dtype == F32 else [pltpu.VMEM((tka, tnb), F32)],
        compiler_params=_params(("parallel", "parallel", "arbitrary")),
    )(a, b)


def _adamw(w, m, v, parts, name, tr, transposed=False):
    R, C = w.shape
    P = parts.shape[0]

    def body(w_ref, m_ref, v_ref, p_ref, g_ref, d_ref, m2_ref, v2_ref):
        g = p_ref[0].astype(F32)
        for i in range(1, P):
            g = g + p_ref[i].astype(F32)
        if transposed:
            g = g.T
        m2 = ADAM_B1 * m_ref[...] + (1.0 - ADAM_B1) * g
        v2 = ADAM_B2 * v_ref[...] + (1.0 - ADAM_B2) * jnp.square(g)
        m_hat = m2 / (1.0 - ADAM_B1 ** ADAM_STEP)
        v_hat = v2 / (1.0 - ADAM_B2 ** ADAM_STEP)
        g_ref[...] = g
        d_ref[...] = -ADAM_LR * (m_hat / (jnp.sqrt(v_hat) + ADAM_EPS) + ADAM_WD * w_ref[...])
        m2_ref[...] = m2
        v2_ref[...] = v2

    spec = _rows(tr, C)
    part_spec = (pl.BlockSpec((P, C, tr), lambda i: (0, 0, i)) if transposed
                 else pl.BlockSpec((P, tr, C), lambda i: (0, i, 0)))
    return pl.pallas_call(
        body, name=name, grid=(R // tr,),
        in_specs=[spec, spec, spec, part_spec],
        out_specs=[spec] * 4,
        out_shape=[jax.ShapeDtypeStruct((R, C), F32)] * 4,
        compiler_params=_params(("parallel",)),
    )(w, m, v, parts)


def _pair_sum(core, grad, recv, name):
    _, _, n, C = grad.shape
    tr = n // 2

    def body(c_ref, a_ref, b_ref, o_ref):
        o_ref[...] = a_ref[...] + b_ref[...]

    spec = pl.BlockSpec((1, tr, C), lambda i, j, c_ref: (i, j, 0))
    return pl.pallas_call(
        body, name=name,
        grid_spec=pltpu.PrefetchScalarGridSpec(
            num_scalar_prefetch=1, grid=(4, n // tr),
            in_specs=[pl.BlockSpec((1, None, tr, C), lambda i, j, c_ref: (i, c_ref[0], j, 0)), spec],
            out_specs=spec),
        out_shape=jax.ShapeDtypeStruct(recv.shape, F32),
        compiler_params=_params(("parallel", "parallel")),
    )(core.reshape(1), grad, recv)


_HBM = pl.BlockSpec(memory_space=pltpu.HBM)


def _place():
    return lax.axis_index("x"), lax.axis_index("y"), lax.axis_index("c")


def _gathered_shape(shard):
    return jax.ShapeDtypeStruct((N_DEV,) + shard.shape, shard.dtype)


def _gather_sems(n):
    return [pltpu.SemaphoreType.DMA((7, n)), pltpu.SemaphoreType.DMA((7, n)), pltpu.SemaphoreType.DMA((n,))]


def _gather_phases(x_refs, out_refs, send_sems, recv_sems, local_sems):
    x, y, c = _place()
    me, sibling = (x, y, c), (x, y, 1 - c)
    chips = [(1 - x, y), (x, 1 - y), (1 - x, 1 - y)]
    arrays = range(len(x_refs))

    def slot(i, px, py, pc):
        return out_refs[i].at[4 * px + 2 * py + pc]

    def copy(i, k, block, to, own=False):
        return pltpu.make_async_remote_copy(
            src_ref=x_refs[i] if own else slot(i, *block), dst_ref=slot(i, *block),
            send_sem=send_sems.at[k, i], recv_sem=recv_sems.at[k, i], device_id=to, device_id_type=MESH)

    def mine(i):
        return pltpu.make_async_copy(x_refs[i], slot(i, *me), local_sems.at[i])

    def start():
        for i in arrays:
            mine(i).start()
            copy(i, 0, me, sibling, own=True).start()
            for j, chip in enumerate(chips):
                copy(i, 1 + j, me, (*chip, c), own=True).start()

    def forward():
        for i in arrays:
            for j, chip in enumerate(chips):
                copy(i, 1 + j, (*chip, c), me).wait_recv()
                copy(i, 4 + j, (*chip, c), sibling).start()

    def finish():
        for i in arrays:
            copy(i, 0, sibling, me).wait_recv()
            copy(i, 0, me, sibling, own=True).wait_send()
            for j, chip in enumerate(chips):
                copy(i, 4 + j, (*chip, 1 - c), me).wait_recv()
                copy(i, 1 + j, me, (*chip, c), own=True).wait_send()
                copy(i, 4 + j, (*chip, c), sibling).wait_send()
            mine(i).wait()

    return start, forward, finish


def _all_gather(shards, name):
    n = len(shards)

    def body(*refs):
        start, forward, finish = _gather_phases(refs[:n], refs[n:2 * n], *refs[2 * n:])
        start()
        forward()
        finish()

    return pl.pallas_call(
        body, name=name,
        out_shape=[_gathered_shape(s) for s in shards],
        in_specs=[_HBM] * n, out_specs=[_HBM] * n,
        scratch_shapes=_gather_sems(n),
    )(*shards)


def _sibling_exchange(grads, name):
    n = len(grads)

    def body(*refs):
        g_refs, r_refs, send_sems, recv_sems = refs[:n], refs[n:2 * n], refs[2 * n], refs[2 * n + 1]
        x, y, c = _place()
        copies = [pltpu.make_async_remote_copy(
            src_ref=g_refs[i].at[:, 1 - c], dst_ref=r_refs[i], send_sem=send_sems.at[i], recv_sem=recv_sems.at[i],
            device_id=(x, y, 1 - c), device_id_type=MESH) for i in range(n)]
        for cp in copies:
            cp.start()
        for cp in copies:
            cp.wait()

    return pl.pallas_call(
        body, name=name,
        out_shape=[jax.ShapeDtypeStruct((g.shape[0],) + g.shape[2:], g.dtype) for g in grads],
        in_specs=[_HBM] * n, out_specs=[_HBM] * n,
        scratch_shapes=[pltpu.SemaphoreType.DMA((n,)), pltpu.SemaphoreType.DMA((n,))],
    )(*grads)


def _owner_exchange_sems(n):
    return [pltpu.SemaphoreType.DMA((7, n)), pltpu.SemaphoreType.DMA((7, n)), pltpu.SemaphoreType.DMA((n,))]


def _owner_exchange_phases(g_refs, r_refs, send_sems, recv_sems, local_sems):
    x, y, c = _place()
    me = 4 * x + 2 * y + c
    flip = lambda v, bit: 1 - v if bit else v
    peers = [(flip(x, k & 4), flip(y, k & 2), flip(c, k & 1)) for k in range(1, N_DEV)]
    arrays = range(len(g_refs))

    def mine(i):
        return pltpu.make_async_copy(g_refs[i].at[me], r_refs[i].at[me], local_sems.at[i])

    def copy(i, k, src_slot, dst_slot):
        return pltpu.make_async_remote_copy(
            src_ref=g_refs[i].at[src_slot], dst_ref=r_refs[i].at[dst_slot],
            send_sem=send_sems.at[k, i], recv_sem=recv_sems.at[k, i], device_id=peers[k], device_id_type=MESH)

    def start():
        for i in arrays:
            mine(i).start()
            for k, (px, py, pc) in enumerate(peers):
                copy(i, k, 4 * px + 2 * py + pc, me).start()

    def finish():
        for i in arrays:
            for k, (px, py, pc) in enumerate(peers):
                copy(i, k, me, 4 * px + 2 * py + pc).wait_recv()
                copy(i, k, 4 * px + 2 * py + pc, me).wait_send()
            mine(i).wait()

    return start, finish


def _chip_exchange_sems(n):
    return [pltpu.SemaphoreType.DMA((3, n)), pltpu.SemaphoreType.DMA((3, n)), pltpu.SemaphoreType.DMA((n,))]


def _chip_exchange_phases(p_refs, r_refs, send_sems, recv_sems, local_sems):
    x, y, c = _place()
    my_chip = 2 * x + y
    chips = [(1 - x, y), (x, 1 - y), (1 - x, 1 - y)]
    arrays = range(len(p_refs))

    def mine(i):
        return pltpu.make_async_copy(p_refs[i].at[my_chip], r_refs[i].at[my_chip], local_sems.at[i])

    def copy(i, k, src_chip, dst_chip):
        px, py = chips[k]
        return pltpu.make_async_remote_copy(
            src_ref=p_refs[i].at[src_chip], dst_ref=r_refs[i].at[dst_chip],
            send_sem=send_sems.at[k, i], recv_sem=recv_sems.at[k, i], device_id=(px, py, c), device_id_type=MESH)

    def start():
        for i in arrays:
            mine(i).start()
            for k, (px, py) in enumerate(chips):
                copy(i, k, 2 * px + py, my_chip).start()

    def finish():
        for i in arrays:
            for k, (px, py) in enumerate(chips):
                copy(i, k, my_chip, 2 * px + py).wait_recv()
                copy(i, k, 2 * px + py, my_chip).wait_send()
            mine(i).wait()

    return start, finish


_R_IN, _R_OUT, _R_FF = IN_W // N_DEV, D_MODEL // N_DEV, D_FF // N_DEV


def _by_owner(g):
    return g.reshape(4, 2, g.shape[0] // N_DEV, D_MODEL)


def _local_step(x, tgt, small, w_in_t, rest, core=None):
    exchange = core is not None
    g1, g2, gf = small["norm1_g"], small["norm2_g"], small["final_norm_g"].reshape(1, D_MODEL)
    ga, gg = small["attn_out_g"], small["gmlp_out_g"]
    ln_g = small["sgu_ln_g"].reshape(1, GMLP_W)
    ln_b = small["sgu_ln_b"].reshape(1, GMLP_W)
    sgu_w = small["sgu_w"][0]
    sgu_bt = small["sgu_b"][0].T

    hn1, q, k, v, u, z = _proj_fwd(x, g1, w_in_t)
    attn, lse, gathered = _attn_fwd(q, k, v, shards=rest if exchange else ())
    w_out, w_ff1_t, w_ff2 = [g.reshape(-1, D_MODEL) for g in gathered] if exchange else rest
    gm = _gmlp_fwd(u, z, ln_g, ln_b, sgu_w, sgu_bt)
    mixed, h1, hn2 = _out_fwd(attn, gm, ga, gg, w_out, x, g2)
    relu, dh2f, dh2b, loss8, dgf8 = _ffn_fwd(hn2, h1, w_ff1_t, w_ff2, gf, tgt)

    da, dh1f, dh1b, dg2 = _ffn_bwd(dh2b, dh2f, relu, h1, g2, w_ff2, w_ff1_t)
    wire = BF16 if exchange else F32
    dw_ff2 = _dw(relu, dh2b, "dw_ff2", DW_TILE, square_a=True, out_dtype=wire)
    dw_ff1_t = _dw(da, hn2, "dw_ff1", DW_TILE, out_dtype=wire)
    dattn, dgm, dga, dgg = _out_bwd(dh1b, w_out, attn, gm, ga, gg)
    dw_out = _dw(mixed, dh1b, "dw_out", DW_TILE, out_dtype=wire)
    early = [dw_out, dw_ff1_t, dw_ff2]
    if exchange:
        early = [g.reshape(N_DEV, -1, D_MODEL) for g in early]
    duz, dlg, dlb, dsw, dsb = _gmlp_bwd(u, z, dgm, ln_g, ln_b, sgu_w, sgu_bt)
    dproj, arrived = _attn_bwd(q, k, v, dattn, attn, lse, duz, owner_grads=early if exchange else ())
    dw_in_t = _dw(dproj, hn1, "dw_in", DW_TILE_IN)
    late = ()
    if exchange:
        by_owner = _by_owner(dw_in_t)
        got, = _sibling_exchange([by_owner], "grad_sibling_exchange")
        late = (_pair_sum(core, by_owner, got, "grad_pair_sum"),)
    dx, dg1, late = _proj_bwd(dproj, w_in_t, x, g1, dh1f, chip_sums=late)
    if exchange:
        dw_in_t, early = late[0], arrived

    small_grads = dict(
        norm1_g=dg1[0], sgu_ln_g=dlg[0], sgu_ln_b=dlb[0], sgu_w=dsw, sgu_b=dsb[:, :N_GROUPS].T,
        attn_out_g=dga[0], gmlp_out_g=dgg[0], norm2_g=dg2[0], final_norm_g=dgf8[0])
    return loss8[0, 0], dx, (dw_in_t, *early), small_grads


SMALL_NAMES = ("norm1_g", "sgu_ln_g", "sgu_ln_b", "sgu_w", "sgu_b", "attn_out_g", "gmlp_out_g", "norm2_g",
               "final_norm_g")
WEIGHT_ORDER = ("norm1_g", "w_in", "sgu_ln_g", "sgu_ln_b", "sgu_w", "sgu_b", "attn_out_g", "gmlp_out_g", "w_out",
                "norm2_g", "w_ff1", "w_ff2", "final_norm_g")


TINY_NAMES = tuple(n for n in SMALL_NAMES if n != "sgu_w")
TINY_ROWS = 48


def _pack_tiny(d, last):
    return jnp.concatenate([d[n].reshape(-1, LANES) for n in TINY_NAMES] + [last], axis=0)


def _unpack_tiny(p, like):
    out, r = {}, 0
    for n in TINY_NAMES:
        rows = like[n].size // LANES
        out[n] = p[r:r + rows].reshape(like[n].shape)
        r += rows
    return out


def kernel(x, norm1_g, w_in, sgu_ln_g, sgu_ln_b, sgu_w, sgu_b, attn_out_g, gmlp_out_g, w_out, norm2_g, w_ff1, w_ff2, final_norm_g, loss_target, m_norm1_g, m_w_in, m_sgu_ln_g, m_sgu_ln_b, m_sgu_w, m_sgu_b, m_attn_out_g, m_gmlp_out_g, m_w_out, m_norm2_g, m_w_ff1, m_w_ff2, m_final_norm_g, v_norm1_g, v_w_in, v_sgu_ln_g, v_sgu_ln_b, v_sgu_w, v_sgu_b, v_attn_out_g, v_gmlp_out_g, v_w_out, v_norm2_g, v_w_ff1, v_w_ff2, v_final_norm_g):
    w = dict(norm1_g=norm1_g, w_in=w_in, sgu_ln_g=sgu_ln_g, sgu_ln_b=sgu_ln_b, sgu_w=sgu_w, sgu_b=sgu_b,
             attn_out_g=attn_out_g, gmlp_out_g=gmlp_out_g, w_out=w_out, norm2_g=norm2_g, w_ff1=w_ff1, w_ff2=w_ff2,
             final_norm_g=final_norm_g)
    m = dict(norm1_g=m_norm1_g, w_in=m_w_in, sgu_ln_g=m_sgu_ln_g, sgu_ln_b=m_sgu_ln_b, sgu_w=m_sgu_w, sgu_b=m_sgu_b,
             attn_out_g=m_attn_out_g, gmlp_out_g=m_gmlp_out_g, w_out=m_w_out, norm2_g=m_norm2_g, w_ff1=m_w_ff1,
             w_ff2=m_w_ff2, final_norm_g=m_final_norm_g)
    v = dict(norm1_g=v_norm1_g, w_in=v_w_in, sgu_ln_g=v_sgu_ln_g, sgu_ln_b=v_sgu_ln_b, sgu_w=v_sgu_w, sgu_b=v_sgu_b,
             attn_out_g=v_attn_out_g, gmlp_out_g=v_gmlp_out_g, w_out=v_w_out, norm2_g=v_norm2_g, w_ff1=v_w_ff1,
             w_ff2=v_w_ff2, final_norm_g=v_final_norm_g)
    big = ("w_in", "w_out", "w_ff1", "w_ff2")
    core = lax.axis_index("c")

    w_in_t, = _all_gather([w_in[0].T.astype(BF16)], "w_in_all_gather")
    rest = (w_out[0].astype(BF16), w_ff1[0].T.astype(BF16), w_ff2[0].astype(BF16))
    loss, dx, parts, small_grads = _local_step(x[0], loss_target[0], {n: w[n] for n in SMALL_NAMES},
                                               w_in_t.reshape(IN_W, D_MODEL), rest, core=core)

    new = {}
    for n, p, transposed, tr in zip(big, parts, (True, False, True, False), (128, 128, 128, 256)):
        new[n] = [a[None] for a in _adamw(w[n][0], m[n][0], v[n][0], p, "adamw_" + n, tr, transposed)]

    flat = lambda a: a.reshape(-1, LANES)
    tiny_parts, sgu_parts = _all_gather(
        [_pack_tiny(small_grads, jnp.full((8, LANES), loss, F32)), flat(small_grads["sgu_w"])], "small_grad_all_gather")
    pad = jnp.ones((8, LANES), F32)
    tiny = _adamw(_pack_tiny(w, pad), _pack_tiny(m, pad), _pack_tiny(v, pad), tiny_parts, "adamw_tiny", TINY_ROWS)
    sgu = _adamw(flat(sgu_w), flat(m_sgu_w), flat(v_sgu_w), sgu_parts, "adamw_sgu_w", 512)
    loss = tiny[0][TINY_ROWS - 8, 0]

    outs = []
    for i in range(4):
        d = {n: new[n][i] for n in big}
        d.update(_unpack_tiny(tiny[i], w))
        d["sgu_w"] = sgu[i].reshape(sgu_w.shape)
        outs.extend(d[n] for n in WEIGHT_ORDER)
    return (loss, dx[None], *outs)
```

```python
import functools
import math

import numpy as np
import jax
import jax.numpy as jnp
from jax import lax
from jax.experimental import pallas as pl
from jax.experimental.pallas import tpu as pltpu

F32 = jnp.float32
BF16 = jnp.bfloat16

D_MODEL = 1024
HEAD_DIM = 64
N_HEADS = 12
ATTN_W = N_HEADS * HEAD_DIM
N_GROUPS = 4
GMLP_W = N_GROUPS * HEAD_DIM
IN_W = 3 * ATTN_W + 2 * GMLP_W
D_FF = 4 * D_MODEL
CHUNK = 128
DILATIONS = (1, 4, 16)
EPS = 1e-6
Q_SCALE = HEAD_DIM ** -0.5
NEG = -1e30

ADAM_LR, ADAM_B1, ADAM_B2, ADAM_EPS, ADAM_WD, ADAM_STEP = 0.001, 0.9, 0.999, 1e-08, 0.01, 10

N_DEV = 8
LANES = 128
VMEM_LIMIT = 56 << 20

TM_PROJ = 512
TM_FFN = 512
FF_CHUNK = 512
TM_GMLP = 1024
DW_TILE = (512, 1024, 4096)
DW_TILE_IN = (IN_W // 2, 1024, 2048)

MESH = pl.DeviceIdType.MESH


def _alibi_slopes(n):
    def pow2(m):
        start = 2.0 ** (-8.0 / m)
        return [start ** (i + 1) for i in range(m)]
    c = 2 ** int(math.floor(math.log2(n)))
    s = pow2(n) if c == n else pow2(c) + pow2(2 * c)[0::2][: n - c]
    return np.asarray(s, dtype=np.float32)


SLOPES = _alibi_slopes(N_HEADS)


def _params(sem=None):
    kw = dict(vmem_limit_bytes=VMEM_LIMIT)
    if sem is not None:
        kw["dimension_semantics"] = sem
    return pltpu.CompilerParams(**kw)


def _rows(tm, n):
    return pl.BlockSpec((tm, n), lambda i: (i, 0))


def _resident(shape):
    return pl.BlockSpec(shape, lambda *_: (0,) * len(shape), pipeline_mode=pl.Buffered(1))


def _rms(x):
    r = lax.rsqrt(jnp.mean(x * x, axis=-1, keepdims=True) + EPS)
    return x * r, r


def _rms_bwd(n, r, g, dy):
    dn = dy * g
    return r * (dn - n * jnp.mean(dn * n, axis=-1, keepdims=True))


def _accum_rows(acc_ref, v):
    acc_ref[...] += jnp.broadcast_to(jnp.sum(v, axis=0, keepdims=True), acc_ref.shape)


_G0 = math.sqrt(2.0 / math.pi)
_G1 = 0.044715


def _gelu(x):
    t = jnp.tanh(_G0 * (x + _G1 * (x * x * x)))
    return x * (0.5 * (1.0 + t)), t


def _gelu_grad(x, t):
    return 0.5 * (1.0 + t) + 0.5 * x * (1.0 - t * t) * (_G0 * (1.0 + 3.0 * _G1 * x * x))


NT = (((1,), (1,)), ((), ()))
TN = (((0,), (0,)), ((), ()))


def _dot(a, b, dims=None):
    if dims is None:
        return jnp.dot(a, b, preferred_element_type=F32)
    return lax.dot_general(a, b, dims, preferred_element_type=F32)


def _proj_fwd(x, g1, w_in_t):
    T = x.shape[0]
    tm = TM_PROJ

    def body(x_ref, g_ref, w_ref, hn_ref, q_ref, k_ref, v_ref, u_ref, z_ref):
        n, _ = _rms(x_ref[...])
        hn = (n * g_ref[...]).astype(BF16)
        hn_ref[...] = hn
        a = ATTN_W
        q_ref[...] = _dot(hn, w_ref[0:a, :], NT) * Q_SCALE
        k_ref[...] = _dot(hn, w_ref[a:2 * a, :], NT)
        v_ref[...] = _dot(hn, w_ref[2 * a:3 * a, :], NT)
        u_ref[...] = _dot(hn, w_ref[3 * a:3 * a + GMLP_W, :], NT)
        z_ref[...] = _dot(hn, w_ref[3 * a + GMLP_W:, :], NT)

    sds = jax.ShapeDtypeStruct
    return pl.pallas_call(
        body, name="proj_fwd", grid=(T // tm,),
        in_specs=[_rows(tm, D_MODEL), _resident((1, D_MODEL)), _resident((IN_W, D_MODEL))],
        out_specs=[_rows(tm, D_MODEL), _rows(tm, ATTN_W), _rows(tm, ATTN_W), _rows(tm, ATTN_W),
                   _rows(tm, GMLP_W), _rows(tm, GMLP_W)],
        out_shape=[sds((T, D_MODEL), BF16), sds((T, ATTN_W), F32), sds((T, ATTN_W), F32),
                   sds((T, ATTN_W), F32), sds((T, GMLP_W), F32), sds((T, GMLP_W), F32)],
        compiler_params=_params(("parallel",)),
    )(x, g1, w_in_t)


ATT_TILE = 2048
ATT_BLOCKS = ATT_TILE // CHUNK
SM_BLOCKS = 4


def _slope_table():
    row = np.repeat(SLOPES, HEAD_DIM)
    return jnp.asarray(np.broadcast_to(row[None], (8, ATTN_W)), F32)


def _residue_view(a):
    return a.reshape(a.shape[0] // ATT_BLOCKS, ATT_BLOCKS, a.shape[1])


def _tile_copies(hbm, buf, sem, hp, t, to_hbm=False, lane0=0):
    rows = pl.ds(pl.multiple_of(t * CHUNK, CHUNK), CHUNK)
    lanes = pl.ds(pl.multiple_of(lane0 + hp * LANES, LANES), LANES)
    pairs = [(hbm.at[rows, r, lanes], buf.at[r]) for r in range(ATT_BLOCKS)]
    return [pltpu.make_async_copy(v, h, sem) if to_hbm else pltpu.make_async_copy(h, v, sem) for h, v in pairs]


def _wait_tile(buf, sem):
    pltpu.make_async_copy(buf, buf, sem).wait()


def _residue_rows(d, j):
    if d == 16:
        return [(j, 0, CHUNK)]
    if d == 4:
        return [(j % 4 + 4 * m, 32 * (j // 4), 32) for m in range(4)]
    return [(r, 8 * j, 8) for r in range(ATT_BLOCKS)]


def _block_order(p, d):
    if d == 16:
        return p
    if d == 4:
        return 4 * (p & 31) + (p >> 5)
    return 16 * (p & 7) + (p >> 3)


def _first_in_tile(d, j):
    return _residue_rows(d, j)[0][1] == 0


def _rm_block(buf, d, j):
    return jnp.concatenate([buf[r, lo:lo + n, :] for r, lo, n in _residue_rows(d, j)], axis=0)


def _rm_block_before(buf, buf_before, d, j):
    if _first_in_tile(d, j):
        return jnp.concatenate([buf_before[r, CHUNK - n:CHUNK, :] for r, _, n in _residue_rows(d, j)], axis=0)
    return jnp.concatenate([buf[r, lo - n:lo, :] for r, lo, n in _residue_rows(d, j)], axis=0)


def _rm_store(buf, d, j, val):
    at = 0
    for r, lo, n in _residue_rows(d, j):
        buf[r, lo:lo + n, :] = val[at:at + n, :]
        at += n


def _rm_add(buf, rows, val):
    at = 0
    for r, lo, n in rows:
        buf[r, lo:lo + n, :] += val[at:at + n, :]
        at += n


def _residue_bias(sl_ref, d):
    shape = (2 * CHUNK, 2 * CHUNK)
    row = lax.broadcasted_iota(jnp.int32, shape, 0)
    col = lax.broadcasted_iota(jnp.int32, shape, 1)
    steps = _block_order(row & (CHUNK - 1), d) + CHUNK - (_block_order(col & (CHUNK - 1), d) + (col & CHUNK))
    band = (steps >= 0) & (steps <= CHUNK)
    sl = sl_ref[0:1, :]
    upper = lax.broadcasted_iota(jnp.int32, (2 * CHUNK, 1), 0) < CHUNK
    slope2 = jnp.where(upper, sl[:, 0:1], sl[:, HEAD_DIM:HEAD_DIM + 1])
    return jnp.where(band, -(float(d) * slope2 * steps.astype(F32)), NEG)


def _stack_heads(xb, head0):
    zero = jnp.zeros_like(xb)
    return jnp.concatenate([jnp.where(head0, xb, zero), jnp.where(head0, zero, xb)], axis=0).astype(BF16)


def _unstack_heads(x2, head0):
    return jnp.where(head0, x2[:CHUNK, :], x2[CHUNK:, :])


def _attn_fwd(q, k, v, shards=()):
    T = q.shape[0]
    nt = T // ATT_TILE
    ns = len(shards)
    steps = (ATTN_W // LANES) * nt

    def body(sl_ref, q_hbm, k_hbm, v_hbm, *rest):
        x_refs, rest = rest[:ns], rest[ns:]
        attn_hbm, lse_hbm = rest[:2]
        g_refs, rest = rest[2:2 + ns], rest[2 + ns:]
        qbuf, kbuf, vbuf, obuf, lbuf = rest[:5]
        o_acc, l_acc = rest[5:8], rest[8:11]
        sem_q, sem_k, sem_v, sem_o, sem_l = rest[11:16]
        hp, t = pl.program_id(0), pl.program_id(1)
        step = hp * nt + t
        two, three = step % 2, step % 3
        before, after = (step + 2) % 3, (step + 1) % 3
        if ns:
            start, forward, finish = _gather_phases(x_refs, g_refs, *rest[16:])
            pl.when(step == 0)(start)
            pl.when(step == steps // 2)(forward)

        def fetch(hp_, t_, two_, three_):
            for cp in (_tile_copies(q_hbm, qbuf.at[two_], sem_q.at[two_], hp_, t_)
                       + _tile_copies(k_hbm, kbuf.at[three_], sem_k.at[three_], hp_, t_)
                       + _tile_copies(v_hbm, vbuf.at[three_], sem_v.at[three_], hp_, t_)):
                cp.start()

        @pl.when(step == 0)
        def _():
            kbuf[2] = jnp.zeros((ATT_BLOCKS, CHUNK, LANES), F32)
            vbuf[2] = jnp.zeros((ATT_BLOCKS, CHUNK, LANES), F32)
            fetch(0, 0, 0, 0)

        @pl.when(step + 1 < steps)
        def _():
            fetch((step + 1) // nt, (step + 1) % nt, 1 - two, after)

        _wait_tile(qbuf.at[two], sem_q.at[two])
        _wait_tile(kbuf.at[three], sem_k.at[three])
        _wait_tile(vbuf.at[three], sem_v.at[three])

        @pl.when(step >= 2)
        def _():
            _wait_tile(obuf.at[two], sem_o.at[two])
            _wait_tile(lbuf.at[two], sem_l.at[two])

        q_t, k_t, v_t = qbuf.at[two], kbuf.at[three], vbuf.at[three]
        k_b, v_b = kbuf.at[before], vbuf.at[before]
        head0 = lax.broadcasted_iota(jnp.int32, (CHUNK, LANES), 1) < HEAD_DIM
        no_key_before = jnp.where(lax.broadcasted_iota(jnp.int32, (2 * CHUNK, 2 * CHUNK), 1) < CHUNK, NEG, 0.0)
        for pi, d in enumerate(DILATIONS):
            bias = _residue_bias(sl_ref, d)

            def scores(j, d=d, bias=bias):
                kcat = jnp.concatenate([_rm_block_before(k_t, k_b, d, j), _rm_block(k_t, d, j)], axis=0).astype(BF16)
                vcat = jnp.concatenate([_rm_block_before(v_t, v_b, d, j), _rm_block(v_t, d, j)], axis=0).astype(BF16)
                s = _dot(_stack_heads(_rm_block(q_t, d, j), head0), kcat, NT) + bias
                if _first_in_tile(d, j):
                    s = s + jnp.where(t == 0, 1.0, 0.0) * no_key_before
                return s, vcat

            def output(j, p, vcat, scale, lse, d=d, pi=pi):
                _rm_store(o_acc[pi], d, j, _unstack_heads(_dot(p, vcat) * scale, head0))
                _rm_store(l_acc[pi], d, j, _unstack_heads(jnp.broadcast_to(lse, (2 * CHUNK, LANES)), head0))

            for j0 in range(0, ATT_BLOCKS, SM_BLOCKS):
                group = [scores(j) for j in range(j0, j0 + SM_BLOCKS)]
                s = jnp.concatenate([g[0] for g in group], axis=0)
                m = jnp.max(s, axis=-1, keepdims=True)
                p = jnp.exp(s - m)
                l = jnp.sum(p, axis=-1, keepdims=True)
                p, scale, lse = p.astype(BF16), 1.0 / l, m + jnp.log(l)
                for i, (_, vcat) in enumerate(group):
                    rows = slice(i * 2 * CHUNK, (i + 1) * 2 * CHUNK)
                    output(j0 + i, p[rows, :], vcat, scale[rows, :], lse[rows, :])

        for r in range(ATT_BLOCKS):
            a, b, c = l_acc[0][r], l_acc[1][r], l_acc[2][r]
            m = jnp.maximum(jnp.maximum(a, b), c)
            ea, eb, ec = jnp.exp(a - m), jnp.exp(b - m), jnp.exp(c - m)
            tot = ea + eb + ec
            obuf[two, r] = (ea * o_acc[0][r] + eb * o_acc[1][r] + ec * o_acc[2][r]) / tot
            lbuf[two, r] = m + jnp.log(tot)

        for cp in (_tile_copies(attn_hbm, obuf.at[two], sem_o.at[two], hp, t, to_hbm=True)
                   + _tile_copies(lse_hbm, lbuf.at[two], sem_l.at[two], hp, t, to_hbm=True)):
            cp.start()

        @pl.when(step == steps - 1)
        def _():
            for slot in (two, 1 - two)[:min(steps, 2)]:
                _wait_tile(obuf.at[slot], sem_o.at[slot])
                _wait_tile(lbuf.at[slot], sem_l.at[slot])

        if ns:
            pl.when(step == steps - 1)(finish)

    tile = lambda n: pltpu.VMEM((n, ATT_BLOCKS, CHUNK, LANES), F32)
    dma = lambda n: pltpu.SemaphoreType.DMA((n,))
    view = jax.ShapeDtypeStruct((T // ATT_BLOCKS, ATT_BLOCKS, ATTN_W), F32)
    outs = pl.pallas_call(
        body, name="attn_fwd", grid=(ATTN_W // LANES, nt),
        in_specs=[pl.BlockSpec((8, LANES), lambda c, t: (0, c))] + [_HBM] * (3 + ns),
        out_specs=[_HBM] * (2 + ns),
        out_shape=[view, view] + [_gathered_shape(s) for s in shards],
        scratch_shapes=[tile(2), tile(3), tile(3), tile(2), tile(2)] + [pltpu.VMEM((ATT_BLOCKS, CHUNK, LANES), F32)] * 6
        + [dma(2), dma(3), dma(3), dma(2), dma(2)] + (_gather_sems(ns) if ns else []),
        compiler_params=_params(("arbitrary", "arbitrary")),
    )(_slope_table(), _residue_view(q), _residue_view(k), _residue_view(v), *shards)
    return outs[0].reshape(T, ATTN_W), outs[1].reshape(T, ATTN_W), tuple(outs[2:])


def _group_mean(v, grp):
    out = jnp.zeros_like(v)
    for g in range(N_GROUPS):
        mk = grp == g
        s = jnp.sum(jnp.where(mk, v, 0.0), axis=-1, keepdims=True) * (1.0 / HEAD_DIM)
        out = jnp.where(mk, s, out)
    return out


def _gmlp_core(uu, zz, lg, lb, ws, sb_ref, grp):
    ug, tu = _gelu(uu)
    zg, tz = _gelu(zz)
    zc = zg - _group_mean(zg, grp)
    rstd = lax.rsqrt(_group_mean(zc * zc, grp) + EPS)
    xhat = zc * rstd
    zn16 = (xhat * lg + lb).astype(BF16)
    mixed = []
    for ci in range(uu.shape[0] // CHUNK):
        rows = slice(ci * CHUNK, (ci + 1) * CHUNK)
        m = jnp.zeros((CHUNK, GMLP_W), F32)
        for g in range(N_GROUPS):
            m = jnp.where(grp[:CHUNK] == g, _dot(ws[g], zn16[rows, :]) + sb_ref[:, g:g + 1], m)
        mixed.append(m)
    return ug, tu, tz, xhat, rstd, zn16, jnp.concatenate(mixed, axis=0)


def _causal_ws(w_ref):
    ti = lax.broadcasted_iota(jnp.int32, (CHUNK, CHUNK), 0)
    si = lax.broadcasted_iota(jnp.int32, (CHUNK, CHUNK), 1)
    causal = si <= ti
    return causal, [jnp.where(causal, w_ref[g], 0.0).astype(BF16) for g in range(N_GROUPS)]


def _gmlp_fwd(u, z, ln_g, ln_b, sgu_w, sgu_bt):
    T = u.shape[0]
    tg = TM_GMLP

    def body(u_ref, z_ref, g_ref, b_ref, w_ref, sb_ref, out_ref):
        grp = lax.broadcasted_iota(jnp.int32, (tg, GMLP_W), 1) // HEAD_DIM
        _, ws = _causal_ws(w_ref)
        ug, _, _, _, _, _, mixed = _gmlp_core(u_ref[...], z_ref[...], g_ref[...], b_ref[...], ws, sb_ref, grp)
        out_ref[...] = ug * mixed

    return pl.pallas_call(
        body, name="gmlp_fwd", grid=(T // tg,),
        in_specs=[_rows(tg, GMLP_W), _rows(tg, GMLP_W), _resident((1, GMLP_W)), _resident((1, GMLP_W)),
                  _resident((N_GROUPS, CHUNK, CHUNK)), _resident((CHUNK, N_GROUPS))],
        out_specs=_rows(tg, GMLP_W),
        out_shape=jax.ShapeDtypeStruct((T, GMLP_W), F32),
        compiler_params=_params(("parallel",)),
    )(u, z, ln_g, ln_b, sgu_w, sgu_bt)


def _out_fwd(attn, gm, ga, gg, w_out, x, g2):
    T = x.shape[0]
    tm = TM_PROJ

    def body(a_ref, m_ref, ga_ref, gg_ref, w_ref, x_ref, g2_ref, mix_ref, h1_ref, hn2_ref):
        an, _ = _rms(a_ref[...])
        gn, _ = _rms(m_ref[...])
        an = (an * ga_ref[...]).astype(BF16)
        gn = (gn * gg_ref[...]).astype(BF16)
        mix_ref[:, 0:ATTN_W] = an
        mix_ref[:, ATTN_W:] = gn
        h1 = x_ref[...] + _dot(an, w_ref[0:ATTN_W, :]) + _dot(gn, w_ref[ATTN_W:, :])
        h1_ref[...] = h1
        n2, _ = _rms(h1)
        hn2_ref[...] = (n2 * g2_ref[...]).astype(BF16)

    sds = jax.ShapeDtypeStruct
    return pl.pallas_call(
        body, name="out_fwd", grid=(T // tm,),
        in_specs=[_rows(tm, ATTN_W), _rows(tm, GMLP_W), _resident((1, ATTN_W)), _resident((1, GMLP_W)),
                  _resident((D_MODEL, D_MODEL)), _rows(tm, D_MODEL), _resident((1, D_MODEL))],
        out_specs=[_rows(tm, D_MODEL)] * 3,
        out_shape=[sds((T, D_MODEL), BF16), sds((T, D_MODEL), F32), sds((T, D_MODEL), BF16)],
        compiler_params=_params(("parallel",)),
    )(attn, gm, ga, gg, w_out, x, g2)


def _ffn_fwd(hn2, h1, w1t, w2, gf, tgt):
    T = h1.shape[0]
    tm = TM_FFN

    def body(hn_ref, h1_ref, w1_ref, w2_ref, gf_ref, t_ref, r_ref, dhf_ref, dhb_ref, loss_ref, dgf_ref):
        i = pl.program_id(0)

        @pl.when(i == 0)
        def _():
            loss_ref[...] = jnp.zeros_like(loss_ref)
            dgf_ref[...] = jnp.zeros_like(dgf_ref)

        hn = hn_ref[...]
        acc = h1_ref[...]
        for j in range(D_FF // FF_CHUNK):
            cols = slice(j * FF_CHUNK, (j + 1) * FF_CHUNK)
            r = jnp.maximum(_dot(hn, w1_ref[cols, :], NT), 0.0)
            r_ref[:, cols] = r.astype(BF16)
            act = jnp.square(r).astype(BF16)
            acc = acc + _dot(act, w2_ref[cols, :])
        n3, r3 = _rms(acc)
        gf_row = gf_ref[...]
        e = n3 * gf_row - t_ref[...]
        loss_ref[...] += 0.5 * jnp.sum(jnp.mean(e * e, axis=-1, keepdims=True))
        dy = e * (1.0 / D_MODEL)
        _accum_rows(dgf_ref, dy * n3)
        dh2 = _rms_bwd(n3, r3, gf_row, dy)
        dhf_ref[...] = dh2
        dhb_ref[...] = dh2.astype(BF16)

    sds = jax.ShapeDtypeStruct
    acc_spec = lambda n: pl.BlockSpec((8, n), lambda i: (0, 0))
    return pl.pallas_call(
        body, name="ffn_fwd", grid=(T // tm,),
        in_specs=[_rows(tm, D_MODEL), _rows(tm, D_MODEL), _resident((D_FF, D_MODEL)), _resident((D_FF, D_MODEL)),
                  _resident((1, D_MODEL)), _rows(tm, D_MODEL)],
        out_specs=[_rows(tm, D_FF), _rows(tm, D_MODEL), _rows(tm, D_MODEL), acc_spec(LANES), acc_spec(D_MODEL)],
        out_shape=[sds((T, D_FF), BF16), sds((T, D_MODEL), F32), sds((T, D_MODEL), BF16),
                   sds((8, LANES), F32), sds((8, D_MODEL), F32)],
        compiler_params=_params(("arbitrary",)),
    )(hn2, h1, w1t, w2, gf, tgt)


def _ffn_bwd(dh2b, dh2f, relu, h1, g2, w2, w1t):
    T = h1.shape[0]
    tm = TM_FFN

    def body(db_ref, df_ref, r_ref, h1_ref, g2_ref, w2_ref, w1t_ref, da_ref, d1f_ref, d1b_ref, dg_ref):
        @pl.when(pl.program_id(0) == 0)
        def _():
            dg_ref[...] = jnp.zeros_like(dg_ref)

        db = db_ref[...]
        acc = jnp.zeros((tm, D_MODEL), F32)
        for j in range(D_FF // FF_CHUNK):
            cols = slice(j * FF_CHUNK, (j + 1) * FF_CHUNK)
            da = (_dot(db, w2_ref[cols, :], NT) * (2.0 * r_ref[:, cols].astype(F32))).astype(BF16)
            da_ref[:, cols] = da
            acc = acc + _dot(da, w1t_ref[cols, :])
        n2, r2 = _rms(h1_ref[...])
        _accum_rows(dg_ref, acc * n2)
        dh1 = df_ref[...] + _rms_bwd(n2, r2, g2_ref[...], acc)
        d1f_ref[...] = dh1
        d1b_ref[...] = dh1.astype(BF16)

    sds = jax.ShapeDtypeStruct
    return pl.pallas_call(
        body, name="ffn_bwd", grid=(T // tm,),
        in_specs=[_rows(tm, D_MODEL), _rows(tm, D_MODEL), _rows(tm, D_FF), _rows(tm, D_MODEL),
                  _resident((1, D_MODEL)), _resident((D_FF, D_MODEL)), _resident((D_FF, D_MODEL))],
        out_specs=[_rows(tm, D_FF), _rows(tm, D_MODEL), _rows(tm, D_MODEL),
                   pl.BlockSpec((8, D_MODEL), lambda i: (0, 0))],
        out_shape=[sds((T, D_FF), BF16), sds((T, D_MODEL), F32), sds((T, D_MODEL), BF16), sds((8, D_MODEL), F32)],
        compiler_params=_params(("arbitrary",)),
    )(dh2b, dh2f, relu, h1, g2, w2, w1t)


def _out_bwd(dh1b, w_out, attn, gm, ga, gg):
    T = attn.shape[0]
    tm = TM_PROJ

    def body(d_ref, w_ref, a_ref, m_ref, ga_ref, gg_ref, da_ref, dm_ref, dga_ref, dgg_ref):
        @pl.when(pl.program_id(0) == 0)
        def _():
            dga_ref[...] = jnp.zeros_like(dga_ref)
            dgg_ref[...] = jnp.zeros_like(dgg_ref)

        d = d_ref[...]
        dan = _dot(d, w_ref[0:ATTN_W, :], NT)
        dgn = _dot(d, w_ref[ATTN_W:, :], NT)
        na, ra = _rms(a_ref[...])
        ng, rg = _rms(m_ref[...])
        _accum_rows(dga_ref, dan * na)
        _accum_rows(dgg_ref, dgn * ng)
        da_ref[...] = _rms_bwd(na, ra, ga_ref[...], dan)
        dm_ref[...] = _rms_bwd(ng, rg, gg_ref[...], dgn)

    sds = jax.ShapeDtypeStruct
    return pl.pallas_call(
        body, name="out_bwd", grid=(T // tm,),
        in_specs=[_rows(tm, D_MODEL), _resident((D_MODEL, D_MODEL)), _rows(tm, ATTN_W), _rows(tm, GMLP_W),
                  _resident((1, ATTN_W)), _resident((1, GMLP_W))],
        out_specs=[_rows(tm, ATTN_W), _rows(tm, GMLP_W), pl.BlockSpec((8, ATTN_W), lambda i: (0, 0)),
                   pl.BlockSpec((8, GMLP_W), lambda i: (0, 0))],
        out_shape=[sds((T, ATTN_W), F32), sds((T, GMLP_W), F32), sds((8, ATTN_W), F32), sds((8, GMLP_W), F32)],
        compiler_params=_params(("arbitrary",)),
    )(dh1b, w_out, attn, gm, ga, gg)


def _gmlp_bwd(u, z, dgm, ln_g, ln_b, sgu_w, sgu_bt):
    T = u.shape[0]
    tg = TM_GMLP
    nsteps = T // tg

    def body(u_ref, z_ref, d_ref, g_ref, b_ref, w_ref, sb_ref, duz_ref, dlg_ref, dlb_ref, dw_ref, dsb_ref):
        i = pl.program_id(0)

        @pl.when(i == 0)
        def _():
            for ref in (dlg_ref, dlb_ref, dw_ref, dsb_ref):
                ref[...] = jnp.zeros_like(ref)

        grp = lax.broadcasted_iota(jnp.int32, (tg, GMLP_W), 1) // HEAD_DIM
        lane = lax.broadcasted_iota(jnp.int32, (CHUNK, LANES), 1)
        causal, ws = _causal_ws(w_ref)
        lg = g_ref[...]
        uu, zz, dgm = u_ref[...], z_ref[...], d_ref[...]
        ug, tu, tz, xhat, rstd, zn16, mixed = _gmlp_core(uu, zz, lg, b_ref[...], ws, sb_ref, grp)
        dmx = dgm * ug
        duz_ref[:, 0:GMLP_W] = dgm * mixed * _gelu_grad(uu, tu)
        dmx16 = dmx.astype(BF16)
        dzn = []
        for ci in range(tg // CHUNK):
            rows = slice(ci * CHUNK, (ci + 1) * CHUNK)
            dmx_c, d = dmx16[rows, :], jnp.zeros((CHUNK, GMLP_W), F32)
            for g in range(N_GROUPS):
                mk = grp[:CHUNK] == g
                d = jnp.where(mk, _dot(ws[g], dmx_c, TN), d)
                dw_ref[g] += _dot(jnp.where(mk, dmx_c, jnp.zeros_like(dmx_c)), zn16[rows, :], NT)
            dzn.append(d)
        dzn = jnp.concatenate(dzn, axis=0)
        dsb = jnp.zeros((CHUNK, LANES), F32)
        for g in range(N_GROUPS):
            per_token = jnp.sum(jnp.where(grp == g, dmx, 0.0), axis=-1, keepdims=True)
            by_position = sum(per_token[ci * CHUNK:(ci + 1) * CHUNK] for ci in range(tg // CHUNK))
            dsb = jnp.where(lane == g, by_position, dsb)
        dsb_ref[...] += dsb
        _accum_rows(dlg_ref, dzn * xhat)
        _accum_rows(dlb_ref, dzn)
        dxh = dzn * lg
        dzg = rstd * (dxh - _group_mean(dxh, grp) - xhat * _group_mean(dxh * xhat, grp))
        duz_ref[:, GMLP_W:] = dzg * _gelu_grad(zz, tz)

        @pl.when(i == nsteps - 1)
        def _():
            for g in range(N_GROUPS):
                dw_ref[g] = jnp.where(causal, dw_ref[g], 0.0)

    sds = jax.ShapeDtypeStruct
    return pl.pallas_call(
        body, name="gmlp_bwd", grid=(nsteps,),
        in_specs=[_rows(tg, GMLP_W)] * 3 + [_resident((1, GMLP_W)), _resident((1, GMLP_W)),
                                              _resident((N_GROUPS, CHUNK, CHUNK)), _resident((CHUNK, N_GROUPS))],
        out_specs=[_rows(tg, 2 * GMLP_W), pl.BlockSpec((8, GMLP_W), lambda i: (0, 0)),
                   pl.BlockSpec((8, GMLP_W), lambda i: (0, 0)),
                   pl.BlockSpec((N_GROUPS, CHUNK, CHUNK), lambda i: (0, 0, 0)),
                   pl.BlockSpec((CHUNK, LANES), lambda i: (0, 0))],
        out_shape=[sds((T, 2 * GMLP_W), F32), sds((8, GMLP_W), F32), sds((8, GMLP_W), F32),
                   sds((N_GROUPS, CHUNK, CHUNK), F32), sds((CHUNK, LANES), F32)],
        compiler_params=_params(("arbitrary",)),
    )(u, z, dgm, ln_g, ln_b, sgu_w, sgu_bt)


def _attn_bwd(q, k, v, dattn, attn, lse, duz, owner_grads=()):
    T = q.shape[0]
    nt = T // ATT_TILE
    ns = len(owner_grads)
    steps = (ATTN_W // LANES) * nt

    def body(sl_ref, q_hbm, k_hbm, v_hbm, do_hbm, o_hbm, lse_hbm, duz_hbm, *rest):
        p_refs, rest = rest[:ns], rest[ns:]
        dq_hbm = dk_hbm = dv_hbm = rest[0]
        r_refs, rest = rest[1:1 + ns], rest[1 + ns:]
        qbuf, dobuf, obuf, lbuf, kbuf, vbuf, dqbuf, dkbuf, dvbuf, delta_s = rest[:10]
        sem_q, sem_do, sem_o, sem_l, sem_k, sem_v, sem_dq, sem_dk, sem_dv, sem_uz = rest[10:20]
        hp, t = pl.program_id(0), pl.program_id(1)
        step = hp * nt + t
        two, three = step % 2, step % 3
        before, after = (step + 2) % 3, (step + 1) % 3
        if ns:
            start, finish = _owner_exchange_phases(p_refs, r_refs, *rest[20:])
            pl.when(step == 0)(start)

        def fetch(hp_, t_, two_, three_):
            for hbm, buf, sem, slot in ((q_hbm, qbuf, sem_q, two_), (do_hbm, dobuf, sem_do, two_),
                                        (o_hbm, obuf, sem_o, two_), (lse_hbm, lbuf, sem_l, two_),
                                        (k_hbm, kbuf, sem_k, three_), (v_hbm, vbuf, sem_v, three_)):
                for cp in _tile_copies(hbm, buf.at[slot], sem.at[slot], hp_, t_):
                    cp.start()

        @pl.when(step == 0)
        def _():
            kbuf[2] = jnp.zeros((ATT_BLOCKS, CHUNK, LANES), F32)
            vbuf[2] = jnp.zeros((ATT_BLOCKS, CHUNK, LANES), F32)
            dkbuf[3] = jnp.zeros((ATT_BLOCKS, CHUNK, LANES), F32)
            dvbuf[3] = jnp.zeros((ATT_BLOCKS, CHUNK, LANES), F32)
            fetch(0, 0, 0, 0)

        @pl.when(step + 1 < steps)
        def _():
            fetch((step + 1) // nt, (step + 1) % nt, 1 - two, after)

        uz_rows = pl.ds(pl.multiple_of(t * CHUNK + hp * (CHUNK // 4), CHUNK // 4), CHUNK // 4)
        uz_copy = pltpu.make_async_copy(duz_hbm.at[uz_rows],
                                        dq_hbm.at[uz_rows, :, pl.ds(3 * ATTN_W, 2 * GMLP_W)], sem_uz)
        pl.when(hp < 4)(uz_copy.start)

        for buf, sem in ((qbuf, sem_q), (dobuf, sem_do), (obuf, sem_o), (lbuf, sem_l)):
            _wait_tile(buf.at[two], sem.at[two])
        _wait_tile(kbuf.at[three], sem_k.at[three])
        _wait_tile(vbuf.at[three], sem_v.at[three])

        @pl.when(step >= 2)
        def _():
            _wait_tile(dqbuf.at[two], sem_dq.at[two])

        @pl.when(step >= 3)
        def _():
            _wait_tile(dkbuf.at[three], sem_dk.at[three])
            _wait_tile(dvbuf.at[three], sem_dv.at[three])

        zero_tile = jnp.zeros((ATT_BLOCKS, CHUNK, LANES), F32)
        dqbuf[two] = zero_tile
        dkbuf[three] = zero_tile
        dvbuf[three] = zero_tile

        q_t, do_t, l_t, k_t, v_t = qbuf.at[two], dobuf.at[two], lbuf.at[two], kbuf.at[three], vbuf.at[three]
        k_b, v_b = kbuf.at[before], vbuf.at[before]
        dq_t, dk_t, dv_t = dqbuf.at[two], dkbuf.at[three], dvbuf.at[three]
        dk_b, dv_b = dkbuf.at[before], dvbuf.at[before]
        sink = jnp.where(t > 0, before, 3)
        dk_sink, dv_sink = dkbuf.at[sink], dvbuf.at[sink]
        head0 = lax.broadcasted_iota(jnp.int32, (CHUNK, LANES), 1) < HEAD_DIM
        for r in range(ATT_BLOCKS):
            dd = dobuf[two, r] * obuf[two, r]
            d0 = jnp.sum(jnp.where(head0, dd, 0.0), axis=-1, keepdims=True)
            d1 = jnp.sum(jnp.where(head0, 0.0, dd), axis=-1, keepdims=True)
            delta_s[r] = jnp.where(head0, d0, d1)

        def column(xb):
            return jnp.concatenate([xb[:, 0:1], xb[:, HEAD_DIM:HEAD_DIM + 1]], axis=0)

        no_key_before = jnp.where(lax.broadcasted_iota(jnp.int32, (2 * CHUNK, 2 * CHUNK), 1) < CHUNK, NEG, 0.0)
        for d in DILATIONS:
            bias = _residue_bias(sl_ref, d)
            for j in range(ATT_BLOCKS):
                kcat = jnp.concatenate([_rm_block_before(k_t, k_b, d, j), _rm_block(k_t, d, j)], axis=0).astype(BF16)
                vcat = jnp.concatenate([_rm_block_before(v_t, v_b, d, j), _rm_block(v_t, d, j)], axis=0).astype(BF16)
                q2 = _stack_heads(_rm_block(q_t, d, j), head0)
                do2 = _stack_heads(_rm_block(do_t, d, j), head0)
                s = _dot(q2, kcat, NT) + bias
                if _first_in_tile(d, j):
                    s = s + jnp.where(t == 0, 1.0, 0.0) * no_key_before
                p = jnp.exp(s - column(_rm_block(l_t, d, j)))
                ds = (p * (_dot(do2, vcat, NT) - column(_rm_block(delta_s, d, j)))).astype(BF16)
                _rm_add(dq_t, _residue_rows(d, j), _unstack_heads(_dot(ds, kcat), head0))
                ck = _dot(ds, q2, TN)
                cv = _dot(p.astype(BF16), do2, TN)
                _rm_add(dk_t, _residue_rows(d, j), ck[CHUNK:, :])
                _rm_add(dv_t, _residue_rows(d, j), cv[CHUNK:, :])
                if _first_in_tile(d, j):
                    rows = [(r, CHUNK - n, n) for r, _, n in _residue_rows(d, j)]
                    _rm_add(dk_sink, rows, ck[:CHUNK, :])
                    _rm_add(dv_sink, rows, cv[:CHUNK, :])
                else:
                    rows = [(r, lo - n, n) for r, lo, n in _residue_rows(d, j)]
                    _rm_add(dk_t, rows, ck[:CHUNK, :])
                    _rm_add(dv_t, rows, cv[:CHUNK, :])

        for r in range(ATT_BLOCKS):
            dqbuf[two, r] = dqbuf[two, r] * Q_SCALE
        for cp in _tile_copies(dq_hbm, dq_t, sem_dq.at[two], hp, t, to_hbm=True):
            cp.start()
        pl.when(hp < 4)(uz_copy.wait)

        @pl.when(t > 0)
        def _():
            for cp in (_tile_copies(dk_hbm, dk_b, sem_dk.at[before], hp, t - 1, to_hbm=True, lane0=ATTN_W)
                       + _tile_copies(dv_hbm, dv_b, sem_dv.at[before], hp, t - 1, to_hbm=True, lane0=2 * ATTN_W)):
                cp.start()

        @pl.when(t == nt - 1)
        def _():
            for cp in (_tile_copies(dk_hbm, dk_t, sem_dk.at[three], hp, t, to_hbm=True, lane0=ATTN_W)
                       + _tile_copies(dv_hbm, dv_t, sem_dv.at[three], hp, t, to_hbm=True, lane0=2 * ATTN_W)):
                cp.start()

        @pl.when(step == steps - 1)
        def _():
            for slot in range(2):
                _wait_tile(dqbuf.at[slot], sem_dq.at[slot])
            for slot in range(3):
                _wait_tile(dkbuf.at[slot], sem_dk.at[slot])
                _wait_tile(dvbuf.at[slot], sem_dv.at[slot])

        if ns:
            pl.when(step == steps - 1)(finish)

    tile = lambda n: pltpu.VMEM((n, ATT_BLOCKS, CHUNK, LANES), F32)
    dma = lambda n: pltpu.SemaphoreType.DMA((n,))
    view = jax.ShapeDtypeStruct((T // ATT_BLOCKS, ATT_BLOCKS, ATTN_W), F32)
    outs = pl.pallas_call(
        body, name="attn_bwd", grid=(ATTN_W // LANES, nt),
        in_specs=[pl.BlockSpec((8, LANES), lambda c, t: (0, c))] + [_HBM] * (7 + ns),
        out_specs=[_HBM] * (1 + ns),
        out_shape=[jax.ShapeDtypeStruct((T // ATT_BLOCKS, ATT_BLOCKS, IN_W), F32)]
        + [jax.ShapeDtypeStruct(p.shape, p.dtype) for p in owner_grads],
        scratch_shapes=[tile(2), tile(2), tile(2), tile(2), tile(3), tile(3), tile(2), tile(4), tile(4),
                        pltpu.VMEM((ATT_BLOCKS, CHUNK, LANES), F32)]
        + [dma(2), dma(2), dma(2), dma(2), dma(3), dma(3), dma(2), dma(3), dma(3), pltpu.SemaphoreType.DMA]
        + (_owner_exchange_sems(ns) if ns else []),
        compiler_params=_params(("arbitrary", "arbitrary")),
    )(_slope_table(), *[_residue_view(a) for a in (q, k, v, dattn, attn, lse, duz)], *owner_grads)
    return outs[0].reshape(T, IN_W), tuple(outs[1:])


def _proj_bwd(dproj, w_in_t, x, g1, dh1, chip_sums=()):
    T = x.shape[0]
    tm = TM_PROJ
    ns = len(chip_sums)
    steps = T // tm

    def body(d_ref, w_ref, x_ref, g_ref, r_ref, *rest):
        p_refs, rest = rest[:ns], rest[ns:]
        dx_ref, dg_ref = rest[:2]
        r_refs, sems = rest[2:2 + ns], rest[2 + ns:]
        step = pl.program_id(0)
        if ns:
            start, finish = _chip_exchange_phases(p_refs, r_refs, *sems)
            pl.when(step == 0)(start)

        @pl.when(step == 0)
        def _():
            dg_ref[...] = jnp.zeros_like(dg_ref)

        dhn = _dot(d_ref[...].astype(BF16), w_ref[...])
        n1, r1 = _rms(x_ref[...])
        _accum_rows(dg_ref, dhn * n1)
        dx_ref[...] = r_ref[...] + _rms_bwd(n1, r1, g_ref[...], dhn)
        if ns:
            pl.when(step == steps - 1)(finish)

    outs = pl.pallas_call(
        body, name="proj_bwd", grid=(steps,),
        in_specs=[_rows(tm, IN_W), _resident((IN_W, D_MODEL)), _rows(tm, D_MODEL), _resident((1, D_MODEL)),
                  _rows(tm, D_MODEL)] + [_HBM] * ns,
        out_specs=[_rows(tm, D_MODEL), pl.BlockSpec((8, D_MODEL), lambda i: (0, 0))] + [_HBM] * ns,
        out_shape=[jax.ShapeDtypeStruct((T, D_MODEL), F32), jax.ShapeDtypeStruct((8, D_MODEL), F32)]
        + [jax.ShapeDtypeStruct(p.shape, p.dtype) for p in chip_sums],
        scratch_shapes=_chip_exchange_sems(ns) if ns else [],
        compiler_params=_params(("arbitrary",)),
    )(dproj, w_in_t, x, g1, dh1, *chip_sums)
    return outs[0], outs[1], tuple(outs[2:])


def _dw(a, b, name, tile, square_a=False, out_dtype=F32):
    T, ka = a.shape
    nb = b.shape[1]
    tka, tnb, tt = tile
    tt = min(tt, T)
    last = T // tt - 1

    def body(a_ref, b_ref, *refs):
        o_ref = refs[0]
        acc_ref = refs[1] if len(refs) > 1 else o_ref
        s = pl.program_id(2)

        @pl.when(s == 0)
        def _():
            acc_ref[...] = jnp.zeros_like(acc_ref)

        a_tile = a_ref[...]
        if square_a:
            a_tile = jnp.square(a_tile.astype(F32))
        acc_ref[...] += _dot(a_tile.astype(BF16), b_ref[...], TN)
        if acc_ref is not o_ref:
            @pl.when(s == last)
            def _():
                o_ref[...] = acc_ref[...].astype(out_dtype)

    return pl.pallas_call(
        body, name=name, grid=(ka // tka, nb // tnb, T // tt),
        in_specs=[pl.BlockSpec((tt, tka), lambda i, j, s: (s, i)), pl.BlockSpec((tt, tnb), lambda i, j, s: (s, j))],
        out_specs=pl.BlockSpec((tka, tnb), lambda i, j, s: (i, j)),
        out_shape=jax.ShapeDtypeStruct((ka, nb), out_dtype),
        scratch_shapes=[] if out_dtype == F32 else [pltpu.VMEM((tka, tnb), F32)],
        compiler_params=_params(("parallel", "parallel", "arbitrary")),
    )(a, b)


def _adamw(w, m, v, parts, name, tr, transposed=False):
    R, C = w.shape
    P = parts.shape[0]

    def body(w_ref, m_ref, v_ref, p_ref, g_ref, d_ref, m2_ref, v2_ref):
        g = p_ref[0].astype(F32)
        for i in range(1, P):
            g = g + p_ref[i].astype(F32)
        if transposed:
            g = g.T
        m2 = ADAM_B1 * m_ref[...] + (1.0 - ADAM_B1) * g
        v2 = ADAM_B2 * v_ref[...] + (1.0 - ADAM_B2) * jnp.square(g)
        m_hat = m2 / (1.0 - ADAM_B1 ** ADAM_STEP)
        v_hat = v2 / (1.0 - ADAM_B2 ** ADAM_STEP)
        g_ref[...] = g
        d_ref[...] = -ADAM_LR * (m_hat / (jnp.sqrt(v_hat) + ADAM_EPS) + ADAM_WD * w_ref[...])
        m2_ref[...] = m2
        v2_ref[...] = v2

    spec = _rows(tr, C)
    part_spec = (pl.BlockSpec((P, C, tr), lambda i: (0, 0, i)) if transposed
                 else pl.BlockSpec((P, tr, C), lambda i: (0, i, 0)))
    return pl.pallas_call(
        body, name=name, grid=(R // tr,),
        in_specs=[spec, spec, spec, part_spec],
        out_specs=[spec] * 4,
        out_shape=[jax.ShapeDtypeStruct((R, C), F32)] * 4,
        compiler_params=_params(("parallel",)),
    )(w, m, v, parts)


def _pair_sum(core, grad, recv, name):
    _, _, n, C = grad.shape
    tr = n // 2

    def body(c_ref, a_ref, b_ref, o_ref):
        o_ref[...] = a_ref[...] + b_ref[...]

    spec = pl.BlockSpec((1, tr, C), lambda i, j, c_ref: (i, j, 0))
    return pl.pallas_call(
        body, name=name,
        grid_spec=pltpu.PrefetchScalarGridSpec(
            num_scalar_prefetch=1, grid=(4, n // tr),
            in_specs=[pl.BlockSpec((1, None, tr, C), lambda i, j, c_ref: (i, c_ref[0], j, 0)), spec],
            out_specs=spec),
        out_shape=jax.ShapeDtypeStruct(recv.shape, F32),
        compiler_params=_params(("parallel", "parallel")),
    )(core.reshape(1), grad, recv)


_HBM = pl.BlockSpec(memory_space=pltpu.HBM)


def _place():
    return lax.axis_index("x"), lax.axis_index("y"), lax.axis_index("c")


def _gathered_shape(shard):
    return jax.ShapeDtypeStruct((N_DEV,) + shard.shape, shard.dtype)


def _gather_sems(n):
    return [pltpu.SemaphoreType.DMA((7, n)), pltpu.SemaphoreType.DMA((7, n)), pltpu.SemaphoreType.DMA((n,))]


def _gather_phases(x_refs, out_refs, send_sems, recv_sems, local_sems):
    x, y, c = _place()
    me, sibling = (x, y, c), (x, y, 1 - c)
    chips = [(1 - x, y), (x, 1 - y), (1 - x, 1 - y)]
    arrays = range(len(x_refs))

    def slot(i, px, py, pc):
        return out_refs[i].at[4 * px + 2 * py + pc]

    def copy(i, k, block, to, own=False):
        return pltpu.make_async_remote_copy(
            src_ref=x_refs[i] if own else slot(i, *block), dst_ref=slot(i, *block),
            send_sem=send_sems.at[k, i], recv_sem=recv_sems.at[k, i], device_id=to, device_id_type=MESH)

    def mine(i):
        return pltpu.make_async_copy(x_refs[i], slot(i, *me), local_sems.at[i])

    def start():
        for i in arrays:
            mine(i).start()
            copy(i, 0, me, sibling, own=True).start()
            for j, chip in enumerate(chips):
                copy(i, 1 + j, me, (*chip, c), own=True).start()

    def forward():
        for i in arrays:
            for j, chip in enumerate(chips):
                copy(i, 1 + j, (*chip, c), me).wait_recv()
                copy(i, 4 + j, (*chip, c), sibling).start()

    def finish():
        for i in arrays:
            copy(i, 0, sibling, me).wait_recv()
            copy(i, 0, me, sibling, own=True).wait_send()
            for j, chip in enumerate(chips):
                copy(i, 4 + j, (*chip, 1 - c), me).wait_recv()
                copy(i, 1 + j, me, (*chip, c), own=True).wait_send()
                copy(i, 4 + j, (*chip, c), sibling).wait_send()
            mine(i).wait()

    return start, forward, finish


def _all_gather(shards, name):
    n = len(shards)

    def body(*refs):
        start, forward, finish = _gather_phases(refs[:n], refs[n:2 * n], *refs[2 * n:])
        start()
        forward()
        finish()

    return pl.pallas_call(
        body, name=name,
        out_shape=[_gathered_shape(s) for s in shards],
        in_specs=[_HBM] * n, out_specs=[_HBM] * n,
        scratch_shapes=_gather_sems(n),
    )(*shards)


def _sibling_exchange(grads, name):
    n = len(grads)

    def body(*refs):
        g_refs, r_refs, send_sems, recv_sems = refs[:n], refs[n:2 * n], refs[2 * n], refs[2 * n + 1]
        x, y, c = _place()
        copies = [pltpu.make_async_remote_copy(
            src_ref=g_refs[i].at[:, 1 - c], dst_ref=r_refs[i], send_sem=send_sems.at[i], recv_sem=recv_sems.at[i],
            device_id=(x, y, 1 - c), device_id_type=MESH) for i in range(n)]
        for cp in copies:
            cp.start()
        for cp in copies:
            cp.wait()

    return pl.pallas_call(
        body, name=name,
        out_shape=[jax.ShapeDtypeStruct((g.shape[0],) + g.shape[2:], g.dtype) for g in grads],
        in_specs=[_HBM] * n, out_specs=[_HBM] * n,
        scratch_shapes=[pltpu.SemaphoreType.DMA((n,)), pltpu.SemaphoreType.DMA((n,))],
    )(*grads)


def _owner_exchange_sems(n):
    return [pltpu.SemaphoreType.DMA((7, n)), pltpu.SemaphoreType.DMA((7, n)), pltpu.SemaphoreType.DMA((n,))]


def _owner_exchange_phases(g_refs, r_refs, send_sems, recv_sems, local_sems):
    x, y, c = _place()
    me = 4 * x + 2 * y + c
    flip = lambda v, bit: 1 - v if bit else v
    peers = [(flip(x, k & 4), flip(y, k & 2), flip(c, k & 1)) for k in range(1, N_DEV)]
    arrays = range(len(g_refs))

    def mine(i):
        return pltpu.make_async_copy(g_refs[i].at[me], r_refs[i].at[me], local_sems.at[i])

    def copy(i, k, src_slot, dst_slot):
        return pltpu.make_async_remote_copy(
            src_ref=g_refs[i].at[src_slot], dst_ref=r_refs[i].at[dst_slot],
            send_sem=send_sems.at[k, i], recv_sem=recv_sems.at[k, i], device_id=peers[k], device_id_type=MESH)

    def start():
        for i in arrays:
            mine(i).start()
            for k, (px, py, pc) in enumerate(peers):
                copy(i, k, 4 * px + 2 * py + pc, me).start()

    def finish():
        for i in arrays:
            for k, (px, py, pc) in enumerate(peers):
                copy(i, k, me, 4 * px + 2 * py + pc).wait_recv()
                copy(i, k, 4 * px + 2 * py + pc, me).wait_send()
            mine(i).wait()

    return start, finish


def _chip_exchange_sems(n):
    return [pltpu.SemaphoreType.DMA((3, n)), pltpu.SemaphoreType.DMA((3, n)), pltpu.SemaphoreType.DMA((n,))]


def _chip_exchange_phases(p_refs, r_refs, send_sems, recv_sems, local_sems):
    x, y, c = _place()
    my_chip = 2 * x + y
    chips = [(1 - x, y), (x, 1 - y), (1 - x, 1 - y)]
    arrays = range(len(p_refs))

    def mine(i):
        return pltpu.make_async_copy(p_refs[i].at[my_chip], r_refs[i].at[my_chip], local_sems.at[i])

    def copy(i, k, src_chip, dst_chip):
        px, py = chips[k]
        return pltpu.make_async_remote_copy(
            src_ref=p_refs[i].at[src_chip], dst_ref=r_refs[i].at[dst_chip],
            send_sem=send_sems.at[k, i], recv_sem=recv_sems.at[k, i], device_id=(px, py, c), device_id_type=MESH)

    def start():
        for i in arrays:
            mine(i).start()
            for k, (px, py) in enumerate(chips):
                copy(i, k, 2 * px + py, my_chip).start()

    def finish():
        for i in arrays:
            for k, (px, py) in enumerate(chips):
                copy(i, k, my_chip, 2 * px + py).wait_recv()
                copy(i, k, 2 * px + py, my_chip).wait_send()
            mine(i).wait()

    return start, finish


_R_IN, _R_OUT, _R_FF = IN_W // N_DEV, D_MODEL // N_DEV, D_FF // N_DEV


def _by_owner(g):
    return g.reshape(4, 2, g.shape[0] // N_DEV, D_MODEL)


def _local_step(x, tgt, small, w_in_t, rest, core=None):
    exchange = core is not None
    g1, g2, gf = small["norm1_g"], small["norm2_g"], small["final_norm_g"].reshape(1, D_MODEL)
    ga, gg = small["attn_out_g"], small["gmlp_out_g"]
    ln_g = small["sgu_ln_g"].reshape(1, GMLP_W)
    ln_b = small["sgu_ln_b"].reshape(1, GMLP_W)
    sgu_w = small["sgu_w"][0]
    sgu_bt = small["sgu_b"][0].T

    hn1, q, k, v, u, z = _proj_fwd(x, g1, w_in_t)
    attn, lse, gathered = _attn_fwd(q, k, v, shards=rest if exchange else ())
    w_out, w_ff1_t, w_ff2 = [g.reshape(-1, D_MODEL) for g in gathered] if exchange else rest
    gm = _gmlp_fwd(u, z, ln_g, ln_b, sgu_w, sgu_bt)
    mixed, h1, hn2 = _out_fwd(attn, gm, ga, gg, w_out, x, g2)
    relu, dh2f, dh2b, loss8, dgf8 = _ffn_fwd(hn2, h1, w_ff1_t, w_ff2, gf, tgt)

    da, dh1f, dh1b, dg2 = _ffn_bwd(dh2b, dh2f, relu, h1, g2, w_ff2, w_ff1_t)
    wire = BF16 if exchange else F32
    dw_ff2 = _dw(relu, dh2b, "dw_ff2", DW_TILE, square_a=True, out_dtype=wire)
    dw_ff1_t = _dw(da, hn2, "dw_ff1", DW_TILE, out_dtype=wire)
    dattn, dgm, dga, dgg = _out_bwd(dh1b, w_out, attn, gm, ga, gg)
    dw_out = _dw(mixed, dh1b, "dw_out", DW_TILE, out_dtype=wire)
    early = [dw_out, dw_ff1_t, dw_ff2]
    if exchange:
        early = [g.reshape(N_DEV, -1, D_MODEL) for g in early]
    duz, dlg, dlb, dsw, dsb = _gmlp_bwd(u, z, dgm, ln_g, ln_b, sgu_w, sgu_bt)
    dproj, arrived = _attn_bwd(q, k, v, dattn, attn, lse, duz, owner_grads=early if exchange else ())
    dw_in_t = _dw(dproj, hn1, "dw_in", DW_TILE_IN)
    late = ()
    if exchange:
        by_owner = _by_owner(dw_in_t)
        got, = _sibling_exchange([by_owner], "grad_sibling_exchange")
        late = (_pair_sum(core, by_owner, got, "grad_pair_sum"),)
    dx, dg1, late = _proj_bwd(dproj, w_in_t, x, g1, dh1f, chip_sums=late)
    if exchange:
        dw_in_t, early = late[0], arrived

    small_grads = dict(
        norm1_g=dg1[0], sgu_ln_g=dlg[0], sgu_ln_b=dlb[0], sgu_w=dsw, sgu_b=dsb[:, :N_GROUPS].T,
        attn_out_g=dga[0], gmlp_out_g=dgg[0], norm2_g=dg2[0], final_norm_g=dgf8[0])
    return loss8[0, 0], dx, (dw_in_t, *early), small_grads


SMALL_NAMES = ("norm1_g", "sgu_ln_g", "sgu_ln_b", "sgu_w", "sgu_b", "attn_out_g", "gmlp_out_g", "norm2_g",
               "final_norm_g")
WEIGHT_ORDER = ("norm1_g", "w_in", "sgu_ln_g", "sgu_ln_b", "sgu_w", "sgu_b", "attn_out_g", "gmlp_out_g", "w_out",
                "norm2_g", "w_ff1", "w_ff2", "final_norm_g")


TINY_NAMES = tuple(n for n in SMALL_NAMES if n != "sgu_w")
TINY_ROWS = 48


def _pack_tiny(d, last):
    return jnp.concatenate([d[n].reshape(-1, LANES) for n in TINY_NAMES] + [last], axis=0)


def _unpack_tiny(p, like):
    out, r = {}, 0
    for n in TINY_NAMES:
        rows = like[n].size // LANES
        out[n] = p[r:r + rows].reshape(like[n].shape)
        r += rows
    return out


def kernel(x, norm1_g, w_in, sgu_ln_g, sgu_ln_b, sgu_w, sgu_b, attn_out_g, gmlp_out_g, w_out, norm2_g, w_ff1, w_ff2, final_norm_g, loss_target, m_norm1_g, m_w_in, m_sgu_ln_g, m_sgu_ln_b, m_sgu_w, m_sgu_b, m_attn_out_g, m_gmlp_out_g, m_w_out, m_norm2_g, m_w_ff1, m_w_ff2, m_final_norm_g, v_norm1_g, v_w_in, v_sgu_ln_g, v_sgu_ln_b, v_sgu_w, v_sgu_b, v_attn_out_g, v_gmlp_out_g, v_w_out, v_norm2_g, v_w_ff1, v_w_ff2, v_final_norm_g):
    w = dict(norm1_g=norm1_g, w_in=w_in, sgu_ln_g=sgu_ln_g, sgu_ln_b=sgu_ln_b, sgu_w=sgu_w, sgu_b=sgu_b,
             attn_out_g=attn_out_g, gmlp_out_g=gmlp_out_g, w_out=w_out, norm2_g=norm2_g, w_ff1=w_ff1, w_ff2=w_ff2,
             final_norm_g=final_norm_g)
    m = dict(norm1_g=m_norm1_g, w_in=m_w_in, sgu_ln_g=m_sgu_ln_g, sgu_ln_b=m_sgu_ln_b, sgu_w=m_sgu_w, sgu_b=m_sgu_b,
             attn_out_g=m_attn_out_g, gmlp_out_g=m_gmlp_out_g, w_out=m_w_out, norm2_g=m_norm2_g, w_ff1=m_w_ff1,
             w_ff2=m_w_ff2, final_norm_g=m_final_norm_g)
    v = dict(norm1_g=v_norm1_g, w_in=v_w_in, sgu_ln_g=v_sgu_ln_g, sgu_ln_b=v_sgu_ln_b, sgu_w=v_sgu_w, sgu_b=v_sgu_b,
             attn_out_g=v_attn_out_g, gmlp_out_g=v_gmlp_out_g, w_out=v_w_out, norm2_g=v_norm2_g, w_ff1=v_w_ff1,
             w_ff2=v_w_ff2, final_norm_g=v_final_norm_g)
    big = ("w_in", "w_out", "w_ff1", "w_ff2")
    core = lax.axis_index("c")

    w_in_t, = _all_gather([w_in[0].T.astype(BF16)], "w_in_all_gather")
    rest = (w_out[0].astype(BF16), w_ff1[0].T.astype(BF16), w_ff2[0].astype(BF16))
    loss, dx, parts, small_grads = _local_step(x[0], loss_target[0], {n: w[n] for n in SMALL_NAMES},
                                               w_in_t.reshape(IN_W, D_MODEL), rest, core=core)

    new = {}
    for n, p, transposed, tr in zip(big, parts, (True, False, True, False), (128, 128, 128, 256)):
        new[n] = [a[None] for a in _adamw(w[n][0], m[n][0], v[n][0], p, "adamw_" + n, tr, transposed)]

    flat = lambda a: a.reshape(-1, LANES)
    tiny_parts, sgu_parts = _all_gather(
        [_pack_tiny(small_grads, jnp.full((8, LANES), loss, F32)), flat(small_grads["sgu_w"])], "small_grad_all_gather")
    pad = jnp.ones((8, LANES), F32)
    tiny = _adamw(_pack_tiny(w, pad), _pack_tiny(m, pad), _pack_tiny(v, pad), tiny_parts, "adamw_tiny", TINY_ROWS)
    sgu = _adamw(flat(sgu_w), flat(m_sgu_w), flat(v_sgu_w), sgu_parts, "adamw_sgu_w", 512)
    loss = tiny[0][TINY_ROWS - 8, 0]

    outs = []
    for i in range(4):
        d = {n: new[n][i] for n in big}
        d.update(_unpack_tiny(tiny[i], w))
        d["sgu_w"] = sgu[i].reshape(sgu_w.shape)
        outs.extend(d[n] for n in WEIGHT_ORDER)
    return (loss, dx[None], *outs)
```

```python
import functools
import math

import numpy as np
import jax
import jax.numpy as jnp
from jax import lax
from jax.experimental import pallas as pl
from jax.experimental.pallas import tpu as pltpu

F32 = jnp.float32
BF16 = jnp.bfloat16

D_MODEL = 1024
HEAD_DIM = 64
N_HEADS = 12
ATTN_W = N_HEADS * HEAD_DIM
N_GROUPS = 4
GMLP_W = N_GROUPS * HEAD_DIM
IN_W = 3 * ATTN_W + 2 * GMLP_W
D_FF = 4 * D_MODEL
CHUNK = 128
DILATIONS = (1, 4, 16)
EPS = 1e-6
Q_SCALE = HEAD_DIM ** -0.5
NEG = -1e30

ADAM_LR, ADAM_B1, ADAM_B2, ADAM_EPS, ADAM_WD, ADAM_STEP = 0.001, 0.9, 0.999, 1e-08, 0.01, 10

N_DEV = 8
LANES = 128
VMEM_LIMIT = 56 << 20

TM_PROJ = 512
TM_FFN = 512
FF_CHUNK = 512
TM_GMLP = 1024
DW_TILE = (512, 1024, 4096)
DW_TILE_IN = (IN_W // 2, 1024, 2048)

MESH = pl.DeviceIdType.MESH


def _alibi_slopes(n):
    def pow2(m):
        start = 2.0 ** (-8.0 / m)
        return [start ** (i + 1) for i in range(m)]
    c = 2 ** int(math.floor(math.log2(n)))
    s = pow2(n) if c == n else pow2(c) + pow2(2 * c)[0::2][: n - c]
    return np.asarray(s, dtype=np.float32)


SLOPES = _alibi_slopes(N_HEADS)


def _params(sem=None):
    kw = dict(vmem_limit_bytes=VMEM_LIMIT)
    if sem is not None:
        kw["dimension_semantics"] = sem
    return pltpu.CompilerParams(**kw)


def _rows(tm, n):
    return pl.BlockSpec((tm, n), lambda i: (i, 0))


def _resident(shape):
    return pl.BlockSpec(shape, lambda *_: (0,) * len(shape), pipeline_mode=pl.Buffered(1))


def _rms(x):
    r = lax.rsqrt(jnp.mean(x * x, axis=-1, keepdims=True) + EPS)
    return x * r, r


def _rms_bwd(n, r, g, dy):
    dn = dy * g
    return r * (dn - n * jnp.mean(dn * n, axis=-1, keepdims=True))


def _accum_rows(acc_ref, v):
    acc_ref[...] += jnp.broadcast_to(jnp.sum(v, axis=0, keepdims=True), acc_ref.shape)


_G0 = math.sqrt(2.0 / math.pi)
_G1 = 0.044715


def _gelu(x):
    t = jnp.tanh(_G0 * (x + _G1 * (x * x * x)))
    return x * (0.5 * (1.0 + t)), t


def _gelu_grad(x, t):
    return 0.5 * (1.0 + t) + 0.5 * x * (1.0 - t * t) * (_G0 * (1.0 + 3.0 * _G1 * x * x))


NT = (((1,), (1,)), ((), ()))
TN = (((0,), (0,)), ((), ()))


def _dot(a, b, dims=None):
    if dims is None:
        return jnp.dot(a, b, preferred_element_type=F32)
    return lax.dot_general(a, b, dims, preferred_element_type=F32)


def _proj_fwd(x, g1, w_in_t):
    T = x.shape[0]
    tm = TM_PROJ

    def body(x_ref, g_ref, w_ref, hn_ref, q_ref, k_ref, v_ref, u_ref, z_ref):
        n, _ = _rms(x_ref[...])
        hn = (n * g_ref[...]).astype(BF16)
        hn_ref[...] = hn
        a = ATTN_W
        q_ref[...] = _dot(hn, w_ref[0:a, :], NT) * Q_SCALE
        k_ref[...] = _dot(hn, w_ref[a:2 * a, :], NT)
        v_ref[...] = _dot(hn, w_ref[2 * a:3 * a, :], NT)
        u_ref[...] = _dot(hn, w_ref[3 * a:3 * a + GMLP_W, :], NT)
        z_ref[...] = _dot(hn, w_ref[3 * a + GMLP_W:, :], NT)

    sds = jax.ShapeDtypeStruct
    return pl.pallas_call(
        body, name="proj_fwd", grid=(T // tm,),
        in_specs=[_rows(tm, D_MODEL), _resident((1, D_MODEL)), _resident((IN_W, D_MODEL))],
        out_specs=[_rows(tm, D_MODEL), _rows(tm, ATTN_W), _rows(tm, ATTN_W), _rows(tm, ATTN_W),
                   _rows(tm, GMLP_W), _rows(tm, GMLP_W)],
        out_shape=[sds((T, D_MODEL), BF16), sds((T, ATTN_W), F32), sds((T, ATTN_W), F32),
                   sds((T, ATTN_W), F32), sds((T, GMLP_W), F32), sds((T, GMLP_W), F32)],
        compiler_params=_params(("parallel",)),
    )(x, g1, w_in_t)


ATT_TILE = 2048
ATT_BLOCKS = ATT_TILE // CHUNK
SM_BLOCKS = 4


def _slope_table():
    row = np.repeat(SLOPES, HEAD_DIM)
    return jnp.asarray(np.broadcast_to(row[None], (8, ATTN_W)), F32)


def _residue_view(a):
    return a.reshape(a.shape[0] // ATT_BLOCKS, ATT_BLOCKS, a.shape[1])


def _tile_copies(hbm, buf, sem, hp, t, to_hbm=False, lane0=0):
    rows = pl.ds(pl.multiple_of(t * CHUNK, CHUNK), CHUNK)
    lanes = pl.ds(pl.multiple_of(lane0 + hp * LANES, LANES), LANES)
    pairs = [(hbm.at[rows, r, lanes], buf.at[r]) for r in range(ATT_BLOCKS)]
    return [pltpu.make_async_copy(v, h, sem) if to_hbm else pltpu.make_async_copy(h, v, sem) for h, v in pairs]


def _wait_tile(buf, sem):
    pltpu.make_async_copy(buf, buf, sem).wait()


def _residue_rows(d, j):
    if d == 16:
        return [(j, 0, CHUNK)]
    if d == 4:
        return [(j % 4 + 4 * m, 32 * (j // 4), 32) for m in range(4)]
    return [(r, 8 * j, 8) for r in range(ATT_BLOCKS)]


def _block_order(p, d):
    if d == 16:
        return p
    if d == 4:
        return 4 * (p & 31) + (p >> 5)
    return 16 * (p & 7) + (p >> 3)


def _first_in_tile(d, j):
    return _residue_rows(d, j)[0][1] == 0


def _rm_block(buf, d, j):
    return jnp.concatenate([buf[r, lo:lo + n, :] for r, lo, n in _residue_rows(d, j)], axis=0)


def _rm_block_before(buf, buf_before, d, j):
    if _first_in_tile(d, j):
        return jnp.concatenate([buf_before[r, CHUNK - n:CHUNK, :] for r, _, n in _residue_rows(d, j)], axis=0)
    return jnp.concatenate([buf[r, lo - n:lo, :] for r, lo, n in _residue_rows(d, j)], axis=0)


def _rm_store(buf, d, j, val):
    at = 0
    for r, lo, n in _residue_rows(d, j):
        buf[r, lo:lo + n, :] = val[at:at + n, :]
        at += n


def _rm_add(buf, rows, val):
    at = 0
    for r, lo, n in rows:
        buf[r, lo:lo + n, :] += val[at:at + n, :]
        at += n


def _residue_bias(sl_ref, d):
    shape = (2 * CHUNK, 2 * CHUNK)
    row = lax.broadcasted_iota(jnp.int32, shape, 0)
    col = lax.broadcasted_iota(jnp.int32, shape, 1)
    steps = _block_order(row & (CHUNK - 1), d) + CHUNK - (_block_order(col & (CHUNK - 1), d) + (col & CHUNK))
    band = (steps >= 0) & (steps <= CHUNK)
    sl = sl_ref[0:1, :]
    upper = lax.broadcasted_iota(jnp.int32, (2 * CHUNK, 1), 0) < CHUNK
    slope2 = jnp.where(upper, sl[:, 0:1], sl[:, HEAD_DIM:HEAD_DIM + 1])
    return jnp.where(band, -(float(d) * slope2 * steps.astype(F32)), NEG)


def _stack_heads(xb, head0):
    zero = jnp.zeros_like(xb)
    return jnp.concatenate([jnp.where(head0, xb, zero), jnp.where(head0, zero, xb)], axis=0).astype(BF16)


def _unstack_heads(x2, head0):
    return jnp.where(head0, x2[:CHUNK, :], x2[CHUNK:, :])


def _attn_fwd(q, k, v, shards=()):
    T = q.shape[0]
    nt = T // ATT_TILE
    ns = len(shards)
    steps = (ATTN_W // LANES) * nt

    def body(sl_ref, q_hbm, k_hbm, v_hbm, *rest):
        x_refs, rest = rest[:ns], rest[ns:]
        attn_hbm, lse_hbm = rest[:2]
        g_refs, rest = rest[2:2 + ns], rest[2 + ns:]
        qbuf, kbuf, vbuf, obuf, lbuf = rest[:5]
        o_acc, l_acc = rest[5:8], rest[8:11]
        sem_q, sem_k, sem_v, sem_o, sem_l = rest[11:16]
        hp, t = pl.program_id(0), pl.program_id(1)
        step = hp * nt + t
        two, three = step % 2, step % 3
        before, after = (step + 2) % 3, (step + 1) % 3
        if ns:
            start, forward, finish = _gather_phases(x_refs, g_refs, *rest[16:])
            pl.when(step == 0)(start)
            pl.when(step == steps // 2)(forward)

        def fetch(hp_, t_, two_, three_):
            for cp in (_tile_copies(q_hbm, qbuf.at[two_], sem_q.at[two_], hp_, t_)
                       + _tile_copies(k_hbm, kbuf.at[three_], sem_k.at[three_], hp_, t_)
                       + _tile_copies(v_hbm, vbuf.at[three_], sem_v.at[three_], hp_, t_)):
                cp.start()

        @pl.when(step == 0)
        def _():
            kbuf[2] = jnp.zeros((ATT_BLOCKS, CHUNK, LANES), F32)
            vbuf[2] = jnp.zeros((ATT_BLOCKS, CHUNK, LANES), F32)
            fetch(0, 0, 0, 0)

        @pl.when(step + 1 < steps)
        def _():
            fetch((step + 1) // nt, (step + 1) % nt, 1 - two, after)

        _wait_tile(qbuf.at[two], sem_q.at[two])
        _wait_tile(kbuf.at[three], sem_k.at[three])
        _wait_tile(vbuf.at[three], sem_v.at[three])

        @pl.when(step >= 2)
        def _():
            _wait_tile(obuf.at[two], sem_o.at[two])
            _wait_tile(lbuf.at[two], sem_l.at[two])

        q_t, k_t, v_t = qbuf.at[two], kbuf.at[three], vbuf.at[three]
        k_b, v_b = kbuf.at[before], vbuf.at[before]
        head0 = lax.broadcasted_iota(jnp.int32, (CHUNK, LANES), 1) < HEAD_DIM
        no_key_before = jnp.where(lax.broadcasted_iota(jnp.int32, (2 * CHUNK, 2 * CHUNK), 1) < CHUNK, NEG, 0.0)
        for pi, d in enumerate(DILATIONS):
            bias = _residue_bias(sl_ref, d)

            def scores(j, d=d, bias=bias):
                kcat = jnp.concatenate([_rm_block_before(k_t, k_b, d, j), _rm_block(k_t, d, j)], axis=0).astype(BF16)
                vcat = jnp.concatenate([_rm_block_before(v_t, v_b, d, j), _rm_block(v_t, d, j)], axis=0).astype(BF16)
                s = _dot(_stack_heads(_rm_block(q_t, d, j), head0), kcat, NT) + bias
                if _first_in_tile(d, j):
                    s = s + jnp.where(t == 0, 1.0, 0.0) * no_key_before
                return s, vcat

            def output(j, p, vcat, scale, lse, d=d, pi=pi):
                _rm_store(o_acc[pi], d, j, _unstack_heads(_dot(p, vcat) * scale, head0))
                _rm_store(l_acc[pi], d, j, _unstack_heads(jnp.broadcast_to(lse, (2 * CHUNK, LANES)), head0))

            for j0 in range(0, ATT_BLOCKS, SM_BLOCKS):
                group = [scores(j) for j in range(j0, j0 + SM_BLOCKS)]
                s = jnp.concatenate([g[0] for g in group], axis=0)
                m = jnp.max(s, axis=-1, keepdims=True)
                p = jnp.exp(s - m)
                l = jnp.sum(p, axis=-1, keepdims=True)
                p, scale, lse = p.astype(BF16), 1.0 / l, m + jnp.log(l)
                for i, (_, vcat) in enumerate(group):
                    rows = slice(i * 2 * CHUNK, (i + 1) * 2 * CHUNK)
                    output(j0 + i, p[rows, :], vcat, scale[rows, :], lse[rows, :])

        for r in range(ATT_BLOCKS):
            a, b, c = l_acc[0][r], l_acc[1][r], l_acc[2][r]
            m = jnp.maximum(jnp.maximum(a, b), c)
            ea, eb, ec = jnp.exp(a - m), jnp.exp(b - m), jnp.exp(c - m)
            tot = ea + eb + ec
            obuf[two, r] = (ea * o_acc[0][r] + eb * o_acc[1][r] + ec * o_acc[2][r]) / tot
            lbuf[two, r] = m + jnp.log(tot)

        for cp in (_tile_copies(attn_hbm, obuf.at[two], sem_o.at[two], hp, t, to_hbm=True)
                   + _tile_copies(lse_hbm, lbuf.at[two], sem_l.at[two], hp, t, to_hbm=True)):
            cp.start()

        @pl.when(step == steps - 1)
        def _():
            for slot in (two, 1 - two)[:min(steps, 2)]:
                _wait_tile(obuf.at[slot], sem_o.at[slot])
                _wait_tile(lbuf.at[slot], sem_l.at[slot])

        if ns:
            pl.when(step == steps - 1)(finish)

    tile = lambda n: pltpu.VMEM((n, ATT_BLOCKS, CHUNK, LANES), F32)
    dma = lambda n: pltpu.SemaphoreType.DMA((n,))
    view = jax.ShapeDtypeStruct((T // ATT_BLOCKS, ATT_BLOCKS, ATTN_W), F32)
    outs = pl.pallas_call(
        body, name="attn_fwd", grid=(ATTN_W // LANES, nt),
        in_specs=[pl.BlockSpec((8, LANES), lambda c, t: (0, c))] + [_HBM] * (3 + ns),
        out_specs=[_HBM] * (2 + ns),
        out_shape=[view, view] + [_gathered_shape(s) for s in shards],
        scratch_shapes=[tile(2), tile(3), tile(3), tile(2), tile(2)] + [pltpu.VMEM((ATT_BLOCKS, CHUNK, LANES), F32)] * 6
        + [dma(2), dma(3), dma(3), dma(2), dma(2)] + (_gather_sems(ns) if ns else []),
        compiler_params=_params(("arbitrary", "arbitrary")),
    )(_slope_table(), _residue_view(q), _residue_view(k), _residue_view(v), *shards)
    return outs[0].reshape(T, ATTN_W), outs[1].reshape(T, ATTN_W), tuple(outs[2:])


def _group_mean(v, grp):
    out = jnp.zeros_like(v)
    for g in range(N_GROUPS):
        mk = grp == g
        s = jnp.sum(jnp.where(mk, v, 0.0), axis=-1, keepdims=True) * (1.0 / HEAD_DIM)
        out = jnp.where(mk, s, out)
    return out


def _gmlp_core(uu, zz, lg, lb, ws, sb_ref, grp):
    ug, tu = _gelu(uu)
    zg, tz = _gelu(zz)
    zc = zg - _group_mean(zg, grp)
    rstd = lax.rsqrt(_group_mean(zc * zc, grp) + EPS)
    xhat = zc * rstd
    zn16 = (xhat * lg + lb).astype(BF16)
    mixed = []
    for ci in range(uu.shape[0] // CHUNK):
        rows = slice(ci * CHUNK, (ci + 1) * CHUNK)
        m = jnp.zeros((CHUNK, GMLP_W), F32)
        for g in range(N_GROUPS):
            m = jnp.where(grp[:CHUNK] == g, _dot(ws[g], zn16[rows, :]) + sb_ref[:, g:g + 1], m)
        mixed.append(m)
    return ug, tu, tz, xhat, rstd, zn16, jnp.concatenate(mixed, axis=0)


def _causal_ws(w_ref):
    ti = lax.broadcasted_iota(jnp.int32, (CHUNK, CHUNK), 0)
    si = lax.broadcasted_iota(jnp.int32, (CHUNK, CHUNK), 1)
    causal = si <= ti
    return causal, [jnp.where(causal, w_ref[g], 0.0).astype(BF16) for g in range(N_GROUPS)]


def _gmlp_fwd(u, z, ln_g, ln_b, sgu_w, sgu_bt):
    T = u.shape[0]
    tg = TM_GMLP

    def body(u_ref, z_ref, g_ref, b_ref, w_ref, sb_ref, out_ref):
        grp = lax.broadcasted_iota(jnp.int32, (tg, GMLP_W), 1) // HEAD_DIM
        _, ws = _causal_ws(w_ref)
        ug, _, _, _, _, _, mixed = _gmlp_core(u_ref[...], z_ref[...], g_ref[...], b_ref[...], ws, sb_ref, grp)
        out_ref[...] = ug * mixed

    return pl.pallas_call(
        body, name="gmlp_fwd", grid=(T // tg,),
        in_specs=[_rows(tg, GMLP_W), _rows(tg, GMLP_W), _resident((1, GMLP_W)), _resident((1, GMLP_W)),
                  _resident((N_GROUPS, CHUNK, CHUNK)), _resident((CHUNK, N_GROUPS))],
        out_specs=_rows(tg, GMLP_W),
        out_shape=jax.ShapeDtypeStruct((T, GMLP_W), F32),
        compiler_params=_params(("parallel",)),
    )(u, z, ln_g, ln_b, sgu_w, sgu_bt)


def _out_fwd(attn, gm, ga, gg, w_out, x, g2):
    T = x.shape[0]
    tm = TM_PROJ

    def body(a_ref, m_ref, ga_ref, gg_ref, w_ref, x_ref, g2_ref, mix_ref, h1_ref, hn2_ref):
        an, _ = _rms(a_ref[...])
        gn, _ = _rms(m_ref[...])
        an = (an * ga_ref[...]).astype(BF16)
        gn = (gn * gg_ref[...]).astype(BF16)
        mix_ref[:, 0:ATTN_W] = an
        mix_ref[:, ATTN_W:] = gn
        h1 = x_ref[...] + _dot(an, w_ref[0:ATTN_W, :]) + _dot(gn, w_ref[ATTN_W:, :])
        h1_ref[...] = h1
        n2, _ = _rms(h1)
        hn2_ref[...] = (n2 * g2_ref[...]).astype(BF16)

    sds = jax.ShapeDtypeStruct
    return pl.pallas_call(
        body, name="out_fwd", grid=(T // tm,),
        in_specs=[_rows(tm, ATTN_W), _rows(tm, GMLP_W), _resident((1, ATTN_W)), _resident((1, GMLP_W)),
                  _resident((D_MODEL, D_MODEL)), _rows(tm, D_MODEL), _resident((1, D_MODEL))],
        out_specs=[_rows(tm, D_MODEL)] * 3,
        out_shape=[sds((T, D_MODEL), BF16), sds((T, D_MODEL), F32), sds((T, D_MODEL), BF16)],
        compiler_params=_params(("parallel",)),
    )(attn, gm, ga, gg, w_out, x, g2)


def _ffn_fwd(hn2, h1, w1t, w2, gf, tgt):
    T = h1.shape[0]
    tm = TM_FFN

    def body(hn_ref, h1_ref, w1_ref, w2_ref, gf_ref, t_ref, r_ref, dhf_ref, dhb_ref, loss_ref, dgf_ref):
        i = pl.program_id(0)

        @pl.when(i == 0)
        def _():
            loss_ref[...] = jnp.zeros_like(loss_ref)
            dgf_ref[...] = jnp.zeros_like(dgf_ref)

        hn = hn_ref[...]
        acc = h1_ref[...]
        for j in range(D_FF // FF_CHUNK):
            cols = slice(j * FF_CHUNK, (j + 1) * FF_CHUNK)
            r = jnp.maximum(_dot(hn, w1_ref[cols, :], NT), 0.0)
            r_ref[:, cols] = r.astype(BF16)
            act = jnp.square(r).astype(BF16)
            acc = acc + _dot(act, w2_ref[cols, :])
        n3, r3 = _rms(acc)
        gf_row = gf_ref[...]
        e = n3 * gf_row - t_ref[...]
        loss_ref[...] += 0.5 * jnp.sum(jnp.mean(e * e, axis=-1, keepdims=True))
        dy = e * (1.0 / D_MODEL)
        _accum_rows(dgf_ref, dy * n3)
        dh2 = _rms_bwd(n3, r3, gf_row, dy)
        dhf_ref[...] = dh2
        dhb_ref[...] = dh2.astype(BF16)

    sds = jax.ShapeDtypeStruct
    acc_spec = lambda n: pl.BlockSpec((8, n), lambda i: (0, 0))
    return pl.pallas_call(
        body, name="ffn_fwd", grid=(T // tm,),
        in_specs=[_rows(tm, D_MODEL), _rows(tm, D_MODEL), _resident((D_FF, D_MODEL)), _resident((D_FF, D_MODEL)),
                  _resident((1, D_MODEL)), _rows(tm, D_MODEL)],
        out_specs=[_rows(tm, D_FF), _rows(tm, D_MODEL), _rows(tm, D_MODEL), acc_spec(LANES), acc_spec(D_MODEL)],
        out_shape=[sds((T, D_FF), BF16), sds((T, D_MODEL), F32), sds((T, D_MODEL), BF16),
                   sds((8, LANES), F32), sds((8, D_MODEL), F32)],
        compiler_params=_params(("arbitrary",)),
    )(hn2, h1, w1t, w2, gf, tgt)


def _ffn_bwd(dh2b, dh2f, relu, h1, g2, w2, w1t):
    T = h1.shape[0]
    tm = TM_FFN

    def body(db_ref, df_ref, r_ref, h1_ref, g2_ref, w2_ref, w1t_ref, da_ref, d1f_ref, d1b_ref, dg_ref):
        @pl.when(pl.program_id(0) == 0)
        def _():
            dg_ref[...] = jnp.zeros_like(dg_ref)

        db = db_ref[...]
        acc = jnp.zeros((tm, D_MODEL), F32)
        for j in range(D_FF // FF_CHUNK):
            cols = slice(j * FF_CHUNK, (j + 1) * FF_CHUNK)
            da = (_dot(db, w2_ref[cols, :], NT) * (2.0 * r_ref[:, cols].astype(F32))).astype(BF16)
            da_ref[:, cols] = da
            acc = acc + _dot(da, w1t_ref[cols, :])
        n2, r2 = _rms(h1_ref[...])
        _accum_rows(dg_ref, acc * n2)
        dh1 = df_ref[...] + _rms_bwd(n2, r2, g2_ref[...], acc)
        d1f_ref[...] = dh1
        d1b_ref[...] = dh1.astype(BF16)

    sds = jax.ShapeDtypeStruct
    return pl.pallas_call(
        body, name="ffn_bwd", grid=(T // tm,),
        in_specs=[_rows(tm, D_MODEL), _rows(tm, D_MODEL), _rows(tm, D_FF), _rows(tm, D_MODEL),
                  _resident((1, D_MODEL)), _resident((D_FF, D_MODEL)), _resident((D_FF, D_MODEL))],
        out_specs=[_rows(tm, D_FF), _rows(tm, D_MODEL), _rows(tm, D_MODEL),
                   pl.BlockSpec((8, D_MODEL), lambda i: (0, 0))],
        out_shape=[sds((T, D_FF), BF16), sds((T, D_MODEL), F32), sds((T, D_MODEL), BF16), sds((8, D_MODEL), F32)],
        compiler_params=_params(("arbitrary",)),
    )(dh2b, dh2f, relu, h1, g2, w2, w1t)


def _out_bwd(dh1b, w_out, attn, gm, ga, gg):
    T = attn.shape[0]
    tm = TM_PROJ

    def body(d_ref, w_ref, a_ref, m_ref, ga_ref, gg_ref, da_ref, dm_ref, dga_ref, dgg_ref):
        @pl.when(pl.program_id(0) == 0)
        def _():
            dga_ref[...] = jnp.zeros_like(dga_ref)
            dgg_ref[...] = jnp.zeros_like(dgg_ref)

        d = d_ref[...]
        dan = _dot(d, w_ref[0:ATTN_W, :], NT)
        dgn = _dot(d, w_ref[ATTN_W:, :], NT)
        na, ra = _rms(a_ref[...])
        ng, rg = _rms(m_ref[...])
        _accum_rows(dga_ref, dan * na)
        _accum_rows(dgg_ref, dgn * ng)
        da_ref[...] = _rms_bwd(na, ra, ga_ref[...], dan)
        dm_ref[...] = _rms_bwd(ng, rg, gg_ref[...], dgn)

    sds = jax.ShapeDtypeStruct
    return pl.pallas_call(
        body, name="out_bwd", grid=(T // tm,),
        in_specs=[_rows(tm, D_MODEL), _resident((D_MODEL, D_MODEL)), _rows(tm, ATTN_W), _rows(tm, GMLP_W),
                  _resident((1, ATTN_W)), _resident((1, GMLP_W))],
        out_specs=[_rows(tm, ATTN_W), _rows(tm, GMLP_W), pl.BlockSpec((8, ATTN_W), lambda i: (0, 0)),
                   pl.BlockSpec((8, GMLP_W), lambda i: (0, 0))],
        out_shape=[sds((T, ATTN_W), F32), sds((T, GMLP_W), F32), sds((8, ATTN_W), F32), sds((8, GMLP_W), F32)],
        compiler_params=_params(("arbitrary",)),
    )(dh1b, w_out, attn, gm, ga, gg)


def _gmlp_bwd(u, z, dgm, ln_g, ln_b, sgu_w, sgu_bt):
    T = u.shape[0]
    tg = TM_GMLP
    nsteps = T // tg

    def body(u_ref, z_ref, d_ref, g_ref, b_ref, w_ref, sb_ref, duz_ref, dlg_ref, dlb_ref, dw_ref, dsb_ref):
        i = pl.program_id(0)

        @pl.when(i == 0)
        def _():
            for ref in (dlg_ref, dlb_ref, dw_ref, dsb_ref):
                ref[...] = jnp.zeros_like(ref)

        grp = lax.broadcasted_iota(jnp.int32, (tg, GMLP_W), 1) // HEAD_DIM
        lane = lax.broadcasted_iota(jnp.int32, (CHUNK, LANES), 1)
        causal, ws = _causal_ws(w_ref)
        lg = g_ref[...]
        uu, zz, dgm = u_ref[...], z_ref[...], d_ref[...]
        ug, tu, tz, xhat, rstd, zn16, mixed = _gmlp_core(uu, zz, lg, b_ref[...], ws, sb_ref, grp)
        dmx = dgm * ug
        duz_ref[:, 0:GMLP_W] = dgm * mixed * _gelu_grad(uu, tu)
        dmx16 = dmx.astype(BF16)
        dzn = []
        for ci in range(tg // CHUNK):
            rows = slice(ci * CHUNK, (ci + 1) * CHUNK)
            dmx_c, d = dmx16[rows, :], jnp.zeros((CHUNK, GMLP_W), F32)
            for g in range(N_GROUPS):
                mk = grp[:CHUNK] == g
                d = jnp.where(mk, _dot(ws[g], dmx_c, TN), d)
                dw_ref[g] += _dot(jnp.where(mk, dmx_c, jnp.zeros_like(dmx_c)), zn16[rows, :], NT)
            dzn.append(d)
        dzn = jnp.concatenate(dzn, axis=0)
        dsb = jnp.zeros((CHUNK, LANES), F32)
        for g in range(N_GROUPS):
            per_token = jnp.sum(jnp.where(grp == g, dmx, 0.0), axis=-1, keepdims=True)
            by_position = sum(per_token[ci * CHUNK:(ci + 1) * CHUNK] for ci in range(tg // CHUNK))
            dsb = jnp.where(lane == g, by_position, dsb)
        dsb_ref[...] += dsb
        _accum_rows(dlg_ref, dzn * xhat)
        _accum_rows(dlb_ref, dzn)
        dxh = dzn * lg
        dzg = rstd * (dxh - _group_mean(dxh, grp) - xhat * _group_mean(dxh * xhat, grp))
        duz_ref[:, GMLP_W:] = dzg * _gelu_grad(zz, tz)

        @pl.when(i == nsteps - 1)
        def _():
            for g in range(N_GROUPS):
                dw_ref[g] = jnp.where(causal, dw_ref[g], 0.0)

    sds = jax.ShapeDtypeStruct
    return pl.pallas_call(
        body, name="gmlp_bwd", grid=(nsteps,),
        in_specs=[_rows(tg, GMLP_W)] * 3 + [_resident((1, GMLP_W)), _resident((1, GMLP_W)),
                                              _resident((N_GROUPS, CHUNK, CHUNK)), _resident((CHUNK, N_GROUPS))],
        out_specs=[_rows(tg, 2 * GMLP_W), pl.BlockSpec((8, GMLP_W), lambda i: (0, 0)),
                   pl.BlockSpec((8, GMLP_W), lambda i: (0, 0)),
                   pl.BlockSpec((N_GROUPS, CHUNK, CHUNK), lambda i: (0, 0, 0)),
                   pl.BlockSpec((CHUNK, LANES), lambda i: (0, 0))],
        out_shape=[sds((T, 2 * GMLP_W), F32), sds((8, GMLP_W), F32), sds((8, GMLP_W), F32),
                   sds((N_GROUPS, CHUNK, CHUNK), F32), sds((CHUNK, LANES), F32)],
        compiler_params=_params(("arbitrary",)),
    )(u, z, dgm, ln_g, ln_b, sgu_w, sgu_bt)


def _attn_bwd(q, k, v, dattn, attn, lse, duz, owner_grads=()):
    T = q.shape[0]
    nt = T // ATT_TILE
    ns = len(owner_grads)
    steps = (ATTN_W // LANES) * nt

    def body(sl_ref, q_hbm, k_hbm, v_hbm, do_hbm, o_hbm, lse_hbm, duz_hbm, *rest):
        p_refs, rest = rest[:ns], rest[ns:]
        dq_hbm = dk_hbm = dv_hbm = rest[0]
        r_refs, rest = rest[1:1 + ns], rest[1 + ns:]
        qbuf, dobuf, obuf, lbuf, kbuf, vbuf, dqbuf, dkbuf, dvbuf, delta_s = rest[:10]
        sem_q, sem_do, sem_o, sem_l, sem_k, sem_v, sem_dq, sem_dk, sem_dv, sem_uz = rest[10:20]
        hp, t = pl.program_id(0), pl.program_id(1)
        step = hp * nt + t
        two, three = step % 2, step % 3
        before, after = (step + 2) % 3, (step + 1) % 3
        if ns:
            start, finish = _owner_exchange_phases(p_refs, r_refs, *rest[20:])
            pl.when(step == 0)(start)

        def fetch(hp_, t_, two_, three_):
            for hbm, buf, sem, slot in ((q_hbm, qbuf, sem_q, two_), (do_hbm, dobuf, sem_do, two_),
                                        (o_hbm, obuf, sem_o, two_), (lse_hbm, lbuf, sem_l, two_),
                                        (k_hbm, kbuf, sem_k, three_), (v_hbm, vbuf, sem_v, three_)):
                for cp in _tile_copies(hbm, buf.at[slot], sem.at[slot], hp_, t_):
                    cp.start()

        @pl.when(step == 0)
        def _():
            kbuf[2] = jnp.zeros((ATT_BLOCKS, CHUNK, LANES), F32)
            vbuf[2] = jnp.zeros((ATT_BLOCKS, CHUNK, LANES), F32)
            dkbuf[3] = jnp.zeros((ATT_BLOCKS, CHUNK, LANES), F32)
            dvbuf[3] = jnp.zeros((ATT_BLOCKS, CHUNK, LANES), F32)
            fetch(0, 0, 0, 0)

        @pl.when(step + 1 < steps)
        def _():
            fetch((step + 1) // nt, (step + 1) % nt, 1 - two, after)

        uz_rows = pl.ds(pl.multiple_of(t * CHUNK + hp * (CHUNK // 4), CHUNK // 4), CHUNK // 4)
        uz_lanes = pl.ds(3 * ATTN_W, 2 * GMLP_W)
        uz_copy = pltpu.make_async_copy(duz_hbm.at[uz_rows], dq_hbm.at[uz_rows, :, uz_lanes], sem_uz)

        @pl.when(hp < 4)
        def _():
            for r in range(ATT_BLOCKS):
                pltpu.make_async_copy(duz_hbm.at[uz_rows, r], dq_hbm.at[uz_rows, r, uz_lanes], sem_uz).start()

        for buf, sem in ((qbuf, sem_q), (dobuf, sem_do), (obuf, sem_o), (lbuf, sem_l)):
            _wait_tile(buf.at[two], sem.at[two])
        _wait_tile(kbuf.at[three], sem_k.at[three])
        _wait_tile(vbuf.at[three], sem_v.at[three])

        @pl.when(step >= 2)
        def _():
            _wait_tile(dqbuf.at[two], sem_dq.at[two])

        @pl.when(step >= 3)
        def _():
            _wait_tile(dkbuf.at[three], sem_dk.at[three])
            _wait_tile(dvbuf.at[three], sem_dv.at[three])

        zero_tile = jnp.zeros((ATT_BLOCKS, CHUNK, LANES), F32)
        dqbuf[two] = zero_tile
        dkbuf[three] = zero_tile
        dvbuf[three] = zero_tile

        q_t, do_t, l_t, k_t, v_t = qbuf.at[two], dobuf.at[two], lbuf.at[two], kbuf.at[three], vbuf.at[three]
        k_b, v_b = kbuf.at[before], vbuf.at[before]
        dq_t, dk_t, dv_t = dqbuf.at[two], dkbuf.at[three], dvbuf.at[three]
        dk_b, dv_b = dkbuf.at[before], dvbuf.at[before]
        sink = jnp.where(t > 0, before, 3)
        dk_sink, dv_sink = dkbuf.at[sink], dvbuf.at[sink]
        head0 = lax.broadcasted_iota(jnp.int32, (CHUNK, LANES), 1) < HEAD_DIM
        for r in range(ATT_BLOCKS):
            dd = dobuf[two, r] * obuf[two, r]
            d0 = jnp.sum(jnp.where(head0, dd, 0.0), axis=-1, keepdims=True)
            d1 = jnp.sum(jnp.where(head0, 0.0, dd), axis=-1, keepdims=True)
            delta_s[r] = jnp.where(head0, d0, d1)

        def column(xb):
            return jnp.concatenate([xb[:, 0:1], xb[:, HEAD_DIM:HEAD_DIM + 1]], axis=0)

        no_key_before = jnp.where(lax.broadcasted_iota(jnp.int32, (2 * CHUNK, 2 * CHUNK), 1) < CHUNK, NEG, 0.0)
        for d in DILATIONS:
            bias = _residue_bias(sl_ref, d)
            for j in range(ATT_BLOCKS):
                kcat = jnp.concatenate([_rm_block_before(k_t, k_b, d, j), _rm_block(k_t, d, j)], axis=0).astype(BF16)
                vcat = jnp.concatenate([_rm_block_before(v_t, v_b, d, j), _rm_block(v_t, d, j)], axis=0).astype(BF16)
                q2 = _stack_heads(_rm_block(q_t, d, j), head0)
                do2 = _stack_heads(_rm_block(do_t, d, j), head0)
                s = _dot(q2, kcat, NT) + bias
                if _first_in_tile(d, j):
                    s = s + jnp.where(t == 0, 1.0, 0.0) * no_key_before
                p = jnp.exp(s - column(_rm_block(l_t, d, j)))
                ds = (p * (_dot(do2, vcat, NT) - column(_rm_block(delta_s, d, j)))).astype(BF16)
                _rm_add(dq_t, _residue_rows(d, j), _unstack_heads(_dot(ds, kcat), head0))
                ck = _dot(ds, q2, TN)
                cv = _dot(p.astype(BF16), do2, TN)
                _rm_add(dk_t, _residue_rows(d, j), ck[CHUNK:, :])
                _rm_add(dv_t, _residue_rows(d, j), cv[CHUNK:, :])
                if _first_in_tile(d, j):
                    rows = [(r, CHUNK - n, n) for r, _, n in _residue_rows(d, j)]
                    _rm_add(dk_sink, rows, ck[:CHUNK, :])
                    _rm_add(dv_sink, rows, cv[:CHUNK, :])
                else:
                    rows = [(r, lo - n, n) for r, lo, n in _residue_rows(d, j)]
                    _rm_add(dk_t, rows, ck[:CHUNK, :])
                    _rm_add(dv_t, rows, cv[:CHUNK, :])

        for r in range(ATT_BLOCKS):
            dqbuf[two, r] = dqbuf[two, r] * Q_SCALE
        for cp in _tile_copies(dq_hbm, dq_t, sem_dq.at[two], hp, t, to_hbm=True):
            cp.start()
        pl.when(hp < 4)(uz_copy.wait)

        @pl.when(t > 0)
        def _():
            for cp in (_tile_copies(dk_hbm, dk_b, sem_dk.at[before], hp, t - 1, to_hbm=True, lane0=ATTN_W)
                       + _tile_copies(dv_hbm, dv_b, sem_dv.at[before], hp, t - 1, to_hbm=True, lane0=2 * ATTN_W)):
                cp.start()

        @pl.when(t == nt - 1)
        def _():
            for cp in (_tile_copies(dk_hbm, dk_t, sem_dk.at[three], hp, t, to_hbm=True, lane0=ATTN_W)
                       + _tile_copies(dv_hbm, dv_t, sem_dv.at[three], hp, t, to_hbm=True, lane0=2 * ATTN_W)):
                cp.start()

        @pl.when(step == steps - 1)
        def _():
            for slot in range(2):
                _wait_tile(dqbuf.at[slot], sem_dq.at[slot])
            for slot in range(3):
                _wait_tile(dkbuf.at[slot], sem_dk.at[slot])
                _wait_tile(dvbuf.at[slot], sem_dv.at[slot])

        if ns:
            pl.when(step == steps - 1)(finish)

    tile = lambda n: pltpu.VMEM((n, ATT_BLOCKS, CHUNK, LANES), F32)
    dma = lambda n: pltpu.SemaphoreType.DMA((n,))
    view = jax.ShapeDtypeStruct((T // ATT_BLOCKS, ATT_BLOCKS, ATTN_W), F32)
    outs = pl.pallas_call(
        body, name="attn_bwd", grid=(ATTN_W // LANES, nt),
        in_specs=[pl.BlockSpec((8, LANES), lambda c, t: (0, c))] + [_HBM] * (7 + ns),
        out_specs=[_HBM] * (1 + ns),
        out_shape=[jax.ShapeDtypeStruct((T // ATT_BLOCKS, ATT_BLOCKS, IN_W), F32)]
        + [jax.ShapeDtypeStruct(p.shape, p.dtype) for p in owner_grads],
        scratch_shapes=[tile(2), tile(2), tile(2), tile(2), tile(3), tile(3), tile(2), tile(4), tile(4),
                        pltpu.VMEM((ATT_BLOCKS, CHUNK, LANES), F32)]
        + [dma(2), dma(2), dma(2), dma(2), dma(3), dma(3), dma(2), dma(3), dma(3), pltpu.SemaphoreType.DMA]
        + (_owner_exchange_sems(ns) if ns else []),
        compiler_params=_params(("arbitrary", "arbitrary")),
    )(_slope_table(), *[_residue_view(a) for a in (q, k, v, dattn, attn, lse, duz)], *owner_grads)
    return outs[0].reshape(T, IN_W), tuple(outs[1:])


def _proj_bwd(dproj, w_in_t, x, g1, dh1, chip_sums=()):
    T = x.shape[0]
    tm = TM_PROJ
    ns = len(chip_sums)
    steps = T // tm

    def body(d_ref, w_ref, x_ref, g_ref, r_ref, *rest):
        p_refs, rest = rest[:ns], rest[ns:]
        dx_ref, dg_ref = rest[:2]
        r_refs, sems = rest[2:2 + ns], rest[2 + ns:]
        step = pl.program_id(0)
        if ns:
            start, finish = _chip_exchange_phases(p_refs, r_refs, *sems)
            pl.when(step == 0)(start)

        @pl.when(step == 0)
        def _():
            dg_ref[...] = jnp.zeros_like(dg_ref)

        dhn = _dot(d_ref[...].astype(BF16), w_ref[...])
        n1, r1 = _rms(x_ref[...])
        _accum_rows(dg_ref, dhn * n1)
        dx_ref[...] = r_ref[...] + _rms_bwd(n1, r1, g_ref[...], dhn)
        if ns:
            pl.when(step == steps - 1)(finish)

    outs = pl.pallas_call(
        body, name="proj_bwd", grid=(steps,),
        in_specs=[_rows(tm, IN_W), _resident((IN_W, D_MODEL)), _rows(tm, D_MODEL), _resident((1, D_MODEL)),
                  _rows(tm, D_MODEL)] + [_HBM] * ns,
        out_specs=[_rows(tm, D_MODEL), pl.BlockSpec((8, D_MODEL), lambda i: (0, 0))] + [_HBM] * ns,
        out_shape=[jax.ShapeDtypeStruct((T, D_MODEL), F32), jax.ShapeDtypeStruct((8, D_MODEL), F32)]
        + [jax.ShapeDtypeStruct(p.shape, p.dtype) for p in chip_sums],
        scratch_shapes=_chip_exchange_sems(ns) if ns else [],
        compiler_params=_params(("arbitrary",)),
    )(dproj, w_in_t, x, g1, dh1, *chip_sums)
    return outs[0], outs[1], tuple(outs[2:])


def _dw(a, b, name, tile, square_a=False, out_dtype=F32):
    T, ka = a.shape
    nb = b.shape[1]
    tka, tnb, tt = tile
    tt = min(tt, T)
    last = T // tt - 1

    def body(a_ref, b_ref, *refs):
        o_ref = refs[0]
        acc_ref = refs[1] if len(refs) > 1 else o_ref
        s = pl.program_id(2)

        @pl.when(s == 0)
        def _():
            acc_ref[...] = jnp.zeros_like(acc_ref)

        a_tile = a_ref[...]
        if square_a:
            a_tile = jnp.square(a_tile.astype(F32))
        acc_ref[...] += _dot(a_tile.astype(BF16), b_ref[...], TN)
        if acc_ref is not o_ref:
            @pl.when(s == last)
            def _():
                o_ref[...] = acc_ref[...].astype(out_dtype)

    return pl.pallas_call(
        body, name=name, grid=(ka // tka, nb // tnb, T // tt),
        in_specs=[pl.BlockSpec((tt, tka), lambda i, j, s: (s, i)), pl.BlockSpec((tt, tnb), lambda i, j, s: (s, j))],
        out_specs=pl.BlockSpec((tka, tnb), lambda i, j, s: (i, j)),
        out_shape=jax.ShapeDtypeStruct((ka, nb), out_dtype),
        scratch_shapes=[] if out_dtype == F32 else [pltpu.VMEM((tka, tnb), F32)],
        compiler_params=_params(("parallel", "parallel", "arbitrary")),
    )(a, b)


def _adamw(w, m, v, parts, name, tr, transposed=False):
    R, C = w.shape
    P = parts.shape[0]

    def body(w_ref, m_ref, v_ref, p_ref, g_ref, d_ref, m2_ref, v2_ref):
        g = p_ref[0].astype(F32)
        for i in range(1, P):
            g = g + p_ref[i].astype(F32)
        if transposed:
            g = g.T
        m2 = ADAM_B1 * m_ref[...] + (1.0 - ADAM_B1) * g
        v2 = ADAM_B2 * v_ref[...] + (1.0 - ADAM_B2) * jnp.square(g)
        m_hat = m2 / (1.0 - ADAM_B1 ** ADAM_STEP)
        v_hat = v2 / (1.0 - ADAM_B2 ** ADAM_STEP)
        g_ref[...] = g
        d_ref[...] = -ADAM_LR * (m_hat / (jnp.sqrt(v_hat) + ADAM_EPS) + ADAM_WD * w_ref[...])
        m2_ref[...] = m2
        v2_ref[...] = v2

    spec = _rows(tr, C)
    part_spec = (pl.BlockSpec((P, C, tr), lambda i: (0, 0, i)) if transposed
                 else pl.BlockSpec((P, tr, C), lambda i: (0, i, 0)))
    return pl.pallas_call(
        body, name=name, grid=(R // tr,),
        in_specs=[spec, spec, spec, part_spec],
        out_specs=[spec] * 4,
        out_shape=[jax.ShapeDtypeStruct((R, C), F32)] * 4,
        compiler_params=_params(("parallel",)),
    )(w, m, v, parts)


def _pair_sum(core, grad, recv, name):
    _, _, n, C = grad.shape
    tr = n // 2

    def body(c_ref, a_ref, b_ref, o_ref):
        o_ref[...] = a_ref[...] + b_ref[...]

    spec = pl.BlockSpec((1, tr, C), lambda i, j, c_ref: (i, j, 0))
    return pl.pallas_call(
        body, name=name,
        grid_spec=pltpu.PrefetchScalarGridSpec(
            num_scalar_prefetch=1, grid=(4, n // tr),
            in_specs=[pl.BlockSpec((1, None, tr, C), lambda i, j, c_ref: (i, c_ref[0], j, 0)), spec],
            out_specs=spec),
        out_shape=jax.ShapeDtypeStruct(recv.shape, F32),
        compiler_params=_params(("parallel", "parallel")),
    )(core.reshape(1), grad, recv)


_HBM = pl.BlockSpec(memory_space=pltpu.HBM)


def _place():
    return lax.axis_index("x"), lax.axis_index("y"), lax.axis_index("c")


def _gathered_shape(shard):
    return jax.ShapeDtypeStruct((N_DEV,) + shard.shape, shard.dtype)


def _gather_sems(n):
    return [pltpu.SemaphoreType.DMA((7, n)), pltpu.SemaphoreType.DMA((7, n)), pltpu.SemaphoreType.DMA((n,))]


def _gather_phases(x_refs, out_refs, send_sems, recv_sems, local_sems):
    x, y, c = _place()
    me, sibling = (x, y, c), (x, y, 1 - c)
    chips = [(1 - x, y), (x, 1 - y), (1 - x, 1 - y)]
    arrays = range(len(x_refs))

    def slot(i, px, py, pc):
        return out_refs[i].at[4 * px + 2 * py + pc]

    def copy(i, k, block, to, own=False):
        return pltpu.make_async_remote_copy(
            src_ref=x_refs[i] if own else slot(i, *block), dst_ref=slot(i, *block),
            send_sem=send_sems.at[k, i], recv_sem=recv_sems.at[k, i], device_id=to, device_id_type=MESH)

    def mine(i):
        return pltpu.make_async_copy(x_refs[i], slot(i, *me), local_sems.at[i])

    def start():
        for i in arrays:
            mine(i).start()
            copy(i, 0, me, sibling, own=True).start()
            for j, chip in enumerate(chips):
                copy(i, 1 + j, me, (*chip, c), own=True).start()

    def forward():
        for i in arrays:
            for j, chip in enumerate(chips):
                copy(i, 1 + j, (*chip, c), me).wait_recv()
                copy(i, 4 + j, (*chip, c), sibling).start()

    def finish():
        for i in arrays:
            copy(i, 0, sibling, me).wait_recv()
            copy(i, 0, me, sibling, own=True).wait_send()
            for j, chip in enumerate(chips):
                copy(i, 4 + j, (*chip, 1 - c), me).wait_recv()
                copy(i, 1 + j, me, (*chip, c), own=True).wait_send()
                copy(i, 4 + j, (*chip, c), sibling).wait_send()
            mine(i).wait()

    return start, forward, finish


def _all_gather(shards, name):
    n = len(shards)

    def body(*refs):
        start, forward, finish = _gather_phases(refs[:n], refs[n:2 * n], *refs[2 * n:])
        start()
        forward()
        finish()

    return pl.pallas_call(
        body, name=name,
        out_shape=[_gathered_shape(s) for s in shards],
        in_specs=[_HBM] * n, out_specs=[_HBM] * n,
        scratch_shapes=_gather_sems(n),
    )(*shards)


def _sibling_exchange(grads, name):
    n = len(grads)

    def body(*refs):
        g_refs, r_refs, send_sems, recv_sems = refs[:n], refs[n:2 * n], refs[2 * n], refs[2 * n + 1]
        x, y, c = _place()
        copies = [pltpu.make_async_remote_copy(
            src_ref=g_refs[i].at[:, 1 - c], dst_ref=r_refs[i], send_sem=send_sems.at[i], recv_sem=recv_sems.at[i],
            device_id=(x, y, 1 - c), device_id_type=MESH) for i in range(n)]
        for cp in copies:
            cp.start()
        for cp in copies:
            cp.wait()

    return pl.pallas_call(
        body, name=name,
        out_shape=[jax.ShapeDtypeStruct((g.shape[0],) + g.shape[2:], g.dtype) for g in grads],
        in_specs=[_HBM] * n, out_specs=[_HBM] * n,
        scratch_shapes=[pltpu.SemaphoreType.DMA((n,)), pltpu.SemaphoreType.DMA((n,))],
    )(*grads)


def _owner_exchange_sems(n):
    return [pltpu.SemaphoreType.DMA((7, n)), pltpu.SemaphoreType.DMA((7, n)), pltpu.SemaphoreType.DMA((n,))]


def _owner_exchange_phases(g_refs, r_refs, send_sems, recv_sems, local_sems):
    x, y, c = _place()
    me = 4 * x + 2 * y + c
    flip = lambda v, bit: 1 - v if bit else v
    peers = [(flip(x, k & 4), flip(y, k & 2), flip(c, k & 1)) for k in range(1, N_DEV)]
    arrays = range(len(g_refs))

    def mine(i):
        return pltpu.make_async_copy(g_refs[i].at[me], r_refs[i].at[me], local_sems.at[i])

    def copy(i, k, src_slot, dst_slot):
        return pltpu.make_async_remote_copy(
            src_ref=g_refs[i].at[src_slot], dst_ref=r_refs[i].at[dst_slot],
            send_sem=send_sems.at[k, i], recv_sem=recv_sems.at[k, i], device_id=peers[k], device_id_type=MESH)

    def start():
        for i in arrays:
            mine(i).start()
            for k, (px, py, pc) in enumerate(peers):
                copy(i, k, 4 * px + 2 * py + pc, me).start()

    def finish():
        for i in arrays:
            for k, (px, py, pc) in enumerate(peers):
                copy(i, k, me, 4 * px + 2 * py + pc).wait_recv()
                copy(i, k, 4 * px + 2 * py + pc, me).wait_send()
            mine(i).wait()

    return start, finish


def _chip_exchange_sems(n):
    return [pltpu.SemaphoreType.DMA((3, n)), pltpu.SemaphoreType.DMA((3, n)), pltpu.SemaphoreType.DMA((n,))]


def _chip_exchange_phases(p_refs, r_refs, send_sems, recv_sems, local_sems):
    x, y, c = _place()
    my_chip = 2 * x + y
    chips = [(1 - x, y), (x, 1 - y), (1 - x, 1 - y)]
    arrays = range(len(p_refs))

    def mine(i):
        return pltpu.make_async_copy(p_refs[i].at[my_chip], r_refs[i].at[my_chip], local_sems.at[i])

    def copy(i, k, src_chip, dst_chip):
        px, py = chips[k]
        return pltpu.make_async_remote_copy(
            src_ref=p_refs[i].at[src_chip], dst_ref=r_refs[i].at[dst_chip],
            send_sem=send_sems.at[k, i], recv_sem=recv_sems.at[k, i], device_id=(px, py, c), device_id_type=MESH)

    def start():
        for i in arrays:
            mine(i).start()
            for k, (px, py) in enumerate(chips):
                copy(i, k, 2 * px + py, my_chip).start()

    def finish():
        for i in arrays:
            for k, (px, py) in enumerate(chips):
                copy(i, k, my_chip, 2 * px + py).wait_recv()
                copy(i, k, 2 * px + py, my_chip).wait_send()
            mine(i).wait()

    return start, finish


_R_IN, _R_OUT, _R_FF = IN_W // N_DEV, D_MODEL // N_DEV, D_FF // N_DEV


def _by_owner(g):
    return g.reshape(4, 2, g.shape[0] // N_DEV, D_MODEL)


def _local_step(x, tgt, small, w_in_t, rest, core=None):
    exchange = core is not None
    g1, g2, gf = small["norm1_g"], small["norm2_g"], small["final_norm_g"].reshape(1, D_MODEL)
    ga, gg = small["attn_out_g"], small["gmlp_out_g"]
    ln_g = small["sgu_ln_g"].reshape(1, GMLP_W)
    ln_b = small["sgu_ln_b"].reshape(1, GMLP_W)
    sgu_w = small["sgu_w"][0]
    sgu_bt = small["sgu_b"][0].T

    hn1, q, k, v, u, z = _proj_fwd(x, g1, w_in_t)
    attn, lse, gathered = _attn_fwd(q, k, v, shards=rest if exchange else ())
    w_out, w_ff1_t, w_ff2 = [g.reshape(-1, D_MODEL) for g in gathered] if exchange else rest
    gm = _gmlp_fwd(u, z, ln_g, ln_b, sgu_w, sgu_bt)
    mixed, h1, hn2 = _out_fwd(attn, gm, ga, gg, w_out, x, g2)
    relu, dh2f, dh2b, loss8, dgf8 = _ffn_fwd(hn2, h1, w_ff1_t, w_ff2, gf, tgt)

    da, dh1f, dh1b, dg2 = _ffn_bwd(dh2b, dh2f, relu, h1, g2, w_ff2, w_ff1_t)
    wire = BF16 if exchange else F32
    dw_ff2 = _dw(relu, dh2b, "dw_ff2", DW_TILE, square_a=True, out_dtype=wire)
    dw_ff1_t = _dw(da, hn2, "dw_ff1", DW_TILE, out_dtype=wire)
    dattn, dgm, dga, dgg = _out_bwd(dh1b, w_out, attn, gm, ga, gg)
    dw_out = _dw(mixed, dh1b, "dw_out", DW_TILE, out_dtype=wire)
    early = [dw_out, dw_ff1_t, dw_ff2]
    if exchange:
        early = [g.reshape(N_DEV, -1, D_MODEL) for g in early]
    duz, dlg, dlb, dsw, dsb = _gmlp_bwd(u, z, dgm, ln_g, ln_b, sgu_w, sgu_bt)
    dproj, arrived = _attn_bwd(q, k, v, dattn, attn, lse, duz, owner_grads=early if exchange else ())
    dw_in_t = _dw(dproj, hn1, "dw_in", DW_TILE_IN)
    late = ()
    if exchange:
        by_owner = _by_owner(dw_in_t)
        got, = _sibling_exchange([by_owner], "grad_sibling_exchange")
        late = (_pair_sum(core, by_owner, got, "grad_pair_sum"),)
    dx, dg1, late = _proj_bwd(dproj, w_in_t, x, g1, dh1f, chip_sums=late)
    if exchange:
        dw_in_t, early = late[0], arrived

    small_grads = dict(
        norm1_g=dg1[0], sgu_ln_g=dlg[0], sgu_ln_b=dlb[0], sgu_w=dsw, sgu_b=dsb[:, :N_GROUPS].T,
        attn_out_g=dga[0], gmlp_out_g=dgg[0], norm2_g=dg2[0], final_norm_g=dgf8[0])
    return loss8[0, 0], dx, (dw_in_t, *early), small_grads


SMALL_NAMES = ("norm1_g", "sgu_ln_g", "sgu_ln_b", "sgu_w", "sgu_b", "attn_out_g", "gmlp_out_g", "norm2_g",
               "final_norm_g")
WEIGHT_ORDER = ("norm1_g", "w_in", "sgu_ln_g", "sgu_ln_b", "sgu_w", "sgu_b", "attn_out_g", "gmlp_out_g", "w_out",
                "norm2_g", "w_ff1", "w_ff2", "final_norm_g")


TINY_NAMES = tuple(n for n in SMALL_NAMES if n != "sgu_w")
TINY_ROWS = 48


def _pack_tiny(d, last):
    return jnp.concatenate([d[n].reshape(-1, LANES) for n in TINY_NAMES] + [last], axis=0)


def _unpack_tiny(p, like):
    out, r = {}, 0
    for n in TINY_NAMES:
        rows = like[n].size // LANES
        out[n] = p[r:r + rows].reshape(like[n].shape)
        r += rows
    return out


def kernel(x, norm1_g, w_in, sgu_ln_g, sgu_ln_b, sgu_w, sgu_b, attn_out_g, gmlp_out_g, w_out, norm2_g, w_ff1, w_ff2, final_norm_g, loss_target, m_norm1_g, m_w_in, m_sgu_ln_g, m_sgu_ln_b, m_sgu_w, m_sgu_b, m_attn_out_g, m_gmlp_out_g, m_w_out, m_norm2_g, m_w_ff1, m_w_ff2, m_final_norm_g, v_norm1_g, v_w_in, v_sgu_ln_g, v_sgu_ln_b, v_sgu_w, v_sgu_b, v_attn_out_g, v_gmlp_out_g, v_w_out, v_norm2_g, v_w_ff1, v_w_ff2, v_final_norm_g):
    w = dict(norm1_g=norm1_g, w_in=w_in, sgu_ln_g=sgu_ln_g, sgu_ln_b=sgu_ln_b, sgu_w=sgu_w, sgu_b=sgu_b,
             attn_out_g=attn_out_g, gmlp_out_g=gmlp_out_g, w_out=w_out, norm2_g=norm2_g, w_ff1=w_ff1, w_ff2=w_ff2,
             final_norm_g=final_norm_g)
    m = dict(norm1_g=m_norm1_g, w_in=m_w_in, sgu_ln_g=m_sgu_ln_g, sgu_ln_b=m_sgu_ln_b, sgu_w=m_sgu_w, sgu_b=m_sgu_b,
             attn_out_g=m_attn_out_g, gmlp_out_g=m_gmlp_out_g, w_out=m_w_out, norm2_g=m_norm2_g, w_ff1=m_w_ff1,
             w_ff2=m_w_ff2, final_norm_g=m_final_norm_g)
    v = dict(norm1_g=v_norm1_g, w_in=v_w_in, sgu_ln_g=v_sgu_ln_g, sgu_ln_b=v_sgu_ln_b, sgu_w=v_sgu_w, sgu_b=v_sgu_b,
             attn_out_g=v_attn_out_g, gmlp_out_g=v_gmlp_out_g, w_out=v_w_out, norm2_g=v_norm2_g, w_ff1=v_w_ff1,
             w_ff2=v_w_ff2, final_norm_g=v_final_norm_g)
    big = ("w_in", "w_out", "w_ff1", "w_ff2")
    core = lax.axis_index("c")

    w_in_t, = _all_gather([w_in[0].T.astype(BF16)], "w_in_all_gather")
    rest = (w_out[0].astype(BF16), w_ff1[0].T.astype(BF16), w_ff2[0].astype(BF16))
    loss, dx, parts, small_grads = _local_step(x[0], loss_target[0], {n: w[n] for n in SMALL_NAMES},
                                               w_in_t.reshape(IN_W, D_MODEL), rest, core=core)

    new = {}
    for n, p, transposed, tr in zip(big, parts, (True, False, True, False), (128, 128, 128, 256)):
        new[n] = [a[None] for a in _adamw(w[n][0], m[n][0], v[n][0], p, "adamw_" + n, tr, transposed)]

    flat = lambda a: a.reshape(-1, LANES)
    tiny_parts, sgu_parts = _all_gather(
        [_pack_tiny(small_grads, jnp.full((8, LANES), loss, F32)), flat(small_grads["sgu_w"])], "small_grad_all_gather")
    pad = jnp.ones((8, LANES), F32)
    tiny = _adamw(_pack_tiny(w, pad), _pack_tiny(m, pad), _pack_tiny(v, pad), tiny_parts, "adamw_tiny", TINY_ROWS)
    sgu = _adamw(flat(sgu_w), flat(m_sgu_w), flat(v_sgu_w), sgu_parts, "adamw_sgu_w", 512)
    loss = tiny[0][TINY_ROWS - 8, 0]

    outs = []
    for i in range(4):
        d = {n: new[n][i] for n in big}
        d.update(_unpack_tiny(tiny[i], w))
        d["sgu_w"] = sgu[i].reshape(sgu_w.shape)
        outs.extend(d[n] for n in WEIGHT_ORDER)
    return (loss, dx[None], *outs)
```

```python
import functools
import math

import numpy as np
import jax
import jax.numpy as jnp
from jax import lax
from jax.experimental import pallas as pl
from jax.experimental.pallas import tpu as pltpu

F32 = jnp.float32
BF16 = jnp.bfloat16

D_MODEL = 1024
HEAD_DIM = 64
N_HEADS = 12
ATTN_W = N_HEADS * HEAD_DIM
N_GROUPS = 4
GMLP_W = N_GROUPS * HEAD_DIM
IN_W = 3 * ATTN_W + 2 * GMLP_W
D_FF = 4 * D_MODEL
CHUNK = 128
DILATIONS = (1, 4, 16)
EPS = 1e-6
Q_SCALE = HEAD_DIM ** -0.5
NEG = -1e30

ADAM_LR, ADAM_B1, ADAM_B2, ADAM_EPS, ADAM_WD, ADAM_STEP = 0.001, 0.9, 0.999, 1e-08, 0.01, 10

N_DEV = 8
LANES = 128
VMEM_LIMIT = 56 << 20

TM_PROJ = 512
TM_FFN = 512
FF_CHUNK = 512
TM_GMLP = 1024
DW_TILE = (512, 1024, 4096)
DW_TILE_IN = (IN_W // 2, 1024, 2048)

MESH = pl.DeviceIdType.MESH


def _alibi_slopes(n):
    def pow2(m):
        start = 2.0 ** (-8.0 / m)
        return [start ** (i + 1) for i in range(m)]
    c = 2 ** int(math.floor(math.log2(n)))
    s = pow2(n) if c == n else pow2(c) + pow2(2 * c)[0::2][: n - c]
    return np.asarray(s, dtype=np.float32)


SLOPES = _alibi_slopes(N_HEADS)


def _params(sem=None):
    kw = dict(vmem_limit_bytes=VMEM_LIMIT)
    if sem is not None:
        kw["dimension_semantics"] = sem
    return pltpu.CompilerParams(**kw)


def _rows(tm, n):
    return pl.BlockSpec((tm, n), lambda i: (i, 0))


def _resident(shape):
    return pl.BlockSpec(shape, lambda *_: (0,) * len(shape), pipeline_mode=pl.Buffered(1))


def _rms(x):
    r = lax.rsqrt(jnp.mean(x * x, axis=-1, keepdims=True) + EPS)
    return x * r, r


def _rms_bwd(n, r, g, dy):
    dn = dy * g
    return r * (dn - n * jnp.mean(dn * n, axis=-1, keepdims=True))


def _accum_rows(acc_ref, v):
    acc_ref[...] += jnp.broadcast_to(jnp.sum(v, axis=0, keepdims=True), acc_ref.shape)


_G0 = math.sqrt(2.0 / math.pi)
_G1 = 0.044715


def _gelu(x):
    t = jnp.tanh(_G0 * (x + _G1 * (x * x * x)))
    return x * (0.5 * (1.0 + t)), t


def _gelu_grad(x, t):
    return 0.5 * (1.0 + t) + 0.5 * x * (1.0 - t * t) * (_G0 * (1.0 + 3.0 * _G1 * x * x))


NT = (((1,), (1,)), ((), ()))
TN = (((0,), (0,)), ((), ()))


def _dot(a, b, dims=None):
    if dims is None:
        return jnp.dot(a, b, preferred_element_type=F32)
    return lax.dot_general(a, b, dims, preferred_element_type=F32)


def _proj_fwd(x, g1, w_in_t):
    T = x.shape[0]
    tm = TM_PROJ

    def body(x_ref, g_ref, w_ref, hn_ref, q_ref, k_ref, v_ref, u_ref, z_ref):
        n, _ = _rms(x_ref[...])
        hn = (n * g_ref[...]).astype(BF16)
        hn_ref[...] = hn
        a = ATTN_W
        q_ref[...] = _dot(hn, w_ref[0:a, :], NT) * Q_SCALE
        k_ref[...] = _dot(hn, w_ref[a:2 * a, :], NT)
        v_ref[...] = _dot(hn, w_ref[2 * a:3 * a, :], NT)
        u_ref[...] = _dot(hn, w_ref[3 * a:3 * a + GMLP_W, :], NT)
        z_ref[...] = _dot(hn, w_ref[3 * a + GMLP_W:, :], NT)

    sds = jax.ShapeDtypeStruct
    return pl.pallas_call(
        body, name="proj_fwd", grid=(T // tm,),
        in_specs=[_rows(tm, D_MODEL), _resident((1, D_MODEL)), _resident((IN_W, D_MODEL))],
        out_specs=[_rows(tm, D_MODEL), _rows(tm, ATTN_W), _rows(tm, ATTN_W), _rows(tm, ATTN_W),
                   _rows(tm, GMLP_W), _rows(tm, GMLP_W)],
        out_shape=[sds((T, D_MODEL), BF16), sds((T, ATTN_W), F32), sds((T, ATTN_W), F32),
                   sds((T, ATTN_W), F32), sds((T, GMLP_W), F32), sds((T, GMLP_W), F32)],
        compiler_params=_params(("parallel",)),
    )(x, g1, w_in_t)


ATT_TILE = 2048
ATT_BLOCKS = ATT_TILE // CHUNK
SM_BLOCKS = 4


def _slope_table():
    row = np.repeat(SLOPES, HEAD_DIM)
    return jnp.asarray(np.broadcast_to(row[None], (8, ATTN_W)), F32)


def _residue_view(a):
    return a.reshape(a.shape[0] // ATT_BLOCKS, ATT_BLOCKS, a.shape[1])


def _tile_copies(hbm, buf, sem, hp, t, to_hbm=False, lane0=0):
    rows = pl.ds(pl.multiple_of(t * CHUNK, CHUNK), CHUNK)
    lanes = pl.ds(pl.multiple_of(lane0 + hp * LANES, LANES), LANES)
    pairs = [(hbm.at[rows, r, lanes], buf.at[r]) for r in range(ATT_BLOCKS)]
    return [pltpu.make_async_copy(v, h, sem) if to_hbm else pltpu.make_async_copy(h, v, sem) for h, v in pairs]


def _wait_tile(buf, sem):
    pltpu.make_async_copy(buf, buf, sem).wait()


def _residue_rows(d, j):
    if d == 16:
        return [(j, 0, CHUNK)]
    if d == 4:
        return [(j % 4 + 4 * m, 32 * (j // 4), 32) for m in range(4)]
    return [(r, 8 * j, 8) for r in range(ATT_BLOCKS)]


def _block_order(p, d):
    if d == 16:
        return p
    if d == 4:
        return 4 * (p & 31) + (p >> 5)
    return 16 * (p & 7) + (p >> 3)


def _first_in_tile(d, j):
    return _residue_rows(d, j)[0][1] == 0


def _rm_block(buf, d, j):
    return jnp.concatenate([buf[r, lo:lo + n, :] for r, lo, n in _residue_rows(d, j)], axis=0)


def _rm_block_before(buf, buf_before, d, j):
    if _first_in_tile(d, j):
        return jnp.concatenate([buf_before[r, CHUNK - n:CHUNK, :] for r, _, n in _residue_rows(d, j)], axis=0)
    return jnp.concatenate([buf[r, lo - n:lo, :] for r, lo, n in _residue_rows(d, j)], axis=0)


def _rm_store(buf, d, j, val):
    at = 0
    for r, lo, n in _residue_rows(d, j):
        buf[r, lo:lo + n, :] = val[at:at + n, :]
        at += n


def _rm_add(buf, rows, val):
    at = 0
    for r, lo, n in rows:
        buf[r, lo:lo + n, :] += val[at:at + n, :]
        at += n


def _residue_bias(sl_ref, d):
    shape = (2 * CHUNK, 2 * CHUNK)
    row = lax.broadcasted_iota(jnp.int32, shape, 0)
    col = lax.broadcasted_iota(jnp.int32, shape, 1)
    steps = _block_order(row & (CHUNK - 1), d) + CHUNK - (_block_order(col & (CHUNK - 1), d) + (col & CHUNK))
    band = (steps >= 0) & (steps <= CHUNK)
    sl = sl_ref[0:1, :]
    upper = lax.broadcasted_iota(jnp.int32, (2 * CHUNK, 1), 0) < CHUNK
    slope2 = jnp.where(upper, sl[:, 0:1], sl[:, HEAD_DIM:HEAD_DIM + 1])
    return jnp.where(band, -(float(d) * slope2 * steps.astype(F32)), NEG)


def _stack_heads(xb, head0):
    zero = jnp.zeros_like(xb)
    return jnp.concatenate([jnp.where(head0, xb, zero), jnp.where(head0, zero, xb)], axis=0).astype(BF16)


def _unstack_heads(x2, head0):
    return jnp.where(head0, x2[:CHUNK, :], x2[CHUNK:, :])


def _attn_fwd(q, k, v, shards=()):
    T = q.shape[0]
    nt = T // ATT_TILE
    ns = len(shards)
    steps = (ATTN_W // LANES) * nt

    def body(sl_ref, q_hbm, k_hbm, v_hbm, *rest):
        x_refs, rest = rest[:ns], rest[ns:]
        attn_hbm, lse_hbm = rest[:2]
        g_refs, rest = rest[2:2 + ns], rest[2 + ns:]
        qbuf, kbuf, vbuf, obuf, lbuf = rest[:5]
        o_acc, l_acc = rest[5:8], rest[8:11]
        sem_q, sem_k, sem_v, sem_o, sem_l = rest[11:16]
        hp, t = pl.program_id(0), pl.program_id(1)
        step = hp * nt + t
        two, three = step % 2, step % 3
        before, after = (step + 2) % 3, (step + 1) % 3
        if ns:
            start, forward, finish = _gather_phases(x_refs, g_refs, *rest[16:])
            pl.when(step == 0)(start)
            pl.when(step == steps // 2)(forward)

        def fetch(hp_, t_, two_, three_):
            for cp in (_tile_copies(q_hbm, qbuf.at[two_], sem_q.at[two_], hp_, t_)
                       + _tile_copies(k_hbm, kbuf.at[three_], sem_k.at[three_], hp_, t_)
                       + _tile_copies(v_hbm, vbuf.at[three_], sem_v.at[three_], hp_, t_)):
                cp.start()

        @pl.when(step == 0)
        def _():
            kbuf[2] = jnp.zeros((ATT_BLOCKS, CHUNK, LANES), F32)
            vbuf[2] = jnp.zeros((ATT_BLOCKS, CHUNK, LANES), F32)
            fetch(0, 0, 0, 0)

        @pl.when(step + 1 < steps)
        def _():
            fetch((step + 1) // nt, (step + 1) % nt, 1 - two, after)

        _wait_tile(qbuf.at[two], sem_q.at[two])
        _wait_tile(kbuf.at[three], sem_k.at[three])
        _wait_tile(vbuf.at[three], sem_v.at[three])

        @pl.when(step >= 2)
        def _():
            _wait_tile(obuf.at[two], sem_o.at[two])
            _wait_tile(lbuf.at[two], sem_l.at[two])

        q_t, k_t, v_t = qbuf.at[two], kbuf.at[three], vbuf.at[three]
        k_b, v_b = kbuf.at[before], vbuf.at[before]
        head0 = lax.broadcasted_iota(jnp.int32, (CHUNK, LANES), 1) < HEAD_DIM
        no_key_before = jnp.where(lax.broadcasted_iota(jnp.int32, (2 * CHUNK, 2 * CHUNK), 1) < CHUNK, NEG, 0.0)
        for pi, d in enumerate(DILATIONS):
            bias = _residue_bias(sl_ref, d)

            def scores(j, d=d, bias=bias):
                kcat = jnp.concatenate([_rm_block_before(k_t, k_b, d, j), _rm_block(k_t, d, j)], axis=0).astype(BF16)
                vcat = jnp.concatenate([_rm_block_before(v_t, v_b, d, j), _rm_block(v_t, d, j)], axis=0).astype(BF16)
                s = _dot(_stack_heads(_rm_block(q_t, d, j), head0), kcat, NT) + bias
                if _first_in_tile(d, j):
                    s = s + jnp.where(t == 0, 1.0, 0.0) * no_key_before
                return s, vcat

            def output(j, p, vcat, scale, lse, d=d, pi=pi):
                _rm_store(o_acc[pi], d, j, _unstack_heads(_dot(p, vcat) * scale, head0))
                _rm_store(l_acc[pi], d, j, _unstack_heads(jnp.broadcast_to(lse, (2 * CHUNK, LANES)), head0))

            for j0 in range(0, ATT_BLOCKS, SM_BLOCKS):
                group = [scores(j) for j in range(j0, j0 + SM_BLOCKS)]
                s = jnp.concatenate([g[0] for g in group], axis=0)
                m = jnp.max(s, axis=-1, keepdims=True)
                p = jnp.exp(s - m)
                l = jnp.sum(p, axis=-1, keepdims=True)
                p, scale, lse = p.astype(BF16), 1.0 / l, m + jnp.log(l)
                for i, (_, vcat) in enumerate(group):
                    rows = slice(i * 2 * CHUNK, (i + 1) * 2 * CHUNK)
                    output(j0 + i, p[rows, :], vcat, scale[rows, :], lse[rows, :])

        for r in range(ATT_BLOCKS):
            a, b, c = l_acc[0][r], l_acc[1][r], l_acc[2][r]
            m = jnp.maximum(jnp.maximum(a, b), c)
            ea, eb, ec = jnp.exp(a - m), jnp.exp(b - m), jnp.exp(c - m)
            tot = ea + eb + ec
            obuf[two, r] = (ea * o_acc[0][r] + eb * o_acc[1][r] + ec * o_acc[2][r]) / tot
            lbuf[two, r] = m + jnp.log(tot)

        for cp in (_tile_copies(attn_hbm, obuf.at[two], sem_o.at[two], hp, t, to_hbm=True)
                   + _tile_copies(lse_hbm, lbuf.at[two], sem_l.at[two], hp, t, to_hbm=True)):
            cp.start()

        @pl.when(step == steps - 1)
        def _():
            for slot in (two, 1 - two)[:min(steps, 2)]:
                _wait_tile(obuf.at[slot], sem_o.at[slot])
                _wait_tile(lbuf.at[slot], sem_l.at[slot])

        if ns:
            pl.when(step == steps - 1)(finish)

    tile = lambda n: pltpu.VMEM((n, ATT_BLOCKS, CHUNK, LANES), F32)
    dma = lambda n: pltpu.SemaphoreType.DMA((n,))
    view = jax.ShapeDtypeStruct((T // ATT_BLOCKS, ATT_BLOCKS, ATTN_W), F32)
    outs = pl.pallas_call(
        body, name="attn_fwd", grid=(ATTN_W // LANES, nt),
        in_specs=[pl.BlockSpec((8, LANES), lambda c, t: (0, c))] + [_HBM] * (3 + ns),
        out_specs=[_HBM] * (2 + ns),
        out_shape=[view, view] + [_gathered_shape(s) for s in shards],
        scratch_shapes=[tile(2), tile(3), tile(3), tile(2), tile(2)] + [pltpu.VMEM((ATT_BLOCKS, CHUNK, LANES), F32)] * 6
        + [dma(2), dma(3), dma(3), dma(2), dma(2)] + (_gather_sems(ns) if ns else []),
        compiler_params=_params(("arbitrary", "arbitrary")),
    )(_slope_table(), _residue_view(q), _residue_view(k), _residue_view(v), *shards)
    return outs[0].reshape(T, ATTN_W), outs[1].reshape(T, ATTN_W), tuple(outs[2:])


def _group_mean(v, grp):
    out = jnp.zeros_like(v)
    for g in range(N_GROUPS):
        mk = grp == g
        s = jnp.sum(jnp.where(mk, v, 0.0), axis=-1, keepdims=True) * (1.0 / HEAD_DIM)
        out = jnp.where(mk, s, out)
    return out


def _gmlp_core(uu, zz, lg, lb, ws, sb_ref, grp):
    ug, tu = _gelu(uu)
    zg, tz = _gelu(zz)
    zc = zg - _group_mean(zg, grp)
    rstd = lax.rsqrt(_group_mean(zc * zc, grp) + EPS)
    xhat = zc * rstd
    zn16 = (xhat * lg + lb).astype(BF16)
    mixed = []
    for ci in range(uu.shape[0] // CHUNK):
        rows = slice(ci * CHUNK, (ci + 1) * CHUNK)
        m = jnp.zeros((CHUNK, GMLP_W), F32)
        for g in range(N_GROUPS):
            m = jnp.where(grp[:CHUNK] == g, _dot(ws[g], zn16[rows, :]) + sb_ref[:, g:g + 1], m)
        mixed.append(m)
    return ug, tu, tz, xhat, rstd, zn16, jnp.concatenate(mixed, axis=0)


def _causal_ws(w_ref):
    ti = lax.broadcasted_iota(jnp.int32, (CHUNK, CHUNK), 0)
    si = lax.broadcasted_iota(jnp.int32, (CHUNK, CHUNK), 1)
    causal = si <= ti
    return causal, [jnp.where(causal, w_ref[g], 0.0).astype(BF16) for g in range(N_GROUPS)]


def _gmlp_fwd(u, z, ln_g, ln_b, sgu_w, sgu_bt):
    T = u.shape[0]
    tg = TM_GMLP

    def body(u_ref, z_ref, g_ref, b_ref, w_ref, sb_ref, out_ref):
        grp = lax.broadcasted_iota(jnp.int32, (tg, GMLP_W), 1) // HEAD_DIM
        _, ws = _causal_ws(w_ref)
        ug, _, _, _, _, _, mixed = _gmlp_core(u_ref[...], z_ref[...], g_ref[...], b_ref[...], ws, sb_ref, grp)
        out_ref[...] = ug * mixed

    return pl.pallas_call(
        body, name="gmlp_fwd", grid=(T // tg,),
        in_specs=[_rows(tg, GMLP_W), _rows(tg, GMLP_W), _resident((1, GMLP_W)), _resident((1, GMLP_W)),
                  _resident((N_GROUPS, CHUNK, CHUNK)), _resident((CHUNK, N_GROUPS))],
        out_specs=_rows(tg, GMLP_W),
        out_shape=jax.ShapeDtypeStruct((T, GMLP_W), F32),
        compiler_params=_params(("parallel",)),
    )(u, z, ln_g, ln_b, sgu_w, sgu_bt)


def _out_fwd(attn, gm, ga, gg, w_out, x, g2):
    T = x.shape[0]
    tm = TM_PROJ

    def body(a_ref, m_ref, ga_ref, gg_ref, w_ref, x_ref, g2_ref, mix_ref, h1_ref, hn2_ref):
        an, _ = _rms(a_ref[...])
        gn, _ = _rms(m_ref[...])
        an = (an * ga_ref[...]).astype(BF16)
        gn = (gn * gg_ref[...]).astype(BF16)
        mix_ref[:, 0:ATTN_W] = an
        mix_ref[:, ATTN_W:] = gn
        h1 = x_ref[...] + _dot(an, w_ref[0:ATTN_W, :]) + _dot(gn, w_ref[ATTN_W:, :])
        h1_ref[...] = h1
        n2, _ = _rms(h1)
        hn2_ref[...] = (n2 * g2_ref[...]).astype(BF16)

    sds = jax.ShapeDtypeStruct
    return pl.pallas_call(
        body, name="out_fwd", grid=(T // tm,),
        in_specs=[_rows(tm, ATTN_W), _rows(tm, GMLP_W), _resident((1, ATTN_W)), _resident((1, GMLP_W)),
                  _resident((D_MODEL, D_MODEL)), _rows(tm, D_MODEL), _resident((1, D_MODEL))],
        out_specs=[_rows(tm, D_MODEL)] * 3,
        out_shape=[sds((T, D_MODEL), BF16), sds((T, D_MODEL), F32), sds((T, D_MODEL), BF16)],
        compiler_params=_params(("parallel",)),
    )(attn, gm, ga, gg, w_out, x, g2)


def _ffn_fwd(hn2, h1, w1t, w2, gf, tgt):
    T = h1.shape[0]
    tm = TM_FFN

    def body(hn_ref, h1_ref, w1_ref, w2_ref, gf_ref, t_ref, r_ref, dhf_ref, dhb_ref, loss_ref, dgf_ref):
        i = pl.program_id(0)

        @pl.when(i == 0)
        def _():
            loss_ref[...] = jnp.zeros_like(loss_ref)
            dgf_ref[...] = jnp.zeros_like(dgf_ref)

        hn = hn_ref[...]
        acc = h1_ref[...]
        for j in range(D_FF // FF_CHUNK):
            cols = slice(j * FF_CHUNK, (j + 1) * FF_CHUNK)
            r = jnp.maximum(_dot(hn, w1_ref[cols, :], NT), 0.0)
            r_ref[:, cols] = r.astype(BF16)
            act = jnp.square(r).astype(BF16)
            acc = acc + _dot(act, w2_ref[cols, :])
        n3, r3 = _rms(acc)
        gf_row = gf_ref[...]
        e = n3 * gf_row - t_ref[...]
        loss_ref[...] += 0.5 * jnp.sum(jnp.mean(e * e, axis=-1, keepdims=True))
        dy = e * (1.0 / D_MODEL)
        _accum_rows(dgf_ref, dy * n3)
        dh2 = _rms_bwd(n3, r3, gf_row, dy)
        dhf_ref[...] = dh2
        dhb_ref[...] = dh2.astype(BF16)

    sds = jax.ShapeDtypeStruct
    acc_spec = lambda n: pl.BlockSpec((8, n), lambda i: (0, 0))
    return pl.pallas_call(
        body, name="ffn_fwd", grid=(T // tm,),
        in_specs=[_rows(tm, D_MODEL), _rows(tm, D_MODEL), _resident((D_FF, D_MODEL)), _resident((D_FF, D_MODEL)),
                  _resident((1, D_MODEL)), _rows(tm, D_MODEL)],
        out_specs=[_rows(tm, D_FF), _rows(tm, D_MODEL), _rows(tm, D_MODEL), acc_spec(LANES), acc_spec(D_MODEL)],
        out_shape=[sds((T, D_FF), BF16), sds((T, D_MODEL), F32), sds((T, D_MODEL), BF16),
                   sds((8, LANES), F32), sds((8, D_MODEL), F32)],
        compiler_params=_params(("arbitrary",)),
    )(hn2, h1, w1t, w2, gf, tgt)


def _ffn_bwd(dh2b, dh2f, relu, h1, g2, w2, w1t):
    T = h1.shape[0]
    tm = TM_FFN

    def body(db_ref, df_ref, r_ref, h1_ref, g2_ref, w2_ref, w1t_ref, da_ref, d1f_ref, d1b_ref, dg_ref):
        @pl.when(pl.program_id(0) == 0)
        def _():
            dg_ref[...] = jnp.zeros_like(dg_ref)

        db = db_ref[...]
        acc = jnp.zeros((tm, D_MODEL), F32)
        for j in range(D_FF // FF_CHUNK):
            cols = slice(j * FF_CHUNK, (j + 1) * FF_CHUNK)
            da = (_dot(db, w2_ref[cols, :], NT) * (2.0 * r_ref[:, cols].astype(F32))).astype(BF16)
            da_ref[:, cols] = da
            acc = acc + _dot(da, w1t_ref[cols, :])
        n2, r2 = _rms(h1_ref[...])
        _accum_rows(dg_ref, acc * n2)
        dh1 = df_ref[...] + _rms_bwd(n2, r2, g2_ref[...], acc)
        d1f_ref[...] = dh1
        d1b_ref[...] = dh1.astype(BF16)

    sds = jax.ShapeDtypeStruct
    return pl.pallas_call(
        body, name="ffn_bwd", grid=(T // tm,),
        in_specs=[_rows(tm, D_MODEL), _rows(tm, D_MODEL), _rows(tm, D_FF), _rows(tm, D_MODEL),
                  _resident((1, D_MODEL)), _resident((D_FF, D_MODEL)), _resident((D_FF, D_MODEL))],
        out_specs=[_rows(tm, D_FF), _rows(tm, D_MODEL), _rows(tm, D_MODEL),
                   pl.BlockSpec((8, D_MODEL), lambda i: (0, 0))],
        out_shape=[sds((T, D_FF), BF16), sds((T, D_MODEL), F32), sds((T, D_MODEL), BF16), sds((8, D_MODEL), F32)],
        compiler_params=_params(("arbitrary",)),
    )(dh2b, dh2f, relu, h1, g2, w2, w1t)


def _out_bwd(dh1b, w_out, attn, gm, ga, gg):
    T = attn.shape[0]
    tm = TM_PROJ

    def body(d_ref, w_ref, a_ref, m_ref, ga_ref, gg_ref, da_ref, dm_ref, dga_ref, dgg_ref):
        @pl.when(pl.program_id(0) == 0)
        def _():
            dga_ref[...] = jnp.zeros_like(dga_ref)
            dgg_ref[...] = jnp.zeros_like(dgg_ref)

        d = d_ref[...]
        dan = _dot(d, w_ref[0:ATTN_W, :], NT)
        dgn = _dot(d, w_ref[ATTN_W:, :], NT)
        na, ra = _rms(a_ref[...])
        ng, rg = _rms(m_ref[...])
        _accum_rows(dga_ref, dan * na)
        _accum_rows(dgg_ref, dgn * ng)
        da_ref[...] = _rms_bwd(na, ra, ga_ref[...], dan)
        dm_ref[...] = _rms_bwd(ng, rg, gg_ref[...], dgn)

    sds = jax.ShapeDtypeStruct
    return pl.pallas_call(
        body, name="out_bwd", grid=(T // tm,),
        in_specs=[_rows(tm, D_MODEL), _resident((D_MODEL, D_MODEL)), _rows(tm, ATTN_W), _rows(tm, GMLP_W),
                  _resident((1, ATTN_W)), _resident((1, GMLP_W))],
        out_specs=[_rows(tm, ATTN_W), _rows(tm, GMLP_W), pl.BlockSpec((8, ATTN_W), lambda i: (0, 0)),
                   pl.BlockSpec((8, GMLP_W), lambda i: (0, 0))],
        out_shape=[sds((T, ATTN_W), F32), sds((T, GMLP_W), F32), sds((8, ATTN_W), F32), sds((8, GMLP_W), F32)],
        compiler_params=_params(("arbitrary",)),
    )(dh1b, w_out, attn, gm, ga, gg)


def _gmlp_bwd(u, z, dgm, ln_g, ln_b, sgu_w, sgu_bt):
    T = u.shape[0]
    tg = TM_GMLP
    nsteps = T // tg

    def body(u_ref, z_ref, d_ref, g_ref, b_ref, w_ref, sb_ref, dproj_hbm, dlg_ref, dlb_ref, dw_ref, dsb_ref,
             stage, sem):
        i = pl.program_id(0)
        slot = i % 2
        duz_ref = stage.at[slot]

        def to_dproj(step, buf):
            rows = pl.ds(pl.multiple_of(step * tg, tg), tg)
            return pltpu.make_async_copy(stage.at[buf], dproj_hbm.at[rows, pl.ds(3 * ATTN_W, 2 * GMLP_W)],
                                         sem.at[buf])

        @pl.when(i == 0)
        def _():
            for ref in (dlg_ref, dlb_ref, dw_ref, dsb_ref):
                ref[...] = jnp.zeros_like(ref)

        @pl.when(i >= 2)
        def _():
            to_dproj(i - 2, slot).wait()

        grp = lax.broadcasted_iota(jnp.int32, (tg, GMLP_W), 1) // HEAD_DIM
        lane = lax.broadcasted_iota(jnp.int32, (CHUNK, LANES), 1)
        causal, ws = _causal_ws(w_ref)
        lg = g_ref[...]
        uu, zz, dgm = u_ref[...], z_ref[...], d_ref[...]
        ug, tu, tz, xhat, rstd, zn16, mixed = _gmlp_core(uu, zz, lg, b_ref[...], ws, sb_ref, grp)
        dmx = dgm * ug
        duz_ref[:, 0:GMLP_W] = dgm * mixed * _gelu_grad(uu, tu)
        dmx16 = dmx.astype(BF16)
        dzn = []
        for ci in range(tg // CHUNK):
            rows = slice(ci * CHUNK, (ci + 1) * CHUNK)
            dmx_c, d = dmx16[rows, :], jnp.zeros((CHUNK, GMLP_W), F32)
            for g in range(N_GROUPS):
                mk = grp[:CHUNK] == g
                d = jnp.where(mk, _dot(ws[g], dmx_c, TN), d)
                dw_ref[g] += _dot(jnp.where(mk, dmx_c, jnp.zeros_like(dmx_c)), zn16[rows, :], NT)
            dzn.append(d)
        dzn = jnp.concatenate(dzn, axis=0)
        dsb = jnp.zeros((CHUNK, LANES), F32)
        for g in range(N_GROUPS):
            per_token = jnp.sum(jnp.where(grp == g, dmx, 0.0), axis=-1, keepdims=True)
            by_position = sum(per_token[ci * CHUNK:(ci + 1) * CHUNK] for ci in range(tg // CHUNK))
            dsb = jnp.where(lane == g, by_position, dsb)
        dsb_ref[...] += dsb
        _accum_rows(dlg_ref, dzn * xhat)
        _accum_rows(dlb_ref, dzn)
        dxh = dzn * lg
        dzg = rstd * (dxh - _group_mean(dxh, grp) - xhat * _group_mean(dxh * xhat, grp))
        duz_ref[:, GMLP_W:] = dzg * _gelu_grad(zz, tz)
        to_dproj(i, slot).start()

        @pl.when(i == nsteps - 1)
        def _():
            for g in range(N_GROUPS):
                dw_ref[g] = jnp.where(causal, dw_ref[g], 0.0)
            to_dproj(i, slot).wait()
            if nsteps >= 2:
                to_dproj(i - 1, 1 - slot).wait()

    sds = jax.ShapeDtypeStruct
    return pl.pallas_call(
        body, name="gmlp_bwd", grid=(nsteps,),
        in_specs=[_rows(tg, GMLP_W)] * 3 + [_resident((1, GMLP_W)), _resident((1, GMLP_W)),
                                              _resident((N_GROUPS, CHUNK, CHUNK)), _resident((CHUNK, N_GROUPS))],
        out_specs=[_HBM, pl.BlockSpec((8, GMLP_W), lambda i: (0, 0)),
                   pl.BlockSpec((8, GMLP_W), lambda i: (0, 0)),
                   pl.BlockSpec((N_GROUPS, CHUNK, CHUNK), lambda i: (0, 0, 0)),
                   pl.BlockSpec((CHUNK, LANES), lambda i: (0, 0))],
        out_shape=[sds((T, IN_W), F32), sds((8, GMLP_W), F32), sds((8, GMLP_W), F32),
                   sds((N_GROUPS, CHUNK, CHUNK), F32), sds((CHUNK, LANES), F32)],
        scratch_shapes=[pltpu.VMEM((2, tg, 2 * GMLP_W), F32), pltpu.SemaphoreType.DMA((2,))],
        compiler_params=_params(("arbitrary",)),
    )(u, z, dgm, ln_g, ln_b, sgu_w, sgu_bt)


def _attn_bwd(q, k, v, dattn, attn, lse, dproj, owner_grads=()):
    T = q.shape[0]
    nt = T // ATT_TILE
    ns = len(owner_grads)
    steps = (ATTN_W // LANES) * nt

    def body(sl_ref, q_hbm, k_hbm, v_hbm, do_hbm, o_hbm, lse_hbm, _, *rest):
        p_refs, rest = rest[:ns], rest[ns:]
        dq_hbm = dk_hbm = dv_hbm = rest[0]
        r_refs, rest = rest[1:1 + ns], rest[1 + ns:]
        qbuf, dobuf, obuf, lbuf, kbuf, vbuf, dqbuf, dkbuf, dvbuf, delta_s = rest[:10]
        sem_q, sem_do, sem_o, sem_l, sem_k, sem_v, sem_dq, sem_dk, sem_dv = rest[10:19]
        hp, t = pl.program_id(0), pl.program_id(1)
        step = hp * nt + t
        two, three = step % 2, step % 3
        before, after = (step + 2) % 3, (step + 1) % 3
        if ns:
            start, finish = _owner_exchange_phases(p_refs, r_refs, *rest[19:])
            pl.when(step == 0)(start)

        def fetch(hp_, t_, two_, three_):
            for hbm, buf, sem, slot in ((q_hbm, qbuf, sem_q, two_), (do_hbm, dobuf, sem_do, two_),
                                        (o_hbm, obuf, sem_o, two_), (lse_hbm, lbuf, sem_l, two_),
                                        (k_hbm, kbuf, sem_k, three_), (v_hbm, vbuf, sem_v, three_)):
                for cp in _tile_copies(hbm, buf.at[slot], sem.at[slot], hp_, t_):
                    cp.start()

        @pl.when(step == 0)
        def _():
            kbuf[2] = jnp.zeros((ATT_BLOCKS, CHUNK, LANES), F32)
            vbuf[2] = jnp.zeros((ATT_BLOCKS, CHUNK, LANES), F32)
            dkbuf[3] = jnp.zeros((ATT_BLOCKS, CHUNK, LANES), F32)
            dvbuf[3] = jnp.zeros((ATT_BLOCKS, CHUNK, LANES), F32)
            fetch(0, 0, 0, 0)

        @pl.when(step + 1 < steps)
        def _():
            fetch((step + 1) // nt, (step + 1) % nt, 1 - two, after)

        for buf, sem in ((qbuf, sem_q), (dobuf, sem_do), (obuf, sem_o), (lbuf, sem_l)):
            _wait_tile(buf.at[two], sem.at[two])
        _wait_tile(kbuf.at[three], sem_k.at[three])
        _wait_tile(vbuf.at[three], sem_v.at[three])

        @pl.when(step >= 2)
        def _():
            _wait_tile(dqbuf.at[two], sem_dq.at[two])

        @pl.when(step >= 3)
        def _():
            _wait_tile(dkbuf.at[three], sem_dk.at[three])
            _wait_tile(dvbuf.at[three], sem_dv.at[three])

        zero_tile = jnp.zeros((ATT_BLOCKS, CHUNK, LANES), F32)
        dqbuf[two] = zero_tile
        dkbuf[three] = zero_tile
        dvbuf[three] = zero_tile

        q_t, do_t, l_t, k_t, v_t = qbuf.at[two], dobuf.at[two], lbuf.at[two], kbuf.at[three], vbuf.at[three]
        k_b, v_b = kbuf.at[before], vbuf.at[before]
        dq_t, dk_t, dv_t = dqbuf.at[two], dkbuf.at[three], dvbuf.at[three]
        dk_b, dv_b = dkbuf.at[before], dvbuf.at[before]
        sink = jnp.where(t > 0, before, 3)
        dk_sink, dv_sink = dkbuf.at[sink], dvbuf.at[sink]
        head0 = lax.broadcasted_iota(jnp.int32, (CHUNK, LANES), 1) < HEAD_DIM
        for r in range(ATT_BLOCKS):
            dd = dobuf[two, r] * obuf[two, r]
            d0 = jnp.sum(jnp.where(head0, dd, 0.0), axis=-1, keepdims=True)
            d1 = jnp.sum(jnp.where(head0, 0.0, dd), axis=-1, keepdims=True)
            delta_s[r] = jnp.where(head0, d0, d1)

        def column(xb):
            return jnp.concatenate([xb[:, 0:1], xb[:, HEAD_DIM:HEAD_DIM + 1]], axis=0)

        no_key_before = jnp.where(lax.broadcasted_iota(jnp.int32, (2 * CHUNK, 2 * CHUNK), 1) < CHUNK, NEG, 0.0)
        for d in DILATIONS:
            bias = _residue_bias(sl_ref, d)
            for j in range(ATT_BLOCKS):
                kcat = jnp.concatenate([_rm_block_before(k_t, k_b, d, j), _rm_block(k_t, d, j)], axis=0).astype(BF16)
                vcat = jnp.concatenate([_rm_block_before(v_t, v_b, d, j), _rm_block(v_t, d, j)], axis=0).astype(BF16)
                q2 = _stack_heads(_rm_block(q_t, d, j), head0)
                do2 = _stack_heads(_rm_block(do_t, d, j), head0)
                s = _dot(q2, kcat, NT) + bias
                if _first_in_tile(d, j):
                    s = s + jnp.where(t == 0, 1.0, 0.0) * no_key_before
                p = jnp.exp(s - column(_rm_block(l_t, d, j)))
                ds = (p * (_dot(do2, vcat, NT) - column(_rm_block(delta_s, d, j)))).astype(BF16)
                _rm_add(dq_t, _residue_rows(d, j), _unstack_heads(_dot(ds, kcat), head0))
                ck = _dot(ds, q2, TN)
                cv = _dot(p.astype(BF16), do2, TN)
                _rm_add(dk_t, _residue_rows(d, j), ck[CHUNK:, :])
                _rm_add(dv_t, _residue_rows(d, j), cv[CHUNK:, :])
                if _first_in_tile(d, j):
                    rows = [(r, CHUNK - n, n) for r, _, n in _residue_rows(d, j)]
                    _rm_add(dk_sink, rows, ck[:CHUNK, :])
                    _rm_add(dv_sink, rows, cv[:CHUNK, :])
                else:
                    rows = [(r, lo - n, n) for r, lo, n in _residue_rows(d, j)]
                    _rm_add(dk_t, rows, ck[:CHUNK, :])
                    _rm_add(dv_t, rows, cv[:CHUNK, :])

        for r in range(ATT_BLOCKS):
            dqbuf[two, r] = dqbuf[two, r] * Q_SCALE
        for cp in _tile_copies(dq_hbm, dq_t, sem_dq.at[two], hp, t, to_hbm=True):
            cp.start()

        @pl.when(t > 0)
        def _():
            for cp in (_tile_copies(dk_hbm, dk_b, sem_dk.at[before], hp, t - 1, to_hbm=True, lane0=ATTN_W)
                       + _tile_copies(dv_hbm, dv_b, sem_dv.at[before], hp, t - 1, to_hbm=True, lane0=2 * ATTN_W)):
                cp.start()

        @pl.when(t == nt - 1)
        def _():
            for cp in (_tile_copies(dk_hbm, dk_t, sem_dk.at[three], hp, t, to_hbm=True, lane0=ATTN_W)
                       + _tile_copies(dv_hbm, dv_t, sem_dv.at[three], hp, t, to_hbm=True, lane0=2 * ATTN_W)):
                cp.start()

        @pl.when(step == steps - 1)
        def _():
            for slot in range(2):
                _wait_tile(dqbuf.at[slot], sem_dq.at[slot])
            for slot in range(3):
                _wait_tile(dkbuf.at[slot], sem_dk.at[slot])
                _wait_tile(dvbuf.at[slot], sem_dv.at[slot])

        if ns:
            pl.when(step == steps - 1)(finish)

    tile = lambda n: pltpu.VMEM((n, ATT_BLOCKS, CHUNK, LANES), F32)
    dma = lambda n: pltpu.SemaphoreType.DMA((n,))
    view = jax.ShapeDtypeStruct((T // ATT_BLOCKS, ATT_BLOCKS, ATTN_W), F32)
    outs = pl.pallas_call(
        body, name="attn_bwd", grid=(ATTN_W // LANES, nt),
        in_specs=[pl.BlockSpec((8, LANES), lambda c, t: (0, c))] + [_HBM] * (7 + ns),
        out_specs=[_HBM] * (1 + ns),
        out_shape=[jax.ShapeDtypeStruct((T // ATT_BLOCKS, ATT_BLOCKS, IN_W), F32)]
        + [jax.ShapeDtypeStruct(p.shape, p.dtype) for p in owner_grads],
        scratch_shapes=[tile(2), tile(2), tile(2), tile(2), tile(3), tile(3), tile(2), tile(4), tile(4),
                        pltpu.VMEM((ATT_BLOCKS, CHUNK, LANES), F32)]
        + [dma(2), dma(2), dma(2), dma(2), dma(3), dma(3), dma(2), dma(3), dma(3)]
        + (_owner_exchange_sems(ns) if ns else []),
        input_output_aliases={7: 0},
        compiler_params=_params(("arbitrary", "arbitrary")),
    )(_slope_table(), *[_residue_view(a) for a in (q, k, v, dattn, attn, lse, dproj)], *owner_grads)
    return outs[0].reshape(T, IN_W), tuple(outs[1:])


def _proj_bwd(dproj, w_in_t, x, g1, dh1, chip_sums=()):
    T = x.shape[0]
    tm = TM_PROJ
    ns = len(chip_sums)
    steps = T // tm

    def body(d_ref, w_ref, x_ref, g_ref, r_ref, *rest):
        p_refs, rest = rest[:ns], rest[ns:]
        dx_ref, dg_ref = rest[:2]
        r_refs, sems = rest[2:2 + ns], rest[2 + ns:]
        step = pl.program_id(0)
        if ns:
            start, finish = _chip_exchange_phases(p_refs, r_refs, *sems)
            pl.when(step == 0)(start)

        @pl.when(step == 0)
        def _():
            dg_ref[...] = jnp.zeros_like(dg_ref)

        dhn = _dot(d_ref[...].astype(BF16), w_ref[...])
        n1, r1 = _rms(x_ref[...])
        _accum_rows(dg_ref, dhn * n1)
        dx_ref[...] = r_ref[...] + _rms_bwd(n1, r1, g_ref[...], dhn)
        if ns:
            pl.when(step == steps - 1)(finish)

    outs = pl.pallas_call(
        body, name="proj_bwd", grid=(steps,),
        in_specs=[_rows(tm, IN_W), _resident((IN_W, D_MODEL)), _rows(tm, D_MODEL), _resident((1, D_MODEL)),
                  _rows(tm, D_MODEL)] + [_HBM] * ns,
        out_specs=[_rows(tm, D_MODEL), pl.BlockSpec((8, D_MODEL), lambda i: (0, 0))] + [_HBM] * ns,
        out_shape=[jax.ShapeDtypeStruct((T, D_MODEL), F32), jax.ShapeDtypeStruct((8, D_MODEL), F32)]
        + [jax.ShapeDtypeStruct(p.shape, p.dtype) for p in chip_sums],
        scratch_shapes=_chip_exchange_sems(ns) if ns else [],
        compiler_params=_params(("arbitrary",)),
    )(dproj, w_in_t, x, g1, dh1, *chip_sums)
    return outs[0], outs[1], tuple(outs[2:])


def _dw(a, b, name, tile, square_a=False, out_dtype=F32):
    T, ka = a.shape
    nb = b.shape[1]
    tka, tnb, tt = tile
    tt = min(tt, T)
    last = T // tt - 1

    def body(a_ref, b_ref, *refs):
        o_ref = refs[0]
        acc_ref = refs[1] if len(refs) > 1 else o_ref
        s = pl.program_id(2)

        @pl.when(s == 0)
        def _():
            acc_ref[...] = jnp.zeros_like(acc_ref)

        a_tile = a_ref[...]
        if square_a:
            a_tile = jnp.square(a_tile.astype(F32))
        acc_ref[...] += _dot(a_tile.astype(BF16), b_ref[...], TN)
        if acc_ref is not o_ref:
            @pl.when(s == last)
            def _():
                o_ref[...] = acc_ref[...].astype(out_dtype)

    return pl.pallas_call(
        body, name=name, grid=(ka // tka, nb // tnb, T // tt),
        in_specs=[pl.BlockSpec((tt, tka), lambda i, j, s: (s, i)), pl.BlockSpec((tt, tnb), lambda i, j, s: (s, j))],
        out_specs=pl.BlockSpec((tka, tnb), lambda i, j, s: (i, j)),
        out_shape=jax.ShapeDtypeStruct((ka, nb), out_dtype),
        scratch_shapes=[] if out_dtype == F32 else [pltpu.VMEM((tka, tnb), F32)],
        compiler_params=_params(("parallel", "parallel", "arbitrary")),
    )(a, b)


def _adamw(w, m, v, parts, name, tr, transposed=False):
    R, C = w.shape
    P = parts.shape[0]

    def body(w_ref, m_ref, v_ref, p_ref, g_ref, d_ref, m2_ref, v2_ref):
        g = p_ref[0].astype(F32)
        for i in range(1, P):
            g = g + p_ref[i].astype(F32)
        if transposed:
            g = g.T
        m2 = ADAM_B1 * m_ref[...] + (1.0 - ADAM_B1) * g
        v2 = ADAM_B2 * v_ref[...] + (1.0 - ADAM_B2) * jnp.square(g)
        m_hat = m2 / (1.0 - ADAM_B1 ** ADAM_STEP)
        v_hat = v2 / (1.0 - ADAM_B2 ** ADAM_STEP)
        g_ref[...] = g
        d_ref[...] = -ADAM_LR * (m_hat / (jnp.sqrt(v_hat) + ADAM_EPS) + ADAM_WD * w_ref[...])
        m2_ref[...] = m2
        v2_ref[...] = v2

    spec = _rows(tr, C)
    part_spec = (pl.BlockSpec((P, C, tr), lambda i: (0, 0, i)) if transposed
                 else pl.BlockSpec((P, tr, C), lambda i: (0, i, 0)))
    return pl.pallas_call(
        body, name=name, grid=(R // tr,),
        in_specs=[spec, spec, spec, part_spec],
        out_specs=[spec] * 4,
        out_shape=[jax.ShapeDtypeStruct((R, C), F32)] * 4,
        compiler_params=_params(("parallel",)),
    )(w, m, v, parts)


def _pair_sum(core, grad, recv, name):
    _, _, n, C = grad.shape
    tr = n // 2

    def body(c_ref, a_ref, b_ref, o_ref):
        o_ref[...] = a_ref[...] + b_ref[...]

    spec = pl.BlockSpec((1, tr, C), lambda i, j, c_ref: (i, j, 0))
    return pl.pallas_call(
        body, name=name,
        grid_spec=pltpu.PrefetchScalarGridSpec(
            num_scalar_prefetch=1, grid=(4, n // tr),
            in_specs=[pl.BlockSpec((1, None, tr, C), lambda i, j, c_ref: (i, c_ref[0], j, 0)), spec],
            out_specs=spec),
        out_shape=jax.ShapeDtypeStruct(recv.shape, F32),
        compiler_params=_params(("parallel", "parallel")),
    )(core.reshape(1), grad, recv)


_HBM = pl.BlockSpec(memory_space=pltpu.HBM)


def _place():
    return lax.axis_index("x"), lax.axis_index("y"), lax.axis_index("c")


def _gathered_shape(shard):
    return jax.ShapeDtypeStruct((N_DEV,) + shard.shape, shard.dtype)


def _gather_sems(n):
    return [pltpu.SemaphoreType.DMA((7, n)), pltpu.SemaphoreType.DMA((7, n)), pltpu.SemaphoreType.DMA((n,))]


def _gather_phases(x_refs, out_refs, send_sems, recv_sems, local_sems):
    x, y, c = _place()
    me, sibling = (x, y, c), (x, y, 1 - c)
    chips = [(1 - x, y), (x, 1 - y), (1 - x, 1 - y)]
    arrays = range(len(x_refs))

    def slot(i, px, py, pc):
        return out_refs[i].at[4 * px + 2 * py + pc]

    def copy(i, k, block, to, own=False):
        return pltpu.make_async_remote_copy(
            src_ref=x_refs[i] if own else slot(i, *block), dst_ref=slot(i, *block),
            send_sem=send_sems.at[k, i], recv_sem=recv_sems.at[k, i], device_id=to, device_id_type=MESH)

    def mine(i):
        return pltpu.make_async_copy(x_refs[i], slot(i, *me), local_sems.at[i])

    def start():
        for i in arrays:
            mine(i).start()
            copy(i, 0, me, sibling, own=True).start()
            for j, chip in enumerate(chips):
                copy(i, 1 + j, me, (*chip, c), own=True).start()

    def forward():
        for i in arrays:
            for j, chip in enumerate(chips):
                copy(i, 1 + j, (*chip, c), me).wait_recv()
                copy(i, 4 + j, (*chip, c), sibling).start()

    def finish():
        for i in arrays:
            copy(i, 0, sibling, me).wait_recv()
            copy(i, 0, me, sibling, own=True).wait_send()
            for j, chip in enumerate(chips):
                copy(i, 4 + j, (*chip, 1 - c), me).wait_recv()
                copy(i, 1 + j, me, (*chip, c), own=True).wait_send()
                copy(i, 4 + j, (*chip, c), sibling).wait_send()
            mine(i).wait()

    return start, forward, finish


def _all_gather(shards, name):
    n = len(shards)

    def body(*refs):
        start, forward, finish = _gather_phases(refs[:n], refs[n:2 * n], *refs[2 * n:])
        start()
        forward()
        finish()

    return pl.pallas_call(
        body, name=name,
        out_shape=[_gathered_shape(s) for s in shards],
        in_specs=[_HBM] * n, out_specs=[_HBM] * n,
        scratch_shapes=_gather_sems(n),
    )(*shards)


def _sibling_exchange(grads, name):
    n = len(grads)

    def body(*refs):
        g_refs, r_refs, send_sems, recv_sems = refs[:n], refs[n:2 * n], refs[2 * n], refs[2 * n + 1]
        x, y, c = _place()
        copies = [pltpu.make_async_remote_copy(
            src_ref=g_refs[i].at[:, 1 - c], dst_ref=r_refs[i], send_sem=send_sems.at[i], recv_sem=recv_sems.at[i],
            device_id=(x, y, 1 - c), device_id_type=MESH) for i in range(n)]
        for cp in copies:
            cp.start()
        for cp in copies:
            cp.wait()

    return pl.pallas_call(
        body, name=name,
        out_shape=[jax.ShapeDtypeStruct((g.shape[0],) + g.shape[2:], g.dtype) for g in grads],
        in_specs=[_HBM] * n, out_specs=[_HBM] * n,
        scratch_shapes=[pltpu.SemaphoreType.DMA((n,)), pltpu.SemaphoreType.DMA((n,))],
    )(*grads)


def _owner_exchange_sems(n):
    return [pltpu.SemaphoreType.DMA((7, n)), pltpu.SemaphoreType.DMA((7, n)), pltpu.SemaphoreType.DMA((n,))]


def _owner_exchange_phases(g_refs, r_refs, send_sems, recv_sems, local_sems):
    x, y, c = _place()
    me = 4 * x + 2 * y + c
    flip = lambda v, bit: 1 - v if bit else v
    peers = [(flip(x, k & 4), flip(y, k & 2), flip(c, k & 1)) for k in range(1, N_DEV)]
    arrays = range(len(g_refs))

    def mine(i):
        return pltpu.make_async_copy(g_refs[i].at[me], r_refs[i].at[me], local_sems.at[i])

    def copy(i, k, src_slot, dst_slot):
        return pltpu.make_async_remote_copy(
            src_ref=g_refs[i].at[src_slot], dst_ref=r_refs[i].at[dst_slot],
            send_sem=send_sems.at[k, i], recv_sem=recv_sems.at[k, i], device_id=peers[k], device_id_type=MESH)

    def start():
        for i in arrays:
            mine(i).start()
            for k, (px, py, pc) in enumerate(peers):
                copy(i, k, 4 * px + 2 * py + pc, me).start()

    def finish():
        for i in arrays:
            for k, (px, py, pc) in enumerate(peers):
                copy(i, k, me, 4 * px + 2 * py + pc).wait_recv()
                copy(i, k, 4 * px + 2 * py + pc, me).wait_send()
            mine(i).wait()

    return start, finish


def _chip_exchange_sems(n):
    return [pltpu.SemaphoreType.DMA((3, n)), pltpu.SemaphoreType.DMA((3, n)), pltpu.SemaphoreType.DMA((n,))]


def _chip_exchange_phases(p_refs, r_refs, send_sems, recv_sems, local_sems):
    x, y, c = _place()
    my_chip = 2 * x + y
    chips = [(1 - x, y), (x, 1 - y), (1 - x, 1 - y)]
    arrays = range(len(p_refs))

    def mine(i):
        return pltpu.make_async_copy(p_refs[i].at[my_chip], r_refs[i].at[my_chip], local_sems.at[i])

    def copy(i, k, src_chip, dst_chip):
        px, py = chips[k]
        return pltpu.make_async_remote_copy(
            src_ref=p_refs[i].at[src_chip], dst_ref=r_refs[i].at[dst_chip],
            send_sem=send_sems.at[k, i], recv_sem=recv_sems.at[k, i], device_id=(px, py, c), device_id_type=MESH)

    def start():
        for i in arrays:
            mine(i).start()
            for k, (px, py) in enumerate(chips):
                copy(i, k, 2 * px + py, my_chip).start()

    def finish():
        for i in arrays:
            for k, (px, py) in enumerate(chips):
                copy(i, k, my_chip, 2 * px + py).wait_recv()
                copy(i, k, 2 * px + py, my_chip).wait_send()
            mine(i).wait()

    return start, finish


_R_IN, _R_OUT, _R_FF = IN_W // N_DEV, D_MODEL // N_DEV, D_FF // N_DEV


def _by_owner(g):
    return g.reshape(4, 2, g.shape[0] // N_DEV, D_MODEL)


def _local_step(x, tgt, small, w_in_t, rest, core=None):
    exchange = core is not None
    g1, g2, gf = small["norm1_g"], small["norm2_g"], small["final_norm_g"].reshape(1, D_MODEL)
    ga, gg = small["attn_out_g"], small["gmlp_out_g"]
    ln_g = small["sgu_ln_g"].reshape(1, GMLP_W)
    ln_b = small["sgu_ln_b"].reshape(1, GMLP_W)
    sgu_w = small["sgu_w"][0]
    sgu_bt = small["sgu_b"][0].T

    hn1, q, k, v, u, z = _proj_fwd(x, g1, w_in_t)
    attn, lse, gathered = _attn_fwd(q, k, v, shards=rest if exchange else ())
    w_out, w_ff1_t, w_ff2 = [g.reshape(-1, D_MODEL) for g in gathered] if exchange else rest
    gm = _gmlp_fwd(u, z, ln_g, ln_b, sgu_w, sgu_bt)
    mixed, h1, hn2 = _out_fwd(attn, gm, ga, gg, w_out, x, g2)
    relu, dh2f, dh2b, loss8, dgf8 = _ffn_fwd(hn2, h1, w_ff1_t, w_ff2, gf, tgt)

    da, dh1f, dh1b, dg2 = _ffn_bwd(dh2b, dh2f, relu, h1, g2, w_ff2, w_ff1_t)
    wire = BF16 if exchange else F32
    dw_ff2 = _dw(relu, dh2b, "dw_ff2", DW_TILE, square_a=True, out_dtype=wire)
    dw_ff1_t = _dw(da, hn2, "dw_ff1", DW_TILE, out_dtype=wire)
    dattn, dgm, dga, dgg = _out_bwd(dh1b, w_out, attn, gm, ga, gg)
    dw_out = _dw(mixed, dh1b, "dw_out", DW_TILE, out_dtype=wire)
    early = [dw_out, dw_ff1_t, dw_ff2]
    if exchange:
        early = [g.reshape(N_DEV, -1, D_MODEL) for g in early]
    dproj, dlg, dlb, dsw, dsb = _gmlp_bwd(u, z, dgm, ln_g, ln_b, sgu_w, sgu_bt)
    dproj, arrived = _attn_bwd(q, k, v, dattn, attn, lse, dproj, owner_grads=early if exchange else ())
    dw_in_t = _dw(dproj, hn1, "dw_in", DW_TILE_IN)
    late = ()
    if exchange:
        by_owner = _by_owner(dw_in_t)
        got, = _sibling_exchange([by_owner], "grad_sibling_exchange")
        late = (_pair_sum(core, by_owner, got, "grad_pair_sum"),)
    dx, dg1, late = _proj_bwd(dproj, w_in_t, x, g1, dh1f, chip_sums=late)
    if exchange:
        dw_in_t, early = late[0], arrived

    small_grads = dict(
        norm1_g=dg1[0], sgu_ln_g=dlg[0], sgu_ln_b=dlb[0], sgu_w=dsw, sgu_b=dsb[:, :N_GROUPS].T,
        attn_out_g=dga[0], gmlp_out_g=dgg[0], norm2_g=dg2[0], final_norm_g=dgf8[0])
    return loss8[0, 0], dx, (dw_in_t, *early), small_grads


SMALL_NAMES = ("norm1_g", "sgu_ln_g", "sgu_ln_b", "sgu_w", "sgu_b", "attn_out_g", "gmlp_out_g", "norm2_g",
               "final_norm_g")
WEIGHT_ORDER = ("norm1_g", "w_in", "sgu_ln_g", "sgu_ln_b", "sgu_w", "sgu_b", "attn_out_g", "gmlp_out_g", "w_out",
                "norm2_g", "w_ff1", "w_ff2", "final_norm_g")


TINY_NAMES = tuple(n for n in SMALL_NAMES if n != "sgu_w")
TINY_ROWS = 48


def _pack_tiny(d, last):
    return jnp.concatenate([d[n].reshape(-1, LANES) for n in TINY_NAMES] + [last], axis=0)


def _unpack_tiny(p, like):
    out, r = {}, 0
    for n in TINY_NAMES:
        rows = like[n].size // LANES
        out[n] = p[r:r + rows].reshape(like[n].shape)
        r += rows
    return out


def kernel(x, norm1_g, w_in, sgu_ln_g, sgu_ln_b, sgu_w, sgu_b, attn_out_g, gmlp_out_g, w_out, norm2_g, w_ff1, w_ff2, final_norm_g, loss_target, m_norm1_g, m_w_in, m_sgu_ln_g, m_sgu_ln_b, m_sgu_w, m_sgu_b, m_attn_out_g, m_gmlp_out_g, m_w_out, m_norm2_g, m_w_ff1, m_w_ff2, m_final_norm_g, v_norm1_g, v_w_in, v_sgu_ln_g, v_sgu_ln_b, v_sgu_w, v_sgu_b, v_attn_out_g, v_gmlp_out_g, v_w_out, v_norm2_g, v_w_ff1, v_w_ff2, v_final_norm_g):
    w = dict(norm1_g=norm1_g, w_in=w_in, sgu_ln_g=sgu_ln_g, sgu_ln_b=sgu_ln_b, sgu_w=sgu_w, sgu_b=sgu_b,
             attn_out_g=attn_out_g, gmlp_out_g=gmlp_out_g, w_out=w_out, norm2_g=norm2_g, w_ff1=w_ff1, w_ff2=w_ff2,
             final_norm_g=final_norm_g)
    m = dict(norm1_g=m_norm1_g, w_in=m_w_in, sgu_ln_g=m_sgu_ln_g, sgu_ln_b=m_sgu_ln_b, sgu_w=m_sgu_w, sgu_b=m_sgu_b,
             attn_out_g=m_attn_out_g, gmlp_out_g=m_gmlp_out_g, w_out=m_w_out, norm2_g=m_norm2_g, w_ff1=m_w_ff1,
             w_ff2=m_w_ff2, final_norm_g=m_final_norm_g)
    v = dict(norm1_g=v_norm1_g, w_in=v_w_in, sgu_ln_g=v_sgu_ln_g, sgu_ln_b=v_sgu_ln_b, sgu_w=v_sgu_w, sgu_b=v_sgu_b,
             attn_out_g=v_attn_out_g, gmlp_out_g=v_gmlp_out_g, w_out=v_w_out, norm2_g=v_norm2_g, w_ff1=v_w_ff1,
             w_ff2=v_w_ff2, final_norm_g=v_final_norm_g)
    big = ("w_in", "w_out", "w_ff1", "w_ff2")
    core = lax.axis_index("c")

    w_in_t, = _all_gather([w_in[0].T.astype(BF16)], "w_in_all_gather")
    rest = (w_out[0].astype(BF16), w_ff1[0].T.astype(BF16), w_ff2[0].astype(BF16))
    loss, dx, parts, small_grads = _local_step(x[0], loss_target[0], {n: w[n] for n in SMALL_NAMES},
                                               w_in_t.reshape(IN_W, D_MODEL), rest, core=core)

    new = {}
    for n, p, transposed, tr in zip(big, parts, (True, False, True, False), (128, 128, 128, 256)):
        new[n] = [a[None] for a in _adamw(w[n][0], m[n][0], v[n][0], p, "adamw_" + n, tr, transposed)]

    flat = lambda a: a.reshape(-1, LANES)
    tiny_parts, sgu_parts = _all_gather(
        [_pack_tiny(small_grads, jnp.full((8, LANES), loss, F32)), flat(small_grads["sgu_w"])], "small_grad_all_gather")
    pad = jnp.ones((8, LANES), F32)
    tiny = _adamw(_pack_tiny(w, pad), _pack_tiny(m, pad), _pack_tiny(v, pad), tiny_parts, "adamw_tiny", TINY_ROWS)
    sgu = _adamw(flat(sgu_w), flat(m_sgu_w), flat(v_sgu_w), sgu_parts, "adamw_sgu_w", 512)
    loss = tiny[0][TINY_ROWS - 8, 0]

    outs = []
    for i in range(4):
        d = {n: new[n][i] for n in big}
        d.update(_unpack_tiny(tiny[i], w))
        d["sgu_w"] = sgu[i].reshape(sgu_w.shape)
        outs.extend(d[n] for n in WEIGHT_ORDER)
    return (loss, dx[None], *outs)
```

```python
import functools
import math

import numpy as np
import jax
import jax.numpy as jnp
from jax import lax
from jax.experimental import pallas as pl
from jax.experimental.pallas import tpu as pltpu

F32 = jnp.float32
BF16 = jnp.bfloat16

D_MODEL = 1024
HEAD_DIM = 64
N_HEADS = 12
ATTN_W = N_HEADS * HEAD_DIM
N_GROUPS = 4
GMLP_W = N_GROUPS * HEAD_DIM
IN_W = 3 * ATTN_W + 2 * GMLP_W
D_FF = 4 * D_MODEL
CHUNK = 128
DILATIONS = (1, 4, 16)
EPS = 1e-6
Q_SCALE = HEAD_DIM ** -0.5
NEG = -1e30

ADAM_LR, ADAM_B1, ADAM_B2, ADAM_EPS, ADAM_WD, ADAM_STEP = 0.001, 0.9, 0.999, 1e-08, 0.01, 10

N_DEV = 8
LANES = 128
VMEM_LIMIT = 56 << 20

TM_PROJ = 512
TM_FFN = 512
FF_CHUNK = 512
TM_GMLP = 1024
DW_TILE = (512, 1024, 4096)
DW_TILE_IN = (IN_W // 2, 1024, 2048)

MESH = pl.DeviceIdType.MESH


def _alibi_slopes(n):
    def pow2(m):
        start = 2.0 ** (-8.0 / m)
        return [start ** (i + 1) for i in range(m)]
    c = 2 ** int(math.floor(math.log2(n)))
    s = pow2(n) if c == n else pow2(c) + pow2(2 * c)[0::2][: n - c]
    return np.asarray(s, dtype=np.float32)


SLOPES = _alibi_slopes(N_HEADS)


def _params(sem=None):
    kw = dict(vmem_limit_bytes=VMEM_LIMIT)
    if sem is not None:
        kw["dimension_semantics"] = sem
    return pltpu.CompilerParams(**kw)


def _rows(tm, n):
    return pl.BlockSpec((tm, n), lambda i: (i, 0))


def _resident(shape):
    return pl.BlockSpec(shape, lambda *_: (0,) * len(shape), pipeline_mode=pl.Buffered(1))


def _rms(x):
    r = lax.rsqrt(jnp.mean(x * x, axis=-1, keepdims=True) + EPS)
    return x * r, r


def _rms_bwd(n, r, g, dy):
    dn = dy * g
    return r * (dn - n * jnp.mean(dn * n, axis=-1, keepdims=True))


def _accum_rows(acc_ref, v):
    acc_ref[...] += jnp.broadcast_to(jnp.sum(v, axis=0, keepdims=True), acc_ref.shape)


_G0 = math.sqrt(2.0 / math.pi)
_G1 = 0.044715


def _gelu(x):
    t = jnp.tanh(_G0 * (x + _G1 * (x * x * x)))
    return x * (0.5 * (1.0 + t)), t


def _gelu_grad(x, t):
    return 0.5 * (1.0 + t) + 0.5 * x * (1.0 - t * t) * (_G0 * (1.0 + 3.0 * _G1 * x * x))


NT = (((1,), (1,)), ((), ()))
TN = (((0,), (0,)), ((), ()))


def _dot(a, b, dims=None):
    if dims is None:
        return jnp.dot(a, b, preferred_element_type=F32)
    return lax.dot_general(a, b, dims, preferred_element_type=F32)


def _proj_fwd(x, g1, w_in_t):
    T = x.shape[0]
    tm = TM_PROJ

    def body(x_ref, g_ref, w_ref, hn_ref, q_ref, k_ref, v_ref, u_ref, z_ref):
        n, _ = _rms(x_ref[...])
        hn = (n * g_ref[...]).astype(BF16)
        hn_ref[...] = hn
        a = ATTN_W
        q_ref[...] = _dot(hn, w_ref[0:a, :], NT) * Q_SCALE
        k_ref[...] = _dot(hn, w_ref[a:2 * a, :], NT)
        v_ref[...] = _dot(hn, w_ref[2 * a:3 * a, :], NT)
        u_ref[...] = _dot(hn, w_ref[3 * a:3 * a + GMLP_W, :], NT)
        z_ref[...] = _dot(hn, w_ref[3 * a + GMLP_W:, :], NT)

    sds = jax.ShapeDtypeStruct
    return pl.pallas_call(
        body, name="proj_fwd", grid=(T // tm,),
        in_specs=[_rows(tm, D_MODEL), _resident((1, D_MODEL)), _resident((IN_W, D_MODEL))],
        out_specs=[_rows(tm, D_MODEL), _rows(tm, ATTN_W), _rows(tm, ATTN_W), _rows(tm, ATTN_W),
                   _rows(tm, GMLP_W), _rows(tm, GMLP_W)],
        out_shape=[sds((T, D_MODEL), BF16), sds((T, ATTN_W), F32), sds((T, ATTN_W), F32),
                   sds((T, ATTN_W), F32), sds((T, GMLP_W), F32), sds((T, GMLP_W), F32)],
        compiler_params=_params(("parallel",)),
    )(x, g1, w_in_t)


ATT_TILE = 2048
ATT_BLOCKS = ATT_TILE // CHUNK
SM_BLOCKS = 4


def _slope_table():
    row = np.repeat(SLOPES, HEAD_DIM)
    return jnp.asarray(np.broadcast_to(row[None], (8, ATTN_W)), F32)


def _residue_view(a):
    return a.reshape(a.shape[0] // ATT_BLOCKS, ATT_BLOCKS, a.shape[1])


def _tile_copies(hbm, buf, sem, hp, t, to_hbm=False, lane0=0):
    rows = pl.ds(pl.multiple_of(t * CHUNK, CHUNK), CHUNK)
    lanes = pl.ds(pl.multiple_of(lane0 + hp * LANES, LANES), LANES)
    pairs = [(hbm.at[rows, r, lanes], buf.at[r]) for r in range(ATT_BLOCKS)]
    return [pltpu.make_async_copy(v, h, sem) if to_hbm else pltpu.make_async_copy(h, v, sem) for h, v in pairs]


def _wait_tile(buf, sem):
    pltpu.make_async_copy(buf, buf, sem).wait()


def _residue_rows(d, j):
    if d == 16:
        return [(j, 0, CHUNK)]
    if d == 4:
        return [(j % 4 + 4 * m, 32 * (j // 4), 32) for m in range(4)]
    return [(r, 8 * j, 8) for r in range(ATT_BLOCKS)]


def _block_order(p, d):
    if d == 16:
        return p
    if d == 4:
        return 4 * (p & 31) + (p >> 5)
    return 16 * (p & 7) + (p >> 3)


def _first_in_tile(d, j):
    return _residue_rows(d, j)[0][1] == 0


def _rm_block(buf, d, j):
    return jnp.concatenate([buf[r, lo:lo + n, :] for r, lo, n in _residue_rows(d, j)], axis=0)


def _rm_block_before(buf, buf_before, d, j):
    if _first_in_tile(d, j):
        return jnp.concatenate([buf_before[r, CHUNK - n:CHUNK, :] for r, _, n in _residue_rows(d, j)], axis=0)
    return jnp.concatenate([buf[r, lo - n:lo, :] for r, lo, n in _residue_rows(d, j)], axis=0)


def _rm_store(buf, d, j, val):
    at = 0
    for r, lo, n in _residue_rows(d, j):
        buf[r, lo:lo + n, :] = val[at:at + n, :]
        at += n


def _rm_add(buf, rows, val):
    at = 0
    for r, lo, n in rows:
        buf[r, lo:lo + n, :] += val[at:at + n, :]
        at += n


def _residue_bias(sl_ref, d):
    shape = (2 * CHUNK, 2 * CHUNK)
    row = lax.broadcasted_iota(jnp.int32, shape, 0)
    col = lax.broadcasted_iota(jnp.int32, shape, 1)
    steps = _block_order(row & (CHUNK - 1), d) + CHUNK - (_block_order(col & (CHUNK - 1), d) + (col & CHUNK))
    band = (steps >= 0) & (steps <= CHUNK)
    sl = sl_ref[0:1, :]
    upper = lax.broadcasted_iota(jnp.int32, (2 * CHUNK, 1), 0) < CHUNK
    slope2 = jnp.where(upper, sl[:, 0:1], sl[:, HEAD_DIM:HEAD_DIM + 1])
    return jnp.where(band, -(float(d) * slope2 * steps.astype(F32)), NEG)


def _stack_heads(xb, head0):
    zero = jnp.zeros_like(xb)
    return jnp.concatenate([jnp.where(head0, xb, zero), jnp.where(head0, zero, xb)], axis=0).astype(BF16)


def _unstack_heads(x2, head0):
    return jnp.where(head0, x2[:CHUNK, :], x2[CHUNK:, :])


def _attn_fwd(q, k, v, shards=()):
    T = q.shape[0]
    nt = T // ATT_TILE
    ns = len(shards)
    steps = (ATTN_W // LANES) * nt

    def body(sl_ref, q_hbm, k_hbm, v_hbm, *rest):
        x_refs, rest = rest[:ns], rest[ns:]
        attn_hbm, lse_hbm = rest[:2]
        g_refs, rest = rest[2:2 + ns], rest[2 + ns:]
        qbuf, kbuf, vbuf, obuf, lbuf = rest[:5]
        o_acc, l_acc = rest[5:8], rest[8:11]
        sem_q, sem_k, sem_v, sem_o, sem_l = rest[11:16]
        hp, t = pl.program_id(0), pl.program_id(1)
        step = hp * nt + t
        two, three = step % 2, step % 3
        before, after = (step + 2) % 3, (step + 1) % 3
        if ns:
            start, forward, finish = _gather_phases(x_refs, g_refs, *rest[16:])
            pl.when(step == 0)(start)
            pl.when(step == steps // 2)(forward)

        def fetch(hp_, t_, two_, three_):
            for cp in (_tile_copies(q_hbm, qbuf.at[two_], sem_q.at[two_], hp_, t_)
                       + _tile_copies(k_hbm, kbuf.at[three_], sem_k.at[three_], hp_, t_)
                       + _tile_copies(v_hbm, vbuf.at[three_], sem_v.at[three_], hp_, t_)):
                cp.start()

        @pl.when(step == 0)
        def _():
            kbuf[2] = jnp.zeros((ATT_BLOCKS, CHUNK, LANES), F32)
            vbuf[2] = jnp.zeros((ATT_BLOCKS, CHUNK, LANES), F32)
            fetch(0, 0, 0, 0)

        @pl.when(step + 1 < steps)
        def _():
            fetch((step + 1) // nt, (step + 1) % nt, 1 - two, after)

        _wait_tile(qbuf.at[two], sem_q.at[two])
        _wait_tile(kbuf.at[three], sem_k.at[three])
        _wait_tile(vbuf.at[three], sem_v.at[three])

        @pl.when(step >= 2)
        def _():
            _wait_tile(obuf.at[two], sem_o.at[two])
            _wait_tile(lbuf.at[two], sem_l.at[two])

        q_t, k_t, v_t = qbuf.at[two], kbuf.at[three], vbuf.at[three]
        k_b, v_b = kbuf.at[before], vbuf.at[before]
        head0 = lax.broadcasted_iota(jnp.int32, (CHUNK, LANES), 1) < HEAD_DIM
        no_key_before = jnp.where(lax.broadcasted_iota(jnp.int32, (2 * CHUNK, 2 * CHUNK), 1) < CHUNK, NEG, 0.0)
        for pi, d in enumerate(DILATIONS):
            bias = _residue_bias(sl_ref, d)

            def scores(j, d=d, bias=bias):
                kcat = jnp.concatenate([_rm_block_before(k_t, k_b, d, j), _rm_block(k_t, d, j)], axis=0).astype(BF16)
                vcat = jnp.concatenate([_rm_block_before(v_t, v_b, d, j), _rm_block(v_t, d, j)], axis=0).astype(BF16)
                s = _dot(_stack_heads(_rm_block(q_t, d, j), head0), kcat, NT) + bias
                if _first_in_tile(d, j):
                    s = s + jnp.where(t == 0, 1.0, 0.0) * no_key_before
                return s, vcat

            def output(j, p, vcat, scale, lse, d=d, pi=pi):
                _rm_store(o_acc[pi], d, j, _unstack_heads(_dot(p, vcat) * scale, head0))
                _rm_store(l_acc[pi], d, j, _unstack_heads(jnp.broadcast_to(lse, (2 * CHUNK, LANES)), head0))

            for j0 in range(0, ATT_BLOCKS, SM_BLOCKS):
                group = [scores(j) for j in range(j0, j0 + SM_BLOCKS)]
                s = jnp.concatenate([g[0] for g in group], axis=0)
                m = jnp.max(s, axis=-1, keepdims=True)
                p = jnp.exp(s - m)
                l = jnp.sum(p, axis=-1, keepdims=True)
                p, scale, lse = p.astype(BF16), 1.0 / l, m + jnp.log(l)
                for i, (_, vcat) in enumerate(group):
                    rows = slice(i * 2 * CHUNK, (i + 1) * 2 * CHUNK)
                    output(j0 + i, p[rows, :], vcat, scale[rows, :], lse[rows, :])

        for r in range(ATT_BLOCKS):
            a, b, c = l_acc[0][r], l_acc[1][r], l_acc[2][r]
            m = jnp.maximum(jnp.maximum(a, b), c)
            ea, eb, ec = jnp.exp(a - m), jnp.exp(b - m), jnp.exp(c - m)
            tot = ea + eb + ec
            obuf[two, r] = (ea * o_acc[0][r] + eb * o_acc[1][r] + ec * o_acc[2][r]) / tot
            lbuf[two, r] = m + jnp.log(tot)

        for cp in (_tile_copies(attn_hbm, obuf.at[two], sem_o.at[two], hp, t, to_hbm=True)
                   + _tile_copies(lse_hbm, lbuf.at[two], sem_l.at[two], hp, t, to_hbm=True)):
            cp.start()

        @pl.when(step == steps - 1)
        def _():
            for slot in (two, 1 - two)[:min(steps, 2)]:
                _wait_tile(obuf.at[slot], sem_o.at[slot])
                _wait_tile(lbuf.at[slot], sem_l.at[slot])

        if ns:
            pl.when(step == steps - 1)(finish)

    tile = lambda n: pltpu.VMEM((n, ATT_BLOCKS, CHUNK, LANES), F32)
    dma = lambda n: pltpu.SemaphoreType.DMA((n,))
    view = jax.ShapeDtypeStruct((T // ATT_BLOCKS, ATT_BLOCKS, ATTN_W), F32)
    outs = pl.pallas_call(
        body, name="attn_fwd", grid=(ATTN_W // LANES, nt),
        in_specs=[pl.BlockSpec((8, LANES), lambda c, t: (0, c))] + [_HBM] * (3 + ns),
        out_specs=[_HBM] * (2 + ns),
        out_shape=[view, view] + [_gathered_shape(s) for s in shards],
        scratch_shapes=[tile(2), tile(3), tile(3), tile(2), tile(2)] + [pltpu.VMEM((ATT_BLOCKS, CHUNK, LANES), F32)] * 6
        + [dma(2), dma(3), dma(3), dma(2), dma(2)] + (_gather_sems(ns) if ns else []),
        compiler_params=_params(("arbitrary", "arbitrary")),
    )(_slope_table(), _residue_view(q), _residue_view(k), _residue_view(v), *shards)
    return outs[0].reshape(T, ATTN_W), outs[1].reshape(T, ATTN_W), tuple(outs[2:])


def _group_mean(v, grp):
    out = jnp.zeros_like(v)
    for g in range(N_GROUPS):
        mk = grp == g
        s = jnp.sum(jnp.where(mk, v, 0.0), axis=-1, keepdims=True) * (1.0 / HEAD_DIM)
        out = jnp.where(mk, s, out)
    return out


def _gmlp_core(uu, zz, lg, lb, ws, sb_ref, grp):
    ug, tu = _gelu(uu)
    zg, tz = _gelu(zz)
    zc = zg - _group_mean(zg, grp)
    rstd = lax.rsqrt(_group_mean(zc * zc, grp) + EPS)
    xhat = zc * rstd
    zn16 = (xhat * lg + lb).astype(BF16)
    mixed = []
    for ci in range(uu.shape[0] // CHUNK):
        rows = slice(ci * CHUNK, (ci + 1) * CHUNK)
        m = jnp.zeros((CHUNK, GMLP_W), F32)
        for g in range(N_GROUPS):
            m = jnp.where(grp[:CHUNK] == g, _dot(ws[g], zn16[rows, :]) + sb_ref[:, g:g + 1], m)
        mixed.append(m)
    return ug, tu, tz, xhat, rstd, zn16, jnp.concatenate(mixed, axis=0)


def _causal_ws(w_ref):
    ti = lax.broadcasted_iota(jnp.int32, (CHUNK, CHUNK), 0)
    si = lax.broadcasted_iota(jnp.int32, (CHUNK, CHUNK), 1)
    causal = si <= ti
    return causal, [jnp.where(causal, w_ref[g], 0.0).astype(BF16) for g in range(N_GROUPS)]


def _gmlp_fwd(u, z, ln_g, ln_b, sgu_w, sgu_bt):
    T = u.shape[0]
    tg = TM_GMLP

    def body(u_ref, z_ref, g_ref, b_ref, w_ref, sb_ref, out_ref):
        grp = lax.broadcasted_iota(jnp.int32, (tg, GMLP_W), 1) // HEAD_DIM
        _, ws = _causal_ws(w_ref)
        ug, _, _, _, _, _, mixed = _gmlp_core(u_ref[...], z_ref[...], g_ref[...], b_ref[...], ws, sb_ref, grp)
        out_ref[...] = ug * mixed

    return pl.pallas_call(
        body, name="gmlp_fwd", grid=(T // tg,),
        in_specs=[_rows(tg, GMLP_W), _rows(tg, GMLP_W), _resident((1, GMLP_W)), _resident((1, GMLP_W)),
                  _resident((N_GROUPS, CHUNK, CHUNK)), _resident((CHUNK, N_GROUPS))],
        out_specs=_rows(tg, GMLP_W),
        out_shape=jax.ShapeDtypeStruct((T, GMLP_W), F32),
        compiler_params=_params(("parallel",)),
    )(u, z, ln_g, ln_b, sgu_w, sgu_bt)


def _out_fwd(attn, gm, ga, gg, w_out, x, g2):
    T = x.shape[0]
    tm = TM_PROJ

    def body(a_ref, m_ref, ga_ref, gg_ref, w_ref, x_ref, g2_ref, mix_ref, h1_ref, hn2_ref):
        an, _ = _rms(a_ref[...])
        gn, _ = _rms(m_ref[...])
        an = (an * ga_ref[...]).astype(BF16)
        gn = (gn * gg_ref[...]).astype(BF16)
        mix_ref[:, 0:ATTN_W] = an
        mix_ref[:, ATTN_W:] = gn
        h1 = x_ref[...] + _dot(an, w_ref[0:ATTN_W, :]) + _dot(gn, w_ref[ATTN_W:, :])
        h1_ref[...] = h1
        n2, _ = _rms(h1)
        hn2_ref[...] = (n2 * g2_ref[...]).astype(BF16)

    sds = jax.ShapeDtypeStruct
    return pl.pallas_call(
        body, name="out_fwd", grid=(T // tm,),
        in_specs=[_rows(tm, ATTN_W), _rows(tm, GMLP_W), _resident((1, ATTN_W)), _resident((1, GMLP_W)),
                  _resident((D_MODEL, D_MODEL)), _rows(tm, D_MODEL), _resident((1, D_MODEL))],
        out_specs=[_rows(tm, D_MODEL)] * 3,
        out_shape=[sds((T, D_MODEL), BF16), sds((T, D_MODEL), F32), sds((T, D_MODEL), BF16)],
        compiler_params=_params(("parallel",)),
    )(attn, gm, ga, gg, w_out, x, g2)


def _ffn_fwd(hn2, h1, w1t, w2, gf, tgt):
    T = h1.shape[0]
    tm = TM_FFN

    def body(hn_ref, h1_ref, w1_ref, w2_ref, gf_ref, t_ref, r_ref, dhf_ref, dhb_ref, loss_ref, dgf_ref):
        i = pl.program_id(0)

        @pl.when(i == 0)
        def _():
            loss_ref[...] = jnp.zeros_like(loss_ref)
            dgf_ref[...] = jnp.zeros_like(dgf_ref)

        hn = hn_ref[...]
        acc = h1_ref[...]
        for j in range(D_FF // FF_CHUNK):
            cols = slice(j * FF_CHUNK, (j + 1) * FF_CHUNK)
            r = jnp.maximum(_dot(hn, w1_ref[cols, :], NT), 0.0)
            r_ref[:, cols] = r.astype(BF16)
            act = jnp.square(r).astype(BF16)
            acc = acc + _dot(act, w2_ref[cols, :])
        n3, r3 = _rms(acc)
        gf_row = gf_ref[...]
        e = n3 * gf_row - t_ref[...]
        loss_ref[...] += 0.5 * jnp.sum(jnp.mean(e * e, axis=-1, keepdims=True))
        dy = e * (1.0 / D_MODEL)
        _accum_rows(dgf_ref, dy * n3)
        dh2 = _rms_bwd(n3, r3, gf_row, dy)
        dhf_ref[...] = dh2
        dhb_ref[...] = dh2.astype(BF16)

    sds = jax.ShapeDtypeStruct
    acc_spec = lambda n: pl.BlockSpec((8, n), lambda i: (0, 0))
    return pl.pallas_call(
        body, name="ffn_fwd", grid=(T // tm,),
        in_specs=[_rows(tm, D_MODEL), _rows(tm, D_MODEL), _resident((D_FF, D_MODEL)), _resident((D_FF, D_MODEL)),
                  _resident((1, D_MODEL)), _rows(tm, D_MODEL)],
        out_specs=[_rows(tm, D_FF), _rows(tm, D_MODEL), _rows(tm, D_MODEL), acc_spec(LANES), acc_spec(D_MODEL)],
        out_shape=[sds((T, D_FF), BF16), sds((T, D_MODEL), F32), sds((T, D_MODEL), BF16),
                   sds((8, LANES), F32), sds((8, D_MODEL), F32)],
        compiler_params=_params(("arbitrary",)),
    )(hn2, h1, w1t, w2, gf, tgt)


def _ffn_bwd(dh2b, dh2f, relu, h1, g2, w2, w1t):
    T = h1.shape[0]
    tm = TM_FFN

    def body(db_ref, df_ref, r_ref, h1_ref, g2_ref, w2_ref, w1t_ref, da_ref, d1f_ref, d1b_ref, dg_ref):
        @pl.when(pl.program_id(0) == 0)
        def _():
            dg_ref[...] = jnp.zeros_like(dg_ref)

        db = db_ref[...]
        acc = jnp.zeros((tm, D_MODEL), F32)
        for j in range(D_FF // FF_CHUNK):
            cols = slice(j * FF_CHUNK, (j + 1) * FF_CHUNK)
            da = (_dot(db, w2_ref[cols, :], NT) * (2.0 * r_ref[:, cols].astype(F32))).astype(BF16)
            da_ref[:, cols] = da
            acc = acc + _dot(da, w1t_ref[cols, :])
        n2, r2 = _rms(h1_ref[...])
        _accum_rows(dg_ref, acc * n2)
        dh1 = df_ref[...] + _rms_bwd(n2, r2, g2_ref[...], acc)
        d1f_ref[...] = dh1
        d1b_ref[...] = dh1.astype(BF16)

    sds = jax.ShapeDtypeStruct
    return pl.pallas_call(
        body, name="ffn_bwd", grid=(T // tm,),
        in_specs=[_rows(tm, D_MODEL), _rows(tm, D_MODEL), _rows(tm, D_FF), _rows(tm, D_MODEL),
                  _resident((1, D_MODEL)), _resident((D_FF, D_MODEL)), _resident((D_FF, D_MODEL))],
        out_specs=[_rows(tm, D_FF), _rows(tm, D_MODEL), _rows(tm, D_MODEL),
                   pl.BlockSpec((8, D_MODEL), lambda i: (0, 0))],
        out_shape=[sds((T, D_FF), BF16), sds((T, D_MODEL), F32), sds((T, D_MODEL), BF16), sds((8, D_MODEL), F32)],
        compiler_params=_params(("arbitrary",)),
    )(dh2b, dh2f, relu, h1, g2, w2, w1t)


def _out_bwd(dh1b, w_out, attn, gm, ga, gg):
    T = attn.shape[0]
    tm = TM_PROJ

    def body(d_ref, w_ref, a_ref, m_ref, ga_ref, gg_ref, da_ref, dm_ref, dga_ref, dgg_ref):
        @pl.when(pl.program_id(0) == 0)
        def _():
            dga_ref[...] = jnp.zeros_like(dga_ref)
            dgg_ref[...] = jnp.zeros_like(dgg_ref)

        d = d_ref[...]
        dan = _dot(d, w_ref[0:ATTN_W, :], NT)
        dgn = _dot(d, w_ref[ATTN_W:, :], NT)
        na, ra = _rms(a_ref[...])
        ng, rg = _rms(m_ref[...])
        _accum_rows(dga_ref, dan * na)
        _accum_rows(dgg_ref, dgn * ng)
        da_ref[...] = _rms_bwd(na, ra, ga_ref[...], dan)
        dm_ref[...] = _rms_bwd(ng, rg, gg_ref[...], dgn)

    sds = jax.ShapeDtypeStruct
    return pl.pallas_call(
        body, name="out_bwd", grid=(T // tm,),
        in_specs=[_rows(tm, D_MODEL), _resident((D_MODEL, D_MODEL)), _rows(tm, ATTN_W), _rows(tm, GMLP_W),
                  _resident((1, ATTN_W)), _resident((1, GMLP_W))],
        out_specs=[_rows(tm, ATTN_W), _rows(tm, GMLP_W), pl.BlockSpec((8, ATTN_W), lambda i: (0, 0)),
                   pl.BlockSpec((8, GMLP_W), lambda i: (0, 0))],
        out_shape=[sds((T, ATTN_W), F32), sds((T, GMLP_W), F32), sds((8, ATTN_W), F32), sds((8, GMLP_W), F32)],
        compiler_params=_params(("arbitrary",)),
    )(dh1b, w_out, attn, gm, ga, gg)


def _gmlp_bwd(u, z, dgm, ln_g, ln_b, sgu_w, sgu_bt):
    T = u.shape[0]
    tg = TM_GMLP
    nsteps = T // tg

    def body(u_ref, z_ref, d_ref, g_ref, b_ref, w_ref, sb_ref, dproj_hbm, dlg_ref, dlb_ref, dw_ref, dsb_ref,
             stage, sem):
        i = pl.program_id(0)
        slot = i % 2
        duz_ref = stage.at[slot]

        def to_dproj(step, buf):
            rows = pl.ds(pl.multiple_of(step * tg, tg), tg)
            return pltpu.make_async_copy(stage.at[buf], dproj_hbm.at[rows, pl.ds(3 * ATTN_W, 2 * GMLP_W)],
                                         sem.at[buf])

        @pl.when(i == 0)
        def _():
            for ref in (dlg_ref, dlb_ref, dw_ref, dsb_ref):
                ref[...] = jnp.zeros_like(ref)

        @pl.when(i >= 2)
        def _():
            to_dproj(i - 2, slot).wait()

        grp = lax.broadcasted_iota(jnp.int32, (tg, GMLP_W), 1) // HEAD_DIM
        lane = lax.broadcasted_iota(jnp.int32, (CHUNK, LANES), 1)
        causal, ws = _causal_ws(w_ref)
        lg = g_ref[...]
        uu, zz, dgm = u_ref[...], z_ref[...], d_ref[...]
        ug, tu, tz, xhat, rstd, zn16, mixed = _gmlp_core(uu, zz, lg, b_ref[...], ws, sb_ref, grp)
        dmx = dgm * ug
        duz_ref[:, 0:GMLP_W] = dgm * mixed * _gelu_grad(uu, tu)
        dmx16 = dmx.astype(BF16)
        dzn = []
        for ci in range(tg // CHUNK):
            rows = slice(ci * CHUNK, (ci + 1) * CHUNK)
            dmx_c, d = dmx16[rows, :], jnp.zeros((CHUNK, GMLP_W), F32)
            for g in range(N_GROUPS):
                mk = grp[:CHUNK] == g
                d = jnp.where(mk, _dot(ws[g], dmx_c, TN), d)
                dw_ref[g] += _dot(jnp.where(mk, dmx_c, jnp.zeros_like(dmx_c)), zn16[rows, :], NT)
            dzn.append(d)
        dzn = jnp.concatenate(dzn, axis=0)
        dsb = jnp.zeros((CHUNK, LANES), F32)
        for g in range(N_GROUPS):
            per_token = jnp.sum(jnp.where(grp == g, dmx, 0.0), axis=-1, keepdims=True)
            by_position = sum(per_token[ci * CHUNK:(ci + 1) * CHUNK] for ci in range(tg // CHUNK))
            dsb = jnp.where(lane == g, by_position, dsb)
        dsb_ref[...] += dsb
        _accum_rows(dlg_ref, dzn * xhat)
        _accum_rows(dlb_ref, dzn)
        dxh = dzn * lg
        dzg = rstd * (dxh - _group_mean(dxh, grp) - xhat * _group_mean(dxh * xhat, grp))
        duz_ref[:, GMLP_W:] = dzg * _gelu_grad(zz, tz)
        to_dproj(i, slot).start()

        @pl.when(i == nsteps - 1)
        def _():
            for g in range(N_GROUPS):
                dw_ref[g] = jnp.where(causal, dw_ref[g], 0.0)
            to_dproj(i, slot).wait()
            if nsteps >= 2:
                to_dproj(i - 1, 1 - slot).wait()

    sds = jax.ShapeDtypeStruct
    return pl.pallas_call(
        body, name="gmlp_bwd", grid=(nsteps,),
        in_specs=[_rows(tg, GMLP_W)] * 3 + [_resident((1, GMLP_W)), _resident((1, GMLP_W)),
                                              _resident((N_GROUPS, CHUNK, CHUNK)), _resident((CHUNK, N_GROUPS))],
        out_specs=[_HBM, pl.BlockSpec((8, GMLP_W), lambda i: (0, 0)),
                   pl.BlockSpec((8, GMLP_W), lambda i: (0, 0)),
                   pl.BlockSpec((N_GROUPS, CHUNK, CHUNK), lambda i: (0, 0, 0)),
                   pl.BlockSpec((CHUNK, LANES), lambda i: (0, 0))],
        out_shape=[sds((T, IN_W), F32), sds((8, GMLP_W), F32), sds((8, GMLP_W), F32),
                   sds((N_GROUPS, CHUNK, CHUNK), F32), sds((CHUNK, LANES), F32)],
        scratch_shapes=[pltpu.VMEM((2, tg, 2 * GMLP_W), F32), pltpu.SemaphoreType.DMA((2,))],
        compiler_params=_params(("arbitrary",)),
    )(u, z, dgm, ln_g, ln_b, sgu_w, sgu_bt)


def _attn_bwd(q, k, v, dattn, attn, lse, dproj, owner_grads=()):
    T = q.shape[0]
    nt = T // ATT_TILE
    ns = len(owner_grads)
    steps = (ATTN_W // LANES) * nt

    def body(sl_ref, q_hbm, k_hbm, v_hbm, do_hbm, o_hbm, lse_hbm, _, *rest):
        p_refs, rest = rest[:ns], rest[ns:]
        dq_hbm = dk_hbm = dv_hbm = rest[0]
        r_refs, rest = rest[1:1 + ns], rest[1 + ns:]
        qbuf, dobuf, obuf, lbuf, kbuf, vbuf, dqbuf, dkbuf, dvbuf, delta_s = rest[:10]
        sem_q, sem_do, sem_o, sem_l, sem_k, sem_v, sem_dq, sem_dk, sem_dv = rest[10:19]
        hp, t = pl.program_id(0), pl.program_id(1)
        step = hp * nt + t
        two, three = step % 2, step % 3
        before, after = (step + 2) % 3, (step + 1) % 3
        if ns:
            start, finish = _owner_exchange_phases(p_refs, r_refs, *rest[19:])
            pl.when(step == 0)(start)

        def fetch(hp_, t_, two_, three_):
            for hbm, buf, sem, slot in ((q_hbm, qbuf, sem_q, two_), (do_hbm, dobuf, sem_do, two_),
                                        (o_hbm, obuf, sem_o, two_), (lse_hbm, lbuf, sem_l, two_),
                                        (k_hbm, kbuf, sem_k, three_), (v_hbm, vbuf, sem_v, three_)):
                for cp in _tile_copies(hbm, buf.at[slot], sem.at[slot], hp_, t_):
                    cp.start()

        @pl.when(step == 0)
        def _():
            kbuf[2] = jnp.zeros((ATT_BLOCKS, CHUNK, LANES), F32)
            vbuf[2] = jnp.zeros((ATT_BLOCKS, CHUNK, LANES), F32)
            dkbuf[3] = jnp.zeros((ATT_BLOCKS, CHUNK, LANES), F32)
            dvbuf[3] = jnp.zeros((ATT_BLOCKS, CHUNK, LANES), F32)
            fetch(0, 0, 0, 0)

        @pl.when(step + 1 < steps)
        def _():
            fetch((step + 1) // nt, (step + 1) % nt, 1 - two, after)

        for buf, sem in ((qbuf, sem_q), (dobuf, sem_do), (obuf, sem_o), (lbuf, sem_l)):
            _wait_tile(buf.at[two], sem.at[two])
        _wait_tile(kbuf.at[three], sem_k.at[three])
        _wait_tile(vbuf.at[three], sem_v.at[three])

        @pl.when(step >= 2)
        def _():
            _wait_tile(dqbuf.at[two], sem_dq.at[two])

        @pl.when(step >= 3)
        def _():
            _wait_tile(dkbuf.at[three], sem_dk.at[three])
            _wait_tile(dvbuf.at[three], sem_dv.at[three])

        zero_tile = jnp.zeros((ATT_BLOCKS, CHUNK, LANES), F32)
        dqbuf[two] = zero_tile
        dkbuf[three] = zero_tile
        dvbuf[three] = zero_tile

        q_t, do_t, l_t, k_t, v_t = qbuf.at[two], dobuf.at[two], lbuf.at[two], kbuf.at[three], vbuf.at[three]
        k_b, v_b = kbuf.at[before], vbuf.at[before]
        dq_t, dk_t, dv_t = dqbuf.at[two], dkbuf.at[three], dvbuf.at[three]
        dk_b, dv_b = dkbuf.at[before], dvbuf.at[before]
        sink = jnp.where(t > 0, before, 3)
        dk_sink, dv_sink = dkbuf.at[sink], dvbuf.at[sink]
        head0 = lax.broadcasted_iota(jnp.int32, (CHUNK, LANES), 1) < HEAD_DIM
        for r in range(ATT_BLOCKS):
            dd = dobuf[two, r] * obuf[two, r]
            d0 = jnp.sum(jnp.where(head0, dd, 0.0), axis=-1, keepdims=True)
            d1 = jnp.sum(jnp.where(head0, 0.0, dd), axis=-1, keepdims=True)
            delta_s[r] = jnp.where(head0, d0, d1)

        def column(xb):
            return jnp.concatenate([xb[:, 0:1], xb[:, HEAD_DIM:HEAD_DIM + 1]], axis=0)

        no_key_before = jnp.where(lax.broadcasted_iota(jnp.int32, (2 * CHUNK, 2 * CHUNK), 1) < CHUNK, NEG, 0.0)
        for d in DILATIONS:
            bias = _residue_bias(sl_ref, d)
            for j in range(ATT_BLOCKS):
                kcat = jnp.concatenate([_rm_block_before(k_t, k_b, d, j), _rm_block(k_t, d, j)], axis=0).astype(BF16)
                vcat = jnp.concatenate([_rm_block_before(v_t, v_b, d, j), _rm_block(v_t, d, j)], axis=0).astype(BF16)
                q2 = _stack_heads(_rm_block(q_t, d, j), head0)
                do2 = _stack_heads(_rm_block(do_t, d, j), head0)
                s = _dot(q2, kcat, NT) + bias
                if _first_in_tile(d, j):
                    s = s + jnp.where(t == 0, 1.0, 0.0) * no_key_before
                p = jnp.exp(s - column(_rm_block(l_t, d, j)))
                ds = (p * (_dot(do2, vcat, NT) - column(_rm_block(delta_s, d, j)))).astype(BF16)
                _rm_add(dq_t, _residue_rows(d, j), _unstack_heads(_dot(ds, kcat), head0))
                ck = _dot(ds, q2, TN)
                cv = _dot(p.astype(BF16), do2, TN)
                _rm_add(dk_t, _residue_rows(d, j), ck[CHUNK:, :])
                _rm_add(dv_t, _residue_rows(d, j), cv[CHUNK:, :])
                if _first_in_tile(d, j):
                    rows = [(r, CHUNK - n, n) for r, _, n in _residue_rows(d, j)]
                    _rm_add(dk_sink, rows, ck[:CHUNK, :])
                    _rm_add(dv_sink, rows, cv[:CHUNK, :])
                else:
                    rows = [(r, lo - n, n) for r, lo, n in _residue_rows(d, j)]
                    _rm_add(dk_t, rows, ck[:CHUNK, :])
                    _rm_add(dv_t, rows, cv[:CHUNK, :])

        for r in range(ATT_BLOCKS):
            dqbuf[two, r] = dqbuf[two, r] * Q_SCALE
        for cp in _tile_copies(dq_hbm, dq_t, sem_dq.at[two], hp, t, to_hbm=True):
            cp.start()

        @pl.when(t > 0)
        def _():
            for cp in (_tile_copies(dk_hbm, dk_b, sem_dk.at[before], hp, t - 1, to_hbm=True, lane0=ATTN_W)
                       + _tile_copies(dv_hbm, dv_b, sem_dv.at[before], hp, t - 1, to_hbm=True, lane0=2 * ATTN_W)):
                cp.start()

        @pl.when(t == nt - 1)
        def _():
            for cp in (_tile_copies(dk_hbm, dk_t, sem_dk.at[three], hp, t, to_hbm=True, lane0=ATTN_W)
                       + _tile_copies(dv_hbm, dv_t, sem_dv.at[three], hp, t, to_hbm=True, lane0=2 * ATTN_W)):
                cp.start()

        @pl.when(step == steps - 1)
        def _():
            for slot in range(2):
                _wait_tile(dqbuf.at[slot], sem_dq.at[slot])
            for slot in range(3):
                _wait_tile(dkbuf.at[slot], sem_dk.at[slot])
                _wait_tile(dvbuf.at[slot], sem_dv.at[slot])

        if ns:
            pl.when(step == steps - 1)(finish)

    tile = lambda n: pltpu.VMEM((n, ATT_BLOCKS, CHUNK, LANES), F32)
    dma = lambda n: pltpu.SemaphoreType.DMA((n,))
    view = jax.ShapeDtypeStruct((T // ATT_BLOCKS, ATT_BLOCKS, ATTN_W), F32)
    outs = pl.pallas_call(
        body, name="attn_bwd", grid=(ATTN_W // LANES, nt),
        in_specs=[pl.BlockSpec((8, LANES), lambda c, t: (0, c))] + [_HBM] * (7 + ns),
        out_specs=[_HBM] * (1 + ns),
        out_shape=[jax.ShapeDtypeStruct((T // ATT_BLOCKS, ATT_BLOCKS, IN_W), F32)]
        + [jax.ShapeDtypeStruct(p.shape, p.dtype) for p in owner_grads],
        scratch_shapes=[tile(2), tile(2), tile(2), tile(2), tile(3), tile(3), tile(2), tile(4), tile(4),
                        pltpu.VMEM((ATT_BLOCKS, CHUNK, LANES), F32)]
        + [dma(2), dma(2), dma(2), dma(2), dma(3), dma(3), dma(2), dma(3), dma(3)]
        + (_owner_exchange_sems(ns) if ns else []),
        input_output_aliases={7: 0},
        compiler_params=_params(("arbitrary", "arbitrary")),
    )(_slope_table(), *[_residue_view(a) for a in (q, k, v, dattn, attn, lse, dproj)], *owner_grads)
    return outs[0].reshape(T, IN_W), tuple(outs[1:])


def _proj_bwd(dproj, w_in_t, x, g1, dh1, chip_sums=()):
    T = x.shape[0]
    tm = TM_PROJ
    ns = len(chip_sums)
    steps = T // tm

    def body(d_ref, w_ref, x_ref, g_ref, r_ref, *rest):
        p_refs, rest = rest[:ns], rest[ns:]
        dx_ref, dg_ref = rest[:2]
        r_refs, sems = rest[2:2 + ns], rest[2 + ns:]
        step = pl.program_id(0)
        if ns:
            start, finish = _chip_exchange_phases(p_refs, r_refs, *sems)
            pl.when(step == 0)(start)

        @pl.when(step == 0)
        def _():
            dg_ref[...] = jnp.zeros_like(dg_ref)

        dhn = _dot(d_ref[...].astype(BF16), w_ref[...])
        n1, r1 = _rms(x_ref[...])
        _accum_rows(dg_ref, dhn * n1)
        dx_ref[...] = r_ref[...] + _rms_bwd(n1, r1, g_ref[...], dhn)
        if ns:
            pl.when(step == steps - 1)(finish)

    outs = pl.pallas_call(
        body, name="proj_bwd", grid=(steps,),
        in_specs=[_rows(tm, IN_W), _resident((IN_W, D_MODEL)), _rows(tm, D_MODEL), _resident((1, D_MODEL)),
                  _rows(tm, D_MODEL)] + [_HBM] * ns,
        out_specs=[_rows(tm, D_MODEL), pl.BlockSpec((8, D_MODEL), lambda i: (0, 0))] + [_HBM] * ns,
        out_shape=[jax.ShapeDtypeStruct((T, D_MODEL), F32), jax.ShapeDtypeStruct((8, D_MODEL), F32)]
        + [jax.ShapeDtypeStruct(p.shape, p.dtype) for p in chip_sums],
        scratch_shapes=_chip_exchange_sems(ns) if ns else [],
        compiler_params=_params(("arbitrary",)),
    )(dproj, w_in_t, x, g1, dh1, *chip_sums)
    return outs[0], outs[1], tuple(outs[2:])


def _dw(a, b, name, tile, square_a=False, out_dtype=F32):
    T, ka = a.shape
    nb = b.shape[1]
    tka, tnb, tt = tile
    tt = min(tt, T)
    last = T // tt - 1

    def body(a_ref, b_ref, *refs):
        o_ref = refs[0]
        acc_ref = refs[1] if len(refs) > 1 else o_ref
        s = pl.program_id(2)

        @pl.when(s == 0)
        def _():
            acc_ref[...] = jnp.zeros_like(acc_ref)

        a_tile = a_ref[...]
        if square_a:
            a_tile = jnp.square(a_tile.astype(F32))
        acc_ref[...] += _dot(a_tile.astype(BF16), b_ref[...], TN)
        if acc_ref is not o_ref:
            @pl.when(s == last)
            def _():
                o_ref[...] = acc_ref[...].astype(out_dtype)

    return pl.pallas_call(
        body, name=name, grid=(ka // tka, nb // tnb, T // tt),
        in_specs=[pl.BlockSpec((tt, tka), lambda i, j, s: (s, i)), pl.BlockSpec((tt, tnb), lambda i, j, s: (s, j))],
        out_specs=pl.BlockSpec((tka, tnb), lambda i, j, s: (i, j)),
        out_shape=jax.ShapeDtypeStruct((ka, nb), out_dtype),
        scratch_shapes=[] if out_dtype == F32 else [pltpu.VMEM((tka, tnb), F32)],
        compiler_params=_params(("parallel", "parallel", "arbitrary")),
    )(a, b)


def _adamw_update(w, m, v, g):
    m2 = ADAM_B1 * m + (1.0 - ADAM_B1) * g
    v2 = ADAM_B2 * v + (1.0 - ADAM_B2) * jnp.square(g)
    m_hat = m2 / (1.0 - ADAM_B1 ** ADAM_STEP)
    v_hat = v2 / (1.0 - ADAM_B2 ** ADAM_STEP)
    return -ADAM_LR * (m_hat / (jnp.sqrt(v_hat) + ADAM_EPS) + ADAM_WD * w), m2, v2


def _adamw_tiny(ws, ms, vs, parts):
    n = len(ws)
    P = parts.shape[0]

    def body(*refs):
        w_refs, m_refs, v_refs, p_ref = refs[:n], refs[n:2 * n], refs[2 * n:3 * n], refs[3 * n]
        outs = refs[3 * n + 1:]

        def total(slot, rows):
            g = p_ref[0, 8 * slot:8 * slot + rows, :]
            for i in range(1, P):
                g = g + p_ref[i, 8 * slot:8 * slot + rows, :]
            return g

        for k in range(n):
            g = total(k, ws[k].shape[0])
            outs[4 * k][...] = g
            outs[4 * k + 1][...], outs[4 * k + 2][...], outs[4 * k + 3][...] = _adamw_update(
                w_refs[k][...], m_refs[k][...], v_refs[k][...], g)
        outs[4 * n][...] = total(n, 8)

    sds = jax.ShapeDtypeStruct
    return pl.pallas_call(
        body, name="adamw_tiny",
        out_shape=[sds(w.shape, F32) for w in ws for _ in range(4)] + [sds((8, LANES), F32)],
    )(*ws, *ms, *vs, parts)


def _adamw(w, m, v, parts, name, tr, transposed=False):
    R, C = w.shape
    P = parts.shape[0]

    def body(w_ref, m_ref, v_ref, p_ref, g_ref, d_ref, m2_ref, v2_ref):
        g = p_ref[0].astype(F32)
        for i in range(1, P):
            g = g + p_ref[i].astype(F32)
        if transposed:
            g = g.T
        g_ref[...] = g
        d_ref[...], m2_ref[...], v2_ref[...] = _adamw_update(w_ref[...], m_ref[...], v_ref[...], g)

    spec = _rows(tr, C)
    part_spec = (pl.BlockSpec((P, C, tr), lambda i: (0, 0, i)) if transposed
                 else pl.BlockSpec((P, tr, C), lambda i: (0, i, 0)))
    return pl.pallas_call(
        body, name=name, grid=(R // tr,),
        in_specs=[spec, spec, spec, part_spec],
        out_specs=[spec] * 4,
        out_shape=[jax.ShapeDtypeStruct((R, C), F32)] * 4,
        compiler_params=_params(("parallel",)),
    )(w, m, v, parts)


def _pair_sum(core, grad, recv, name):
    _, _, n, C = grad.shape
    tr = n // 2

    def body(c_ref, a_ref, b_ref, o_ref):
        o_ref[...] = a_ref[...] + b_ref[...]

    spec = pl.BlockSpec((1, tr, C), lambda i, j, c_ref: (i, j, 0))
    return pl.pallas_call(
        body, name=name,
        grid_spec=pltpu.PrefetchScalarGridSpec(
            num_scalar_prefetch=1, grid=(4, n // tr),
            in_specs=[pl.BlockSpec((1, None, tr, C), lambda i, j, c_ref: (i, c_ref[0], j, 0)), spec],
            out_specs=spec),
        out_shape=jax.ShapeDtypeStruct(recv.shape, F32),
        compiler_params=_params(("parallel", "parallel")),
    )(core.reshape(1), grad, recv)


_HBM = pl.BlockSpec(memory_space=pltpu.HBM)


def _place():
    return lax.axis_index("x"), lax.axis_index("y"), lax.axis_index("c")


def _gathered_shape(shard):
    return jax.ShapeDtypeStruct((N_DEV,) + shard.shape, shard.dtype)


def _gather_sems(n):
    return [pltpu.SemaphoreType.DMA((7, n)), pltpu.SemaphoreType.DMA((7, n)), pltpu.SemaphoreType.DMA((n,))]


def _gather_phases(x_refs, out_refs, send_sems, recv_sems, local_sems):
    x, y, c = _place()
    me, sibling = (x, y, c), (x, y, 1 - c)
    chips = [(1 - x, y), (x, 1 - y), (1 - x, 1 - y)]
    arrays = range(len(x_refs))

    def slot(i, px, py, pc):
        return out_refs[i].at[4 * px + 2 * py + pc]

    def copy(i, k, block, to, own=False):
        return pltpu.make_async_remote_copy(
            src_ref=x_refs[i] if own else slot(i, *block), dst_ref=slot(i, *block),
            send_sem=send_sems.at[k, i], recv_sem=recv_sems.at[k, i], device_id=to, device_id_type=MESH)

    def mine(i):
        return pltpu.make_async_copy(x_refs[i], slot(i, *me), local_sems.at[i])

    def start():
        for i in arrays:
            mine(i).start()
            copy(i, 0, me, sibling, own=True).start()
            for j, chip in enumerate(chips):
                copy(i, 1 + j, me, (*chip, c), own=True).start()

    def forward():
        for i in arrays:
            for j, chip in enumerate(chips):
                copy(i, 1 + j, (*chip, c), me).wait_recv()
                copy(i, 4 + j, (*chip, c), sibling).start()

    def finish():
        for i in arrays:
            copy(i, 0, sibling, me).wait_recv()
            copy(i, 0, me, sibling, own=True).wait_send()
            for j, chip in enumerate(chips):
                copy(i, 4 + j, (*chip, 1 - c), me).wait_recv()
                copy(i, 1 + j, me, (*chip, c), own=True).wait_send()
                copy(i, 4 + j, (*chip, c), sibling).wait_send()
            mine(i).wait()

    return start, forward, finish


def _all_gather(shards, name):
    n = len(shards)

    def body(*refs):
        start, forward, finish = _gather_phases(refs[:n], refs[n:2 * n], *refs[2 * n:])
        start()
        forward()
        finish()

    return pl.pallas_call(
        body, name=name,
        out_shape=[_gathered_shape(s) for s in shards],
        in_specs=[_HBM] * n, out_specs=[_HBM] * n,
        scratch_shapes=_gather_sems(n),
    )(*shards)


def _sibling_exchange(grads, name):
    n = len(grads)

    def body(*refs):
        g_refs, r_refs, send_sems, recv_sems = refs[:n], refs[n:2 * n], refs[2 * n], refs[2 * n + 1]
        x, y, c = _place()
        copies = [pltpu.make_async_remote_copy(
            src_ref=g_refs[i].at[:, 1 - c], dst_ref=r_refs[i], send_sem=send_sems.at[i], recv_sem=recv_sems.at[i],
            device_id=(x, y, 1 - c), device_id_type=MESH) for i in range(n)]
        for cp in copies:
            cp.start()
        for cp in copies:
            cp.wait()

    return pl.pallas_call(
        body, name=name,
        out_shape=[jax.ShapeDtypeStruct((g.shape[0],) + g.shape[2:], g.dtype) for g in grads],
        in_specs=[_HBM] * n, out_specs=[_HBM] * n,
        scratch_shapes=[pltpu.SemaphoreType.DMA((n,)), pltpu.SemaphoreType.DMA((n,))],
    )(*grads)


def _owner_exchange_sems(n):
    return [pltpu.SemaphoreType.DMA((7, n)), pltpu.SemaphoreType.DMA((7, n)), pltpu.SemaphoreType.DMA((n,))]


def _owner_exchange_phases(g_refs, r_refs, send_sems, recv_sems, local_sems):
    x, y, c = _place()
    me = 4 * x + 2 * y + c
    flip = lambda v, bit: 1 - v if bit else v
    peers = [(flip(x, k & 4), flip(y, k & 2), flip(c, k & 1)) for k in range(1, N_DEV)]
    arrays = range(len(g_refs))

    def mine(i):
        return pltpu.make_async_copy(g_refs[i].at[me], r_refs[i].at[me], local_sems.at[i])

    def copy(i, k, src_slot, dst_slot):
        return pltpu.make_async_remote_copy(
            src_ref=g_refs[i].at[src_slot], dst_ref=r_refs[i].at[dst_slot],
            send_sem=send_sems.at[k, i], recv_sem=recv_sems.at[k, i], device_id=peers[k], device_id_type=MESH)

    def start():
        for i in arrays:
            mine(i).start()
            for k, (px, py, pc) in enumerate(peers):
                copy(i, k, 4 * px + 2 * py + pc, me).start()

    def finish():
        for i in arrays:
            for k, (px, py, pc) in enumerate(peers):
                copy(i, k, me, 4 * px + 2 * py + pc).wait_recv()
                copy(i, k, 4 * px + 2 * py + pc, me).wait_send()
            mine(i).wait()

    return start, finish


def _chip_exchange_sems(n):
    return [pltpu.SemaphoreType.DMA((3, n)), pltpu.SemaphoreType.DMA((3, n)), pltpu.SemaphoreType.DMA((n,))]


def _chip_exchange_phases(p_refs, r_refs, send_sems, recv_sems, local_sems):
    x, y, c = _place()
    my_chip = 2 * x + y
    chips = [(1 - x, y), (x, 1 - y), (1 - x, 1 - y)]
    arrays = range(len(p_refs))

    def mine(i):
        return pltpu.make_async_copy(p_refs[i].at[my_chip], r_refs[i].at[my_chip], local_sems.at[i])

    def copy(i, k, src_chip, dst_chip):
        px, py = chips[k]
        return pltpu.make_async_remote_copy(
            src_ref=p_refs[i].at[src_chip], dst_ref=r_refs[i].at[dst_chip],
            send_sem=send_sems.at[k, i], recv_sem=recv_sems.at[k, i], device_id=(px, py, c), device_id_type=MESH)

    def start():
        for i in arrays:
            mine(i).start()
            for k, (px, py) in enumerate(chips):
                copy(i, k, 2 * px + py, my_chip).start()

    def finish():
        for i in arrays:
            for k, (px, py) in enumerate(chips):
                copy(i, k, my_chip, 2 * px + py).wait_recv()
                copy(i, k, 2 * px + py, my_chip).wait_send()
            mine(i).wait()

    return start, finish


_R_IN, _R_OUT, _R_FF = IN_W // N_DEV, D_MODEL // N_DEV, D_FF // N_DEV


def _by_owner(g):
    return g.reshape(4, 2, g.shape[0] // N_DEV, D_MODEL)


def _local_step(x, tgt, small, w_in_t, rest, core=None):
    exchange = core is not None
    g1, g2, gf = small["norm1_g"], small["norm2_g"], small["final_norm_g"].reshape(1, D_MODEL)
    ga, gg = small["attn_out_g"], small["gmlp_out_g"]
    ln_g = small["sgu_ln_g"].reshape(1, GMLP_W)
    ln_b = small["sgu_ln_b"].reshape(1, GMLP_W)
    sgu_w = small["sgu_w"][0]
    sgu_bt = small["sgu_b"][0].T

    hn1, q, k, v, u, z = _proj_fwd(x, g1, w_in_t)
    attn, lse, gathered = _attn_fwd(q, k, v, shards=rest if exchange else ())
    w_out, w_ff1_t, w_ff2 = [g.reshape(-1, D_MODEL) for g in gathered] if exchange else rest
    gm = _gmlp_fwd(u, z, ln_g, ln_b, sgu_w, sgu_bt)
    mixed, h1, hn2 = _out_fwd(attn, gm, ga, gg, w_out, x, g2)
    relu, dh2f, dh2b, loss8, dgf8 = _ffn_fwd(hn2, h1, w_ff1_t, w_ff2, gf, tgt)

    da, dh1f, dh1b, dg2 = _ffn_bwd(dh2b, dh2f, relu, h1, g2, w_ff2, w_ff1_t)
    wire = BF16 if exchange else F32
    dw_ff2 = _dw(relu, dh2b, "dw_ff2", DW_TILE, square_a=True, out_dtype=wire)
    dw_ff1_t = _dw(da, hn2, "dw_ff1", DW_TILE, out_dtype=wire)
    dattn, dgm, dga, dgg = _out_bwd(dh1b, w_out, attn, gm, ga, gg)
    dw_out = _dw(mixed, dh1b, "dw_out", DW_TILE, out_dtype=wire)
    early = [dw_out, dw_ff1_t, dw_ff2]
    if exchange:
        early = [g.reshape(N_DEV, -1, D_MODEL) for g in early]
    dproj, dlg, dlb, dsw, dsb = _gmlp_bwd(u, z, dgm, ln_g, ln_b, sgu_w, sgu_bt)
    dproj, arrived = _attn_bwd(q, k, v, dattn, attn, lse, dproj, owner_grads=early if exchange else ())
    dw_in_t = _dw(dproj, hn1, "dw_in", DW_TILE_IN)
    late = ()
    if exchange:
        by_owner = _by_owner(dw_in_t)
        got, = _sibling_exchange([by_owner], "grad_sibling_exchange")
        late = (_pair_sum(core, by_owner, got, "grad_pair_sum"),)
    dx, dg1, late = _proj_bwd(dproj, w_in_t, x, g1, dh1f, chip_sums=late)
    if exchange:
        dw_in_t, early = late[0], arrived

    small_grads = dict(
        norm1_g=dg1[0], sgu_ln_g=dlg[0], sgu_ln_b=dlb[0], sgu_w=dsw, sgu_b=dsb[:, :N_GROUPS].T,
        attn_out_g=dga[0], gmlp_out_g=dgg[0], norm2_g=dg2[0], final_norm_g=dgf8[0])
    return loss8[0, 0], dx, (dw_in_t, *early), small_grads


SMALL_NAMES = ("norm1_g", "sgu_ln_g", "sgu_ln_b", "sgu_w", "sgu_b", "attn_out_g", "gmlp_out_g", "norm2_g",
               "final_norm_g")
WEIGHT_ORDER = ("norm1_g", "w_in", "sgu_ln_g", "sgu_ln_b", "sgu_w", "sgu_b", "attn_out_g", "gmlp_out_g", "w_out",
                "norm2_g", "w_ff1", "w_ff2", "final_norm_g")


TINY_NAMES = tuple(n for n in SMALL_NAMES if n != "sgu_w")


def _as_rows(a):
    return a.reshape(-1, LANES)


def _pack_tiny_grads(d, loss):
    slots = [jnp.pad(_as_rows(d[n]), ((0, 8 - d[n].size // LANES), (0, 0))) for n in TINY_NAMES]
    return jnp.concatenate(slots + [jnp.full((8, LANES), loss, F32)], axis=0)


def kernel(x, norm1_g, w_in, sgu_ln_g, sgu_ln_b, sgu_w, sgu_b, attn_out_g, gmlp_out_g, w_out, norm2_g, w_ff1, w_ff2, final_norm_g, loss_target, m_norm1_g, m_w_in, m_sgu_ln_g, m_sgu_ln_b, m_sgu_w, m_sgu_b, m_attn_out_g, m_gmlp_out_g, m_w_out, m_norm2_g, m_w_ff1, m_w_ff2, m_final_norm_g, v_norm1_g, v_w_in, v_sgu_ln_g, v_sgu_ln_b, v_sgu_w, v_sgu_b, v_attn_out_g, v_gmlp_out_g, v_w_out, v_norm2_g, v_w_ff1, v_w_ff2, v_final_norm_g):
    w = dict(norm1_g=norm1_g, w_in=w_in, sgu_ln_g=sgu_ln_g, sgu_ln_b=sgu_ln_b, sgu_w=sgu_w, sgu_b=sgu_b,
             attn_out_g=attn_out_g, gmlp_out_g=gmlp_out_g, w_out=w_out, norm2_g=norm2_g, w_ff1=w_ff1, w_ff2=w_ff2,
             final_norm_g=final_norm_g)
    m = dict(norm1_g=m_norm1_g, w_in=m_w_in, sgu_ln_g=m_sgu_ln_g, sgu_ln_b=m_sgu_ln_b, sgu_w=m_sgu_w, sgu_b=m_sgu_b,
             attn_out_g=m_attn_out_g, gmlp_out_g=m_gmlp_out_g, w_out=m_w_out, norm2_g=m_norm2_g, w_ff1=m_w_ff1,
             w_ff2=m_w_ff2, final_norm_g=m_final_norm_g)
    v = dict(norm1_g=v_norm1_g, w_in=v_w_in, sgu_ln_g=v_sgu_ln_g, sgu_ln_b=v_sgu_ln_b, sgu_w=v_sgu_w, sgu_b=v_sgu_b,
             attn_out_g=v_attn_out_g, gmlp_out_g=v_gmlp_out_g, w_out=v_w_out, norm2_g=v_norm2_g, w_ff1=v_w_ff1,
             w_ff2=v_w_ff2, final_norm_g=v_final_norm_g)
    big = ("w_in", "w_out", "w_ff1", "w_ff2")
    core = lax.axis_index("c")

    w_in_t, = _all_gather([w_in[0].T.astype(BF16)], "w_in_all_gather")
    rest = (w_out[0].astype(BF16), w_ff1[0].T.astype(BF16), w_ff2[0].astype(BF16))
    loss, dx, parts, small_grads = _local_step(x[0], loss_target[0], {n: w[n] for n in SMALL_NAMES},
                                               w_in_t.reshape(IN_W, D_MODEL), rest, core=core)

    new = {}
    for n, p, transposed, tr in zip(big, parts, (True, False, True, False), (128, 128, 128, 256)):
        new[n] = [a[None] for a in _adamw(w[n][0], m[n][0], v[n][0], p, "adamw_" + n, tr, transposed)]

    tiny_parts, sgu_parts = _all_gather(
        [_pack_tiny_grads(small_grads, loss), _as_rows(small_grads["sgu_w"])], "small_grad_all_gather")
    tiny = _adamw_tiny(*[[_as_rows(src[n]) for n in TINY_NAMES] for src in (w, m, v)], tiny_parts)
    sgu = _adamw(_as_rows(sgu_w), _as_rows(m_sgu_w), _as_rows(v_sgu_w), sgu_parts, "adamw_sgu_w", 512)
    loss = tiny[-1][0, 0]

    outs = []
    for i in range(4):
        d = {n: new[n][i] for n in big}
        d.update({n: tiny[4 * k + i].reshape(w[n].shape) for k, n in enumerate(TINY_NAMES)})
        d["sgu_w"] = sgu[i].reshape(sgu_w.shape)
        outs.extend(d[n] for n in WEIGHT_ORDER)
    return (loss, dx[None], *outs)
```

```python
import functools
import math

import numpy as np
import jax
import jax.numpy as jnp
from jax import lax
from jax.experimental import pallas as pl
from jax.experimental.pallas import tpu as pltpu

F32 = jnp.float32
BF16 = jnp.bfloat16

D_MODEL = 1024
HEAD_DIM = 64
N_HEADS = 12
ATTN_W = N_HEADS * HEAD_DIM
N_GROUPS = 4
GMLP_W = N_GROUPS * HEAD_DIM
IN_W = 3 * ATTN_W + 2 * GMLP_W
D_FF = 4 * D_MODEL
CHUNK = 128
DILATIONS = (1, 4, 16)
EPS = 1e-6
Q_SCALE = HEAD_DIM ** -0.5
NEG = -1e30

ADAM_LR, ADAM_B1, ADAM_B2, ADAM_EPS, ADAM_WD, ADAM_STEP = 0.001, 0.9, 0.999, 1e-08, 0.01, 10

N_DEV = 8
LANES = 128
VMEM_LIMIT = 56 << 20

TM_PROJ = 512
TM_FFN = 512
FF_CHUNK = 512
TM_GMLP = 1024
DW_TILE = (512, 1024, 4096)
DW_TILE_IN = (IN_W // 2, 1024, 2048)

MESH = pl.DeviceIdType.MESH


def _alibi_slopes(n):
    def pow2(m):
        start = 2.0 ** (-8.0 / m)
        return [start ** (i + 1) for i in range(m)]
    c = 2 ** int(math.floor(math.log2(n)))
    s = pow2(n) if c == n else pow2(c) + pow2(2 * c)[0::2][: n - c]
    return np.asarray(s, dtype=np.float32)


SLOPES = _alibi_slopes(N_HEADS)


def _params(sem=None):
    kw = dict(vmem_limit_bytes=VMEM_LIMIT)
    if sem is not None:
        kw["dimension_semantics"] = sem
    return pltpu.CompilerParams(**kw)


def _rows(tm, n):
    return pl.BlockSpec((tm, n), lambda i: (i, 0))


def _resident(shape):
    return pl.BlockSpec(shape, lambda *_: (0,) * len(shape), pipeline_mode=pl.Buffered(1))


def _rms(x):
    r = lax.rsqrt(jnp.mean(x * x, axis=-1, keepdims=True) + EPS)
    return x * r, r


def _rms_bwd(n, r, g, dy):
    dn = dy * g
    return r * (dn - n * jnp.mean(dn * n, axis=-1, keepdims=True))


def _accum_rows(acc_ref, v):
    acc_ref[...] += jnp.broadcast_to(jnp.sum(v, axis=0, keepdims=True), acc_ref.shape)


_G0 = math.sqrt(2.0 / math.pi)
_G1 = 0.044715


def _gelu(x):
    t = jnp.tanh(_G0 * (x + _G1 * (x * x * x)))
    return x * (0.5 * (1.0 + t)), t


def _gelu_grad(x, t):
    return 0.5 * (1.0 + t) + 0.5 * x * (1.0 - t * t) * (_G0 * (1.0 + 3.0 * _G1 * x * x))


NT = (((1,), (1,)), ((), ()))
TN = (((0,), (0,)), ((), ()))


def _dot(a, b, dims=None):
    if dims is None:
        return jnp.dot(a, b, preferred_element_type=F32)
    return lax.dot_general(a, b, dims, preferred_element_type=F32)


def _norm1(x, g1, shards=()):
    T = x.shape[0]
    tm = TM_PROJ
    ns = len(shards)
    steps = T // tm

    def body(x_ref, g_ref, *rest):
        x_refs, hn_ref, g_refs, sems = rest[:ns], rest[ns], rest[ns + 1:2 * ns + 1], rest[2 * ns + 1:]
        step = pl.program_id(0)
        if ns:
            start, forward, finish = _gather_phases(x_refs, g_refs, *sems)
            pl.when(step == 0)(start)
            pl.when(step == steps // 2)(forward)
        n, _ = _rms(x_ref[...])
        hn_ref[...] = (n * g_ref[...]).astype(BF16)
        if ns:
            pl.when(step == steps - 1)(finish)

    outs = pl.pallas_call(
        body, name="norm1", grid=(steps,),
        in_specs=[_rows(tm, D_MODEL), _resident((1, D_MODEL))] + [_HBM] * ns,
        out_specs=[_rows(tm, D_MODEL)] + [_HBM] * ns,
        out_shape=[jax.ShapeDtypeStruct((T, D_MODEL), BF16)] + [_gathered_shape(s) for s in shards],
        scratch_shapes=_gather_sems(ns) if ns else [],
        compiler_params=_params(("arbitrary",)),
    )(x, g1, *shards)
    return outs[0], tuple(outs[1:])


def _proj_fwd(hn1, w_in_t):
    T = hn1.shape[0]
    tm = TM_PROJ

    def body(hn_ref, w_ref, q_ref, k_ref, v_ref, u_ref, z_ref):
        hn = hn_ref[...]
        a = ATTN_W
        q_ref[...] = _dot(hn, w_ref[0:a, :], NT) * Q_SCALE
        k_ref[...] = _dot(hn, w_ref[a:2 * a, :], NT)
        v_ref[...] = _dot(hn, w_ref[2 * a:3 * a, :], NT)
        u_ref[...] = _dot(hn, w_ref[3 * a:3 * a + GMLP_W, :], NT)
        z_ref[...] = _dot(hn, w_ref[3 * a + GMLP_W:, :], NT)

    sds = jax.ShapeDtypeStruct
    return pl.pallas_call(
        body, name="proj_fwd", grid=(T // tm,),
        in_specs=[_rows(tm, D_MODEL), _resident((IN_W, D_MODEL))],
        out_specs=[_rows(tm, ATTN_W), _rows(tm, ATTN_W), _rows(tm, ATTN_W), _rows(tm, GMLP_W), _rows(tm, GMLP_W)],
        out_shape=[sds((T, ATTN_W), F32), sds((T, ATTN_W), F32), sds((T, ATTN_W), F32), sds((T, GMLP_W), F32),
                   sds((T, GMLP_W), F32)],
        compiler_params=_params(("parallel",)),
    )(hn1, w_in_t)


ATT_TILE = 2048
ATT_BLOCKS = ATT_TILE // CHUNK
SM_BLOCKS = 4


def _slope_table():
    row = np.repeat(SLOPES, HEAD_DIM)
    return jnp.asarray(np.broadcast_to(row[None], (8, ATTN_W)), F32)


def _residue_view(a):
    return a.reshape(a.shape[0] // ATT_BLOCKS, ATT_BLOCKS, a.shape[1])


def _tile_copies(hbm, buf, sem, hp, t, to_hbm=False, lane0=0):
    rows = pl.ds(pl.multiple_of(t * CHUNK, CHUNK), CHUNK)
    lanes = pl.ds(pl.multiple_of(lane0 + hp * LANES, LANES), LANES)
    pairs = [(hbm.at[rows, r, lanes], buf.at[r]) for r in range(ATT_BLOCKS)]
    return [pltpu.make_async_copy(v, h, sem) if to_hbm else pltpu.make_async_copy(h, v, sem) for h, v in pairs]


def _wait_tile(buf, sem):
    pltpu.make_async_copy(buf, buf, sem).wait()


def _residue_rows(d, j):
    if d == 16:
        return [(j, 0, CHUNK)]
    if d == 4:
        return [(j % 4 + 4 * m, 32 * (j // 4), 32) for m in range(4)]
    return [(r, 8 * j, 8) for r in range(ATT_BLOCKS)]


def _block_order(p, d):
    if d == 16:
        return p
    if d == 4:
        return 4 * (p & 31) + (p >> 5)
    return 16 * (p & 7) + (p >> 3)


def _first_in_tile(d, j):
    return _residue_rows(d, j)[0][1] == 0


def _rm_block(buf, d, j):
    return jnp.concatenate([buf[r, lo:lo + n, :] for r, lo, n in _residue_rows(d, j)], axis=0)


def _rm_block_before(buf, buf_before, d, j):
    if _first_in_tile(d, j):
        return jnp.concatenate([buf_before[r, CHUNK - n:CHUNK, :] for r, _, n in _residue_rows(d, j)], axis=0)
    return jnp.concatenate([buf[r, lo - n:lo, :] for r, lo, n in _residue_rows(d, j)], axis=0)


def _rm_store(buf, d, j, val):
    at = 0
    for r, lo, n in _residue_rows(d, j):
        buf[r, lo:lo + n, :] = val[at:at + n, :]
        at += n


def _rm_add(buf, rows, val):
    at = 0
    for r, lo, n in rows:
        buf[r, lo:lo + n, :] += val[at:at + n, :]
        at += n


def _residue_bias(sl_ref, d):
    shape = (2 * CHUNK, 2 * CHUNK)
    row = lax.broadcasted_iota(jnp.int32, shape, 0)
    col = lax.broadcasted_iota(jnp.int32, shape, 1)
    steps = _block_order(row & (CHUNK - 1), d) + CHUNK - (_block_order(col & (CHUNK - 1), d) + (col & CHUNK))
    band = (steps >= 0) & (steps <= CHUNK)
    sl = sl_ref[0:1, :]
    upper = lax.broadcasted_iota(jnp.int32, (2 * CHUNK, 1), 0) < CHUNK
    slope2 = jnp.where(upper, sl[:, 0:1], sl[:, HEAD_DIM:HEAD_DIM + 1])
    return jnp.where(band, -(float(d) * slope2 * steps.astype(F32)), NEG)


def _stack_heads(xb, head0):
    zero = jnp.zeros_like(xb)
    return jnp.concatenate([jnp.where(head0, xb, zero), jnp.where(head0, zero, xb)], axis=0).astype(BF16)


def _unstack_heads(x2, head0):
    return jnp.where(head0, x2[:CHUNK, :], x2[CHUNK:, :])


def _attn_fwd(q, k, v, shards=()):
    T = q.shape[0]
    nt = T // ATT_TILE
    ns = len(shards)
    steps = (ATTN_W // LANES) * nt

    def body(sl_ref, q_hbm, k_hbm, v_hbm, *rest):
        x_refs, rest = rest[:ns], rest[ns:]
        attn_hbm, lse_hbm = rest[:2]
        g_refs, rest = rest[2:2 + ns], rest[2 + ns:]
        qbuf, kbuf, vbuf, obuf, lbuf = rest[:5]
        o_acc, l_acc = rest[5:8], rest[8:11]
        sem_q, sem_k, sem_v, sem_o, sem_l = rest[11:16]
        hp, t = pl.program_id(0), pl.program_id(1)
        step = hp * nt + t
        two, three = step % 2, step % 3
        before, after = (step + 2) % 3, (step + 1) % 3
        if ns:
            start, forward, finish = _gather_phases(x_refs, g_refs, *rest[16:])
            pl.when(step == 0)(start)
            pl.when(step == steps // 2)(forward)

        def fetch(hp_, t_, two_, three_):
            for cp in (_tile_copies(q_hbm, qbuf.at[two_], sem_q.at[two_], hp_, t_)
                       + _tile_copies(k_hbm, kbuf.at[three_], sem_k.at[three_], hp_, t_)
                       + _tile_copies(v_hbm, vbuf.at[three_], sem_v.at[three_], hp_, t_)):
                cp.start()

        @pl.when(step == 0)
        def _():
            kbuf[2] = jnp.zeros((ATT_BLOCKS, CHUNK, LANES), F32)
            vbuf[2] = jnp.zeros((ATT_BLOCKS, CHUNK, LANES), F32)
            fetch(0, 0, 0, 0)

        @pl.when(step + 1 < steps)
        def _():
            fetch((step + 1) // nt, (step + 1) % nt, 1 - two, after)

        _wait_tile(qbuf.at[two], sem_q.at[two])
        _wait_tile(kbuf.at[three], sem_k.at[three])
        _wait_tile(vbuf.at[three], sem_v.at[three])

        @pl.when(step >= 2)
        def _():
            _wait_tile(obuf.at[two], sem_o.at[two])
            _wait_tile(lbuf.at[two], sem_l.at[two])

        q_t, k_t, v_t = qbuf.at[two], kbuf.at[three], vbuf.at[three]
        k_b, v_b = kbuf.at[before], vbuf.at[before]
        head0 = lax.broadcasted_iota(jnp.int32, (CHUNK, LANES), 1) < HEAD_DIM
        no_key_before = jnp.where(lax.broadcasted_iota(jnp.int32, (2 * CHUNK, 2 * CHUNK), 1) < CHUNK, NEG, 0.0)
        for pi, d in enumerate(DILATIONS):
            bias = _residue_bias(sl_ref, d)

            def scores(j, d=d, bias=bias):
                kcat = jnp.concatenate([_rm_block_before(k_t, k_b, d, j), _rm_block(k_t, d, j)], axis=0).astype(BF16)
                vcat = jnp.concatenate([_rm_block_before(v_t, v_b, d, j), _rm_block(v_t, d, j)], axis=0).astype(BF16)
                s = _dot(_stack_heads(_rm_block(q_t, d, j), head0), kcat, NT) + bias
                if _first_in_tile(d, j):
                    s = s + jnp.where(t == 0, 1.0, 0.0) * no_key_before
                return s, vcat

            def output(j, p, vcat, scale, lse, d=d, pi=pi):
                _rm_store(o_acc[pi], d, j, _unstack_heads(_dot(p, vcat) * scale, head0))
                _rm_store(l_acc[pi], d, j, _unstack_heads(jnp.broadcast_to(lse, (2 * CHUNK, LANES)), head0))

            for j0 in range(0, ATT_BLOCKS, SM_BLOCKS):
                group = [scores(j) for j in range(j0, j0 + SM_BLOCKS)]
                s = jnp.concatenate([g[0] for g in group], axis=0)
                m = jnp.max(s, axis=-1, keepdims=True)
                p = jnp.exp(s - m)
                l = jnp.sum(p, axis=-1, keepdims=True)
                p, scale, lse = p.astype(BF16), 1.0 / l, m + jnp.log(l)
                for i, (_, vcat) in enumerate(group):
                    rows = slice(i * 2 * CHUNK, (i + 1) * 2 * CHUNK)
                    output(j0 + i, p[rows, :], vcat, scale[rows, :], lse[rows, :])

        for r in range(ATT_BLOCKS):
            a, b, c = l_acc[0][r], l_acc[1][r], l_acc[2][r]
            m = jnp.maximum(jnp.maximum(a, b), c)
            ea, eb, ec = jnp.exp(a - m), jnp.exp(b - m), jnp.exp(c - m)
            tot = ea + eb + ec
            obuf[two, r] = (ea * o_acc[0][r] + eb * o_acc[1][r] + ec * o_acc[2][r]) / tot
            lbuf[two, r] = m + jnp.log(tot)

        for cp in (_tile_copies(attn_hbm, obuf.at[two], sem_o.at[two], hp, t, to_hbm=True)
                   + _tile_copies(lse_hbm, lbuf.at[two], sem_l.at[two], hp, t, to_hbm=True)):
            cp.start()

        @pl.when(step == steps - 1)
        def _():
            for slot in (two, 1 - two)[:min(steps, 2)]:
                _wait_tile(obuf.at[slot], sem_o.at[slot])
                _wait_tile(lbuf.at[slot], sem_l.at[slot])

        if ns:
            pl.when(step == steps - 1)(finish)

    tile = lambda n: pltpu.VMEM((n, ATT_BLOCKS, CHUNK, LANES), F32)
    dma = lambda n: pltpu.SemaphoreType.DMA((n,))
    view = jax.ShapeDtypeStruct((T // ATT_BLOCKS, ATT_BLOCKS, ATTN_W), F32)
    outs = pl.pallas_call(
        body, name="attn_fwd", grid=(ATTN_W // LANES, nt),
        in_specs=[pl.BlockSpec((8, LANES), lambda c, t: (0, c))] + [_HBM] * (3 + ns),
        out_specs=[_HBM] * (2 + ns),
        out_shape=[view, view] + [_gathered_shape(s) for s in shards],
        scratch_shapes=[tile(2), tile(3), tile(3), tile(2), tile(2)] + [pltpu.VMEM((ATT_BLOCKS, CHUNK, LANES), F32)] * 6
        + [dma(2), dma(3), dma(3), dma(2), dma(2)] + (_gather_sems(ns) if ns else []),
        compiler_params=_params(("arbitrary", "arbitrary")),
    )(_slope_table(), _residue_view(q), _residue_view(k), _residue_view(v), *shards)
    return outs[0].reshape(T, ATTN_W), outs[1].reshape(T, ATTN_W), tuple(outs[2:])


def _group_mean(v, grp):
    out = jnp.zeros_like(v)
    for g in range(N_GROUPS):
        mk = grp == g
        s = jnp.sum(jnp.where(mk, v, 0.0), axis=-1, keepdims=True) * (1.0 / HEAD_DIM)
        out = jnp.where(mk, s, out)
    return out


def _gmlp_core(uu, zz, lg, lb, ws, sb_ref, grp):
    ug, tu = _gelu(uu)
    zg, tz = _gelu(zz)
    zc = zg - _group_mean(zg, grp)
    rstd = lax.rsqrt(_group_mean(zc * zc, grp) + EPS)
    xhat = zc * rstd
    zn16 = (xhat * lg + lb).astype(BF16)
    mixed = []
    for ci in range(uu.shape[0] // CHUNK):
        rows = slice(ci * CHUNK, (ci + 1) * CHUNK)
        m = jnp.zeros((CHUNK, GMLP_W), F32)
        for g in range(N_GROUPS):
            m = jnp.where(grp[:CHUNK] == g, _dot(ws[g], zn16[rows, :]) + sb_ref[:, g:g + 1], m)
        mixed.append(m)
    return ug, tu, tz, xhat, rstd, zn16, jnp.concatenate(mixed, axis=0)


def _causal_ws(w_ref):
    ti = lax.broadcasted_iota(jnp.int32, (CHUNK, CHUNK), 0)
    si = lax.broadcasted_iota(jnp.int32, (CHUNK, CHUNK), 1)
    causal = si <= ti
    return causal, [jnp.where(causal, w_ref[g], 0.0).astype(BF16) for g in range(N_GROUPS)]


def _gmlp_fwd(u, z, ln_g, ln_b, sgu_w, sgu_bt):
    T = u.shape[0]
    tg = TM_GMLP

    def body(u_ref, z_ref, g_ref, b_ref, w_ref, sb_ref, out_ref):
        grp = lax.broadcasted_iota(jnp.int32, (tg, GMLP_W), 1) // HEAD_DIM
        _, ws = _causal_ws(w_ref)
        ug, _, _, _, _, _, mixed = _gmlp_core(u_ref[...], z_ref[...], g_ref[...], b_ref[...], ws, sb_ref, grp)
        out_ref[...] = ug * mixed

    return pl.pallas_call(
        body, name="gmlp_fwd", grid=(T // tg,),
        in_specs=[_rows(tg, GMLP_W), _rows(tg, GMLP_W), _resident((1, GMLP_W)), _resident((1, GMLP_W)),
                  _resident((N_GROUPS, CHUNK, CHUNK)), _resident((CHUNK, N_GROUPS))],
        out_specs=_rows(tg, GMLP_W),
        out_shape=jax.ShapeDtypeStruct((T, GMLP_W), F32),
        compiler_params=_params(("parallel",)),
    )(u, z, ln_g, ln_b, sgu_w, sgu_bt)


def _out_fwd(attn, gm, ga, gg, w_out, x, g2):
    T = x.shape[0]
    tm = TM_PROJ

    def body(a_ref, m_ref, ga_ref, gg_ref, w_ref, x_ref, g2_ref, mix_ref, h1_ref, hn2_ref):
        an, _ = _rms(a_ref[...])
        gn, _ = _rms(m_ref[...])
        an = (an * ga_ref[...]).astype(BF16)
        gn = (gn * gg_ref[...]).astype(BF16)
        mix_ref[:, 0:ATTN_W] = an
        mix_ref[:, ATTN_W:] = gn
        h1 = x_ref[...] + _dot(an, w_ref[0:ATTN_W, :]) + _dot(gn, w_ref[ATTN_W:, :])
        h1_ref[...] = h1
        n2, _ = _rms(h1)
        hn2_ref[...] = (n2 * g2_ref[...]).astype(BF16)

    sds = jax.ShapeDtypeStruct
    return pl.pallas_call(
        body, name="out_fwd", grid=(T // tm,),
        in_specs=[_rows(tm, ATTN_W), _rows(tm, GMLP_W), _resident((1, ATTN_W)), _resident((1, GMLP_W)),
                  _resident((D_MODEL, D_MODEL)), _rows(tm, D_MODEL), _resident((1, D_MODEL))],
        out_specs=[_rows(tm, D_MODEL)] * 3,
        out_shape=[sds((T, D_MODEL), BF16), sds((T, D_MODEL), F32), sds((T, D_MODEL), BF16)],
        compiler_params=_params(("parallel",)),
    )(attn, gm, ga, gg, w_out, x, g2)


def _ffn_fwd(hn2, h1, w1t, w2, gf, tgt):
    T = h1.shape[0]
    tm = TM_FFN

    def body(hn_ref, h1_ref, w1_ref, w2_ref, gf_ref, t_ref, r_ref, dhf_ref, dhb_ref, loss_ref, dgf_ref):
        i = pl.program_id(0)

        @pl.when(i == 0)
        def _():
            loss_ref[...] = jnp.zeros_like(loss_ref)
            dgf_ref[...] = jnp.zeros_like(dgf_ref)

        hn = hn_ref[...]
        acc = h1_ref[...]
        for j in range(D_FF // FF_CHUNK):
            cols = slice(j * FF_CHUNK, (j + 1) * FF_CHUNK)
            r = jnp.maximum(_dot(hn, w1_ref[cols, :], NT), 0.0)
            r_ref[:, cols] = r.astype(BF16)
            act = jnp.square(r).astype(BF16)
            acc = acc + _dot(act, w2_ref[cols, :])
        n3, r3 = _rms(acc)
        gf_row = gf_ref[...]
        e = n3 * gf_row - t_ref[...]
        loss_ref[...] += 0.5 * jnp.sum(jnp.mean(e * e, axis=-1, keepdims=True))
        dy = e * (1.0 / D_MODEL)
        _accum_rows(dgf_ref, dy * n3)
        dh2 = _rms_bwd(n3, r3, gf_row, dy)
        dhf_ref[...] = dh2
        dhb_ref[...] = dh2.astype(BF16)

    sds = jax.ShapeDtypeStruct
    acc_spec = lambda n: pl.BlockSpec((8, n), lambda i: (0, 0))
    return pl.pallas_call(
        body, name="ffn_fwd", grid=(T // tm,),
        in_specs=[_rows(tm, D_MODEL), _rows(tm, D_MODEL), _resident((D_FF, D_MODEL)), _resident((D_FF, D_MODEL)),
                  _resident((1, D_MODEL)), _rows(tm, D_MODEL)],
        out_specs=[_rows(tm, D_FF), _rows(tm, D_MODEL), _rows(tm, D_MODEL), acc_spec(LANES), acc_spec(D_MODEL)],
        out_shape=[sds((T, D_FF), BF16), sds((T, D_MODEL), F32), sds((T, D_MODEL), BF16),
                   sds((8, LANES), F32), sds((8, D_MODEL), F32)],
        compiler_params=_params(("arbitrary",)),
    )(hn2, h1, w1t, w2, gf, tgt)


def _ffn_bwd(dh2b, dh2f, relu, h1, g2, w2, w1t):
    T = h1.shape[0]
    tm = TM_FFN

    def body(db_ref, df_ref, r_ref, h1_ref, g2_ref, w2_ref, w1t_ref, da_ref, d1f_ref, d1b_ref, dg_ref):
        @pl.when(pl.program_id(0) == 0)
        def _():
            dg_ref[...] = jnp.zeros_like(dg_ref)

        db = db_ref[...]
        acc = jnp.zeros((tm, D_MODEL), F32)
        for j in range(D_FF // FF_CHUNK):
            cols = slice(j * FF_CHUNK, (j + 1) * FF_CHUNK)
            da = (_dot(db, w2_ref[cols, :], NT) * (2.0 * r_ref[:, cols].astype(F32))).astype(BF16)
            da_ref[:, cols] = da
            acc = acc + _dot(da, w1t_ref[cols, :])
        n2, r2 = _rms(h1_ref[...])
        _accum_rows(dg_ref, acc * n2)
        dh1 = df_ref[...] + _rms_bwd(n2, r2, g2_ref[...], acc)
        d1f_ref[...] = dh1
        d1b_ref[...] = dh1.astype(BF16)

    sds = jax.ShapeDtypeStruct
    return pl.pallas_call(
        body, name="ffn_bwd", grid=(T // tm,),
        in_specs=[_rows(tm, D_MODEL), _rows(tm, D_MODEL), _rows(tm, D_FF), _rows(tm, D_MODEL),
                  _resident((1, D_MODEL)), _resident((D_FF, D_MODEL)), _resident((D_FF, D_MODEL))],
        out_specs=[_rows(tm, D_FF), _rows(tm, D_MODEL), _rows(tm, D_MODEL),
                   pl.BlockSpec((8, D_MODEL), lambda i: (0, 0))],
        out_shape=[sds((T, D_FF), BF16), sds((T, D_MODEL), F32), sds((T, D_MODEL), BF16), sds((8, D_MODEL), F32)],
        compiler_params=_params(("arbitrary",)),
    )(dh2b, dh2f, relu, h1, g2, w2, w1t)


def _out_bwd(dh1b, w_out, attn, gm, ga, gg):
    T = attn.shape[0]
    tm = TM_PROJ

    def body(d_ref, w_ref, a_ref, m_ref, ga_ref, gg_ref, da_ref, dm_ref, dga_ref, dgg_ref):
        @pl.when(pl.program_id(0) == 0)
        def _():
            dga_ref[...] = jnp.zeros_like(dga_ref)
            dgg_ref[...] = jnp.zeros_like(dgg_ref)

        d = d_ref[...]
        dan = _dot(d, w_ref[0:ATTN_W, :], NT)
        dgn = _dot(d, w_ref[ATTN_W:, :], NT)
        na, ra = _rms(a_ref[...])
        ng, rg = _rms(m_ref[...])
        _accum_rows(dga_ref, dan * na)
        _accum_rows(dgg_ref, dgn * ng)
        da_ref[...] = _rms_bwd(na, ra, ga_ref[...], dan)
        dm_ref[...] = _rms_bwd(ng, rg, gg_ref[...], dgn)

    sds = jax.ShapeDtypeStruct
    return pl.pallas_call(
        body, name="out_bwd", grid=(T // tm,),
        in_specs=[_rows(tm, D_MODEL), _resident((D_MODEL, D_MODEL)), _rows(tm, ATTN_W), _rows(tm, GMLP_W),
                  _resident((1, ATTN_W)), _resident((1, GMLP_W))],
        out_specs=[_rows(tm, ATTN_W), _rows(tm, GMLP_W), pl.BlockSpec((8, ATTN_W), lambda i: (0, 0)),
                   pl.BlockSpec((8, GMLP_W), lambda i: (0, 0))],
        out_shape=[sds((T, ATTN_W), F32), sds((T, GMLP_W), F32), sds((8, ATTN_W), F32), sds((8, GMLP_W), F32)],
        compiler_params=_params(("arbitrary",)),
    )(dh1b, w_out, attn, gm, ga, gg)


def _gmlp_bwd(u, z, dgm, ln_g, ln_b, sgu_w, sgu_bt):
    T = u.shape[0]
    tg = TM_GMLP
    nsteps = T // tg

    def body(u_ref, z_ref, d_ref, g_ref, b_ref, w_ref, sb_ref, dproj_hbm, dlg_ref, dlb_ref, dw_ref, dsb_ref,
             stage, sem):
        i = pl.program_id(0)
        slot = i % 2
        duz_ref = stage.at[slot]

        def to_dproj(step, buf):
            rows = pl.ds(pl.multiple_of(step * tg, tg), tg)
            return pltpu.make_async_copy(stage.at[buf], dproj_hbm.at[rows, pl.ds(3 * ATTN_W, 2 * GMLP_W)],
                                         sem.at[buf])

        @pl.when(i == 0)
        def _():
            for ref in (dlg_ref, dlb_ref, dw_ref, dsb_ref):
                ref[...] = jnp.zeros_like(ref)

        @pl.when(i >= 2)
        def _():
            to_dproj(i - 2, slot).wait()

        grp = lax.broadcasted_iota(jnp.int32, (tg, GMLP_W), 1) // HEAD_DIM
        lane = lax.broadcasted_iota(jnp.int32, (CHUNK, LANES), 1)
        causal, ws = _causal_ws(w_ref)
        lg = g_ref[...]
        uu, zz, dgm = u_ref[...], z_ref[...], d_ref[...]
        ug, tu, tz, xhat, rstd, zn16, mixed = _gmlp_core(uu, zz, lg, b_ref[...], ws, sb_ref, grp)
        dmx = dgm * ug
        duz_ref[:, 0:GMLP_W] = dgm * mixed * _gelu_grad(uu, tu)
        dmx16 = dmx.astype(BF16)
        dzn = []
        for ci in range(tg // CHUNK):
            rows = slice(ci * CHUNK, (ci + 1) * CHUNK)
            dmx_c, d = dmx16[rows, :], jnp.zeros((CHUNK, GMLP_W), F32)
            for g in range(N_GROUPS):
                mk = grp[:CHUNK] == g
                d = jnp.where(mk, _dot(ws[g], dmx_c, TN), d)
                dw_ref[g] += _dot(jnp.where(mk, dmx_c, jnp.zeros_like(dmx_c)), zn16[rows, :], NT)
            dzn.append(d)
        dzn = jnp.concatenate(dzn, axis=0)
        dsb = jnp.zeros((CHUNK, LANES), F32)
        for g in range(N_GROUPS):
            per_token = jnp.sum(jnp.where(grp == g, dmx, 0.0), axis=-1, keepdims=True)
            by_position = sum(per_token[ci * CHUNK:(ci + 1) * CHUNK] for ci in range(tg // CHUNK))
            dsb = jnp.where(lane == g, by_position, dsb)
        dsb_ref[...] += dsb
        _accum_rows(dlg_ref, dzn * xhat)
        _accum_rows(dlb_ref, dzn)
        dxh = dzn * lg
        dzg = rstd * (dxh - _group_mean(dxh, grp) - xhat * _group_mean(dxh * xhat, grp))
        duz_ref[:, GMLP_W:] = dzg * _gelu_grad(zz, tz)
        to_dproj(i, slot).start()

        @pl.when(i == nsteps - 1)
        def _():
            for g in range(N_GROUPS):
                dw_ref[g] = jnp.where(causal, dw_ref[g], 0.0)
            to_dproj(i, slot).wait()
            if nsteps >= 2:
                to_dproj(i - 1, 1 - slot).wait()

    sds = jax.ShapeDtypeStruct
    return pl.pallas_call(
        body, name="gmlp_bwd", grid=(nsteps,),
        in_specs=[_rows(tg, GMLP_W)] * 3 + [_resident((1, GMLP_W)), _resident((1, GMLP_W)),
                                              _resident((N_GROUPS, CHUNK, CHUNK)), _resident((CHUNK, N_GROUPS))],
        out_specs=[_HBM, pl.BlockSpec((8, GMLP_W), lambda i: (0, 0)),
                   pl.BlockSpec((8, GMLP_W), lambda i: (0, 0)),
                   pl.BlockSpec((N_GROUPS, CHUNK, CHUNK), lambda i: (0, 0, 0)),
                   pl.BlockSpec((CHUNK, LANES), lambda i: (0, 0))],
        out_shape=[sds((T, IN_W), F32), sds((8, GMLP_W), F32), sds((8, GMLP_W), F32),
                   sds((N_GROUPS, CHUNK, CHUNK), F32), sds((CHUNK, LANES), F32)],
        scratch_shapes=[pltpu.VMEM((2, tg, 2 * GMLP_W), F32), pltpu.SemaphoreType.DMA((2,))],
        compiler_params=_params(("arbitrary",)),
    )(u, z, dgm, ln_g, ln_b, sgu_w, sgu_bt)


def _attn_bwd(q, k, v, dattn, attn, lse, dproj, owner_grads=()):
    T = q.shape[0]
    nt = T // ATT_TILE
    ns = len(owner_grads)
    steps = (ATTN_W // LANES) * nt

    def body(sl_ref, q_hbm, k_hbm, v_hbm, do_hbm, o_hbm, lse_hbm, _, *rest):
        p_refs, rest = rest[:ns], rest[ns:]
        dq_hbm = dk_hbm = dv_hbm = rest[0]
        r_refs, rest = rest[1:1 + ns], rest[1 + ns:]
        qbuf, dobuf, obuf, lbuf, kbuf, vbuf, dqbuf, dkbuf, dvbuf, delta_s = rest[:10]
        sem_q, sem_do, sem_o, sem_l, sem_k, sem_v, sem_dq, sem_dk, sem_dv = rest[10:19]
        hp, t = pl.program_id(0), pl.program_id(1)
        step = hp * nt + t
        two, three = step % 2, step % 3
        before, after = (step + 2) % 3, (step + 1) % 3
        if ns:
            start, finish = _owner_exchange_phases(p_refs, r_refs, *rest[19:])
            pl.when(step == 0)(start)

        def fetch(hp_, t_, two_, three_):
            for hbm, buf, sem, slot in ((q_hbm, qbuf, sem_q, two_), (do_hbm, dobuf, sem_do, two_),
                                        (o_hbm, obuf, sem_o, two_), (lse_hbm, lbuf, sem_l, two_),
                                        (k_hbm, kbuf, sem_k, three_), (v_hbm, vbuf, sem_v, three_)):
                for cp in _tile_copies(hbm, buf.at[slot], sem.at[slot], hp_, t_):
                    cp.start()

        @pl.when(step == 0)
        def _():
            kbuf[2] = jnp.zeros((ATT_BLOCKS, CHUNK, LANES), F32)
            vbuf[2] = jnp.zeros((ATT_BLOCKS, CHUNK, LANES), F32)
            dkbuf[3] = jnp.zeros((ATT_BLOCKS, CHUNK, LANES), F32)
            dvbuf[3] = jnp.zeros((ATT_BLOCKS, CHUNK, LANES), F32)
            fetch(0, 0, 0, 0)

        @pl.when(step + 1 < steps)
        def _():
            fetch((step + 1) // nt, (step + 1) % nt, 1 - two, after)

        for buf, sem in ((qbuf, sem_q), (dobuf, sem_do), (obuf, sem_o), (lbuf, sem_l)):
            _wait_tile(buf.at[two], sem.at[two])
        _wait_tile(kbuf.at[three], sem_k.at[three])
        _wait_tile(vbuf.at[three], sem_v.at[three])

        @pl.when(step >= 2)
        def _():
            _wait_tile(dqbuf.at[two], sem_dq.at[two])

        @pl.when(step >= 3)
        def _():
            _wait_tile(dkbuf.at[three], sem_dk.at[three])
            _wait_tile(dvbuf.at[three], sem_dv.at[three])

        zero_tile = jnp.zeros((ATT_BLOCKS, CHUNK, LANES), F32)
        dqbuf[two] = zero_tile
        dkbuf[three] = zero_tile
        dvbuf[three] = zero_tile

        q_t, do_t, l_t, k_t, v_t = qbuf.at[two], dobuf.at[two], lbuf.at[two], kbuf.at[three], vbuf.at[three]
        k_b, v_b = kbuf.at[before], vbuf.at[before]
        dq_t, dk_t, dv_t = dqbuf.at[two], dkbuf.at[three], dvbuf.at[three]
        dk_b, dv_b = dkbuf.at[before], dvbuf.at[before]
        sink = jnp.where(t > 0, before, 3)
        dk_sink, dv_sink = dkbuf.at[sink], dvbuf.at[sink]
        head0 = lax.broadcasted_iota(jnp.int32, (CHUNK, LANES), 1) < HEAD_DIM
        for r in range(ATT_BLOCKS):
            dd = dobuf[two, r] * obuf[two, r]
            d0 = jnp.sum(jnp.where(head0, dd, 0.0), axis=-1, keepdims=True)
            d1 = jnp.sum(jnp.where(head0, 0.0, dd), axis=-1, keepdims=True)
            delta_s[r] = jnp.where(head0, d0, d1)

        def column(xb):
            return jnp.concatenate([xb[:, 0:1], xb[:, HEAD_DIM:HEAD_DIM + 1]], axis=0)

        no_key_before = jnp.where(lax.broadcasted_iota(jnp.int32, (2 * CHUNK, 2 * CHUNK), 1) < CHUNK, NEG, 0.0)
        for d in DILATIONS:
            bias = _residue_bias(sl_ref, d)
            for j in range(ATT_BLOCKS):
                kcat = jnp.concatenate([_rm_block_before(k_t, k_b, d, j), _rm_block(k_t, d, j)], axis=0).astype(BF16)
                vcat = jnp.concatenate([_rm_block_before(v_t, v_b, d, j), _rm_block(v_t, d, j)], axis=0).astype(BF16)
                q2 = _stack_heads(_rm_block(q_t, d, j), head0)
                do2 = _stack_heads(_rm_block(do_t, d, j), head0)
                s = _dot(q2, kcat, NT) + bias
                if _first_in_tile(d, j):
                    s = s + jnp.where(t == 0, 1.0, 0.0) * no_key_before
                p = jnp.exp(s - column(_rm_block(l_t, d, j)))
                ds = (p * (_dot(do2, vcat, NT) - column(_rm_block(delta_s, d, j)))).astype(BF16)
                _rm_add(dq_t, _residue_rows(d, j), _unstack_heads(_dot(ds, kcat), head0))
                ck = _dot(ds, q2, TN)
                cv = _dot(p.astype(BF16), do2, TN)
                _rm_add(dk_t, _residue_rows(d, j), ck[CHUNK:, :])
                _rm_add(dv_t, _residue_rows(d, j), cv[CHUNK:, :])
                if _first_in_tile(d, j):
                    rows = [(r, CHUNK - n, n) for r, _, n in _residue_rows(d, j)]
                    _rm_add(dk_sink, rows, ck[:CHUNK, :])
                    _rm_add(dv_sink, rows, cv[:CHUNK, :])
                else:
                    rows = [(r, lo - n, n) for r, lo, n in _residue_rows(d, j)]
                    _rm_add(dk_t, rows, ck[:CHUNK, :])
                    _rm_add(dv_t, rows, cv[:CHUNK, :])

        for r in range(ATT_BLOCKS):
            dqbuf[two, r] = dqbuf[two, r] * Q_SCALE
        for cp in _tile_copies(dq_hbm, dq_t, sem_dq.at[two], hp, t, to_hbm=True):
            cp.start()

        @pl.when(t > 0)
        def _():
            for cp in (_tile_copies(dk_hbm, dk_b, sem_dk.at[before], hp, t - 1, to_hbm=True, lane0=ATTN_W)
                       + _tile_copies(dv_hbm, dv_b, sem_dv.at[before], hp, t - 1, to_hbm=True, lane0=2 * ATTN_W)):
                cp.start()

        @pl.when(t == nt - 1)
        def _():
            for cp in (_tile_copies(dk_hbm, dk_t, sem_dk.at[three], hp, t, to_hbm=True, lane0=ATTN_W)
                       + _tile_copies(dv_hbm, dv_t, sem_dv.at[three], hp, t, to_hbm=True, lane0=2 * ATTN_W)):
                cp.start()

        @pl.when(step == steps - 1)
        def _():
            for slot in range(2):
                _wait_tile(dqbuf.at[slot], sem_dq.at[slot])
            for slot in range(3):
                _wait_tile(dkbuf.at[slot], sem_dk.at[slot])
                _wait_tile(dvbuf.at[slot], sem_dv.at[slot])

        if ns:
            pl.when(step == steps - 1)(finish)

    tile = lambda n: pltpu.VMEM((n, ATT_BLOCKS, CHUNK, LANES), F32)
    dma = lambda n: pltpu.SemaphoreType.DMA((n,))
    view = jax.ShapeDtypeStruct((T // ATT_BLOCKS, ATT_BLOCKS, ATTN_W), F32)
    outs = pl.pallas_call(
        body, name="attn_bwd", grid=(ATTN_W // LANES, nt),
        in_specs=[pl.BlockSpec((8, LANES), lambda c, t: (0, c))] + [_HBM] * (7 + ns),
        out_specs=[_HBM] * (1 + ns),
        out_shape=[jax.ShapeDtypeStruct((T // ATT_BLOCKS, ATT_BLOCKS, IN_W), F32)]
        + [jax.ShapeDtypeStruct(p.shape, p.dtype) for p in owner_grads],
        scratch_shapes=[tile(2), tile(2), tile(2), tile(2), tile(3), tile(3), tile(2), tile(4), tile(4),
                        pltpu.VMEM((ATT_BLOCKS, CHUNK, LANES), F32)]
        + [dma(2), dma(2), dma(2), dma(2), dma(3), dma(3), dma(2), dma(3), dma(3)]
        + (_owner_exchange_sems(ns) if ns else []),
        input_output_aliases={7: 0},
        compiler_params=_params(("arbitrary", "arbitrary")),
    )(_slope_table(), *[_residue_view(a) for a in (q, k, v, dattn, attn, lse, dproj)], *owner_grads)
    return outs[0].reshape(T, IN_W), tuple(outs[1:])


def _proj_bwd(dproj, w_in_t, x, g1, dh1, chip_sums=()):
    T = x.shape[0]
    tm = TM_PROJ
    ns = len(chip_sums)
    steps = T // tm

    def body(d_ref, w_ref, x_ref, g_ref, r_ref, *rest):
        p_refs, rest = rest[:ns], rest[ns:]
        dx_ref, dg_ref = rest[:2]
        r_refs, sems = rest[2:2 + ns], rest[2 + ns:]
        step = pl.program_id(0)
        if ns:
            start, finish = _chip_exchange_phases(p_refs, r_refs, *sems)
            pl.when(step == 0)(start)

        @pl.when(step == 0)
        def _():
            dg_ref[...] = jnp.zeros_like(dg_ref)

        dhn = _dot(d_ref[...].astype(BF16), w_ref[...])
        n1, r1 = _rms(x_ref[...])
        _accum_rows(dg_ref, dhn * n1)
        dx_ref[...] = r_ref[...] + _rms_bwd(n1, r1, g_ref[...], dhn)
        if ns:
            pl.when(step == steps - 1)(finish)

    outs = pl.pallas_call(
        body, name="proj_bwd", grid=(steps,),
        in_specs=[_rows(tm, IN_W), _resident((IN_W, D_MODEL)), _rows(tm, D_MODEL), _resident((1, D_MODEL)),
                  _rows(tm, D_MODEL)] + [_HBM] * ns,
        out_specs=[_rows(tm, D_MODEL), pl.BlockSpec((8, D_MODEL), lambda i: (0, 0))] + [_HBM] * ns,
        out_shape=[jax.ShapeDtypeStruct((T, D_MODEL), F32), jax.ShapeDtypeStruct((8, D_MODEL), F32)]
        + [jax.ShapeDtypeStruct(p.shape, p.dtype) for p in chip_sums],
        scratch_shapes=_chip_exchange_sems(ns) if ns else [],
        compiler_params=_params(("arbitrary",)),
    )(dproj, w_in_t, x, g1, dh1, *chip_sums)
    return outs[0], outs[1], tuple(outs[2:])


def _dw(a, b, name, tile, square_a=False, out_dtype=F32):
    T, ka = a.shape
    nb = b.shape[1]
    tka, tnb, tt = tile
    tt = min(tt, T)
    last = T // tt - 1

    def body(a_ref, b_ref, *refs):
        o_ref = refs[0]
        acc_ref = refs[1] if len(refs) > 1 else o_ref
        s = pl.program_id(2)

        @pl.when(s == 0)
        def _():
            acc_ref[...] = jnp.zeros_like(acc_ref)

        a_tile = a_ref[...]
        if square_a:
            a_tile = jnp.square(a_tile.astype(F32))
        acc_ref[...] += _dot(a_tile.astype(BF16), b_ref[...], TN)
        if acc_ref is not o_ref:
            @pl.when(s == last)
            def _():
                o_ref[...] = acc_ref[...].astype(out_dtype)

    return pl.pallas_call(
        body, name=name, grid=(ka // tka, nb // tnb, T // tt),
        in_specs=[pl.BlockSpec((tt, tka), lambda i, j, s: (s, i)), pl.BlockSpec((tt, tnb), lambda i, j, s: (s, j))],
        out_specs=pl.BlockSpec((tka, tnb), lambda i, j, s: (i, j)),
        out_shape=jax.ShapeDtypeStruct((ka, nb), out_dtype),
        scratch_shapes=[] if out_dtype == F32 else [pltpu.VMEM((tka, tnb), F32)],
        compiler_params=_params(("parallel", "parallel", "arbitrary")),
    )(a, b)


def _adamw_update(w, m, v, g):
    m2 = ADAM_B1 * m + (1.0 - ADAM_B1) * g
    v2 = ADAM_B2 * v + (1.0 - ADAM_B2) * jnp.square(g)
    m_hat = m2 / (1.0 - ADAM_B1 ** ADAM_STEP)
    v_hat = v2 / (1.0 - ADAM_B2 ** ADAM_STEP)
    return -ADAM_LR * (m_hat / (jnp.sqrt(v_hat) + ADAM_EPS) + ADAM_WD * w), m2, v2


def _adamw_tiny(ws, ms, vs, parts):
    n = len(ws)
    P = parts.shape[0]

    def body(*refs):
        w_refs, m_refs, v_refs, p_ref = refs[:n], refs[n:2 * n], refs[2 * n:3 * n], refs[3 * n]
        outs = refs[3 * n + 1:]

        def total(slot, rows):
            g = p_ref[0, 8 * slot:8 * slot + rows, :]
            for i in range(1, P):
                g = g + p_ref[i, 8 * slot:8 * slot + rows, :]
            return g

        for k in range(n):
            g = total(k, ws[k].shape[0])
            outs[4 * k][...] = g
            outs[4 * k + 1][...], outs[4 * k + 2][...], outs[4 * k + 3][...] = _adamw_update(
                w_refs[k][...], m_refs[k][...], v_refs[k][...], g)
        outs[4 * n][...] = total(n, 8)

    sds = jax.ShapeDtypeStruct
    return pl.pallas_call(
        body, name="adamw_tiny",
        out_shape=[sds(w.shape, F32) for w in ws for _ in range(4)] + [sds((8, LANES), F32)],
    )(*ws, *ms, *vs, parts)


def _adamw(w, m, v, parts, name, tr, transposed=False):
    R, C = w.shape
    P = parts.shape[0]

    def body(w_ref, m_ref, v_ref, p_ref, g_ref, d_ref, m2_ref, v2_ref):
        g = p_ref[0].astype(F32)
        for i in range(1, P):
            g = g + p_ref[i].astype(F32)
        if transposed:
            g = g.T
        g_ref[...] = g
        d_ref[...], m2_ref[...], v2_ref[...] = _adamw_update(w_ref[...], m_ref[...], v_ref[...], g)

    spec = _rows(tr, C)
    part_spec = (pl.BlockSpec((P, C, tr), lambda i: (0, 0, i)) if transposed
                 else pl.BlockSpec((P, tr, C), lambda i: (0, i, 0)))
    return pl.pallas_call(
        body, name=name, grid=(R // tr,),
        in_specs=[spec, spec, spec, part_spec],
        out_specs=[spec] * 4,
        out_shape=[jax.ShapeDtypeStruct((R, C), F32)] * 4,
        compiler_params=_params(("parallel",)),
    )(w, m, v, parts)


def _pair_sum(core, grad, recv, name):
    _, _, n, C = grad.shape
    tr = n // 2

    def body(c_ref, a_ref, b_ref, o_ref):
        o_ref[...] = a_ref[...] + b_ref[...]

    spec = pl.BlockSpec((1, tr, C), lambda i, j, c_ref: (i, j, 0))
    return pl.pallas_call(
        body, name=name,
        grid_spec=pltpu.PrefetchScalarGridSpec(
            num_scalar_prefetch=1, grid=(4, n // tr),
            in_specs=[pl.BlockSpec((1, None, tr, C), lambda i, j, c_ref: (i, c_ref[0], j, 0)), spec],
            out_specs=spec),
        out_shape=jax.ShapeDtypeStruct(recv.shape, F32),
        compiler_params=_params(("parallel", "parallel")),
    )(core.reshape(1), grad, recv)


_HBM = pl.BlockSpec(memory_space=pltpu.HBM)


def _place():
    return lax.axis_index("x"), lax.axis_index("y"), lax.axis_index("c")


def _gathered_shape(shard):
    return jax.ShapeDtypeStruct((N_DEV,) + shard.shape, shard.dtype)


def _gather_sems(n):
    return [pltpu.SemaphoreType.DMA((7, n)), pltpu.SemaphoreType.DMA((7, n)), pltpu.SemaphoreType.DMA((n,))]


def _gather_phases(x_refs, out_refs, send_sems, recv_sems, local_sems):
    x, y, c = _place()
    me, sibling = (x, y, c), (x, y, 1 - c)
    chips = [(1 - x, y), (x, 1 - y), (1 - x, 1 - y)]
    arrays = range(len(x_refs))

    def slot(i, px, py, pc):
        return out_refs[i].at[4 * px + 2 * py + pc]

    def copy(i, k, block, to, own=False):
        return pltpu.make_async_remote_copy(
            src_ref=x_refs[i] if own else slot(i, *block), dst_ref=slot(i, *block),
            send_sem=send_sems.at[k, i], recv_sem=recv_sems.at[k, i], device_id=to, device_id_type=MESH)

    def mine(i):
        return pltpu.make_async_copy(x_refs[i], slot(i, *me), local_sems.at[i])

    def start():
        for i in arrays:
            mine(i).start()
            copy(i, 0, me, sibling, own=True).start()
            for j, chip in enumerate(chips):
                copy(i, 1 + j, me, (*chip, c), own=True).start()

    def forward():
        for i in arrays:
            for j, chip in enumerate(chips):
                copy(i, 1 + j, (*chip, c), me).wait_recv()
                copy(i, 4 + j, (*chip, c), sibling).start()

    def finish():
        for i in arrays:
            copy(i, 0, sibling, me).wait_recv()
            copy(i, 0, me, sibling, own=True).wait_send()
            for j, chip in enumerate(chips):
                copy(i, 4 + j, (*chip, 1 - c), me).wait_recv()
                copy(i, 1 + j, me, (*chip, c), own=True).wait_send()
                copy(i, 4 + j, (*chip, c), sibling).wait_send()
            mine(i).wait()

    return start, forward, finish


def _all_gather(shards, name):
    n = len(shards)

    def body(*refs):
        start, forward, finish = _gather_phases(refs[:n], refs[n:2 * n], *refs[2 * n:])
        start()
        forward()
        finish()

    return pl.pallas_call(
        body, name=name,
        out_shape=[_gathered_shape(s) for s in shards],
        in_specs=[_HBM] * n, out_specs=[_HBM] * n,
        scratch_shapes=_gather_sems(n),
    )(*shards)


def _sibling_exchange(grads, name):
    n = len(grads)

    def body(*refs):
        g_refs, r_refs, send_sems, recv_sems = refs[:n], refs[n:2 * n], refs[2 * n], refs[2 * n + 1]
        x, y, c = _place()
        copies = [pltpu.make_async_remote_copy(
            src_ref=g_refs[i].at[:, 1 - c], dst_ref=r_refs[i], send_sem=send_sems.at[i], recv_sem=recv_sems.at[i],
            device_id=(x, y, 1 - c), device_id_type=MESH) for i in range(n)]
        for cp in copies:
            cp.start()
        for cp in copies:
            cp.wait()

    return pl.pallas_call(
        body, name=name,
        out_shape=[jax.ShapeDtypeStruct((g.shape[0],) + g.shape[2:], g.dtype) for g in grads],
        in_specs=[_HBM] * n, out_specs=[_HBM] * n,
        scratch_shapes=[pltpu.SemaphoreType.DMA((n,)), pltpu.SemaphoreType.DMA((n,))],
    )(*grads)


def _owner_exchange_sems(n):
    return [pltpu.SemaphoreType.DMA((7, n)), pltpu.SemaphoreType.DMA((7, n)), pltpu.SemaphoreType.DMA((n,))]


def _owner_exchange_phases(g_refs, r_refs, send_sems, recv_sems, local_sems):
    x, y, c = _place()
    me = 4 * x + 2 * y + c
    flip = lambda v, bit: 1 - v if bit else v
    peers = [(flip(x, k & 4), flip(y, k & 2), flip(c, k & 1)) for k in range(1, N_DEV)]
    arrays = range(len(g_refs))

    def mine(i):
        return pltpu.make_async_copy(g_refs[i].at[me], r_refs[i].at[me], local_sems.at[i])

    def copy(i, k, src_slot, dst_slot):
        return pltpu.make_async_remote_copy(
            src_ref=g_refs[i].at[src_slot], dst_ref=r_refs[i].at[dst_slot],
            send_sem=send_sems.at[k, i], recv_sem=recv_sems.at[k, i], device_id=peers[k], device_id_type=MESH)

    def start():
        for i in arrays:
            mine(i).start()
            for k, (px, py, pc) in enumerate(peers):
                copy(i, k, 4 * px + 2 * py + pc, me).start()

    def finish():
        for i in arrays:
            for k, (px, py, pc) in enumerate(peers):
                copy(i, k, me, 4 * px + 2 * py + pc).wait_recv()
                copy(i, k, 4 * px + 2 * py + pc, me).wait_send()
            mine(i).wait()

    return start, finish


def _chip_exchange_sems(n):
    return [pltpu.SemaphoreType.DMA((3, n)), pltpu.SemaphoreType.DMA((3, n)), pltpu.SemaphoreType.DMA((n,))]


def _chip_exchange_phases(p_refs, r_refs, send_sems, recv_sems, local_sems):
    x, y, c = _place()
    my_chip = 2 * x + y
    chips = [(1 - x, y), (x, 1 - y), (1 - x, 1 - y)]
    arrays = range(len(p_refs))

    def mine(i):
        return pltpu.make_async_copy(p_refs[i].at[my_chip], r_refs[i].at[my_chip], local_sems.at[i])

    def copy(i, k, src_chip, dst_chip):
        px, py = chips[k]
        return pltpu.make_async_remote_copy(
            src_ref=p_refs[i].at[src_chip], dst_ref=r_refs[i].at[dst_chip],
            send_sem=send_sems.at[k, i], recv_sem=recv_sems.at[k, i], device_id=(px, py, c), device_id_type=MESH)

    def start():
        for i in arrays:
            mine(i).start()
            for k, (px, py) in enumerate(chips):
                copy(i, k, 2 * px + py, my_chip).start()

    def finish():
        for i in arrays:
            for k, (px, py) in enumerate(chips):
                copy(i, k, my_chip, 2 * px + py).wait_recv()
                copy(i, k, 2 * px + py, my_chip).wait_send()
            mine(i).wait()

    return start, finish


_R_IN, _R_OUT, _R_FF = IN_W // N_DEV, D_MODEL // N_DEV, D_FF // N_DEV


def _by_owner(g):
    return g.reshape(4, 2, g.shape[0] // N_DEV, D_MODEL)


def _local_step(x, tgt, small, w_in_t, rest, core=None):
    exchange = core is not None
    g1, g2, gf = small["norm1_g"], small["norm2_g"], small["final_norm_g"].reshape(1, D_MODEL)
    ga, gg = small["attn_out_g"], small["gmlp_out_g"]
    ln_g = small["sgu_ln_g"].reshape(1, GMLP_W)
    ln_b = small["sgu_ln_b"].reshape(1, GMLP_W)
    sgu_w = small["sgu_w"][0]
    sgu_bt = small["sgu_b"][0].T

    hn1, gathered = _norm1(x, g1, shards=(w_in_t,) if exchange else ())
    if exchange:
        w_in_t = gathered[0].reshape(IN_W, D_MODEL)
    q, k, v, u, z = _proj_fwd(hn1, w_in_t)
    attn, lse, gathered = _attn_fwd(q, k, v, shards=rest if exchange else ())
    w_out, w_ff1_t, w_ff2 = [g.reshape(-1, D_MODEL) for g in gathered] if exchange else rest
    gm = _gmlp_fwd(u, z, ln_g, ln_b, sgu_w, sgu_bt)
    mixed, h1, hn2 = _out_fwd(attn, gm, ga, gg, w_out, x, g2)
    relu, dh2f, dh2b, loss8, dgf8 = _ffn_fwd(hn2, h1, w_ff1_t, w_ff2, gf, tgt)

    da, dh1f, dh1b, dg2 = _ffn_bwd(dh2b, dh2f, relu, h1, g2, w_ff2, w_ff1_t)
    wire = BF16 if exchange else F32
    dw_ff2 = _dw(relu, dh2b, "dw_ff2", DW_TILE, square_a=True, out_dtype=wire)
    dw_ff1_t = _dw(da, hn2, "dw_ff1", DW_TILE, out_dtype=wire)
    dattn, dgm, dga, dgg = _out_bwd(dh1b, w_out, attn, gm, ga, gg)
    dw_out = _dw(mixed, dh1b, "dw_out", DW_TILE, out_dtype=wire)
    early = [dw_out, dw_ff1_t, dw_ff2]
    if exchange:
        early = [g.reshape(N_DEV, -1, D_MODEL) for g in early]
    dproj, dlg, dlb, dsw, dsb = _gmlp_bwd(u, z, dgm, ln_g, ln_b, sgu_w, sgu_bt)
    dproj, arrived = _attn_bwd(q, k, v, dattn, attn, lse, dproj, owner_grads=early if exchange else ())
    dw_in_t = _dw(dproj, hn1, "dw_in", DW_TILE_IN)
    late = ()
    if exchange:
        by_owner = _by_owner(dw_in_t)
        got, = _sibling_exchange([by_owner], "grad_sibling_exchange")
        late = (_pair_sum(core, by_owner, got, "grad_pair_sum"),)
    dx, dg1, late = _proj_bwd(dproj, w_in_t, x, g1, dh1f, chip_sums=late)
    if exchange:
        dw_in_t, early = late[0], arrived

    small_grads = dict(
        norm1_g=dg1[0], sgu_ln_g=dlg[0], sgu_ln_b=dlb[0], sgu_w=dsw, sgu_b=dsb[:, :N_GROUPS].T,
        attn_out_g=dga[0], gmlp_out_g=dgg[0], norm2_g=dg2[0], final_norm_g=dgf8[0])
    return loss8[0, 0], dx, (dw_in_t, *early), small_grads


SMALL_NAMES = ("norm1_g", "sgu_ln_g", "sgu_ln_b", "sgu_w", "sgu_b", "attn_out_g", "gmlp_out_g", "norm2_g",
               "final_norm_g")
WEIGHT_ORDER = ("norm1_g", "w_in", "sgu_ln_g", "sgu_ln_b", "sgu_w", "sgu_b", "attn_out_g", "gmlp_out_g", "w_out",
                "norm2_g", "w_ff1", "w_ff2", "final_norm_g")


TINY_NAMES = tuple(n for n in SMALL_NAMES if n != "sgu_w")


def _as_rows(a):
    return a.reshape(-1, LANES)


def _pack_tiny_grads(d, loss):
    slots = [jnp.pad(_as_rows(d[n]), ((0, 8 - d[n].size // LANES), (0, 0))) for n in TINY_NAMES]
    return jnp.concatenate(slots + [jnp.full((8, LANES), loss, F32)], axis=0)


def kernel(x, norm1_g, w_in, sgu_ln_g, sgu_ln_b, sgu_w, sgu_b, attn_out_g, gmlp_out_g, w_out, norm2_g, w_ff1, w_ff2, final_norm_g, loss_target, m_norm1_g, m_w_in, m_sgu_ln_g, m_sgu_ln_b, m_sgu_w, m_sgu_b, m_attn_out_g, m_gmlp_out_g, m_w_out, m_norm2_g, m_w_ff1, m_w_ff2, m_final_norm_g, v_norm1_g, v_w_in, v_sgu_ln_g, v_sgu_ln_b, v_sgu_w, v_sgu_b, v_attn_out_g, v_gmlp_out_g, v_w_out, v_norm2_g, v_w_ff1, v_w_ff2, v_final_norm_g):
    w = dict(norm1_g=norm1_g, w_in=w_in, sgu_ln_g=sgu_ln_g, sgu_ln_b=sgu_ln_b, sgu_w=sgu_w, sgu_b=sgu_b,
             attn_out_g=attn_out_g, gmlp_out_g=gmlp_out_g, w_out=w_out, norm2_g=norm2_g, w_ff1=w_ff1, w_ff2=w_ff2,
             final_norm_g=final_norm_g)
    m = dict(norm1_g=m_norm1_g, w_in=m_w_in, sgu_ln_g=m_sgu_ln_g, sgu_ln_b=m_sgu_ln_b, sgu_w=m_sgu_w, sgu_b=m_sgu_b,
             attn_out_g=m_attn_out_g, gmlp_out_g=m_gmlp_out_g, w_out=m_w_out, norm2_g=m_norm2_g, w_ff1=m_w_ff1,
             w_ff2=m_w_ff2, final_norm_g=m_final_norm_g)
    v = dict(norm1_g=v_norm1_g, w_in=v_w_in, sgu_ln_g=v_sgu_ln_g, sgu_ln_b=v_sgu_ln_b, sgu_w=v_sgu_w, sgu_b=v_sgu_b,
             attn_out_g=v_attn_out_g, gmlp_out_g=v_gmlp_out_g, w_out=v_w_out, norm2_g=v_norm2_g, w_ff1=v_w_ff1,
             w_ff2=v_w_ff2, final_norm_g=v_final_norm_g)
    big = ("w_in", "w_out", "w_ff1", "w_ff2")
    core = lax.axis_index("c")

    rest = (w_out[0].astype(BF16), w_ff1[0].T.astype(BF16), w_ff2[0].astype(BF16))
    loss, dx, parts, small_grads = _local_step(x[0], loss_target[0], {n: w[n] for n in SMALL_NAMES},
                                               w_in[0].T.astype(BF16), rest, core=core)

    new = {}
    for n, p, transposed, tr in zip(big, parts, (True, False, True, False), (128, 128, 128, 256)):
        new[n] = [a[None] for a in _adamw(w[n][0], m[n][0], v[n][0], p, "adamw_" + n, tr, transposed)]

    tiny_parts, sgu_parts = _all_gather(
        [_pack_tiny_grads(small_grads, loss), _as_rows(small_grads["sgu_w"])], "small_grad_all_gather")
    tiny = _adamw_tiny(*[[_as_rows(src[n]) for n in TINY_NAMES] for src in (w, m, v)], tiny_parts)
    sgu = _adamw(_as_rows(sgu_w), _as_rows(m_sgu_w), _as_rows(v_sgu_w), sgu_parts, "adamw_sgu_w", 512)
    loss = tiny[-1][0, 0]

    outs = []
    for i in range(4):
        d = {n: new[n][i] for n in big}
        d.update({n: tiny[4 * k + i].reshape(w[n].shape) for k, n in enumerate(TINY_NAMES)})
        d["sgu_w"] = sgu[i].reshape(sgu_w.shape)
        outs.extend(d[n] for n in WEIGHT_ORDER)
    return (loss, dx[None], *outs)
```

```python
import functools
import math

import numpy as np
import jax
import jax.numpy as jnp
from jax import lax
from jax.experimental import pallas as pl
from jax.experimental.pallas import tpu as pltpu

F32 = jnp.float32
BF16 = jnp.bfloat16

D_MODEL = 1024
HEAD_DIM = 64
N_HEADS = 12
ATTN_W = N_HEADS * HEAD_DIM
N_GROUPS = 4
GMLP_W = N_GROUPS * HEAD_DIM
IN_W = 3 * ATTN_W + 2 * GMLP_W
D_FF = 4 * D_MODEL
CHUNK = 128
DILATIONS = (1, 4, 16)
EPS = 1e-6
Q_SCALE = HEAD_DIM ** -0.5
NEG = -1e30

ADAM_LR, ADAM_B1, ADAM_B2, ADAM_EPS, ADAM_WD, ADAM_STEP = 0.001, 0.9, 0.999, 1e-08, 0.01, 10

N_DEV = 8
LANES = 128
VMEM_LIMIT = 56 << 20

TM_PROJ = 512
TM_FFN = 512
FF_CHUNK = 512
TM_GMLP = 1024
DW_TILE = (512, 1024, 4096)
DW_TILE_IN = (IN_W // 2, 1024, 2048)

MESH = pl.DeviceIdType.MESH


def _alibi_slopes(n):
    def pow2(m):
        start = 2.0 ** (-8.0 / m)
        return [start ** (i + 1) for i in range(m)]
    c = 2 ** int(math.floor(math.log2(n)))
    s = pow2(n) if c == n else pow2(c) + pow2(2 * c)[0::2][: n - c]
    return np.asarray(s, dtype=np.float32)


SLOPES = _alibi_slopes(N_HEADS)


def _params(sem=None):
    kw = dict(vmem_limit_bytes=VMEM_LIMIT)
    if sem is not None:
        kw["dimension_semantics"] = sem
    return pltpu.CompilerParams(**kw)


def _rows(tm, n):
    return pl.BlockSpec((tm, n), lambda i: (i, 0))


def _resident(shape):
    return pl.BlockSpec(shape, lambda *_: (0,) * len(shape), pipeline_mode=pl.Buffered(1))


def _rms(x):
    r = lax.rsqrt(jnp.mean(x * x, axis=-1, keepdims=True) + EPS)
    return x * r, r


def _rms_bwd(n, r, g, dy):
    dn = dy * g
    return r * (dn - n * jnp.mean(dn * n, axis=-1, keepdims=True))


def _accum_rows(acc_ref, v):
    acc_ref[...] += jnp.broadcast_to(jnp.sum(v, axis=0, keepdims=True), acc_ref.shape)


_G0 = math.sqrt(2.0 / math.pi)
_G1 = 0.044715


def _gelu(x):
    t = jnp.tanh(_G0 * (x + _G1 * (x * x * x)))
    return x * (0.5 * (1.0 + t)), t


def _gelu_grad(x, t):
    return 0.5 * (1.0 + t) + 0.5 * x * (1.0 - t * t) * (_G0 * (1.0 + 3.0 * _G1 * x * x))


NT = (((1,), (1,)), ((), ()))
TN = (((0,), (0,)), ((), ()))


def _dot(a, b, dims=None):
    if dims is None:
        return jnp.dot(a, b, preferred_element_type=F32)
    return lax.dot_general(a, b, dims, preferred_element_type=F32)


def _norm1(x, g1, shards=()):
    T = x.shape[0]
    tm = TM_PROJ
    ns = len(shards)
    steps = T // tm

    def body(x_ref, g_ref, *rest):
        x_refs, hn_ref, g_refs, sems = rest[:ns], rest[ns], rest[ns + 1:2 * ns + 1], rest[2 * ns + 1:]
        step = pl.program_id(0)
        if ns:
            start, forward, finish = _gather_phases(x_refs, g_refs, *sems)
            pl.when(step == 0)(start)
        n, _ = _rms(x_ref[...])
        hn_ref[...] = (n * g_ref[...]).astype(BF16)
        if ns:
            @pl.when(step == steps - 1)
            def _():
                forward()
                finish()

    outs = pl.pallas_call(
        body, name="norm1", grid=(steps,),
        in_specs=[_rows(tm, D_MODEL), _resident((1, D_MODEL))] + [_HBM] * ns,
        out_specs=[_rows(tm, D_MODEL)] + [_HBM] * ns,
        out_shape=[jax.ShapeDtypeStruct((T, D_MODEL), BF16)] + [_gathered_shape(s) for s in shards],
        scratch_shapes=_gather_sems(ns) if ns else [],
        compiler_params=_params(("arbitrary",)),
    )(x, g1, *shards)
    return outs[0], tuple(outs[1:])


def _proj_fwd(hn1, w_in_t):
    T = hn1.shape[0]
    tm = TM_PROJ

    def body(hn_ref, w_ref, q_ref, k_ref, v_ref, u_ref, z_ref):
        hn = hn_ref[...]
        a = ATTN_W
        q_ref[...] = _dot(hn, w_ref[0:a, :], NT) * Q_SCALE
        k_ref[...] = _dot(hn, w_ref[a:2 * a, :], NT)
        v_ref[...] = _dot(hn, w_ref[2 * a:3 * a, :], NT)
        u_ref[...] = _dot(hn, w_ref[3 * a:3 * a + GMLP_W, :], NT)
        z_ref[...] = _dot(hn, w_ref[3 * a + GMLP_W:, :], NT)

    sds = jax.ShapeDtypeStruct
    return pl.pallas_call(
        body, name="proj_fwd", grid=(T // tm,),
        in_specs=[_rows(tm, D_MODEL), _resident((IN_W, D_MODEL))],
        out_specs=[_rows(tm, ATTN_W), _rows(tm, ATTN_W), _rows(tm, ATTN_W), _rows(tm, GMLP_W), _rows(tm, GMLP_W)],
        out_shape=[sds((T, ATTN_W), F32), sds((T, ATTN_W), F32), sds((T, ATTN_W), F32), sds((T, GMLP_W), F32),
                   sds((T, GMLP_W), F32)],
        compiler_params=_params(("parallel",)),
    )(hn1, w_in_t)


ATT_TILE = 2048
ATT_BLOCKS = ATT_TILE // CHUNK
SM_BLOCKS = 4


def _slope_table():
    row = np.repeat(SLOPES, HEAD_DIM)
    return jnp.asarray(np.broadcast_to(row[None], (8, ATTN_W)), F32)


def _residue_view(a):
    return a.reshape(a.shape[0] // ATT_BLOCKS, ATT_BLOCKS, a.shape[1])


def _tile_copies(hbm, buf, sem, hp, t, to_hbm=False, lane0=0):
    rows = pl.ds(pl.multiple_of(t * CHUNK, CHUNK), CHUNK)
    lanes = pl.ds(pl.multiple_of(lane0 + hp * LANES, LANES), LANES)
    pairs = [(hbm.at[rows, r, lanes], buf.at[r]) for r in range(ATT_BLOCKS)]
    return [pltpu.make_async_copy(v, h, sem) if to_hbm else pltpu.make_async_copy(h, v, sem) for h, v in pairs]


def _wait_tile(buf, sem):
    pltpu.make_async_copy(buf, buf, sem).wait()


def _residue_rows(d, j):
    if d == 16:
        return [(j, 0, CHUNK)]
    if d == 4:
        return [(j % 4 + 4 * m, 32 * (j // 4), 32) for m in range(4)]
    return [(r, 8 * j, 8) for r in range(ATT_BLOCKS)]


def _block_order(p, d):
    if d == 16:
        return p
    if d == 4:
        return 4 * (p & 31) + (p >> 5)
    return 16 * (p & 7) + (p >> 3)


def _first_in_tile(d, j):
    return _residue_rows(d, j)[0][1] == 0


def _rm_block(buf, d, j):
    return jnp.concatenate([buf[r, lo:lo + n, :] for r, lo, n in _residue_rows(d, j)], axis=0)


def _rm_block_before(buf, buf_before, d, j):
    if _first_in_tile(d, j):
        return jnp.concatenate([buf_before[r, CHUNK - n:CHUNK, :] for r, _, n in _residue_rows(d, j)], axis=0)
    return jnp.concatenate([buf[r, lo - n:lo, :] for r, lo, n in _residue_rows(d, j)], axis=0)


def _rm_store(buf, d, j, val):
    at = 0
    for r, lo, n in _residue_rows(d, j):
        buf[r, lo:lo + n, :] = val[at:at + n, :]
        at += n


def _rm_add(buf, rows, val):
    at = 0
    for r, lo, n in rows:
        buf[r, lo:lo + n, :] += val[at:at + n, :]
        at += n


def _residue_bias(sl_ref, d):
    shape = (2 * CHUNK, 2 * CHUNK)
    row = lax.broadcasted_iota(jnp.int32, shape, 0)
    col = lax.broadcasted_iota(jnp.int32, shape, 1)
    steps = _block_order(row & (CHUNK - 1), d) + CHUNK - (_block_order(col & (CHUNK - 1), d) + (col & CHUNK))
    band = (steps >= 0) & (steps <= CHUNK)
    sl = sl_ref[0:1, :]
    upper = lax.broadcasted_iota(jnp.int32, (2 * CHUNK, 1), 0) < CHUNK
    slope2 = jnp.where(upper, sl[:, 0:1], sl[:, HEAD_DIM:HEAD_DIM + 1])
    return jnp.where(band, -(float(d) * slope2 * steps.astype(F32)), NEG)


def _stack_heads(xb, head0):
    zero = jnp.zeros_like(xb)
    return jnp.concatenate([jnp.where(head0, xb, zero), jnp.where(head0, zero, xb)], axis=0).astype(BF16)


def _unstack_heads(x2, head0):
    return jnp.where(head0, x2[:CHUNK, :], x2[CHUNK:, :])


def _attn_fwd(q, k, v, shards=()):
    T = q.shape[0]
    nt = T // ATT_TILE
    ns = len(shards)
    steps = (ATTN_W // LANES) * nt

    def body(sl_ref, q_hbm, k_hbm, v_hbm, *rest):
        x_refs, rest = rest[:ns], rest[ns:]
        attn_hbm, lse_hbm = rest[:2]
        g_refs, rest = rest[2:2 + ns], rest[2 + ns:]
        qbuf, kbuf, vbuf, obuf, lbuf = rest[:5]
        o_acc, l_acc = rest[5:8], rest[8:11]
        sem_q, sem_k, sem_v, sem_o, sem_l = rest[11:16]
        hp, t = pl.program_id(0), pl.program_id(1)
        step = hp * nt + t
        two, three = step % 2, step % 3
        before, after = (step + 2) % 3, (step + 1) % 3
        if ns:
            start, forward, finish = _gather_phases(x_refs, g_refs, *rest[16:])
            pl.when(step == 0)(start)
            pl.when(step == steps // 2)(forward)

        def fetch(hp_, t_, two_, three_):
            for cp in (_tile_copies(q_hbm, qbuf.at[two_], sem_q.at[two_], hp_, t_)
                       + _tile_copies(k_hbm, kbuf.at[three_], sem_k.at[three_], hp_, t_)
                       + _tile_copies(v_hbm, vbuf.at[three_], sem_v.at[three_], hp_, t_)):
                cp.start()

        @pl.when(step == 0)
        def _():
            kbuf[2] = jnp.zeros((ATT_BLOCKS, CHUNK, LANES), F32)
            vbuf[2] = jnp.zeros((ATT_BLOCKS, CHUNK, LANES), F32)
            fetch(0, 0, 0, 0)

        @pl.when(step + 1 < steps)
        def _():
            fetch((step + 1) // nt, (step + 1) % nt, 1 - two, after)

        _wait_tile(qbuf.at[two], sem_q.at[two])
        _wait_tile(kbuf.at[three], sem_k.at[three])
        _wait_tile(vbuf.at[three], sem_v.at[three])

        @pl.when(step >= 2)
        def _():
            _wait_tile(obuf.at[two], sem_o.at[two])
            _wait_tile(lbuf.at[two], sem_l.at[two])

        q_t, k_t, v_t = qbuf.at[two], kbuf.at[three], vbuf.at[three]
        k_b, v_b = kbuf.at[before], vbuf.at[before]
        head0 = lax.broadcasted_iota(jnp.int32, (CHUNK, LANES), 1) < HEAD_DIM
        no_key_before = jnp.where(lax.broadcasted_iota(jnp.int32, (2 * CHUNK, 2 * CHUNK), 1) < CHUNK, NEG, 0.0)
        for pi, d in enumerate(DILATIONS):
            bias = _residue_bias(sl_ref, d)

            def scores(j, d=d, bias=bias):
                kcat = jnp.concatenate([_rm_block_before(k_t, k_b, d, j), _rm_block(k_t, d, j)], axis=0).astype(BF16)
                vcat = jnp.concatenate([_rm_block_before(v_t, v_b, d, j), _rm_block(v_t, d, j)], axis=0).astype(BF16)
                s = _dot(_stack_heads(_rm_block(q_t, d, j), head0), kcat, NT) + bias
                if _first_in_tile(d, j):
                    s = s + jnp.where(t == 0, 1.0, 0.0) * no_key_before
                return s, vcat

            def output(j, p, vcat, scale, lse, d=d, pi=pi):
                _rm_store(o_acc[pi], d, j, _unstack_heads(_dot(p, vcat) * scale, head0))
                _rm_store(l_acc[pi], d, j, _unstack_heads(jnp.broadcast_to(lse, (2 * CHUNK, LANES)), head0))

            for j0 in range(0, ATT_BLOCKS, SM_BLOCKS):
                group = [scores(j) for j in range(j0, j0 + SM_BLOCKS)]
                s = jnp.concatenate([g[0] for g in group], axis=0)
                m = jnp.max(s, axis=-1, keepdims=True)
                p = jnp.exp(s - m)
                l = jnp.sum(p, axis=-1, keepdims=True)
                p, scale, lse = p.astype(BF16), 1.0 / l, m + jnp.log(l)
                for i, (_, vcat) in enumerate(group):
                    rows = slice(i * 2 * CHUNK, (i + 1) * 2 * CHUNK)
                    output(j0 + i, p[rows, :], vcat, scale[rows, :], lse[rows, :])

        for r in range(ATT_BLOCKS):
            a, b, c = l_acc[0][r], l_acc[1][r], l_acc[2][r]
            m = jnp.maximum(jnp.maximum(a, b), c)
            ea, eb, ec = jnp.exp(a - m), jnp.exp(b - m), jnp.exp(c - m)
            tot = ea + eb + ec
            obuf[two, r] = (ea * o_acc[0][r] + eb * o_acc[1][r] + ec * o_acc[2][r]) / tot
            lbuf[two, r] = m + jnp.log(tot)

        for cp in (_tile_copies(attn_hbm, obuf.at[two], sem_o.at[two], hp, t, to_hbm=True)
                   + _tile_copies(lse_hbm, lbuf.at[two], sem_l.at[two], hp, t, to_hbm=True)):
            cp.start()

        @pl.when(step == steps - 1)
        def _():
            for slot in (two, 1 - two)[:min(steps, 2)]:
                _wait_tile(obuf.at[slot], sem_o.at[slot])
                _wait_tile(lbuf.at[slot], sem_l.at[slot])

        if ns:
            pl.when(step == steps - 1)(finish)

    tile = lambda n: pltpu.VMEM((n, ATT_BLOCKS, CHUNK, LANES), F32)
    dma = lambda n: pltpu.SemaphoreType.DMA((n,))
    view = jax.ShapeDtypeStruct((T // ATT_BLOCKS, ATT_BLOCKS, ATTN_W), F32)
    outs = pl.pallas_call(
        body, name="attn_fwd", grid=(ATTN_W // LANES, nt),
        in_specs=[pl.BlockSpec((8, LANES), lambda c, t: (0, c))] + [_HBM] * (3 + ns),
        out_specs=[_HBM] * (2 + ns),
        out_shape=[view, view] + [_gathered_shape(s) for s in shards],
        scratch_shapes=[tile(2), tile(3), tile(3), tile(2), tile(2)] + [pltpu.VMEM((ATT_BLOCKS, CHUNK, LANES), F32)] * 6
        + [dma(2), dma(3), dma(3), dma(2), dma(2)] + (_gather_sems(ns) if ns else []),
        compiler_params=_params(("arbitrary", "arbitrary")),
    )(_slope_table(), _residue_view(q), _residue_view(k), _residue_view(v), *shards)
    return outs[0].reshape(T, ATTN_W), outs[1].reshape(T, ATTN_W), tuple(outs[2:])


def _group_mean(v, grp):
    out = jnp.zeros_like(v)
    for g in range(N_GROUPS):
        mk = grp == g
        s = jnp.sum(jnp.where(mk, v, 0.0), axis=-1, keepdims=True) * (1.0 / HEAD_DIM)
        out = jnp.where(mk, s, out)
    return out


def _gmlp_core(uu, zz, lg, lb, ws, sb_ref, grp):
    ug, tu = _gelu(uu)
    zg, tz = _gelu(zz)
    zc = zg - _group_mean(zg, grp)
    rstd = lax.rsqrt(_group_mean(zc * zc, grp) + EPS)
    xhat = zc * rstd
    zn16 = (xhat * lg + lb).astype(BF16)
    mixed = []
    for ci in range(uu.shape[0] // CHUNK):
        rows = slice(ci * CHUNK, (ci + 1) * CHUNK)
        m = jnp.zeros((CHUNK, GMLP_W), F32)
        for g in range(N_GROUPS):
            m = jnp.where(grp[:CHUNK] == g, _dot(ws[g], zn16[rows, :]) + sb_ref[:, g:g + 1], m)
        mixed.append(m)
    return ug, tu, tz, xhat, rstd, zn16, jnp.concatenate(mixed, axis=0)


def _causal_ws(w_ref):
    ti = lax.broadcasted_iota(jnp.int32, (CHUNK, CHUNK), 0)
    si = lax.broadcasted_iota(jnp.int32, (CHUNK, CHUNK), 1)
    causal = si <= ti
    return causal, [jnp.where(causal, w_ref[g], 0.0).astype(BF16) for g in range(N_GROUPS)]


def _gmlp_fwd(u, z, ln_g, ln_b, sgu_w, sgu_bt):
    T = u.shape[0]
    tg = TM_GMLP

    def body(u_ref, z_ref, g_ref, b_ref, w_ref, sb_ref, out_ref):
        grp = lax.broadcasted_iota(jnp.int32, (tg, GMLP_W), 1) // HEAD_DIM
        _, ws = _causal_ws(w_ref)
        ug, _, _, _, _, _, mixed = _gmlp_core(u_ref[...], z_ref[...], g_ref[...], b_ref[...], ws, sb_ref, grp)
        out_ref[...] = ug * mixed

    return pl.pallas_call(
        body, name="gmlp_fwd", grid=(T // tg,),
        in_specs=[_rows(tg, GMLP_W), _rows(tg, GMLP_W), _resident((1, GMLP_W)), _resident((1, GMLP_W)),
                  _resident((N_GROUPS, CHUNK, CHUNK)), _resident((CHUNK, N_GROUPS))],
        out_specs=_rows(tg, GMLP_W),
        out_shape=jax.ShapeDtypeStruct((T, GMLP_W), F32),
        compiler_params=_params(("parallel",)),
    )(u, z, ln_g, ln_b, sgu_w, sgu_bt)


def _out_fwd(attn, gm, ga, gg, w_out, x, g2):
    T = x.shape[0]
    tm = TM_PROJ

    def body(a_ref, m_ref, ga_ref, gg_ref, w_ref, x_ref, g2_ref, mix_ref, h1_ref, hn2_ref):
        an, _ = _rms(a_ref[...])
        gn, _ = _rms(m_ref[...])
        an = (an * ga_ref[...]).astype(BF16)
        gn = (gn * gg_ref[...]).astype(BF16)
        mix_ref[:, 0:ATTN_W] = an
        mix_ref[:, ATTN_W:] = gn
        h1 = x_ref[...] + _dot(an, w_ref[0:ATTN_W, :]) + _dot(gn, w_ref[ATTN_W:, :])
        h1_ref[...] = h1
        n2, _ = _rms(h1)
        hn2_ref[...] = (n2 * g2_ref[...]).astype(BF16)

    sds = jax.ShapeDtypeStruct
    return pl.pallas_call(
        body, name="out_fwd", grid=(T // tm,),
        in_specs=[_rows(tm, ATTN_W), _rows(tm, GMLP_W), _resident((1, ATTN_W)), _resident((1, GMLP_W)),
                  _resident((D_MODEL, D_MODEL)), _rows(tm, D_MODEL), _resident((1, D_MODEL))],
        out_specs=[_rows(tm, D_MODEL)] * 3,
        out_shape=[sds((T, D_MODEL), BF16), sds((T, D_MODEL), F32), sds((T, D_MODEL), BF16)],
        compiler_params=_params(("parallel",)),
    )(attn, gm, ga, gg, w_out, x, g2)


def _ffn_fwd(hn2, h1, w1t, w2, gf, tgt):
    T = h1.shape[0]
    tm = TM_FFN

    def body(hn_ref, h1_ref, w1_ref, w2_ref, gf_ref, t_ref, r_ref, dhf_ref, dhb_ref, loss_ref, dgf_ref):
        i = pl.program_id(0)

        @pl.when(i == 0)
        def _():
            loss_ref[...] = jnp.zeros_like(loss_ref)
            dgf_ref[...] = jnp.zeros_like(dgf_ref)

        hn = hn_ref[...]
        acc = h1_ref[...]
        for j in range(D_FF // FF_CHUNK):
            cols = slice(j * FF_CHUNK, (j + 1) * FF_CHUNK)
            r = jnp.maximum(_dot(hn, w1_ref[cols, :], NT), 0.0)
            r_ref[:, cols] = r.astype(BF16)
            act = jnp.square(r).astype(BF16)
            acc = acc + _dot(act, w2_ref[cols, :])
        n3, r3 = _rms(acc)
        gf_row = gf_ref[...]
        e = n3 * gf_row - t_ref[...]
        loss_ref[...] += 0.5 * jnp.sum(jnp.mean(e * e, axis=-1, keepdims=True))
        dy = e * (1.0 / D_MODEL)
        _accum_rows(dgf_ref, dy * n3)
        dh2 = _rms_bwd(n3, r3, gf_row, dy)
        dhf_ref[...] = dh2
        dhb_ref[...] = dh2.astype(BF16)

    sds = jax.ShapeDtypeStruct
    acc_spec = lambda n: pl.BlockSpec((8, n), lambda i: (0, 0))
    return pl.pallas_call(
        body, name="ffn_fwd", grid=(T // tm,),
        in_specs=[_rows(tm, D_MODEL), _rows(tm, D_MODEL), _resident((D_FF, D_MODEL)), _resident((D_FF, D_MODEL)),
                  _resident((1, D_MODEL)), _rows(tm, D_MODEL)],
        out_specs=[_rows(tm, D_FF), _rows(tm, D_MODEL), _rows(tm, D_MODEL), acc_spec(LANES), acc_spec(D_MODEL)],
        out_shape=[sds((T, D_FF), BF16), sds((T, D_MODEL), F32), sds((T, D_MODEL), BF16),
                   sds((8, LANES), F32), sds((8, D_MODEL), F32)],
        compiler_params=_params(("arbitrary",)),
    )(hn2, h1, w1t, w2, gf, tgt)


def _ffn_bwd(dh2b, dh2f, relu, h1, g2, w2, w1t):
    T = h1.shape[0]
    tm = TM_FFN

    def body(db_ref, df_ref, r_ref, h1_ref, g2_ref, w2_ref, w1t_ref, da_ref, d1f_ref, d1b_ref, dg_ref):
        @pl.when(pl.program_id(0) == 0)
        def _():
            dg_ref[...] = jnp.zeros_like(dg_ref)

        db = db_ref[...]
        acc = jnp.zeros((tm, D_MODEL), F32)
        for j in range(D_FF // FF_CHUNK):
            cols = slice(j * FF_CHUNK, (j + 1) * FF_CHUNK)
            da = (_dot(db, w2_ref[cols, :], NT) * (2.0 * r_ref[:, cols].astype(F32))).astype(BF16)
            da_ref[:, cols] = da
            acc = acc + _dot(da, w1t_ref[cols, :])
        n2, r2 = _rms(h1_ref[...])
        _accum_rows(dg_ref, acc * n2)
        dh1 = df_ref[...] + _rms_bwd(n2, r2, g2_ref[...], acc)
        d1f_ref[...] = dh1
        d1b_ref[...] = dh1.astype(BF16)

    sds = jax.ShapeDtypeStruct
    return pl.pallas_call(
        body, name="ffn_bwd", grid=(T // tm,),
        in_specs=[_rows(tm, D_MODEL), _rows(tm, D_MODEL), _rows(tm, D_FF), _rows(tm, D_MODEL),
                  _resident((1, D_MODEL)), _resident((D_FF, D_MODEL)), _resident((D_FF, D_MODEL))],
        out_specs=[_rows(tm, D_FF), _rows(tm, D_MODEL), _rows(tm, D_MODEL),
                   pl.BlockSpec((8, D_MODEL), lambda i: (0, 0))],
        out_shape=[sds((T, D_FF), BF16), sds((T, D_MODEL), F32), sds((T, D_MODEL), BF16), sds((8, D_MODEL), F32)],
        compiler_params=_params(("arbitrary",)),
    )(dh2b, dh2f, relu, h1, g2, w2, w1t)


def _out_bwd(dh1b, w_out, attn, gm, ga, gg):
    T = attn.shape[0]
    tm = TM_PROJ

    def body(d_ref, w_ref, a_ref, m_ref, ga_ref, gg_ref, da_ref, dm_ref, dga_ref, dgg_ref):
        @pl.when(pl.program_id(0) == 0)
        def _():
            dga_ref[...] = jnp.zeros_like(dga_ref)
            dgg_ref[...] = jnp.zeros_like(dgg_ref)

        d = d_ref[...]
        dan = _dot(d, w_ref[0:ATTN_W, :], NT)
        dgn = _dot(d, w_ref[ATTN_W:, :], NT)
        na, ra = _rms(a_ref[...])
        ng, rg = _rms(m_ref[...])
        _accum_rows(dga_ref, dan * na)
        _accum_rows(dgg_ref, dgn * ng)
        da_ref[...] = _rms_bwd(na, ra, ga_ref[...], dan)
        dm_ref[...] = _rms_bwd(ng, rg, gg_ref[...], dgn)

    sds = jax.ShapeDtypeStruct
    return pl.pallas_call(
        body, name="out_bwd", grid=(T // tm,),
        in_specs=[_rows(tm, D_MODEL), _resident((D_MODEL, D_MODEL)), _rows(tm, ATTN_W), _rows(tm, GMLP_W),
                  _resident((1, ATTN_W)), _resident((1, GMLP_W))],
        out_specs=[_rows(tm, ATTN_W), _rows(tm, GMLP_W), pl.BlockSpec((8, ATTN_W), lambda i: (0, 0)),
                   pl.BlockSpec((8, GMLP_W), lambda i: (0, 0))],
        out_shape=[sds((T, ATTN_W), F32), sds((T, GMLP_W), F32), sds((8, ATTN_W), F32), sds((8, GMLP_W), F32)],
        compiler_params=_params(("arbitrary",)),
    )(dh1b, w_out, attn, gm, ga, gg)


def _gmlp_bwd(u, z, dgm, ln_g, ln_b, sgu_w, sgu_bt):
    T = u.shape[0]
    tg = TM_GMLP
    nsteps = T // tg

    def body(u_ref, z_ref, d_ref, g_ref, b_ref, w_ref, sb_ref, dproj_hbm, dlg_ref, dlb_ref, dw_ref, dsb_ref,
             stage, sem):
        i = pl.program_id(0)
        slot = i % 2
        duz_ref = stage.at[slot]

        def to_dproj(step, buf):
            rows = pl.ds(pl.multiple_of(step * tg, tg), tg)
            return pltpu.make_async_copy(stage.at[buf], dproj_hbm.at[rows, pl.ds(3 * ATTN_W, 2 * GMLP_W)],
                                         sem.at[buf])

        @pl.when(i == 0)
        def _():
            for ref in (dlg_ref, dlb_ref, dw_ref, dsb_ref):
                ref[...] = jnp.zeros_like(ref)

        @pl.when(i >= 2)
        def _():
            to_dproj(i - 2, slot).wait()

        grp = lax.broadcasted_iota(jnp.int32, (tg, GMLP_W), 1) // HEAD_DIM
        lane = lax.broadcasted_iota(jnp.int32, (CHUNK, LANES), 1)
        causal, ws = _causal_ws(w_ref)
        lg = g_ref[...]
        uu, zz, dgm = u_ref[...], z_ref[...], d_ref[...]
        ug, tu, tz, xhat, rstd, zn16, mixed = _gmlp_core(uu, zz, lg, b_ref[...], ws, sb_ref, grp)
        dmx = dgm * ug
        duz_ref[:, 0:GMLP_W] = dgm * mixed * _gelu_grad(uu, tu)
        dmx16 = dmx.astype(BF16)
        dzn = []
        for ci in range(tg // CHUNK):
            rows = slice(ci * CHUNK, (ci + 1) * CHUNK)
            dmx_c, d = dmx16[rows, :], jnp.zeros((CHUNK, GMLP_W), F32)
            for g in range(N_GROUPS):
                mk = grp[:CHUNK] == g
                d = jnp.where(mk, _dot(ws[g], dmx_c, TN), d)
                dw_ref[g] += _dot(jnp.where(mk, dmx_c, jnp.zeros_like(dmx_c)), zn16[rows, :], NT)
            dzn.append(d)
        dzn = jnp.concatenate(dzn, axis=0)
        dsb = jnp.zeros((CHUNK, LANES), F32)
        for g in range(N_GROUPS):
            per_token = jnp.sum(jnp.where(grp == g, dmx, 0.0), axis=-1, keepdims=True)
            by_position = sum(per_token[ci * CHUNK:(ci + 1) * CHUNK] for ci in range(tg // CHUNK))
            dsb = jnp.where(lane == g, by_position, dsb)
        dsb_ref[...] += dsb
        _accum_rows(dlg_ref, dzn * xhat)
        _accum_rows(dlb_ref, dzn)
        dxh = dzn * lg
        dzg = rstd * (dxh - _group_mean(dxh, grp) - xhat * _group_mean(dxh * xhat, grp))
        duz_ref[:, GMLP_W:] = dzg * _gelu_grad(zz, tz)
        to_dproj(i, slot).start()

        @pl.when(i == nsteps - 1)
        def _():
            for g in range(N_GROUPS):
                dw_ref[g] = jnp.where(causal, dw_ref[g], 0.0)
            to_dproj(i, slot).wait()
            if nsteps >= 2:
                to_dproj(i - 1, 1 - slot).wait()

    sds = jax.ShapeDtypeStruct
    return pl.pallas_call(
        body, name="gmlp_bwd", grid=(nsteps,),
        in_specs=[_rows(tg, GMLP_W)] * 3 + [_resident((1, GMLP_W)), _resident((1, GMLP_W)),
                                              _resident((N_GROUPS, CHUNK, CHUNK)), _resident((CHUNK, N_GROUPS))],
        out_specs=[_HBM, pl.BlockSpec((8, GMLP_W), lambda i: (0, 0)),
                   pl.BlockSpec((8, GMLP_W), lambda i: (0, 0)),
                   pl.BlockSpec((N_GROUPS, CHUNK, CHUNK), lambda i: (0, 0, 0)),
                   pl.BlockSpec((CHUNK, LANES), lambda i: (0, 0))],
        out_shape=[sds((T, IN_W), F32), sds((8, GMLP_W), F32), sds((8, GMLP_W), F32),
                   sds((N_GROUPS, CHUNK, CHUNK), F32), sds((CHUNK, LANES), F32)],
        scratch_shapes=[pltpu.VMEM((2, tg, 2 * GMLP_W), F32), pltpu.SemaphoreType.DMA((2,))],
        compiler_params=_params(("arbitrary",)),
    )(u, z, dgm, ln_g, ln_b, sgu_w, sgu_bt)


def _attn_bwd(q, k, v, dattn, attn, lse, dproj, owner_grads=()):
    T = q.shape[0]
    nt = T // ATT_TILE
    ns = len(owner_grads)
    steps = (ATTN_W // LANES) * nt

    def body(sl_ref, q_hbm, k_hbm, v_hbm, do_hbm, o_hbm, lse_hbm, _, *rest):
        p_refs, rest = rest[:ns], rest[ns:]
        dq_hbm = dk_hbm = dv_hbm = rest[0]
        r_refs, rest = rest[1:1 + ns], rest[1 + ns:]
        qbuf, dobuf, obuf, lbuf, kbuf, vbuf, dqbuf, dkbuf, dvbuf, delta_s = rest[:10]
        sem_q, sem_do, sem_o, sem_l, sem_k, sem_v, sem_dq, sem_dk, sem_dv = rest[10:19]
        hp, t = pl.program_id(0), pl.program_id(1)
        step = hp * nt + t
        two, three = step % 2, step % 3
        before, after = (step + 2) % 3, (step + 1) % 3
        if ns:
            start, finish = _owner_exchange_phases(p_refs, r_refs, *rest[19:])
            pl.when(step == 0)(start)

        def fetch(hp_, t_, two_, three_):
            for hbm, buf, sem, slot in ((q_hbm, qbuf, sem_q, two_), (do_hbm, dobuf, sem_do, two_),
                                        (o_hbm, obuf, sem_o, two_), (lse_hbm, lbuf, sem_l, two_),
                                        (k_hbm, kbuf, sem_k, three_), (v_hbm, vbuf, sem_v, three_)):
                for cp in _tile_copies(hbm, buf.at[slot], sem.at[slot], hp_, t_):
                    cp.start()

        @pl.when(step == 0)
        def _():
            kbuf[2] = jnp.zeros((ATT_BLOCKS, CHUNK, LANES), F32)
            vbuf[2] = jnp.zeros((ATT_BLOCKS, CHUNK, LANES), F32)
            dkbuf[3] = jnp.zeros((ATT_BLOCKS, CHUNK, LANES), F32)
            dvbuf[3] = jnp.zeros((ATT_BLOCKS, CHUNK, LANES), F32)
            fetch(0, 0, 0, 0)

        @pl.when(step + 1 < steps)
        def _():
            fetch((step + 1) // nt, (step + 1) % nt, 1 - two, after)

        for buf, sem in ((qbuf, sem_q), (dobuf, sem_do), (obuf, sem_o), (lbuf, sem_l)):
            _wait_tile(buf.at[two], sem.at[two])
        _wait_tile(kbuf.at[three], sem_k.at[three])
        _wait_tile(vbuf.at[three], sem_v.at[three])

        @pl.when(step >= 2)
        def _():
            _wait_tile(dqbuf.at[two], sem_dq.at[two])

        @pl.when(step >= 3)
        def _():
            _wait_tile(dkbuf.at[three], sem_dk.at[three])
            _wait_tile(dvbuf.at[three], sem_dv.at[three])

        zero_tile = jnp.zeros((ATT_BLOCKS, CHUNK, LANES), F32)
        dqbuf[two] = zero_tile
        dkbuf[three] = zero_tile
        dvbuf[three] = zero_tile

        q_t, do_t, l_t, k_t, v_t = qbuf.at[two], dobuf.at[two], lbuf.at[two], kbuf.at[three], vbuf.at[three]
        k_b, v_b = kbuf.at[before], vbuf.at[before]
        dq_t, dk_t, dv_t = dqbuf.at[two], dkbuf.at[three], dvbuf.at[three]
        dk_b, dv_b = dkbuf.at[before], dvbuf.at[before]
        sink = jnp.where(t > 0, before, 3)
        dk_sink, dv_sink = dkbuf.at[sink], dvbuf.at[sink]
        head0 = lax.broadcasted_iota(jnp.int32, (CHUNK, LANES), 1) < HEAD_DIM
        for r in range(ATT_BLOCKS):
            dd = dobuf[two, r] * obuf[two, r]
            d0 = jnp.sum(jnp.where(head0, dd, 0.0), axis=-1, keepdims=True)
            d1 = jnp.sum(jnp.where(head0, 0.0, dd), axis=-1, keepdims=True)
            delta_s[r] = jnp.where(head0, d0, d1)

        def column(xb):
            return jnp.concatenate([xb[:, 0:1], xb[:, HEAD_DIM:HEAD_DIM + 1]], axis=0)

        no_key_before = jnp.where(lax.broadcasted_iota(jnp.int32, (2 * CHUNK, 2 * CHUNK), 1) < CHUNK, NEG, 0.0)
        for d in DILATIONS:
            bias = _residue_bias(sl_ref, d)
            for j in range(ATT_BLOCKS):
                kcat = jnp.concatenate([_rm_block_before(k_t, k_b, d, j), _rm_block(k_t, d, j)], axis=0).astype(BF16)
                vcat = jnp.concatenate([_rm_block_before(v_t, v_b, d, j), _rm_block(v_t, d, j)], axis=0).astype(BF16)
                q2 = _stack_heads(_rm_block(q_t, d, j), head0)
                do2 = _stack_heads(_rm_block(do_t, d, j), head0)
                s = _dot(q2, kcat, NT) + bias
                if _first_in_tile(d, j):
                    s = s + jnp.where(t == 0, 1.0, 0.0) * no_key_before
                p = jnp.exp(s - column(_rm_block(l_t, d, j)))
                ds = (p * (_dot(do2, vcat, NT) - column(_rm_block(delta_s, d, j)))).astype(BF16)
                _rm_add(dq_t, _residue_rows(d, j), _unstack_heads(_dot(ds, kcat), head0))
                ck = _dot(ds, q2, TN)
                cv = _dot(p.astype(BF16), do2, TN)
                _rm_add(dk_t, _residue_rows(d, j), ck[CHUNK:, :])
                _rm_add(dv_t, _residue_rows(d, j), cv[CHUNK:, :])
                if _first_in_tile(d, j):
                    rows = [(r, CHUNK - n, n) for r, _, n in _residue_rows(d, j)]
                    _rm_add(dk_sink, rows, ck[:CHUNK, :])
                    _rm_add(dv_sink, rows, cv[:CHUNK, :])
                else:
                    rows = [(r, lo - n, n) for r, lo, n in _residue_rows(d, j)]
                    _rm_add(dk_t, rows, ck[:CHUNK, :])
                    _rm_add(dv_t, rows, cv[:CHUNK, :])

        for r in range(ATT_BLOCKS):
            dqbuf[two, r] = dqbuf[two, r] * Q_SCALE
        for cp in _tile_copies(dq_hbm, dq_t, sem_dq.at[two], hp, t, to_hbm=True):
            cp.start()

        @pl.when(t > 0)
        def _():
            for cp in (_tile_copies(dk_hbm, dk_b, sem_dk.at[before], hp, t - 1, to_hbm=True, lane0=ATTN_W)
                       + _tile_copies(dv_hbm, dv_b, sem_dv.at[before], hp, t - 1, to_hbm=True, lane0=2 * ATTN_W)):
                cp.start()

        @pl.when(t == nt - 1)
        def _():
            for cp in (_tile_copies(dk_hbm, dk_t, sem_dk.at[three], hp, t, to_hbm=True, lane0=ATTN_W)
                       + _tile_copies(dv_hbm, dv_t, sem_dv.at[three], hp, t, to_hbm=True, lane0=2 * ATTN_W)):
                cp.start()

        @pl.when(step == steps - 1)
        def _():
            for slot in range(2):
                _wait_tile(dqbuf.at[slot], sem_dq.at[slot])
            for slot in range(3):
                _wait_tile(dkbuf.at[slot], sem_dk.at[slot])
                _wait_tile(dvbuf.at[slot], sem_dv.at[slot])

        if ns:
            pl.when(step == steps - 1)(finish)

    tile = lambda n: pltpu.VMEM((n, ATT_BLOCKS, CHUNK, LANES), F32)
    dma = lambda n: pltpu.SemaphoreType.DMA((n,))
    view = jax.ShapeDtypeStruct((T // ATT_BLOCKS, ATT_BLOCKS, ATTN_W), F32)
    outs = pl.pallas_call(
        body, name="attn_bwd", grid=(ATTN_W // LANES, nt),
        in_specs=[pl.BlockSpec((8, LANES), lambda c, t: (0, c))] + [_HBM] * (7 + ns),
        out_specs=[_HBM] * (1 + ns),
        out_shape=[jax.ShapeDtypeStruct((T // ATT_BLOCKS, ATT_BLOCKS, IN_W), F32)]
        + [jax.ShapeDtypeStruct(p.shape, p.dtype) for p in owner_grads],
        scratch_shapes=[tile(2), tile(2), tile(2), tile(2), tile(3), tile(3), tile(2), tile(4), tile(4),
                        pltpu.VMEM((ATT_BLOCKS, CHUNK, LANES), F32)]
        + [dma(2), dma(2), dma(2), dma(2), dma(3), dma(3), dma(2), dma(3), dma(3)]
        + (_owner_exchange_sems(ns) if ns else []),
        input_output_aliases={7: 0},
        compiler_params=_params(("arbitrary", "arbitrary")),
    )(_slope_table(), *[_residue_view(a) for a in (q, k, v, dattn, attn, lse, dproj)], *owner_grads)
    return outs[0].reshape(T, IN_W), tuple(outs[1:])


def _proj_bwd(dproj, w_in_t, x, g1, dh1, chip_sums=()):
    T = x.shape[0]
    tm = TM_PROJ
    ns = len(chip_sums)
    steps = T // tm

    def body(d_ref, w_ref, x_ref, g_ref, r_ref, *rest):
        p_refs, rest = rest[:ns], rest[ns:]
        dx_ref, dg_ref = rest[:2]
        r_refs, sems = rest[2:2 + ns], rest[2 + ns:]
        step = pl.program_id(0)
        if ns:
            start, finish = _chip_exchange_phases(p_refs, r_refs, *sems)
            pl.when(step == 0)(start)

        @pl.when(step == 0)
        def _():
            dg_ref[...] = jnp.zeros_like(dg_ref)

        dhn = _dot(d_ref[...].astype(BF16), w_ref[...])
        n1, r1 = _rms(x_ref[...])
        _accum_rows(dg_ref, dhn * n1)
        dx_ref[...] = r_ref[...] + _rms_bwd(n1, r1, g_ref[...], dhn)
        if ns:
            pl.when(step == steps - 1)(finish)

    outs = pl.pallas_call(
        body, name="proj_bwd", grid=(steps,),
        in_specs=[_rows(tm, IN_W), _resident((IN_W, D_MODEL)), _rows(tm, D_MODEL), _resident((1, D_MODEL)),
                  _rows(tm, D_MODEL)] + [_HBM] * ns,
        out_specs=[_rows(tm, D_MODEL), pl.BlockSpec((8, D_MODEL), lambda i: (0, 0))] + [_HBM] * ns,
        out_shape=[jax.ShapeDtypeStruct((T, D_MODEL), F32), jax.ShapeDtypeStruct((8, D_MODEL), F32)]
        + [jax.ShapeDtypeStruct(p.shape, p.dtype) for p in chip_sums],
        scratch_shapes=_chip_exchange_sems(ns) if ns else [],
        compiler_params=_params(("arbitrary",)),
    )(dproj, w_in_t, x, g1, dh1, *chip_sums)
    return outs[0], outs[1], tuple(outs[2:])


def _dw(a, b, name, tile, square_a=False, out_dtype=F32):
    T, ka = a.shape
    nb = b.shape[1]
    tka, tnb, tt = tile
    tt = min(tt, T)
    last = T // tt - 1

    def body(a_ref, b_ref, *refs):
        o_ref = refs[0]
        acc_ref = refs[1] if len(refs) > 1 else o_ref
        s = pl.program_id(2)

        @pl.when(s == 0)
        def _():
            acc_ref[...] = jnp.zeros_like(acc_ref)

        a_tile = a_ref[...]
        if square_a:
            a_tile = jnp.square(a_tile.astype(F32))
        acc_ref[...] += _dot(a_tile.astype(BF16), b_ref[...], TN)
        if acc_ref is not o_ref:
            @pl.when(s == last)
            def _():
                o_ref[...] = acc_ref[...].astype(out_dtype)

    return pl.pallas_call(
        body, name=name, grid=(ka // tka, nb // tnb, T // tt),
        in_specs=[pl.BlockSpec((tt, tka), lambda i, j, s: (s, i)), pl.BlockSpec((tt, tnb), lambda i, j, s: (s, j))],
        out_specs=pl.BlockSpec((tka, tnb), lambda i, j, s: (i, j)),
        out_shape=jax.ShapeDtypeStruct((ka, nb), out_dtype),
        scratch_shapes=[] if out_dtype == F32 else [pltpu.VMEM((tka, tnb), F32)],
        compiler_params=_params(("parallel", "parallel", "arbitrary")),
    )(a, b)


def _adamw_update(w, m, v, g):
    m2 = ADAM_B1 * m + (1.0 - ADAM_B1) * g
    v2 = ADAM_B2 * v + (1.0 - ADAM_B2) * jnp.square(g)
    m_hat = m2 / (1.0 - ADAM_B1 ** ADAM_STEP)
    v_hat = v2 / (1.0 - ADAM_B2 ** ADAM_STEP)
    return -ADAM_LR * (m_hat / (jnp.sqrt(v_hat) + ADAM_EPS) + ADAM_WD * w), m2, v2


def _adamw_tiny(ws, ms, vs, parts):
    n = len(ws)
    P = parts.shape[0]

    def body(*refs):
        w_refs, m_refs, v_refs, p_ref = refs[:n], refs[n:2 * n], refs[2 * n:3 * n], refs[3 * n]
        outs = refs[3 * n + 1:]

        def total(slot, rows):
            g = p_ref[0, 8 * slot:8 * slot + rows, :]
            for i in range(1, P):
                g = g + p_ref[i, 8 * slot:8 * slot + rows, :]
            return g

        for k in range(n):
            g = total(k, ws[k].shape[0])
            outs[4 * k][...] = g
            outs[4 * k + 1][...], outs[4 * k + 2][...], outs[4 * k + 3][...] = _adamw_update(
                w_refs[k][...], m_refs[k][...], v_refs[k][...], g)
        outs[4 * n][...] = total(n, 8)

    sds = jax.ShapeDtypeStruct
    return pl.pallas_call(
        body, name="adamw_tiny",
        out_shape=[sds(w.shape, F32) for w in ws for _ in range(4)] + [sds((8, LANES), F32)],
    )(*ws, *ms, *vs, parts)


def _adamw(w, m, v, parts, name, tr, transposed=False):
    R, C = w.shape
    P = parts.shape[0]

    def body(w_ref, m_ref, v_ref, p_ref, g_ref, d_ref, m2_ref, v2_ref):
        g = p_ref[0].astype(F32)
        for i in range(1, P):
            g = g + p_ref[i].astype(F32)
        if transposed:
            g = g.T
        g_ref[...] = g
        d_ref[...], m2_ref[...], v2_ref[...] = _adamw_update(w_ref[...], m_ref[...], v_ref[...], g)

    spec = _rows(tr, C)
    part_spec = (pl.BlockSpec((P, C, tr), lambda i: (0, 0, i)) if transposed
                 else pl.BlockSpec((P, tr, C), lambda i: (0, i, 0)))
    return pl.pallas_call(
        body, name=name, grid=(R // tr,),
        in_specs=[spec, spec, spec, part_spec],
        out_specs=[spec] * 4,
        out_shape=[jax.ShapeDtypeStruct((R, C), F32)] * 4,
        compiler_params=_params(("parallel",)),
    )(w, m, v, parts)


def _pair_sum(core, grad, recv, name):
    _, _, n, C = grad.shape
    tr = n // 2

    def body(c_ref, a_ref, b_ref, o_ref):
        o_ref[...] = a_ref[...] + b_ref[...]

    spec = pl.BlockSpec((1, tr, C), lambda i, j, c_ref: (i, j, 0))
    return pl.pallas_call(
        body, name=name,
        grid_spec=pltpu.PrefetchScalarGridSpec(
            num_scalar_prefetch=1, grid=(4, n // tr),
            in_specs=[pl.BlockSpec((1, None, tr, C), lambda i, j, c_ref: (i, c_ref[0], j, 0)), spec],
            out_specs=spec),
        out_shape=jax.ShapeDtypeStruct(recv.shape, F32),
        compiler_params=_params(("parallel", "parallel")),
    )(core.reshape(1), grad, recv)


_HBM = pl.BlockSpec(memory_space=pltpu.HBM)


def _place():
    return lax.axis_index("x"), lax.axis_index("y"), lax.axis_index("c")


def _gathered_shape(shard):
    return jax.ShapeDtypeStruct((N_DEV,) + shard.shape, shard.dtype)


def _gather_sems(n):
    return [pltpu.SemaphoreType.DMA((7, n)), pltpu.SemaphoreType.DMA((7, n)), pltpu.SemaphoreType.DMA((n,))]


def _gather_phases(x_refs, out_refs, send_sems, recv_sems, local_sems):
    x, y, c = _place()
    me, sibling = (x, y, c), (x, y, 1 - c)
    chips = [(1 - x, y), (x, 1 - y), (1 - x, 1 - y)]
    arrays = range(len(x_refs))

    def slot(i, px, py, pc):
        return out_refs[i].at[4 * px + 2 * py + pc]

    def copy(i, k, block, to, own=False):
        return pltpu.make_async_remote_copy(
            src_ref=x_refs[i] if own else slot(i, *block), dst_ref=slot(i, *block),
            send_sem=send_sems.at[k, i], recv_sem=recv_sems.at[k, i], device_id=to, device_id_type=MESH)

    def mine(i):
        return pltpu.make_async_copy(x_refs[i], slot(i, *me), local_sems.at[i])

    def start():
        for i in arrays:
            mine(i).start()
            copy(i, 0, me, sibling, own=True).start()
            for j, chip in enumerate(chips):
                copy(i, 1 + j, me, (*chip, c), own=True).start()

    def forward():
        for i in arrays:
            for j, chip in enumerate(chips):
                copy(i, 1 + j, (*chip, c), me).wait_recv()
                copy(i, 4 + j, (*chip, c), sibling).start()

    def finish():
        for i in arrays:
            copy(i, 0, sibling, me).wait_recv()
            copy(i, 0, me, sibling, own=True).wait_send()
            for j, chip in enumerate(chips):
                copy(i, 4 + j, (*chip, 1 - c), me).wait_recv()
                copy(i, 1 + j, me, (*chip, c), own=True).wait_send()
                copy(i, 4 + j, (*chip, c), sibling).wait_send()
            mine(i).wait()

    return start, forward, finish


def _all_gather(shards, name):
    n = len(shards)

    def body(*refs):
        start, forward, finish = _gather_phases(refs[:n], refs[n:2 * n], *refs[2 * n:])
        start()
        forward()
        finish()

    return pl.pallas_call(
        body, name=name,
        out_shape=[_gathered_shape(s) for s in shards],
        in_specs=[_HBM] * n, out_specs=[_HBM] * n,
        scratch_shapes=_gather_sems(n),
    )(*shards)


def _sibling_exchange(grads, name):
    n = len(grads)

    def body(*refs):
        g_refs, r_refs, send_sems, recv_sems = refs[:n], refs[n:2 * n], refs[2 * n], refs[2 * n + 1]
        x, y, c = _place()
        copies = [pltpu.make_async_remote_copy(
            src_ref=g_refs[i].at[:, 1 - c], dst_ref=r_refs[i], send_sem=send_sems.at[i], recv_sem=recv_sems.at[i],
            device_id=(x, y, 1 - c), device_id_type=MESH) for i in range(n)]
        for cp in copies:
            cp.start()
        for cp in copies:
            cp.wait()

    return pl.pallas_call(
        body, name=name,
        out_shape=[jax.ShapeDtypeStruct((g.shape[0],) + g.shape[2:], g.dtype) for g in grads],
        in_specs=[_HBM] * n, out_specs=[_HBM] * n,
        scratch_shapes=[pltpu.SemaphoreType.DMA((n,)), pltpu.SemaphoreType.DMA((n,))],
    )(*grads)


def _owner_exchange_sems(n):
    return [pltpu.SemaphoreType.DMA((7, n)), pltpu.SemaphoreType.DMA((7, n)), pltpu.SemaphoreType.DMA((n,))]


def _owner_exchange_phases(g_refs, r_refs, send_sems, recv_sems, local_sems):
    x, y, c = _place()
    me = 4 * x + 2 * y + c
    flip = lambda v, bit: 1 - v if bit else v
    peers = [(flip(x, k & 4), flip(y, k & 2), flip(c, k & 1)) for k in range(1, N_DEV)]
    arrays = range(len(g_refs))

    def mine(i):
        return pltpu.make_async_copy(g_refs[i].at[me], r_refs[i].at[me], local_sems.at[i])

    def copy(i, k, src_slot, dst_slot):
        return pltpu.make_async_remote_copy(
            src_ref=g_refs[i].at[src_slot], dst_ref=r_refs[i].at[dst_slot],
            send_sem=send_sems.at[k, i], recv_sem=recv_sems.at[k, i], device_id=peers[k], device_id_type=MESH)

    def start():
        for i in arrays:
            mine(i).start()
            for k, (px, py, pc) in enumerate(peers):
                copy(i, k, 4 * px + 2 * py + pc, me).start()

    def finish():
        for i in arrays:
            for k, (px, py, pc) in enumerate(peers):
                copy(i, k, me, 4 * px + 2 * py + pc).wait_recv()
                copy(i, k, 4 * px + 2 * py + pc, me).wait_send()
            mine(i).wait()

    return start, finish


def _chip_exchange_sems(n):
    return [pltpu.SemaphoreType.DMA((3, n)), pltpu.SemaphoreType.DMA((3, n)), pltpu.SemaphoreType.DMA((n,))]


def _chip_exchange_phases(p_refs, r_refs, send_sems, recv_sems, local_sems):
    x, y, c = _place()
    my_chip = 2 * x + y
    chips = [(1 - x, y), (x, 1 - y), (1 - x, 1 - y)]
    arrays = range(len(p_refs))

    def mine(i):
        return pltpu.make_async_copy(p_refs[i].at[my_chip], r_refs[i].at[my_chip], local_sems.at[i])

    def copy(i, k, src_chip, dst_chip):
        px, py = chips[k]
        return pltpu.make_async_remote_copy(
            src_ref=p_refs[i].at[src_chip], dst_ref=r_refs[i].at[dst_chip],
            send_sem=send_sems.at[k, i], recv_sem=recv_sems.at[k, i], device_id=(px, py, c), device_id_type=MESH)

    def start():
        for i in arrays:
            mine(i).start()
            for k, (px, py) in enumerate(chips):
                copy(i, k, 2 * px + py, my_chip).start()

    def finish():
        for i in arrays:
            for k, (px, py) in enumerate(chips):
                copy(i, k, my_chip, 2 * px + py).wait_recv()
                copy(i, k, 2 * px + py, my_chip).wait_send()
            mine(i).wait()

    return start, finish


_R_IN, _R_OUT, _R_FF = IN_W // N_DEV, D_MODEL // N_DEV, D_FF // N_DEV


def _by_owner(g):
    return g.reshape(4, 2, g.shape[0] // N_DEV, D_MODEL)


def _local_step(x, tgt, small, w_in_t, rest, core=None):
    exchange = core is not None
    g1, g2, gf = small["norm1_g"], small["norm2_g"], small["final_norm_g"].reshape(1, D_MODEL)
    ga, gg = small["attn_out_g"], small["gmlp_out_g"]
    ln_g = small["sgu_ln_g"].reshape(1, GMLP_W)
    ln_b = small["sgu_ln_b"].reshape(1, GMLP_W)
    sgu_w = small["sgu_w"][0]
    sgu_bt = small["sgu_b"][0].T

    hn1, gathered = _norm1(x, g1, shards=(w_in_t,) if exchange else ())
    if exchange:
        w_in_t = gathered[0].reshape(IN_W, D_MODEL)
    q, k, v, u, z = _proj_fwd(hn1, w_in_t)
    attn, lse, gathered = _attn_fwd(q, k, v, shards=rest if exchange else ())
    w_out, w_ff1_t, w_ff2 = [g.reshape(-1, D_MODEL) for g in gathered] if exchange else rest
    gm = _gmlp_fwd(u, z, ln_g, ln_b, sgu_w, sgu_bt)
    mixed, h1, hn2 = _out_fwd(attn, gm, ga, gg, w_out, x, g2)
    relu, dh2f, dh2b, loss8, dgf8 = _ffn_fwd(hn2, h1, w_ff1_t, w_ff2, gf, tgt)

    da, dh1f, dh1b, dg2 = _ffn_bwd(dh2b, dh2f, relu, h1, g2, w_ff2, w_ff1_t)
    wire = BF16 if exchange else F32
    dw_ff2 = _dw(relu, dh2b, "dw_ff2", DW_TILE, square_a=True, out_dtype=wire)
    dw_ff1_t = _dw(da, hn2, "dw_ff1", DW_TILE, out_dtype=wire)
    dattn, dgm, dga, dgg = _out_bwd(dh1b, w_out, attn, gm, ga, gg)
    dw_out = _dw(mixed, dh1b, "dw_out", DW_TILE, out_dtype=wire)
    early = [dw_out, dw_ff1_t, dw_ff2]
    if exchange:
        early = [g.reshape(N_DEV, -1, D_MODEL) for g in early]
    dproj, dlg, dlb, dsw, dsb = _gmlp_bwd(u, z, dgm, ln_g, ln_b, sgu_w, sgu_bt)
    dproj, arrived = _attn_bwd(q, k, v, dattn, attn, lse, dproj, owner_grads=early if exchange else ())
    dw_in_t = _dw(dproj, hn1, "dw_in", DW_TILE_IN)
    late = ()
    if exchange:
        by_owner = _by_owner(dw_in_t)
        got, = _sibling_exchange([by_owner], "grad_sibling_exchange")
        late = (_pair_sum(core, by_owner, got, "grad_pair_sum"),)
    dx, dg1, late = _proj_bwd(dproj, w_in_t, x, g1, dh1f, chip_sums=late)
    if exchange:
        dw_in_t, early = late[0], arrived

    small_grads = dict(
        norm1_g=dg1[0], sgu_ln_g=dlg[0], sgu_ln_b=dlb[0], sgu_w=dsw, sgu_b=dsb[:, :N_GROUPS].T,
        attn_out_g=dga[0], gmlp_out_g=dgg[0], norm2_g=dg2[0], final_norm_g=dgf8[0])
    return loss8[0, 0], dx, (dw_in_t, *early), small_grads


SMALL_NAMES = ("norm1_g", "sgu_ln_g", "sgu_ln_b", "sgu_w", "sgu_b", "attn_out_g", "gmlp_out_g", "norm2_g",
               "final_norm_g")
WEIGHT_ORDER = ("norm1_g", "w_in", "sgu_ln_g", "sgu_ln_b", "sgu_w", "sgu_b", "attn_out_g", "gmlp_out_g", "w_out",
                "norm2_g", "w_ff1", "w_ff2", "final_norm_g")


TINY_NAMES = tuple(n for n in SMALL_NAMES if n != "sgu_w")


def _as_rows(a):
    return a.reshape(-1, LANES)


def _pack_tiny_grads(d, loss):
    slots = [jnp.pad(_as_rows(d[n]), ((0, 8 - d[n].size // LANES), (0, 0))) for n in TINY_NAMES]
    return jnp.concatenate(slots + [jnp.full((8, LANES), loss, F32)], axis=0)


def kernel(x, norm1_g, w_in, sgu_ln_g, sgu_ln_b, sgu_w, sgu_b, attn_out_g, gmlp_out_g, w_out, norm2_g, w_ff1, w_ff2, final_norm_g, loss_target, m_norm1_g, m_w_in, m_sgu_ln_g, m_sgu_ln_b, m_sgu_w, m_sgu_b, m_attn_out_g, m_gmlp_out_g, m_w_out, m_norm2_g, m_w_ff1, m_w_ff2, m_final_norm_g, v_norm1_g, v_w_in, v_sgu_ln_g, v_sgu_ln_b, v_sgu_w, v_sgu_b, v_attn_out_g, v_gmlp_out_g, v_w_out, v_norm2_g, v_w_ff1, v_w_ff2, v_final_norm_g):
    w = dict(norm1_g=norm1_g, w_in=w_in, sgu_ln_g=sgu_ln_g, sgu_ln_b=sgu_ln_b, sgu_w=sgu_w, sgu_b=sgu_b,
             attn_out_g=attn_out_g, gmlp_out_g=gmlp_out_g, w_out=w_out, norm2_g=norm2_g, w_ff1=w_ff1, w_ff2=w_ff2,
             final_norm_g=final_norm_g)
    m = dict(norm1_g=m_norm1_g, w_in=m_w_in, sgu_ln_g=m_sgu_ln_g, sgu_ln_b=m_sgu_ln_b, sgu_w=m_sgu_w, sgu_b=m_sgu_b,
             attn_out_g=m_attn_out_g, gmlp_out_g=m_gmlp_out_g, w_out=m_w_out, norm2_g=m_norm2_g, w_ff1=m_w_ff1,
             w_ff2=m_w_ff2, final_norm_g=m_final_norm_g)
    v = dict(norm1_g=v_norm1_g, w_in=v_w_in, sgu_ln_g=v_sgu_ln_g, sgu_ln_b=v_sgu_ln_b, sgu_w=v_sgu_w, sgu_b=v_sgu_b,
             attn_out_g=v_attn_out_g, gmlp_out_g=v_gmlp_out_g, w_out=v_w_out, norm2_g=v_norm2_g, w_ff1=v_w_ff1,
             w_ff2=v_w_ff2, final_norm_g=v_final_norm_g)
    big = ("w_in", "w_out", "w_ff1", "w_ff2")
    core = lax.axis_index("c")

    rest = (w_out[0].astype(BF16), w_ff1[0].T.astype(BF16), w_ff2[0].astype(BF16))
    loss, dx, parts, small_grads = _local_step(x[0], loss_target[0], {n: w[n] for n in SMALL_NAMES},
                                               w_in[0].T.astype(BF16), rest, core=core)

    new = {}
    for n, p, transposed, tr in zip(big, parts, (True, False, True, False), (128, 128, 128, 256)):
        new[n] = [a[None] for a in _adamw(w[n][0], m[n][0], v[n][0], p, "adamw_" + n, tr, transposed)]

    tiny_parts, sgu_parts = _all_gather(
        [_pack_tiny_grads(small_grads, loss), _as_rows(small_grads["sgu_w"])], "small_grad_all_gather")
    tiny = _adamw_tiny(*[[_as_rows(src[n]) for n in TINY_NAMES] for src in (w, m, v)], tiny_parts)
    sgu = _adamw(_as_rows(sgu_w), _as_rows(m_sgu_w), _as_rows(v_sgu_w), sgu_parts, "adamw_sgu_w", 512)
    loss = tiny[-1][0, 0]

    outs = []
    for i in range(4):
        d = {n: new[n][i] for n in big}
        d.update({n: tiny[4 * k + i].reshape(w[n].shape) for k, n in enumerate(TINY_NAMES)})
        d["sgu_w"] = sgu[i].reshape(sgu_w.shape)
        outs.extend(d[n] for n in WEIGHT_ORDER)
    return (loss, dx[None], *outs)
```

```python
import functools
import math

import numpy as np
import jax
import jax.numpy as jnp
from jax import lax
from jax.experimental import pallas as pl
from jax.experimental.pallas import tpu as pltpu

F32 = jnp.float32
BF16 = jnp.bfloat16

D_MODEL = 1024
HEAD_DIM = 64
N_HEADS = 12
ATTN_W = N_HEADS * HEAD_DIM
N_GROUPS = 4
GMLP_W = N_GROUPS * HEAD_DIM
IN_W = 3 * ATTN_W + 2 * GMLP_W
D_FF = 4 * D_MODEL
CHUNK = 128
DILATIONS = (1, 4, 16)
EPS = 1e-6
Q_SCALE = HEAD_DIM ** -0.5
NEG = -1e30

ADAM_LR, ADAM_B1, ADAM_B2, ADAM_EPS, ADAM_WD, ADAM_STEP = 0.001, 0.9, 0.999, 1e-08, 0.01, 10

N_DEV = 8
LANES = 128
VMEM_LIMIT = 56 << 20

TM_PROJ = 512
TM_FFN = 512
FF_CHUNK = 512
TM_GMLP = 1024
DW_TILE = (512, 1024, 8192)
DW_TILE_IN = (IN_W // 2, 1024, 2048)

MESH = pl.DeviceIdType.MESH


def _alibi_slopes(n):
    def pow2(m):
        start = 2.0 ** (-8.0 / m)
        return [start ** (i + 1) for i in range(m)]
    c = 2 ** int(math.floor(math.log2(n)))
    s = pow2(n) if c == n else pow2(c) + pow2(2 * c)[0::2][: n - c]
    return np.asarray(s, dtype=np.float32)


SLOPES = _alibi_slopes(N_HEADS)


def _params(sem=None):
    kw = dict(vmem_limit_bytes=VMEM_LIMIT)
    if sem is not None:
        kw["dimension_semantics"] = sem
    return pltpu.CompilerParams(**kw)


def _rows(tm, n):
    return pl.BlockSpec((tm, n), lambda i: (i, 0))


def _resident(shape):
    return pl.BlockSpec(shape, lambda *_: (0,) * len(shape), pipeline_mode=pl.Buffered(1))


def _rms(x):
    r = lax.rsqrt(jnp.mean(x * x, axis=-1, keepdims=True) + EPS)
    return x * r, r


def _rms_bwd(n, r, g, dy):
    dn = dy * g
    return r * (dn - n * jnp.mean(dn * n, axis=-1, keepdims=True))


def _accum_rows(acc_ref, v):
    acc_ref[...] += jnp.broadcast_to(jnp.sum(v, axis=0, keepdims=True), acc_ref.shape)


_G0 = math.sqrt(2.0 / math.pi)
_G1 = 0.044715


def _gelu(x):
    t = jnp.tanh(_G0 * (x + _G1 * (x * x * x)))
    return x * (0.5 * (1.0 + t)), t


def _gelu_grad(x, t):
    return 0.5 * (1.0 + t) + 0.5 * x * (1.0 - t * t) * (_G0 * (1.0 + 3.0 * _G1 * x * x))


NT = (((1,), (1,)), ((), ()))
TN = (((0,), (0,)), ((), ()))


def _dot(a, b, dims=None):
    if dims is None:
        return jnp.dot(a, b, preferred_element_type=F32)
    return lax.dot_general(a, b, dims, preferred_element_type=F32)


def _proj_fwd(x, g1, w_in_t):
    T = x.shape[0]
    tm = TM_PROJ

    def body(x_ref, g_ref, w_ref, hn_ref, q_ref, k_ref, v_ref, u_ref, z_ref):
        n, _ = _rms(x_ref[...])
        hn = (n * g_ref[...]).astype(BF16)
        hn_ref[...] = hn
        a = ATTN_W
        q_ref[...] = _dot(hn, w_ref[0:a, :], NT) * Q_SCALE
        k_ref[...] = _dot(hn, w_ref[a:2 * a, :], NT)
        v_ref[...] = _dot(hn, w_ref[2 * a:3 * a, :], NT)
        u_ref[...] = _dot(hn, w_ref[3 * a:3 * a + GMLP_W, :], NT)
        z_ref[...] = _dot(hn, w_ref[3 * a + GMLP_W:, :], NT)

    sds = jax.ShapeDtypeStruct
    return pl.pallas_call(
        body, name="proj_fwd", grid=(T // tm,),
        in_specs=[_rows(tm, D_MODEL), _resident((1, D_MODEL)), _resident((IN_W, D_MODEL))],
        out_specs=[_rows(tm, D_MODEL), _rows(tm, ATTN_W), _rows(tm, ATTN_W), _rows(tm, ATTN_W),
                   _rows(tm, GMLP_W), _rows(tm, GMLP_W)],
        out_shape=[sds((T, D_MODEL), BF16), sds((T, ATTN_W), F32), sds((T, ATTN_W), F32),
                   sds((T, ATTN_W), F32), sds((T, GMLP_W), F32), sds((T, GMLP_W), F32)],
        compiler_params=_params(("parallel",)),
    )(x, g1, w_in_t)


ATT_TILE = 2048
ATT_BLOCKS = ATT_TILE // CHUNK
SM_BLOCKS = 4


def _slope_table():
    row = np.repeat(SLOPES, HEAD_DIM)
    return jnp.asarray(np.broadcast_to(row[None], (8, ATTN_W)), F32)


def _residue_view(a):
    return a.reshape(a.shape[0] // ATT_BLOCKS, ATT_BLOCKS, a.shape[1])


def _tile_copies(hbm, buf, sem, hp, t, to_hbm=False, lane0=0):
    rows = pl.ds(pl.multiple_of(t * CHUNK, CHUNK), CHUNK)
    lanes = pl.ds(pl.multiple_of(lane0 + hp * LANES, LANES), LANES)
    pairs = [(hbm.at[rows, r, lanes], buf.at[r]) for r in range(ATT_BLOCKS)]
    return [pltpu.make_async_copy(v, h, sem) if to_hbm else pltpu.make_async_copy(h, v, sem) for h, v in pairs]


def _wait_tile(buf, sem):
    pltpu.make_async_copy(buf, buf, sem).wait()


def _residue_rows(d, j):
    if d == 16:
        return [(j, 0, CHUNK)]
    if d == 4:
        return [(j % 4 + 4 * m, 32 * (j // 4), 32) for m in range(4)]
    return [(r, 8 * j, 8) for r in range(ATT_BLOCKS)]


def _block_order(p, d):
    if d == 16:
        return p
    if d == 4:
        return 4 * (p & 31) + (p >> 5)
    return 16 * (p & 7) + (p >> 3)


def _first_in_tile(d, j):
    return _residue_rows(d, j)[0][1] == 0


def _rm_block(buf, d, j):
    return jnp.concatenate([buf[r, lo:lo + n, :] for r, lo, n in _residue_rows(d, j)], axis=0)


def _rm_block_before(buf, buf_before, d, j):
    if _first_in_tile(d, j):
        return jnp.concatenate([buf_before[r, CHUNK - n:CHUNK, :] for r, _, n in _residue_rows(d, j)], axis=0)
    return jnp.concatenate([buf[r, lo - n:lo, :] for r, lo, n in _residue_rows(d, j)], axis=0)


def _rm_store(buf, d, j, val):
    at = 0
    for r, lo, n in _residue_rows(d, j):
        buf[r, lo:lo + n, :] = val[at:at + n, :]
        at += n


def _rm_add(buf, rows, val):
    at = 0
    for r, lo, n in rows:
        buf[r, lo:lo + n, :] += val[at:at + n, :]
        at += n


def _residue_bias(sl_ref, d):
    shape = (2 * CHUNK, 2 * CHUNK)
    row = lax.broadcasted_iota(jnp.int32, shape, 0)
    col = lax.broadcasted_iota(jnp.int32, shape, 1)
    steps = _block_order(row & (CHUNK - 1), d) + CHUNK - (_block_order(col & (CHUNK - 1), d) + (col & CHUNK))
    band = (steps >= 0) & (steps <= CHUNK)
    sl = sl_ref[0:1, :]
    upper = lax.broadcasted_iota(jnp.int32, (2 * CHUNK, 1), 0) < CHUNK
    slope2 = jnp.where(upper, sl[:, 0:1], sl[:, HEAD_DIM:HEAD_DIM + 1])
    return jnp.where(band, -(float(d) * slope2 * steps.astype(F32)), NEG)


def _stack_heads(xb, head0):
    zero = jnp.zeros_like(xb)
    return jnp.concatenate([jnp.where(head0, xb, zero), jnp.where(head0, zero, xb)], axis=0).astype(BF16)


def _unstack_heads(x2, head0):
    return jnp.where(head0, x2[:CHUNK, :], x2[CHUNK:, :])


def _attn_fwd(q, k, v, shards=()):
    T = q.shape[0]
    nt = T // ATT_TILE
    ns = len(shards)
    steps = (ATTN_W // LANES) * nt

    def body(sl_ref, q_hbm, k_hbm, v_hbm, *rest):
        x_refs, rest = rest[:ns], rest[ns:]
        attn_hbm, lse_hbm = rest[:2]
        g_refs, rest = rest[2:2 + ns], rest[2 + ns:]
        qbuf, kbuf, vbuf, obuf, lbuf = rest[:5]
        o_acc, l_acc = rest[5:8], rest[8:11]
        sem_q, sem_k, sem_v, sem_o, sem_l = rest[11:16]
        hp, t = pl.program_id(0), pl.program_id(1)
        step = hp * nt + t
        two, three = step % 2, step % 3
        before, after = (step + 2) % 3, (step + 1) % 3
        if ns:
            start, forward, finish = _gather_phases(x_refs, g_refs, *rest[16:])
            pl.when(step == 0)(start)
            pl.when(step == steps // 2)(forward)

        def fetch(hp_, t_, two_, three_):
            for cp in (_tile_copies(q_hbm, qbuf.at[two_], sem_q.at[two_], hp_, t_)
                       + _tile_copies(k_hbm, kbuf.at[three_], sem_k.at[three_], hp_, t_)
                       + _tile_copies(v_hbm, vbuf.at[three_], sem_v.at[three_], hp_, t_)):
                cp.start()

        @pl.when(step == 0)
        def _():
            kbuf[2] = jnp.zeros((ATT_BLOCKS, CHUNK, LANES), F32)
            vbuf[2] = jnp.zeros((ATT_BLOCKS, CHUNK, LANES), F32)
            fetch(0, 0, 0, 0)

        @pl.when(step + 1 < steps)
        def _():
            fetch((step + 1) // nt, (step + 1) % nt, 1 - two, after)

        _wait_tile(qbuf.at[two], sem_q.at[two])
        _wait_tile(kbuf.at[three], sem_k.at[three])
        _wait_tile(vbuf.at[three], sem_v.at[three])

        @pl.when(step >= 2)
        def _():
            _wait_tile(obuf.at[two], sem_o.at[two])
            _wait_tile(lbuf.at[two], sem_l.at[two])

        q_t, k_t, v_t = qbuf.at[two], kbuf.at[three], vbuf.at[three]
        k_b, v_b = kbuf.at[before], vbuf.at[before]
        head0 = lax.broadcasted_iota(jnp.int32, (CHUNK, LANES), 1) < HEAD_DIM
        no_key_before = jnp.where(lax.broadcasted_iota(jnp.int32, (2 * CHUNK, 2 * CHUNK), 1) < CHUNK, NEG, 0.0)
        for pi, d in enumerate(DILATIONS):
            bias = _residue_bias(sl_ref, d)

            def scores(j, d=d, bias=bias):
                kcat = jnp.concatenate([_rm_block_before(k_t, k_b, d, j), _rm_block(k_t, d, j)], axis=0).astype(BF16)
                vcat = jnp.concatenate([_rm_block_before(v_t, v_b, d, j), _rm_block(v_t, d, j)], axis=0).astype(BF16)
                s = _dot(_stack_heads(_rm_block(q_t, d, j), head0), kcat, NT) + bias
                if _first_in_tile(d, j):
                    s = s + jnp.where(t == 0, 1.0, 0.0) * no_key_before
                return s, vcat

            def output(j, p, vcat, scale, lse, d=d, pi=pi):
                _rm_store(o_acc[pi], d, j, _unstack_heads(_dot(p, vcat) * scale, head0))
                _rm_store(l_acc[pi], d, j, _unstack_heads(jnp.broadcast_to(lse, (2 * CHUNK, LANES)), head0))

            for j0 in range(0, ATT_BLOCKS, SM_BLOCKS):
                group = [scores(j) for j in range(j0, j0 + SM_BLOCKS)]
                s = jnp.concatenate([g[0] for g in group], axis=0)
                m = jnp.max(s, axis=-1, keepdims=True)
                p = jnp.exp(s - m)
                l = jnp.sum(p, axis=-1, keepdims=True)
                p, scale, lse = p.astype(BF16), 1.0 / l, m + jnp.log(l)
                for i, (_, vcat) in enumerate(group):
                    rows = slice(i * 2 * CHUNK, (i + 1) * 2 * CHUNK)
                    output(j0 + i, p[rows, :], vcat, scale[rows, :], lse[rows, :])

        for r in range(ATT_BLOCKS):
            a, b, c = l_acc[0][r], l_acc[1][r], l_acc[2][r]
            m = jnp.maximum(jnp.maximum(a, b), c)
            ea, eb, ec = jnp.exp(a - m), jnp.exp(b - m), jnp.exp(c - m)
            tot = ea + eb + ec
            obuf[two, r] = (ea * o_acc[0][r] + eb * o_acc[1][r] + ec * o_acc[2][r]) / tot
            lbuf[two, r] = m + jnp.log(tot)

        for cp in (_tile_copies(attn_hbm, obuf.at[two], sem_o.at[two], hp, t, to_hbm=True)
                   + _tile_copies(lse_hbm, lbuf.at[two], sem_l.at[two], hp, t, to_hbm=True)):
            cp.start()

        @pl.when(step == steps - 1)
        def _():
            for slot in (two, 1 - two)[:min(steps, 2)]:
                _wait_tile(obuf.at[slot], sem_o.at[slot])
                _wait_tile(lbuf.at[slot], sem_l.at[slot])

        if ns:
            pl.when(step == steps - 1)(finish)

    tile = lambda n: pltpu.VMEM((n, ATT_BLOCKS, CHUNK, LANES), F32)
    dma = lambda n: pltpu.SemaphoreType.DMA((n,))
    view = jax.ShapeDtypeStruct((T // ATT_BLOCKS, ATT_BLOCKS, ATTN_W), F32)
    outs = pl.pallas_call(
        body, name="attn_fwd", grid=(ATTN_W // LANES, nt),
        in_specs=[pl.BlockSpec((8, LANES), lambda c, t: (0, c))] + [_HBM] * (3 + ns),
        out_specs=[_HBM] * (2 + ns),
        out_shape=[view, view] + [_gathered_shape(s) for s in shards],
        scratch_shapes=[tile(2), tile(3), tile(3), tile(2), tile(2)] + [pltpu.VMEM((ATT_BLOCKS, CHUNK, LANES), F32)] * 6
        + [dma(2), dma(3), dma(3), dma(2), dma(2)] + (_gather_sems(ns) if ns else []),
        compiler_params=_params(("arbitrary", "arbitrary")),
    )(_slope_table(), _residue_view(q), _residue_view(k), _residue_view(v), *shards)
    return outs[0].reshape(T, ATTN_W), outs[1].reshape(T, ATTN_W), tuple(outs[2:])


def _group_mean(v, grp):
    out = jnp.zeros_like(v)
    for g in range(N_GROUPS):
        mk = grp == g
        s = jnp.sum(jnp.where(mk, v, 0.0), axis=-1, keepdims=True) * (1.0 / HEAD_DIM)
        out = jnp.where(mk, s, out)
    return out


def _gmlp_core(uu, zz, lg, lb, ws, sb_ref, grp):
    ug, tu = _gelu(uu)
    zg, tz = _gelu(zz)
    zc = zg - _group_mean(zg, grp)
    rstd = lax.rsqrt(_group_mean(zc * zc, grp) + EPS)
    xhat = zc * rstd
    zn16 = (xhat * lg + lb).astype(BF16)
    mixed = []
    for ci in range(uu.shape[0] // CHUNK):
        rows = slice(ci * CHUNK, (ci + 1) * CHUNK)
        m = jnp.zeros((CHUNK, GMLP_W), F32)
        for g in range(N_GROUPS):
            m = jnp.where(grp[:CHUNK] == g, _dot(ws[g], zn16[rows, :]) + sb_ref[:, g:g + 1], m)
        mixed.append(m)
    return ug, tu, tz, xhat, rstd, zn16, jnp.concatenate(mixed, axis=0)


def _causal_ws(w_ref):
    ti = lax.broadcasted_iota(jnp.int32, (CHUNK, CHUNK), 0)
    si = lax.broadcasted_iota(jnp.int32, (CHUNK, CHUNK), 1)
    causal = si <= ti
    return causal, [jnp.where(causal, w_ref[g], 0.0).astype(BF16) for g in range(N_GROUPS)]


def _gmlp_fwd(u, z, ln_g, ln_b, sgu_w, sgu_bt):
    T = u.shape[0]
    tg = TM_GMLP

    def body(u_ref, z_ref, g_ref, b_ref, w_ref, sb_ref, out_ref):
        grp = lax.broadcasted_iota(jnp.int32, (tg, GMLP_W), 1) // HEAD_DIM
        _, ws = _causal_ws(w_ref)
        ug, _, _, _, _, _, mixed = _gmlp_core(u_ref[...], z_ref[...], g_ref[...], b_ref[...], ws, sb_ref, grp)
        out_ref[...] = ug * mixed

    return pl.pallas_call(
        body, name="gmlp_fwd", grid=(T // tg,),
        in_specs=[_rows(tg, GMLP_W), _rows(tg, GMLP_W), _resident((1, GMLP_W)), _resident((1, GMLP_W)),
                  _resident((N_GROUPS, CHUNK, CHUNK)), _resident((CHUNK, N_GROUPS))],
        out_specs=_rows(tg, GMLP_W),
        out_shape=jax.ShapeDtypeStruct((T, GMLP_W), F32),
        compiler_params=_params(("parallel",)),
    )(u, z, ln_g, ln_b, sgu_w, sgu_bt)


def _out_fwd(attn, gm, ga, gg, w_out, x, g2):
    T = x.shape[0]
    tm = TM_PROJ

    def body(a_ref, m_ref, ga_ref, gg_ref, w_ref, x_ref, g2_ref, mix_ref, h1_ref, hn2_ref):
        an, _ = _rms(a_ref[...])
        gn, _ = _rms(m_ref[...])
        an = (an * ga_ref[...]).astype(BF16)
        gn = (gn * gg_ref[...]).astype(BF16)
        mix_ref[:, 0:ATTN_W] = an
        mix_ref[:, ATTN_W:] = gn
        h1 = x_ref[...] + _dot(an, w_ref[0:ATTN_W, :]) + _dot(gn, w_ref[ATTN_W:, :])
        h1_ref[...] = h1
        n2, _ = _rms(h1)
        hn2_ref[...] = (n2 * g2_ref[...]).astype(BF16)

    sds = jax.ShapeDtypeStruct
    return pl.pallas_call(
        body, name="out_fwd", grid=(T // tm,),
        in_specs=[_rows(tm, ATTN_W), _rows(tm, GMLP_W), _resident((1, ATTN_W)), _resident((1, GMLP_W)),
                  _resident((D_MODEL, D_MODEL)), _rows(tm, D_MODEL), _resident((1, D_MODEL))],
        out_specs=[_rows(tm, D_MODEL)] * 3,
        out_shape=[sds((T, D_MODEL), BF16), sds((T, D_MODEL), F32), sds((T, D_MODEL), BF16)],
        compiler_params=_params(("parallel",)),
    )(attn, gm, ga, gg, w_out, x, g2)


def _ffn_fwd(hn2, h1, w1t, w2, gf, tgt):
    T = h1.shape[0]
    tm = TM_FFN

    def body(hn_ref, h1_ref, w1_ref, w2_ref, gf_ref, t_ref, r_ref, dhf_ref, dhb_ref, loss_ref, dgf_ref):
        i = pl.program_id(0)

        @pl.when(i == 0)
        def _():
            loss_ref[...] = jnp.zeros_like(loss_ref)
            dgf_ref[...] = jnp.zeros_like(dgf_ref)

        hn = hn_ref[...]
        acc = h1_ref[...]
        for j in range(D_FF // FF_CHUNK):
            cols = slice(j * FF_CHUNK, (j + 1) * FF_CHUNK)
            r = jnp.maximum(_dot(hn, w1_ref[cols, :], NT), 0.0)
            r_ref[:, cols] = r.astype(BF16)
            act = jnp.square(r).astype(BF16)
            acc = acc + _dot(act, w2_ref[cols, :])
        n3, r3 = _rms(acc)
        gf_row = gf_ref[...]
        e = n3 * gf_row - t_ref[...]
        loss_ref[...] += 0.5 * jnp.sum(jnp.mean(e * e, axis=-1, keepdims=True))
        dy = e * (1.0 / D_MODEL)
        _accum_rows(dgf_ref, dy * n3)
        dh2 = _rms_bwd(n3, r3, gf_row, dy)
        dhf_ref[...] = dh2
        dhb_ref[...] = dh2.astype(BF16)

    sds = jax.ShapeDtypeStruct
    acc_spec = lambda n: pl.BlockSpec((8, n), lambda i: (0, 0))
    return pl.pallas_call(
        body, name="ffn_fwd", grid=(T // tm,),
        in_specs=[_rows(tm, D_MODEL), _rows(tm, D_MODEL), _resident((D_FF, D_MODEL)), _resident((D_FF, D_MODEL)),
                  _resident((1, D_MODEL)), _rows(tm, D_MODEL)],
        out_specs=[_rows(tm, D_FF), _rows(tm, D_MODEL), _rows(tm, D_MODEL), acc_spec(LANES), acc_spec(D_MODEL)],
        out_shape=[sds((T, D_FF), BF16), sds((T, D_MODEL), F32), sds((T, D_MODEL), BF16),
                   sds((8, LANES), F32), sds((8, D_MODEL), F32)],
        compiler_params=_params(("arbitrary",)),
    )(hn2, h1, w1t, w2, gf, tgt)


def _ffn_bwd(dh2b, dh2f, relu, h1, g2, w2, w1t):
    T = h1.shape[0]
    tm = TM_FFN

    def body(db_ref, df_ref, r_ref, h1_ref, g2_ref, w2_ref, w1t_ref, da_ref, d1f_ref, d1b_ref, dg_ref):
        @pl.when(pl.program_id(0) == 0)
        def _():
            dg_ref[...] = jnp.zeros_like(dg_ref)

        db = db_ref[...]
        acc = jnp.zeros((tm, D_MODEL), F32)
        for j in range(D_FF // FF_CHUNK):
            cols = slice(j * FF_CHUNK, (j + 1) * FF_CHUNK)
            da = (_dot(db, w2_ref[cols, :], NT) * (2.0 * r_ref[:, cols].astype(F32))).astype(BF16)
            da_ref[:, cols] = da
            acc = acc + _dot(da, w1t_ref[cols, :])
        n2, r2 = _rms(h1_ref[...])
        _accum_rows(dg_ref, acc * n2)
        dh1 = df_ref[...] + _rms_bwd(n2, r2, g2_ref[...], acc)
        d1f_ref[...] = dh1
        d1b_ref[...] = dh1.astype(BF16)

    sds = jax.ShapeDtypeStruct
    return pl.pallas_call(
        body, name="ffn_bwd", grid=(T // tm,),
        in_specs=[_rows(tm, D_MODEL), _rows(tm, D_MODEL), _rows(tm, D_FF), _rows(tm, D_MODEL),
                  _resident((1, D_MODEL)), _resident((D_FF, D_MODEL)), _resident((D_FF, D_MODEL))],
        out_specs=[_rows(tm, D_FF), _rows(tm, D_MODEL), _rows(tm, D_MODEL),
                   pl.BlockSpec((8, D_MODEL), lambda i: (0, 0))],
        out_shape=[sds((T, D_FF), BF16), sds((T, D_MODEL), F32), sds((T, D_MODEL), BF16), sds((8, D_MODEL), F32)],
        compiler_params=_params(("arbitrary",)),
    )(dh2b, dh2f, relu, h1, g2, w2, w1t)


def _out_bwd(dh1b, w_out, attn, gm, ga, gg):
    T = attn.shape[0]
    tm = TM_PROJ

    def body(d_ref, w_ref, a_ref, m_ref, ga_ref, gg_ref, da_ref, dm_ref, dga_ref, dgg_ref):
        @pl.when(pl.program_id(0) == 0)
        def _():
            dga_ref[...] = jnp.zeros_like(dga_ref)
            dgg_ref[...] = jnp.zeros_like(dgg_ref)

        d = d_ref[...]
        dan = _dot(d, w_ref[0:ATTN_W, :], NT)
        dgn = _dot(d, w_ref[ATTN_W:, :], NT)
        na, ra = _rms(a_ref[...])
        ng, rg = _rms(m_ref[...])
        _accum_rows(dga_ref, dan * na)
        _accum_rows(dgg_ref, dgn * ng)
        da_ref[...] = _rms_bwd(na, ra, ga_ref[...], dan)
        dm_ref[...] = _rms_bwd(ng, rg, gg_ref[...], dgn)

    sds = jax.ShapeDtypeStruct
    return pl.pallas_call(
        body, name="out_bwd", grid=(T // tm,),
        in_specs=[_rows(tm, D_MODEL), _resident((D_MODEL, D_MODEL)), _rows(tm, ATTN_W), _rows(tm, GMLP_W),
                  _resident((1, ATTN_W)), _resident((1, GMLP_W))],
        out_specs=[_rows(tm, ATTN_W), _rows(tm, GMLP_W), pl.BlockSpec((8, ATTN_W), lambda i: (0, 0)),
                   pl.BlockSpec((8, GMLP_W), lambda i: (0, 0))],
        out_shape=[sds((T, ATTN_W), F32), sds((T, GMLP_W), F32), sds((8, ATTN_W), F32), sds((8, GMLP_W), F32)],
        compiler_params=_params(("arbitrary",)),
    )(dh1b, w_out, attn, gm, ga, gg)


def _gmlp_bwd(u, z, dgm, ln_g, ln_b, sgu_w, sgu_bt):
    T = u.shape[0]
    tg = TM_GMLP
    nsteps = T // tg

    def body(u_ref, z_ref, d_ref, g_ref, b_ref, w_ref, sb_ref, dproj_hbm, dlg_ref, dlb_ref, dw_ref, dsb_ref,
             stage, sem):
        i = pl.program_id(0)
        slot = i % 2
        duz_ref = stage.at[slot]

        def to_dproj(step, buf):
            rows = pl.ds(pl.multiple_of(step * tg, tg), tg)
            return pltpu.make_async_copy(stage.at[buf], dproj_hbm.at[rows, pl.ds(3 * ATTN_W, 2 * GMLP_W)],
                                         sem.at[buf])

        @pl.when(i == 0)
        def _():
            for ref in (dlg_ref, dlb_ref, dw_ref, dsb_ref):
                ref[...] = jnp.zeros_like(ref)

        @pl.when(i >= 2)
        def _():
            to_dproj(i - 2, slot).wait()

        grp = lax.broadcasted_iota(jnp.int32, (tg, GMLP_W), 1) // HEAD_DIM
        lane = lax.broadcasted_iota(jnp.int32, (CHUNK, LANES), 1)
        causal, ws = _causal_ws(w_ref)
        lg = g_ref[...]
        uu, zz, dgm = u_ref[...], z_ref[...], d_ref[...]
        ug, tu, tz, xhat, rstd, zn16, mixed = _gmlp_core(uu, zz, lg, b_ref[...], ws, sb_ref, grp)
        dmx = dgm * ug
        duz_ref[:, 0:GMLP_W] = dgm * mixed * _gelu_grad(uu, tu)
        dmx16 = dmx.astype(BF16)
        dzn = []
        for ci in range(tg // CHUNK):
            rows = slice(ci * CHUNK, (ci + 1) * CHUNK)
            dmx_c, d = dmx16[rows, :], jnp.zeros((CHUNK, GMLP_W), F32)
            for g in range(N_GROUPS):
                mk = grp[:CHUNK] == g
                d = jnp.where(mk, _dot(ws[g], dmx_c, TN), d)
                dw_ref[g] += _dot(jnp.where(mk, dmx_c, jnp.zeros_like(dmx_c)), zn16[rows, :], NT)
            dzn.append(d)
        dzn = jnp.concatenate(dzn, axis=0)
        dsb = jnp.zeros((CHUNK, LANES), F32)
        for g in range(N_GROUPS):
            per_token = jnp.sum(jnp.where(grp == g, dmx, 0.0), axis=-1, keepdims=True)
            by_position = sum(per_token[ci * CHUNK:(ci + 1) * CHUNK] for ci in range(tg // CHUNK))
            dsb = jnp.where(lane == g, by_position, dsb)
        dsb_ref[...] += dsb
        _accum_rows(dlg_ref, dzn * xhat)
        _accum_rows(dlb_ref, dzn)
        dxh = dzn * lg
        dzg = rstd * (dxh - _group_mean(dxh, grp) - xhat * _group_mean(dxh * xhat, grp))
        duz_ref[:, GMLP_W:] = dzg * _gelu_grad(zz, tz)
        to_dproj(i, slot).start()

        @pl.when(i == nsteps - 1)
        def _():
            for g in range(N_GROUPS):
                dw_ref[g] = jnp.where(causal, dw_ref[g], 0.0)
            to_dproj(i, slot).wait()
            if nsteps >= 2:
                to_dproj(i - 1, 1 - slot).wait()

    sds = jax.ShapeDtypeStruct
    return pl.pallas_call(
        body, name="gmlp_bwd", grid=(nsteps,),
        in_specs=[_rows(tg, GMLP_W)] * 3 + [_resident((1, GMLP_W)), _resident((1, GMLP_W)),
                                              _resident((N_GROUPS, CHUNK, CHUNK)), _resident((CHUNK, N_GROUPS))],
        out_specs=[_HBM, pl.BlockSpec((8, GMLP_W), lambda i: (0, 0)),
                   pl.BlockSpec((8, GMLP_W), lambda i: (0, 0)),
                   pl.BlockSpec((N_GROUPS, CHUNK, CHUNK), lambda i: (0, 0, 0)),
                   pl.BlockSpec((CHUNK, LANES), lambda i: (0, 0))],
        out_shape=[sds((T, IN_W), F32), sds((8, GMLP_W), F32), sds((8, GMLP_W), F32),
                   sds((N_GROUPS, CHUNK, CHUNK), F32), sds((CHUNK, LANES), F32)],
        scratch_shapes=[pltpu.VMEM((2, tg, 2 * GMLP_W), F32), pltpu.SemaphoreType.DMA((2,))],
        compiler_params=_params(("arbitrary",)),
    )(u, z, dgm, ln_g, ln_b, sgu_w, sgu_bt)


def _attn_bwd(q, k, v, dattn, attn, lse, dproj, owner_grads=()):
    T = q.shape[0]
    nt = T // ATT_TILE
    ns = len(owner_grads)
    steps = (ATTN_W // LANES) * nt

    def body(sl_ref, q_hbm, k_hbm, v_hbm, do_hbm, o_hbm, lse_hbm, _, *rest):
        p_refs, rest = rest[:ns], rest[ns:]
        dq_hbm = dk_hbm = dv_hbm = rest[0]
        r_refs, rest = rest[1:1 + ns], rest[1 + ns:]
        qbuf, dobuf, obuf, lbuf, kbuf, vbuf, dqbuf, dkbuf, dvbuf, delta_s = rest[:10]
        sem_q, sem_do, sem_o, sem_l, sem_k, sem_v, sem_dq, sem_dk, sem_dv = rest[10:19]
        hp, t = pl.program_id(0), pl.program_id(1)
        step = hp * nt + t
        two, three = step % 2, step % 3
        before, after = (step + 2) % 3, (step + 1) % 3
        if ns:
            start, finish = _owner_exchange_phases(p_refs, r_refs, *rest[19:])
            pl.when(step == 0)(start)

        def fetch(hp_, t_, two_, three_):
            for hbm, buf, sem, slot in ((q_hbm, qbuf, sem_q, two_), (do_hbm, dobuf, sem_do, two_),
                                        (o_hbm, obuf, sem_o, two_), (lse_hbm, lbuf, sem_l, two_),
                                        (k_hbm, kbuf, sem_k, three_), (v_hbm, vbuf, sem_v, three_)):
                for cp in _tile_copies(hbm, buf.at[slot], sem.at[slot], hp_, t_):
                    cp.start()

        @pl.when(step == 0)
        def _():
            kbuf[2] = jnp.zeros((ATT_BLOCKS, CHUNK, LANES), F32)
            vbuf[2] = jnp.zeros((ATT_BLOCKS, CHUNK, LANES), F32)
            dkbuf[3] = jnp.zeros((ATT_BLOCKS, CHUNK, LANES), F32)
            dvbuf[3] = jnp.zeros((ATT_BLOCKS, CHUNK, LANES), F32)
            fetch(0, 0, 0, 0)

        @pl.when(step + 1 < steps)
        def _():
            fetch((step + 1) // nt, (step + 1) % nt, 1 - two, after)

        for buf, sem in ((qbuf, sem_q), (dobuf, sem_do), (obuf, sem_o), (lbuf, sem_l)):
            _wait_tile(buf.at[two], sem.at[two])
        _wait_tile(kbuf.at[three], sem_k.at[three])
        _wait_tile(vbuf.at[three], sem_v.at[three])

        @pl.when(step >= 2)
        def _():
            _wait_tile(dqbuf.at[two], sem_dq.at[two])

        @pl.when(step >= 3)
        def _():
            _wait_tile(dkbuf.at[three], sem_dk.at[three])
            _wait_tile(dvbuf.at[three], sem_dv.at[three])

        zero_tile = jnp.zeros((ATT_BLOCKS, CHUNK, LANES), F32)
        dqbuf[two] = zero_tile
        dkbuf[three] = zero_tile
        dvbuf[three] = zero_tile

        q_t, do_t, l_t, k_t, v_t = qbuf.at[two], dobuf.at[two], lbuf.at[two], kbuf.at[three], vbuf.at[three]
        k_b, v_b = kbuf.at[before], vbuf.at[before]
        dq_t, dk_t, dv_t = dqbuf.at[two], dkbuf.at[three], dvbuf.at[three]
        dk_b, dv_b = dkbuf.at[before], dvbuf.at[before]
        sink = jnp.where(t > 0, before, 3)
        dk_sink, dv_sink = dkbuf.at[sink], dvbuf.at[sink]
        head0 = lax.broadcasted_iota(jnp.int32, (CHUNK, LANES), 1) < HEAD_DIM
        for r in range(ATT_BLOCKS):
            dd = dobuf[two, r] * obuf[two, r]
            d0 = jnp.sum(jnp.where(head0, dd, 0.0), axis=-1, keepdims=True)
            d1 = jnp.sum(jnp.where(head0, 0.0, dd), axis=-1, keepdims=True)
            delta_s[r] = jnp.where(head0, d0, d1)

        def column(xb):
            return jnp.concatenate([xb[:, 0:1], xb[:, HEAD_DIM:HEAD_DIM + 1]], axis=0)

        no_key_before = jnp.where(lax.broadcasted_iota(jnp.int32, (2 * CHUNK, 2 * CHUNK), 1) < CHUNK, NEG, 0.0)
        for d in DILATIONS:
            bias = _residue_bias(sl_ref, d)
            for j in range(ATT_BLOCKS):
                kcat = jnp.concatenate([_rm_block_before(k_t, k_b, d, j), _rm_block(k_t, d, j)], axis=0).astype(BF16)
                vcat = jnp.concatenate([_rm_block_before(v_t, v_b, d, j), _rm_block(v_t, d, j)], axis=0).astype(BF16)
                q2 = _stack_heads(_rm_block(q_t, d, j), head0)
                do2 = _stack_heads(_rm_block(do_t, d, j), head0)
                s = _dot(q2, kcat, NT) + bias
                if _first_in_tile(d, j):
                    s = s + jnp.where(t == 0, 1.0, 0.0) * no_key_before
                p = jnp.exp(s - column(_rm_block(l_t, d, j)))
                ds = (p * (_dot(do2, vcat, NT) - column(_rm_block(delta_s, d, j)))).astype(BF16)
                _rm_add(dq_t, _residue_rows(d, j), _unstack_heads(_dot(ds, kcat), head0))
                ck = _dot(ds, q2, TN)
                cv = _dot(p.astype(BF16), do2, TN)
                _rm_add(dk_t, _residue_rows(d, j), ck[CHUNK:, :])
                _rm_add(dv_t, _residue_rows(d, j), cv[CHUNK:, :])
                if _first_in_tile(d, j):
                    rows = [(r, CHUNK - n, n) for r, _, n in _residue_rows(d, j)]
                    _rm_add(dk_sink, rows, ck[:CHUNK, :])
                    _rm_add(dv_sink, rows, cv[:CHUNK, :])
                else:
                    rows = [(r, lo - n, n) for r, lo, n in _residue_rows(d, j)]
                    _rm_add(dk_t, rows, ck[:CHUNK, :])
                    _rm_add(dv_t, rows, cv[:CHUNK, :])

        for r in range(ATT_BLOCKS):
            dqbuf[two, r] = dqbuf[two, r] * Q_SCALE
        for cp in _tile_copies(dq_hbm, dq_t, sem_dq.at[two], hp, t, to_hbm=True):
            cp.start()

        @pl.when(t > 0)
        def _():
            for cp in (_tile_copies(dk_hbm, dk_b, sem_dk.at[before], hp, t - 1, to_hbm=True, lane0=ATTN_W)
                       + _tile_copies(dv_hbm, dv_b, sem_dv.at[before], hp, t - 1, to_hbm=True, lane0=2 * ATTN_W)):
                cp.start()

        @pl.when(t == nt - 1)
        def _():
            for cp in (_tile_copies(dk_hbm, dk_t, sem_dk.at[three], hp, t, to_hbm=True, lane0=ATTN_W)
                       + _tile_copies(dv_hbm, dv_t, sem_dv.at[three], hp, t, to_hbm=True, lane0=2 * ATTN_W)):
                cp.start()

        @pl.when(step == steps - 1)
        def _():
            for slot in range(2):
                _wait_tile(dqbuf.at[slot], sem_dq.at[slot])
            for slot in range(3):
                _wait_tile(dkbuf.at[slot], sem_dk.at[slot])
                _wait_tile(dvbuf.at[slot], sem_dv.at[slot])

        if ns:
            pl.when(step == steps - 1)(finish)

    tile = lambda n: pltpu.VMEM((n, ATT_BLOCKS, CHUNK, LANES), F32)
    dma = lambda n: pltpu.SemaphoreType.DMA((n,))
    view = jax.ShapeDtypeStruct((T // ATT_BLOCKS, ATT_BLOCKS, ATTN_W), F32)
    outs = pl.pallas_call(
        body, name="attn_bwd", grid=(ATTN_W // LANES, nt),
        in_specs=[pl.BlockSpec((8, LANES), lambda c, t: (0, c))] + [_HBM] * (7 + ns),
        out_specs=[_HBM] * (1 + ns),
        out_shape=[jax.ShapeDtypeStruct((T // ATT_BLOCKS, ATT_BLOCKS, IN_W), F32)]
        + [jax.ShapeDtypeStruct(p.shape, p.dtype) for p in owner_grads],
        scratch_shapes=[tile(2), tile(2), tile(2), tile(2), tile(3), tile(3), tile(2), tile(4), tile(4),
                        pltpu.VMEM((ATT_BLOCKS, CHUNK, LANES), F32)]
        + [dma(2), dma(2), dma(2), dma(2), dma(3), dma(3), dma(2), dma(3), dma(3)]
        + (_owner_exchange_sems(ns) if ns else []),
        input_output_aliases={7: 0},
        compiler_params=_params(("arbitrary", "arbitrary")),
    )(_slope_table(), *[_residue_view(a) for a in (q, k, v, dattn, attn, lse, dproj)], *owner_grads)
    return outs[0].reshape(T, IN_W), tuple(outs[1:])


def _proj_bwd(dproj, w_in_t, x, g1, dh1, owner_grads=()):
    T = x.shape[0]
    tm = TM_PROJ
    ns = len(owner_grads)
    steps = T // tm

    def body(d_ref, w_ref, x_ref, g_ref, r_ref, *rest):
        p_refs, rest = rest[:ns], rest[ns:]
        dx_ref, dg_ref = rest[:2]
        r_refs, sems = rest[2:2 + ns], rest[2 + ns:]
        step = pl.program_id(0)
        if ns:
            start, finish = _owner_exchange_phases(p_refs, r_refs, *sems)
            pl.when(step == 0)(start)

        @pl.when(step == 0)
        def _():
            dg_ref[...] = jnp.zeros_like(dg_ref)

        dhn = _dot(d_ref[...].astype(BF16), w_ref[...])
        n1, r1 = _rms(x_ref[...])
        _accum_rows(dg_ref, dhn * n1)
        dx_ref[...] = r_ref[...] + _rms_bwd(n1, r1, g_ref[...], dhn)
        if ns:
            pl.when(step == steps - 1)(finish)

    outs = pl.pallas_call(
        body, name="proj_bwd", grid=(steps,),
        in_specs=[_rows(tm, IN_W), _resident((IN_W, D_MODEL)), _rows(tm, D_MODEL), _resident((1, D_MODEL)),
                  _rows(tm, D_MODEL)] + [_HBM] * ns,
        out_specs=[_rows(tm, D_MODEL), pl.BlockSpec((8, D_MODEL), lambda i: (0, 0))] + [_HBM] * ns,
        out_shape=[jax.ShapeDtypeStruct((T, D_MODEL), F32), jax.ShapeDtypeStruct((8, D_MODEL), F32)]
        + [jax.ShapeDtypeStruct(p.shape, p.dtype) for p in owner_grads],
        scratch_shapes=_owner_exchange_sems(ns) if ns else [],
        compiler_params=_params(("arbitrary",)),
    )(dproj, w_in_t, x, g1, dh1, *owner_grads)
    return outs[0], outs[1], tuple(outs[2:])


def _dw(a, b, name, tile, square_a=False, out_dtype=F32):
    T, ka = a.shape
    nb = b.shape[1]
    tka, tnb, tt = tile
    tt = min(tt, T)
    last = T // tt - 1

    def body(a_ref, b_ref, *refs):
        o_ref = refs[0]
        acc_ref = refs[1] if len(refs) > 1 else o_ref
        s = pl.program_id(2)

        @pl.when(s == 0)
        def _():
            acc_ref[...] = jnp.zeros_like(acc_ref)

        a_tile = a_ref[...]
        if square_a:
            a_tile = jnp.square(a_tile.astype(F32))
        acc_ref[...] += _dot(a_tile.astype(BF16), b_ref[...], TN)
        if acc_ref is not o_ref:
            @pl.when(s == last)
            def _():
                o_ref[...] = acc_ref[...].astype(out_dtype)

    return pl.pallas_call(
        body, name=name, grid=(ka // tka, nb // tnb, T // tt),
        in_specs=[pl.BlockSpec((tt, tka), lambda i, j, s: (s, i)), pl.BlockSpec((tt, tnb), lambda i, j, s: (s, j))],
        out_specs=pl.BlockSpec((tka, tnb), lambda i, j, s: (i, j)),
        out_shape=jax.ShapeDtypeStruct((ka, nb), out_dtype),
        scratch_shapes=[] if out_dtype == F32 else [pltpu.VMEM((tka, tnb), F32)],
        compiler_params=_params(("parallel", "parallel", "arbitrary")),
    )(a, b)


def _adamw_update(w, m, v, g):
    m2 = ADAM_B1 * m + (1.0 - ADAM_B1) * g
    v2 = ADAM_B2 * v + (1.0 - ADAM_B2) * jnp.square(g)
    m_hat = m2 / (1.0 - ADAM_B1 ** ADAM_STEP)
    v_hat = v2 / (1.0 - ADAM_B2 ** ADAM_STEP)
    return -ADAM_LR * (m_hat / (jnp.sqrt(v_hat) + ADAM_EPS) + ADAM_WD * w), m2, v2


def _adamw_tiny(ws, ms, vs, parts):
    n = len(ws)
    P = parts.shape[0]

    def body(*refs):
        w_refs, m_refs, v_refs, p_ref = refs[:n], refs[n:2 * n], refs[2 * n:3 * n], refs[3 * n]
        outs = refs[3 * n + 1:]

        def total(slot, rows):
            g = p_ref[0, 8 * slot:8 * slot + rows, :]
            for i in range(1, P):
                g = g + p_ref[i, 8 * slot:8 * slot + rows, :]
            return g

        for k in range(n):
            g = total(k, ws[k].shape[0])
            outs[4 * k][...] = g
            outs[4 * k + 1][...], outs[4 * k + 2][...], outs[4 * k + 3][...] = _adamw_update(
                w_refs[k][...], m_refs[k][...], v_refs[k][...], g)
        outs[4 * n][...] = total(n, 8)

    sds = jax.ShapeDtypeStruct
    return pl.pallas_call(
        body, name="adamw_tiny",
        out_shape=[sds(w.shape, F32) for w in ws for _ in range(4)] + [sds((8, LANES), F32)],
    )(*ws, *ms, *vs, parts)


def _adamw(w, m, v, parts, name, tr, transposed=False):
    R, C = w.shape
    P = parts.shape[0]

    def body(w_ref, m_ref, v_ref, p_ref, g_ref, d_ref, m2_ref, v2_ref):
        g = p_ref[0].astype(F32)
        for i in range(1, P):
            g = g + p_ref[i].astype(F32)
        if transposed:
            g = g.T
        g_ref[...] = g
        d_ref[...], m2_ref[...], v2_ref[...] = _adamw_update(w_ref[...], m_ref[...], v_ref[...], g)

    spec = _rows(tr, C)
    part_spec = (pl.BlockSpec((P, C, tr), lambda i: (0, 0, i)) if transposed
                 else pl.BlockSpec((P, tr, C), lambda i: (0, i, 0)))
    return pl.pallas_call(
        body, name=name, grid=(R // tr,),
        in_specs=[spec, spec, spec, part_spec],
        out_specs=[spec] * 4,
        out_shape=[jax.ShapeDtypeStruct((R, C), F32)] * 4,
        compiler_params=_params(("parallel",)),
    )(w, m, v, parts)


def _pair_sum(core, grad, recv, name):
    _, _, n, C = grad.shape
    tr = n // 2

    def body(c_ref, a_ref, b_ref, o_ref):
        o_ref[...] = a_ref[...] + b_ref[...]

    spec = pl.BlockSpec((1, tr, C), lambda i, j, c_ref: (i, j, 0))
    return pl.pallas_call(
        body, name=name,
        grid_spec=pltpu.PrefetchScalarGridSpec(
            num_scalar_prefetch=1, grid=(4, n // tr),
            in_specs=[pl.BlockSpec((1, None, tr, C), lambda i, j, c_ref: (i, c_ref[0], j, 0)), spec],
            out_specs=spec),
        out_shape=jax.ShapeDtypeStruct(recv.shape, F32),
        compiler_params=_params(("parallel", "parallel")),
    )(core.reshape(1), grad, recv)


_HBM = pl.BlockSpec(memory_space=pltpu.HBM)


def _place():
    return lax.axis_index("x"), lax.axis_index("y"), lax.axis_index("c")


def _gathered_shape(shard):
    return jax.ShapeDtypeStruct((N_DEV,) + shard.shape, shard.dtype)


def _gather_sems(n):
    return [pltpu.SemaphoreType.DMA((7, n)), pltpu.SemaphoreType.DMA((7, n)), pltpu.SemaphoreType.DMA((n,))]


def _gather_phases(x_refs, out_refs, send_sems, recv_sems, local_sems):
    x, y, c = _place()
    me, sibling = (x, y, c), (x, y, 1 - c)
    chips = [(1 - x, y), (x, 1 - y), (1 - x, 1 - y)]
    arrays = range(len(x_refs))

    def slot(i, px, py, pc):
        return out_refs[i].at[4 * px + 2 * py + pc]

    def copy(i, k, block, to, own=False):
        return pltpu.make_async_remote_copy(
            src_ref=x_refs[i] if own else slot(i, *block), dst_ref=slot(i, *block),
            send_sem=send_sems.at[k, i], recv_sem=recv_sems.at[k, i], device_id=to, device_id_type=MESH)

    def mine(i):
        return pltpu.make_async_copy(x_refs[i], slot(i, *me), local_sems.at[i])

    def start():
        for i in arrays:
            mine(i).start()
            copy(i, 0, me, sibling, own=True).start()
            for j, chip in enumerate(chips):
                copy(i, 1 + j, me, (*chip, c), own=True).start()

    def forward():
        for i in arrays:
            for j, chip in enumerate(chips):
                copy(i, 1 + j, (*chip, c), me).wait_recv()
                copy(i, 4 + j, (*chip, c), sibling).start()

    def finish():
        for i in arrays:
            copy(i, 0, sibling, me).wait_recv()
            copy(i, 0, me, sibling, own=True).wait_send()
            for j, chip in enumerate(chips):
                copy(i, 4 + j, (*chip, 1 - c), me).wait_recv()
                copy(i, 1 + j, me, (*chip, c), own=True).wait_send()
                copy(i, 4 + j, (*chip, c), sibling).wait_send()
            mine(i).wait()

    return start, forward, finish


def _all_gather(shards, name):
    n = len(shards)

    def body(*refs):
        start, forward, finish = _gather_phases(refs[:n], refs[n:2 * n], *refs[2 * n:])
        start()
        forward()
        finish()

    return pl.pallas_call(
        body, name=name,
        out_shape=[_gathered_shape(s) for s in shards],
        in_specs=[_HBM] * n, out_specs=[_HBM] * n,
        scratch_shapes=_gather_sems(n),
    )(*shards)


def _sibling_exchange(grads, name):
    n = len(grads)

    def body(*refs):
        g_refs, r_refs, send_sems, recv_sems = refs[:n], refs[n:2 * n], refs[2 * n], refs[2 * n + 1]
        x, y, c = _place()
        copies = [pltpu.make_async_remote_copy(
            src_ref=g_refs[i].at[:, 1 - c], dst_ref=r_refs[i], send_sem=send_sems.at[i], recv_sem=recv_sems.at[i],
            device_id=(x, y, 1 - c), device_id_type=MESH) for i in range(n)]
        for cp in copies:
            cp.start()
        for cp in copies:
            cp.wait()

    return pl.pallas_call(
        body, name=name,
        out_shape=[jax.ShapeDtypeStruct((g.shape[0],) + g.shape[2:], g.dtype) for g in grads],
        in_specs=[_HBM] * n, out_specs=[_HBM] * n,
        scratch_shapes=[pltpu.SemaphoreType.DMA((n,)), pltpu.SemaphoreType.DMA((n,))],
    )(*grads)


def _owner_exchange_sems(n):
    return [pltpu.SemaphoreType.DMA((7, n)), pltpu.SemaphoreType.DMA((7, n)), pltpu.SemaphoreType.DMA((n,))]


def _owner_exchange_phases(g_refs, r_refs, send_sems, recv_sems, local_sems):
    x, y, c = _place()
    me = 4 * x + 2 * y + c
    flip = lambda v, bit: 1 - v if bit else v
    peers = [(flip(x, k & 4), flip(y, k & 2), flip(c, k & 1)) for k in range(1, N_DEV)]
    arrays = range(len(g_refs))

    def mine(i):
        return pltpu.make_async_copy(g_refs[i].at[me], r_refs[i].at[me], local_sems.at[i])

    def copy(i, k, src_slot, dst_slot):
        return pltpu.make_async_remote_copy(
            src_ref=g_refs[i].at[src_slot], dst_ref=r_refs[i].at[dst_slot],
            send_sem=send_sems.at[k, i], recv_sem=recv_sems.at[k, i], device_id=peers[k], device_id_type=MESH)

    def start():
        for i in arrays:
            mine(i).start()
            for k, (px, py, pc) in enumerate(peers):
                copy(i, k, 4 * px + 2 * py + pc, me).start()

    def finish():
        for i in arrays:
            for k, (px, py, pc) in enumerate(peers):
                copy(i, k, me, 4 * px + 2 * py + pc).wait_recv()
                copy(i, k, 4 * px + 2 * py + pc, me).wait_send()
            mine(i).wait()

    return start, finish


def _chip_exchange_sems(n):
    return [pltpu.SemaphoreType.DMA((3, n)), pltpu.SemaphoreType.DMA((3, n)), pltpu.SemaphoreType.DMA((n,))]


def _chip_exchange_phases(p_refs, r_refs, send_sems, recv_sems, local_sems):
    x, y, c = _place()
    my_chip = 2 * x + y
    chips = [(1 - x, y), (x, 1 - y), (1 - x, 1 - y)]
    arrays = range(len(p_refs))

    def mine(i):
        return pltpu.make_async_copy(p_refs[i].at[my_chip], r_refs[i].at[my_chip], local_sems.at[i])

    def copy(i, k, src_chip, dst_chip):
        px, py = chips[k]
        return pltpu.make_async_remote_copy(
            src_ref=p_refs[i].at[src_chip], dst_ref=r_refs[i].at[dst_chip],
            send_sem=send_sems.at[k, i], recv_sem=recv_sems.at[k, i], device_id=(px, py, c), device_id_type=MESH)

    def start():
        for i in arrays:
            mine(i).start()
            for k, (px, py) in enumerate(chips):
                copy(i, k, 2 * px + py, my_chip).start()

    def finish():
        for i in arrays:
            for k, (px, py) in enumerate(chips):
                copy(i, k, my_chip, 2 * px + py).wait_recv()
                copy(i, k, 2 * px + py, my_chip).wait_send()
            mine(i).wait()

    return start, finish


_R_IN, _R_OUT, _R_FF = IN_W // N_DEV, D_MODEL // N_DEV, D_FF // N_DEV


def _by_owner(g):
    return g.reshape(4, 2, g.shape[0] // N_DEV, D_MODEL)


def _local_step(x, tgt, small, w_in_t, rest, core=None):
    exchange = core is not None
    g1, g2, gf = small["norm1_g"], small["norm2_g"], small["final_norm_g"].reshape(1, D_MODEL)
    ga, gg = small["attn_out_g"], small["gmlp_out_g"]
    ln_g = small["sgu_ln_g"].reshape(1, GMLP_W)
    ln_b = small["sgu_ln_b"].reshape(1, GMLP_W)
    sgu_w = small["sgu_w"][0]
    sgu_bt = small["sgu_b"][0].T

    hn1, q, k, v, u, z = _proj_fwd(x, g1, w_in_t)
    attn, lse, gathered = _attn_fwd(q, k, v, shards=rest if exchange else ())
    w_out, w_ff1_t, w_ff2 = [g.reshape(-1, D_MODEL) for g in gathered] if exchange else rest
    gm = _gmlp_fwd(u, z, ln_g, ln_b, sgu_w, sgu_bt)
    mixed, h1, hn2 = _out_fwd(attn, gm, ga, gg, w_out, x, g2)
    relu, dh2f, dh2b, loss8, dgf8 = _ffn_fwd(hn2, h1, w_ff1_t, w_ff2, gf, tgt)

    da, dh1f, dh1b, dg2 = _ffn_bwd(dh2b, dh2f, relu, h1, g2, w_ff2, w_ff1_t)
    wire = BF16 if exchange else F32
    dw_ff2 = _dw(relu, dh2b, "dw_ff2", DW_TILE, square_a=True, out_dtype=wire)
    dw_ff1_t = _dw(da, hn2, "dw_ff1", DW_TILE, out_dtype=wire)
    dattn, dgm, dga, dgg = _out_bwd(dh1b, w_out, attn, gm, ga, gg)
    dw_out = _dw(mixed, dh1b, "dw_out", DW_TILE, out_dtype=wire)
    early = [dw_out, dw_ff1_t, dw_ff2]
    if exchange:
        early = [g.reshape(N_DEV, -1, D_MODEL) for g in early]
    dproj, dlg, dlb, dsw, dsb = _gmlp_bwd(u, z, dgm, ln_g, ln_b, sgu_w, sgu_bt)
    dproj, arrived = _attn_bwd(q, k, v, dattn, attn, lse, dproj, owner_grads=early if exchange else ())
    dw_in_t = _dw(dproj, hn1, "dw_in", DW_TILE_IN, out_dtype=wire)
    late = (dw_in_t.reshape(N_DEV, -1, D_MODEL),) if exchange else ()
    dx, dg1, late = _proj_bwd(dproj, w_in_t, x, g1, dh1f, owner_grads=late)
    if exchange:
        dw_in_t, early = late[0], arrived

    small_grads = dict(
        norm1_g=dg1[0], sgu_ln_g=dlg[0], sgu_ln_b=dlb[0], sgu_w=dsw, sgu_b=dsb[:, :N_GROUPS].T,
        attn_out_g=dga[0], gmlp_out_g=dgg[0], norm2_g=dg2[0], final_norm_g=dgf8[0])
    return loss8[0, 0], dx, (dw_in_t, *early), small_grads


SMALL_NAMES = ("norm1_g", "sgu_ln_g", "sgu_ln_b", "sgu_w", "sgu_b", "attn_out_g", "gmlp_out_g", "norm2_g",
               "final_norm_g")
WEIGHT_ORDER = ("norm1_g", "w_in", "sgu_ln_g", "sgu_ln_b", "sgu_w", "sgu_b", "attn_out_g", "gmlp_out_g", "w_out",
                "norm2_g", "w_ff1", "w_ff2", "final_norm_g")


TINY_NAMES = tuple(n for n in SMALL_NAMES if n != "sgu_w")


def _as_rows(a):
    return a.reshape(-1, LANES)


def _pack_tiny_grads(d, loss):
    slots = [jnp.pad(_as_rows(d[n]), ((0, 8 - d[n].size // LANES), (0, 0))) for n in TINY_NAMES]
    return jnp.concatenate(slots + [jnp.full((8, LANES), loss, F32)], axis=0)


def kernel(x, norm1_g, w_in, sgu_ln_g, sgu_ln_b, sgu_w, sgu_b, attn_out_g, gmlp_out_g, w_out, norm2_g, w_ff1, w_ff2, final_norm_g, loss_target, m_norm1_g, m_w_in, m_sgu_ln_g, m_sgu_ln_b, m_sgu_w, m_sgu_b, m_attn_out_g, m_gmlp_out_g, m_w_out, m_norm2_g, m_w_ff1, m_w_ff2, m_final_norm_g, v_norm1_g, v_w_in, v_sgu_ln_g, v_sgu_ln_b, v_sgu_w, v_sgu_b, v_attn_out_g, v_gmlp_out_g, v_w_out, v_norm2_g, v_w_ff1, v_w_ff2, v_final_norm_g):
    w = dict(norm1_g=norm1_g, w_in=w_in, sgu_ln_g=sgu_ln_g, sgu_ln_b=sgu_ln_b, sgu_w=sgu_w, sgu_b=sgu_b,
             attn_out_g=attn_out_g, gmlp_out_g=gmlp_out_g, w_out=w_out, norm2_g=norm2_g, w_ff1=w_ff1, w_ff2=w_ff2,
             final_norm_g=final_norm_g)
    m = dict(norm1_g=m_norm1_g, w_in=m_w_in, sgu_ln_g=m_sgu_ln_g, sgu_ln_b=m_sgu_ln_b, sgu_w=m_sgu_w, sgu_b=m_sgu_b,
             attn_out_g=m_attn_out_g, gmlp_out_g=m_gmlp_out_g, w_out=m_w_out, norm2_g=m_norm2_g, w_ff1=m_w_ff1,
             w_ff2=m_w_ff2, final_norm_g=m_final_norm_g)
    v = dict(norm1_g=v_norm1_g, w_in=v_w_in, sgu_ln_g=v_sgu_ln_g, sgu_ln_b=v_sgu_ln_b, sgu_w=v_sgu_w, sgu_b=v_sgu_b,
             attn_out_g=v_attn_out_g, gmlp_out_g=v_gmlp_out_g, w_out=v_w_out, norm2_g=v_norm2_g, w_ff1=v_w_ff1,
             w_ff2=v_w_ff2, final_norm_g=v_final_norm_g)
    big = ("w_in", "w_out", "w_ff1", "w_ff2")
    core = lax.axis_index("c")

    w_in_t, = _all_gather([w_in[0].T.astype(BF16)], "w_in_all_gather")
    rest = (w_out[0].astype(BF16), w_ff1[0].T.astype(BF16), w_ff2[0].astype(BF16))
    loss, dx, parts, small_grads = _local_step(x[0], loss_target[0], {n: w[n] for n in SMALL_NAMES},
                                               w_in_t.reshape(IN_W, D_MODEL), rest, core=core)

    new = {}
    for n, p, transposed, tr in zip(big, parts, (True, False, True, False), (128, 128, 128, 256)):
        new[n] = [a[None] for a in _adamw(w[n][0], m[n][0], v[n][0], p, "adamw_" + n, tr, transposed)]

    tiny_parts, sgu_parts = _all_gather(
        [_pack_tiny_grads(small_grads, loss), _as_rows(small_grads["sgu_w"]).astype(BF16)], "small_grad_all_gather")
    tiny = _adamw_tiny(*[[_as_rows(src[n]) for n in TINY_NAMES] for src in (w, m, v)], tiny_parts)
    sgu = _adamw(_as_rows(sgu_w), _as_rows(m_sgu_w), _as_rows(v_sgu_w), sgu_parts, "adamw_sgu_w", 512)
    loss = tiny[-1][0, 0]

    outs = []
    for i in range(4):
        d = {n: new[n][i] for n in big}
        d.update({n: tiny[4 * k + i].reshape(w[n].shape) for k, n in enumerate(TINY_NAMES)})
        d["sgu_w"] = sgu[i].reshape(sgu_w.shape)
        outs.extend(d[n] for n in WEIGHT_ORDER)
    return (loss, dx[None], *outs)
```

```python
import math

import numpy as np
import jax
import jax.numpy as jnp
from jax import lax
from jax.experimental import pallas as pl
from jax.experimental.pallas import tpu as pltpu

F32 = jnp.float32
BF16 = jnp.bfloat16

D_MODEL = 1024
HEAD_DIM = 64
N_HEADS = 12
ATTN_W = N_HEADS * HEAD_DIM
N_GROUPS = 4
GMLP_W = N_GROUPS * HEAD_DIM
IN_W = 3 * ATTN_W + 2 * GMLP_W
D_FF = 4 * D_MODEL
CHUNK = 128
DILATIONS = (1, 4, 16)
EPS = 1e-6
Q_SCALE = HEAD_DIM ** -0.5
NEG = -1e30

ADAM_LR, ADAM_B1, ADAM_B2, ADAM_EPS, ADAM_WD, ADAM_STEP = 0.001, 0.9, 0.999, 1e-08, 0.01, 10

N_DEV = 8
LANES = 128
VMEM_LIMIT = 56 << 20

TM_PROJ = 512
TM_FFN = 512
FF_CHUNK = 512
TM_GMLP = 1024
DW_TILE = (512, 1024, 8192)
DW_TILE_IN = (IN_W // 2, 1024, 2048)

MESH = pl.DeviceIdType.MESH


def _alibi_slopes(n):
    def pow2(m):
        start = 2.0 ** (-8.0 / m)
        return [start ** (i + 1) for i in range(m)]
    c = 2 ** int(math.floor(math.log2(n)))
    s = pow2(n) if c == n else pow2(c) + pow2(2 * c)[0::2][: n - c]
    return np.asarray(s, dtype=np.float32)


SLOPES = _alibi_slopes(N_HEADS)


def _params(sem=None):
    kw = dict(vmem_limit_bytes=VMEM_LIMIT)
    if sem is not None:
        kw["dimension_semantics"] = sem
    return pltpu.CompilerParams(**kw)


def _rows(tm, n):
    return pl.BlockSpec((tm, n), lambda i: (i, 0))


def _resident(shape):
    return pl.BlockSpec(shape, lambda *_: (0,) * len(shape), pipeline_mode=pl.Buffered(1))


def _rms(x):
    r = lax.rsqrt(jnp.mean(x * x, axis=-1, keepdims=True) + EPS)
    return x * r, r


def _rms_bwd(n, r, g, dy):
    dn = dy * g
    return r * (dn - n * jnp.mean(dn * n, axis=-1, keepdims=True))


def _accum_rows(acc_ref, v):
    acc_ref[...] += jnp.broadcast_to(jnp.sum(v, axis=0, keepdims=True), acc_ref.shape)


_G0 = math.sqrt(2.0 / math.pi)
_G1 = 0.044715


def _gelu(x):
    t = jnp.tanh(_G0 * (x + _G1 * (x * x * x)))
    return x * (0.5 * (1.0 + t)), t


def _gelu_grad(x, t):
    return 0.5 * (1.0 + t) + 0.5 * x * (1.0 - t * t) * (_G0 * (1.0 + 3.0 * _G1 * x * x))


NT = (((1,), (1,)), ((), ()))
TN = (((0,), (0,)), ((), ()))


def _dot(a, b, dims=None):
    if dims is None:
        return jnp.dot(a, b, preferred_element_type=F32)
    return lax.dot_general(a, b, dims, preferred_element_type=F32)


def _proj_fwd(x, g1, w_in_t):
    T = x.shape[0]
    tm = TM_PROJ

    def body(x_ref, g_ref, w_ref, hn_ref, q_ref, k_ref, v_ref, u_ref, z_ref):
        n, _ = _rms(x_ref[...])
        hn = (n * g_ref[...]).astype(BF16)
        hn_ref[...] = hn
        a = ATTN_W
        q_ref[...] = _dot(hn, w_ref[0:a, :], NT) * Q_SCALE
        k_ref[...] = _dot(hn, w_ref[a:2 * a, :], NT)
        v_ref[...] = _dot(hn, w_ref[2 * a:3 * a, :], NT)
        u_ref[...] = _dot(hn, w_ref[3 * a:3 * a + GMLP_W, :], NT)
        z_ref[...] = _dot(hn, w_ref[3 * a + GMLP_W:, :], NT)

    sds = jax.ShapeDtypeStruct
    return pl.pallas_call(
        body, name="proj_fwd", grid=(T // tm,),
        in_specs=[_rows(tm, D_MODEL), _resident((1, D_MODEL)), _resident((IN_W, D_MODEL))],
        out_specs=[_rows(tm, D_MODEL), _rows(tm, ATTN_W), _rows(tm, ATTN_W), _rows(tm, ATTN_W),
                   _rows(tm, GMLP_W), _rows(tm, GMLP_W)],
        out_shape=[sds((T, D_MODEL), BF16), sds((T, ATTN_W), F32), sds((T, ATTN_W), F32),
                   sds((T, ATTN_W), F32), sds((T, GMLP_W), F32), sds((T, GMLP_W), F32)],
        compiler_params=_params(("parallel",)),
    )(x, g1, w_in_t)


ATT_TILE = 2048
ATT_BLOCKS = ATT_TILE // CHUNK
SM_BLOCKS = 4


def _slope_table():
    row = np.repeat(SLOPES, HEAD_DIM)
    return jnp.asarray(np.broadcast_to(row[None], (8, ATTN_W)), F32)


def _residue_view(a):
    return a.reshape(a.shape[0] // ATT_BLOCKS, ATT_BLOCKS, a.shape[1])


def _tile_copies(hbm, buf, sem, hp, t, to_hbm=False, lane0=0):
    rows = pl.ds(pl.multiple_of(t * CHUNK, CHUNK), CHUNK)
    lanes = pl.ds(pl.multiple_of(lane0 + hp * LANES, LANES), LANES)
    pairs = [(hbm.at[rows, r, lanes], buf.at[r]) for r in range(ATT_BLOCKS)]
    return [pltpu.make_async_copy(v, h, sem) if to_hbm else pltpu.make_async_copy(h, v, sem) for h, v in pairs]


def _wait_tile(buf, sem):
    pltpu.make_async_copy(buf, buf, sem).wait()


def _residue_rows(d, j):
    if d == 16:
        return [(j, 0, CHUNK)]
    if d == 4:
        return [(j % 4 + 4 * m, 32 * (j // 4), 32) for m in range(4)]
    return [(r, 8 * j, 8) for r in range(ATT_BLOCKS)]


def _block_order(p, d):
    if d == 16:
        return p
    if d == 4:
        return 4 * (p & 31) + (p >> 5)
    return 16 * (p & 7) + (p >> 3)


def _first_in_tile(d, j):
    return _residue_rows(d, j)[0][1] == 0


def _rm_block(buf, d, j):
    return jnp.concatenate([buf[r, lo:lo + n, :] for r, lo, n in _residue_rows(d, j)], axis=0)


def _rm_block_before(buf, buf_before, d, j):
    if _first_in_tile(d, j):
        return jnp.concatenate([buf_before[r, CHUNK - n:CHUNK, :] for r, _, n in _residue_rows(d, j)], axis=0)
    return jnp.concatenate([buf[r, lo - n:lo, :] for r, lo, n in _residue_rows(d, j)], axis=0)


def _rm_store(buf, d, j, val):
    at = 0
    for r, lo, n in _residue_rows(d, j):
        buf[r, lo:lo + n, :] = val[at:at + n, :]
        at += n


def _rm_add(buf, rows, val, first=False):
    at = 0
    for r, lo, n in rows:
        if first:
            buf[r, lo:lo + n, :] = val[at:at + n, :]
        else:
            buf[r, lo:lo + n, :] += val[at:at + n, :]
        at += n


def _residue_bias(sl_ref, d):
    shape = (2 * CHUNK, 2 * CHUNK)
    row = lax.broadcasted_iota(jnp.int32, shape, 0)
    col = lax.broadcasted_iota(jnp.int32, shape, 1)
    steps = _block_order(row & (CHUNK - 1), d) + CHUNK - (_block_order(col & (CHUNK - 1), d) + (col & CHUNK))
    band = (steps >= 0) & (steps <= CHUNK)
    sl = sl_ref[0:1, :]
    upper = lax.broadcasted_iota(jnp.int32, (2 * CHUNK, 1), 0) < CHUNK
    slope2 = jnp.where(upper, sl[:, 0:1], sl[:, HEAD_DIM:HEAD_DIM + 1])
    return jnp.where(band, -(float(d) * slope2 * steps.astype(F32)), NEG)


def _stack_heads(xb, head0):
    zero = jnp.zeros_like(xb)
    return jnp.concatenate([jnp.where(head0, xb, zero), jnp.where(head0, zero, xb)], axis=0).astype(BF16)


def _unstack_heads(x2, head0):
    return jnp.where(head0, x2[:CHUNK, :], x2[CHUNK:, :])


def _attn_fwd(q, k, v, shards=()):
    T = q.shape[0]
    nt = T // ATT_TILE
    ns = len(shards)
    steps = (ATTN_W // LANES) * nt

    def body(sl_ref, q_hbm, k_hbm, v_hbm, *rest):
        x_refs, rest = rest[:ns], rest[ns:]
        attn_hbm, lse_hbm = rest[:2]
        g_refs, rest = rest[2:2 + ns], rest[2 + ns:]
        qbuf, kbuf, vbuf, obuf, lbuf = rest[:5]
        o_acc, l_acc = rest[5:8], rest[8:11]
        sem_q, sem_k, sem_v, sem_o, sem_l = rest[11:16]
        hp, t = pl.program_id(0), pl.program_id(1)
        step = hp * nt + t
        two, three = step % 2, step % 3
        before, after = (step + 2) % 3, (step + 1) % 3
        if ns:
            start, forward, finish = _gather_phases(x_refs, g_refs, *rest[16:])
            pl.when(step == 0)(start)
            pl.when(step == steps // 2)(forward)

        def fetch(hp_, t_, two_, three_):
            for cp in (_tile_copies(q_hbm, qbuf.at[two_], sem_q.at[two_], hp_, t_)
                       + _tile_copies(k_hbm, kbuf.at[three_], sem_k.at[three_], hp_, t_)
                       + _tile_copies(v_hbm, vbuf.at[three_], sem_v.at[three_], hp_, t_)):
                cp.start()

        @pl.when(step == 0)
        def _():
            kbuf[2] = jnp.zeros((ATT_BLOCKS, CHUNK, LANES), F32)
            vbuf[2] = jnp.zeros((ATT_BLOCKS, CHUNK, LANES), F32)
            fetch(0, 0, 0, 0)

        @pl.when(step + 1 < steps)
        def _():
            fetch((step + 1) // nt, (step + 1) % nt, 1 - two, after)

        _wait_tile(qbuf.at[two], sem_q.at[two])
        _wait_tile(kbuf.at[three], sem_k.at[three])
        _wait_tile(vbuf.at[three], sem_v.at[three])

        @pl.when(step >= 2)
        def _():
            _wait_tile(obuf.at[two], sem_o.at[two])
            _wait_tile(lbuf.at[two], sem_l.at[two])

        q_t, k_t, v_t = qbuf.at[two], kbuf.at[three], vbuf.at[three]
        k_b, v_b = kbuf.at[before], vbuf.at[before]
        head0 = lax.broadcasted_iota(jnp.int32, (CHUNK, LANES), 1) < HEAD_DIM
        no_key_before = jnp.where(lax.broadcasted_iota(jnp.int32, (2 * CHUNK, 2 * CHUNK), 1) < CHUNK, NEG, 0.0)
        for pi, d in enumerate(DILATIONS):
            bias = _residue_bias(sl_ref, d)

            def scores(j, d=d, bias=bias):
                kcat = jnp.concatenate([_rm_block_before(k_t, k_b, d, j), _rm_block(k_t, d, j)], axis=0).astype(BF16)
                vcat = jnp.concatenate([_rm_block_before(v_t, v_b, d, j), _rm_block(v_t, d, j)], axis=0).astype(BF16)
                s = _dot(_stack_heads(_rm_block(q_t, d, j), head0), kcat, NT) + bias
                if _first_in_tile(d, j):
                    s = s + jnp.where(t == 0, 1.0, 0.0) * no_key_before
                return s, vcat

            def output(j, p, vcat, scale, lse, d=d, pi=pi):
                _rm_store(o_acc[pi], d, j, _unstack_heads(_dot(p, vcat) * scale, head0))
                _rm_store(l_acc[pi], d, j, _unstack_heads(jnp.broadcast_to(lse, (2 * CHUNK, LANES)), head0))

            for j0 in range(0, ATT_BLOCKS, SM_BLOCKS):
                group = [scores(j) for j in range(j0, j0 + SM_BLOCKS)]
                s = jnp.concatenate([g[0] for g in group], axis=0)
                m = jnp.max(s, axis=-1, keepdims=True)
                p = jnp.exp(s - m)
                l = jnp.sum(p, axis=-1, keepdims=True)
                p, scale, lse = p.astype(BF16), 1.0 / l, m + jnp.log(l)
                for i, (_, vcat) in enumerate(group):
                    rows = slice(i * 2 * CHUNK, (i + 1) * 2 * CHUNK)
                    output(j0 + i, p[rows, :], vcat, scale[rows, :], lse[rows, :])

        for r in range(ATT_BLOCKS):
            a, b, c = l_acc[0][r], l_acc[1][r], l_acc[2][r]
            m = jnp.maximum(jnp.maximum(a, b), c)
            ea, eb, ec = jnp.exp(a - m), jnp.exp(b - m), jnp.exp(c - m)
            tot = ea + eb + ec
            obuf[two, r] = (ea * o_acc[0][r] + eb * o_acc[1][r] + ec * o_acc[2][r]) / tot
            lbuf[two, r] = m + jnp.log(tot)

        for cp in (_tile_copies(attn_hbm, obuf.at[two], sem_o.at[two], hp, t, to_hbm=True)
                   + _tile_copies(lse_hbm, lbuf.at[two], sem_l.at[two], hp, t, to_hbm=True)):
            cp.start()

        @pl.when(step == steps - 1)
        def _():
            for slot in (two, 1 - two)[:min(steps, 2)]:
                _wait_tile(obuf.at[slot], sem_o.at[slot])
                _wait_tile(lbuf.at[slot], sem_l.at[slot])

        if ns:
            pl.when(step == steps - 1)(finish)

    tile = lambda n: pltpu.VMEM((n, ATT_BLOCKS, CHUNK, LANES), F32)
    dma = lambda n: pltpu.SemaphoreType.DMA((n,))
    view = jax.ShapeDtypeStruct((T // ATT_BLOCKS, ATT_BLOCKS, ATTN_W), F32)
    outs = pl.pallas_call(
        body, name="attn_fwd", grid=(ATTN_W // LANES, nt),
        in_specs=[pl.BlockSpec((8, LANES), lambda c, t: (0, c))] + [_HBM] * (3 + ns),
        out_specs=[_HBM] * (2 + ns),
        out_shape=[view, view] + [_gathered_shape(s) for s in shards],
        scratch_shapes=[tile(2), tile(3), tile(3), tile(2), tile(2)] + [pltpu.VMEM((ATT_BLOCKS, CHUNK, LANES), F32)] * 6
        + [dma(2), dma(3), dma(3), dma(2), dma(2)] + (_gather_sems(ns) if ns else []),
        compiler_params=_params(("arbitrary", "arbitrary")),
    )(_slope_table(), _residue_view(q), _residue_view(k), _residue_view(v), *shards)
    return outs[0].reshape(T, ATTN_W), outs[1].reshape(T, ATTN_W), tuple(outs[2:])


def _group_mean(v, grp):
    out = jnp.zeros_like(v)
    for g in range(N_GROUPS):
        mk = grp == g
        s = jnp.sum(jnp.where(mk, v, 0.0), axis=-1, keepdims=True) * (1.0 / HEAD_DIM)
        out = jnp.where(mk, s, out)
    return out


def _gmlp_core(uu, zz, lg, lb, ws, sb_ref, grp):
    ug, tu = _gelu(uu)
    zg, tz = _gelu(zz)
    zc = zg - _group_mean(zg, grp)
    rstd = lax.rsqrt(_group_mean(zc * zc, grp) + EPS)
    xhat = zc * rstd
    zn16 = (xhat * lg + lb).astype(BF16)
    mixed = []
    for ci in range(uu.shape[0] // CHUNK):
        rows = slice(ci * CHUNK, (ci + 1) * CHUNK)
        m = jnp.zeros((CHUNK, GMLP_W), F32)
        for g in range(N_GROUPS):
            m = jnp.where(grp[:CHUNK] == g, _dot(ws[g], zn16[rows, :]) + sb_ref[:, g:g + 1], m)
        mixed.append(m)
    return ug, tu, tz, xhat, rstd, zn16, jnp.concatenate(mixed, axis=0)


def _causal_ws(w_ref):
    ti = lax.broadcasted_iota(jnp.int32, (CHUNK, CHUNK), 0)
    si = lax.broadcasted_iota(jnp.int32, (CHUNK, CHUNK), 1)
    causal = si <= ti
    return causal, [jnp.where(causal, w_ref[g], 0.0).astype(BF16) for g in range(N_GROUPS)]


def _gmlp_fwd(u, z, ln_g, ln_b, sgu_w, sgu_bt):
    T = u.shape[0]
    tg = TM_GMLP

    def body(u_ref, z_ref, g_ref, b_ref, w_ref, sb_ref, out_ref):
        grp = lax.broadcasted_iota(jnp.int32, (tg, GMLP_W), 1) // HEAD_DIM
        _, ws = _causal_ws(w_ref)
        ug, _, _, _, _, _, mixed = _gmlp_core(u_ref[...], z_ref[...], g_ref[...], b_ref[...], ws, sb_ref, grp)
        out_ref[...] = ug * mixed

    return pl.pallas_call(
        body, name="gmlp_fwd", grid=(T // tg,),
        in_specs=[_rows(tg, GMLP_W), _rows(tg, GMLP_W), _resident((1, GMLP_W)), _resident((1, GMLP_W)),
                  _resident((N_GROUPS, CHUNK, CHUNK)), _resident((CHUNK, N_GROUPS))],
        out_specs=_rows(tg, GMLP_W),
        out_shape=jax.ShapeDtypeStruct((T, GMLP_W), F32),
        compiler_params=_params(("parallel",)),
    )(u, z, ln_g, ln_b, sgu_w, sgu_bt)


def _out_fwd(attn, gm, ga, gg, w_out, x, g2):
    T = x.shape[0]
    tm = TM_PROJ

    def body(a_ref, m_ref, ga_ref, gg_ref, w_ref, x_ref, g2_ref, mix_ref, h1_ref, hn2_ref):
        an, _ = _rms(a_ref[...])
        gn, _ = _rms(m_ref[...])
        an = (an * ga_ref[...]).astype(BF16)
        gn = (gn * gg_ref[...]).astype(BF16)
        mix_ref[:, 0:ATTN_W] = an
        mix_ref[:, ATTN_W:] = gn
        h1 = x_ref[...] + _dot(an, w_ref[0:ATTN_W, :]) + _dot(gn, w_ref[ATTN_W:, :])
        h1_ref[...] = h1
        n2, _ = _rms(h1)
        hn2_ref[...] = (n2 * g2_ref[...]).astype(BF16)

    sds = jax.ShapeDtypeStruct
    return pl.pallas_call(
        body, name="out_fwd", grid=(T // tm,),
        in_specs=[_rows(tm, ATTN_W), _rows(tm, GMLP_W), _resident((1, ATTN_W)), _resident((1, GMLP_W)),
                  _resident((D_MODEL, D_MODEL)), _rows(tm, D_MODEL), _resident((1, D_MODEL))],
        out_specs=[_rows(tm, D_MODEL)] * 3,
        out_shape=[sds((T, D_MODEL), BF16), sds((T, D_MODEL), F32), sds((T, D_MODEL), BF16)],
        compiler_params=_params(("parallel",)),
    )(attn, gm, ga, gg, w_out, x, g2)


def _ffn_fwd(hn2, h1, w1t, w2, gf, tgt):
    T = h1.shape[0]
    tm = TM_FFN

    def body(hn_ref, h1_ref, w1_ref, w2_ref, gf_ref, t_ref, r_ref, dhf_ref, dhb_ref, loss_ref, dgf_ref):
        i = pl.program_id(0)

        @pl.when(i == 0)
        def _():
            loss_ref[...] = jnp.zeros_like(loss_ref)
            dgf_ref[...] = jnp.zeros_like(dgf_ref)

        hn = hn_ref[...]
        acc = h1_ref[...]
        for j in range(D_FF // FF_CHUNK):
            cols = slice(j * FF_CHUNK, (j + 1) * FF_CHUNK)
            r = jnp.maximum(_dot(hn, w1_ref[cols, :], NT), 0.0)
            r_ref[:, cols] = r.astype(BF16)
            act = jnp.square(r).astype(BF16)
            acc = acc + _dot(act, w2_ref[cols, :])
        n3, r3 = _rms(acc)
        gf_row = gf_ref[...]
        e = n3 * gf_row - t_ref[...]
        loss_ref[...] += 0.5 * jnp.sum(jnp.mean(e * e, axis=-1, keepdims=True))
        dy = e * (1.0 / D_MODEL)
        _accum_rows(dgf_ref, dy * n3)
        dh2 = _rms_bwd(n3, r3, gf_row, dy)
        dhf_ref[...] = dh2
        dhb_ref[...] = dh2.astype(BF16)

    sds = jax.ShapeDtypeStruct
    acc_spec = lambda n: pl.BlockSpec((8, n), lambda i: (0, 0))
    return pl.pallas_call(
        body, name="ffn_fwd", grid=(T // tm,),
        in_specs=[_rows(tm, D_MODEL), _rows(tm, D_MODEL), _resident((D_FF, D_MODEL)), _resident((D_FF, D_MODEL)),
                  _resident((1, D_MODEL)), _rows(tm, D_MODEL)],
        out_specs=[_rows(tm, D_FF), _rows(tm, D_MODEL), _rows(tm, D_MODEL), acc_spec(LANES), acc_spec(D_MODEL)],
        out_shape=[sds((T, D_FF), BF16), sds((T, D_MODEL), F32), sds((T, D_MODEL), BF16),
                   sds((8, LANES), F32), sds((8, D_MODEL), F32)],
        compiler_params=_params(("arbitrary",)),
    )(hn2, h1, w1t, w2, gf, tgt)


def _ffn_bwd(dh2b, dh2f, relu, h1, g2, w2, w1t):
    T = h1.shape[0]
    tm = TM_FFN

    def body(db_ref, df_ref, r_ref, h1_ref, g2_ref, w2_ref, w1t_ref, da_ref, d1f_ref, d1b_ref, dg_ref):
        @pl.when(pl.program_id(0) == 0)
        def _():
            dg_ref[...] = jnp.zeros_like(dg_ref)

        db = db_ref[...]
        acc = jnp.zeros((tm, D_MODEL), F32)
        for j in range(D_FF // FF_CHUNK):
            cols = slice(j * FF_CHUNK, (j + 1) * FF_CHUNK)
            da = (_dot(db, w2_ref[cols, :], NT) * (2.0 * r_ref[:, cols].astype(F32))).astype(BF16)
            da_ref[:, cols] = da
            acc = acc + _dot(da, w1t_ref[cols, :])
        n2, r2 = _rms(h1_ref[...])
        _accum_rows(dg_ref, acc * n2)
        dh1 = df_ref[...] + _rms_bwd(n2, r2, g2_ref[...], acc)
        d1f_ref[...] = dh1
        d1b_ref[...] = dh1.astype(BF16)

    sds = jax.ShapeDtypeStruct
    return pl.pallas_call(
        body, name="ffn_bwd", grid=(T // tm,),
        in_specs=[_rows(tm, D_MODEL), _rows(tm, D_MODEL), _rows(tm, D_FF), _rows(tm, D_MODEL),
                  _resident((1, D_MODEL)), _resident((D_FF, D_MODEL)), _resident((D_FF, D_MODEL))],
        out_specs=[_rows(tm, D_FF), _rows(tm, D_MODEL), _rows(tm, D_MODEL),
                   pl.BlockSpec((8, D_MODEL), lambda i: (0, 0))],
        out_shape=[sds((T, D_FF), BF16), sds((T, D_MODEL), F32), sds((T, D_MODEL), BF16), sds((8, D_MODEL), F32)],
        compiler_params=_params(("arbitrary",)),
    )(dh2b, dh2f, relu, h1, g2, w2, w1t)


def _out_bwd(dh1b, w_out, attn, gm, ga, gg):
    T = attn.shape[0]
    tm = TM_PROJ

    def body(d_ref, w_ref, a_ref, m_ref, ga_ref, gg_ref, da_ref, dm_ref, dga_ref, dgg_ref):
        @pl.when(pl.program_id(0) == 0)
        def _():
            dga_ref[...] = jnp.zeros_like(dga_ref)
            dgg_ref[...] = jnp.zeros_like(dgg_ref)

        d = d_ref[...]
        dan = _dot(d, w_ref[0:ATTN_W, :], NT)
        dgn = _dot(d, w_ref[ATTN_W:, :], NT)
        na, ra = _rms(a_ref[...])
        ng, rg = _rms(m_ref[...])
        _accum_rows(dga_ref, dan * na)
        _accum_rows(dgg_ref, dgn * ng)
        da_ref[...] = _rms_bwd(na, ra, ga_ref[...], dan)
        dm_ref[...] = _rms_bwd(ng, rg, gg_ref[...], dgn)

    sds = jax.ShapeDtypeStruct
    return pl.pallas_call(
        body, name="out_bwd", grid=(T // tm,),
        in_specs=[_rows(tm, D_MODEL), _resident((D_MODEL, D_MODEL)), _rows(tm, ATTN_W), _rows(tm, GMLP_W),
                  _resident((1, ATTN_W)), _resident((1, GMLP_W))],
        out_specs=[_rows(tm, ATTN_W), _rows(tm, GMLP_W), pl.BlockSpec((8, ATTN_W), lambda i: (0, 0)),
                   pl.BlockSpec((8, GMLP_W), lambda i: (0, 0))],
        out_shape=[sds((T, ATTN_W), F32), sds((T, GMLP_W), F32), sds((8, ATTN_W), F32), sds((8, GMLP_W), F32)],
        compiler_params=_params(("arbitrary",)),
    )(dh1b, w_out, attn, gm, ga, gg)


def _gmlp_bwd(u, z, dgm, ln_g, ln_b, sgu_w, sgu_bt):
    T = u.shape[0]
    tg = TM_GMLP
    nsteps = T // tg

    def body(u_ref, z_ref, d_ref, g_ref, b_ref, w_ref, sb_ref, dproj_hbm, dlg_ref, dlb_ref, dw_ref, dsb_ref,
             stage, sem):
        i = pl.program_id(0)
        slot = i % 2
        duz_ref = stage.at[slot]

        def to_dproj(step, buf):
            rows = pl.ds(pl.multiple_of(step * tg, tg), tg)
            return pltpu.make_async_copy(stage.at[buf], dproj_hbm.at[rows, pl.ds(3 * ATTN_W, 2 * GMLP_W)],
                                         sem.at[buf])

        @pl.when(i == 0)
        def _():
            for ref in (dlg_ref, dlb_ref, dw_ref, dsb_ref):
                ref[...] = jnp.zeros_like(ref)

        @pl.when(i >= 2)
        def _():
            to_dproj(i - 2, slot).wait()

        grp = lax.broadcasted_iota(jnp.int32, (tg, GMLP_W), 1) // HEAD_DIM
        lane = lax.broadcasted_iota(jnp.int32, (CHUNK, LANES), 1)
        causal, ws = _causal_ws(w_ref)
        lg = g_ref[...]
        uu, zz, dgm = u_ref[...], z_ref[...], d_ref[...]
        ug, tu, tz, xhat, rstd, zn16, mixed = _gmlp_core(uu, zz, lg, b_ref[...], ws, sb_ref, grp)
        dmx = dgm * ug
        duz_ref[:, 0:GMLP_W] = dgm * mixed * _gelu_grad(uu, tu)
        dmx16 = dmx.astype(BF16)
        dzn = []
        for ci in range(tg // CHUNK):
            rows = slice(ci * CHUNK, (ci + 1) * CHUNK)
            dmx_c, d = dmx16[rows, :], jnp.zeros((CHUNK, GMLP_W), F32)
            for g in range(N_GROUPS):
                mk = grp[:CHUNK] == g
                d = jnp.where(mk, _dot(ws[g], dmx_c, TN), d)
                dw_ref[g] += _dot(jnp.where(mk, dmx_c, jnp.zeros_like(dmx_c)), zn16[rows, :], NT)
            dzn.append(d)
        dzn = jnp.concatenate(dzn, axis=0)
        dsb = jnp.zeros((CHUNK, LANES), F32)
        for g in range(N_GROUPS):
            per_token = jnp.sum(jnp.where(grp == g, dmx, 0.0), axis=-1, keepdims=True)
            by_position = sum(per_token[ci * CHUNK:(ci + 1) * CHUNK] for ci in range(tg // CHUNK))
            dsb = jnp.where(lane == g, by_position, dsb)
        dsb_ref[...] += dsb
        _accum_rows(dlg_ref, dzn * xhat)
        _accum_rows(dlb_ref, dzn)
        dxh = dzn * lg
        dzg = rstd * (dxh - _group_mean(dxh, grp) - xhat * _group_mean(dxh * xhat, grp))
        duz_ref[:, GMLP_W:] = dzg * _gelu_grad(zz, tz)
        to_dproj(i, slot).start()

        @pl.when(i == nsteps - 1)
        def _():
            for g in range(N_GROUPS):
                dw_ref[g] = jnp.where(causal, dw_ref[g], 0.0)
            to_dproj(i, slot).wait()
            if nsteps >= 2:
                to_dproj(i - 1, 1 - slot).wait()

    sds = jax.ShapeDtypeStruct
    return pl.pallas_call(
        body, name="gmlp_bwd", grid=(nsteps,),
        in_specs=[_rows(tg, GMLP_W)] * 3 + [_resident((1, GMLP_W)), _resident((1, GMLP_W)),
                                              _resident((N_GROUPS, CHUNK, CHUNK)), _resident((CHUNK, N_GROUPS))],
        out_specs=[_HBM, pl.BlockSpec((8, GMLP_W), lambda i: (0, 0)),
                   pl.BlockSpec((8, GMLP_W), lambda i: (0, 0)),
                   pl.BlockSpec((N_GROUPS, CHUNK, CHUNK), lambda i: (0, 0, 0)),
                   pl.BlockSpec((CHUNK, LANES), lambda i: (0, 0))],
        out_shape=[sds((T, IN_W), F32), sds((8, GMLP_W), F32), sds((8, GMLP_W), F32),
                   sds((N_GROUPS, CHUNK, CHUNK), F32), sds((CHUNK, LANES), F32)],
        scratch_shapes=[pltpu.VMEM((2, tg, 2 * GMLP_W), F32), pltpu.SemaphoreType.DMA((2,))],
        compiler_params=_params(("arbitrary",)),
    )(u, z, dgm, ln_g, ln_b, sgu_w, sgu_bt)


def _attn_bwd(q, k, v, dattn, attn, lse, dproj, owner_grads=()):
    T = q.shape[0]
    nt = T // ATT_TILE
    ns = len(owner_grads)
    steps = (ATTN_W // LANES) * nt

    def body(sl_ref, q_hbm, k_hbm, v_hbm, do_hbm, o_hbm, lse_hbm, _, *rest):
        p_refs, rest = rest[:ns], rest[ns:]
        dq_hbm = dk_hbm = dv_hbm = rest[0]
        r_refs, rest = rest[1:1 + ns], rest[1 + ns:]
        qbuf, dobuf, obuf, lbuf, kbuf, vbuf, dqbuf, dkbuf, dvbuf, delta_s = rest[:10]
        sem_q, sem_do, sem_o, sem_l, sem_k, sem_v, sem_dq, sem_dk, sem_dv = rest[10:19]
        hp, t = pl.program_id(0), pl.program_id(1)
        step = hp * nt + t
        two, three = step % 2, step % 3
        before, after = (step + 2) % 3, (step + 1) % 3
        if ns:
            start, finish = _owner_exchange_phases(p_refs, r_refs, *rest[19:])
            pl.when(step == 0)(start)

        def fetch(hp_, t_, two_, three_):
            for hbm, buf, sem, slot in ((q_hbm, qbuf, sem_q, two_), (do_hbm, dobuf, sem_do, two_),
                                        (o_hbm, obuf, sem_o, two_), (lse_hbm, lbuf, sem_l, two_),
                                        (k_hbm, kbuf, sem_k, three_), (v_hbm, vbuf, sem_v, three_)):
                for cp in _tile_copies(hbm, buf.at[slot], sem.at[slot], hp_, t_):
                    cp.start()

        @pl.when(step == 0)
        def _():
            kbuf[2] = jnp.zeros((ATT_BLOCKS, CHUNK, LANES), F32)
            vbuf[2] = jnp.zeros((ATT_BLOCKS, CHUNK, LANES), F32)
            dkbuf[3] = jnp.zeros((ATT_BLOCKS, CHUNK, LANES), F32)
            dvbuf[3] = jnp.zeros((ATT_BLOCKS, CHUNK, LANES), F32)
            fetch(0, 0, 0, 0)

        @pl.when(step + 1 < steps)
        def _():
            fetch((step + 1) // nt, (step + 1) % nt, 1 - two, after)

        for buf, sem in ((qbuf, sem_q), (dobuf, sem_do), (obuf, sem_o), (lbuf, sem_l)):
            _wait_tile(buf.at[two], sem.at[two])
        _wait_tile(kbuf.at[three], sem_k.at[three])
        _wait_tile(vbuf.at[three], sem_v.at[three])

        @pl.when(step >= 2)
        def _():
            _wait_tile(dqbuf.at[two], sem_dq.at[two])

        @pl.when(step >= 3)
        def _():
            _wait_tile(dkbuf.at[three], sem_dk.at[three])
            _wait_tile(dvbuf.at[three], sem_dv.at[three])

        q_t, do_t, l_t, k_t, v_t = qbuf.at[two], dobuf.at[two], lbuf.at[two], kbuf.at[three], vbuf.at[three]
        k_b, v_b = kbuf.at[before], vbuf.at[before]
        dq_t, dk_t, dv_t = dqbuf.at[two], dkbuf.at[three], dvbuf.at[three]
        dk_b, dv_b = dkbuf.at[before], dvbuf.at[before]
        sink = jnp.where(t > 0, before, 3)
        dk_sink, dv_sink = dkbuf.at[sink], dvbuf.at[sink]
        head0 = lax.broadcasted_iota(jnp.int32, (CHUNK, LANES), 1) < HEAD_DIM
        for r in range(ATT_BLOCKS):
            dd = dobuf[two, r] * obuf[two, r]
            d0 = jnp.sum(jnp.where(head0, dd, 0.0), axis=-1, keepdims=True)
            d1 = jnp.sum(jnp.where(head0, 0.0, dd), axis=-1, keepdims=True)
            delta_s[r] = jnp.where(head0, d0, d1)

        def column(xb):
            return jnp.concatenate([xb[:, 0:1], xb[:, HEAD_DIM:HEAD_DIM + 1]], axis=0)

        no_key_before = jnp.where(lax.broadcasted_iota(jnp.int32, (2 * CHUNK, 2 * CHUNK), 1) < CHUNK, NEG, 0.0)
        for d in DILATIONS:
            bias = _residue_bias(sl_ref, d)
            for j in range(ATT_BLOCKS):
                kcat = jnp.concatenate([_rm_block_before(k_t, k_b, d, j), _rm_block(k_t, d, j)], axis=0).astype(BF16)
                vcat = jnp.concatenate([_rm_block_before(v_t, v_b, d, j), _rm_block(v_t, d, j)], axis=0).astype(BF16)
                q2 = _stack_heads(_rm_block(q_t, d, j), head0)
                do2 = _stack_heads(_rm_block(do_t, d, j), head0)
                s = _dot(q2, kcat, NT) + bias
                if _first_in_tile(d, j):
                    s = s + jnp.where(t == 0, 1.0, 0.0) * no_key_before
                p = jnp.exp(s - column(_rm_block(l_t, d, j)))
                ds = (p * (_dot(do2, vcat, NT) - column(_rm_block(delta_s, d, j)))).astype(BF16)
                first = d == DILATIONS[0]
                _rm_add(dq_t, _residue_rows(d, j), _unstack_heads(_dot(ds, kcat), head0), first)
                ck = _dot(ds, q2, TN)
                cv = _dot(p.astype(BF16), do2, TN)
                _rm_add(dk_t, _residue_rows(d, j), ck[CHUNK:, :], first)
                _rm_add(dv_t, _residue_rows(d, j), cv[CHUNK:, :], first)
                if _first_in_tile(d, j):
                    rows = [(r, CHUNK - n, n) for r, _, n in _residue_rows(d, j)]
                    _rm_add(dk_sink, rows, ck[:CHUNK, :])
                    _rm_add(dv_sink, rows, cv[:CHUNK, :])
                else:
                    rows = [(r, lo - n, n) for r, lo, n in _residue_rows(d, j)]
                    _rm_add(dk_t, rows, ck[:CHUNK, :])
                    _rm_add(dv_t, rows, cv[:CHUNK, :])

        for r in range(ATT_BLOCKS):
            dqbuf[two, r] = dqbuf[two, r] * Q_SCALE
        for cp in _tile_copies(dq_hbm, dq_t, sem_dq.at[two], hp, t, to_hbm=True):
            cp.start()

        @pl.when(t > 0)
        def _():
            for cp in (_tile_copies(dk_hbm, dk_b, sem_dk.at[before], hp, t - 1, to_hbm=True, lane0=ATTN_W)
                       + _tile_copies(dv_hbm, dv_b, sem_dv.at[before], hp, t - 1, to_hbm=True, lane0=2 * ATTN_W)):
                cp.start()

        @pl.when(t == nt - 1)
        def _():
            for cp in (_tile_copies(dk_hbm, dk_t, sem_dk.at[three], hp, t, to_hbm=True, lane0=ATTN_W)
                       + _tile_copies(dv_hbm, dv_t, sem_dv.at[three], hp, t, to_hbm=True, lane0=2 * ATTN_W)):
                cp.start()

        @pl.when(step == steps - 1)
        def _():
            for slot in range(2):
                _wait_tile(dqbuf.at[slot], sem_dq.at[slot])
            for slot in range(3):
                _wait_tile(dkbuf.at[slot], sem_dk.at[slot])
                _wait_tile(dvbuf.at[slot], sem_dv.at[slot])

        if ns:
            pl.when(step == steps - 1)(finish)

    tile = lambda n: pltpu.VMEM((n, ATT_BLOCKS, CHUNK, LANES), F32)
    dma = lambda n: pltpu.SemaphoreType.DMA((n,))
    view = jax.ShapeDtypeStruct((T // ATT_BLOCKS, ATT_BLOCKS, ATTN_W), F32)
    outs = pl.pallas_call(
        body, name="attn_bwd", grid=(ATTN_W // LANES, nt),
        in_specs=[pl.BlockSpec((8, LANES), lambda c, t: (0, c))] + [_HBM] * (7 + ns),
        out_specs=[_HBM] * (1 + ns),
        out_shape=[jax.ShapeDtypeStruct((T // ATT_BLOCKS, ATT_BLOCKS, IN_W), F32)]
        + [jax.ShapeDtypeStruct(p.shape, p.dtype) for p in owner_grads],
        scratch_shapes=[tile(2), tile(2), tile(2), tile(2), tile(3), tile(3), tile(2), tile(4), tile(4),
                        pltpu.VMEM((ATT_BLOCKS, CHUNK, LANES), F32)]
        + [dma(2), dma(2), dma(2), dma(2), dma(3), dma(3), dma(2), dma(3), dma(3)]
        + (_owner_exchange_sems(ns) if ns else []),
        input_output_aliases={7: 0},
        compiler_params=_params(("arbitrary", "arbitrary")),
    )(_slope_table(), *[_residue_view(a) for a in (q, k, v, dattn, attn, lse, dproj)], *owner_grads)
    return outs[0].reshape(T, IN_W), tuple(outs[1:])


def _proj_bwd(dproj, w_in_t, x, g1, dh1, owner_grads=()):
    T = x.shape[0]
    tm = TM_PROJ
    ns = len(owner_grads)
    steps = T // tm

    def body(d_ref, w_ref, x_ref, g_ref, r_ref, *rest):
        p_refs, rest = rest[:ns], rest[ns:]
        dx_ref, dg_ref = rest[:2]
        r_refs, sems = rest[2:2 + ns], rest[2 + ns:]
        step = pl.program_id(0)
        if ns:
            start, finish = _owner_exchange_phases(p_refs, r_refs, *sems)
            pl.when(step == 0)(start)

        @pl.when(step == 0)
        def _():
            dg_ref[...] = jnp.zeros_like(dg_ref)

        dhn = _dot(d_ref[...].astype(BF16), w_ref[...])
        n1, r1 = _rms(x_ref[...])
        _accum_rows(dg_ref, dhn * n1)
        dx_ref[...] = r_ref[...] + _rms_bwd(n1, r1, g_ref[...], dhn)
        if ns:
            pl.when(step == steps - 1)(finish)

    outs = pl.pallas_call(
        body, name="proj_bwd", grid=(steps,),
        in_specs=[_rows(tm, IN_W), _resident((IN_W, D_MODEL)), _rows(tm, D_MODEL), _resident((1, D_MODEL)),
                  _rows(tm, D_MODEL)] + [_HBM] * ns,
        out_specs=[_rows(tm, D_MODEL), pl.BlockSpec((8, D_MODEL), lambda i: (0, 0))] + [_HBM] * ns,
        out_shape=[jax.ShapeDtypeStruct((T, D_MODEL), F32), jax.ShapeDtypeStruct((8, D_MODEL), F32)]
        + [jax.ShapeDtypeStruct(p.shape, p.dtype) for p in owner_grads],
        scratch_shapes=_owner_exchange_sems(ns) if ns else [],
        compiler_params=_params(("arbitrary",)),
    )(dproj, w_in_t, x, g1, dh1, *owner_grads)
    return outs[0], outs[1], tuple(outs[2:])


def _dw(a, b, name, tile, square_a=False, out_dtype=F32):
    T, ka = a.shape
    nb = b.shape[1]
    tka, tnb, tt = tile
    tt = min(tt, T)
    last = T // tt - 1

    def body(a_ref, b_ref, *refs):
        o_ref = refs[0]
        acc_ref = refs[1] if len(refs) > 1 else o_ref
        s = pl.program_id(2)

        @pl.when(s == 0)
        def _():
            acc_ref[...] = jnp.zeros_like(acc_ref)

        a_tile = a_ref[...]
        if square_a:
            a_tile = jnp.square(a_tile.astype(F32))
        acc_ref[...] += _dot(a_tile.astype(BF16), b_ref[...], TN)
        if acc_ref is not o_ref:
            @pl.when(s == last)
            def _():
                o_ref[...] = acc_ref[...].astype(out_dtype)

    return pl.pallas_call(
        body, name=name, grid=(ka // tka, nb // tnb, T // tt),
        in_specs=[pl.BlockSpec((tt, tka), lambda i, j, s: (s, i)), pl.BlockSpec((tt, tnb), lambda i, j, s: (s, j))],
        out_specs=pl.BlockSpec((tka, tnb), lambda i, j, s: (i, j)),
        out_shape=jax.ShapeDtypeStruct((ka, nb), out_dtype),
        scratch_shapes=[] if out_dtype == F32 else [pltpu.VMEM((tka, tnb), F32)],
        compiler_params=_params(("parallel", "parallel", "arbitrary")),
    )(a, b)


def _adamw_update(w, m, v, g):
    m2 = ADAM_B1 * m + (1.0 - ADAM_B1) * g
    v2 = ADAM_B2 * v + (1.0 - ADAM_B2) * jnp.square(g)
    m_hat = m2 / (1.0 - ADAM_B1 ** ADAM_STEP)
    v_hat = v2 / (1.0 - ADAM_B2 ** ADAM_STEP)
    return -ADAM_LR * (m_hat / (jnp.sqrt(v_hat) + ADAM_EPS) + ADAM_WD * w), m2, v2


def _adamw_tiny(ws, ms, vs, parts):
    n = len(ws)
    P = parts.shape[0]

    def body(*refs):
        w_refs, m_refs, v_refs, p_ref = refs[:n], refs[n:2 * n], refs[2 * n:3 * n], refs[3 * n]
        outs = refs[3 * n + 1:]

        def total(slot, rows):
            g = p_ref[0, 8 * slot:8 * slot + rows, :]
            for i in range(1, P):
                g = g + p_ref[i, 8 * slot:8 * slot + rows, :]
            return g

        for k in range(n):
            g = total(k, ws[k].shape[0])
            outs[4 * k][...] = g
            outs[4 * k + 1][...], outs[4 * k + 2][...], outs[4 * k + 3][...] = _adamw_update(
                w_refs[k][...], m_refs[k][...], v_refs[k][...], g)
        outs[4 * n][...] = total(n, 8)

    sds = jax.ShapeDtypeStruct
    return pl.pallas_call(
        body, name="adamw_tiny",
        out_shape=[sds(w.shape, F32) for w in ws for _ in range(4)] + [sds((8, LANES), F32)],
    )(*ws, *ms, *vs, parts)


def _adamw(w, m, v, parts, name, tr, transposed=False):
    R, C = w.shape
    P = parts.shape[0]

    def body(w_ref, m_ref, v_ref, p_ref, g_ref, d_ref, m2_ref, v2_ref):
        g = p_ref[0].astype(F32)
        for i in range(1, P):
            g = g + p_ref[i].astype(F32)
        if transposed:
            g = g.T
        g_ref[...] = g
        d_ref[...], m2_ref[...], v2_ref[...] = _adamw_update(w_ref[...], m_ref[...], v_ref[...], g)

    spec = _rows(tr, C)
    part_spec = (pl.BlockSpec((P, C, tr), lambda i: (0, 0, i)) if transposed
                 else pl.BlockSpec((P, tr, C), lambda i: (0, i, 0)))
    return pl.pallas_call(
        body, name=name, grid=(R // tr,),
        in_specs=[spec, spec, spec, part_spec],
        out_specs=[spec] * 4,
        out_shape=[jax.ShapeDtypeStruct((R, C), F32)] * 4,
        compiler_params=_params(("parallel",)),
    )(w, m, v, parts)


_HBM = pl.BlockSpec(memory_space=pltpu.HBM)


def _place():
    return lax.axis_index("x"), lax.axis_index("y"), lax.axis_index("c")


def _gathered_shape(shard):
    return jax.ShapeDtypeStruct((N_DEV,) + shard.shape, shard.dtype)


def _gather_sems(n):
    return [pltpu.SemaphoreType.DMA((7, n)), pltpu.SemaphoreType.DMA((7, n)), pltpu.SemaphoreType.DMA((n,))]


def _gather_phases(x_refs, out_refs, send_sems, recv_sems, local_sems):
    x, y, c = _place()
    me, sibling = (x, y, c), (x, y, 1 - c)
    chips = [(1 - x, y), (x, 1 - y), (1 - x, 1 - y)]
    arrays = range(len(x_refs))

    def slot(i, px, py, pc):
        return out_refs[i].at[4 * px + 2 * py + pc]

    def copy(i, k, block, to, own=False):
        return pltpu.make_async_remote_copy(
            src_ref=x_refs[i] if own else slot(i, *block), dst_ref=slot(i, *block),
            send_sem=send_sems.at[k, i], recv_sem=recv_sems.at[k, i], device_id=to, device_id_type=MESH)

    def mine(i):
        return pltpu.make_async_copy(x_refs[i], slot(i, *me), local_sems.at[i])

    def start():
        for i in arrays:
            mine(i).start()
            copy(i, 0, me, sibling, own=True).start()
            for j, chip in enumerate(chips):
                copy(i, 1 + j, me, (*chip, c), own=True).start()

    def forward():
        for i in arrays:
            for j, chip in enumerate(chips):
                copy(i, 1 + j, (*chip, c), me).wait_recv()
                copy(i, 4 + j, (*chip, c), sibling).start()

    def finish():
        for i in arrays:
            copy(i, 0, sibling, me).wait_recv()
            copy(i, 0, me, sibling, own=True).wait_send()
            for j, chip in enumerate(chips):
                copy(i, 4 + j, (*chip, 1 - c), me).wait_recv()
                copy(i, 1 + j, me, (*chip, c), own=True).wait_send()
                copy(i, 4 + j, (*chip, c), sibling).wait_send()
            mine(i).wait()

    return start, forward, finish


def _all_gather(shards, name):
    n = len(shards)

    def body(*refs):
        start, forward, finish = _gather_phases(refs[:n], refs[n:2 * n], *refs[2 * n:])
        start()
        forward()
        finish()

    return pl.pallas_call(
        body, name=name,
        out_shape=[_gathered_shape(s) for s in shards],
        in_specs=[_HBM] * n, out_specs=[_HBM] * n,
        scratch_shapes=_gather_sems(n),
    )(*shards)


def _owner_exchange_sems(n):
    return [pltpu.SemaphoreType.DMA((7, n)), pltpu.SemaphoreType.DMA((7, n)), pltpu.SemaphoreType.DMA((n,))]


def _owner_exchange_phases(g_refs, r_refs, send_sems, recv_sems, local_sems):
    x, y, c = _place()
    me = 4 * x + 2 * y + c
    flip = lambda v, bit: 1 - v if bit else v
    peers = [(flip(x, k & 4), flip(y, k & 2), flip(c, k & 1)) for k in range(1, N_DEV)]
    arrays = range(len(g_refs))

    def mine(i):
        return pltpu.make_async_copy(g_refs[i].at[me], r_refs[i].at[me], local_sems.at[i])

    def copy(i, k, src_slot, dst_slot):
        return pltpu.make_async_remote_copy(
            src_ref=g_refs[i].at[src_slot], dst_ref=r_refs[i].at[dst_slot],
            send_sem=send_sems.at[k, i], recv_sem=recv_sems.at[k, i], device_id=peers[k], device_id_type=MESH)

    def start():
        for i in arrays:
            mine(i).start()
            for k, (px, py, pc) in enumerate(peers):
                copy(i, k, 4 * px + 2 * py + pc, me).start()

    def finish():
        for i in arrays:
            for k, (px, py, pc) in enumerate(peers):
                copy(i, k, me, 4 * px + 2 * py + pc).wait_recv()
                copy(i, k, 4 * px + 2 * py + pc, me).wait_send()
            mine(i).wait()

    return start, finish


def _local_step(x, tgt, small, w_in_t, rest, exchange=False):
    g1, g2, gf = small["norm1_g"], small["norm2_g"], small["final_norm_g"].reshape(1, D_MODEL)
    ga, gg = small["attn_out_g"], small["gmlp_out_g"]
    ln_g = small["sgu_ln_g"].reshape(1, GMLP_W)
    ln_b = small["sgu_ln_b"].reshape(1, GMLP_W)
    sgu_w = small["sgu_w"][0]
    sgu_bt = small["sgu_b"][0].T

    hn1, q, k, v, u, z = _proj_fwd(x, g1, w_in_t)
    attn, lse, gathered = _attn_fwd(q, k, v, shards=rest if exchange else ())
    w_out, w_ff1_t, w_ff2 = [g.reshape(-1, D_MODEL) for g in gathered] if exchange else rest
    gm = _gmlp_fwd(u, z, ln_g, ln_b, sgu_w, sgu_bt)
    mixed, h1, hn2 = _out_fwd(attn, gm, ga, gg, w_out, x, g2)
    relu, dh2f, dh2b, loss8, dgf8 = _ffn_fwd(hn2, h1, w_ff1_t, w_ff2, gf, tgt)

    da, dh1f, dh1b, dg2 = _ffn_bwd(dh2b, dh2f, relu, h1, g2, w_ff2, w_ff1_t)
    wire = BF16 if exchange else F32
    dw_ff2 = _dw(relu, dh2b, "dw_ff2", DW_TILE, square_a=True, out_dtype=wire)
    dw_ff1_t = _dw(da, hn2, "dw_ff1", DW_TILE, out_dtype=wire)
    dattn, dgm, dga, dgg = _out_bwd(dh1b, w_out, attn, gm, ga, gg)
    dw_out = _dw(mixed, dh1b, "dw_out", DW_TILE, out_dtype=wire)
    early = [dw_out, dw_ff1_t, dw_ff2]
    if exchange:
        early = [g.reshape(N_DEV, -1, D_MODEL) for g in early]
    dproj, dlg, dlb, dsw, dsb = _gmlp_bwd(u, z, dgm, ln_g, ln_b, sgu_w, sgu_bt)
    dproj, arrived = _attn_bwd(q, k, v, dattn, attn, lse, dproj, owner_grads=early if exchange else ())
    dw_in_t = _dw(dproj, hn1, "dw_in", DW_TILE_IN, out_dtype=wire)
    late = (dw_in_t.reshape(N_DEV, -1, D_MODEL),) if exchange else ()
    dx, dg1, late = _proj_bwd(dproj, w_in_t, x, g1, dh1f, owner_grads=late)
    if exchange:
        dw_in_t, early = late[0], arrived

    small_grads = dict(
        norm1_g=dg1[0], sgu_ln_g=dlg[0], sgu_ln_b=dlb[0], sgu_w=dsw, sgu_b=dsb[:, :N_GROUPS].T,
        attn_out_g=dga[0], gmlp_out_g=dgg[0], norm2_g=dg2[0], final_norm_g=dgf8[0])
    return loss8[0, 0], dx, (dw_in_t, *early), small_grads


SMALL_NAMES = ("norm1_g", "sgu_ln_g", "sgu_ln_b", "sgu_w", "sgu_b", "attn_out_g", "gmlp_out_g", "norm2_g",
               "final_norm_g")
WEIGHT_ORDER = ("norm1_g", "w_in", "sgu_ln_g", "sgu_ln_b", "sgu_w", "sgu_b", "attn_out_g", "gmlp_out_g", "w_out",
                "norm2_g", "w_ff1", "w_ff2", "final_norm_g")


TINY_NAMES = tuple(n for n in SMALL_NAMES if n != "sgu_w")


def _as_rows(a):
    return a.reshape(-1, LANES)


def _pack_tiny_grads(d, loss):
    slots = [jnp.pad(_as_rows(d[n]), ((0, 8 - d[n].size // LANES), (0, 0))) for n in TINY_NAMES]
    return jnp.concatenate(slots + [jnp.full((8, LANES), loss, F32)], axis=0)


def kernel(x, norm1_g, w_in, sgu_ln_g, sgu_ln_b, sgu_w, sgu_b, attn_out_g, gmlp_out_g, w_out, norm2_g, w_ff1, w_ff2, final_norm_g, loss_target, m_norm1_g, m_w_in, m_sgu_ln_g, m_sgu_ln_b, m_sgu_w, m_sgu_b, m_attn_out_g, m_gmlp_out_g, m_w_out, m_norm2_g, m_w_ff1, m_w_ff2, m_final_norm_g, v_norm1_g, v_w_in, v_sgu_ln_g, v_sgu_ln_b, v_sgu_w, v_sgu_b, v_attn_out_g, v_gmlp_out_g, v_w_out, v_norm2_g, v_w_ff1, v_w_ff2, v_final_norm_g):
    w = dict(norm1_g=norm1_g, w_in=w_in, sgu_ln_g=sgu_ln_g, sgu_ln_b=sgu_ln_b, sgu_w=sgu_w, sgu_b=sgu_b,
             attn_out_g=attn_out_g, gmlp_out_g=gmlp_out_g, w_out=w_out, norm2_g=norm2_g, w_ff1=w_ff1, w_ff2=w_ff2,
             final_norm_g=final_norm_g)
    m = dict(norm1_g=m_norm1_g, w_in=m_w_in, sgu_ln_g=m_sgu_ln_g, sgu_ln_b=m_sgu_ln_b, sgu_w=m_sgu_w, sgu_b=m_sgu_b,
             attn_out_g=m_attn_out_g, gmlp_out_g=m_gmlp_out_g, w_out=m_w_out, norm2_g=m_norm2_g, w_ff1=m_w_ff1,
             w_ff2=m_w_ff2, final_norm_g=m_final_norm_g)
    v = dict(norm1_g=v_norm1_g, w_in=v_w_in, sgu_ln_g=v_sgu_ln_g, sgu_ln_b=v_sgu_ln_b, sgu_w=v_sgu_w, sgu_b=v_sgu_b,
             attn_out_g=v_attn_out_g, gmlp_out_g=v_gmlp_out_g, w_out=v_w_out, norm2_g=v_norm2_g, w_ff1=v_w_ff1,
             w_ff2=v_w_ff2, final_norm_g=v_final_norm_g)
    big = ("w_in", "w_out", "w_ff1", "w_ff2")

    w_in_t, = _all_gather([w_in[0].T.astype(BF16)], "w_in_all_gather")
    rest = (w_out[0].astype(BF16), w_ff1[0].T.astype(BF16), w_ff2[0].astype(BF16))
    loss, dx, parts, small_grads = _local_step(x[0], loss_target[0], {n: w[n] for n in SMALL_NAMES},
                                               w_in_t.reshape(IN_W, D_MODEL), rest, exchange=True)

    new = {}
    for n, p, transposed, tr in zip(big, parts, (True, False, True, False), (128, 128, 128, 256)):
        new[n] = [a[None] for a in _adamw(w[n][0], m[n][0], v[n][0], p, "adamw_" + n, tr, transposed)]

    tiny_parts, sgu_parts = _all_gather(
        [_pack_tiny_grads(small_grads, loss), _as_rows(small_grads["sgu_w"]).astype(BF16)], "small_grad_all_gather")
    tiny = _adamw_tiny(*[[_as_rows(src[n]) for n in TINY_NAMES] for src in (w, m, v)], tiny_parts)
    sgu = _adamw(_as_rows(sgu_w), _as_rows(m_sgu_w), _as_rows(v_sgu_w), sgu_parts, "adamw_sgu_w", 512)
    loss = tiny[-1][0, 0]

    outs = []
    for i in range(4):
        d = {n: new[n][i] for n in big}
        d.update({n: tiny[4 * k + i].reshape(w[n].shape) for k, n in enumerate(TINY_NAMES)})
        d["sgu_w"] = sgu[i].reshape(sgu_w.shape)
        outs.extend(d[n] for n in WEIGHT_ORDER)
    return (loss, dx[None], *outs)
```

```python
import math

import numpy as np
import jax
import jax.numpy as jnp
from jax import lax
from jax.experimental import pallas as pl
from jax.experimental.pallas import tpu as pltpu

F32 = jnp.float32
BF16 = jnp.bfloat16

D_MODEL = 1024
HEAD_DIM = 64
N_HEADS = 12
ATTN_W = N_HEADS * HEAD_DIM
N_GROUPS = 4
GMLP_W = N_GROUPS * HEAD_DIM
IN_W = 3 * ATTN_W + 2 * GMLP_W
D_FF = 4 * D_MODEL
CHUNK = 128
DILATIONS = (1, 4, 16)
EPS = 1e-6
Q_SCALE = HEAD_DIM ** -0.5
NEG = -1e30

ADAM_LR, ADAM_B1, ADAM_B2, ADAM_EPS, ADAM_WD, ADAM_STEP = 0.001, 0.9, 0.999, 1e-08, 0.01, 10

N_DEV = 8
LANES = 128
VMEM_LIMIT = 56 << 20

TM_PROJ = 512
TM_FFN = 512
FF_CHUNK = 512
TM_GMLP = 1024
DW_TILE = (512, 1024, 8192)
DW_TILE_IN = (IN_W // 2, 1024, 2048)

MESH = pl.DeviceIdType.MESH


def _alibi_slopes(n):
    def pow2(m):
        start = 2.0 ** (-8.0 / m)
        return [start ** (i + 1) for i in range(m)]
    c = 2 ** int(math.floor(math.log2(n)))
    s = pow2(n) if c == n else pow2(c) + pow2(2 * c)[0::2][: n - c]
    return np.asarray(s, dtype=np.float32)


SLOPES = _alibi_slopes(N_HEADS)


def _params(sem=None):
    kw = dict(vmem_limit_bytes=VMEM_LIMIT)
    if sem is not None:
        kw["dimension_semantics"] = sem
    return pltpu.CompilerParams(**kw)


def _rows(tm, n):
    return pl.BlockSpec((tm, n), lambda i: (i, 0))


def _resident(shape):
    return pl.BlockSpec(shape, lambda *_: (0,) * len(shape), pipeline_mode=pl.Buffered(1))


def _rms(x):
    r = lax.rsqrt(jnp.mean(x * x, axis=-1, keepdims=True) + EPS)
    return x * r, r


def _rms_bwd(n, r, g, dy):
    dn = dy * g
    return r * (dn - n * jnp.mean(dn * n, axis=-1, keepdims=True))


def _accum_rows(acc_ref, v):
    acc_ref[...] += jnp.broadcast_to(jnp.sum(v, axis=0, keepdims=True), acc_ref.shape)


_G0 = math.sqrt(2.0 / math.pi)
_G1 = 0.044715


def _gelu(x):
    t = jnp.tanh(_G0 * (x + _G1 * (x * x * x)))
    return x * (0.5 * (1.0 + t)), t


def _gelu_grad(x, t):
    return 0.5 * (1.0 + t) + 0.5 * x * (1.0 - t * t) * (_G0 * (1.0 + 3.0 * _G1 * x * x))


NT = (((1,), (1,)), ((), ()))
TN = (((0,), (0,)), ((), ()))


def _dot(a, b, dims=None):
    if dims is None:
        return jnp.dot(a, b, preferred_element_type=F32)
    return lax.dot_general(a, b, dims, preferred_element_type=F32)


def _proj_fwd(x, g1, w_in_t):
    T = x.shape[0]
    tm = TM_PROJ

    def body(x_ref, g_ref, w_ref, hn_ref, q_ref, k_ref, v_ref, u_ref, z_ref):
        n, _ = _rms(x_ref[...])
        hn = (n * g_ref[...]).astype(BF16)
        hn_ref[...] = hn
        a = ATTN_W
        q_ref[...] = _dot(hn, w_ref[0:a, :], NT) * Q_SCALE
        k_ref[...] = _dot(hn, w_ref[a:2 * a, :], NT)
        v_ref[...] = _dot(hn, w_ref[2 * a:3 * a, :], NT)
        u_ref[...] = _dot(hn, w_ref[3 * a:3 * a + GMLP_W, :], NT)
        z_ref[...] = _dot(hn, w_ref[3 * a + GMLP_W:, :], NT)

    sds = jax.ShapeDtypeStruct
    return pl.pallas_call(
        body, name="proj_fwd", grid=(T // tm,),
        in_specs=[_rows(tm, D_MODEL), _resident((1, D_MODEL)), _resident((IN_W, D_MODEL))],
        out_specs=[_rows(tm, D_MODEL), _rows(tm, ATTN_W), _rows(tm, ATTN_W), _rows(tm, ATTN_W),
                   _rows(tm, GMLP_W), _rows(tm, GMLP_W)],
        out_shape=[sds((T, D_MODEL), BF16), sds((T, ATTN_W), F32), sds((T, ATTN_W), F32),
                   sds((T, ATTN_W), F32), sds((T, GMLP_W), F32), sds((T, GMLP_W), F32)],
        compiler_params=_params(("parallel",)),
    )(x, g1, w_in_t)


ATT_TILE = 2048
ATT_BLOCKS = ATT_TILE // CHUNK
SM_BLOCKS = 4


def _slope_table():
    row = np.repeat(SLOPES, HEAD_DIM)
    return jnp.asarray(np.broadcast_to(row[None], (8, ATTN_W)), F32)


def _residue_view(a):
    return a.reshape(a.shape[0] // ATT_BLOCKS, ATT_BLOCKS, a.shape[1])


def _tile_copies(hbm, buf, sem, hp, t, to_hbm=False, lane0=0):
    rows = pl.ds(pl.multiple_of(t * CHUNK, CHUNK), CHUNK)
    lanes = pl.ds(pl.multiple_of(lane0 + hp * LANES, LANES), LANES)
    pairs = [(hbm.at[rows, r, lanes], buf.at[r]) for r in range(ATT_BLOCKS)]
    return [pltpu.make_async_copy(v, h, sem) if to_hbm else pltpu.make_async_copy(h, v, sem) for h, v in pairs]


def _wait_tile(buf, sem):
    pltpu.make_async_copy(buf, buf, sem).wait()


def _residue_rows(d, j):
    if d == 16:
        return [(j, 0, CHUNK)]
    if d == 4:
        return [(j % 4 + 4 * m, 32 * (j // 4), 32) for m in range(4)]
    return [(r, 8 * j, 8) for r in range(ATT_BLOCKS)]


def _block_order(p, d):
    if d == 16:
        return p
    if d == 4:
        return 4 * (p & 31) + (p >> 5)
    return 16 * (p & 7) + (p >> 3)


def _first_in_tile(d, j):
    return _residue_rows(d, j)[0][1] == 0


def _rm_block(buf, d, j):
    return jnp.concatenate([buf[r, lo:lo + n, :] for r, lo, n in _residue_rows(d, j)], axis=0)


def _rm_block_before(buf, buf_before, d, j):
    if _first_in_tile(d, j):
        return jnp.concatenate([buf_before[r, CHUNK - n:CHUNK, :] for r, _, n in _residue_rows(d, j)], axis=0)
    return jnp.concatenate([buf[r, lo - n:lo, :] for r, lo, n in _residue_rows(d, j)], axis=0)


def _rm_store(buf, d, j, val):
    at = 0
    for r, lo, n in _residue_rows(d, j):
        buf[r, lo:lo + n, :] = val[at:at + n, :]
        at += n


def _rm_add(buf, rows, val, first=False):
    at = 0
    for r, lo, n in rows:
        if first:
            buf[r, lo:lo + n, :] = val[at:at + n, :]
        else:
            buf[r, lo:lo + n, :] += val[at:at + n, :]
        at += n


def _residue_bias(sl_ref, d):
    shape = (2 * CHUNK, 2 * CHUNK)
    row = lax.broadcasted_iota(jnp.int32, shape, 0)
    col = lax.broadcasted_iota(jnp.int32, shape, 1)
    steps = _block_order(row & (CHUNK - 1), d) + CHUNK - (_block_order(col & (CHUNK - 1), d) + (col & CHUNK))
    band = (steps >= 0) & (steps <= CHUNK)
    sl = sl_ref[0:1, :]
    upper = lax.broadcasted_iota(jnp.int32, (2 * CHUNK, 1), 0) < CHUNK
    slope2 = jnp.where(upper, sl[:, 0:1], sl[:, HEAD_DIM:HEAD_DIM + 1])
    return jnp.where(band, -(float(d) * slope2 * steps.astype(F32)), NEG)


def _stack_heads(xb, head0):
    zero = jnp.zeros_like(xb)
    return jnp.concatenate([jnp.where(head0, xb, zero), jnp.where(head0, zero, xb)], axis=0).astype(BF16)


def _unstack_heads(x2, head0):
    return jnp.where(head0, x2[:CHUNK, :], x2[CHUNK:, :])


def _attn_fwd(q, k, v, shards=()):
    T = q.shape[0]
    nt = T // ATT_TILE
    ns = len(shards)
    steps = (ATTN_W // LANES) * nt

    def body(sl_ref, q_hbm, k_hbm, v_hbm, *rest):
        x_refs, rest = rest[:ns], rest[ns:]
        attn_hbm, lse_hbm = rest[:2]
        g_refs, rest = rest[2:2 + ns], rest[2 + ns:]
        qbuf, kbuf, vbuf, obuf, lbuf = rest[:5]
        o_acc, l_acc = rest[5:8], rest[8:11]
        sem_q, sem_k, sem_v, sem_o, sem_l = rest[11:16]
        hp, t = pl.program_id(0), pl.program_id(1)
        step = hp * nt + t
        two, three = step % 2, step % 3
        before, after = (step + 2) % 3, (step + 1) % 3
        if ns:
            start, forward, finish = _gather_phases(x_refs, g_refs, *rest[16:])
            pl.when(step == 0)(start)
            pl.when(step == steps // 2)(forward)

        def fetch(hp_, t_, two_, three_):
            for cp in (_tile_copies(q_hbm, qbuf.at[two_], sem_q.at[two_], hp_, t_)
                       + _tile_copies(k_hbm, kbuf.at[three_], sem_k.at[three_], hp_, t_)
                       + _tile_copies(v_hbm, vbuf.at[three_], sem_v.at[three_], hp_, t_)):
                cp.start()

        @pl.when(step == 0)
        def _():
            kbuf[2] = jnp.zeros((ATT_BLOCKS, CHUNK, LANES), F32)
            vbuf[2] = jnp.zeros((ATT_BLOCKS, CHUNK, LANES), F32)
            fetch(0, 0, 0, 0)

        @pl.when(step + 1 < steps)
        def _():
            fetch((step + 1) // nt, (step + 1) % nt, 1 - two, after)

        _wait_tile(qbuf.at[two], sem_q.at[two])
        _wait_tile(kbuf.at[three], sem_k.at[three])
        _wait_tile(vbuf.at[three], sem_v.at[three])

        @pl.when(step >= 2)
        def _():
            _wait_tile(obuf.at[two], sem_o.at[two])
            _wait_tile(lbuf.at[two], sem_l.at[two])

        q_t, k_t, v_t = qbuf.at[two], kbuf.at[three], vbuf.at[three]
        k_b, v_b = kbuf.at[before], vbuf.at[before]
        head0 = lax.broadcasted_iota(jnp.int32, (CHUNK, LANES), 1) < HEAD_DIM
        no_key_before = jnp.where(lax.broadcasted_iota(jnp.int32, (2 * CHUNK, 2 * CHUNK), 1) < CHUNK, NEG, 0.0)
        for pi, d in enumerate(DILATIONS):
            bias = _residue_bias(sl_ref, d)

            def scores(j, d=d, bias=bias):
                kcat = jnp.concatenate([_rm_block_before(k_t, k_b, d, j), _rm_block(k_t, d, j)], axis=0).astype(BF16)
                vcat = jnp.concatenate([_rm_block_before(v_t, v_b, d, j), _rm_block(v_t, d, j)], axis=0).astype(BF16)
                s = _dot(_stack_heads(_rm_block(q_t, d, j), head0), kcat, NT) + bias
                if _first_in_tile(d, j):
                    s = s + jnp.where(t == 0, 1.0, 0.0) * no_key_before
                return s, vcat

            def output(j, p, vcat, scale, lse, d=d, pi=pi):
                _rm_store(o_acc[pi], d, j, _unstack_heads(_dot(p, vcat) * scale, head0))
                _rm_store(l_acc[pi], d, j, _unstack_heads(jnp.broadcast_to(lse, (2 * CHUNK, LANES)), head0))

            for j0 in range(0, ATT_BLOCKS, SM_BLOCKS):
                group = [scores(j) for j in range(j0, j0 + SM_BLOCKS)]
                s = jnp.concatenate([g[0] for g in group], axis=0)
                m = jnp.max(s, axis=-1, keepdims=True)
                p = jnp.exp(s - m)
                l = jnp.sum(p, axis=-1, keepdims=True)
                p, scale, lse = p.astype(BF16), 1.0 / l, m + jnp.log(l)
                for i, (_, vcat) in enumerate(group):
                    rows = slice(i * 2 * CHUNK, (i + 1) * 2 * CHUNK)
                    output(j0 + i, p[rows, :], vcat, scale[rows, :], lse[rows, :])

        for r in range(ATT_BLOCKS):
            a, b, c = l_acc[0][r], l_acc[1][r], l_acc[2][r]
            m = jnp.maximum(jnp.maximum(a, b), c)
            ea, eb, ec = jnp.exp(a - m), jnp.exp(b - m), jnp.exp(c - m)
            tot = ea + eb + ec
            obuf[two, r] = (ea * o_acc[0][r] + eb * o_acc[1][r] + ec * o_acc[2][r]) / tot
            lbuf[two, r] = m + jnp.log(tot)

        for cp in (_tile_copies(attn_hbm, obuf.at[two], sem_o.at[two], hp, t, to_hbm=True)
                   + _tile_copies(lse_hbm, lbuf.at[two], sem_l.at[two], hp, t, to_hbm=True)):
            cp.start()

        @pl.when(step == steps - 1)
        def _():
            for slot in (two, 1 - two)[:min(steps, 2)]:
                _wait_tile(obuf.at[slot], sem_o.at[slot])
                _wait_tile(lbuf.at[slot], sem_l.at[slot])

        if ns:
            pl.when(step == steps - 1)(finish)

    tile = lambda n: pltpu.VMEM((n, ATT_BLOCKS, CHUNK, LANES), F32)
    dma = lambda n: pltpu.SemaphoreType.DMA((n,))
    view = jax.ShapeDtypeStruct((T // ATT_BLOCKS, ATT_BLOCKS, ATTN_W), F32)
    outs = pl.pallas_call(
        body, name="attn_fwd", grid=(ATTN_W // LANES, nt),
        in_specs=[pl.BlockSpec((8, LANES), lambda c, t: (0, c))] + [_HBM] * (3 + ns),
        out_specs=[_HBM] * (2 + ns),
        out_shape=[view, view] + [_gathered_shape(s) for s in shards],
        scratch_shapes=[tile(2), tile(3), tile(3), tile(2), tile(2)] + [pltpu.VMEM((ATT_BLOCKS, CHUNK, LANES), F32)] * 6
        + [dma(2), dma(3), dma(3), dma(2), dma(2)] + (_gather_sems(ns) if ns else []),
        compiler_params=_params(("arbitrary", "arbitrary")),
    )(_slope_table(), _residue_view(q), _residue_view(k), _residue_view(v), *shards)
    return outs[0].reshape(T, ATTN_W), outs[1].reshape(T, ATTN_W), tuple(outs[2:])


def _group_mean(v, grp):
    out = jnp.zeros_like(v)
    for g in range(N_GROUPS):
        mk = grp == g
        s = jnp.sum(jnp.where(mk, v, 0.0), axis=-1, keepdims=True) * (1.0 / HEAD_DIM)
        out = jnp.where(mk, s, out)
    return out


def _gmlp_core(uu, zz, lg, lb, ws, sb_ref, grp):
    ug, tu = _gelu(uu)
    zg, tz = _gelu(zz)
    zc = zg - _group_mean(zg, grp)
    rstd = lax.rsqrt(_group_mean(zc * zc, grp) + EPS)
    xhat = zc * rstd
    zn16 = (xhat * lg + lb).astype(BF16)
    mixed = []
    for ci in range(uu.shape[0] // CHUNK):
        rows = slice(ci * CHUNK, (ci + 1) * CHUNK)
        m = jnp.zeros((CHUNK, GMLP_W), F32)
        for g in range(N_GROUPS):
            m = jnp.where(grp[:CHUNK] == g, _dot(ws[g], zn16[rows, :]) + sb_ref[:, g:g + 1], m)
        mixed.append(m)
    return ug, tu, tz, xhat, rstd, zn16, jnp.concatenate(mixed, axis=0)


def _causal_ws(w_ref):
    ti = lax.broadcasted_iota(jnp.int32, (CHUNK, CHUNK), 0)
    si = lax.broadcasted_iota(jnp.int32, (CHUNK, CHUNK), 1)
    causal = si <= ti
    return causal, [jnp.where(causal, w_ref[g], 0.0).astype(BF16) for g in range(N_GROUPS)]


def _gmlp_fwd(u, z, ln_g, ln_b, sgu_w, sgu_bt):
    T = u.shape[0]
    tg = TM_GMLP

    def body(u_ref, z_ref, g_ref, b_ref, w_ref, sb_ref, out_ref):
        grp = lax.broadcasted_iota(jnp.int32, (tg, GMLP_W), 1) // HEAD_DIM
        _, ws = _causal_ws(w_ref)
        ug, _, _, _, _, _, mixed = _gmlp_core(u_ref[...], z_ref[...], g_ref[...], b_ref[...], ws, sb_ref, grp)
        out_ref[...] = ug * mixed

    return pl.pallas_call(
        body, name="gmlp_fwd", grid=(T // tg,),
        in_specs=[_rows(tg, GMLP_W), _rows(tg, GMLP_W), _resident((1, GMLP_W)), _resident((1, GMLP_W)),
                  _resident((N_GROUPS, CHUNK, CHUNK)), _resident((CHUNK, N_GROUPS))],
        out_specs=_rows(tg, GMLP_W),
        out_shape=jax.ShapeDtypeStruct((T, GMLP_W), F32),
        compiler_params=_params(("parallel",)),
    )(u, z, ln_g, ln_b, sgu_w, sgu_bt)


def _out_fwd(attn, gm, ga, gg, w_out, x, g2):
    T = x.shape[0]
    tm = TM_PROJ

    def body(a_ref, m_ref, ga_ref, gg_ref, w_ref, x_ref, g2_ref, mix_ref, h1_ref, hn2_ref):
        an, _ = _rms(a_ref[...])
        gn, _ = _rms(m_ref[...])
        an = (an * ga_ref[...]).astype(BF16)
        gn = (gn * gg_ref[...]).astype(BF16)
        mix_ref[:, 0:ATTN_W] = an
        mix_ref[:, ATTN_W:] = gn
        h1 = x_ref[...] + _dot(an, w_ref[0:ATTN_W, :]) + _dot(gn, w_ref[ATTN_W:, :])
        h1_ref[...] = h1
        n2, _ = _rms(h1)
        hn2_ref[...] = (n2 * g2_ref[...]).astype(BF16)

    sds = jax.ShapeDtypeStruct
    return pl.pallas_call(
        body, name="out_fwd", grid=(T // tm,),
        in_specs=[_rows(tm, ATTN_W), _rows(tm, GMLP_W), _resident((1, ATTN_W)), _resident((1, GMLP_W)),
                  _resident((D_MODEL, D_MODEL)), _rows(tm, D_MODEL), _resident((1, D_MODEL))],
        out_specs=[_rows(tm, D_MODEL)] * 3,
        out_shape=[sds((T, D_MODEL), BF16), sds((T, D_MODEL), F32), sds((T, D_MODEL), BF16)],
        compiler_params=_params(("parallel",)),
    )(attn, gm, ga, gg, w_out, x, g2)


def _ffn_fwd(hn2, h1, w1t, w2, gf, tgt):
    T = h1.shape[0]
    tm = TM_FFN

    def body(hn_ref, h1_ref, w1_ref, w2_ref, gf_ref, t_ref, r_ref, dhf_ref, dhb_ref, loss_ref, dgf_ref):
        i = pl.program_id(0)

        @pl.when(i == 0)
        def _():
            loss_ref[...] = jnp.zeros_like(loss_ref)
            dgf_ref[...] = jnp.zeros_like(dgf_ref)

        hn = hn_ref[...]
        acc = h1_ref[...]
        for j in range(D_FF // FF_CHUNK):
            cols = slice(j * FF_CHUNK, (j + 1) * FF_CHUNK)
            r = jnp.maximum(_dot(hn, w1_ref[cols, :], NT), 0.0)
            r_ref[:, cols] = r.astype(BF16)
            act = jnp.square(r).astype(BF16)
            acc = acc + _dot(act, w2_ref[cols, :])
        n3, r3 = _rms(acc)
        gf_row = gf_ref[...]
        e = n3 * gf_row - t_ref[...]
        loss_ref[...] += 0.5 * jnp.sum(jnp.mean(e * e, axis=-1, keepdims=True))
        dy = e * (1.0 / D_MODEL)
        _accum_rows(dgf_ref, dy * n3)
        dh2 = _rms_bwd(n3, r3, gf_row, dy)
        dhf_ref[...] = dh2
        dhb_ref[...] = dh2.astype(BF16)

    sds = jax.ShapeDtypeStruct
    acc_spec = lambda n: pl.BlockSpec((8, n), lambda i: (0, 0))
    return pl.pallas_call(
        body, name="ffn_fwd", grid=(T // tm,),
        in_specs=[_rows(tm, D_MODEL), _rows(tm, D_MODEL), _resident((D_FF, D_MODEL)), _resident((D_FF, D_MODEL)),
                  _resident((1, D_MODEL)), _rows(tm, D_MODEL)],
        out_specs=[_rows(tm, D_FF), _rows(tm, D_MODEL), _rows(tm, D_MODEL), acc_spec(LANES), acc_spec(D_MODEL)],
        out_shape=[sds((T, D_FF), BF16), sds((T, D_MODEL), F32), sds((T, D_MODEL), BF16),
                   sds((8, LANES), F32), sds((8, D_MODEL), F32)],
        compiler_params=_params(("arbitrary",)),
    )(hn2, h1, w1t, w2, gf, tgt)


def _ffn_bwd(dh2b, dh2f, relu, h1, g2, w2, w1t):
    T = h1.shape[0]
    tm = TM_FFN

    def body(db_ref, df_ref, r_ref, h1_ref, g2_ref, w2_ref, w1t_ref, da_ref, d1f_ref, d1b_ref, dg_ref):
        @pl.when(pl.program_id(0) == 0)
        def _():
            dg_ref[...] = jnp.zeros_like(dg_ref)

        db = db_ref[...]
        acc = jnp.zeros((tm, D_MODEL), F32)
        for j in range(D_FF // FF_CHUNK):
            cols = slice(j * FF_CHUNK, (j + 1) * FF_CHUNK)
            da = (_dot(db, w2_ref[cols, :], NT) * (2.0 * r_ref[:, cols].astype(F32))).astype(BF16)
            da_ref[:, cols] = da
            acc = acc + _dot(da, w1t_ref[cols, :])
        n2, r2 = _rms(h1_ref[...])
        _accum_rows(dg_ref, acc * n2)
        dh1 = df_ref[...] + _rms_bwd(n2, r2, g2_ref[...], acc)
        d1f_ref[...] = dh1
        d1b_ref[...] = dh1.astype(BF16)

    sds = jax.ShapeDtypeStruct
    return pl.pallas_call(
        body, name="ffn_bwd", grid=(T // tm,),
        in_specs=[_rows(tm, D_MODEL), _rows(tm, D_MODEL), _rows(tm, D_FF), _rows(tm, D_MODEL),
                  _resident((1, D_MODEL)), _resident((D_FF, D_MODEL)), _resident((D_FF, D_MODEL))],
        out_specs=[_rows(tm, D_FF), _rows(tm, D_MODEL), _rows(tm, D_MODEL),
                   pl.BlockSpec((8, D_MODEL), lambda i: (0, 0))],
        out_shape=[sds((T, D_FF), BF16), sds((T, D_MODEL), F32), sds((T, D_MODEL), BF16), sds((8, D_MODEL), F32)],
        compiler_params=_params(("arbitrary",)),
    )(dh2b, dh2f, relu, h1, g2, w2, w1t)


def _out_bwd(dh1b, w_out, attn, gm, ga, gg):
    T = attn.shape[0]
    tm = TM_PROJ

    def body(d_ref, w_ref, a_ref, m_ref, ga_ref, gg_ref, da_ref, dm_ref, dga_ref, dgg_ref):
        @pl.when(pl.program_id(0) == 0)
        def _():
            dga_ref[...] = jnp.zeros_like(dga_ref)
            dgg_ref[...] = jnp.zeros_like(dgg_ref)

        d = d_ref[...]
        dan = _dot(d, w_ref[0:ATTN_W, :], NT)
        dgn = _dot(d, w_ref[ATTN_W:, :], NT)
        na, ra = _rms(a_ref[...])
        ng, rg = _rms(m_ref[...])
        _accum_rows(dga_ref, dan * na)
        _accum_rows(dgg_ref, dgn * ng)
        da_ref[...] = _rms_bwd(na, ra, ga_ref[...], dan)
        dm_ref[...] = _rms_bwd(ng, rg, gg_ref[...], dgn)

    sds = jax.ShapeDtypeStruct
    return pl.pallas_call(
        body, name="out_bwd", grid=(T // tm,),
        in_specs=[_rows(tm, D_MODEL), _resident((D_MODEL, D_MODEL)), _rows(tm, ATTN_W), _rows(tm, GMLP_W),
                  _resident((1, ATTN_W)), _resident((1, GMLP_W))],
        out_specs=[_rows(tm, ATTN_W), _rows(tm, GMLP_W), pl.BlockSpec((8, ATTN_W), lambda i: (0, 0)),
                   pl.BlockSpec((8, GMLP_W), lambda i: (0, 0))],
        out_shape=[sds((T, ATTN_W), F32), sds((T, GMLP_W), F32), sds((8, ATTN_W), F32), sds((8, GMLP_W), F32)],
        compiler_params=_params(("arbitrary",)),
    )(dh1b, w_out, attn, gm, ga, gg)


def _gmlp_bwd(u, z, dgm, ln_g, ln_b, sgu_w, sgu_bt):
    T = u.shape[0]
    tg = TM_GMLP
    nsteps = T // tg

    def body(u_ref, z_ref, d_ref, g_ref, b_ref, w_ref, sb_ref, dproj_hbm, dlg_ref, dlb_ref, dw_ref, dsb_ref,
             stage, sem):
        i = pl.program_id(0)
        slot = i % 2
        duz_ref = stage.at[slot]

        def to_dproj(step, buf):
            rows = pl.ds(pl.multiple_of(step * tg, tg), tg)
            return pltpu.make_async_copy(stage.at[buf], dproj_hbm.at[rows, pl.ds(3 * ATTN_W, 2 * GMLP_W)],
                                         sem.at[buf])

        @pl.when(i == 0)
        def _():
            for ref in (dlg_ref, dlb_ref, dw_ref, dsb_ref):
                ref[...] = jnp.zeros_like(ref)

        @pl.when(i >= 2)
        def _():
            to_dproj(i - 2, slot).wait()

        grp = lax.broadcasted_iota(jnp.int32, (tg, GMLP_W), 1) // HEAD_DIM
        lane = lax.broadcasted_iota(jnp.int32, (CHUNK, LANES), 1)
        causal, ws = _causal_ws(w_ref)
        lg = g_ref[...]
        uu, zz, dgm = u_ref[...], z_ref[...], d_ref[...]
        ug, tu, tz, xhat, rstd, zn16, mixed = _gmlp_core(uu, zz, lg, b_ref[...], ws, sb_ref, grp)
        dmx = dgm * ug
        duz_ref[:, 0:GMLP_W] = dgm * mixed * _gelu_grad(uu, tu)
        dmx16 = dmx.astype(BF16)
        dzn = []
        for ci in range(tg // CHUNK):
            rows = slice(ci * CHUNK, (ci + 1) * CHUNK)
            dmx_c, d = dmx16[rows, :], jnp.zeros((CHUNK, GMLP_W), F32)
            for g in range(N_GROUPS):
                mk = grp[:CHUNK] == g
                d = jnp.where(mk, _dot(ws[g], dmx_c, TN), d)
                dw_ref[g] += _dot(jnp.where(mk, dmx_c, jnp.zeros_like(dmx_c)), zn16[rows, :], NT)
            dzn.append(d)
        dzn = jnp.concatenate(dzn, axis=0)
        dsb = jnp.zeros((CHUNK, LANES), F32)
        for g in range(N_GROUPS):
            per_token = jnp.sum(jnp.where(grp == g, dmx, 0.0), axis=-1, keepdims=True)
            by_position = sum(per_token[ci * CHUNK:(ci + 1) * CHUNK] for ci in range(tg // CHUNK))
            dsb = jnp.where(lane == g, by_position, dsb)
        dsb_ref[...] += dsb
        _accum_rows(dlg_ref, dzn * xhat)
        _accum_rows(dlb_ref, dzn)
        dxh = dzn * lg
        dzg = rstd * (dxh - _group_mean(dxh, grp) - xhat * _group_mean(dxh * xhat, grp))
        duz_ref[:, GMLP_W:] = dzg * _gelu_grad(zz, tz)
        to_dproj(i, slot).start()

        @pl.when(i == nsteps - 1)
        def _():
            for g in range(N_GROUPS):
                dw_ref[g] = jnp.where(causal, dw_ref[g], 0.0)
            to_dproj(i, slot).wait()
            if nsteps >= 2:
                to_dproj(i - 1, 1 - slot).wait()

    sds = jax.ShapeDtypeStruct
    return pl.pallas_call(
        body, name="gmlp_bwd", grid=(nsteps,),
        in_specs=[_rows(tg, GMLP_W)] * 3 + [_resident((1, GMLP_W)), _resident((1, GMLP_W)),
                                              _resident((N_GROUPS, CHUNK, CHUNK)), _resident((CHUNK, N_GROUPS))],
        out_specs=[_HBM, pl.BlockSpec((8, GMLP_W), lambda i: (0, 0)),
                   pl.BlockSpec((8, GMLP_W), lambda i: (0, 0)),
                   pl.BlockSpec((N_GROUPS, CHUNK, CHUNK), lambda i: (0, 0, 0)),
                   pl.BlockSpec((CHUNK, LANES), lambda i: (0, 0))],
        out_shape=[sds((T, IN_W), F32), sds((8, GMLP_W), F32), sds((8, GMLP_W), F32),
                   sds((N_GROUPS, CHUNK, CHUNK), F32), sds((CHUNK, LANES), F32)],
        scratch_shapes=[pltpu.VMEM((2, tg, 2 * GMLP_W), F32), pltpu.SemaphoreType.DMA((2,))],
        compiler_params=_params(("arbitrary",)),
    )(u, z, dgm, ln_g, ln_b, sgu_w, sgu_bt)


def _attn_bwd(q, k, v, dattn, attn, lse, dproj, owner_grads=()):
    T = q.shape[0]
    nt = T // ATT_TILE
    ns = len(owner_grads)
    steps = (ATTN_W // LANES) * nt

    def body(sl_ref, q_hbm, k_hbm, v_hbm, do_hbm, o_hbm, lse_hbm, _, *rest):
        p_refs, rest = rest[:ns], rest[ns:]
        dq_hbm = dk_hbm = dv_hbm = rest[0]
        r_refs, rest = rest[1:1 + ns], rest[1 + ns:]
        qbuf, dobuf, obuf, lbuf, kbuf, vbuf, dqbuf, dkbuf, dvbuf, delta_s = rest[:10]
        sem_q, sem_do, sem_o, sem_l, sem_k, sem_v, sem_dq, sem_dk, sem_dv = rest[10:19]
        hp, t = pl.program_id(0), pl.program_id(1)
        step = hp * nt + t
        two, three = step % 2, step % 3
        before, after = (step + 2) % 3, (step + 1) % 3
        if ns:
            start, finish = _owner_exchange_phases(p_refs, r_refs, *rest[19:])
            pl.when(step == 0)(start)

        def fetch(hp_, t_, two_, three_):
            for hbm, buf, sem, slot in ((q_hbm, qbuf, sem_q, two_), (do_hbm, dobuf, sem_do, two_),
                                        (o_hbm, obuf, sem_o, two_), (lse_hbm, lbuf, sem_l, two_),
                                        (k_hbm, kbuf, sem_k, three_), (v_hbm, vbuf, sem_v, three_)):
                for cp in _tile_copies(hbm, buf.at[slot], sem.at[slot], hp_, t_):
                    cp.start()

        @pl.when(step == 0)
        def _():
            kbuf[2] = jnp.zeros((ATT_BLOCKS, CHUNK, LANES), F32)
            vbuf[2] = jnp.zeros((ATT_BLOCKS, CHUNK, LANES), F32)
            dkbuf[3] = jnp.zeros((ATT_BLOCKS, CHUNK, LANES), F32)
            dvbuf[3] = jnp.zeros((ATT_BLOCKS, CHUNK, LANES), F32)
            fetch(0, 0, 0, 0)

        @pl.when(step + 1 < steps)
        def _():
            fetch((step + 1) // nt, (step + 1) % nt, 1 - two, after)

        for buf, sem in ((qbuf, sem_q), (dobuf, sem_do), (obuf, sem_o), (lbuf, sem_l)):
            _wait_tile(buf.at[two], sem.at[two])
        _wait_tile(kbuf.at[three], sem_k.at[three])
        _wait_tile(vbuf.at[three], sem_v.at[three])

        @pl.when(step >= 2)
        def _():
            _wait_tile(dqbuf.at[two], sem_dq.at[two])

        @pl.when(step >= 3)
        def _():
            _wait_tile(dkbuf.at[three], sem_dk.at[three])
            _wait_tile(dvbuf.at[three], sem_dv.at[three])

        q_t, do_t, l_t, k_t, v_t = qbuf.at[two], dobuf.at[two], lbuf.at[two], kbuf.at[three], vbuf.at[three]
        k_b, v_b = kbuf.at[before], vbuf.at[before]
        dq_t, dk_t, dv_t = dqbuf.at[two], dkbuf.at[three], dvbuf.at[three]
        dk_b, dv_b = dkbuf.at[before], dvbuf.at[before]
        sink = jnp.where(t > 0, before, 3)
        dk_sink, dv_sink = dkbuf.at[sink], dvbuf.at[sink]
        head0 = lax.broadcasted_iota(jnp.int32, (CHUNK, LANES), 1) < HEAD_DIM
        for r in range(ATT_BLOCKS):
            dd = dobuf[two, r] * obuf[two, r]
            d0 = jnp.sum(jnp.where(head0, dd, 0.0), axis=-1, keepdims=True)
            d1 = jnp.sum(jnp.where(head0, 0.0, dd), axis=-1, keepdims=True)
            delta_s[r] = jnp.where(head0, d0, d1)

        def column(xb):
            return jnp.concatenate([xb[:, 0:1], xb[:, HEAD_DIM:HEAD_DIM + 1]], axis=0)

        no_key_before = jnp.where(lax.broadcasted_iota(jnp.int32, (2 * CHUNK, 2 * CHUNK), 1) < CHUNK, NEG, 0.0)
        for d in DILATIONS:
            bias = _residue_bias(sl_ref, d)
            def scores(j, d=d, bias=bias):
                kcat = jnp.concatenate([_rm_block_before(k_t, k_b, d, j), _rm_block(k_t, d, j)], axis=0).astype(BF16)
                vcat = jnp.concatenate([_rm_block_before(v_t, v_b, d, j), _rm_block(v_t, d, j)], axis=0).astype(BF16)
                q2 = _stack_heads(_rm_block(q_t, d, j), head0)
                do2 = _stack_heads(_rm_block(do_t, d, j), head0)
                s = _dot(q2, kcat, NT) + bias
                if _first_in_tile(d, j):
                    s = s + jnp.where(t == 0, 1.0, 0.0) * no_key_before
                return (s - column(_rm_block(l_t, d, j)), _dot(do2, vcat, NT) - column(_rm_block(delta_s, d, j)),
                        kcat, q2, do2)

            group = {}
            for j in range(ATT_BLOCKS):
                if j % SM_BLOCKS == 0:
                    group = {i: scores(i) for i in range(j, j + SM_BLOCKS)}
                    p_all = jnp.exp(jnp.concatenate([g[0] for g in group.values()], axis=0))
                    ds_all = (p_all * jnp.concatenate([g[1] for g in group.values()], axis=0)).astype(BF16)
                    p_all = p_all.astype(BF16)
                at = slice((j % SM_BLOCKS) * 2 * CHUNK, (j % SM_BLOCKS + 1) * 2 * CHUNK)
                ds, p16 = ds_all[at, :], p_all[at, :]
                _, _, kcat, q2, do2 = group[j]
                first = d == DILATIONS[0]
                _rm_add(dq_t, _residue_rows(d, j), _unstack_heads(_dot(ds, kcat), head0), first)
                ck = _dot(ds, q2, TN)
                cv = _dot(p16, do2, TN)
                _rm_add(dk_t, _residue_rows(d, j), ck[CHUNK:, :], first)
                _rm_add(dv_t, _residue_rows(d, j), cv[CHUNK:, :], first)
                if _first_in_tile(d, j):
                    rows = [(r, CHUNK - n, n) for r, _, n in _residue_rows(d, j)]
                    _rm_add(dk_sink, rows, ck[:CHUNK, :])
                    _rm_add(dv_sink, rows, cv[:CHUNK, :])
                else:
                    rows = [(r, lo - n, n) for r, lo, n in _residue_rows(d, j)]
                    _rm_add(dk_t, rows, ck[:CHUNK, :])
                    _rm_add(dv_t, rows, cv[:CHUNK, :])

        for r in range(ATT_BLOCKS):
            dqbuf[two, r] = dqbuf[two, r] * Q_SCALE
        for cp in _tile_copies(dq_hbm, dq_t, sem_dq.at[two], hp, t, to_hbm=True):
            cp.start()

        @pl.when(t > 0)
        def _():
            for cp in (_tile_copies(dk_hbm, dk_b, sem_dk.at[before], hp, t - 1, to_hbm=True, lane0=ATTN_W)
                       + _tile_copies(dv_hbm, dv_b, sem_dv.at[before], hp, t - 1, to_hbm=True, lane0=2 * ATTN_W)):
                cp.start()

        @pl.when(t == nt - 1)
        def _():
            for cp in (_tile_copies(dk_hbm, dk_t, sem_dk.at[three], hp, t, to_hbm=True, lane0=ATTN_W)
                       + _tile_copies(dv_hbm, dv_t, sem_dv.at[three], hp, t, to_hbm=True, lane0=2 * ATTN_W)):
                cp.start()

        @pl.when(step == steps - 1)
        def _():
            for slot in range(2):
                _wait_tile(dqbuf.at[slot], sem_dq.at[slot])
            for slot in range(3):
                _wait_tile(dkbuf.at[slot], sem_dk.at[slot])
                _wait_tile(dvbuf.at[slot], sem_dv.at[slot])

        if ns:
            pl.when(step == steps - 1)(finish)

    tile = lambda n: pltpu.VMEM((n, ATT_BLOCKS, CHUNK, LANES), F32)
    dma = lambda n: pltpu.SemaphoreType.DMA((n,))
    view = jax.ShapeDtypeStruct((T // ATT_BLOCKS, ATT_BLOCKS, ATTN_W), F32)
    outs = pl.pallas_call(
        body, name="attn_bwd", grid=(ATTN_W // LANES, nt),
        in_specs=[pl.BlockSpec((8, LANES), lambda c, t: (0, c))] + [_HBM] * (7 + ns),
        out_specs=[_HBM] * (1 + ns),
        out_shape=[jax.ShapeDtypeStruct((T // ATT_BLOCKS, ATT_BLOCKS, IN_W), F32)]
        + [jax.ShapeDtypeStruct(p.shape, p.dtype) for p in owner_grads],
        scratch_shapes=[tile(2), tile(2), tile(2), tile(2), tile(3), tile(3), tile(2), tile(4), tile(4),
                        pltpu.VMEM((ATT_BLOCKS, CHUNK, LANES), F32)]
        + [dma(2), dma(2), dma(2), dma(2), dma(3), dma(3), dma(2), dma(3), dma(3)]
        + (_owner_exchange_sems(ns) if ns else []),
        input_output_aliases={7: 0},
        compiler_params=_params(("arbitrary", "arbitrary")),
    )(_slope_table(), *[_residue_view(a) for a in (q, k, v, dattn, attn, lse, dproj)], *owner_grads)
    return outs[0].reshape(T, IN_W), tuple(outs[1:])


def _proj_bwd(dproj, w_in_t, x, g1, dh1, owner_grads=()):
    T = x.shape[0]
    tm = TM_PROJ
    ns = len(owner_grads)
    steps = T // tm

    def body(d_ref, w_ref, x_ref, g_ref, r_ref, *rest):
        p_refs, rest = rest[:ns], rest[ns:]
        dx_ref, dg_ref = rest[:2]
        r_refs, sems = rest[2:2 + ns], rest[2 + ns:]
        step = pl.program_id(0)
        if ns:
            start, finish = _owner_exchange_phases(p_refs, r_refs, *sems)
            pl.when(step == 0)(start)

        @pl.when(step == 0)
        def _():
            dg_ref[...] = jnp.zeros_like(dg_ref)

        dhn = _dot(d_ref[...].astype(BF16), w_ref[...])
        n1, r1 = _rms(x_ref[...])
        _accum_rows(dg_ref, dhn * n1)
        dx_ref[...] = r_ref[...] + _rms_bwd(n1, r1, g_ref[...], dhn)
        if ns:
            pl.when(step == steps - 1)(finish)

    outs = pl.pallas_call(
        body, name="proj_bwd", grid=(steps,),
        in_specs=[_rows(tm, IN_W), _resident((IN_W, D_MODEL)), _rows(tm, D_MODEL), _resident((1, D_MODEL)),
                  _rows(tm, D_MODEL)] + [_HBM] * ns,
        out_specs=[_rows(tm, D_MODEL), pl.BlockSpec((8, D_MODEL), lambda i: (0, 0))] + [_HBM] * ns,
        out_shape=[jax.ShapeDtypeStruct((T, D_MODEL), F32), jax.ShapeDtypeStruct((8, D_MODEL), F32)]
        + [jax.ShapeDtypeStruct(p.shape, p.dtype) for p in owner_grads],
        scratch_shapes=_owner_exchange_sems(ns) if ns else [],
        compiler_params=_params(("arbitrary",)),
    )(dproj, w_in_t, x, g1, dh1, *owner_grads)
    return outs[0], outs[1], tuple(outs[2:])


def _dw(a, b, name, tile, square_a=False, out_dtype=F32):
    T, ka = a.shape
    nb = b.shape[1]
    tka, tnb, tt = tile
    tt = min(tt, T)
    last = T // tt - 1

    def body(a_ref, b_ref, *refs):
        o_ref = refs[0]
        acc_ref = refs[1] if len(refs) > 1 else o_ref
        s = pl.program_id(2)

        @pl.when(s == 0)
        def _():
            acc_ref[...] = jnp.zeros_like(acc_ref)

        a_tile = a_ref[...]
        if square_a:
            a_tile = jnp.square(a_tile.astype(F32))
        acc_ref[...] += _dot(a_tile.astype(BF16), b_ref[...], TN)
        if acc_ref is not o_ref:
            @pl.when(s == last)
            def _():
                o_ref[...] = acc_ref[...].astype(out_dtype)

    return pl.pallas_call(
        body, name=name, grid=(ka // tka, nb // tnb, T // tt),
        in_specs=[pl.BlockSpec((tt, tka), lambda i, j, s: (s, i)), pl.BlockSpec((tt, tnb), lambda i, j, s: (s, j))],
        out_specs=pl.BlockSpec((tka, tnb), lambda i, j, s: (i, j)),
        out_shape=jax.ShapeDtypeStruct((ka, nb), out_dtype),
        scratch_shapes=[] if out_dtype == F32 else [pltpu.VMEM((tka, tnb), F32)],
        compiler_params=_params(("parallel", "parallel", "arbitrary")),
    )(a, b)


def _adamw_update(w, m, v, g):
    m2 = ADAM_B1 * m + (1.0 - ADAM_B1) * g
    v2 = ADAM_B2 * v + (1.0 - ADAM_B2) * jnp.square(g)
    m_hat = m2 / (1.0 - ADAM_B1 ** ADAM_STEP)
    v_hat = v2 / (1.0 - ADAM_B2 ** ADAM_STEP)
    return -ADAM_LR * (m_hat / (jnp.sqrt(v_hat) + ADAM_EPS) + ADAM_WD * w), m2, v2


def _adamw_tiny(ws, ms, vs, parts):
    n = len(ws)
    P = parts.shape[0]

    def body(*refs):
        w_refs, m_refs, v_refs, p_ref = refs[:n], refs[n:2 * n], refs[2 * n:3 * n], refs[3 * n]
        outs = refs[3 * n + 1:]

        def total(slot, rows):
            g = p_ref[0, 8 * slot:8 * slot + rows, :]
            for i in range(1, P):
                g = g + p_ref[i, 8 * slot:8 * slot + rows, :]
            return g

        for k in range(n):
            g = total(k, ws[k].shape[0])
            outs[4 * k][...] = g
            outs[4 * k + 1][...], outs[4 * k + 2][...], outs[4 * k + 3][...] = _adamw_update(
                w_refs[k][...], m_refs[k][...], v_refs[k][...], g)
        outs[4 * n][...] = total(n, 8)

    sds = jax.ShapeDtypeStruct
    return pl.pallas_call(
        body, name="adamw_tiny",
        out_shape=[sds(w.shape, F32) for w in ws for _ in range(4)] + [sds((8, LANES), F32)],
    )(*ws, *ms, *vs, parts)


def _adamw(w, m, v, parts, name, tr, transposed=False):
    R, C = w.shape
    P = parts.shape[0]

    def body(w_ref, m_ref, v_ref, p_ref, g_ref, d_ref, m2_ref, v2_ref):
        g = p_ref[0].astype(F32)
        for i in range(1, P):
            g = g + p_ref[i].astype(F32)
        if transposed:
            g = g.T
        g_ref[...] = g
        d_ref[...], m2_ref[...], v2_ref[...] = _adamw_update(w_ref[...], m_ref[...], v_ref[...], g)

    spec = _rows(tr, C)
    part_spec = (pl.BlockSpec((P, C, tr), lambda i: (0, 0, i)) if transposed
                 else pl.BlockSpec((P, tr, C), lambda i: (0, i, 0)))
    return pl.pallas_call(
        body, name=name, grid=(R // tr,),
        in_specs=[spec, spec, spec, part_spec],
        out_specs=[spec] * 4,
        out_shape=[jax.ShapeDtypeStruct((R, C), F32)] * 4,
        compiler_params=_params(("parallel",)),
    )(w, m, v, parts)


_HBM = pl.BlockSpec(memory_space=pltpu.HBM)


def _place():
    return lax.axis_index("x"), lax.axis_index("y"), lax.axis_index("c")


def _gathered_shape(shard):
    return jax.ShapeDtypeStruct((N_DEV,) + shard.shape, shard.dtype)


def _gather_sems(n):
    return [pltpu.SemaphoreType.DMA((7, n)), pltpu.SemaphoreType.DMA((7, n)), pltpu.SemaphoreType.DMA((n,))]


def _gather_phases(x_refs, out_refs, send_sems, recv_sems, local_sems):
    x, y, c = _place()
    me, sibling = (x, y, c), (x, y, 1 - c)
    chips = [(1 - x, y), (x, 1 - y), (1 - x, 1 - y)]
    arrays = range(len(x_refs))

    def slot(i, px, py, pc):
        return out_refs[i].at[4 * px + 2 * py + pc]

    def copy(i, k, block, to, own=False):
        return pltpu.make_async_remote_copy(
            src_ref=x_refs[i] if own else slot(i, *block), dst_ref=slot(i, *block),
            send_sem=send_sems.at[k, i], recv_sem=recv_sems.at[k, i], device_id=to, device_id_type=MESH)

    def mine(i):
        return pltpu.make_async_copy(x_refs[i], slot(i, *me), local_sems.at[i])

    def start():
        for i in arrays:
            mine(i).start()
            copy(i, 0, me, sibling, own=True).start()
            for j, chip in enumerate(chips):
                copy(i, 1 + j, me, (*chip, c), own=True).start()

    def forward():
        for i in arrays:
            for j, chip in enumerate(chips):
                copy(i, 1 + j, (*chip, c), me).wait_recv()
                copy(i, 4 + j, (*chip, c), sibling).start()

    def finish():
        for i in arrays:
            copy(i, 0, sibling, me).wait_recv()
            copy(i, 0, me, sibling, own=True).wait_send()
            for j, chip in enumerate(chips):
                copy(i, 4 + j, (*chip, 1 - c), me).wait_recv()
                copy(i, 1 + j, me, (*chip, c), own=True).wait_send()
                copy(i, 4 + j, (*chip, c), sibling).wait_send()
            mine(i).wait()

    return start, forward, finish


def _all_gather(shards, name):
    n = len(shards)

    def body(*refs):
        start, forward, finish = _gather_phases(refs[:n], refs[n:2 * n], *refs[2 * n:])
        start()
        forward()
        finish()

    return pl.pallas_call(
        body, name=name,
        out_shape=[_gathered_shape(s) for s in shards],
        in_specs=[_HBM] * n, out_specs=[_HBM] * n,
        scratch_shapes=_gather_sems(n),
    )(*shards)


def _owner_exchange_sems(n):
    return [pltpu.SemaphoreType.DMA((7, n)), pltpu.SemaphoreType.DMA((7, n)), pltpu.SemaphoreType.DMA((n,))]


def _owner_exchange_phases(g_refs, r_refs, send_sems, recv_sems, local_sems):
    x, y, c = _place()
    me = 4 * x + 2 * y + c
    flip = lambda v, bit: 1 - v if bit else v
    peers = [(flip(x, k & 4), flip(y, k & 2), flip(c, k & 1)) for k in range(1, N_DEV)]
    arrays = range(len(g_refs))

    def mine(i):
        return pltpu.make_async_copy(g_refs[i].at[me], r_refs[i].at[me], local_sems.at[i])

    def copy(i, k, src_slot, dst_slot):
        return pltpu.make_async_remote_copy(
            src_ref=g_refs[i].at[src_slot], dst_ref=r_refs[i].at[dst_slot],
            send_sem=send_sems.at[k, i], recv_sem=recv_sems.at[k, i], device_id=peers[k], device_id_type=MESH)

    def start():
        for i in arrays:
            mine(i).start()
            for k, (px, py, pc) in enumerate(peers):
                copy(i, k, 4 * px + 2 * py + pc, me).start()

    def finish():
        for i in arrays:
            for k, (px, py, pc) in enumerate(peers):
                copy(i, k, me, 4 * px + 2 * py + pc).wait_recv()
                copy(i, k, 4 * px + 2 * py + pc, me).wait_send()
            mine(i).wait()

    return start, finish


def _local_step(x, tgt, small, w_in_t, rest, exchange=False):
    g1, g2, gf = small["norm1_g"], small["norm2_g"], small["final_norm_g"].reshape(1, D_MODEL)
    ga, gg = small["attn_out_g"], small["gmlp_out_g"]
    ln_g = small["sgu_ln_g"].reshape(1, GMLP_W)
    ln_b = small["sgu_ln_b"].reshape(1, GMLP_W)
    sgu_w = small["sgu_w"][0]
    sgu_bt = small["sgu_b"][0].T

    hn1, q, k, v, u, z = _proj_fwd(x, g1, w_in_t)
    attn, lse, gathered = _attn_fwd(q, k, v, shards=rest if exchange else ())
    w_out, w_ff1_t, w_ff2 = [g.reshape(-1, D_MODEL) for g in gathered] if exchange else rest
    gm = _gmlp_fwd(u, z, ln_g, ln_b, sgu_w, sgu_bt)
    mixed, h1, hn2 = _out_fwd(attn, gm, ga, gg, w_out, x, g2)
    relu, dh2f, dh2b, loss8, dgf8 = _ffn_fwd(hn2, h1, w_ff1_t, w_ff2, gf, tgt)

    da, dh1f, dh1b, dg2 = _ffn_bwd(dh2b, dh2f, relu, h1, g2, w_ff2, w_ff1_t)
    wire = BF16 if exchange else F32
    dw_ff2 = _dw(relu, dh2b, "dw_ff2", DW_TILE, square_a=True, out_dtype=wire)
    dw_ff1_t = _dw(da, hn2, "dw_ff1", DW_TILE, out_dtype=wire)
    dattn, dgm, dga, dgg = _out_bwd(dh1b, w_out, attn, gm, ga, gg)
    dw_out = _dw(mixed, dh1b, "dw_out", DW_TILE, out_dtype=wire)
    early = [dw_out, dw_ff1_t, dw_ff2]
    if exchange:
        early = [g.reshape(N_DEV, -1, D_MODEL) for g in early]
    dproj, dlg, dlb, dsw, dsb = _gmlp_bwd(u, z, dgm, ln_g, ln_b, sgu_w, sgu_bt)
    dproj, arrived = _attn_bwd(q, k, v, dattn, attn, lse, dproj, owner_grads=early if exchange else ())
    dw_in_t = _dw(dproj, hn1, "dw_in", DW_TILE_IN, out_dtype=wire)
    late = (dw_in_t.reshape(N_DEV, -1, D_MODEL),) if exchange else ()
    dx, dg1, late = _proj_bwd(dproj, w_in_t, x, g1, dh1f, owner_grads=late)
    if exchange:
        dw_in_t, early = late[0], arrived

    small_grads = dict(
        norm1_g=dg1[0], sgu_ln_g=dlg[0], sgu_ln_b=dlb[0], sgu_w=dsw, sgu_b=dsb[:, :N_GROUPS].T,
        attn_out_g=dga[0], gmlp_out_g=dgg[0], norm2_g=dg2[0], final_norm_g=dgf8[0])
    return loss8[0, 0], dx, (dw_in_t, *early), small_grads


SMALL_NAMES = ("norm1_g", "sgu_ln_g", "sgu_ln_b", "sgu_w", "sgu_b", "attn_out_g", "gmlp_out_g", "norm2_g",
               "final_norm_g")
WEIGHT_ORDER = ("norm1_g", "w_in", "sgu_ln_g", "sgu_ln_b", "sgu_w", "sgu_b", "attn_out_g", "gmlp_out_g", "w_out",
                "norm2_g", "w_ff1", "w_ff2", "final_norm_g")


TINY_NAMES = tuple(n for n in SMALL_NAMES if n != "sgu_w")


def _as_rows(a):
    return a.reshape(-1, LANES)


def _pack_tiny_grads(d, loss):
    slots = [jnp.pad(_as_rows(d[n]), ((0, 8 - d[n].size // LANES), (0, 0))) for n in TINY_NAMES]
    return jnp.concatenate(slots + [jnp.full((8, LANES), loss, F32)], axis=0)


def kernel(x, norm1_g, w_in, sgu_ln_g, sgu_ln_b, sgu_w, sgu_b, attn_out_g, gmlp_out_g, w_out, norm2_g, w_ff1, w_ff2, final_norm_g, loss_target, m_norm1_g, m_w_in, m_sgu_ln_g, m_sgu_ln_b, m_sgu_w, m_sgu_b, m_attn_out_g, m_gmlp_out_g, m_w_out, m_norm2_g, m_w_ff1, m_w_ff2, m_final_norm_g, v_norm1_g, v_w_in, v_sgu_ln_g, v_sgu_ln_b, v_sgu_w, v_sgu_b, v_attn_out_g, v_gmlp_out_g, v_w_out, v_norm2_g, v_w_ff1, v_w_ff2, v_final_norm_g):
    w = dict(norm1_g=norm1_g, w_in=w_in, sgu_ln_g=sgu_ln_g, sgu_ln_b=sgu_ln_b, sgu_w=sgu_w, sgu_b=sgu_b,
             attn_out_g=attn_out_g, gmlp_out_g=gmlp_out_g, w_out=w_out, norm2_g=norm2_g, w_ff1=w_ff1, w_ff2=w_ff2,
             final_norm_g=final_norm_g)
    m = dict(norm1_g=m_norm1_g, w_in=m_w_in, sgu_ln_g=m_sgu_ln_g, sgu_ln_b=m_sgu_ln_b, sgu_w=m_sgu_w, sgu_b=m_sgu_b,
             attn_out_g=m_attn_out_g, gmlp_out_g=m_gmlp_out_g, w_out=m_w_out, norm2_g=m_norm2_g, w_ff1=m_w_ff1,
             w_ff2=m_w_ff2, final_norm_g=m_final_norm_g)
    v = dict(norm1_g=v_norm1_g, w_in=v_w_in, sgu_ln_g=v_sgu_ln_g, sgu_ln_b=v_sgu_ln_b, sgu_w=v_sgu_w, sgu_b=v_sgu_b,
             attn_out_g=v_attn_out_g, gmlp_out_g=v_gmlp_out_g, w_out=v_w_out, norm2_g=v_norm2_g, w_ff1=v_w_ff1,
             w_ff2=v_w_ff2, final_norm_g=v_final_norm_g)
    big = ("w_in", "w_out", "w_ff1", "w_ff2")

    w_in_t, = _all_gather([w_in[0].T.astype(BF16)], "w_in_all_gather")
    rest = (w_out[0].astype(BF16), w_ff1[0].T.astype(BF16), w_ff2[0].astype(BF16))
    loss, dx, parts, small_grads = _local_step(x[0], loss_target[0], {n: w[n] for n in SMALL_NAMES},
                                               w_in_t.reshape(IN_W, D_MODEL), rest, exchange=True)

    new = {}
    for n, p, transposed, tr in zip(big, parts, (True, False, True, False), (128, 128, 128, 256)):
        new[n] = [a[None] for a in _adamw(w[n][0], m[n][0], v[n][0], p, "adamw_" + n, tr, transposed)]

    tiny_parts, sgu_parts = _all_gather(
        [_pack_tiny_grads(small_grads, loss), _as_rows(small_grads["sgu_w"]).astype(BF16)], "small_grad_all_gather")
    tiny = _adamw_tiny(*[[_as_rows(src[n]) for n in TINY_NAMES] for src in (w, m, v)], tiny_parts)
    sgu = _adamw(_as_rows(sgu_w), _as_rows(m_sgu_w), _as_rows(v_sgu_w), sgu_parts, "adamw_sgu_w", 512)
    loss = tiny[-1][0, 0]

    outs = []
    for i in range(4):
        d = {n: new[n][i] for n in big}
        d.update({n: tiny[4 * k + i].reshape(w[n].shape) for k, n in enumerate(TINY_NAMES)})
        d["sgu_w"] = sgu[i].reshape(sgu_w.shape)
        outs.extend(d[n] for n in WEIGHT_ORDER)
    return (loss, dx[None], *outs)
```

```python
import math

import numpy as np
import jax
import jax.numpy as jnp
from jax import lax
from jax.experimental import pallas as pl
from jax.experimental.pallas import tpu as pltpu

F32 = jnp.float32
BF16 = jnp.bfloat16

D_MODEL = 1024
HEAD_DIM = 64
N_HEADS = 12
ATTN_W = N_HEADS * HEAD_DIM
N_GROUPS = 4
GMLP_W = N_GROUPS * HEAD_DIM
IN_W = 3 * ATTN_W + 2 * GMLP_W
D_FF = 4 * D_MODEL
CHUNK = 128
DILATIONS = (1, 4, 16)
EPS = 1e-6
Q_SCALE = HEAD_DIM ** -0.5
NEG = -1e30

ADAM_LR, ADAM_B1, ADAM_B2, ADAM_EPS, ADAM_WD, ADAM_STEP = 0.001, 0.9, 0.999, 1e-08, 0.01, 10

N_DEV = 8
LANES = 128
VMEM_LIMIT = 56 << 20

TM_PROJ = 512
TM_FFN = 512
FF_CHUNK = 512
TM_GMLP = 1024
DW_TILE = (512, 1024, 8192)
DW_TILE_IN = (IN_W // 2, 1024, 2048)

MESH = pl.DeviceIdType.MESH


def _alibi_slopes(n):
    def pow2(m):
        start = 2.0 ** (-8.0 / m)
        return [start ** (i + 1) for i in range(m)]
    c = 2 ** int(math.floor(math.log2(n)))
    s = pow2(n) if c == n else pow2(c) + pow2(2 * c)[0::2][: n - c]
    return np.asarray(s, dtype=np.float32)


SLOPES = _alibi_slopes(N_HEADS)


def _params(sem=None):
    kw = dict(vmem_limit_bytes=VMEM_LIMIT)
    if sem is not None:
        kw["dimension_semantics"] = sem
    return pltpu.CompilerParams(**kw)


def _rows(tm, n):
    return pl.BlockSpec((tm, n), lambda i: (i, 0))


def _resident(shape):
    return pl.BlockSpec(shape, lambda *_: (0,) * len(shape), pipeline_mode=pl.Buffered(1))


def _rms(x):
    r = lax.rsqrt(jnp.mean(x * x, axis=-1, keepdims=True) + EPS)
    return x * r, r


def _rms_bwd(n, r, g, dy):
    dn = dy * g
    return r * (dn - n * jnp.mean(dn * n, axis=-1, keepdims=True))


def _accum_rows(acc_ref, v):
    acc_ref[...] += jnp.broadcast_to(jnp.sum(v, axis=0, keepdims=True), acc_ref.shape)


_G0 = math.sqrt(2.0 / math.pi)
_G1 = 0.044715


def _gelu(x):
    t = jnp.tanh(_G0 * (x + _G1 * (x * x * x)))
    return x * (0.5 * (1.0 + t)), t


def _gelu_grad(x, t):
    return 0.5 * (1.0 + t) + 0.5 * x * (1.0 - t * t) * (_G0 * (1.0 + 3.0 * _G1 * x * x))


NT = (((1,), (1,)), ((), ()))
TN = (((0,), (0,)), ((), ()))


def _dot(a, b, dims=None):
    if dims is None:
        return jnp.dot(a, b, preferred_element_type=F32)
    return lax.dot_general(a, b, dims, preferred_element_type=F32)


def _proj_fwd(x, g1, w_in_t):
    T = x.shape[0]
    tm = TM_PROJ

    def body(x_ref, g_ref, w_ref, hn_ref, q_ref, k_ref, v_ref, u_ref, z_ref):
        n, _ = _rms(x_ref[...])
        hn = (n * g_ref[...]).astype(BF16)
        hn_ref[...] = hn
        a = ATTN_W
        q_ref[...] = _dot(hn, w_ref[0:a, :], NT) * Q_SCALE
        k_ref[...] = _dot(hn, w_ref[a:2 * a, :], NT)
        v_ref[...] = _dot(hn, w_ref[2 * a:3 * a, :], NT)
        u_ref[...] = _dot(hn, w_ref[3 * a:3 * a + GMLP_W, :], NT)
        z_ref[...] = _dot(hn, w_ref[3 * a + GMLP_W:, :], NT)

    sds = jax.ShapeDtypeStruct
    return pl.pallas_call(
        body, name="proj_fwd", grid=(T // tm,),
        in_specs=[_rows(tm, D_MODEL), _resident((1, D_MODEL)), _resident((IN_W, D_MODEL))],
        out_specs=[_rows(tm, D_MODEL), _rows(tm, ATTN_W), _rows(tm, ATTN_W), _rows(tm, ATTN_W),
                   _rows(tm, GMLP_W), _rows(tm, GMLP_W)],
        out_shape=[sds((T, D_MODEL), BF16), sds((T, ATTN_W), F32), sds((T, ATTN_W), F32),
                   sds((T, ATTN_W), F32), sds((T, GMLP_W), F32), sds((T, GMLP_W), F32)],
        compiler_params=_params(("parallel",)),
    )(x, g1, w_in_t)


ATT_TILE = 2048
ATT_BLOCKS = ATT_TILE // CHUNK
SM_BLOCKS = 4


def _slope_table():
    row = np.repeat(SLOPES, HEAD_DIM)
    return jnp.asarray(np.broadcast_to(row[None], (8, ATTN_W)), F32)


def _residue_view(a):
    return a.reshape(a.shape[0] // ATT_BLOCKS, ATT_BLOCKS, a.shape[1])


def _tile_copies(hbm, buf, sem, hp, t, to_hbm=False, lane0=0):
    rows = pl.ds(pl.multiple_of(t * CHUNK, CHUNK), CHUNK)
    lanes = pl.ds(pl.multiple_of(lane0 + hp * LANES, LANES), LANES)
    pairs = [(hbm.at[rows, r, lanes], buf.at[r]) for r in range(ATT_BLOCKS)]
    return [pltpu.make_async_copy(v, h, sem) if to_hbm else pltpu.make_async_copy(h, v, sem) for h, v in pairs]


def _wait_tile(buf, sem):
    pltpu.make_async_copy(buf, buf, sem).wait()


def _residue_rows(d, j):
    if d == 16:
        return [(j, 0, CHUNK)]
    if d == 4:
        return [(j % 4 + 4 * m, 32 * (j // 4), 32) for m in range(4)]
    return [(r, 8 * j, 8) for r in range(ATT_BLOCKS)]


def _block_order(p, d):
    if d == 16:
        return p
    if d == 4:
        return 4 * (p & 31) + (p >> 5)
    return 16 * (p & 7) + (p >> 3)


def _first_in_tile(d, j):
    return _residue_rows(d, j)[0][1] == 0


def _rm_block(buf, d, j):
    return jnp.concatenate([buf[r, lo:lo + n, :] for r, lo, n in _residue_rows(d, j)], axis=0)


def _rm_block_before(buf, buf_before, d, j):
    if _first_in_tile(d, j):
        return jnp.concatenate([buf_before[r, CHUNK - n:CHUNK, :] for r, _, n in _residue_rows(d, j)], axis=0)
    return jnp.concatenate([buf[r, lo - n:lo, :] for r, lo, n in _residue_rows(d, j)], axis=0)


def _rm_store(buf, d, j, val):
    at = 0
    for r, lo, n in _residue_rows(d, j):
        buf[r, lo:lo + n, :] = val[at:at + n, :]
        at += n


def _rm_add(buf, rows, val, first=False):
    at = 0
    for r, lo, n in rows:
        if first:
            buf[r, lo:lo + n, :] = val[at:at + n, :]
        else:
            buf[r, lo:lo + n, :] += val[at:at + n, :]
        at += n


def _residue_bias(sl_ref, d):
    shape = (2 * CHUNK, 2 * CHUNK)
    row = lax.broadcasted_iota(jnp.int32, shape, 0)
    col = lax.broadcasted_iota(jnp.int32, shape, 1)
    steps = _block_order(row & (CHUNK - 1), d) + CHUNK - (_block_order(col & (CHUNK - 1), d) + (col & CHUNK))
    band = (steps >= 0) & (steps <= CHUNK)
    sl = sl_ref[0:1, :]
    upper = lax.broadcasted_iota(jnp.int32, (2 * CHUNK, 1), 0) < CHUNK
    slope2 = jnp.where(upper, sl[:, 0:1], sl[:, HEAD_DIM:HEAD_DIM + 1])
    return jnp.where(band, -(float(d) * slope2 * steps.astype(F32)), NEG)


def _stack_heads(xb, head0):
    zero = jnp.zeros_like(xb)
    return jnp.concatenate([jnp.where(head0, xb, zero), jnp.where(head0, zero, xb)], axis=0).astype(BF16)


def _unstack_heads(x2, head0):
    return jnp.where(head0, x2[:CHUNK, :], x2[CHUNK:, :])


def _attn_fwd(q, k, v, shards=()):
    T = q.shape[0]
    nt = T // ATT_TILE
    ns = len(shards)
    steps = (ATTN_W // LANES) * nt

    def body(sl_ref, q_hbm, k_hbm, v_hbm, *rest):
        x_refs, rest = rest[:ns], rest[ns:]
        attn_hbm, lse_hbm = rest[:2]
        g_refs, rest = rest[2:2 + ns], rest[2 + ns:]
        qbuf, kbuf, vbuf, obuf, lbuf = rest[:5]
        o_acc, l_acc = rest[5:8], rest[8:11]
        sem_q, sem_k, sem_v, sem_o, sem_l = rest[11:16]
        hp, t = pl.program_id(0), pl.program_id(1)
        step = hp * nt + t
        two, three = step % 2, step % 3
        before, after = (step + 2) % 3, (step + 1) % 3
        if ns:
            start, forward, finish = _gather_phases(x_refs, g_refs, *rest[16:])
            pl.when(step == 0)(start)
            pl.when(step == steps // 2)(forward)

        def fetch(hp_, t_, two_, three_):
            for cp in (_tile_copies(q_hbm, qbuf.at[two_], sem_q.at[two_], hp_, t_)
                       + _tile_copies(k_hbm, kbuf.at[three_], sem_k.at[three_], hp_, t_)
                       + _tile_copies(v_hbm, vbuf.at[three_], sem_v.at[three_], hp_, t_)):
                cp.start()

        @pl.when(step == 0)
        def _():
            kbuf[2] = jnp.zeros((ATT_BLOCKS, CHUNK, LANES), F32)
            vbuf[2] = jnp.zeros((ATT_BLOCKS, CHUNK, LANES), F32)
            fetch(0, 0, 0, 0)

        @pl.when(step + 1 < steps)
        def _():
            fetch((step + 1) // nt, (step + 1) % nt, 1 - two, after)

        _wait_tile(qbuf.at[two], sem_q.at[two])
        _wait_tile(kbuf.at[three], sem_k.at[three])
        _wait_tile(vbuf.at[three], sem_v.at[three])

        @pl.when(step >= 2)
        def _():
            _wait_tile(obuf.at[two], sem_o.at[two])
            _wait_tile(lbuf.at[two], sem_l.at[two])

        q_t, k_t, v_t = qbuf.at[two], kbuf.at[three], vbuf.at[three]
        k_b, v_b = kbuf.at[before], vbuf.at[before]
        head0 = lax.broadcasted_iota(jnp.int32, (CHUNK, LANES), 1) < HEAD_DIM
        no_key_before = jnp.where(lax.broadcasted_iota(jnp.int32, (2 * CHUNK, 2 * CHUNK), 1) < CHUNK, NEG, 0.0)
        for pi, d in enumerate(DILATIONS):
            bias = _residue_bias(sl_ref, d)

            def scores(j, d=d, bias=bias):
                kcat = jnp.concatenate([_rm_block_before(k_t, k_b, d, j), _rm_block(k_t, d, j)], axis=0).astype(BF16)
                vcat = jnp.concatenate([_rm_block_before(v_t, v_b, d, j), _rm_block(v_t, d, j)], axis=0).astype(BF16)
                s = _dot(_stack_heads(_rm_block(q_t, d, j), head0), kcat, NT)
                return s, vcat, bias_first if _first_in_tile(d, j) else bias

            bias_first = bias + jnp.where(t == 0, 1.0, 0.0) * no_key_before
            for j0 in range(0, ATT_BLOCKS, SM_BLOCKS):
                group = [scores(j) for j in range(j0, j0 + SM_BLOCKS)]
                s = jnp.concatenate([g[0] for g in group], axis=0) + jnp.concatenate([g[2] for g in group], axis=0)
                m = jnp.max(s, axis=-1, keepdims=True)
                p = jnp.exp(s - m)
                l = jnp.sum(p, axis=-1, keepdims=True)
                p = p.astype(BF16)
                block = lambda a, i: a[i * 2 * CHUNK:(i + 1) * 2 * CHUNK, :]
                o = jnp.concatenate([_dot(block(p, i), g[1]) for i, g in enumerate(group)], axis=0) * (1.0 / l)
                lse = jnp.broadcast_to(m + jnp.log(l), o.shape)
                for i in range(SM_BLOCKS):
                    _rm_store(o_acc[pi], d, j0 + i, _unstack_heads(block(o, i), head0))
                    _rm_store(l_acc[pi], d, j0 + i, _unstack_heads(block(lse, i), head0))

        for r in range(ATT_BLOCKS):
            a, b, c = l_acc[0][r], l_acc[1][r], l_acc[2][r]
            m = jnp.maximum(jnp.maximum(a, b), c)
            ea, eb, ec = jnp.exp(a - m), jnp.exp(b - m), jnp.exp(c - m)
            tot = ea + eb + ec
            obuf[two, r] = (ea * o_acc[0][r] + eb * o_acc[1][r] + ec * o_acc[2][r]) / tot
            lbuf[two, r] = m + jnp.log(tot)

        for cp in (_tile_copies(attn_hbm, obuf.at[two], sem_o.at[two], hp, t, to_hbm=True)
                   + _tile_copies(lse_hbm, lbuf.at[two], sem_l.at[two], hp, t, to_hbm=True)):
            cp.start()

        @pl.when(step == steps - 1)
        def _():
            for slot in (two, 1 - two)[:min(steps, 2)]:
                _wait_tile(obuf.at[slot], sem_o.at[slot])
                _wait_tile(lbuf.at[slot], sem_l.at[slot])

        if ns:
            pl.when(step == steps - 1)(finish)

    tile = lambda n: pltpu.VMEM((n, ATT_BLOCKS, CHUNK, LANES), F32)
    dma = lambda n: pltpu.SemaphoreType.DMA((n,))
    view = jax.ShapeDtypeStruct((T // ATT_BLOCKS, ATT_BLOCKS, ATTN_W), F32)
    outs = pl.pallas_call(
        body, name="attn_fwd", grid=(ATTN_W // LANES, nt),
        in_specs=[pl.BlockSpec((8, LANES), lambda c, t: (0, c))] + [_HBM] * (3 + ns),
        out_specs=[_HBM] * (2 + ns),
        out_shape=[view, view] + [_gathered_shape(s) for s in shards],
        scratch_shapes=[tile(2), tile(3), tile(3), tile(2), tile(2)] + [pltpu.VMEM((ATT_BLOCKS, CHUNK, LANES), F32)] * 6
        + [dma(2), dma(3), dma(3), dma(2), dma(2)] + (_gather_sems(ns) if ns else []),
        compiler_params=_params(("arbitrary", "arbitrary")),
    )(_slope_table(), _residue_view(q), _residue_view(k), _residue_view(v), *shards)
    return outs[0].reshape(T, ATTN_W), outs[1].reshape(T, ATTN_W), tuple(outs[2:])


def _group_mean(v, grp):
    out = jnp.zeros_like(v)
    for g in range(N_GROUPS):
        mk = grp == g
        s = jnp.sum(jnp.where(mk, v, 0.0), axis=-1, keepdims=True) * (1.0 / HEAD_DIM)
        out = jnp.where(mk, s, out)
    return out


def _gmlp_core(uu, zz, lg, lb, ws, sb_ref, grp):
    ug, tu = _gelu(uu)
    zg, tz = _gelu(zz)
    zc = zg - _group_mean(zg, grp)
    rstd = lax.rsqrt(_group_mean(zc * zc, grp) + EPS)
    xhat = zc * rstd
    zn16 = (xhat * lg + lb).astype(BF16)
    mixed = []
    for ci in range(uu.shape[0] // CHUNK):
        rows = slice(ci * CHUNK, (ci + 1) * CHUNK)
        m = jnp.zeros((CHUNK, GMLP_W), F32)
        for g in range(N_GROUPS):
            m = jnp.where(grp[:CHUNK] == g, _dot(ws[g], zn16[rows, :]) + sb_ref[:, g:g + 1], m)
        mixed.append(m)
    return ug, tu, tz, xhat, rstd, zn16, jnp.concatenate(mixed, axis=0)


def _causal_ws(w_ref):
    ti = lax.broadcasted_iota(jnp.int32, (CHUNK, CHUNK), 0)
    si = lax.broadcasted_iota(jnp.int32, (CHUNK, CHUNK), 1)
    causal = si <= ti
    return causal, [jnp.where(causal, w_ref[g], 0.0).astype(BF16) for g in range(N_GROUPS)]


def _gmlp_fwd(u, z, ln_g, ln_b, sgu_w, sgu_bt):
    T = u.shape[0]
    tg = TM_GMLP

    def body(u_ref, z_ref, g_ref, b_ref, w_ref, sb_ref, out_ref):
        grp = lax.broadcasted_iota(jnp.int32, (tg, GMLP_W), 1) // HEAD_DIM
        _, ws = _causal_ws(w_ref)
        ug, _, _, _, _, _, mixed = _gmlp_core(u_ref[...], z_ref[...], g_ref[...], b_ref[...], ws, sb_ref, grp)
        out_ref[...] = ug * mixed

    return pl.pallas_call(
        body, name="gmlp_fwd", grid=(T // tg,),
        in_specs=[_rows(tg, GMLP_W), _rows(tg, GMLP_W), _resident((1, GMLP_W)), _resident((1, GMLP_W)),
                  _resident((N_GROUPS, CHUNK, CHUNK)), _resident((CHUNK, N_GROUPS))],
        out_specs=_rows(tg, GMLP_W),
        out_shape=jax.ShapeDtypeStruct((T, GMLP_W), F32),
        compiler_params=_params(("parallel",)),
    )(u, z, ln_g, ln_b, sgu_w, sgu_bt)


def _out_fwd(attn, gm, ga, gg, w_out, x, g2):
    T = x.shape[0]
    tm = TM_PROJ

    def body(a_ref, m_ref, ga_ref, gg_ref, w_ref, x_ref, g2_ref, mix_ref, h1_ref, hn2_ref):
        an, _ = _rms(a_ref[...])
        gn, _ = _rms(m_ref[...])
        an = (an * ga_ref[...]).astype(BF16)
        gn = (gn * gg_ref[...]).astype(BF16)
        mix_ref[:, 0:ATTN_W] = an
        mix_ref[:, ATTN_W:] = gn
        h1 = x_ref[...] + _dot(an, w_ref[0:ATTN_W, :]) + _dot(gn, w_ref[ATTN_W:, :])
        h1_ref[...] = h1
        n2, _ = _rms(h1)
        hn2_ref[...] = (n2 * g2_ref[...]).astype(BF16)

    sds = jax.ShapeDtypeStruct
    return pl.pallas_call(
        body, name="out_fwd", grid=(T // tm,),
        in_specs=[_rows(tm, ATTN_W), _rows(tm, GMLP_W), _resident((1, ATTN_W)), _resident((1, GMLP_W)),
                  _resident((D_MODEL, D_MODEL)), _rows(tm, D_MODEL), _resident((1, D_MODEL))],
        out_specs=[_rows(tm, D_MODEL)] * 3,
        out_shape=[sds((T, D_MODEL), BF16), sds((T, D_MODEL), F32), sds((T, D_MODEL), BF16)],
        compiler_params=_params(("parallel",)),
    )(attn, gm, ga, gg, w_out, x, g2)


def _ffn_fwd(hn2, h1, w1t, w2, gf, tgt):
    T = h1.shape[0]
    tm = TM_FFN

    def body(hn_ref, h1_ref, w1_ref, w2_ref, gf_ref, t_ref, r_ref, dhf_ref, dhb_ref, loss_ref, dgf_ref):
        i = pl.program_id(0)

        @pl.when(i == 0)
        def _():
            loss_ref[...] = jnp.zeros_like(loss_ref)
            dgf_ref[...] = jnp.zeros_like(dgf_ref)

        hn = hn_ref[...]
        acc = h1_ref[...]
        for j in range(D_FF // FF_CHUNK):
            cols = slice(j * FF_CHUNK, (j + 1) * FF_CHUNK)
            r = jnp.maximum(_dot(hn, w1_ref[cols, :], NT), 0.0)
            r_ref[:, cols] = r.astype(BF16)
            act = jnp.square(r).astype(BF16)
            acc = acc + _dot(act, w2_ref[cols, :])
        n3, r3 = _rms(acc)
        gf_row = gf_ref[...]
        e = n3 * gf_row - t_ref[...]
        loss_ref[...] += 0.5 * jnp.sum(jnp.mean(e * e, axis=-1, keepdims=True))
        dy = e * (1.0 / D_MODEL)
        _accum_rows(dgf_ref, dy * n3)
        dh2 = _rms_bwd(n3, r3, gf_row, dy)
        dhf_ref[...] = dh2
        dhb_ref[...] = dh2.astype(BF16)

    sds = jax.ShapeDtypeStruct
    acc_spec = lambda n: pl.BlockSpec((8, n), lambda i: (0, 0))
    return pl.pallas_call(
        body, name="ffn_fwd", grid=(T // tm,),
        in_specs=[_rows(tm, D_MODEL), _rows(tm, D_MODEL), _resident((D_FF, D_MODEL)), _resident((D_FF, D_MODEL)),
                  _resident((1, D_MODEL)), _rows(tm, D_MODEL)],
        out_specs=[_rows(tm, D_FF), _rows(tm, D_MODEL), _rows(tm, D_MODEL), acc_spec(LANES), acc_spec(D_MODEL)],
        out_shape=[sds((T, D_FF), BF16), sds((T, D_MODEL), F32), sds((T, D_MODEL), BF16),
                   sds((8, LANES), F32), sds((8, D_MODEL), F32)],
        compiler_params=_params(("arbitrary",)),
    )(hn2, h1, w1t, w2, gf, tgt)


def _ffn_bwd(dh2b, dh2f, relu, h1, g2, w2, w1t):
    T = h1.shape[0]
    tm = TM_FFN

    def body(db_ref, df_ref, r_ref, h1_ref, g2_ref, w2_ref, w1t_ref, da_ref, d1f_ref, d1b_ref, dg_ref):
        @pl.when(pl.program_id(0) == 0)
        def _():
            dg_ref[...] = jnp.zeros_like(dg_ref)

        db = db_ref[...]
        acc = jnp.zeros((tm, D_MODEL), F32)
        for j in range(D_FF // FF_CHUNK):
            cols = slice(j * FF_CHUNK, (j + 1) * FF_CHUNK)
            da = (_dot(db, w2_ref[cols, :], NT) * (2.0 * r_ref[:, cols].astype(F32))).astype(BF16)
            da_ref[:, cols] = da
            acc = acc + _dot(da, w1t_ref[cols, :])
        n2, r2 = _rms(h1_ref[...])
        _accum_rows(dg_ref, acc * n2)
        dh1 = df_ref[...] + _rms_bwd(n2, r2, g2_ref[...], acc)
        d1f_ref[...] = dh1
        d1b_ref[...] = dh1.astype(BF16)

    sds = jax.ShapeDtypeStruct
    return pl.pallas_call(
        body, name="ffn_bwd", grid=(T // tm,),
        in_specs=[_rows(tm, D_MODEL), _rows(tm, D_MODEL), _rows(tm, D_FF), _rows(tm, D_MODEL),
                  _resident((1, D_MODEL)), _resident((D_FF, D_MODEL)), _resident((D_FF, D_MODEL))],
        out_specs=[_rows(tm, D_FF), _rows(tm, D_MODEL), _rows(tm, D_MODEL),
                   pl.BlockSpec((8, D_MODEL), lambda i: (0, 0))],
        out_shape=[sds((T, D_FF), BF16), sds((T, D_MODEL), F32), sds((T, D_MODEL), BF16), sds((8, D_MODEL), F32)],
        compiler_params=_params(("arbitrary",)),
    )(dh2b, dh2f, relu, h1, g2, w2, w1t)


def _out_bwd(dh1b, w_out, attn, gm, ga, gg):
    T = attn.shape[0]
    tm = TM_PROJ

    def body(d_ref, w_ref, a_ref, m_ref, ga_ref, gg_ref, da_ref, dm_ref, dga_ref, dgg_ref):
        @pl.when(pl.program_id(0) == 0)
        def _():
            dga_ref[...] = jnp.zeros_like(dga_ref)
            dgg_ref[...] = jnp.zeros_like(dgg_ref)

        d = d_ref[...]
        dan = _dot(d, w_ref[0:ATTN_W, :], NT)
        dgn = _dot(d, w_ref[ATTN_W:, :], NT)
        na, ra = _rms(a_ref[...])
        ng, rg = _rms(m_ref[...])
        _accum_rows(dga_ref, dan * na)
        _accum_rows(dgg_ref, dgn * ng)
        da_ref[...] = _rms_bwd(na, ra, ga_ref[...], dan)
        dm_ref[...] = _rms_bwd(ng, rg, gg_ref[...], dgn)

    sds = jax.ShapeDtypeStruct
    return pl.pallas_call(
        body, name="out_bwd", grid=(T // tm,),
        in_specs=[_rows(tm, D_MODEL), _resident((D_MODEL, D_MODEL)), _rows(tm, ATTN_W), _rows(tm, GMLP_W),
                  _resident((1, ATTN_W)), _resident((1, GMLP_W))],
        out_specs=[_rows(tm, ATTN_W), _rows(tm, GMLP_W), pl.BlockSpec((8, ATTN_W), lambda i: (0, 0)),
                   pl.BlockSpec((8, GMLP_W), lambda i: (0, 0))],
        out_shape=[sds((T, ATTN_W), F32), sds((T, GMLP_W), F32), sds((8, ATTN_W), F32), sds((8, GMLP_W), F32)],
        compiler_params=_params(("arbitrary",)),
    )(dh1b, w_out, attn, gm, ga, gg)


def _gmlp_bwd(u, z, dgm, ln_g, ln_b, sgu_w, sgu_bt):
    T = u.shape[0]
    tg = TM_GMLP
    nsteps = T // tg

    def body(u_ref, z_ref, d_ref, g_ref, b_ref, w_ref, sb_ref, dproj_hbm, dlg_ref, dlb_ref, dw_ref, dsb_ref,
             stage, sem):
        i = pl.program_id(0)
        slot = i % 2
        duz_ref = stage.at[slot]

        def to_dproj(step, buf):
            rows = pl.ds(pl.multiple_of(step * tg, tg), tg)
            return pltpu.make_async_copy(stage.at[buf], dproj_hbm.at[rows, pl.ds(3 * ATTN_W, 2 * GMLP_W)],
                                         sem.at[buf])

        @pl.when(i == 0)
        def _():
            for ref in (dlg_ref, dlb_ref, dw_ref, dsb_ref):
                ref[...] = jnp.zeros_like(ref)

        @pl.when(i >= 2)
        def _():
            to_dproj(i - 2, slot).wait()

        grp = lax.broadcasted_iota(jnp.int32, (tg, GMLP_W), 1) // HEAD_DIM
        lane = lax.broadcasted_iota(jnp.int32, (CHUNK, LANES), 1)
        causal, ws = _causal_ws(w_ref)
        lg = g_ref[...]
        uu, zz, dgm = u_ref[...], z_ref[...], d_ref[...]
        ug, tu, tz, xhat, rstd, zn16, mixed = _gmlp_core(uu, zz, lg, b_ref[...], ws, sb_ref, grp)
        dmx = dgm * ug
        duz_ref[:, 0:GMLP_W] = dgm * mixed * _gelu_grad(uu, tu)
        dmx16 = dmx.astype(BF16)
        dzn = []
        for ci in range(tg // CHUNK):
            rows = slice(ci * CHUNK, (ci + 1) * CHUNK)
            dmx_c, d = dmx16[rows, :], jnp.zeros((CHUNK, GMLP_W), F32)
            for g in range(N_GROUPS):
                mk = grp[:CHUNK] == g
                d = jnp.where(mk, _dot(ws[g], dmx_c, TN), d)
                dw_ref[g] += _dot(jnp.where(mk, dmx_c, jnp.zeros_like(dmx_c)), zn16[rows, :], NT)
            dzn.append(d)
        dzn = jnp.concatenate(dzn, axis=0)
        dsb = jnp.zeros((CHUNK, LANES), F32)
        for g in range(N_GROUPS):
            per_token = jnp.sum(jnp.where(grp == g, dmx, 0.0), axis=-1, keepdims=True)
            by_position = sum(per_token[ci * CHUNK:(ci + 1) * CHUNK] for ci in range(tg // CHUNK))
            dsb = jnp.where(lane == g, by_position, dsb)
        dsb_ref[...] += dsb
        _accum_rows(dlg_ref, dzn * xhat)
        _accum_rows(dlb_ref, dzn)
        dxh = dzn * lg
        dzg = rstd * (dxh - _group_mean(dxh, grp) - xhat * _group_mean(dxh * xhat, grp))
        duz_ref[:, GMLP_W:] = dzg * _gelu_grad(zz, tz)
        to_dproj(i, slot).start()

        @pl.when(i == nsteps - 1)
        def _():
            for g in range(N_GROUPS):
                dw_ref[g] = jnp.where(causal, dw_ref[g], 0.0)
            to_dproj(i, slot).wait()
            if nsteps >= 2:
                to_dproj(i - 1, 1 - slot).wait()

    sds = jax.ShapeDtypeStruct
    return pl.pallas_call(
        body, name="gmlp_bwd", grid=(nsteps,),
        in_specs=[_rows(tg, GMLP_W)] * 3 + [_resident((1, GMLP_W)), _resident((1, GMLP_W)),
                                              _resident((N_GROUPS, CHUNK, CHUNK)), _resident((CHUNK, N_GROUPS))],
        out_specs=[_HBM, pl.BlockSpec((8, GMLP_W), lambda i: (0, 0)),
                   pl.BlockSpec((8, GMLP_W), lambda i: (0, 0)),
                   pl.BlockSpec((N_GROUPS, CHUNK, CHUNK), lambda i: (0, 0, 0)),
                   pl.BlockSpec((CHUNK, LANES), lambda i: (0, 0))],
        out_shape=[sds((T, IN_W), F32), sds((8, GMLP_W), F32), sds((8, GMLP_W), F32),
                   sds((N_GROUPS, CHUNK, CHUNK), F32), sds((CHUNK, LANES), F32)],
        scratch_shapes=[pltpu.VMEM((2, tg, 2 * GMLP_W), F32), pltpu.SemaphoreType.DMA((2,))],
        compiler_params=_params(("arbitrary",)),
    )(u, z, dgm, ln_g, ln_b, sgu_w, sgu_bt)


def _attn_bwd(q, k, v, dattn, attn, lse, dproj, owner_grads=()):
    T = q.shape[0]
    nt = T // ATT_TILE
    ns = len(owner_grads)
    steps = (ATTN_W // LANES) * nt

    def body(sl_ref, q_hbm, k_hbm, v_hbm, do_hbm, o_hbm, lse_hbm, _, *rest):
        p_refs, rest = rest[:ns], rest[ns:]
        dq_hbm = dk_hbm = dv_hbm = rest[0]
        r_refs, rest = rest[1:1 + ns], rest[1 + ns:]
        qbuf, dobuf, obuf, lbuf, kbuf, vbuf, dqbuf, dkbuf, dvbuf, delta_s = rest[:10]
        sem_q, sem_do, sem_o, sem_l, sem_k, sem_v, sem_dq, sem_dk, sem_dv = rest[10:19]
        hp, t = pl.program_id(0), pl.program_id(1)
        step = hp * nt + t
        two, three = step % 2, step % 3
        before, after = (step + 2) % 3, (step + 1) % 3
        if ns:
            start, finish = _owner_exchange_phases(p_refs, r_refs, *rest[19:])
            pl.when(step == 0)(start)

        def fetch(hp_, t_, two_, three_):
            for hbm, buf, sem, slot in ((q_hbm, qbuf, sem_q, two_), (do_hbm, dobuf, sem_do, two_),
                                        (o_hbm, obuf, sem_o, two_), (lse_hbm, lbuf, sem_l, two_),
                                        (k_hbm, kbuf, sem_k, three_), (v_hbm, vbuf, sem_v, three_)):
                for cp in _tile_copies(hbm, buf.at[slot], sem.at[slot], hp_, t_):
                    cp.start()

        @pl.when(step == 0)
        def _():
            kbuf[2] = jnp.zeros((ATT_BLOCKS, CHUNK, LANES), F32)
            vbuf[2] = jnp.zeros((ATT_BLOCKS, CHUNK, LANES), F32)
            dkbuf[3] = jnp.zeros((ATT_BLOCKS, CHUNK, LANES), F32)
            dvbuf[3] = jnp.zeros((ATT_BLOCKS, CHUNK, LANES), F32)
            fetch(0, 0, 0, 0)

        @pl.when(step + 1 < steps)
        def _():
            fetch((step + 1) // nt, (step + 1) % nt, 1 - two, after)

        for buf, sem in ((qbuf, sem_q), (dobuf, sem_do), (obuf, sem_o), (lbuf, sem_l)):
            _wait_tile(buf.at[two], sem.at[two])
        _wait_tile(kbuf.at[three], sem_k.at[three])
        _wait_tile(vbuf.at[three], sem_v.at[three])

        @pl.when(step >= 2)
        def _():
            _wait_tile(dqbuf.at[two], sem_dq.at[two])

        @pl.when(step >= 3)
        def _():
            _wait_tile(dkbuf.at[three], sem_dk.at[three])
            _wait_tile(dvbuf.at[three], sem_dv.at[three])

        q_t, do_t, l_t, k_t, v_t = qbuf.at[two], dobuf.at[two], lbuf.at[two], kbuf.at[three], vbuf.at[three]
        k_b, v_b = kbuf.at[before], vbuf.at[before]
        dq_t, dk_t, dv_t = dqbuf.at[two], dkbuf.at[three], dvbuf.at[three]
        dk_b, dv_b = dkbuf.at[before], dvbuf.at[before]
        sink = jnp.where(t > 0, before, 3)
        dk_sink, dv_sink = dkbuf.at[sink], dvbuf.at[sink]
        head0 = lax.broadcasted_iota(jnp.int32, (CHUNK, LANES), 1) < HEAD_DIM
        for r in range(ATT_BLOCKS):
            dd = dobuf[two, r] * obuf[two, r]
            d0 = jnp.sum(jnp.where(head0, dd, 0.0), axis=-1, keepdims=True)
            d1 = jnp.sum(jnp.where(head0, 0.0, dd), axis=-1, keepdims=True)
            delta_s[r] = jnp.where(head0, d0, d1)

        def column(xb):
            return jnp.concatenate([xb[:, 0:1], xb[:, HEAD_DIM:HEAD_DIM + 1]], axis=0)

        no_key_before = jnp.where(lax.broadcasted_iota(jnp.int32, (2 * CHUNK, 2 * CHUNK), 1) < CHUNK, NEG, 0.0)
        for d in DILATIONS:
            bias = _residue_bias(sl_ref, d)
            def scores(j, d=d, bias=bias):
                kcat = jnp.concatenate([_rm_block_before(k_t, k_b, d, j), _rm_block(k_t, d, j)], axis=0).astype(BF16)
                vcat = jnp.concatenate([_rm_block_before(v_t, v_b, d, j), _rm_block(v_t, d, j)], axis=0).astype(BF16)
                q2 = _stack_heads(_rm_block(q_t, d, j), head0)
                do2 = _stack_heads(_rm_block(do_t, d, j), head0)
                return (_dot(q2, kcat, NT), _dot(do2, vcat, NT), column(_rm_block(l_t, d, j)),
                        column(_rm_block(delta_s, d, j)), bias_first if _first_in_tile(d, j) else bias, kcat, q2, do2)

            bias_first = bias + jnp.where(t == 0, 1.0, 0.0) * no_key_before
            group = {}
            for j in range(ATT_BLOCKS):
                if j % SM_BLOCKS == 0:
                    group = {i: scores(i) for i in range(j, j + SM_BLOCKS)}
                    s_all, dp_all, lse_all, delta_all, bias_all = (
                        jnp.concatenate([g[i] for g in group.values()], axis=0) for i in range(5))
                    p_all = jnp.exp(s_all + bias_all - lse_all)
                    ds_all = (p_all * (dp_all - delta_all)).astype(BF16)
                    p_all = p_all.astype(BF16)
                at = slice((j % SM_BLOCKS) * 2 * CHUNK, (j % SM_BLOCKS + 1) * 2 * CHUNK)
                ds, p16 = ds_all[at, :], p_all[at, :]
                kcat, q2, do2 = group[j][5:]
                first = d == DILATIONS[0]
                _rm_add(dq_t, _residue_rows(d, j), _unstack_heads(_dot(ds, kcat), head0), first)
                ck = _dot(ds, q2, TN)
                cv = _dot(p16, do2, TN)
                _rm_add(dk_t, _residue_rows(d, j), ck[CHUNK:, :], first)
                _rm_add(dv_t, _residue_rows(d, j), cv[CHUNK:, :], first)
                if _first_in_tile(d, j):
                    rows = [(r, CHUNK - n, n) for r, _, n in _residue_rows(d, j)]
                    _rm_add(dk_sink, rows, ck[:CHUNK, :])
                    _rm_add(dv_sink, rows, cv[:CHUNK, :])
                else:
                    rows = [(r, lo - n, n) for r, lo, n in _residue_rows(d, j)]
                    _rm_add(dk_t, rows, ck[:CHUNK, :])
                    _rm_add(dv_t, rows, cv[:CHUNK, :])

        for r in range(ATT_BLOCKS):
            dqbuf[two, r] = dqbuf[two, r] * Q_SCALE
        for cp in _tile_copies(dq_hbm, dq_t, sem_dq.at[two], hp, t, to_hbm=True):
            cp.start()

        @pl.when(t > 0)
        def _():
            for cp in (_tile_copies(dk_hbm, dk_b, sem_dk.at[before], hp, t - 1, to_hbm=True, lane0=ATTN_W)
                       + _tile_copies(dv_hbm, dv_b, sem_dv.at[before], hp, t - 1, to_hbm=True, lane0=2 * ATTN_W)):
                cp.start()

        @pl.when(t == nt - 1)
        def _():
            for cp in (_tile_copies(dk_hbm, dk_t, sem_dk.at[three], hp, t, to_hbm=True, lane0=ATTN_W)
                       + _tile_copies(dv_hbm, dv_t, sem_dv.at[three], hp, t, to_hbm=True, lane0=2 * ATTN_W)):
                cp.start()

        @pl.when(step == steps - 1)
        def _():
            for slot in range(2):
                _wait_tile(dqbuf.at[slot], sem_dq.at[slot])
            for slot in range(3):
                _wait_tile(dkbuf.at[slot], sem_dk.at[slot])
                _wait_tile(dvbuf.at[slot], sem_dv.at[slot])

        if ns:
            pl.when(step == steps - 1)(finish)

    tile = lambda n: pltpu.VMEM((n, ATT_BLOCKS, CHUNK, LANES), F32)
    dma = lambda n: pltpu.SemaphoreType.DMA((n,))
    view = jax.ShapeDtypeStruct((T // ATT_BLOCKS, ATT_BLOCKS, ATTN_W), F32)
    outs = pl.pallas_call(
        body, name="attn_bwd", grid=(ATTN_W // LANES, nt),
        in_specs=[pl.BlockSpec((8, LANES), lambda c, t: (0, c))] + [_HBM] * (7 + ns),
        out_specs=[_HBM] * (1 + ns),
        out_shape=[jax.ShapeDtypeStruct((T // ATT_BLOCKS, ATT_BLOCKS, IN_W), F32)]
        + [jax.ShapeDtypeStruct(p.shape, p.dtype) for p in owner_grads],
        scratch_shapes=[tile(2), tile(2), tile(2), tile(2), tile(3), tile(3), tile(2), tile(4), tile(4),
                        pltpu.VMEM((ATT_BLOCKS, CHUNK, LANES), F32)]
        + [dma(2), dma(2), dma(2), dma(2), dma(3), dma(3), dma(2), dma(3), dma(3)]
        + (_owner_exchange_sems(ns) if ns else []),
        input_output_aliases={7: 0},
        compiler_params=_params(("arbitrary", "arbitrary")),
    )(_slope_table(), *[_residue_view(a) for a in (q, k, v, dattn, attn, lse, dproj)], *owner_grads)
    return outs[0].reshape(T, IN_W), tuple(outs[1:])


def _proj_bwd(dproj, w_in_t, x, g1, dh1, owner_grads=()):
    T = x.shape[0]
    tm = TM_PROJ
    ns = len(owner_grads)
    steps = T // tm

    def body(d_ref, w_ref, x_ref, g_ref, r_ref, *rest):
        p_refs, rest = rest[:ns], rest[ns:]
        dx_ref, dg_ref = rest[:2]
        r_refs, sems = rest[2:2 + ns], rest[2 + ns:]
        step = pl.program_id(0)
        if ns:
            start, finish = _owner_exchange_phases(p_refs, r_refs, *sems)
            pl.when(step == 0)(start)

        @pl.when(step == 0)
        def _():
            dg_ref[...] = jnp.zeros_like(dg_ref)

        dhn = _dot(d_ref[...].astype(BF16), w_ref[...])
        n1, r1 = _rms(x_ref[...])
        _accum_rows(dg_ref, dhn * n1)
        dx_ref[...] = r_ref[...] + _rms_bwd(n1, r1, g_ref[...], dhn)
        if ns:
            pl.when(step == steps - 1)(finish)

    outs = pl.pallas_call(
        body, name="proj_bwd", grid=(steps,),
        in_specs=[_rows(tm, IN_W), _resident((IN_W, D_MODEL)), _rows(tm, D_MODEL), _resident((1, D_MODEL)),
                  _rows(tm, D_MODEL)] + [_HBM] * ns,
        out_specs=[_rows(tm, D_MODEL), pl.BlockSpec((8, D_MODEL), lambda i: (0, 0))] + [_HBM] * ns,
        out_shape=[jax.ShapeDtypeStruct((T, D_MODEL), F32), jax.ShapeDtypeStruct((8, D_MODEL), F32)]
        + [jax.ShapeDtypeStruct(p.shape, p.dtype) for p in owner_grads],
        scratch_shapes=_owner_exchange_sems(ns) if ns else [],
        compiler_params=_params(("arbitrary",)),
    )(dproj, w_in_t, x, g1, dh1, *owner_grads)
    return outs[0], outs[1], tuple(outs[2:])


def _dw(a, b, name, tile, square_a=False, out_dtype=F32):
    T, ka = a.shape
    nb = b.shape[1]
    tka, tnb, tt = tile
    tt = min(tt, T)
    last = T // tt - 1

    def body(a_ref, b_ref, *refs):
        o_ref = refs[0]
        acc_ref = refs[1] if len(refs) > 1 else o_ref
        s = pl.program_id(2)

        @pl.when(s == 0)
        def _():
            acc_ref[...] = jnp.zeros_like(acc_ref)

        a_tile = a_ref[...]
        if square_a:
            a_tile = jnp.square(a_tile.astype(F32))
        acc_ref[...] += _dot(a_tile.astype(BF16), b_ref[...], TN)
        if acc_ref is not o_ref:
            @pl.when(s == last)
            def _():
                o_ref[...] = acc_ref[...].astype(out_dtype)

    return pl.pallas_call(
        body, name=name, grid=(ka // tka, nb // tnb, T // tt),
        in_specs=[pl.BlockSpec((tt, tka), lambda i, j, s: (s, i)), pl.BlockSpec((tt, tnb), lambda i, j, s: (s, j))],
        out_specs=pl.BlockSpec((tka, tnb), lambda i, j, s: (i, j)),
        out_shape=jax.ShapeDtypeStruct((ka, nb), out_dtype),
        scratch_shapes=[] if out_dtype == F32 else [pltpu.VMEM((tka, tnb), F32)],
        compiler_params=_params(("parallel", "parallel", "arbitrary")),
    )(a, b)


def _adamw_update(w, m, v, g):
    m2 = ADAM_B1 * m + (1.0 - ADAM_B1) * g
    v2 = ADAM_B2 * v + (1.0 - ADAM_B2) * jnp.square(g)
    m_hat = m2 / (1.0 - ADAM_B1 ** ADAM_STEP)
    v_hat = v2 / (1.0 - ADAM_B2 ** ADAM_STEP)
    return -ADAM_LR * (m_hat / (jnp.sqrt(v_hat) + ADAM_EPS) + ADAM_WD * w), m2, v2


def _adamw_tiny(ws, ms, vs, parts):
    n = len(ws)
    P = parts.shape[0]

    def body(*refs):
        w_refs, m_refs, v_refs, p_ref = refs[:n], refs[n:2 * n], refs[2 * n:3 * n], refs[3 * n]
        outs = refs[3 * n + 1:]

        def total(slot, rows):
            g = p_ref[0, 8 * slot:8 * slot + rows, :]
            for i in range(1, P):
                g = g + p_ref[i, 8 * slot:8 * slot + rows, :]
            return g

        for k in range(n):
            g = total(k, ws[k].shape[0])
            outs[4 * k][...] = g
            outs[4 * k + 1][...], outs[4 * k + 2][...], outs[4 * k + 3][...] = _adamw_update(
                w_refs[k][...], m_refs[k][...], v_refs[k][...], g)
        outs[4 * n][...] = total(n, 8)

    sds = jax.ShapeDtypeStruct
    return pl.pallas_call(
        body, name="adamw_tiny",
        out_shape=[sds(w.shape, F32) for w in ws for _ in range(4)] + [sds((8, LANES), F32)],
    )(*ws, *ms, *vs, parts)


def _adamw(w, m, v, parts, name, tr, transposed=False):
    R, C = w.shape
    P = parts.shape[0]

    def body(w_ref, m_ref, v_ref, p_ref, g_ref, d_ref, m2_ref, v2_ref):
        g = p_ref[0].astype(F32)
        for i in range(1, P):
            g = g + p_ref[i].astype(F32)
        if transposed:
            g = g.T
        g_ref[...] = g
        d_ref[...], m2_ref[...], v2_ref[...] = _adamw_update(w_ref[...], m_ref[...], v_ref[...], g)

    spec = _rows(tr, C)
    part_spec = (pl.BlockSpec((P, C, tr), lambda i: (0, 0, i)) if transposed
                 else pl.BlockSpec((P, tr, C), lambda i: (0, i, 0)))
    return pl.pallas_call(
        body, name=name, grid=(R // tr,),
        in_specs=[spec, spec, spec, part_spec],
        out_specs=[spec] * 4,
        out_shape=[jax.ShapeDtypeStruct((R, C), F32)] * 4,
        compiler_params=_params(("parallel",)),
    )(w, m, v, parts)


_HBM = pl.BlockSpec(memory_space=pltpu.HBM)


def _place():
    return lax.axis_index("x"), lax.axis_index("y"), lax.axis_index("c")


def _gathered_shape(shard):
    return jax.ShapeDtypeStruct((N_DEV,) + shard.shape, shard.dtype)


def _gather_sems(n):
    return [pltpu.SemaphoreType.DMA((7, n)), pltpu.SemaphoreType.DMA((7, n)), pltpu.SemaphoreType.DMA((n,))]


def _gather_phases(x_refs, out_refs, send_sems, recv_sems, local_sems):
    x, y, c = _place()
    me, sibling = (x, y, c), (x, y, 1 - c)
    chips = [(1 - x, y), (x, 1 - y), (1 - x, 1 - y)]
    arrays = range(len(x_refs))

    def slot(i, px, py, pc):
        return out_refs[i].at[4 * px + 2 * py + pc]

    def copy(i, k, block, to, own=False):
        return pltpu.make_async_remote_copy(
            src_ref=x_refs[i] if own else slot(i, *block), dst_ref=slot(i, *block),
            send_sem=send_sems.at[k, i], recv_sem=recv_sems.at[k, i], device_id=to, device_id_type=MESH)

    def mine(i):
        return pltpu.make_async_copy(x_refs[i], slot(i, *me), local_sems.at[i])

    def start():
        for i in arrays:
            mine(i).start()
            copy(i, 0, me, sibling, own=True).start()
            for j, chip in enumerate(chips):
                copy(i, 1 + j, me, (*chip, c), own=True).start()

    def forward():
        for i in arrays:
            for j, chip in enumerate(chips):
                copy(i, 1 + j, (*chip, c), me).wait_recv()
                copy(i, 4 + j, (*chip, c), sibling).start()

    def finish():
        for i in arrays:
            copy(i, 0, sibling, me).wait_recv()
            copy(i, 0, me, sibling, own=True).wait_send()
            for j, chip in enumerate(chips):
                copy(i, 4 + j, (*chip, 1 - c), me).wait_recv()
                copy(i, 1 + j, me, (*chip, c), own=True).wait_send()
                copy(i, 4 + j, (*chip, c), sibling).wait_send()
            mine(i).wait()

    return start, forward, finish


def _all_gather(shards, name):
    n = len(shards)

    def body(*refs):
        start, forward, finish = _gather_phases(refs[:n], refs[n:2 * n], *refs[2 * n:])
        start()
        forward()
        finish()

    return pl.pallas_call(
        body, name=name,
        out_shape=[_gathered_shape(s) for s in shards],
        in_specs=[_HBM] * n, out_specs=[_HBM] * n,
        scratch_shapes=_gather_sems(n),
    )(*shards)


def _owner_exchange_sems(n):
    return [pltpu.SemaphoreType.DMA((7, n)), pltpu.SemaphoreType.DMA((7, n)), pltpu.SemaphoreType.DMA((n,))]


def _owner_exchange_phases(g_refs, r_refs, send_sems, recv_sems, local_sems):
    x, y, c = _place()
    me = 4 * x + 2 * y + c
    flip = lambda v, bit: 1 - v if bit else v
    peers = [(flip(x, k & 4), flip(y, k & 2), flip(c, k & 1)) for k in range(1, N_DEV)]
    arrays = range(len(g_refs))

    def mine(i):
        return pltpu.make_async_copy(g_refs[i].at[me], r_refs[i].at[me], local_sems.at[i])

    def copy(i, k, src_slot, dst_slot):
        return pltpu.make_async_remote_copy(
            src_ref=g_refs[i].at[src_slot], dst_ref=r_refs[i].at[dst_slot],
            send_sem=send_sems.at[k, i], recv_sem=recv_sems.at[k, i], device_id=peers[k], device_id_type=MESH)

    def start():
        for i in arrays:
            mine(i).start()
            for k, (px, py, pc) in enumerate(peers):
                copy(i, k, 4 * px + 2 * py + pc, me).start()

    def finish():
        for i in arrays:
            for k, (px, py, pc) in enumerate(peers):
                copy(i, k, me, 4 * px + 2 * py + pc).wait_recv()
                copy(i, k, 4 * px + 2 * py + pc, me).wait_send()
            mine(i).wait()

    return start, finish


def _local_step(x, tgt, small, w_in_t, rest, exchange=False):
    g1, g2, gf = small["norm1_g"], small["norm2_g"], small["final_norm_g"].reshape(1, D_MODEL)
    ga, gg = small["attn_out_g"], small["gmlp_out_g"]
    ln_g = small["sgu_ln_g"].reshape(1, GMLP_W)
    ln_b = small["sgu_ln_b"].reshape(1, GMLP_W)
    sgu_w = small["sgu_w"][0]
    sgu_bt = small["sgu_b"][0].T

    hn1, q, k, v, u, z = _proj_fwd(x, g1, w_in_t)
    attn, lse, gathered = _attn_fwd(q, k, v, shards=rest if exchange else ())
    w_out, w_ff1_t, w_ff2 = [g.reshape(-1, D_MODEL) for g in gathered] if exchange else rest
    gm = _gmlp_fwd(u, z, ln_g, ln_b, sgu_w, sgu_bt)
    mixed, h1, hn2 = _out_fwd(attn, gm, ga, gg, w_out, x, g2)
    relu, dh2f, dh2b, loss8, dgf8 = _ffn_fwd(hn2, h1, w_ff1_t, w_ff2, gf, tgt)

    da, dh1f, dh1b, dg2 = _ffn_bwd(dh2b, dh2f, relu, h1, g2, w_ff2, w_ff1_t)
    wire = BF16 if exchange else F32
    dw_ff2 = _dw(relu, dh2b, "dw_ff2", DW_TILE, square_a=True, out_dtype=wire)
    dw_ff1_t = _dw(da, hn2, "dw_ff1", DW_TILE, out_dtype=wire)
    dattn, dgm, dga, dgg = _out_bwd(dh1b, w_out, attn, gm, ga, gg)
    dw_out = _dw(mixed, dh1b, "dw_out", DW_TILE, out_dtype=wire)
    early = [dw_out, dw_ff1_t, dw_ff2]
    if exchange:
        early = [g.reshape(N_DEV, -1, D_MODEL) for g in early]
    dproj, dlg, dlb, dsw, dsb = _gmlp_bwd(u, z, dgm, ln_g, ln_b, sgu_w, sgu_bt)
    dproj, arrived = _attn_bwd(q, k, v, dattn, attn, lse, dproj, owner_grads=early if exchange else ())
    dw_in_t = _dw(dproj, hn1, "dw_in", DW_TILE_IN, out_dtype=wire)
    late = (dw_in_t.reshape(N_DEV, -1, D_MODEL),) if exchange else ()
    dx, dg1, late = _proj_bwd(dproj, w_in_t, x, g1, dh1f, owner_grads=late)
    if exchange:
        dw_in_t, early = late[0], arrived

    small_grads = dict(
        norm1_g=dg1[0], sgu_ln_g=dlg[0], sgu_ln_b=dlb[0], sgu_w=dsw, sgu_b=dsb[:, :N_GROUPS].T,
        attn_out_g=dga[0], gmlp_out_g=dgg[0], norm2_g=dg2[0], final_norm_g=dgf8[0])
    return loss8[0, 0], dx, (dw_in_t, *early), small_grads


SMALL_NAMES = ("norm1_g", "sgu_ln_g", "sgu_ln_b", "sgu_w", "sgu_b", "attn_out_g", "gmlp_out_g", "norm2_g",
               "final_norm_g")
WEIGHT_ORDER = ("norm1_g", "w_in", "sgu_ln_g", "sgu_ln_b", "sgu_w", "sgu_b", "attn_out_g", "gmlp_out_g", "w_out",
                "norm2_g", "w_ff1", "w_ff2", "final_norm_g")


TINY_NAMES = tuple(n for n in SMALL_NAMES if n != "sgu_w")


def _as_rows(a):
    return a.reshape(-1, LANES)


def _pack_tiny_grads(d, loss):
    slots = [jnp.pad(_as_rows(d[n]), ((0, 8 - d[n].size // LANES), (0, 0))) for n in TINY_NAMES]
    return jnp.concatenate(slots + [jnp.full((8, LANES), loss, F32)], axis=0)


def kernel(x, norm1_g, w_in, sgu_ln_g, sgu_ln_b, sgu_w, sgu_b, attn_out_g, gmlp_out_g, w_out, norm2_g, w_ff1, w_ff2, final_norm_g, loss_target, m_norm1_g, m_w_in, m_sgu_ln_g, m_sgu_ln_b, m_sgu_w, m_sgu_b, m_attn_out_g, m_gmlp_out_g, m_w_out, m_norm2_g, m_w_ff1, m_w_ff2, m_final_norm_g, v_norm1_g, v_w_in, v_sgu_ln_g, v_sgu_ln_b, v_sgu_w, v_sgu_b, v_attn_out_g, v_gmlp_out_g, v_w_out, v_norm2_g, v_w_ff1, v_w_ff2, v_final_norm_g):
    w = dict(norm1_g=norm1_g, w_in=w_in, sgu_ln_g=sgu_ln_g, sgu_ln_b=sgu_ln_b, sgu_w=sgu_w, sgu_b=sgu_b,
             attn_out_g=attn_out_g, gmlp_out_g=gmlp_out_g, w_out=w_out, norm2_g=norm2_g, w_ff1=w_ff1, w_ff2=w_ff2,
             final_norm_g=final_norm_g)
    m = dict(norm1_g=m_norm1_g, w_in=m_w_in, sgu_ln_g=m_sgu_ln_g, sgu_ln_b=m_sgu_ln_b, sgu_w=m_sgu_w, sgu_b=m_sgu_b,
             attn_out_g=m_attn_out_g, gmlp_out_g=m_gmlp_out_g, w_out=m_w_out, norm2_g=m_norm2_g, w_ff1=m_w_ff1,
             w_ff2=m_w_ff2, final_norm_g=m_final_norm_g)
    v = dict(norm1_g=v_norm1_g, w_in=v_w_in, sgu_ln_g=v_sgu_ln_g, sgu_ln_b=v_sgu_ln_b, sgu_w=v_sgu_w, sgu_b=v_sgu_b,
             attn_out_g=v_attn_out_g, gmlp_out_g=v_gmlp_out_g, w_out=v_w_out, norm2_g=v_norm2_g, w_ff1=v_w_ff1,
             w_ff2=v_w_ff2, final_norm_g=v_final_norm_g)
    big = ("w_in", "w_out", "w_ff1", "w_ff2")

    w_in_t, = _all_gather([w_in[0].T.astype(BF16)], "w_in_all_gather")
    rest = (w_out[0].astype(BF16), w_ff1[0].T.astype(BF16), w_ff2[0].astype(BF16))
    loss, dx, parts, small_grads = _local_step(x[0], loss_target[0], {n: w[n] for n in SMALL_NAMES},
                                               w_in_t.reshape(IN_W, D_MODEL), rest, exchange=True)

    new = {}
    for n, p, transposed, tr in zip(big, parts, (True, False, True, False), (128, 128, 128, 256)):
        new[n] = [a[None] for a in _adamw(w[n][0], m[n][0], v[n][0], p, "adamw_" + n, tr, transposed)]

    tiny_parts, sgu_parts = _all_gather(
        [_pack_tiny_grads(small_grads, loss), _as_rows(small_grads["sgu_w"]).astype(BF16)], "small_grad_all_gather")
    tiny = _adamw_tiny(*[[_as_rows(src[n]) for n in TINY_NAMES] for src in (w, m, v)], tiny_parts)
    sgu = _adamw(_as_rows(sgu_w), _as_rows(m_sgu_w), _as_rows(v_sgu_w), sgu_parts, "adamw_sgu_w", 512)
    loss = tiny[-1][0, 0]

    outs = []
    for i in range(4):
        d = {n: new[n][i] for n in big}
        d.update({n: tiny[4 * k + i].reshape(w[n].shape) for k, n in enumerate(TINY_NAMES)})
        d["sgu_w"] = sgu[i].reshape(sgu_w.shape)
        outs.extend(d[n] for n in WEIGHT_ORDER)
    return (loss, dx[None], *outs)
```

```python
import math

import numpy as np
import jax
import jax.numpy as jnp
from jax import lax
from jax.experimental import pallas as pl
from jax.experimental.pallas import tpu as pltpu

F32 = jnp.float32
BF16 = jnp.bfloat16

D_MODEL = 1024
HEAD_DIM = 64
N_HEADS = 12
ATTN_W = N_HEADS * HEAD_DIM
N_GROUPS = 4
GMLP_W = N_GROUPS * HEAD_DIM
IN_W = 3 * ATTN_W + 2 * GMLP_W
D_FF = 4 * D_MODEL
CHUNK = 128
DILATIONS = (1, 4, 16)
EPS = 1e-6
Q_SCALE = HEAD_DIM ** -0.5
NEG = -1e30

ADAM_LR, ADAM_B1, ADAM_B2, ADAM_EPS, ADAM_WD, ADAM_STEP = 0.001, 0.9, 0.999, 1e-08, 0.01, 10

N_DEV = 8
LANES = 128
VMEM_LIMIT = 56 << 20

TM_PROJ = 512
TM_FFN = 512
FF_CHUNK = 512
TM_GMLP = 1024
DW_TILE = (512, 1024, 8192)
DW_TILE_IN = (IN_W // 2, 1024, 2048)

MESH = pl.DeviceIdType.MESH


def _alibi_slopes(n):
    def pow2(m):
        start = 2.0 ** (-8.0 / m)
        return [start ** (i + 1) for i in range(m)]
    c = 2 ** int(math.floor(math.log2(n)))
    s = pow2(n) if c == n else pow2(c) + pow2(2 * c)[0::2][: n - c]
    return np.asarray(s, dtype=np.float32)


SLOPES = _alibi_slopes(N_HEADS)


def _params(sem=None):
    kw = dict(vmem_limit_bytes=VMEM_LIMIT)
    if sem is not None:
        kw["dimension_semantics"] = sem
    return pltpu.CompilerParams(**kw)


def _rows(tm, n):
    return pl.BlockSpec((tm, n), lambda i: (i, 0))


def _resident(shape):
    return pl.BlockSpec(shape, lambda *_: (0,) * len(shape), pipeline_mode=pl.Buffered(1))


def _rms(x):
    r = lax.rsqrt(jnp.mean(x * x, axis=-1, keepdims=True) + EPS)
    return x * r, r


def _rms_bwd(n, r, g, dy):
    dn = dy * g
    return r * (dn - n * jnp.mean(dn * n, axis=-1, keepdims=True))


def _accum_rows(acc_ref, v):
    acc_ref[...] += jnp.broadcast_to(jnp.sum(v, axis=0, keepdims=True), acc_ref.shape)


_G0 = math.sqrt(2.0 / math.pi)
_G1 = 0.044715


def _gelu(x):
    t = jnp.tanh(_G0 * (x + _G1 * (x * x * x)))
    return x * (0.5 * (1.0 + t)), t


def _gelu_grad(x, t):
    return 0.5 * (1.0 + t) + 0.5 * x * (1.0 - t * t) * (_G0 * (1.0 + 3.0 * _G1 * x * x))


NT = (((1,), (1,)), ((), ()))
TN = (((0,), (0,)), ((), ()))


def _dot(a, b, dims=None):
    if dims is None:
        return jnp.dot(a, b, preferred_element_type=F32)
    return lax.dot_general(a, b, dims, preferred_element_type=F32)


def _proj_fwd(x, g1, w_in_t):
    T = x.shape[0]
    tm = TM_PROJ

    def body(x_ref, g_ref, w_ref, hn_ref, q_ref, k_ref, v_ref, u_ref, z_ref):
        n, _ = _rms(x_ref[...])
        hn = (n * g_ref[...]).astype(BF16)
        hn_ref[...] = hn
        a = ATTN_W
        q_ref[...] = _dot(hn, w_ref[0:a, :], NT) * Q_SCALE
        k_ref[...] = _dot(hn, w_ref[a:2 * a, :], NT)
        v_ref[...] = _dot(hn, w_ref[2 * a:3 * a, :], NT)
        u_ref[...] = _dot(hn, w_ref[3 * a:3 * a + GMLP_W, :], NT)
        z_ref[...] = _dot(hn, w_ref[3 * a + GMLP_W:, :], NT)

    sds = jax.ShapeDtypeStruct
    return pl.pallas_call(
        body, name="proj_fwd", grid=(T // tm,),
        in_specs=[_rows(tm, D_MODEL), _resident((1, D_MODEL)), _resident((IN_W, D_MODEL))],
        out_specs=[_rows(tm, D_MODEL), _rows(tm, ATTN_W), _rows(tm, ATTN_W), _rows(tm, ATTN_W),
                   _rows(tm, GMLP_W), _rows(tm, GMLP_W)],
        out_shape=[sds((T, D_MODEL), BF16), sds((T, ATTN_W), F32), sds((T, ATTN_W), F32),
                   sds((T, ATTN_W), F32), sds((T, GMLP_W), F32), sds((T, GMLP_W), F32)],
        compiler_params=_params(("parallel",)),
    )(x, g1, w_in_t)


ATT_TILE = 2048
ATT_BLOCKS = ATT_TILE // CHUNK
SM_BLOCKS = 4


def _slope_table():
    row = np.repeat(SLOPES, HEAD_DIM)
    return jnp.asarray(np.broadcast_to(row[None], (8, ATTN_W)), F32)


def _residue_view(a):
    return a.reshape(a.shape[0] // ATT_BLOCKS, ATT_BLOCKS, a.shape[1])


def _tile_copies(hbm, buf, sem, hp, t, to_hbm=False, lane0=0):
    rows = pl.ds(pl.multiple_of(t * CHUNK, CHUNK), CHUNK)
    lanes = pl.ds(pl.multiple_of(lane0 + hp * LANES, LANES), LANES)
    pairs = [(hbm.at[rows, r, lanes], buf.at[r]) for r in range(ATT_BLOCKS)]
    return [pltpu.make_async_copy(v, h, sem) if to_hbm else pltpu.make_async_copy(h, v, sem) for h, v in pairs]


def _wait_tile(buf, sem):
    pltpu.make_async_copy(buf, buf, sem).wait()


def _residue_rows(d, j):
    if d == 16:
        return [(j, 0, CHUNK)]
    if d == 4:
        return [(j % 4 + 4 * m, 32 * (j // 4), 32) for m in range(4)]
    return [(r, 8 * j, 8) for r in range(ATT_BLOCKS)]


def _block_order(p, d):
    if d == 16:
        return p
    if d == 4:
        return 4 * (p & 31) + (p >> 5)
    return 16 * (p & 7) + (p >> 3)


def _first_in_tile(d, j):
    return _residue_rows(d, j)[0][1] == 0


def _rm_block(buf, d, j):
    return jnp.concatenate([buf[r, lo:lo + n, :] for r, lo, n in _residue_rows(d, j)], axis=0)


def _rm_block_before(buf, buf_before, d, j):
    if _first_in_tile(d, j):
        return jnp.concatenate([buf_before[r, CHUNK - n:CHUNK, :] for r, _, n in _residue_rows(d, j)], axis=0)
    return jnp.concatenate([buf[r, lo - n:lo, :] for r, lo, n in _residue_rows(d, j)], axis=0)


def _rm_store(buf, d, j, val):
    at = 0
    for r, lo, n in _residue_rows(d, j):
        buf[r, lo:lo + n, :] = val[at:at + n, :]
        at += n


def _rm_add(buf, rows, val, first=False):
    at = 0
    for r, lo, n in rows:
        if first:
            buf[r, lo:lo + n, :] = val[at:at + n, :]
        else:
            buf[r, lo:lo + n, :] += val[at:at + n, :]
        at += n


def _residue_bias(sl_ref, d):
    shape = (2 * CHUNK, 2 * CHUNK)
    row = lax.broadcasted_iota(jnp.int32, shape, 0)
    col = lax.broadcasted_iota(jnp.int32, shape, 1)
    steps = _block_order(row & (CHUNK - 1), d) + CHUNK - (_block_order(col & (CHUNK - 1), d) + (col & CHUNK))
    band = (steps >= 0) & (steps <= CHUNK)
    sl = sl_ref[0:1, :]
    upper = lax.broadcasted_iota(jnp.int32, (2 * CHUNK, 1), 0) < CHUNK
    slope2 = jnp.where(upper, sl[:, 0:1], sl[:, HEAD_DIM:HEAD_DIM + 1])
    return jnp.where(band, -(float(d) * slope2 * steps.astype(F32)), NEG)


def _stack_heads(xb, head0):
    zero = jnp.zeros_like(xb)
    return jnp.concatenate([jnp.where(head0, xb, zero), jnp.where(head0, zero, xb)], axis=0).astype(BF16)


def _unstack_heads(x2, head0):
    return jnp.where(head0, x2[:CHUNK, :], x2[CHUNK:, :])


def _attn_fwd(q, k, v, shards=()):
    T = q.shape[0]
    nt = T // ATT_TILE
    ns = len(shards)
    steps = (ATTN_W // LANES) * nt

    def body(sl_ref, q_hbm, k_hbm, v_hbm, *rest):
        x_refs, rest = rest[:ns], rest[ns:]
        attn_hbm, lse_hbm = rest[:2]
        g_refs, rest = rest[2:2 + ns], rest[2 + ns:]
        qbuf, kbuf, vbuf, obuf, lbuf = rest[:5]
        o_acc, l_acc = rest[5:8], rest[8:11]
        sem_q, sem_k, sem_v, sem_o, sem_l = rest[11:16]
        hp, t = pl.program_id(0), pl.program_id(1)
        step = hp * nt + t
        two, three = step % 2, step % 3
        before, after = (step + 2) % 3, (step + 1) % 3
        if ns:
            start, forward, finish = _gather_phases(x_refs, g_refs, *rest[16:])
            pl.when(step == 0)(start)
            pl.when(step == steps // 2)(forward)

        def fetch(hp_, t_, two_, three_):
            for cp in (_tile_copies(q_hbm, qbuf.at[two_], sem_q.at[two_], hp_, t_)
                       + _tile_copies(k_hbm, kbuf.at[three_], sem_k.at[three_], hp_, t_)
                       + _tile_copies(v_hbm, vbuf.at[three_], sem_v.at[three_], hp_, t_)):
                cp.start()

        @pl.when(step == 0)
        def _():
            kbuf[2] = jnp.zeros((ATT_BLOCKS, CHUNK, LANES), F32)
            vbuf[2] = jnp.zeros((ATT_BLOCKS, CHUNK, LANES), F32)
            fetch(0, 0, 0, 0)

        @pl.when(step + 1 < steps)
        def _():
            fetch((step + 1) // nt, (step + 1) % nt, 1 - two, after)

        _wait_tile(qbuf.at[two], sem_q.at[two])
        _wait_tile(kbuf.at[three], sem_k.at[three])
        _wait_tile(vbuf.at[three], sem_v.at[three])

        @pl.when(step >= 2)
        def _():
            _wait_tile(obuf.at[two], sem_o.at[two])
            _wait_tile(lbuf.at[two], sem_l.at[two])

        q_t, k_t, v_t = qbuf.at[two], kbuf.at[three], vbuf.at[three]
        k_b, v_b = kbuf.at[before], vbuf.at[before]
        head0 = lax.broadcasted_iota(jnp.int32, (CHUNK, LANES), 1) < HEAD_DIM
        no_key_before = jnp.where(lax.broadcasted_iota(jnp.int32, (2 * CHUNK, 2 * CHUNK), 1) < CHUNK, NEG, 0.0)
        for pi, d in enumerate(DILATIONS):
            bias = _residue_bias(sl_ref, d)

            def scores(j, d=d, bias=bias):
                kcat = jnp.concatenate([_rm_block_before(k_t, k_b, d, j), _rm_block(k_t, d, j)], axis=0).astype(BF16)
                vcat = jnp.concatenate([_rm_block_before(v_t, v_b, d, j), _rm_block(v_t, d, j)], axis=0).astype(BF16)
                s = _dot(_stack_heads(_rm_block(q_t, d, j), head0), kcat, NT)
                return s, vcat, bias_first if _first_in_tile(d, j) else bias

            bias_first = bias + jnp.where(t == 0, 1.0, 0.0) * no_key_before
            for j0 in range(0, ATT_BLOCKS, SM_BLOCKS):
                group = [scores(j) for j in range(j0, j0 + SM_BLOCKS)]
                s = jnp.concatenate([g[0] for g in group], axis=0) + jnp.concatenate([g[2] for g in group], axis=0)
                m = jnp.max(s, axis=-1, keepdims=True)
                p = jnp.exp(s - m)
                l = jnp.sum(p, axis=-1, keepdims=True)
                p = p.astype(BF16)
                block = lambda a, i: a[i * 2 * CHUNK:(i + 1) * 2 * CHUNK, :]
                o = jnp.concatenate([_dot(block(p, i), g[1]) for i, g in enumerate(group)], axis=0) * (1.0 / l)
                lse = jnp.broadcast_to(m + jnp.log(l), o.shape)
                for i in range(SM_BLOCKS):
                    _rm_store(o_acc[pi], d, j0 + i, _unstack_heads(block(o, i), head0))
                    _rm_store(l_acc[pi], d, j0 + i, _unstack_heads(block(lse, i), head0))

        a, b, c = l_acc[0][...], l_acc[1][...], l_acc[2][...]
        m = jnp.maximum(jnp.maximum(a, b), c)
        ea, eb, ec = jnp.exp(a - m), jnp.exp(b - m), jnp.exp(c - m)
        tot = ea + eb + ec
        obuf[two] = (ea * o_acc[0][...] + eb * o_acc[1][...] + ec * o_acc[2][...]) / tot
        lbuf[two] = m + jnp.log(tot)

        for cp in (_tile_copies(attn_hbm, obuf.at[two], sem_o.at[two], hp, t, to_hbm=True)
                   + _tile_copies(lse_hbm, lbuf.at[two], sem_l.at[two], hp, t, to_hbm=True)):
            cp.start()

        @pl.when(step == steps - 1)
        def _():
            for slot in (two, 1 - two)[:min(steps, 2)]:
                _wait_tile(obuf.at[slot], sem_o.at[slot])
                _wait_tile(lbuf.at[slot], sem_l.at[slot])

        if ns:
            pl.when(step == steps - 1)(finish)

    tile = lambda n: pltpu.VMEM((n, ATT_BLOCKS, CHUNK, LANES), F32)
    dma = lambda n: pltpu.SemaphoreType.DMA((n,))
    view = jax.ShapeDtypeStruct((T // ATT_BLOCKS, ATT_BLOCKS, ATTN_W), F32)
    outs = pl.pallas_call(
        body, name="attn_fwd", grid=(ATTN_W // LANES, nt),
        in_specs=[pl.BlockSpec((8, LANES), lambda c, t: (0, c))] + [_HBM] * (3 + ns),
        out_specs=[_HBM] * (2 + ns),
        out_shape=[view, view] + [_gathered_shape(s) for s in shards],
        scratch_shapes=[tile(2), tile(3), tile(3), tile(2), tile(2)] + [pltpu.VMEM((ATT_BLOCKS, CHUNK, LANES), F32)] * 6
        + [dma(2), dma(3), dma(3), dma(2), dma(2)] + (_gather_sems(ns) if ns else []),
        compiler_params=_params(("arbitrary", "arbitrary")),
    )(_slope_table(), _residue_view(q), _residue_view(k), _residue_view(v), *shards)
    return outs[0].reshape(T, ATTN_W), outs[1].reshape(T, ATTN_W), tuple(outs[2:])


def _group_mean(v, grp):
    out = jnp.zeros_like(v)
    for g in range(N_GROUPS):
        mk = grp == g
        s = jnp.sum(jnp.where(mk, v, 0.0), axis=-1, keepdims=True) * (1.0 / HEAD_DIM)
        out = jnp.where(mk, s, out)
    return out


def _gmlp_core(uu, zz, lg, lb, ws, sb_ref, grp):
    ug, tu = _gelu(uu)
    zg, tz = _gelu(zz)
    zc = zg - _group_mean(zg, grp)
    rstd = lax.rsqrt(_group_mean(zc * zc, grp) + EPS)
    xhat = zc * rstd
    zn16 = (xhat * lg + lb).astype(BF16)
    mixed = []
    for ci in range(uu.shape[0] // CHUNK):
        rows = slice(ci * CHUNK, (ci + 1) * CHUNK)
        m = jnp.zeros((CHUNK, GMLP_W), F32)
        for g in range(N_GROUPS):
            m = jnp.where(grp[:CHUNK] == g, _dot(ws[g], zn16[rows, :]) + sb_ref[:, g:g + 1], m)
        mixed.append(m)
    return ug, tu, tz, xhat, rstd, zn16, jnp.concatenate(mixed, axis=0)


def _causal_ws(w_ref):
    ti = lax.broadcasted_iota(jnp.int32, (CHUNK, CHUNK), 0)
    si = lax.broadcasted_iota(jnp.int32, (CHUNK, CHUNK), 1)
    causal = si <= ti
    return causal, [jnp.where(causal, w_ref[g], 0.0).astype(BF16) for g in range(N_GROUPS)]


def _gmlp_fwd(u, z, ln_g, ln_b, sgu_w, sgu_bt):
    T = u.shape[0]
    tg = TM_GMLP

    def body(u_ref, z_ref, g_ref, b_ref, w_ref, sb_ref, out_ref):
        grp = lax.broadcasted_iota(jnp.int32, (tg, GMLP_W), 1) // HEAD_DIM
        _, ws = _causal_ws(w_ref)
        ug, _, _, _, _, _, mixed = _gmlp_core(u_ref[...], z_ref[...], g_ref[...], b_ref[...], ws, sb_ref, grp)
        out_ref[...] = ug * mixed

    return pl.pallas_call(
        body, name="gmlp_fwd", grid=(T // tg,),
        in_specs=[_rows(tg, GMLP_W), _rows(tg, GMLP_W), _resident((1, GMLP_W)), _resident((1, GMLP_W)),
                  _resident((N_GROUPS, CHUNK, CHUNK)), _resident((CHUNK, N_GROUPS))],
        out_specs=_rows(tg, GMLP_W),
        out_shape=jax.ShapeDtypeStruct((T, GMLP_W), F32),
        compiler_params=_params(("parallel",)),
    )(u, z, ln_g, ln_b, sgu_w, sgu_bt)


def _out_fwd(attn, gm, ga, gg, w_out, x, g2):
    T = x.shape[0]
    tm = TM_PROJ

    def body(a_ref, m_ref, ga_ref, gg_ref, w_ref, x_ref, g2_ref, mix_ref, h1_ref, hn2_ref):
        an, _ = _rms(a_ref[...])
        gn, _ = _rms(m_ref[...])
        an = (an * ga_ref[...]).astype(BF16)
        gn = (gn * gg_ref[...]).astype(BF16)
        mix_ref[:, 0:ATTN_W] = an
        mix_ref[:, ATTN_W:] = gn
        h1 = x_ref[...] + _dot(an, w_ref[0:ATTN_W, :]) + _dot(gn, w_ref[ATTN_W:, :])
        h1_ref[...] = h1
        n2, _ = _rms(h1)
        hn2_ref[...] = (n2 * g2_ref[...]).astype(BF16)

    sds = jax.ShapeDtypeStruct
    return pl.pallas_call(
        body, name="out_fwd", grid=(T // tm,),
        in_specs=[_rows(tm, ATTN_W), _rows(tm, GMLP_W), _resident((1, ATTN_W)), _resident((1, GMLP_W)),
                  _resident((D_MODEL, D_MODEL)), _rows(tm, D_MODEL), _resident((1, D_MODEL))],
        out_specs=[_rows(tm, D_MODEL)] * 3,
        out_shape=[sds((T, D_MODEL), BF16), sds((T, D_MODEL), F32), sds((T, D_MODEL), BF16)],
        compiler_params=_params(("parallel",)),
    )(attn, gm, ga, gg, w_out, x, g2)


def _ffn_fwd(hn2, h1, w1t, w2, gf, tgt):
    T = h1.shape[0]
    tm = TM_FFN

    def body(hn_ref, h1_ref, w1_ref, w2_ref, gf_ref, t_ref, r_ref, dhf_ref, dhb_ref, loss_ref, dgf_ref):
        i = pl.program_id(0)

        @pl.when(i == 0)
        def _():
            loss_ref[...] = jnp.zeros_like(loss_ref)
            dgf_ref[...] = jnp.zeros_like(dgf_ref)

        hn = hn_ref[...]
        acc = h1_ref[...]
        for j in range(D_FF // FF_CHUNK):
            cols = slice(j * FF_CHUNK, (j + 1) * FF_CHUNK)
            r = jnp.maximum(_dot(hn, w1_ref[cols, :], NT), 0.0)
            r_ref[:, cols] = r.astype(BF16)
            act = jnp.square(r).astype(BF16)
            acc = acc + _dot(act, w2_ref[cols, :])
        n3, r3 = _rms(acc)
        gf_row = gf_ref[...]
        e = n3 * gf_row - t_ref[...]
        loss_ref[...] += 0.5 * jnp.sum(jnp.mean(e * e, axis=-1, keepdims=True))
        dy = e * (1.0 / D_MODEL)
        _accum_rows(dgf_ref, dy * n3)
        dh2 = _rms_bwd(n3, r3, gf_row, dy)
        dhf_ref[...] = dh2
        dhb_ref[...] = dh2.astype(BF16)

    sds = jax.ShapeDtypeStruct
    acc_spec = lambda n: pl.BlockSpec((8, n), lambda i: (0, 0))
    return pl.pallas_call(
        body, name="ffn_fwd", grid=(T // tm,),
        in_specs=[_rows(tm, D_MODEL), _rows(tm, D_MODEL), _resident((D_FF, D_MODEL)), _resident((D_FF, D_MODEL)),
                  _resident((1, D_MODEL)), _rows(tm, D_MODEL)],
        out_specs=[_rows(tm, D_FF), _rows(tm, D_MODEL), _rows(tm, D_MODEL), acc_spec(LANES), acc_spec(D_MODEL)],
        out_shape=[sds((T, D_FF), BF16), sds((T, D_MODEL), F32), sds((T, D_MODEL), BF16),
                   sds((8, LANES), F32), sds((8, D_MODEL), F32)],
        compiler_params=_params(("arbitrary",)),
    )(hn2, h1, w1t, w2, gf, tgt)


def _ffn_bwd(dh2b, dh2f, relu, h1, g2, w2, w1t):
    T = h1.shape[0]
    tm = TM_FFN

    def body(db_ref, df_ref, r_ref, h1_ref, g2_ref, w2_ref, w1t_ref, da_ref, d1f_ref, d1b_ref, dg_ref):
        @pl.when(pl.program_id(0) == 0)
        def _():
            dg_ref[...] = jnp.zeros_like(dg_ref)

        db = db_ref[...]
        acc = jnp.zeros((tm, D_MODEL), F32)
        for j in range(D_FF // FF_CHUNK):
            cols = slice(j * FF_CHUNK, (j + 1) * FF_CHUNK)
            da = (_dot(db, w2_ref[cols, :], NT) * (2.0 * r_ref[:, cols].astype(F32))).astype(BF16)
            da_ref[:, cols] = da
            acc = acc + _dot(da, w1t_ref[cols, :])
        n2, r2 = _rms(h1_ref[...])
        _accum_rows(dg_ref, acc * n2)
        dh1 = df_ref[...] + _rms_bwd(n2, r2, g2_ref[...], acc)
        d1f_ref[...] = dh1
        d1b_ref[...] = dh1.astype(BF16)

    sds = jax.ShapeDtypeStruct
    return pl.pallas_call(
        body, name="ffn_bwd", grid=(T // tm,),
        in_specs=[_rows(tm, D_MODEL), _rows(tm, D_MODEL), _rows(tm, D_FF), _rows(tm, D_MODEL),
                  _resident((1, D_MODEL)), _resident((D_FF, D_MODEL)), _resident((D_FF, D_MODEL))],
        out_specs=[_rows(tm, D_FF), _rows(tm, D_MODEL), _rows(tm, D_MODEL),
                   pl.BlockSpec((8, D_MODEL), lambda i: (0, 0))],
        out_shape=[sds((T, D_FF), BF16), sds((T, D_MODEL), F32), sds((T, D_MODEL), BF16), sds((8, D_MODEL), F32)],
        compiler_params=_params(("arbitrary",)),
    )(dh2b, dh2f, relu, h1, g2, w2, w1t)


def _out_bwd(dh1b, w_out, attn, gm, ga, gg):
    T = attn.shape[0]
    tm = TM_PROJ

    def body(d_ref, w_ref, a_ref, m_ref, ga_ref, gg_ref, da_ref, dm_ref, dga_ref, dgg_ref):
        @pl.when(pl.program_id(0) == 0)
        def _():
            dga_ref[...] = jnp.zeros_like(dga_ref)
            dgg_ref[...] = jnp.zeros_like(dgg_ref)

        d = d_ref[...]
        dan = _dot(d, w_ref[0:ATTN_W, :], NT)
        dgn = _dot(d, w_ref[ATTN_W:, :], NT)
        na, ra = _rms(a_ref[...])
        ng, rg = _rms(m_ref[...])
        _accum_rows(dga_ref, dan * na)
        _accum_rows(dgg_ref, dgn * ng)
        da_ref[...] = _rms_bwd(na, ra, ga_ref[...], dan)
        dm_ref[...] = _rms_bwd(ng, rg, gg_ref[...], dgn)

    sds = jax.ShapeDtypeStruct
    return pl.pallas_call(
        body, name="out_bwd", grid=(T // tm,),
        in_specs=[_rows(tm, D_MODEL), _resident((D_MODEL, D_MODEL)), _rows(tm, ATTN_W), _rows(tm, GMLP_W),
                  _resident((1, ATTN_W)), _resident((1, GMLP_W))],
        out_specs=[_rows(tm, ATTN_W), _rows(tm, GMLP_W), pl.BlockSpec((8, ATTN_W), lambda i: (0, 0)),
                   pl.BlockSpec((8, GMLP_W), lambda i: (0, 0))],
        out_shape=[sds((T, ATTN_W), F32), sds((T, GMLP_W), F32), sds((8, ATTN_W), F32), sds((8, GMLP_W), F32)],
        compiler_params=_params(("arbitrary",)),
    )(dh1b, w_out, attn, gm, ga, gg)


def _gmlp_bwd(u, z, dgm, ln_g, ln_b, sgu_w, sgu_bt):
    T = u.shape[0]
    tg = TM_GMLP
    nsteps = T // tg

    def body(u_ref, z_ref, d_ref, g_ref, b_ref, w_ref, sb_ref, dproj_hbm, dlg_ref, dlb_ref, dw_ref, dsb_ref,
             stage, sem):
        i = pl.program_id(0)
        slot = i % 2
        duz_ref = stage.at[slot]

        def to_dproj(step, buf):
            rows = pl.ds(pl.multiple_of(step * tg, tg), tg)
            return pltpu.make_async_copy(stage.at[buf], dproj_hbm.at[rows, pl.ds(3 * ATTN_W, 2 * GMLP_W)],
                                         sem.at[buf])

        @pl.when(i == 0)
        def _():
            for ref in (dlg_ref, dlb_ref, dw_ref, dsb_ref):
                ref[...] = jnp.zeros_like(ref)

        @pl.when(i >= 2)
        def _():
            to_dproj(i - 2, slot).wait()

        grp = lax.broadcasted_iota(jnp.int32, (tg, GMLP_W), 1) // HEAD_DIM
        lane = lax.broadcasted_iota(jnp.int32, (CHUNK, LANES), 1)
        causal, ws = _causal_ws(w_ref)
        lg = g_ref[...]
        uu, zz, dgm = u_ref[...], z_ref[...], d_ref[...]
        ug, tu, tz, xhat, rstd, zn16, mixed = _gmlp_core(uu, zz, lg, b_ref[...], ws, sb_ref, grp)
        dmx = dgm * ug
        duz_ref[:, 0:GMLP_W] = dgm * mixed * _gelu_grad(uu, tu)
        dmx16 = dmx.astype(BF16)
        dzn = []
        for ci in range(tg // CHUNK):
            rows = slice(ci * CHUNK, (ci + 1) * CHUNK)
            dmx_c, d = dmx16[rows, :], jnp.zeros((CHUNK, GMLP_W), F32)
            for g in range(N_GROUPS):
                mk = grp[:CHUNK] == g
                d = jnp.where(mk, _dot(ws[g], dmx_c, TN), d)
                dw_ref[g] += _dot(jnp.where(mk, dmx_c, jnp.zeros_like(dmx_c)), zn16[rows, :], NT)
            dzn.append(d)
        dzn = jnp.concatenate(dzn, axis=0)
        dsb = jnp.zeros((CHUNK, LANES), F32)
        for g in range(N_GROUPS):
            per_token = jnp.sum(jnp.where(grp == g, dmx, 0.0), axis=-1, keepdims=True)
            by_position = sum(per_token[ci * CHUNK:(ci + 1) * CHUNK] for ci in range(tg // CHUNK))
            dsb = jnp.where(lane == g, by_position, dsb)
        dsb_ref[...] += dsb
        _accum_rows(dlg_ref, dzn * xhat)
        _accum_rows(dlb_ref, dzn)
        dxh = dzn * lg
        dzg = rstd * (dxh - _group_mean(dxh, grp) - xhat * _group_mean(dxh * xhat, grp))
        duz_ref[:, GMLP_W:] = dzg * _gelu_grad(zz, tz)
        to_dproj(i, slot).start()

        @pl.when(i == nsteps - 1)
        def _():
            for g in range(N_GROUPS):
                dw_ref[g] = jnp.where(causal, dw_ref[g], 0.0)
            to_dproj(i, slot).wait()
            if nsteps >= 2:
                to_dproj(i - 1, 1 - slot).wait()

    sds = jax.ShapeDtypeStruct
    return pl.pallas_call(
        body, name="gmlp_bwd", grid=(nsteps,),
        in_specs=[_rows(tg, GMLP_W)] * 3 + [_resident((1, GMLP_W)), _resident((1, GMLP_W)),
                                              _resident((N_GROUPS, CHUNK, CHUNK)), _resident((CHUNK, N_GROUPS))],
        out_specs=[_HBM, pl.BlockSpec((8, GMLP_W), lambda i: (0, 0)),
                   pl.BlockSpec((8, GMLP_W), lambda i: (0, 0)),
                   pl.BlockSpec((N_GROUPS, CHUNK, CHUNK), lambda i: (0, 0, 0)),
                   pl.BlockSpec((CHUNK, LANES), lambda i: (0, 0))],
        out_shape=[sds((T, IN_W), F32), sds((8, GMLP_W), F32), sds((8, GMLP_W), F32),
                   sds((N_GROUPS, CHUNK, CHUNK), F32), sds((CHUNK, LANES), F32)],
        scratch_shapes=[pltpu.VMEM((2, tg, 2 * GMLP_W), F32), pltpu.SemaphoreType.DMA((2,))],
        compiler_params=_params(("arbitrary",)),
    )(u, z, dgm, ln_g, ln_b, sgu_w, sgu_bt)


def _attn_bwd(q, k, v, dattn, attn, lse, dproj, owner_grads=()):
    T = q.shape[0]
    nt = T // ATT_TILE
    ns = len(owner_grads)
    steps = (ATTN_W // LANES) * nt

    def body(sl_ref, q_hbm, k_hbm, v_hbm, do_hbm, o_hbm, lse_hbm, _, *rest):
        p_refs, rest = rest[:ns], rest[ns:]
        dq_hbm = dk_hbm = dv_hbm = rest[0]
        r_refs, rest = rest[1:1 + ns], rest[1 + ns:]
        qbuf, dobuf, obuf, lbuf, kbuf, vbuf, dqbuf, dkbuf, dvbuf, delta_s = rest[:10]
        sem_q, sem_do, sem_o, sem_l, sem_k, sem_v, sem_dq, sem_dk, sem_dv = rest[10:19]
        hp, t = pl.program_id(0), pl.program_id(1)
        step = hp * nt + t
        two, three = step % 2, step % 3
        before, after = (step + 2) % 3, (step + 1) % 3
        if ns:
            start, finish = _owner_exchange_phases(p_refs, r_refs, *rest[19:])
            pl.when(step == 0)(start)

        def fetch(hp_, t_, two_, three_):
            for hbm, buf, sem, slot in ((q_hbm, qbuf, sem_q, two_), (do_hbm, dobuf, sem_do, two_),
                                        (o_hbm, obuf, sem_o, two_), (lse_hbm, lbuf, sem_l, two_),
                                        (k_hbm, kbuf, sem_k, three_), (v_hbm, vbuf, sem_v, three_)):
                for cp in _tile_copies(hbm, buf.at[slot], sem.at[slot], hp_, t_):
                    cp.start()

        @pl.when(step == 0)
        def _():
            kbuf[2] = jnp.zeros((ATT_BLOCKS, CHUNK, LANES), F32)
            vbuf[2] = jnp.zeros((ATT_BLOCKS, CHUNK, LANES), F32)
            dkbuf[3] = jnp.zeros((ATT_BLOCKS, CHUNK, LANES), F32)
            dvbuf[3] = jnp.zeros((ATT_BLOCKS, CHUNK, LANES), F32)
            fetch(0, 0, 0, 0)

        @pl.when(step + 1 < steps)
        def _():
            fetch((step + 1) // nt, (step + 1) % nt, 1 - two, after)

        for buf, sem in ((qbuf, sem_q), (dobuf, sem_do), (obuf, sem_o), (lbuf, sem_l)):
            _wait_tile(buf.at[two], sem.at[two])
        _wait_tile(kbuf.at[three], sem_k.at[three])
        _wait_tile(vbuf.at[three], sem_v.at[three])

        @pl.when(step >= 2)
        def _():
            _wait_tile(dqbuf.at[two], sem_dq.at[two])

        @pl.when(step >= 3)
        def _():
            _wait_tile(dkbuf.at[three], sem_dk.at[three])
            _wait_tile(dvbuf.at[three], sem_dv.at[three])

        q_t, do_t, l_t, k_t, v_t = qbuf.at[two], dobuf.at[two], lbuf.at[two], kbuf.at[three], vbuf.at[three]
        k_b, v_b = kbuf.at[before], vbuf.at[before]
        dq_t, dk_t, dv_t = dqbuf.at[two], dkbuf.at[three], dvbuf.at[three]
        dk_b, dv_b = dkbuf.at[before], dvbuf.at[before]
        sink = jnp.where(t > 0, before, 3)
        dk_sink, dv_sink = dkbuf.at[sink], dvbuf.at[sink]
        head0 = lax.broadcasted_iota(jnp.int32, (CHUNK, LANES), 1) < HEAD_DIM
        dd = dobuf[two] * obuf[two]
        tile_head0 = lax.broadcasted_iota(jnp.int32, dd.shape, 2) < HEAD_DIM
        d0 = jnp.sum(jnp.where(tile_head0, dd, 0.0), axis=-1, keepdims=True)
        d1 = jnp.sum(jnp.where(tile_head0, 0.0, dd), axis=-1, keepdims=True)
        delta_s[...] = jnp.where(tile_head0, d0, d1)

        def column(xb):
            return jnp.concatenate([xb[:, 0:1], xb[:, HEAD_DIM:HEAD_DIM + 1]], axis=0)

        no_key_before = jnp.where(lax.broadcasted_iota(jnp.int32, (2 * CHUNK, 2 * CHUNK), 1) < CHUNK, NEG, 0.0)
        for d in DILATIONS:
            bias = _residue_bias(sl_ref, d)
            def scores(j, d=d, bias=bias):
                kcat = jnp.concatenate([_rm_block_before(k_t, k_b, d, j), _rm_block(k_t, d, j)], axis=0).astype(BF16)
                vcat = jnp.concatenate([_rm_block_before(v_t, v_b, d, j), _rm_block(v_t, d, j)], axis=0).astype(BF16)
                q2 = _stack_heads(_rm_block(q_t, d, j), head0)
                do2 = _stack_heads(_rm_block(do_t, d, j), head0)
                return (_dot(q2, kcat, NT), _dot(do2, vcat, NT), column(_rm_block(l_t, d, j)),
                        column(_rm_block(delta_s, d, j)), bias_first if _first_in_tile(d, j) else bias, kcat, q2, do2)

            bias_first = bias + jnp.where(t == 0, 1.0, 0.0) * no_key_before
            group = {}
            for j in range(ATT_BLOCKS):
                if j % SM_BLOCKS == 0:
                    group = {i: scores(i) for i in range(j, j + SM_BLOCKS)}
                    s_all, dp_all, lse_all, delta_all, bias_all = (
                        jnp.concatenate([g[i] for g in group.values()], axis=0) for i in range(5))
                    p_all = jnp.exp(s_all + bias_all - lse_all)
                    ds_all = (p_all * (dp_all - delta_all)).astype(BF16)
                    p_all = p_all.astype(BF16)
                at = slice((j % SM_BLOCKS) * 2 * CHUNK, (j % SM_BLOCKS + 1) * 2 * CHUNK)
                ds, p16 = ds_all[at, :], p_all[at, :]
                kcat, q2, do2 = group[j][5:]
                first = d == DILATIONS[0]
                _rm_add(dq_t, _residue_rows(d, j), _unstack_heads(_dot(ds, kcat), head0), first)
                ck = _dot(ds, q2, TN)
                cv = _dot(p16, do2, TN)
                _rm_add(dk_t, _residue_rows(d, j), ck[CHUNK:, :], first)
                _rm_add(dv_t, _residue_rows(d, j), cv[CHUNK:, :], first)
                if _first_in_tile(d, j):
                    rows = [(r, CHUNK - n, n) for r, _, n in _residue_rows(d, j)]
                    _rm_add(dk_sink, rows, ck[:CHUNK, :])
                    _rm_add(dv_sink, rows, cv[:CHUNK, :])
                else:
                    rows = [(r, lo - n, n) for r, lo, n in _residue_rows(d, j)]
                    _rm_add(dk_t, rows, ck[:CHUNK, :])
                    _rm_add(dv_t, rows, cv[:CHUNK, :])

        dqbuf[two] = dqbuf[two] * Q_SCALE
        for cp in _tile_copies(dq_hbm, dq_t, sem_dq.at[two], hp, t, to_hbm=True):
            cp.start()

        @pl.when(t > 0)
        def _():
            for cp in (_tile_copies(dk_hbm, dk_b, sem_dk.at[before], hp, t - 1, to_hbm=True, lane0=ATTN_W)
                       + _tile_copies(dv_hbm, dv_b, sem_dv.at[before], hp, t - 1, to_hbm=True, lane0=2 * ATTN_W)):
                cp.start()

        @pl.when(t == nt - 1)
        def _():
            for cp in (_tile_copies(dk_hbm, dk_t, sem_dk.at[three], hp, t, to_hbm=True, lane0=ATTN_W)
                       + _tile_copies(dv_hbm, dv_t, sem_dv.at[three], hp, t, to_hbm=True, lane0=2 * ATTN_W)):
                cp.start()

        @pl.when(step == steps - 1)
        def _():
            for slot in range(2):
                _wait_tile(dqbuf.at[slot], sem_dq.at[slot])
            for slot in range(3):
                _wait_tile(dkbuf.at[slot], sem_dk.at[slot])
                _wait_tile(dvbuf.at[slot], sem_dv.at[slot])

        if ns:
            pl.when(step == steps - 1)(finish)

    tile = lambda n: pltpu.VMEM((n, ATT_BLOCKS, CHUNK, LANES), F32)
    dma = lambda n: pltpu.SemaphoreType.DMA((n,))
    view = jax.ShapeDtypeStruct((T // ATT_BLOCKS, ATT_BLOCKS, ATTN_W), F32)
    outs = pl.pallas_call(
        body, name="attn_bwd", grid=(ATTN_W // LANES, nt),
        in_specs=[pl.BlockSpec((8, LANES), lambda c, t: (0, c))] + [_HBM] * (7 + ns),
        out_specs=[_HBM] * (1 + ns),
        out_shape=[jax.ShapeDtypeStruct((T // ATT_BLOCKS, ATT_BLOCKS, IN_W), F32)]
        + [jax.ShapeDtypeStruct(p.shape, p.dtype) for p in owner_grads],
        scratch_shapes=[tile(2), tile(2), tile(2), tile(2), tile(3), tile(3), tile(2), tile(4), tile(4),
                        pltpu.VMEM((ATT_BLOCKS, CHUNK, LANES), F32)]
        + [dma(2), dma(2), dma(2), dma(2), dma(3), dma(3), dma(2), dma(3), dma(3)]
        + (_owner_exchange_sems(ns) if ns else []),
        input_output_aliases={7: 0},
        compiler_params=_params(("arbitrary", "arbitrary")),
    )(_slope_table(), *[_residue_view(a) for a in (q, k, v, dattn, attn, lse, dproj)], *owner_grads)
    return outs[0].reshape(T, IN_W), tuple(outs[1:])


def _proj_bwd(dproj, w_in_t, x, g1, dh1, owner_grads=()):
    T = x.shape[0]
    tm = TM_PROJ
    ns = len(owner_grads)
    steps = T // tm

    def body(d_ref, w_ref, x_ref, g_ref, r_ref, *rest):
        p_refs, rest = rest[:ns], rest[ns:]
        dx_ref, dg_ref = rest[:2]
        r_refs, sems = rest[2:2 + ns], rest[2 + ns:]
        step = pl.program_id(0)
        if ns:
            start, finish = _owner_exchange_phases(p_refs, r_refs, *sems)
            pl.when(step == 0)(start)

        @pl.when(step == 0)
        def _():
            dg_ref[...] = jnp.zeros_like(dg_ref)

        dhn = _dot(d_ref[...].astype(BF16), w_ref[...])
        n1, r1 = _rms(x_ref[...])
        _accum_rows(dg_ref, dhn * n1)
        dx_ref[...] = r_ref[...] + _rms_bwd(n1, r1, g_ref[...], dhn)
        if ns:
            pl.when(step == steps - 1)(finish)

    outs = pl.pallas_call(
        body, name="proj_bwd", grid=(steps,),
        in_specs=[_rows(tm, IN_W), _resident((IN_W, D_MODEL)), _rows(tm, D_MODEL), _resident((1, D_MODEL)),
                  _rows(tm, D_MODEL)] + [_HBM] * ns,
        out_specs=[_rows(tm, D_MODEL), pl.BlockSpec((8, D_MODEL), lambda i: (0, 0))] + [_HBM] * ns,
        out_shape=[jax.ShapeDtypeStruct((T, D_MODEL), F32), jax.ShapeDtypeStruct((8, D_MODEL), F32)]
        + [jax.ShapeDtypeStruct(p.shape, p.dtype) for p in owner_grads],
        scratch_shapes=_owner_exchange_sems(ns) if ns else [],
        compiler_params=_params(("arbitrary",)),
    )(dproj, w_in_t, x, g1, dh1, *owner_grads)
    return outs[0], outs[1], tuple(outs[2:])


def _dw(a, b, name, tile, square_a=False, out_dtype=F32):
    T, ka = a.shape
    nb = b.shape[1]
    tka, tnb, tt = tile
    tt = min(tt, T)
    last = T // tt - 1

    def body(a_ref, b_ref, *refs):
        o_ref = refs[0]
        acc_ref = refs[1] if len(refs) > 1 else o_ref
        s = pl.program_id(2)

        @pl.when(s == 0)
        def _():
            acc_ref[...] = jnp.zeros_like(acc_ref)

        a_tile = a_ref[...]
        if square_a:
            a_tile = jnp.square(a_tile.astype(F32))
        acc_ref[...] += _dot(a_tile.astype(BF16), b_ref[...], TN)
        if acc_ref is not o_ref:
            @pl.when(s == last)
            def _():
                o_ref[...] = acc_ref[...].astype(out_dtype)

    return pl.pallas_call(
        body, name=name, grid=(ka // tka, nb // tnb, T // tt),
        in_specs=[pl.BlockSpec((tt, tka), lambda i, j, s: (s, i)), pl.BlockSpec((tt, tnb), lambda i, j, s: (s, j))],
        out_specs=pl.BlockSpec((tka, tnb), lambda i, j, s: (i, j)),
        out_shape=jax.ShapeDtypeStruct((ka, nb), out_dtype),
        scratch_shapes=[] if out_dtype == F32 else [pltpu.VMEM((tka, tnb), F32)],
        compiler_params=_params(("parallel", "parallel", "arbitrary")),
    )(a, b)


def _adamw_update(w, m, v, g):
    m2 = ADAM_B1 * m + (1.0 - ADAM_B1) * g
    v2 = ADAM_B2 * v + (1.0 - ADAM_B2) * jnp.square(g)
    m_hat = m2 / (1.0 - ADAM_B1 ** ADAM_STEP)
    v_hat = v2 / (1.0 - ADAM_B2 ** ADAM_STEP)
    return -ADAM_LR * (m_hat / (jnp.sqrt(v_hat) + ADAM_EPS) + ADAM_WD * w), m2, v2


def _adamw_tiny(ws, ms, vs, parts):
    n = len(ws)
    P = parts.shape[0]

    def body(*refs):
        w_refs, m_refs, v_refs, p_ref = refs[:n], refs[n:2 * n], refs[2 * n:3 * n], refs[3 * n]
        outs = refs[3 * n + 1:]

        def total(slot, rows):
            g = p_ref[0, 8 * slot:8 * slot + rows, :]
            for i in range(1, P):
                g = g + p_ref[i, 8 * slot:8 * slot + rows, :]
            return g

        for k in range(n):
            g = total(k, ws[k].shape[0])
            outs[4 * k][...] = g
            outs[4 * k + 1][...], outs[4 * k + 2][...], outs[4 * k + 3][...] = _adamw_update(
                w_refs[k][...], m_refs[k][...], v_refs[k][...], g)
        outs[4 * n][...] = total(n, 8)

    sds = jax.ShapeDtypeStruct
    return pl.pallas_call(
        body, name="adamw_tiny",
        out_shape=[sds(w.shape, F32) for w in ws for _ in range(4)] + [sds((8, LANES), F32)],
    )(*ws, *ms, *vs, parts)


def _adamw(w, m, v, parts, name, tr, transposed=False):
    R, C = w.shape
    P = parts.shape[0]

    def body(w_ref, m_ref, v_ref, p_ref, g_ref, d_ref, m2_ref, v2_ref):
        g = p_ref[0].astype(F32)
        for i in range(1, P):
            g = g + p_ref[i].astype(F32)
        if transposed:
            g = g.T
        g_ref[...] = g
        d_ref[...], m2_ref[...], v2_ref[...] = _adamw_update(w_ref[...], m_ref[...], v_ref[...], g)

    spec = _rows(tr, C)
    part_spec = (pl.BlockSpec((P, C, tr), lambda i: (0, 0, i)) if transposed
                 else pl.BlockSpec((P, tr, C), lambda i: (0, i, 0)))
    return pl.pallas_call(
        body, name=name, grid=(R // tr,),
        in_specs=[spec, spec, spec, part_spec],
        out_specs=[spec] * 4,
        out_shape=[jax.ShapeDtypeStruct((R, C), F32)] * 4,
        compiler_params=_params(("parallel",)),
    )(w, m, v, parts)


_HBM = pl.BlockSpec(memory_space=pltpu.HBM)


def _place():
    return lax.axis_index("x"), lax.axis_index("y"), lax.axis_index("c")


def _gathered_shape(shard):
    return jax.ShapeDtypeStruct((N_DEV,) + shard.shape, shard.dtype)


def _gather_sems(n):
    return [pltpu.SemaphoreType.DMA((7, n)), pltpu.SemaphoreType.DMA((7, n)), pltpu.SemaphoreType.DMA((n,))]


def _gather_phases(x_refs, out_refs, send_sems, recv_sems, local_sems):
    x, y, c = _place()
    me, sibling = (x, y, c), (x, y, 1 - c)
    chips = [(1 - x, y), (x, 1 - y), (1 - x, 1 - y)]
    arrays = range(len(x_refs))

    def slot(i, px, py, pc):
        return out_refs[i].at[4 * px + 2 * py + pc]

    def copy(i, k, block, to, own=False):
        return pltpu.make_async_remote_copy(
            src_ref=x_refs[i] if own else slot(i, *block), dst_ref=slot(i, *block),
            send_sem=send_sems.at[k, i], recv_sem=recv_sems.at[k, i], device_id=to, device_id_type=MESH)

    def mine(i):
        return pltpu.make_async_copy(x_refs[i], slot(i, *me), local_sems.at[i])

    def start():
        for i in arrays:
            mine(i).start()
            copy(i, 0, me, sibling, own=True).start()
            for j, chip in enumerate(chips):
                copy(i, 1 + j, me, (*chip, c), own=True).start()

    def forward():
        for i in arrays:
            for j, chip in enumerate(chips):
                copy(i, 1 + j, (*chip, c), me).wait_recv()
                copy(i, 4 + j, (*chip, c), sibling).start()

    def finish():
        for i in arrays:
            copy(i, 0, sibling, me).wait_recv()
            copy(i, 0, me, sibling, own=True).wait_send()
            for j, chip in enumerate(chips):
                copy(i, 4 + j, (*chip, 1 - c), me).wait_recv()
                copy(i, 1 + j, me, (*chip, c), own=True).wait_send()
                copy(i, 4 + j, (*chip, c), sibling).wait_send()
            mine(i).wait()

    return start, forward, finish


def _all_gather(shards, name):
    n = len(shards)

    def body(*refs):
        start, forward, finish = _gather_phases(refs[:n], refs[n:2 * n], *refs[2 * n:])
        start()
        forward()
        finish()

    return pl.pallas_call(
        body, name=name,
        out_shape=[_gathered_shape(s) for s in shards],
        in_specs=[_HBM] * n, out_specs=[_HBM] * n,
        scratch_shapes=_gather_sems(n),
    )(*shards)


def _owner_exchange_sems(n):
    return [pltpu.SemaphoreType.DMA((7, n)), pltpu.SemaphoreType.DMA((7, n)), pltpu.SemaphoreType.DMA((n,))]


def _owner_exchange_phases(g_refs, r_refs, send_sems, recv_sems, local_sems):
    x, y, c = _place()
    me = 4 * x + 2 * y + c
    flip = lambda v, bit: 1 - v if bit else v
    peers = [(flip(x, k & 4), flip(y, k & 2), flip(c, k & 1)) for k in range(1, N_DEV)]
    arrays = range(len(g_refs))

    def mine(i):
        return pltpu.make_async_copy(g_refs[i].at[me], r_refs[i].at[me], local_sems.at[i])

    def copy(i, k, src_slot, dst_slot):
        return pltpu.make_async_remote_copy(
            src_ref=g_refs[i].at[src_slot], dst_ref=r_refs[i].at[dst_slot],
            send_sem=send_sems.at[k, i], recv_sem=recv_sems.at[k, i], device_id=peers[k], device_id_type=MESH)

    def start():
        for i in arrays:
            mine(i).start()
            for k, (px, py, pc) in enumerate(peers):
                copy(i, k, 4 * px + 2 * py + pc, me).start()

    def finish():
        for i in arrays:
            for k, (px, py, pc) in enumerate(peers):
                copy(i, k, me, 4 * px + 2 * py + pc).wait_recv()
                copy(i, k, 4 * px + 2 * py + pc, me).wait_send()
            mine(i).wait()

    return start, finish


def _local_step(x, tgt, small, w_in_t, rest, exchange=False):
    g1, g2, gf = small["norm1_g"], small["norm2_g"], small["final_norm_g"].reshape(1, D_MODEL)
    ga, gg = small["attn_out_g"], small["gmlp_out_g"]
    ln_g = small["sgu_ln_g"].reshape(1, GMLP_W)
    ln_b = small["sgu_ln_b"].reshape(1, GMLP_W)
    sgu_w = small["sgu_w"][0]
    sgu_bt = small["sgu_b"][0].T

    hn1, q, k, v, u, z = _proj_fwd(x, g1, w_in_t)
    attn, lse, gathered = _attn_fwd(q, k, v, shards=rest if exchange else ())
    w_out, w_ff1_t, w_ff2 = [g.reshape(-1, D_MODEL) for g in gathered] if exchange else rest
    gm = _gmlp_fwd(u, z, ln_g, ln_b, sgu_w, sgu_bt)
    mixed, h1, hn2 = _out_fwd(attn, gm, ga, gg, w_out, x, g2)
    relu, dh2f, dh2b, loss8, dgf8 = _ffn_fwd(hn2, h1, w_ff1_t, w_ff2, gf, tgt)

    da, dh1f, dh1b, dg2 = _ffn_bwd(dh2b, dh2f, relu, h1, g2, w_ff2, w_ff1_t)
    wire = BF16 if exchange else F32
    dw_ff2 = _dw(relu, dh2b, "dw_ff2", DW_TILE, square_a=True, out_dtype=wire)
    dw_ff1_t = _dw(da, hn2, "dw_ff1", DW_TILE, out_dtype=wire)
    dattn, dgm, dga, dgg = _out_bwd(dh1b, w_out, attn, gm, ga, gg)
    dw_out = _dw(mixed, dh1b, "dw_out", DW_TILE, out_dtype=wire)
    early = [dw_out, dw_ff1_t, dw_ff2]
    if exchange:
        early = [g.reshape(N_DEV, -1, D_MODEL) for g in early]
    dproj, dlg, dlb, dsw, dsb = _gmlp_bwd(u, z, dgm, ln_g, ln_b, sgu_w, sgu_bt)
    dproj, arrived = _attn_bwd(q, k, v, dattn, attn, lse, dproj, owner_grads=early if exchange else ())
    dw_in_t = _dw(dproj, hn1, "dw_in", DW_TILE_IN, out_dtype=wire)
    late = (dw_in_t.reshape(N_DEV, -1, D_MODEL),) if exchange else ()
    dx, dg1, late = _proj_bwd(dproj, w_in_t, x, g1, dh1f, owner_grads=late)
    if exchange:
        dw_in_t, early = late[0], arrived

    small_grads = dict(
        norm1_g=dg1[0], sgu_ln_g=dlg[0], sgu_ln_b=dlb[0], sgu_w=dsw, sgu_b=dsb[:, :N_GROUPS].T,
        attn_out_g=dga[0], gmlp_out_g=dgg[0], norm2_g=dg2[0], final_norm_g=dgf8[0])
    return loss8[0, 0], dx, (dw_in_t, *early), small_grads


SMALL_NAMES = ("norm1_g", "sgu_ln_g", "sgu_ln_b", "sgu_w", "sgu_b", "attn_out_g", "gmlp_out_g", "norm2_g",
               "final_norm_g")
WEIGHT_ORDER = ("norm1_g", "w_in", "sgu_ln_g", "sgu_ln_b", "sgu_w", "sgu_b", "attn_out_g", "gmlp_out_g", "w_out",
                "norm2_g", "w_ff1", "w_ff2", "final_norm_g")


TINY_NAMES = tuple(n for n in SMALL_NAMES if n != "sgu_w")


def _as_rows(a):
    return a.reshape(-1, LANES)


def _pack_tiny_grads(d, loss):
    slots = [jnp.pad(_as_rows(d[n]), ((0, 8 - d[n].size // LANES), (0, 0))) for n in TINY_NAMES]
    return jnp.concatenate(slots + [jnp.full((8, LANES), loss, F32)], axis=0)


def kernel(x, norm1_g, w_in, sgu_ln_g, sgu_ln_b, sgu_w, sgu_b, attn_out_g, gmlp_out_g, w_out, norm2_g, w_ff1, w_ff2, final_norm_g, loss_target, m_norm1_g, m_w_in, m_sgu_ln_g, m_sgu_ln_b, m_sgu_w, m_sgu_b, m_attn_out_g, m_gmlp_out_g, m_w_out, m_norm2_g, m_w_ff1, m_w_ff2, m_final_norm_g, v_norm1_g, v_w_in, v_sgu_ln_g, v_sgu_ln_b, v_sgu_w, v_sgu_b, v_attn_out_g, v_gmlp_out_g, v_w_out, v_norm2_g, v_w_ff1, v_w_ff2, v_final_norm_g):
    w = dict(norm1_g=norm1_g, w_in=w_in, sgu_ln_g=sgu_ln_g, sgu_ln_b=sgu_ln_b, sgu_w=sgu_w, sgu_b=sgu_b,
             attn_out_g=attn_out_g, gmlp_out_g=gmlp_out_g, w_out=w_out, norm2_g=norm2_g, w_ff1=w_ff1, w_ff2=w_ff2,
             final_norm_g=final_norm_g)
    m = dict(norm1_g=m_norm1_g, w_in=m_w_in, sgu_ln_g=m_sgu_ln_g, sgu_ln_b=m_sgu_ln_b, sgu_w=m_sgu_w, sgu_b=m_sgu_b,
             attn_out_g=m_attn_out_g, gmlp_out_g=m_gmlp_out_g, w_out=m_w_out, norm2_g=m_norm2_g, w_ff1=m_w_ff1,
             w_ff2=m_w_ff2, final_norm_g=m_final_norm_g)
    v = dict(norm1_g=v_norm1_g, w_in=v_w_in, sgu_ln_g=v_sgu_ln_g, sgu_ln_b=v_sgu_ln_b, sgu_w=v_sgu_w, sgu_b=v_sgu_b,
             attn_out_g=v_attn_out_g, gmlp_out_g=v_gmlp_out_g, w_out=v_w_out, norm2_g=v_norm2_g, w_ff1=v_w_ff1,
             w_ff2=v_w_ff2, final_norm_g=v_final_norm_g)
    big = ("w_in", "w_out", "w_ff1", "w_ff2")

    w_in_t, = _all_gather([w_in[0].T.astype(BF16)], "w_in_all_gather")
    rest = (w_out[0].astype(BF16), w_ff1[0].T.astype(BF16), w_ff2[0].astype(BF16))
    loss, dx, parts, small_grads = _local_step(x[0], loss_target[0], {n: w[n] for n in SMALL_NAMES},
                                               w_in_t.reshape(IN_W, D_MODEL), rest, exchange=True)

    new = {}
    for n, p, transposed, tr in zip(big, parts, (True, False, True, False), (128, 128, 128, 256)):
        new[n] = [a[None] for a in _adamw(w[n][0], m[n][0], v[n][0], p, "adamw_" + n, tr, transposed)]

    tiny_parts, sgu_parts = _all_gather(
        [_pack_tiny_grads(small_grads, loss), _as_rows(small_grads["sgu_w"]).astype(BF16)], "small_grad_all_gather")
    tiny = _adamw_tiny(*[[_as_rows(src[n]) for n in TINY_NAMES] for src in (w, m, v)], tiny_parts)
    sgu = _adamw(_as_rows(sgu_w), _as_rows(m_sgu_w), _as_rows(v_sgu_w), sgu_parts, "adamw_sgu_w", 512)
    loss = tiny[-1][0, 0]

    outs = []
    for i in range(4):
        d = {n: new[n][i] for n in big}
        d.update({n: tiny[4 * k + i].reshape(w[n].shape) for k, n in enumerate(TINY_NAMES)})
        d["sgu_w"] = sgu[i].reshape(sgu_w.shape)
        outs.extend(d[n] for n in WEIGHT_ORDER)
    return (loss, dx[None], *outs)
```

```python
import math

import numpy as np
import jax
import jax.numpy as jnp
from jax import lax
from jax.experimental import pallas as pl
from jax.experimental.pallas import tpu as pltpu

F32 = jnp.float32
BF16 = jnp.bfloat16

D_MODEL = 1024
HEAD_DIM = 64
N_HEADS = 12
ATTN_W = N_HEADS * HEAD_DIM
N_GROUPS = 4
GMLP_W = N_GROUPS * HEAD_DIM
IN_W = 3 * ATTN_W + 2 * GMLP_W
D_FF = 4 * D_MODEL
CHUNK = 128
DILATIONS = (1, 4, 16)
EPS = 1e-6
Q_SCALE = HEAD_DIM ** -0.5
NEG = -1e30

ADAM_LR, ADAM_B1, ADAM_B2, ADAM_EPS, ADAM_WD, ADAM_STEP = 0.001, 0.9, 0.999, 1e-08, 0.01, 10

N_DEV = 8
LANES = 128
VMEM_LIMIT = 56 << 20

TM_PROJ = 512
TM_FFN = 512
FF_CHUNK = 512
TM_GMLP = 1024
DW_TILE = (512, 1024, 8192)
DW_TILE_IN = (IN_W // 2, 1024, 2048)

MESH = pl.DeviceIdType.MESH


def _alibi_slopes(n):
    def pow2(m):
        start = 2.0 ** (-8.0 / m)
        return [start ** (i + 1) for i in range(m)]
    c = 2 ** int(math.floor(math.log2(n)))
    s = pow2(n) if c == n else pow2(c) + pow2(2 * c)[0::2][: n - c]
    return np.asarray(s, dtype=np.float32)


SLOPES = _alibi_slopes(N_HEADS)


def _params(sem=None):
    kw = dict(vmem_limit_bytes=VMEM_LIMIT)
    if sem is not None:
        kw["dimension_semantics"] = sem
    return pltpu.CompilerParams(**kw)


def _rows(tm, n):
    return pl.BlockSpec((tm, n), lambda i: (i, 0))


def _resident(shape):
    return pl.BlockSpec(shape, lambda *_: (0,) * len(shape), pipeline_mode=pl.Buffered(1))


def _rms(x):
    r = lax.rsqrt(jnp.mean(x * x, axis=-1, keepdims=True) + EPS)
    return x * r, r


def _rms_bwd(n, r, g, dy):
    dn = dy * g
    return r * (dn - n * jnp.mean(dn * n, axis=-1, keepdims=True))


def _accum_rows(acc_ref, v):
    acc_ref[...] += jnp.broadcast_to(jnp.sum(v, axis=0, keepdims=True), acc_ref.shape)


_G0 = math.sqrt(2.0 / math.pi)
_G1 = 0.044715


def _gelu(x):
    t = jnp.tanh(_G0 * (x + _G1 * (x * x * x)))
    return x * (0.5 * (1.0 + t)), t


def _gelu_grad(x, t):
    return 0.5 * (1.0 + t) + 0.5 * x * (1.0 - t * t) * (_G0 * (1.0 + 3.0 * _G1 * x * x))


NT = (((1,), (1,)), ((), ()))
TN = (((0,), (0,)), ((), ()))


def _dot(a, b, dims=None):
    if dims is None:
        return jnp.dot(a, b, preferred_element_type=F32)
    return lax.dot_general(a, b, dims, preferred_element_type=F32)


def _proj_fwd(x, g1, w_in_t):
    T = x.shape[0]
    tm = TM_PROJ

    def body(x_ref, g_ref, w_ref, hn_ref, q_ref, k_ref, v_ref, u_ref, z_ref):
        n, _ = _rms(x_ref[...])
        hn = (n * g_ref[...]).astype(BF16)
        hn_ref[...] = hn
        a = ATTN_W
        q_ref[...] = _dot(hn, w_ref[0:a, :], NT) * Q_SCALE
        k_ref[...] = _dot(hn, w_ref[a:2 * a, :], NT)
        v_ref[...] = _dot(hn, w_ref[2 * a:3 * a, :], NT)
        u_ref[...] = _dot(hn, w_ref[3 * a:3 * a + GMLP_W, :], NT)
        z_ref[...] = _dot(hn, w_ref[3 * a + GMLP_W:, :], NT)

    sds = jax.ShapeDtypeStruct
    return pl.pallas_call(
        body, name="proj_fwd", grid=(T // tm,),
        in_specs=[_rows(tm, D_MODEL), _resident((1, D_MODEL)), _resident((IN_W, D_MODEL))],
        out_specs=[_rows(tm, D_MODEL), _rows(tm, ATTN_W), _rows(tm, ATTN_W), _rows(tm, ATTN_W),
                   _rows(tm, GMLP_W), _rows(tm, GMLP_W)],
        out_shape=[sds((T, D_MODEL), BF16), sds((T, ATTN_W), F32), sds((T, ATTN_W), F32),
                   sds((T, ATTN_W), F32), sds((T, GMLP_W), F32), sds((T, GMLP_W), F32)],
        compiler_params=_params(("parallel",)),
    )(x, g1, w_in_t)


ATT_TILE = 2048
ATT_BLOCKS = ATT_TILE // CHUNK
SM_BLOCKS = 4


def _slope_table():
    row = np.repeat(SLOPES, HEAD_DIM)
    return jnp.asarray(np.broadcast_to(row[None], (8, ATTN_W)), F32)


def _residue_view(a):
    return a.reshape(a.shape[0] // ATT_BLOCKS, ATT_BLOCKS, a.shape[1])


def _tile_copies(hbm, buf, sem, hp, t, to_hbm=False, lane0=0):
    rows = pl.ds(pl.multiple_of(t * CHUNK, CHUNK), CHUNK)
    lanes = pl.ds(pl.multiple_of(lane0 + hp * LANES, LANES), LANES)
    pairs = [(hbm.at[rows, r, lanes], buf.at[r]) for r in range(ATT_BLOCKS)]
    return [pltpu.make_async_copy(v, h, sem) if to_hbm else pltpu.make_async_copy(h, v, sem) for h, v in pairs]


def _wait_tile(buf, sem):
    pltpu.make_async_copy(buf, buf, sem).wait()


def _residue_rows(d, j):
    if d == 16:
        return [(j, 0, CHUNK)]
    if d == 4:
        return [(j % 4 + 4 * m, 32 * (j // 4), 32) for m in range(4)]
    return [(r, 8 * j, 8) for r in range(ATT_BLOCKS)]


def _block_order(p, d):
    if d == 16:
        return p
    if d == 4:
        return 4 * (p & 31) + (p >> 5)
    return 16 * (p & 7) + (p >> 3)


def _first_in_tile(d, j):
    return _residue_rows(d, j)[0][1] == 0


def _rm_block(buf, d, j):
    return jnp.concatenate([buf[r, lo:lo + n, :] for r, lo, n in _residue_rows(d, j)], axis=0)


def _rm_block_before(buf, buf_before, d, j):
    if _first_in_tile(d, j):
        return jnp.concatenate([buf_before[r, CHUNK - n:CHUNK, :] for r, _, n in _residue_rows(d, j)], axis=0)
    return jnp.concatenate([buf[r, lo - n:lo, :] for r, lo, n in _residue_rows(d, j)], axis=0)


def _rm_store(buf, d, j, val):
    at = 0
    for r, lo, n in _residue_rows(d, j):
        buf[r, lo:lo + n, :] = val[at:at + n, :]
        at += n


def _rm_add(buf, rows, val, first=False):
    at = 0
    for r, lo, n in rows:
        if first:
            buf[r, lo:lo + n, :] = val[at:at + n, :]
        else:
            buf[r, lo:lo + n, :] += val[at:at + n, :]
        at += n


def _residue_bias(sl_ref, d):
    shape = (2 * CHUNK, 2 * CHUNK)
    row = lax.broadcasted_iota(jnp.int32, shape, 0)
    col = lax.broadcasted_iota(jnp.int32, shape, 1)
    steps = _block_order(row & (CHUNK - 1), d) + CHUNK - (_block_order(col & (CHUNK - 1), d) + (col & CHUNK))
    band = (steps >= 0) & (steps <= CHUNK)
    sl = sl_ref[0:1, :]
    upper = lax.broadcasted_iota(jnp.int32, (2 * CHUNK, 1), 0) < CHUNK
    slope2 = jnp.where(upper, sl[:, 0:1], sl[:, HEAD_DIM:HEAD_DIM + 1])
    return jnp.where(band, -(float(d) * slope2 * steps.astype(F32)), NEG)


def _stack_heads(xb, head0):
    zero = jnp.zeros_like(xb)
    return jnp.concatenate([jnp.where(head0, xb, zero), jnp.where(head0, zero, xb)], axis=0).astype(BF16)


def _unstack_heads(x2, head0):
    return jnp.where(head0, x2[:CHUNK, :], x2[CHUNK:, :])


def _attn_fwd(q, k, v, shards=()):
    T = q.shape[0]
    nt = T // ATT_TILE
    ns = len(shards)
    steps = (ATTN_W // LANES) * nt

    def body(sl_ref, q_hbm, k_hbm, v_hbm, *rest):
        x_refs, rest = rest[:ns], rest[ns:]
        attn_hbm, lse_hbm = rest[:2]
        g_refs, rest = rest[2:2 + ns], rest[2 + ns:]
        qbuf, kbuf, vbuf, obuf, lbuf = rest[:5]
        o_acc, l_acc = rest[5:8], rest[8:11]
        sem_q, sem_k, sem_v, sem_o, sem_l = rest[11:16]
        hp, t = pl.program_id(0), pl.program_id(1)
        step = hp * nt + t
        two, three = step % 2, step % 3
        before, after = (step + 2) % 3, (step + 1) % 3
        if ns:
            start, forward, finish = _gather_phases(x_refs, g_refs, *rest[16:])
            pl.when(step == 0)(start)
            pl.when(step == (2 * steps) // 3)(forward)

        def fetch(hp_, t_, two_, three_):
            for cp in (_tile_copies(q_hbm, qbuf.at[two_], sem_q.at[two_], hp_, t_)
                       + _tile_copies(k_hbm, kbuf.at[three_], sem_k.at[three_], hp_, t_)
                       + _tile_copies(v_hbm, vbuf.at[three_], sem_v.at[three_], hp_, t_)):
                cp.start()

        @pl.when(step == 0)
        def _():
            kbuf[2] = jnp.zeros((ATT_BLOCKS, CHUNK, LANES), F32)
            vbuf[2] = jnp.zeros((ATT_BLOCKS, CHUNK, LANES), F32)
            fetch(0, 0, 0, 0)

        @pl.when(step + 1 < steps)
        def _():
            fetch((step + 1) // nt, (step + 1) % nt, 1 - two, after)

        _wait_tile(qbuf.at[two], sem_q.at[two])
        _wait_tile(kbuf.at[three], sem_k.at[three])
        _wait_tile(vbuf.at[three], sem_v.at[three])

        @pl.when(step >= 2)
        def _():
            _wait_tile(obuf.at[two], sem_o.at[two])
            _wait_tile(lbuf.at[two], sem_l.at[two])

        q_t, k_t, v_t = qbuf.at[two], kbuf.at[three], vbuf.at[three]
        k_b, v_b = kbuf.at[before], vbuf.at[before]
        head0 = lax.broadcasted_iota(jnp.int32, (CHUNK, LANES), 1) < HEAD_DIM
        no_key_before = jnp.where(lax.broadcasted_iota(jnp.int32, (2 * CHUNK, 2 * CHUNK), 1) < CHUNK, NEG, 0.0)
        for pi, d in enumerate(DILATIONS):
            bias = _residue_bias(sl_ref, d)

            def scores(j, d=d, bias=bias):
                kcat = jnp.concatenate([_rm_block_before(k_t, k_b, d, j), _rm_block(k_t, d, j)], axis=0).astype(BF16)
                vcat = jnp.concatenate([_rm_block_before(v_t, v_b, d, j), _rm_block(v_t, d, j)], axis=0).astype(BF16)
                s = _dot(_stack_heads(_rm_block(q_t, d, j), head0), kcat, NT)
                return s, vcat, bias_first if _first_in_tile(d, j) else bias

            bias_first = bias + jnp.where(t == 0, 1.0, 0.0) * no_key_before
            for j0 in range(0, ATT_BLOCKS, SM_BLOCKS):
                group = [scores(j) for j in range(j0, j0 + SM_BLOCKS)]
                s = jnp.concatenate([g[0] for g in group], axis=0) + jnp.concatenate([g[2] for g in group], axis=0)
                m = jnp.max(s, axis=-1, keepdims=True)
                p = jnp.exp(s - m)
                l = jnp.sum(p, axis=-1, keepdims=True)
                p = p.astype(BF16)
                block = lambda a, i: a[i * 2 * CHUNK:(i + 1) * 2 * CHUNK, :]
                o = jnp.concatenate([_dot(block(p, i), g[1]) for i, g in enumerate(group)], axis=0) * (1.0 / l)
                lse = jnp.broadcast_to(m + jnp.log(l), o.shape)
                for i in range(SM_BLOCKS):
                    _rm_store(o_acc[pi], d, j0 + i, _unstack_heads(block(o, i), head0))
                    _rm_store(l_acc[pi], d, j0 + i, _unstack_heads(block(lse, i), head0))

        for r in range(ATT_BLOCKS):
            a, b, c = l_acc[0][r], l_acc[1][r], l_acc[2][r]
            m = jnp.maximum(jnp.maximum(a, b), c)
            ea, eb, ec = jnp.exp(a - m), jnp.exp(b - m), jnp.exp(c - m)
            tot = ea + eb + ec
            obuf[two, r] = (ea * o_acc[0][r] + eb * o_acc[1][r] + ec * o_acc[2][r]) / tot
            lbuf[two, r] = m + jnp.log(tot)

        for cp in (_tile_copies(attn_hbm, obuf.at[two], sem_o.at[two], hp, t, to_hbm=True)
                   + _tile_copies(lse_hbm, lbuf.at[two], sem_l.at[two], hp, t, to_hbm=True)):
            cp.start()

        @pl.when(step == steps - 1)
        def _():
            for slot in (two, 1 - two)[:min(steps, 2)]:
                _wait_tile(obuf.at[slot], sem_o.at[slot])
                _wait_tile(lbuf.at[slot], sem_l.at[slot])

        if ns:
            pl.when(step == steps - 1)(finish)

    tile = lambda n: pltpu.VMEM((n, ATT_BLOCKS, CHUNK, LANES), F32)
    dma = lambda n: pltpu.SemaphoreType.DMA((n,))
    view = jax.ShapeDtypeStruct((T // ATT_BLOCKS, ATT_BLOCKS, ATTN_W), F32)
    outs = pl.pallas_call(
        body, name="attn_fwd", grid=(ATTN_W // LANES, nt),
        in_specs=[pl.BlockSpec((8, LANES), lambda c, t: (0, c))] + [_HBM] * (3 + ns),
        out_specs=[_HBM] * (2 + ns),
        out_shape=[view, view] + [_gathered_shape(s) for s in shards],
        scratch_shapes=[tile(2), tile(3), tile(3), tile(2), tile(2)] + [pltpu.VMEM((ATT_BLOCKS, CHUNK, LANES), F32)] * 6
        + [dma(2), dma(3), dma(3), dma(2), dma(2)] + (_gather_sems(ns) if ns else []),
        compiler_params=_params(("arbitrary", "arbitrary")),
    )(_slope_table(), _residue_view(q), _residue_view(k), _residue_view(v), *shards)
    return outs[0].reshape(T, ATTN_W), outs[1].reshape(T, ATTN_W), tuple(outs[2:])


def _group_mean(v, grp):
    halves = []
    for h in range(GMLP_W // LANES):
        x = v[:, h * LANES:(h + 1) * LANES]
        low = grp[:, h * LANES:(h + 1) * LANES] == 2 * h
        a = jnp.sum(jnp.where(low, x, 0.0), axis=-1, keepdims=True)
        b = jnp.sum(jnp.where(low, 0.0, x), axis=-1, keepdims=True)
        halves.append(jnp.where(low, a, b) * (1.0 / HEAD_DIM))
    return jnp.concatenate(halves, axis=1)


def _gmlp_core(uu, zz, lg, lb, ws, sb_ref, grp):
    ug, tu = _gelu(uu)
    zg, tz = _gelu(zz)
    zc = zg - _group_mean(zg, grp)
    rstd = lax.rsqrt(_group_mean(zc * zc, grp) + EPS)
    xhat = zc * rstd
    zn16 = (xhat * lg + lb).astype(BF16)
    low = grp[:CHUNK, :LANES] == 0
    mixed = []
    for ci in range(uu.shape[0] // CHUNK):
        rows = slice(ci * CHUNK, (ci + 1) * CHUNK)
        halves = []
        for h in range(GMLP_W // LANES):
            zh = zn16[rows, h * LANES:(h + 1) * LANES]
            halves.append(jnp.where(low, _dot(ws[2 * h], zh) + sb_ref[:, 2 * h:2 * h + 1],
                                    _dot(ws[2 * h + 1], zh) + sb_ref[:, 2 * h + 1:2 * h + 2]))
        mixed.append(jnp.concatenate(halves, axis=1))
    return ug, tu, tz, xhat, rstd, zn16, jnp.concatenate(mixed, axis=0)


def _causal_ws(w_ref):
    ti = lax.broadcasted_iota(jnp.int32, (CHUNK, CHUNK), 0)
    si = lax.broadcasted_iota(jnp.int32, (CHUNK, CHUNK), 1)
    causal = si <= ti
    return causal, [jnp.where(causal, w_ref[g], 0.0).astype(BF16) for g in range(N_GROUPS)]


def _gmlp_fwd(u, z, ln_g, ln_b, sgu_w, sgu_bt):
    T = u.shape[0]
    tg = TM_GMLP

    def body(u_ref, z_ref, g_ref, b_ref, w_ref, sb_ref, out_ref):
        grp = lax.broadcasted_iota(jnp.int32, (tg, GMLP_W), 1) // HEAD_DIM
        _, ws = _causal_ws(w_ref)
        ug, _, _, _, _, _, mixed = _gmlp_core(u_ref[...], z_ref[...], g_ref[...], b_ref[...], ws, sb_ref, grp)
        out_ref[...] = ug * mixed

    return pl.pallas_call(
        body, name="gmlp_fwd", grid=(T // tg,),
        in_specs=[_rows(tg, GMLP_W), _rows(tg, GMLP_W), _resident((1, GMLP_W)), _resident((1, GMLP_W)),
                  _resident((N_GROUPS, CHUNK, CHUNK)), _resident((CHUNK, N_GROUPS))],
        out_specs=_rows(tg, GMLP_W),
        out_shape=jax.ShapeDtypeStruct((T, GMLP_W), F32),
        compiler_params=_params(("parallel",)),
    )(u, z, ln_g, ln_b, sgu_w, sgu_bt)


def _out_fwd(attn, gm, ga, gg, w_out, x, g2):
    T = x.shape[0]
    tm = TM_PROJ

    def body(a_ref, m_ref, ga_ref, gg_ref, w_ref, x_ref, g2_ref, mix_ref, h1_ref, hn2_ref):
        an, _ = _rms(a_ref[...])
        gn, _ = _rms(m_ref[...])
        an = (an * ga_ref[...]).astype(BF16)
        gn = (gn * gg_ref[...]).astype(BF16)
        mix_ref[:, 0:ATTN_W] = an
        mix_ref[:, ATTN_W:] = gn
        h1 = x_ref[...] + _dot(an, w_ref[0:ATTN_W, :]) + _dot(gn, w_ref[ATTN_W:, :])
        h1_ref[...] = h1
        n2, _ = _rms(h1)
        hn2_ref[...] = (n2 * g2_ref[...]).astype(BF16)

    sds = jax.ShapeDtypeStruct
    return pl.pallas_call(
        body, name="out_fwd", grid=(T // tm,),
        in_specs=[_rows(tm, ATTN_W), _rows(tm, GMLP_W), _resident((1, ATTN_W)), _resident((1, GMLP_W)),
                  _resident((D_MODEL, D_MODEL)), _rows(tm, D_MODEL), _resident((1, D_MODEL))],
        out_specs=[_rows(tm, D_MODEL)] * 3,
        out_shape=[sds((T, D_MODEL), BF16), sds((T, D_MODEL), F32), sds((T, D_MODEL), BF16)],
        compiler_params=_params(("parallel",)),
    )(attn, gm, ga, gg, w_out, x, g2)


def _ffn_fwd(hn2, h1, w1t, w2, gf, tgt):
    T = h1.shape[0]
    tm = TM_FFN

    def body(hn_ref, h1_ref, w1_ref, w2_ref, gf_ref, t_ref, r_ref, dhf_ref, dhb_ref, loss_ref, dgf_ref):
        i = pl.program_id(0)

        @pl.when(i == 0)
        def _():
            loss_ref[...] = jnp.zeros_like(loss_ref)
            dgf_ref[...] = jnp.zeros_like(dgf_ref)

        hn = hn_ref[...]
        acc = h1_ref[...]
        for j in range(D_FF // FF_CHUNK):
            cols = slice(j * FF_CHUNK, (j + 1) * FF_CHUNK)
            r = jnp.maximum(_dot(hn, w1_ref[cols, :], NT), 0.0)
            r_ref[:, cols] = r.astype(BF16)
            act = jnp.square(r).astype(BF16)
            acc = acc + _dot(act, w2_ref[cols, :])
        n3, r3 = _rms(acc)
        gf_row = gf_ref[...]
        e = n3 * gf_row - t_ref[...]
        loss_ref[...] += 0.5 * jnp.sum(jnp.mean(e * e, axis=-1, keepdims=True))
        dy = e * (1.0 / D_MODEL)
        _accum_rows(dgf_ref, dy * n3)
        dh2 = _rms_bwd(n3, r3, gf_row, dy)
        dhf_ref[...] = dh2
        dhb_ref[...] = dh2.astype(BF16)

    sds = jax.ShapeDtypeStruct
    acc_spec = lambda n: pl.BlockSpec((8, n), lambda i: (0, 0))
    return pl.pallas_call(
        body, name="ffn_fwd", grid=(T // tm,),
        in_specs=[_rows(tm, D_MODEL), _rows(tm, D_MODEL), _resident((D_FF, D_MODEL)), _resident((D_FF, D_MODEL)),
                  _resident((1, D_MODEL)), _rows(tm, D_MODEL)],
        out_specs=[_rows(tm, D_FF), _rows(tm, D_MODEL), _rows(tm, D_MODEL), acc_spec(LANES), acc_spec(D_MODEL)],
        out_shape=[sds((T, D_FF), BF16), sds((T, D_MODEL), F32), sds((T, D_MODEL), BF16),
                   sds((8, LANES), F32), sds((8, D_MODEL), F32)],
        compiler_params=_params(("arbitrary",)),
    )(hn2, h1, w1t, w2, gf, tgt)


def _ffn_bwd(dh2b, dh2f, relu, h1, g2, w2, w1t):
    T = h1.shape[0]
    tm = TM_FFN

    def body(db_ref, df_ref, r_ref, h1_ref, g2_ref, w2_ref, w1t_ref, da_ref, d1f_ref, d1b_ref, dg_ref):
        @pl.when(pl.program_id(0) == 0)
        def _():
            dg_ref[...] = jnp.zeros_like(dg_ref)

        db = db_ref[...]
        acc = jnp.zeros((tm, D_MODEL), F32)
        for j in range(D_FF // FF_CHUNK):
            cols = slice(j * FF_CHUNK, (j + 1) * FF_CHUNK)
            da = (_dot(db, w2_ref[cols, :], NT) * (2.0 * r_ref[:, cols].astype(F32))).astype(BF16)
            da_ref[:, cols] = da
            acc = acc + _dot(da, w1t_ref[cols, :])
        n2, r2 = _rms(h1_ref[...])
        _accum_rows(dg_ref, acc * n2)
        dh1 = df_ref[...] + _rms_bwd(n2, r2, g2_ref[...], acc)
        d1f_ref[...] = dh1
        d1b_ref[...] = dh1.astype(BF16)

    sds = jax.ShapeDtypeStruct
    return pl.pallas_call(
        body, name="ffn_bwd", grid=(T // tm,),
        in_specs=[_rows(tm, D_MODEL), _rows(tm, D_MODEL), _rows(tm, D_FF), _rows(tm, D_MODEL),
                  _resident((1, D_MODEL)), _resident((D_FF, D_MODEL)), _resident((D_FF, D_MODEL))],
        out_specs=[_rows(tm, D_FF), _rows(tm, D_MODEL), _rows(tm, D_MODEL),
                   pl.BlockSpec((8, D_MODEL), lambda i: (0, 0))],
        out_shape=[sds((T, D_FF), BF16), sds((T, D_MODEL), F32), sds((T, D_MODEL), BF16), sds((8, D_MODEL), F32)],
        compiler_params=_params(("arbitrary",)),
    )(dh2b, dh2f, relu, h1, g2, w2, w1t)


def _out_bwd(dh1b, w_out, attn, gm, ga, gg):
    T = attn.shape[0]
    tm = TM_PROJ

    def body(d_ref, w_ref, a_ref, m_ref, ga_ref, gg_ref, da_ref, dm_ref, dga_ref, dgg_ref):
        @pl.when(pl.program_id(0) == 0)
        def _():
            dga_ref[...] = jnp.zeros_like(dga_ref)
            dgg_ref[...] = jnp.zeros_like(dgg_ref)

        d = d_ref[...]
        dan = _dot(d, w_ref[0:ATTN_W, :], NT)
        dgn = _dot(d, w_ref[ATTN_W:, :], NT)
        na, ra = _rms(a_ref[...])
        ng, rg = _rms(m_ref[...])
        _accum_rows(dga_ref, dan * na)
        _accum_rows(dgg_ref, dgn * ng)
        da_ref[...] = _rms_bwd(na, ra, ga_ref[...], dan)
        dm_ref[...] = _rms_bwd(ng, rg, gg_ref[...], dgn)

    sds = jax.ShapeDtypeStruct
    return pl.pallas_call(
        body, name="out_bwd", grid=(T // tm,),
        in_specs=[_rows(tm, D_MODEL), _resident((D_MODEL, D_MODEL)), _rows(tm, ATTN_W), _rows(tm, GMLP_W),
                  _resident((1, ATTN_W)), _resident((1, GMLP_W))],
        out_specs=[_rows(tm, ATTN_W), _rows(tm, GMLP_W), pl.BlockSpec((8, ATTN_W), lambda i: (0, 0)),
                   pl.BlockSpec((8, GMLP_W), lambda i: (0, 0))],
        out_shape=[sds((T, ATTN_W), F32), sds((T, GMLP_W), F32), sds((8, ATTN_W), F32), sds((8, GMLP_W), F32)],
        compiler_params=_params(("arbitrary",)),
    )(dh1b, w_out, attn, gm, ga, gg)


def _gmlp_bwd(u, z, dgm, ln_g, ln_b, sgu_w, sgu_bt):
    T = u.shape[0]
    tg = TM_GMLP
    nsteps = T // tg

    def body(u_ref, z_ref, d_ref, g_ref, b_ref, w_ref, sb_ref, dproj_hbm, dlg_ref, dlb_ref, dw_ref, dsb_ref,
             stage, sem):
        i = pl.program_id(0)
        slot = i % 2
        duz_ref = stage.at[slot]

        def to_dproj(step, buf):
            rows = pl.ds(pl.multiple_of(step * tg, tg), tg)
            return pltpu.make_async_copy(stage.at[buf], dproj_hbm.at[rows, pl.ds(3 * ATTN_W, 2 * GMLP_W)],
                                         sem.at[buf])

        @pl.when(i == 0)
        def _():
            for ref in (dlg_ref, dlb_ref, dw_ref, dsb_ref):
                ref[...] = jnp.zeros_like(ref)

        @pl.when(i >= 2)
        def _():
            to_dproj(i - 2, slot).wait()

        grp = lax.broadcasted_iota(jnp.int32, (tg, GMLP_W), 1) // HEAD_DIM
        lane = lax.broadcasted_iota(jnp.int32, (CHUNK, LANES), 1)
        causal, ws = _causal_ws(w_ref)
        lg = g_ref[...]
        uu, zz, dgm = u_ref[...], z_ref[...], d_ref[...]
        ug, tu, tz, xhat, rstd, zn16, mixed = _gmlp_core(uu, zz, lg, b_ref[...], ws, sb_ref, grp)
        dmx = dgm * ug
        duz_ref[:, 0:GMLP_W] = dgm * mixed * _gelu_grad(uu, tu)
        dmx16 = dmx.astype(BF16)
        low = grp[:CHUNK, :LANES] == 0
        dzn = []
        for ci in range(tg // CHUNK):
            rows = slice(ci * CHUNK, (ci + 1) * CHUNK)
            halves = []
            for h in range(GMLP_W // LANES):
                lanes = slice(h * LANES, (h + 1) * LANES)
                dmx_h, zn_h, zero = dmx16[rows, lanes], zn16[rows, lanes], jnp.zeros((CHUNK, LANES), BF16)
                halves.append(jnp.where(low, _dot(ws[2 * h], dmx_h, TN), _dot(ws[2 * h + 1], dmx_h, TN)))
                dw_ref[2 * h] += _dot(jnp.where(low, dmx_h, zero), zn_h, NT)
                dw_ref[2 * h + 1] += _dot(jnp.where(low, zero, dmx_h), zn_h, NT)
            dzn.append(jnp.concatenate(halves, axis=1))
        dzn = jnp.concatenate(dzn, axis=0)
        dsb = jnp.zeros((CHUNK, LANES), F32)
        for g in range(N_GROUPS):
            half = slice((g // 2) * LANES, (g // 2 + 1) * LANES)
            per_token = jnp.sum(jnp.where(grp[:, half] == g, dmx[:, half], 0.0), axis=-1, keepdims=True)
            by_position = sum(per_token[ci * CHUNK:(ci + 1) * CHUNK] for ci in range(tg // CHUNK))
            dsb = jnp.where(lane == g, by_position, dsb)
        dsb_ref[...] += dsb
        _accum_rows(dlg_ref, dzn * xhat)
        _accum_rows(dlb_ref, dzn)
        dxh = dzn * lg
        dzg = rstd * (dxh - _group_mean(dxh, grp) - xhat * _group_mean(dxh * xhat, grp))
        duz_ref[:, GMLP_W:] = dzg * _gelu_grad(zz, tz)
        to_dproj(i, slot).start()

        @pl.when(i == nsteps - 1)
        def _():
            for g in range(N_GROUPS):
                dw_ref[g] = jnp.where(causal, dw_ref[g], 0.0)
            to_dproj(i, slot).wait()
            if nsteps >= 2:
                to_dproj(i - 1, 1 - slot).wait()

    sds = jax.ShapeDtypeStruct
    return pl.pallas_call(
        body, name="gmlp_bwd", grid=(nsteps,),
        in_specs=[_rows(tg, GMLP_W)] * 3 + [_resident((1, GMLP_W)), _resident((1, GMLP_W)),
                                              _resident((N_GROUPS, CHUNK, CHUNK)), _resident((CHUNK, N_GROUPS))],
        out_specs=[_HBM, pl.BlockSpec((8, GMLP_W), lambda i: (0, 0)),
                   pl.BlockSpec((8, GMLP_W), lambda i: (0, 0)),
                   pl.BlockSpec((N_GROUPS, CHUNK, CHUNK), lambda i: (0, 0, 0)),
                   pl.BlockSpec((CHUNK, LANES), lambda i: (0, 0))],
        out_shape=[sds((T, IN_W), F32), sds((8, GMLP_W), F32), sds((8, GMLP_W), F32),
                   sds((N_GROUPS, CHUNK, CHUNK), F32), sds((CHUNK, LANES), F32)],
        scratch_shapes=[pltpu.VMEM((2, tg, 2 * GMLP_W), F32), pltpu.SemaphoreType.DMA((2,))],
        compiler_params=_params(("arbitrary",)),
    )(u, z, dgm, ln_g, ln_b, sgu_w, sgu_bt)


def _attn_bwd(q, k, v, dattn, attn, lse, dproj, owner_grads=()):
    T = q.shape[0]
    nt = T // ATT_TILE
    ns = len(owner_grads)
    steps = (ATTN_W // LANES) * nt

    def body(sl_ref, q_hbm, k_hbm, v_hbm, do_hbm, o_hbm, lse_hbm, _, *rest):
        p_refs, rest = rest[:ns], rest[ns:]
        dq_hbm = dk_hbm = dv_hbm = rest[0]
        r_refs, rest = rest[1:1 + ns], rest[1 + ns:]
        qbuf, dobuf, obuf, lbuf, kbuf, vbuf, dqbuf, dkbuf, dvbuf, delta_s = rest[:10]
        sem_q, sem_do, sem_o, sem_l, sem_k, sem_v, sem_dq, sem_dk, sem_dv = rest[10:19]
        hp, t = pl.program_id(0), pl.program_id(1)
        step = hp * nt + t
        two, three = step % 2, step % 3
        before, after = (step + 2) % 3, (step + 1) % 3
        if ns:
            start, finish = _owner_exchange_phases(p_refs, r_refs, *rest[19:])
            pl.when(step == 0)(start)

        def fetch(hp_, t_, two_, three_):
            for hbm, buf, sem, slot in ((q_hbm, qbuf, sem_q, two_), (do_hbm, dobuf, sem_do, two_),
                                        (o_hbm, obuf, sem_o, two_), (lse_hbm, lbuf, sem_l, two_),
                                        (k_hbm, kbuf, sem_k, three_), (v_hbm, vbuf, sem_v, three_)):
                for cp in _tile_copies(hbm, buf.at[slot], sem.at[slot], hp_, t_):
                    cp.start()

        @pl.when(step == 0)
        def _():
            kbuf[2] = jnp.zeros((ATT_BLOCKS, CHUNK, LANES), F32)
            vbuf[2] = jnp.zeros((ATT_BLOCKS, CHUNK, LANES), F32)
            dkbuf[3] = jnp.zeros((ATT_BLOCKS, CHUNK, LANES), F32)
            dvbuf[3] = jnp.zeros((ATT_BLOCKS, CHUNK, LANES), F32)
            fetch(0, 0, 0, 0)

        @pl.when(step + 1 < steps)
        def _():
            fetch((step + 1) // nt, (step + 1) % nt, 1 - two, after)

        for buf, sem in ((qbuf, sem_q), (dobuf, sem_do), (obuf, sem_o), (lbuf, sem_l)):
            _wait_tile(buf.at[two], sem.at[two])
        _wait_tile(kbuf.at[three], sem_k.at[three])
        _wait_tile(vbuf.at[three], sem_v.at[three])

        @pl.when(step >= 2)
        def _():
            _wait_tile(dqbuf.at[two], sem_dq.at[two])

        @pl.when(step >= 3)
        def _():
            _wait_tile(dkbuf.at[three], sem_dk.at[three])
            _wait_tile(dvbuf.at[three], sem_dv.at[three])

        q_t, do_t, l_t, k_t, v_t = qbuf.at[two], dobuf.at[two], lbuf.at[two], kbuf.at[three], vbuf.at[three]
        k_b, v_b = kbuf.at[before], vbuf.at[before]
        dq_t, dk_t, dv_t = dqbuf.at[two], dkbuf.at[three], dvbuf.at[three]
        dk_b, dv_b = dkbuf.at[before], dvbuf.at[before]
        sink = jnp.where(t > 0, before, 3)
        dk_sink, dv_sink = dkbuf.at[sink], dvbuf.at[sink]
        head0 = lax.broadcasted_iota(jnp.int32, (CHUNK, LANES), 1) < HEAD_DIM
        for r in range(ATT_BLOCKS):
            dd = dobuf[two, r] * obuf[two, r]
            d0 = jnp.sum(jnp.where(head0, dd, 0.0), axis=-1, keepdims=True)
            d1 = jnp.sum(jnp.where(head0, 0.0, dd), axis=-1, keepdims=True)
            delta_s[r] = jnp.where(head0, d0, d1)

        def column(xb):
            return jnp.concatenate([xb[:, 0:1], xb[:, HEAD_DIM:HEAD_DIM + 1]], axis=0)

        no_key_before = jnp.where(lax.broadcasted_iota(jnp.int32, (2 * CHUNK, 2 * CHUNK), 1) < CHUNK, NEG, 0.0)
        for d in DILATIONS:
            bias = _residue_bias(sl_ref, d)
            def scores(j, d=d, bias=bias):
                kcat = jnp.concatenate([_rm_block_before(k_t, k_b, d, j), _rm_block(k_t, d, j)], axis=0).astype(BF16)
                vcat = jnp.concatenate([_rm_block_before(v_t, v_b, d, j), _rm_block(v_t, d, j)], axis=0).astype(BF16)
                q2 = _stack_heads(_rm_block(q_t, d, j), head0)
                do2 = _stack_heads(_rm_block(do_t, d, j), head0)
                return (_dot(q2, kcat, NT), _dot(do2, vcat, NT), column(_rm_block(l_t, d, j)),
                        column(_rm_block(delta_s, d, j)), bias_first if _first_in_tile(d, j) else bias, kcat, q2, do2)

            bias_first = bias + jnp.where(t == 0, 1.0, 0.0) * no_key_before
            group = {}
            for j in range(ATT_BLOCKS):
                if j % SM_BLOCKS == 0:
                    group = {i: scores(i) for i in range(j, j + SM_BLOCKS)}
                    s_all, dp_all, lse_all, delta_all, bias_all = (
                        jnp.concatenate([g[i] for g in group.values()], axis=0) for i in range(5))
                    p_all = jnp.exp(s_all + bias_all - lse_all)
                    ds_all = (p_all * (dp_all - delta_all)).astype(BF16)
                    p_all = p_all.astype(BF16)
                at = slice((j % SM_BLOCKS) * 2 * CHUNK, (j % SM_BLOCKS + 1) * 2 * CHUNK)
                ds, p16 = ds_all[at, :], p_all[at, :]
                kcat, q2, do2 = group[j][5:]
                first = d == DILATIONS[0]
                _rm_add(dq_t, _residue_rows(d, j), _unstack_heads(_dot(ds, kcat), head0), first)
                ck = _dot(ds, q2, TN)
                cv = _dot(p16, do2, TN)
                _rm_add(dk_t, _residue_rows(d, j), ck[CHUNK:, :], first)
                _rm_add(dv_t, _residue_rows(d, j), cv[CHUNK:, :], first)
                if _first_in_tile(d, j):
                    rows = [(r, CHUNK - n, n) for r, _, n in _residue_rows(d, j)]
                    _rm_add(dk_sink, rows, ck[:CHUNK, :])
                    _rm_add(dv_sink, rows, cv[:CHUNK, :])
                else:
                    rows = [(r, lo - n, n) for r, lo, n in _residue_rows(d, j)]
                    _rm_add(dk_t, rows, ck[:CHUNK, :])
                    _rm_add(dv_t, rows, cv[:CHUNK, :])

        for r in range(ATT_BLOCKS):
            dqbuf[two, r] = dqbuf[two, r] * Q_SCALE
        for cp in _tile_copies(dq_hbm, dq_t, sem_dq.at[two], hp, t, to_hbm=True):
            cp.start()

        @pl.when(t > 0)
        def _():
            for cp in (_tile_copies(dk_hbm, dk_b, sem_dk.at[before], hp, t - 1, to_hbm=True, lane0=ATTN_W)
                       + _tile_copies(dv_hbm, dv_b, sem_dv.at[before], hp, t - 1, to_hbm=True, lane0=2 * ATTN_W)):
                cp.start()

        @pl.when(t == nt - 1)
        def _():
            for cp in (_tile_copies(dk_hbm, dk_t, sem_dk.at[three], hp, t, to_hbm=True, lane0=ATTN_W)
                       + _tile_copies(dv_hbm, dv_t, sem_dv.at[three], hp, t, to_hbm=True, lane0=2 * ATTN_W)):
                cp.start()

        @pl.when(step == steps - 1)
        def _():
            for slot in range(2):
                _wait_tile(dqbuf.at[slot], sem_dq.at[slot])
            for slot in range(3):
                _wait_tile(dkbuf.at[slot], sem_dk.at[slot])
                _wait_tile(dvbuf.at[slot], sem_dv.at[slot])

        if ns:
            pl.when(step == steps - 1)(finish)

    tile = lambda n: pltpu.VMEM((n, ATT_BLOCKS, CHUNK, LANES), F32)
    dma = lambda n: pltpu.SemaphoreType.DMA((n,))
    view = jax.ShapeDtypeStruct((T // ATT_BLOCKS, ATT_BLOCKS, ATTN_W), F32)
    outs = pl.pallas_call(
        body, name="attn_bwd", grid=(ATTN_W // LANES, nt),
        in_specs=[pl.BlockSpec((8, LANES), lambda c, t: (0, c))] + [_HBM] * (7 + ns),
        out_specs=[_HBM] * (1 + ns),
        out_shape=[jax.ShapeDtypeStruct((T // ATT_BLOCKS, ATT_BLOCKS, IN_W), F32)]
        + [jax.ShapeDtypeStruct(p.shape, p.dtype) for p in owner_grads],
        scratch_shapes=[tile(2), tile(2), tile(2), tile(2), tile(3), tile(3), tile(2), tile(4), tile(4),
                        pltpu.VMEM((ATT_BLOCKS, CHUNK, LANES), F32)]
        + [dma(2), dma(2), dma(2), dma(2), dma(3), dma(3), dma(2), dma(3), dma(3)]
        + (_owner_exchange_sems(ns) if ns else []),
        input_output_aliases={7: 0},
        compiler_params=_params(("arbitrary", "arbitrary")),
    )(_slope_table(), *[_residue_view(a) for a in (q, k, v, dattn, attn, lse, dproj)], *owner_grads)
    return outs[0].reshape(T, IN_W), tuple(outs[1:])


def _proj_bwd(dproj, w_in_t, x, g1, dh1, owner_grads=()):
    T = x.shape[0]
    tm = TM_PROJ
    ns = len(owner_grads)
    steps = T // tm

    def body(d_ref, w_ref, x_ref, g_ref, r_ref, *rest):
        p_refs, rest = rest[:ns], rest[ns:]
        dx_ref, dg_ref = rest[:2]
        r_refs, sems = rest[2:2 + ns], rest[2 + ns:]
        step = pl.program_id(0)
        if ns:
            start, finish = _owner_exchange_phases(p_refs, r_refs, *sems)
            pl.when(step == 0)(start)

        @pl.when(step == 0)
        def _():
            dg_ref[...] = jnp.zeros_like(dg_ref)

        dhn = _dot(d_ref[...].astype(BF16), w_ref[...])
        n1, r1 = _rms(x_ref[...])
        _accum_rows(dg_ref, dhn * n1)
        dx_ref[...] = r_ref[...] + _rms_bwd(n1, r1, g_ref[...], dhn)
        if ns:
            pl.when(step == steps - 1)(finish)

    outs = pl.pallas_call(
        body, name="proj_bwd", grid=(steps,),
        in_specs=[_rows(tm, IN_W), _resident((IN_W, D_MODEL)), _rows(tm, D_MODEL), _resident((1, D_MODEL)),
                  _rows(tm, D_MODEL)] + [_HBM] * ns,
        out_specs=[_rows(tm, D_MODEL), pl.BlockSpec((8, D_MODEL), lambda i: (0, 0))] + [_HBM] * ns,
        out_shape=[jax.ShapeDtypeStruct((T, D_MODEL), F32), jax.ShapeDtypeStruct((8, D_MODEL), F32)]
        + [jax.ShapeDtypeStruct(p.shape, p.dtype) for p in owner_grads],
        scratch_shapes=_owner_exchange_sems(ns) if ns else [],
        compiler_params=_params(("arbitrary",)),
    )(dproj, w_in_t, x, g1, dh1, *owner_grads)
    return outs[0], outs[1], tuple(outs[2:])


def _dw(a, b, name, tile, square_a=False, out_dtype=F32):
    T, ka = a.shape
    nb = b.shape[1]
    tka, tnb, tt = tile
    tt = min(tt, T)
    last = T // tt - 1

    def body(a_ref, b_ref, *refs):
        o_ref = refs[0]
        acc_ref = refs[1] if len(refs) > 1 else o_ref
        s = pl.program_id(2)

        @pl.when(s == 0)
        def _():
            acc_ref[...] = jnp.zeros_like(acc_ref)

        a_tile = a_ref[...]
        if square_a:
            a_tile = jnp.square(a_tile.astype(F32))
        acc_ref[...] += _dot(a_tile.astype(BF16), b_ref[...], TN)
        if acc_ref is not o_ref:
            @pl.when(s == last)
            def _():
                o_ref[...] = acc_ref[...].astype(out_dtype)

    return pl.pallas_call(
        body, name=name, grid=(ka // tka, nb // tnb, T // tt),
        in_specs=[pl.BlockSpec((tt, tka), lambda i, j, s: (s, i)), pl.BlockSpec((tt, tnb), lambda i, j, s: (s, j))],
        out_specs=pl.BlockSpec((tka, tnb), lambda i, j, s: (i, j)),
        out_shape=jax.ShapeDtypeStruct((ka, nb), out_dtype),
        scratch_shapes=[] if out_dtype == F32 else [pltpu.VMEM((tka, tnb), F32)],
        compiler_params=_params(("parallel", "parallel", "arbitrary")),
    )(a, b)


def _adamw_update(w, m, v, g):
    m2 = ADAM_B1 * m + (1.0 - ADAM_B1) * g
    v2 = ADAM_B2 * v + (1.0 - ADAM_B2) * jnp.square(g)
    m_hat = m2 / (1.0 - ADAM_B1 ** ADAM_STEP)
    v_hat = v2 / (1.0 - ADAM_B2 ** ADAM_STEP)
    return -ADAM_LR * (m_hat / (jnp.sqrt(v_hat) + ADAM_EPS) + ADAM_WD * w), m2, v2


def _adamw_tiny(ws, ms, vs, parts):
    n = len(ws)
    P = parts.shape[0]

    def body(*refs):
        w_refs, m_refs, v_refs, p_ref = refs[:n], refs[n:2 * n], refs[2 * n:3 * n], refs[3 * n]
        outs = refs[3 * n + 1:]

        def total(slot, rows):
            g = p_ref[0, 8 * slot:8 * slot + rows, :]
            for i in range(1, P):
                g = g + p_ref[i, 8 * slot:8 * slot + rows, :]
            return g

        for k in range(n):
            g = total(k, ws[k].shape[0])
            outs[4 * k][...] = g
            outs[4 * k + 1][...], outs[4 * k + 2][...], outs[4 * k + 3][...] = _adamw_update(
                w_refs[k][...], m_refs[k][...], v_refs[k][...], g)
        outs[4 * n][...] = total(n, 8)

    sds = jax.ShapeDtypeStruct
    return pl.pallas_call(
        body, name="adamw_tiny",
        out_shape=[sds(w.shape, F32) for w in ws for _ in range(4)] + [sds((8, LANES), F32)],
    )(*ws, *ms, *vs, parts)


def _adamw(w, m, v, parts, name, tr, transposed=False):
    R, C = w.shape
    P = parts.shape[0]

    def body(w_ref, m_ref, v_ref, p_ref, g_ref, d_ref, m2_ref, v2_ref):
        g = p_ref[0].astype(F32)
        for i in range(1, P):
            g = g + p_ref[i].astype(F32)
        if transposed:
            g = g.T
        g_ref[...] = g
        d_ref[...], m2_ref[...], v2_ref[...] = _adamw_update(w_ref[...], m_ref[...], v_ref[...], g)

    spec = _rows(tr, C)
    part_spec = (pl.BlockSpec((P, C, tr), lambda i: (0, 0, i)) if transposed
                 else pl.BlockSpec((P, tr, C), lambda i: (0, i, 0)))
    return pl.pallas_call(
        body, name=name, grid=(R // tr,),
        in_specs=[spec, spec, spec, part_spec],
        out_specs=[spec] * 4,
        out_shape=[jax.ShapeDtypeStruct((R, C), F32)] * 4,
        compiler_params=_params(("parallel",)),
    )(w, m, v, parts)


_HBM = pl.BlockSpec(memory_space=pltpu.HBM)


def _place():
    return lax.axis_index("x"), lax.axis_index("y"), lax.axis_index("c")


def _gathered_shape(shard):
    return jax.ShapeDtypeStruct((N_DEV,) + shard.shape, shard.dtype)


def _gather_sems(n):
    return [pltpu.SemaphoreType.DMA((7, n)), pltpu.SemaphoreType.DMA((7, n)), pltpu.SemaphoreType.DMA((n,))]


def _gather_phases(x_refs, out_refs, send_sems, recv_sems, local_sems):
    x, y, c = _place()
    me, sibling = (x, y, c), (x, y, 1 - c)
    chips = [(1 - x, y), (x, 1 - y), (1 - x, 1 - y)]
    arrays = range(len(x_refs))

    def slot(i, px, py, pc):
        return out_refs[i].at[4 * px + 2 * py + pc]

    def copy(i, k, block, to, own=False):
        return pltpu.make_async_remote_copy(
            src_ref=x_refs[i] if own else slot(i, *block), dst_ref=slot(i, *block),
            send_sem=send_sems.at[k, i], recv_sem=recv_sems.at[k, i], device_id=to, device_id_type=MESH)

    def mine(i):
        return pltpu.make_async_copy(x_refs[i], slot(i, *me), local_sems.at[i])

    def start():
        for i in arrays:
            mine(i).start()
            copy(i, 0, me, sibling, own=True).start()
            for j, chip in enumerate(chips):
                copy(i, 1 + j, me, (*chip, c), own=True).start()

    def forward():
        for i in arrays:
            for j, chip in enumerate(chips):
                copy(i, 1 + j, (*chip, c), me).wait_recv()
                copy(i, 4 + j, (*chip, c), sibling).start()

    def finish():
        for i in arrays:
            copy(i, 0, sibling, me).wait_recv()
            copy(i, 0, me, sibling, own=True).wait_send()
            for j, chip in enumerate(chips):
                copy(i, 4 + j, (*chip, 1 - c), me).wait_recv()
                copy(i, 1 + j, me, (*chip, c), own=True).wait_send()
                copy(i, 4 + j, (*chip, c), sibling).wait_send()
            mine(i).wait()

    return start, forward, finish


def _all_gather(shards, name):
    n = len(shards)

    def body(*refs):
        start, forward, finish = _gather_phases(refs[:n], refs[n:2 * n], *refs[2 * n:])
        start()
        forward()
        finish()

    return pl.pallas_call(
        body, name=name,
        out_shape=[_gathered_shape(s) for s in shards],
        in_specs=[_HBM] * n, out_specs=[_HBM] * n,
        scratch_shapes=_gather_sems(n),
    )(*shards)


def _owner_exchange_sems(n):
    return [pltpu.SemaphoreType.DMA((7, n)), pltpu.SemaphoreType.DMA((7, n)), pltpu.SemaphoreType.DMA((n,))]


def _owner_exchange_phases(g_refs, r_refs, send_sems, recv_sems, local_sems):
    x, y, c = _place()
    me = 4 * x + 2 * y + c
    flip = lambda v, bit: 1 - v if bit else v
    peers = [(flip(x, k & 4), flip(y, k & 2), flip(c, k & 1)) for k in range(1, N_DEV)]
    arrays = range(len(g_refs))

    def mine(i):
        return pltpu.make_async_copy(g_refs[i].at[me], r_refs[i].at[me], local_sems.at[i])

    def copy(i, k, src_slot, dst_slot):
        return pltpu.make_async_remote_copy(
            src_ref=g_refs[i].at[src_slot], dst_ref=r_refs[i].at[dst_slot],
            send_sem=send_sems.at[k, i], recv_sem=recv_sems.at[k, i], device_id=peers[k], device_id_type=MESH)

    def start():
        for i in arrays:
            mine(i).start()
            for k, (px, py, pc) in enumerate(peers):
                copy(i, k, 4 * px + 2 * py + pc, me).start()

    def finish():
        for i in arrays:
            for k, (px, py, pc) in enumerate(peers):
                copy(i, k, me, 4 * px + 2 * py + pc).wait_recv()
                copy(i, k, 4 * px + 2 * py + pc, me).wait_send()
            mine(i).wait()

    return start, finish


def _local_step(x, tgt, small, w_in_t, rest, exchange=False):
    g1, g2, gf = small["norm1_g"], small["norm2_g"], small["final_norm_g"].reshape(1, D_MODEL)
    ga, gg = small["attn_out_g"], small["gmlp_out_g"]
    ln_g = small["sgu_ln_g"].reshape(1, GMLP_W)
    ln_b = small["sgu_ln_b"].reshape(1, GMLP_W)
    sgu_w = small["sgu_w"][0]
    sgu_bt = small["sgu_b"][0].T

    hn1, q, k, v, u, z = _proj_fwd(x, g1, w_in_t)
    attn, lse, gathered = _attn_fwd(q, k, v, shards=rest if exchange else ())
    w_out, w_ff1_t, w_ff2 = [g.reshape(-1, D_MODEL) for g in gathered] if exchange else rest
    gm = _gmlp_fwd(u, z, ln_g, ln_b, sgu_w, sgu_bt)
    mixed, h1, hn2 = _out_fwd(attn, gm, ga, gg, w_out, x, g2)
    relu, dh2f, dh2b, loss8, dgf8 = _ffn_fwd(hn2, h1, w_ff1_t, w_ff2, gf, tgt)

    da, dh1f, dh1b, dg2 = _ffn_bwd(dh2b, dh2f, relu, h1, g2, w_ff2, w_ff1_t)
    wire = BF16 if exchange else F32
    dw_ff2 = _dw(relu, dh2b, "dw_ff2", DW_TILE, square_a=True, out_dtype=wire)
    dw_ff1_t = _dw(da, hn2, "dw_ff1", DW_TILE, out_dtype=wire)
    dattn, dgm, dga, dgg = _out_bwd(dh1b, w_out, attn, gm, ga, gg)
    dw_out = _dw(mixed, dh1b, "dw_out", DW_TILE, out_dtype=wire)
    early = [dw_out, dw_ff1_t, dw_ff2]
    if exchange:
        early = [g.reshape(N_DEV, -1, D_MODEL) for g in early]
    dproj, dlg, dlb, dsw, dsb = _gmlp_bwd(u, z, dgm, ln_g, ln_b, sgu_w, sgu_bt)
    dproj, arrived = _attn_bwd(q, k, v, dattn, attn, lse, dproj, owner_grads=early if exchange else ())
    dw_in_t = _dw(dproj, hn1, "dw_in", DW_TILE_IN, out_dtype=wire)
    late = (dw_in_t.reshape(N_DEV, -1, D_MODEL),) if exchange else ()
    dx, dg1, late = _proj_bwd(dproj, w_in_t, x, g1, dh1f, owner_grads=late)
    if exchange:
        dw_in_t, early = late[0], arrived

    small_grads = dict(
        norm1_g=dg1[0], sgu_ln_g=dlg[0], sgu_ln_b=dlb[0], sgu_w=dsw, sgu_b=dsb[:, :N_GROUPS].T,
        attn_out_g=dga[0], gmlp_out_g=dgg[0], norm2_g=dg2[0], final_norm_g=dgf8[0])
    return loss8[0, 0], dx, (dw_in_t, *early), small_grads


SMALL_NAMES = ("norm1_g", "sgu_ln_g", "sgu_ln_b", "sgu_w", "sgu_b", "attn_out_g", "gmlp_out_g", "norm2_g",
               "final_norm_g")
WEIGHT_ORDER = ("norm1_g", "w_in", "sgu_ln_g", "sgu_ln_b", "sgu_w", "sgu_b", "attn_out_g", "gmlp_out_g", "w_out",
                "norm2_g", "w_ff1", "w_ff2", "final_norm_g")


TINY_NAMES = tuple(n for n in SMALL_NAMES if n != "sgu_w")


def _as_rows(a):
    return a.reshape(-1, LANES)


def _pack_tiny_grads(d, loss):
    slots = [jnp.pad(_as_rows(d[n]), ((0, 8 - d[n].size // LANES), (0, 0))) for n in TINY_NAMES]
    return jnp.concatenate(slots + [jnp.full((8, LANES), loss, F32)], axis=0)


def kernel(x, norm1_g, w_in, sgu_ln_g, sgu_ln_b, sgu_w, sgu_b, attn_out_g, gmlp_out_g, w_out, norm2_g, w_ff1, w_ff2, final_norm_g, loss_target, m_norm1_g, m_w_in, m_sgu_ln_g, m_sgu_ln_b, m_sgu_w, m_sgu_b, m_attn_out_g, m_gmlp_out_g, m_w_out, m_norm2_g, m_w_ff1, m_w_ff2, m_final_norm_g, v_norm1_g, v_w_in, v_sgu_ln_g, v_sgu_ln_b, v_sgu_w, v_sgu_b, v_attn_out_g, v_gmlp_out_g, v_w_out, v_norm2_g, v_w_ff1, v_w_ff2, v_final_norm_g):
    w = dict(norm1_g=norm1_g, w_in=w_in, sgu_ln_g=sgu_ln_g, sgu_ln_b=sgu_ln_b, sgu_w=sgu_w, sgu_b=sgu_b,
             attn_out_g=attn_out_g, gmlp_out_g=gmlp_out_g, w_out=w_out, norm2_g=norm2_g, w_ff1=w_ff1, w_ff2=w_ff2,
             final_norm_g=final_norm_g)
    m = dict(norm1_g=m_norm1_g, w_in=m_w_in, sgu_ln_g=m_sgu_ln_g, sgu_ln_b=m_sgu_ln_b, sgu_w=m_sgu_w, sgu_b=m_sgu_b,
             attn_out_g=m_attn_out_g, gmlp_out_g=m_gmlp_out_g, w_out=m_w_out, norm2_g=m_norm2_g, w_ff1=m_w_ff1,
             w_ff2=m_w_ff2, final_norm_g=m_final_norm_g)
    v = dict(norm1_g=v_norm1_g, w_in=v_w_in, sgu_ln_g=v_sgu_ln_g, sgu_ln_b=v_sgu_ln_b, sgu_w=v_sgu_w, sgu_b=v_sgu_b,
             attn_out_g=v_attn_out_g, gmlp_out_g=v_gmlp_out_g, w_out=v_w_out, norm2_g=v_norm2_g, w_ff1=v_w_ff1,
             w_ff2=v_w_ff2, final_norm_g=v_final_norm_g)
    big = ("w_in", "w_out", "w_ff1", "w_ff2")

    w_in_t, = _all_gather([w_in[0].T.astype(BF16)], "w_in_all_gather")
    rest = (w_out[0].astype(BF16), w_ff1[0].T.astype(BF16), w_ff2[0].astype(BF16))
    loss, dx, parts, small_grads = _local_step(x[0], loss_target[0], {n: w[n] for n in SMALL_NAMES},
                                               w_in_t.reshape(IN_W, D_MODEL), rest, exchange=True)

    new = {}
    for n, p, transposed, tr in zip(big, parts, (True, False, True, False), (128, 128, 128, 256)):
        new[n] = [a[None] for a in _adamw(w[n][0], m[n][0], v[n][0], p, "adamw_" + n, tr, transposed)]

    tiny_parts, sgu_parts = _all_gather(
        [_pack_tiny_grads(small_grads, loss), _as_rows(small_grads["sgu_w"]).astype(BF16)], "small_grad_all_gather")
    tiny = _adamw_tiny(*[[_as_rows(src[n]) for n in TINY_NAMES] for src in (w, m, v)], tiny_parts)
    sgu = _adamw(_as_rows(sgu_w), _as_rows(m_sgu_w), _as_rows(v_sgu_w), sgu_parts, "adamw_sgu_w", 512)
    loss = tiny[-1][0, 0]

    outs = []
    for i in range(4):
        d = {n: new[n][i] for n in big}
        d.update({n: tiny[4 * k + i].reshape(w[n].shape) for k, n in enumerate(TINY_NAMES)})
        d["sgu_w"] = sgu[i].reshape(sgu_w.shape)
        outs.extend(d[n] for n in WEIGHT_ORDER)
    return (loss, dx[None], *outs)
```

```python
import math

import numpy as np
import jax
import jax.numpy as jnp
from jax import lax
from jax.experimental import pallas as pl
from jax.experimental.pallas import tpu as pltpu

F32 = jnp.float32
BF16 = jnp.bfloat16

D_MODEL = 1024
HEAD_DIM = 64
N_HEADS = 12
ATTN_W = N_HEADS * HEAD_DIM
N_GROUPS = 4
GMLP_W = N_GROUPS * HEAD_DIM
IN_W = 3 * ATTN_W + 2 * GMLP_W
D_FF = 4 * D_MODEL
CHUNK = 128
DILATIONS = (1, 4, 16)
EPS = 1e-6
Q_SCALE = HEAD_DIM ** -0.5
NEG = -1e30

ADAM_LR, ADAM_B1, ADAM_B2, ADAM_EPS, ADAM_WD, ADAM_STEP = 0.001, 0.9, 0.999, 1e-08, 0.01, 10

N_DEV = 8
LANES = 128
VMEM_LIMIT = 56 << 20

TM_PROJ = 512
TM_FFN = 512
FF_CHUNK = 512
TM_GMLP = 1024
DW_TILE = (512, 1024, 8192)
DW_TILE_IN = (IN_W // 2, 1024, 2048)

MESH = pl.DeviceIdType.MESH


def _alibi_slopes(n):
    def pow2(m):
        start = 2.0 ** (-8.0 / m)
        return [start ** (i + 1) for i in range(m)]
    c = 2 ** int(math.floor(math.log2(n)))
    s = pow2(n) if c == n else pow2(c) + pow2(2 * c)[0::2][: n - c]
    return np.asarray(s, dtype=np.float32)


SLOPES = _alibi_slopes(N_HEADS)


def _params(sem=None):
    kw = dict(vmem_limit_bytes=VMEM_LIMIT)
    if sem is not None:
        kw["dimension_semantics"] = sem
    return pltpu.CompilerParams(**kw)


def _rows(tm, n):
    return pl.BlockSpec((tm, n), lambda i: (i, 0))


def _resident(shape):
    return pl.BlockSpec(shape, lambda *_: (0,) * len(shape), pipeline_mode=pl.Buffered(1))


def _rms(x):
    r = lax.rsqrt(jnp.mean(x * x, axis=-1, keepdims=True) + EPS)
    return x * r, r


def _rms_bwd(n, r, g, dy):
    dn = dy * g
    return r * (dn - n * jnp.mean(dn * n, axis=-1, keepdims=True))


def _accum_rows(acc_ref, v):
    acc_ref[...] += jnp.broadcast_to(jnp.sum(v, axis=0, keepdims=True), acc_ref.shape)


_G0 = math.sqrt(2.0 / math.pi)
_G1 = 0.044715


def _gelu(x):
    t = jnp.tanh(_G0 * (x + _G1 * (x * x * x)))
    return x * (0.5 * (1.0 + t)), t


def _gelu_grad(x, t):
    return 0.5 * (1.0 + t) + 0.5 * x * (1.0 - t * t) * (_G0 * (1.0 + 3.0 * _G1 * x * x))


NT = (((1,), (1,)), ((), ()))
TN = (((0,), (0,)), ((), ()))


def _dot(a, b, dims=None):
    if dims is None:
        return jnp.dot(a, b, preferred_element_type=F32)
    return lax.dot_general(a, b, dims, preferred_element_type=F32)


def _proj_fwd(x, g1, w_in_t):
    T = x.shape[0]
    tm = TM_PROJ

    def body(x_ref, g_ref, w_ref, hn_ref, q_ref, k_ref, v_ref, u_ref, z_ref):
        n, _ = _rms(x_ref[...])
        hn = (n * g_ref[...]).astype(BF16)
        hn_ref[...] = hn
        a = ATTN_W
        q_ref[...] = _dot(hn, w_ref[0:a, :], NT) * Q_SCALE
        k_ref[...] = _dot(hn, w_ref[a:2 * a, :], NT)
        v_ref[...] = _dot(hn, w_ref[2 * a:3 * a, :], NT)
        u_ref[...] = _dot(hn, w_ref[3 * a:3 * a + GMLP_W, :], NT)
        z_ref[...] = _dot(hn, w_ref[3 * a + GMLP_W:, :], NT)

    sds = jax.ShapeDtypeStruct
    return pl.pallas_call(
        body, name="proj_fwd", grid=(T // tm,),
        in_specs=[_rows(tm, D_MODEL), _resident((1, D_MODEL)), _resident((IN_W, D_MODEL))],
        out_specs=[_rows(tm, D_MODEL), _rows(tm, ATTN_W), _rows(tm, ATTN_W), _rows(tm, ATTN_W),
                   _rows(tm, GMLP_W), _rows(tm, GMLP_W)],
        out_shape=[sds((T, D_MODEL), BF16), sds((T, ATTN_W), F32), sds((T, ATTN_W), F32),
                   sds((T, ATTN_W), F32), sds((T, GMLP_W), F32), sds((T, GMLP_W), F32)],
        compiler_params=_params(("parallel",)),
    )(x, g1, w_in_t)


ATT_TILE = 2048
ATT_BLOCKS = ATT_TILE // CHUNK
SM_BLOCKS = 4


def _slope_table():
    row = np.repeat(SLOPES, HEAD_DIM)
    return jnp.asarray(np.broadcast_to(row[None], (8, ATTN_W)), F32)


def _residue_view(a):
    return a.reshape(a.shape[0] // ATT_BLOCKS, ATT_BLOCKS, a.shape[1])


def _tile_copies(hbm, buf, sem, hp, t, to_hbm=False, lane0=0):
    rows = pl.ds(pl.multiple_of(t * CHUNK, CHUNK), CHUNK)
    lanes = pl.ds(pl.multiple_of(lane0 + hp * LANES, LANES), LANES)
    pairs = [(hbm.at[rows, r, lanes], buf.at[r]) for r in range(ATT_BLOCKS)]
    return [pltpu.make_async_copy(v, h, sem) if to_hbm else pltpu.make_async_copy(h, v, sem) for h, v in pairs]


def _wait_tile(buf, sem):
    pltpu.make_async_copy(buf, buf, sem).wait()


def _residue_rows(d, j):
    if d == 16:
        return [(j, 0, CHUNK)]
    if d == 4:
        return [(j % 4 + 4 * m, 32 * (j // 4), 32) for m in range(4)]
    return [(r, 8 * j, 8) for r in range(ATT_BLOCKS)]


def _block_order(p, d):
    if d == 16:
        return p
    if d == 4:
        return 4 * (p & 31) + (p >> 5)
    return 16 * (p & 7) + (p >> 3)


def _first_in_tile(d, j):
    return _residue_rows(d, j)[0][1] == 0


def _rm_block(buf, d, j):
    return jnp.concatenate([buf[r, lo:lo + n, :] for r, lo, n in _residue_rows(d, j)], axis=0)


def _rm_block_before(buf, buf_before, d, j):
    if _first_in_tile(d, j):
        return jnp.concatenate([buf_before[r, CHUNK - n:CHUNK, :] for r, _, n in _residue_rows(d, j)], axis=0)
    return jnp.concatenate([buf[r, lo - n:lo, :] for r, lo, n in _residue_rows(d, j)], axis=0)


def _rm_store(buf, d, j, val):
    at = 0
    for r, lo, n in _residue_rows(d, j):
        buf[r, lo:lo + n, :] = val[at:at + n, :]
        at += n


def _rm_add(buf, rows, val, first=False):
    at = 0
    for r, lo, n in rows:
        if first:
            buf[r, lo:lo + n, :] = val[at:at + n, :]
        else:
            buf[r, lo:lo + n, :] += val[at:at + n, :]
        at += n


def _residue_bias(sl_ref, d):
    shape = (2 * CHUNK, 2 * CHUNK)
    row = lax.broadcasted_iota(jnp.int32, shape, 0)
    col = lax.broadcasted_iota(jnp.int32, shape, 1)
    steps = _block_order(row & (CHUNK - 1), d) + CHUNK - (_block_order(col & (CHUNK - 1), d) + (col & CHUNK))
    band = (steps >= 0) & (steps <= CHUNK)
    sl = sl_ref[0:1, :]
    upper = lax.broadcasted_iota(jnp.int32, (2 * CHUNK, 1), 0) < CHUNK
    slope2 = jnp.where(upper, sl[:, 0:1], sl[:, HEAD_DIM:HEAD_DIM + 1])
    return jnp.where(band, -(float(d) * slope2 * steps.astype(F32)), NEG)


def _stack_heads(xb, head0):
    zero = jnp.zeros_like(xb)
    return jnp.concatenate([jnp.where(head0, xb, zero), jnp.where(head0, zero, xb)], axis=0).astype(BF16)


def _unstack_heads(x2, head0):
    return jnp.where(head0, x2[:CHUNK, :], x2[CHUNK:, :])


def _attn_fwd(q, k, v, shards=()):
    T = q.shape[0]
    nt = T // ATT_TILE
    ns = len(shards)
    steps = (ATTN_W // LANES) * nt

    def body(sl_ref, q_hbm, k_hbm, v_hbm, *rest):
        x_refs, rest = rest[:ns], rest[ns:]
        attn_hbm, lse_hbm = rest[:2]
        g_refs, rest = rest[2:2 + ns], rest[2 + ns:]
        qbuf, kbuf, vbuf, obuf, lbuf = rest[:5]
        o_acc, l_acc = rest[5:8], rest[8:11]
        sem_q, sem_k, sem_v, sem_o, sem_l = rest[11:16]
        hp, t = pl.program_id(0), pl.program_id(1)
        step = hp * nt + t
        two, three = step % 2, step % 3
        before, after = (step + 2) % 3, (step + 1) % 3
        if ns:
            start, forward, finish = _gather_phases(x_refs, g_refs, *rest[16:])
            pl.when(step == 0)(start)
            pl.when(step == (3 * steps) // 4)(forward)

        def fetch(hp_, t_, two_, three_):
            for cp in (_tile_copies(q_hbm, qbuf.at[two_], sem_q.at[two_], hp_, t_)
                       + _tile_copies(k_hbm, kbuf.at[three_], sem_k.at[three_], hp_, t_)
                       + _tile_copies(v_hbm, vbuf.at[three_], sem_v.at[three_], hp_, t_)):
                cp.start()

        @pl.when(step == 0)
        def _():
            kbuf[2] = jnp.zeros((ATT_BLOCKS, CHUNK, LANES), F32)
            vbuf[2] = jnp.zeros((ATT_BLOCKS, CHUNK, LANES), F32)
            fetch(0, 0, 0, 0)

        @pl.when(step + 1 < steps)
        def _():
            fetch((step + 1) // nt, (step + 1) % nt, 1 - two, after)

        _wait_tile(qbuf.at[two], sem_q.at[two])
        _wait_tile(kbuf.at[three], sem_k.at[three])
        _wait_tile(vbuf.at[three], sem_v.at[three])

        @pl.when(step >= 2)
        def _():
            _wait_tile(obuf.at[two], sem_o.at[two])
            _wait_tile(lbuf.at[two], sem_l.at[two])

        q_t, k_t, v_t = qbuf.at[two], kbuf.at[three], vbuf.at[three]
        k_b, v_b = kbuf.at[before], vbuf.at[before]
        head0 = lax.broadcasted_iota(jnp.int32, (CHUNK, LANES), 1) < HEAD_DIM
        no_key_before = jnp.where(lax.broadcasted_iota(jnp.int32, (2 * CHUNK, 2 * CHUNK), 1) < CHUNK, NEG, 0.0)
        for pi, d in enumerate(DILATIONS):
            bias = _residue_bias(sl_ref, d)

            def scores(j, d=d, bias=bias):
                kcat = jnp.concatenate([_rm_block_before(k_t, k_b, d, j), _rm_block(k_t, d, j)], axis=0).astype(BF16)
                vcat = jnp.concatenate([_rm_block_before(v_t, v_b, d, j), _rm_block(v_t, d, j)], axis=0).astype(BF16)
                s = _dot(_stack_heads(_rm_block(q_t, d, j), head0), kcat, NT)
                return s, vcat, bias_first if _first_in_tile(d, j) else bias

            bias_first = bias + jnp.where(t == 0, 1.0, 0.0) * no_key_before
            for j0 in range(0, ATT_BLOCKS, SM_BLOCKS):
                group = [scores(j) for j in range(j0, j0 + SM_BLOCKS)]
                s = jnp.concatenate([g[0] for g in group], axis=0) + jnp.concatenate([g[2] for g in group], axis=0)
                m = jnp.max(s, axis=-1, keepdims=True)
                p = jnp.exp(s - m)
                l = jnp.sum(p, axis=-1, keepdims=True)
                p = p.astype(BF16)
                block = lambda a, i: a[i * 2 * CHUNK:(i + 1) * 2 * CHUNK, :]
                o = jnp.concatenate([_dot(block(p, i), g[1]) for i, g in enumerate(group)], axis=0) * (1.0 / l)
                lse = jnp.broadcast_to(m + jnp.log(l), o.shape)
                for i in range(SM_BLOCKS):
                    _rm_store(o_acc[pi], d, j0 + i, _unstack_heads(block(o, i), head0))
                    _rm_store(l_acc[pi], d, j0 + i, _unstack_heads(block(lse, i), head0))

        for r in range(ATT_BLOCKS):
            a, b, c = l_acc[0][r], l_acc[1][r], l_acc[2][r]
            m = jnp.maximum(jnp.maximum(a, b), c)
            ea, eb, ec = jnp.exp(a - m), jnp.exp(b - m), jnp.exp(c - m)
            tot = ea + eb + ec
            obuf[two, r] = (ea * o_acc[0][r] + eb * o_acc[1][r] + ec * o_acc[2][r]) / tot
            lbuf[two, r] = m + jnp.log(tot)

        for cp in (_tile_copies(attn_hbm, obuf.at[two], sem_o.at[two], hp, t, to_hbm=True)
                   + _tile_copies(lse_hbm, lbuf.at[two], sem_l.at[two], hp, t, to_hbm=True)):
            cp.start()

        @pl.when(step == steps - 1)
        def _():
            for slot in (two, 1 - two)[:min(steps, 2)]:
                _wait_tile(obuf.at[slot], sem_o.at[slot])
                _wait_tile(lbuf.at[slot], sem_l.at[slot])

        if ns:
            pl.when(step == steps - 1)(finish)

    tile = lambda n: pltpu.VMEM((n, ATT_BLOCKS, CHUNK, LANES), F32)
    dma = lambda n: pltpu.SemaphoreType.DMA((n,))
    view = jax.ShapeDtypeStruct((T // ATT_BLOCKS, ATT_BLOCKS, ATTN_W), F32)
    outs = pl.pallas_call(
        body, name="attn_fwd", grid=(ATTN_W // LANES, nt),
        in_specs=[pl.BlockSpec((8, LANES), lambda c, t: (0, c))] + [_HBM] * (3 + ns),
        out_specs=[_HBM] * (2 + ns),
        out_shape=[view, view] + [_gathered_shape(s) for s in shards],
        scratch_shapes=[tile(2), tile(3), tile(3), tile(2), tile(2)] + [pltpu.VMEM((ATT_BLOCKS, CHUNK, LANES), F32)] * 6
        + [dma(2), dma(3), dma(3), dma(2), dma(2)] + (_gather_sems(ns) if ns else []),
        compiler_params=_params(("arbitrary", "arbitrary")),
    )(_slope_table(), _residue_view(q), _residue_view(k), _residue_view(v), *shards)
    return outs[0].reshape(T, ATTN_W), outs[1].reshape(T, ATTN_W), tuple(outs[2:])


def _group_mean(v, grp):
    halves = []
    for h in range(GMLP_W // LANES):
        x = v[:, h * LANES:(h + 1) * LANES]
        low = grp[:, h * LANES:(h + 1) * LANES] == 2 * h
        a = jnp.sum(jnp.where(low, x, 0.0), axis=-1, keepdims=True)
        b = jnp.sum(jnp.where(low, 0.0, x), axis=-1, keepdims=True)
        halves.append(jnp.where(low, a, b) * (1.0 / HEAD_DIM))
    return jnp.concatenate(halves, axis=1)


def _gmlp_core(uu, zz, lg, lb, ws, sb_ref, grp):
    ug, tu = _gelu(uu)
    zg, tz = _gelu(zz)
    zc = zg - _group_mean(zg, grp)
    rstd = lax.rsqrt(_group_mean(zc * zc, grp) + EPS)
    xhat = zc * rstd
    zn16 = (xhat * lg + lb).astype(BF16)
    low = grp[:CHUNK, :LANES] == 0
    mixed = []
    for ci in range(uu.shape[0] // CHUNK):
        rows = slice(ci * CHUNK, (ci + 1) * CHUNK)
        halves = []
        for h in range(GMLP_W // LANES):
            zh = zn16[rows, h * LANES:(h + 1) * LANES]
            halves.append(jnp.where(low, _dot(ws[2 * h], zh) + sb_ref[:, 2 * h:2 * h + 1],
                                    _dot(ws[2 * h + 1], zh) + sb_ref[:, 2 * h + 1:2 * h + 2]))
        mixed.append(jnp.concatenate(halves, axis=1))
    return ug, tu, tz, xhat, rstd, zn16, jnp.concatenate(mixed, axis=0)


def _causal_ws(w_ref):
    ti = lax.broadcasted_iota(jnp.int32, (CHUNK, CHUNK), 0)
    si = lax.broadcasted_iota(jnp.int32, (CHUNK, CHUNK), 1)
    causal = si <= ti
    return causal, [jnp.where(causal, w_ref[g], 0.0).astype(BF16) for g in range(N_GROUPS)]


def _gmlp_fwd(u, z, ln_g, ln_b, sgu_w, sgu_bt):
    T = u.shape[0]
    tg = TM_GMLP

    def body(u_ref, z_ref, g_ref, b_ref, w_ref, sb_ref, out_ref):
        grp = lax.broadcasted_iota(jnp.int32, (tg, GMLP_W), 1) // HEAD_DIM
        _, ws = _causal_ws(w_ref)
        ug, _, _, _, _, _, mixed = _gmlp_core(u_ref[...], z_ref[...], g_ref[...], b_ref[...], ws, sb_ref, grp)
        out_ref[...] = ug * mixed

    return pl.pallas_call(
        body, name="gmlp_fwd", grid=(T // tg,),
        in_specs=[_rows(tg, GMLP_W), _rows(tg, GMLP_W), _resident((1, GMLP_W)), _resident((1, GMLP_W)),
                  _resident((N_GROUPS, CHUNK, CHUNK)), _resident((CHUNK, N_GROUPS))],
        out_specs=_rows(tg, GMLP_W),
        out_shape=jax.ShapeDtypeStruct((T, GMLP_W), F32),
        compiler_params=_params(("parallel",)),
    )(u, z, ln_g, ln_b, sgu_w, sgu_bt)


def _out_fwd(attn, gm, ga, gg, w_out, x, g2):
    T = x.shape[0]
    tm = TM_PROJ

    def body(a_ref, m_ref, ga_ref, gg_ref, w_ref, x_ref, g2_ref, mix_ref, h1_ref, hn2_ref):
        an, _ = _rms(a_ref[...])
        gn, _ = _rms(m_ref[...])
        an = (an * ga_ref[...]).astype(BF16)
        gn = (gn * gg_ref[...]).astype(BF16)
        mix_ref[:, 0:ATTN_W] = an
        mix_ref[:, ATTN_W:] = gn
        h1 = x_ref[...] + _dot(an, w_ref[0:ATTN_W, :]) + _dot(gn, w_ref[ATTN_W:, :])
        h1_ref[...] = h1
        n2, _ = _rms(h1)
        hn2_ref[...] = (n2 * g2_ref[...]).astype(BF16)

    sds = jax.ShapeDtypeStruct
    return pl.pallas_call(
        body, name="out_fwd", grid=(T // tm,),
        in_specs=[_rows(tm, ATTN_W), _rows(tm, GMLP_W), _resident((1, ATTN_W)), _resident((1, GMLP_W)),
                  _resident((D_MODEL, D_MODEL)), _rows(tm, D_MODEL), _resident((1, D_MODEL))],
        out_specs=[_rows(tm, D_MODEL)] * 3,
        out_shape=[sds((T, D_MODEL), BF16), sds((T, D_MODEL), F32), sds((T, D_MODEL), BF16)],
        compiler_params=_params(("parallel",)),
    )(attn, gm, ga, gg, w_out, x, g2)


def _ffn_fwd(hn2, h1, w1t, w2, gf, tgt):
    T = h1.shape[0]
    tm = TM_FFN

    def body(hn_ref, h1_ref, w1_ref, w2_ref, gf_ref, t_ref, r_ref, dhf_ref, dhb_ref, loss_ref, dgf_ref):
        i = pl.program_id(0)

        @pl.when(i == 0)
        def _():
            loss_ref[...] = jnp.zeros_like(loss_ref)
            dgf_ref[...] = jnp.zeros_like(dgf_ref)

        hn = hn_ref[...]
        acc = h1_ref[...]
        for j in range(D_FF // FF_CHUNK):
            cols = slice(j * FF_CHUNK, (j + 1) * FF_CHUNK)
            r = jnp.maximum(_dot(hn, w1_ref[cols, :], NT), 0.0)
            r_ref[:, cols] = r.astype(BF16)
            act = jnp.square(r).astype(BF16)
            acc = acc + _dot(act, w2_ref[cols, :])
        n3, r3 = _rms(acc)
        gf_row = gf_ref[...]
        e = n3 * gf_row - t_ref[...]
        loss_ref[...] += 0.5 * jnp.sum(jnp.mean(e * e, axis=-1, keepdims=True))
        dy = e * (1.0 / D_MODEL)
        _accum_rows(dgf_ref, dy * n3)
        dh2 = _rms_bwd(n3, r3, gf_row, dy)
        dhf_ref[...] = dh2
        dhb_ref[...] = dh2.astype(BF16)

    sds = jax.ShapeDtypeStruct
    acc_spec = lambda n: pl.BlockSpec((8, n), lambda i: (0, 0))
    return pl.pallas_call(
        body, name="ffn_fwd", grid=(T // tm,),
        in_specs=[_rows(tm, D_MODEL), _rows(tm, D_MODEL), _resident((D_FF, D_MODEL)), _resident((D_FF, D_MODEL)),
                  _resident((1, D_MODEL)), _rows(tm, D_MODEL)],
        out_specs=[_rows(tm, D_FF), _rows(tm, D_MODEL), _rows(tm, D_MODEL), acc_spec(LANES), acc_spec(D_MODEL)],
        out_shape=[sds((T, D_FF), BF16), sds((T, D_MODEL), F32), sds((T, D_MODEL), BF16),
                   sds((8, LANES), F32), sds((8, D_MODEL), F32)],
        compiler_params=_params(("arbitrary",)),
    )(hn2, h1, w1t, w2, gf, tgt)


def _ffn_bwd(dh2b, dh2f, relu, h1, g2, w2, w1t):
    T = h1.shape[0]
    tm = TM_FFN

    def body(db_ref, df_ref, r_ref, h1_ref, g2_ref, w2_ref, w1t_ref, da_ref, d1f_ref, d1b_ref, dg_ref):
        @pl.when(pl.program_id(0) == 0)
        def _():
            dg_ref[...] = jnp.zeros_like(dg_ref)

        db = db_ref[...]
        acc = jnp.zeros((tm, D_MODEL), F32)
        for j in range(D_FF // FF_CHUNK):
            cols = slice(j * FF_CHUNK, (j + 1) * FF_CHUNK)
            da = (_dot(db, w2_ref[cols, :], NT) * (2.0 * r_ref[:, cols].astype(F32))).astype(BF16)
            da_ref[:, cols] = da
            acc = acc + _dot(da, w1t_ref[cols, :])
        n2, r2 = _rms(h1_ref[...])
        _accum_rows(dg_ref, acc * n2)
        dh1 = df_ref[...] + _rms_bwd(n2, r2, g2_ref[...], acc)
        d1f_ref[...] = dh1
        d1b_ref[...] = dh1.astype(BF16)

    sds = jax.ShapeDtypeStruct
    return pl.pallas_call(
        body, name="ffn_bwd", grid=(T // tm,),
        in_specs=[_rows(tm, D_MODEL), _rows(tm, D_MODEL), _rows(tm, D_FF), _rows(tm, D_MODEL),
                  _resident((1, D_MODEL)), _resident((D_FF, D_MODEL)), _resident((D_FF, D_MODEL))],
        out_specs=[_rows(tm, D_FF), _rows(tm, D_MODEL), _rows(tm, D_MODEL),
                   pl.BlockSpec((8, D_MODEL), lambda i: (0, 0))],
        out_shape=[sds((T, D_FF), BF16), sds((T, D_MODEL), F32), sds((T, D_MODEL), BF16), sds((8, D_MODEL), F32)],
        compiler_params=_params(("arbitrary",)),
    )(dh2b, dh2f, relu, h1, g2, w2, w1t)


def _out_bwd(dh1b, w_out, attn, gm, ga, gg):
    T = attn.shape[0]
    tm = TM_PROJ

    def body(d_ref, w_ref, a_ref, m_ref, ga_ref, gg_ref, da_ref, dm_ref, dga_ref, dgg_ref):
        @pl.when(pl.program_id(0) == 0)
        def _():
            dga_ref[...] = jnp.zeros_like(dga_ref)
            dgg_ref[...] = jnp.zeros_like(dgg_ref)

        d = d_ref[...]
        dan = _dot(d, w_ref[0:ATTN_W, :], NT)
        dgn = _dot(d, w_ref[ATTN_W:, :], NT)
        na, ra = _rms(a_ref[...])
        ng, rg = _rms(m_ref[...])
        _accum_rows(dga_ref, dan * na)
        _accum_rows(dgg_ref, dgn * ng)
        da_ref[...] = _rms_bwd(na, ra, ga_ref[...], dan)
        dm_ref[...] = _rms_bwd(ng, rg, gg_ref[...], dgn)

    sds = jax.ShapeDtypeStruct
    return pl.pallas_call(
        body, name="out_bwd", grid=(T // tm,),
        in_specs=[_rows(tm, D_MODEL), _resident((D_MODEL, D_MODEL)), _rows(tm, ATTN_W), _rows(tm, GMLP_W),
                  _resident((1, ATTN_W)), _resident((1, GMLP_W))],
        out_specs=[_rows(tm, ATTN_W), _rows(tm, GMLP_W), pl.BlockSpec((8, ATTN_W), lambda i: (0, 0)),
                   pl.BlockSpec((8, GMLP_W), lambda i: (0, 0))],
        out_shape=[sds((T, ATTN_W), F32), sds((T, GMLP_W), F32), sds((8, ATTN_W), F32), sds((8, GMLP_W), F32)],
        compiler_params=_params(("arbitrary",)),
    )(dh1b, w_out, attn, gm, ga, gg)


def _gmlp_bwd(u, z, dgm, ln_g, ln_b, sgu_w, sgu_bt):
    T = u.shape[0]
    tg = TM_GMLP
    nsteps = T // tg

    def body(u_ref, z_ref, d_ref, g_ref, b_ref, w_ref, sb_ref, dproj_hbm, dlg_ref, dlb_ref, dw_ref, dsb_ref,
             stage, sem):
        i = pl.program_id(0)
        slot = i % 2
        duz_ref = stage.at[slot]

        def to_dproj(step, buf):
            rows = pl.ds(pl.multiple_of(step * tg, tg), tg)
            return pltpu.make_async_copy(stage.at[buf], dproj_hbm.at[rows, pl.ds(3 * ATTN_W, 2 * GMLP_W)],
                                         sem.at[buf])

        @pl.when(i == 0)
        def _():
            for ref in (dlg_ref, dlb_ref, dw_ref, dsb_ref):
                ref[...] = jnp.zeros_like(ref)

        @pl.when(i >= 2)
        def _():
            to_dproj(i - 2, slot).wait()

        grp = lax.broadcasted_iota(jnp.int32, (tg, GMLP_W), 1) // HEAD_DIM
        lane = lax.broadcasted_iota(jnp.int32, (CHUNK, LANES), 1)
        causal, ws = _causal_ws(w_ref)
        lg = g_ref[...]
        uu, zz, dgm = u_ref[...], z_ref[...], d_ref[...]
        ug, tu, tz, xhat, rstd, zn16, mixed = _gmlp_core(uu, zz, lg, b_ref[...], ws, sb_ref, grp)
        dmx = dgm * ug
        duz_ref[:, 0:GMLP_W] = dgm * mixed * _gelu_grad(uu, tu)
        dmx16 = dmx.astype(BF16)
        low = grp[:CHUNK, :LANES] == 0
        dzn = []
        for ci in range(tg // CHUNK):
            rows = slice(ci * CHUNK, (ci + 1) * CHUNK)
            halves = []
            for h in range(GMLP_W // LANES):
                lanes = slice(h * LANES, (h + 1) * LANES)
                dmx_h, zn_h, zero = dmx16[rows, lanes], zn16[rows, lanes], jnp.zeros((CHUNK, LANES), BF16)
                halves.append(jnp.where(low, _dot(ws[2 * h], dmx_h, TN), _dot(ws[2 * h + 1], dmx_h, TN)))
                dw_ref[2 * h] += _dot(jnp.where(low, dmx_h, zero), zn_h, NT)
                dw_ref[2 * h + 1] += _dot(jnp.where(low, zero, dmx_h), zn_h, NT)
            dzn.append(jnp.concatenate(halves, axis=1))
        dzn = jnp.concatenate(dzn, axis=0)
        dsb = jnp.zeros((CHUNK, LANES), F32)
        for g in range(N_GROUPS):
            half = slice((g // 2) * LANES, (g // 2 + 1) * LANES)
            per_token = jnp.sum(jnp.where(grp[:, half] == g, dmx[:, half], 0.0), axis=-1, keepdims=True)
            by_position = sum(per_token[ci * CHUNK:(ci + 1) * CHUNK] for ci in range(tg // CHUNK))
            dsb = jnp.where(lane == g, by_position, dsb)
        dsb_ref[...] += dsb
        _accum_rows(dlg_ref, dzn * xhat)
        _accum_rows(dlb_ref, dzn)
        dxh = dzn * lg
        dzg = rstd * (dxh - _group_mean(dxh, grp) - xhat * _group_mean(dxh * xhat, grp))
        duz_ref[:, GMLP_W:] = dzg * _gelu_grad(zz, tz)
        to_dproj(i, slot).start()

        @pl.when(i == nsteps - 1)
        def _():
            for g in range(N_GROUPS):
                dw_ref[g] = jnp.where(causal, dw_ref[g], 0.0)
            to_dproj(i, slot).wait()
            if nsteps >= 2:
                to_dproj(i - 1, 1 - slot).wait()

    sds = jax.ShapeDtypeStruct
    return pl.pallas_call(
        body, name="gmlp_bwd", grid=(nsteps,),
        in_specs=[_rows(tg, GMLP_W)] * 3 + [_resident((1, GMLP_W)), _resident((1, GMLP_W)),
                                              _resident((N_GROUPS, CHUNK, CHUNK)), _resident((CHUNK, N_GROUPS))],
        out_specs=[_HBM, pl.BlockSpec((8, GMLP_W), lambda i: (0, 0)),
                   pl.BlockSpec((8, GMLP_W), lambda i: (0, 0)),
                   pl.BlockSpec((N_GROUPS, CHUNK, CHUNK), lambda i: (0, 0, 0)),
                   pl.BlockSpec((CHUNK, LANES), lambda i: (0, 0))],
        out_shape=[sds((T, IN_W), F32), sds((8, GMLP_W), F32), sds((8, GMLP_W), F32),
                   sds((N_GROUPS, CHUNK, CHUNK), F32), sds((CHUNK, LANES), F32)],
        scratch_shapes=[pltpu.VMEM((2, tg, 2 * GMLP_W), F32), pltpu.SemaphoreType.DMA((2,))],
        compiler_params=_params(("arbitrary",)),
    )(u, z, dgm, ln_g, ln_b, sgu_w, sgu_bt)


def _attn_bwd(q, k, v, dattn, attn, lse, dproj, owner_grads=()):
    T = q.shape[0]
    nt = T // ATT_TILE
    ns = len(owner_grads)
    steps = (ATTN_W // LANES) * nt

    def body(sl_ref, q_hbm, k_hbm, v_hbm, do_hbm, o_hbm, lse_hbm, _, *rest):
        p_refs, rest = rest[:ns], rest[ns:]
        dq_hbm = dk_hbm = dv_hbm = rest[0]
        r_refs, rest = rest[1:1 + ns], rest[1 + ns:]
        qbuf, dobuf, obuf, lbuf, kbuf, vbuf, dqbuf, dkbuf, dvbuf, delta_s = rest[:10]
        sem_q, sem_do, sem_o, sem_l, sem_k, sem_v, sem_dq, sem_dk, sem_dv = rest[10:19]
        hp, t = pl.program_id(0), pl.program_id(1)
        step = hp * nt + t
        two, three = step % 2, step % 3
        before, after = (step + 2) % 3, (step + 1) % 3
        if ns:
            start, finish = _owner_exchange_phases(p_refs, r_refs, *rest[19:])
            pl.when(step == 0)(start)

        def fetch(hp_, t_, two_, three_):
            for hbm, buf, sem, slot in ((q_hbm, qbuf, sem_q, two_), (do_hbm, dobuf, sem_do, two_),
                                        (o_hbm, obuf, sem_o, two_), (lse_hbm, lbuf, sem_l, two_),
                                        (k_hbm, kbuf, sem_k, three_), (v_hbm, vbuf, sem_v, three_)):
                for cp in _tile_copies(hbm, buf.at[slot], sem.at[slot], hp_, t_):
                    cp.start()

        @pl.when(step == 0)
        def _():
            kbuf[2] = jnp.zeros((ATT_BLOCKS, CHUNK, LANES), F32)
            vbuf[2] = jnp.zeros((ATT_BLOCKS, CHUNK, LANES), F32)
            dkbuf[3] = jnp.zeros((ATT_BLOCKS, CHUNK, LANES), F32)
            dvbuf[3] = jnp.zeros((ATT_BLOCKS, CHUNK, LANES), F32)
            fetch(0, 0, 0, 0)

        @pl.when(step + 1 < steps)
        def _():
            fetch((step + 1) // nt, (step + 1) % nt, 1 - two, after)

        for buf, sem in ((qbuf, sem_q), (dobuf, sem_do), (obuf, sem_o), (lbuf, sem_l)):
            _wait_tile(buf.at[two], sem.at[two])
        _wait_tile(kbuf.at[three], sem_k.at[three])
        _wait_tile(vbuf.at[three], sem_v.at[three])

        @pl.when(step >= 2)
        def _():
            _wait_tile(dqbuf.at[two], sem_dq.at[two])

        @pl.when(step >= 3)
        def _():
            _wait_tile(dkbuf.at[three], sem_dk.at[three])
            _wait_tile(dvbuf.at[three], sem_dv.at[three])

        q_t, do_t, l_t, k_t, v_t = qbuf.at[two], dobuf.at[two], lbuf.at[two], kbuf.at[three], vbuf.at[three]
        k_b, v_b = kbuf.at[before], vbuf.at[before]
        dq_t, dk_t, dv_t = dqbuf.at[two], dkbuf.at[three], dvbuf.at[three]
        dk_b, dv_b = dkbuf.at[before], dvbuf.at[before]
        sink = jnp.where(t > 0, before, 3)
        dk_sink, dv_sink = dkbuf.at[sink], dvbuf.at[sink]
        head0 = lax.broadcasted_iota(jnp.int32, (CHUNK, LANES), 1) < HEAD_DIM
        for r in range(ATT_BLOCKS):
            dd = dobuf[two, r] * obuf[two, r]
            d0 = jnp.sum(jnp.where(head0, dd, 0.0), axis=-1, keepdims=True)
            d1 = jnp.sum(jnp.where(head0, 0.0, dd), axis=-1, keepdims=True)
            delta_s[r] = jnp.where(head0, d0, d1)

        def column(xb):
            return jnp.concatenate([xb[:, 0:1], xb[:, HEAD_DIM:HEAD_DIM + 1]], axis=0)

        no_key_before = jnp.where(lax.broadcasted_iota(jnp.int32, (2 * CHUNK, 2 * CHUNK), 1) < CHUNK, NEG, 0.0)
        for d in DILATIONS:
            bias = _residue_bias(sl_ref, d)
            def scores(j, d=d, bias=bias):
                kcat = jnp.concatenate([_rm_block_before(k_t, k_b, d, j), _rm_block(k_t, d, j)], axis=0).astype(BF16)
                vcat = jnp.concatenate([_rm_block_before(v_t, v_b, d, j), _rm_block(v_t, d, j)], axis=0).astype(BF16)
                q2 = _stack_heads(_rm_block(q_t, d, j), head0)
                do2 = _stack_heads(_rm_block(do_t, d, j), head0)
                return (_dot(q2, kcat, NT), _dot(do2, vcat, NT), column(_rm_block(l_t, d, j)),
                        column(_rm_block(delta_s, d, j)), bias_first if _first_in_tile(d, j) else bias, kcat, q2, do2)

            bias_first = bias + jnp.where(t == 0, 1.0, 0.0) * no_key_before
            group = {}
            for j in range(ATT_BLOCKS):
                if j % SM_BLOCKS == 0:
                    group = {i: scores(i) for i in range(j, j + SM_BLOCKS)}
                    s_all, dp_all, lse_all, delta_all, bias_all = (
                        jnp.concatenate([g[i] for g in group.values()], axis=0) for i in range(5))
                    p_all = jnp.exp(s_all + bias_all - lse_all)
                    ds_all = (p_all * (dp_all - delta_all)).astype(BF16)
                    p_all = p_all.astype(BF16)
                at = slice((j % SM_BLOCKS) * 2 * CHUNK, (j % SM_BLOCKS + 1) * 2 * CHUNK)
                ds, p16 = ds_all[at, :], p_all[at, :]
                kcat, q2, do2 = group[j][5:]
                first = d == DILATIONS[0]
                _rm_add(dq_t, _residue_rows(d, j), _unstack_heads(_dot(ds, kcat), head0), first)
                ck = _dot(ds, q2, TN)
                cv = _dot(p16, do2, TN)
                _rm_add(dk_t, _residue_rows(d, j), ck[CHUNK:, :], first)
                _rm_add(dv_t, _residue_rows(d, j), cv[CHUNK:, :], first)
                if _first_in_tile(d, j):
                    rows = [(r, CHUNK - n, n) for r, _, n in _residue_rows(d, j)]
                    _rm_add(dk_sink, rows, ck[:CHUNK, :])
                    _rm_add(dv_sink, rows, cv[:CHUNK, :])
                else:
                    rows = [(r, lo - n, n) for r, lo, n in _residue_rows(d, j)]
                    _rm_add(dk_t, rows, ck[:CHUNK, :])
                    _rm_add(dv_t, rows, cv[:CHUNK, :])

        for r in range(ATT_BLOCKS):
            dqbuf[two, r] = dqbuf[two, r] * Q_SCALE
        for cp in _tile_copies(dq_hbm, dq_t, sem_dq.at[two], hp, t, to_hbm=True):
            cp.start()

        @pl.when(t > 0)
        def _():
            for cp in (_tile_copies(dk_hbm, dk_b, sem_dk.at[before], hp, t - 1, to_hbm=True, lane0=ATTN_W)
                       + _tile_copies(dv_hbm, dv_b, sem_dv.at[before], hp, t - 1, to_hbm=True, lane0=2 * ATTN_W)):
                cp.start()

        @pl.when(t == nt - 1)
        def _():
            for cp in (_tile_copies(dk_hbm, dk_t, sem_dk.at[three], hp, t, to_hbm=True, lane0=ATTN_W)
                       + _tile_copies(dv_hbm, dv_t, sem_dv.at[three], hp, t, to_hbm=True, lane0=2 * ATTN_W)):
                cp.start()

        @pl.when(step == steps - 1)
        def _():
            for slot in range(2):
                _wait_tile(dqbuf.at[slot], sem_dq.at[slot])
            for slot in range(3):
                _wait_tile(dkbuf.at[slot], sem_dk.at[slot])
                _wait_tile(dvbuf.at[slot], sem_dv.at[slot])

        if ns:
            pl.when(step == steps - 1)(finish)

    tile = lambda n: pltpu.VMEM((n, ATT_BLOCKS, CHUNK, LANES), F32)
    dma = lambda n: pltpu.SemaphoreType.DMA((n,))
    view = jax.ShapeDtypeStruct((T // ATT_BLOCKS, ATT_BLOCKS, ATTN_W), F32)
    outs = pl.pallas_call(
        body, name="attn_bwd", grid=(ATTN_W // LANES, nt),
        in_specs=[pl.BlockSpec((8, LANES), lambda c, t: (0, c))] + [_HBM] * (7 + ns),
        out_specs=[_HBM] * (1 + ns),
        out_shape=[jax.ShapeDtypeStruct((T // ATT_BLOCKS, ATT_BLOCKS, IN_W), F32)]
        + [jax.ShapeDtypeStruct(p.shape, p.dtype) for p in owner_grads],
        scratch_shapes=[tile(2), tile(2), tile(2), tile(2), tile(3), tile(3), tile(2), tile(4), tile(4),
                        pltpu.VMEM((ATT_BLOCKS, CHUNK, LANES), F32)]
        + [dma(2), dma(2), dma(2), dma(2), dma(3), dma(3), dma(2), dma(3), dma(3)]
        + (_owner_exchange_sems(ns) if ns else []),
        input_output_aliases={7: 0},
        compiler_params=_params(("arbitrary", "arbitrary")),
    )(_slope_table(), *[_residue_view(a) for a in (q, k, v, dattn, attn, lse, dproj)], *owner_grads)
    return outs[0].reshape(T, IN_W), tuple(outs[1:])


def _proj_bwd(dproj, w_in_t, x, g1, dh1, owner_grads=()):
    T = x.shape[0]
    tm = TM_PROJ
    ns = len(owner_grads)
    steps = T // tm

    def body(d_ref, w_ref, x_ref, g_ref, r_ref, *rest):
        p_refs, rest = rest[:ns], rest[ns:]
        dx_ref, dg_ref = rest[:2]
        r_refs, sems = rest[2:2 + ns], rest[2 + ns:]
        step = pl.program_id(0)
        if ns:
            start, finish = _owner_exchange_phases(p_refs, r_refs, *sems)
            pl.when(step == 0)(start)

        @pl.when(step == 0)
        def _():
            dg_ref[...] = jnp.zeros_like(dg_ref)

        dhn = _dot(d_ref[...].astype(BF16), w_ref[...])
        n1, r1 = _rms(x_ref[...])
        _accum_rows(dg_ref, dhn * n1)
        dx_ref[...] = r_ref[...] + _rms_bwd(n1, r1, g_ref[...], dhn)
        if ns:
            pl.when(step == steps - 1)(finish)

    outs = pl.pallas_call(
        body, name="proj_bwd", grid=(steps,),
        in_specs=[_rows(tm, IN_W), _resident((IN_W, D_MODEL)), _rows(tm, D_MODEL), _resident((1, D_MODEL)),
                  _rows(tm, D_MODEL)] + [_HBM] * ns,
        out_specs=[_rows(tm, D_MODEL), pl.BlockSpec((8, D_MODEL), lambda i: (0, 0))] + [_HBM] * ns,
        out_shape=[jax.ShapeDtypeStruct((T, D_MODEL), F32), jax.ShapeDtypeStruct((8, D_MODEL), F32)]
        + [jax.ShapeDtypeStruct(p.shape, p.dtype) for p in owner_grads],
        scratch_shapes=_owner_exchange_sems(ns) if ns else [],
        compiler_params=_params(("arbitrary",)),
    )(dproj, w_in_t, x, g1, dh1, *owner_grads)
    return outs[0], outs[1], tuple(outs[2:])


def _dw(a, b, name, tile, square_a=False, out_dtype=F32):
    T, ka = a.shape
    nb = b.shape[1]
    tka, tnb, tt = tile
    tt = min(tt, T)
    last = T // tt - 1

    def body(a_ref, b_ref, *refs):
        o_ref = refs[0]
        acc_ref = refs[1] if len(refs) > 1 else o_ref
        s = pl.program_id(2)

        @pl.when(s == 0)
        def _():
            acc_ref[...] = jnp.zeros_like(acc_ref)

        a_tile = a_ref[...]
        if square_a:
            a_tile = jnp.square(a_tile.astype(F32))
        acc_ref[...] += _dot(a_tile.astype(BF16), b_ref[...], TN)
        if acc_ref is not o_ref:
            @pl.when(s == last)
            def _():
                o_ref[...] = acc_ref[...].astype(out_dtype)

    return pl.pallas_call(
        body, name=name, grid=(ka // tka, nb // tnb, T // tt),
        in_specs=[pl.BlockSpec((tt, tka), lambda i, j, s: (s, i)), pl.BlockSpec((tt, tnb), lambda i, j, s: (s, j))],
        out_specs=pl.BlockSpec((tka, tnb), lambda i, j, s: (i, j)),
        out_shape=jax.ShapeDtypeStruct((ka, nb), out_dtype),
        scratch_shapes=[] if out_dtype == F32 else [pltpu.VMEM((tka, tnb), F32)],
        compiler_params=_params(("parallel", "parallel", "arbitrary")),
    )(a, b)


def _adamw_update(w, m, v, g):
    m2 = ADAM_B1 * m + (1.0 - ADAM_B1) * g
    v2 = ADAM_B2 * v + (1.0 - ADAM_B2) * jnp.square(g)
    m_hat = m2 / (1.0 - ADAM_B1 ** ADAM_STEP)
    v_hat = v2 / (1.0 - ADAM_B2 ** ADAM_STEP)
    return -ADAM_LR * (m_hat / (jnp.sqrt(v_hat) + ADAM_EPS) + ADAM_WD * w), m2, v2


def _adamw_tiny(ws, ms, vs, parts):
    n = len(ws)
    P = parts.shape[0]

    def body(*refs):
        w_refs, m_refs, v_refs, p_ref = refs[:n], refs[n:2 * n], refs[2 * n:3 * n], refs[3 * n]
        outs = refs[3 * n + 1:]

        def total(slot, rows):
            g = p_ref[0, 8 * slot:8 * slot + rows, :]
            for i in range(1, P):
                g = g + p_ref[i, 8 * slot:8 * slot + rows, :]
            return g

        for k in range(n):
            g = total(k, ws[k].shape[0])
            outs[4 * k][...] = g
            outs[4 * k + 1][...], outs[4 * k + 2][...], outs[4 * k + 3][...] = _adamw_update(
                w_refs[k][...], m_refs[k][...], v_refs[k][...], g)
        outs[4 * n][...] = total(n, 8)

    sds = jax.ShapeDtypeStruct
    return pl.pallas_call(
        body, name="adamw_tiny",
        out_shape=[sds(w.shape, F32) for w in ws for _ in range(4)] + [sds((8, LANES), F32)],
    )(*ws, *ms, *vs, parts)


def _adamw(w, m, v, parts, name, tr, transposed=False):
    R, C = w.shape
    P = parts.shape[0]

    def body(w_ref, m_ref, v_ref, p_ref, g_ref, d_ref, m2_ref, v2_ref):
        g = p_ref[0].astype(F32)
        for i in range(1, P):
            g = g + p_ref[i].astype(F32)
        if transposed:
            g = g.T
        g_ref[...] = g
        d_ref[...], m2_ref[...], v2_ref[...] = _adamw_update(w_ref[...], m_ref[...], v_ref[...], g)

    spec = _rows(tr, C)
    part_spec = (pl.BlockSpec((P, C, tr), lambda i: (0, 0, i)) if transposed
                 else pl.BlockSpec((P, tr, C), lambda i: (0, i, 0)))
    return pl.pallas_call(
        body, name=name, grid=(R // tr,),
        in_specs=[spec, spec, spec, part_spec],
        out_specs=[spec] * 4,
        out_shape=[jax.ShapeDtypeStruct((R, C), F32)] * 4,
        compiler_params=_params(("parallel",)),
    )(w, m, v, parts)


_HBM = pl.BlockSpec(memory_space=pltpu.HBM)


def _place():
    return lax.axis_index("x"), lax.axis_index("y"), lax.axis_index("c")


def _gathered_shape(shard):
    return jax.ShapeDtypeStruct((N_DEV,) + shard.shape, shard.dtype)


def _gather_sems(n):
    return [pltpu.SemaphoreType.DMA((7, n)), pltpu.SemaphoreType.DMA((7, n)), pltpu.SemaphoreType.DMA((n,))]


def _gather_phases(x_refs, out_refs, send_sems, recv_sems, local_sems):
    x, y, c = _place()
    me, sibling = (x, y, c), (x, y, 1 - c)
    chips = [(1 - x, y), (x, 1 - y), (1 - x, 1 - y)]
    arrays = range(len(x_refs))

    def slot(i, px, py, pc):
        return out_refs[i].at[4 * px + 2 * py + pc]

    def copy(i, k, block, to, own=False):
        return pltpu.make_async_remote_copy(
            src_ref=x_refs[i] if own else slot(i, *block), dst_ref=slot(i, *block),
            send_sem=send_sems.at[k, i], recv_sem=recv_sems.at[k, i], device_id=to, device_id_type=MESH)

    def mine(i):
        return pltpu.make_async_copy(x_refs[i], slot(i, *me), local_sems.at[i])

    def start():
        for i in arrays:
            mine(i).start()
            copy(i, 0, me, sibling, own=True).start()
            for j, chip in enumerate(chips):
                copy(i, 1 + j, me, (*chip, c), own=True).start()

    def forward():
        for i in arrays:
            for j, chip in enumerate(chips):
                copy(i, 1 + j, (*chip, c), me).wait_recv()
                copy(i, 4 + j, (*chip, c), sibling).start()

    def finish():
        for i in arrays:
            copy(i, 0, sibling, me).wait_recv()
            copy(i, 0, me, sibling, own=True).wait_send()
            for j, chip in enumerate(chips):
                copy(i, 4 + j, (*chip, 1 - c), me).wait_recv()
                copy(i, 1 + j, me, (*chip, c), own=True).wait_send()
                copy(i, 4 + j, (*chip, c), sibling).wait_send()
            mine(i).wait()

    return start, forward, finish


def _all_gather(shards, name):
    n = len(shards)

    def body(*refs):
        start, forward, finish = _gather_phases(refs[:n], refs[n:2 * n], *refs[2 * n:])
        start()
        forward()
        finish()

    return pl.pallas_call(
        body, name=name,
        out_shape=[_gathered_shape(s) for s in shards],
        in_specs=[_HBM] * n, out_specs=[_HBM] * n,
        scratch_shapes=_gather_sems(n),
    )(*shards)


def _owner_exchange_sems(n):
    return [pltpu.SemaphoreType.DMA((7, n)), pltpu.SemaphoreType.DMA((7, n)), pltpu.SemaphoreType.DMA((n,))]


def _owner_exchange_phases(g_refs, r_refs, send_sems, recv_sems, local_sems):
    x, y, c = _place()
    me = 4 * x + 2 * y + c
    flip = lambda v, bit: 1 - v if bit else v
    peers = [(flip(x, k & 4), flip(y, k & 2), flip(c, k & 1)) for k in range(1, N_DEV)]
    arrays = range(len(g_refs))

    def mine(i):
        return pltpu.make_async_copy(g_refs[i].at[me], r_refs[i].at[me], local_sems.at[i])

    def copy(i, k, src_slot, dst_slot):
        return pltpu.make_async_remote_copy(
            src_ref=g_refs[i].at[src_slot], dst_ref=r_refs[i].at[dst_slot],
            send_sem=send_sems.at[k, i], recv_sem=recv_sems.at[k, i], device_id=peers[k], device_id_type=MESH)

    def start():
        for i in arrays:
            mine(i).start()
            for k, (px, py, pc) in enumerate(peers):
                copy(i, k, 4 * px + 2 * py + pc, me).start()

    def finish():
        for i in arrays:
            for k, (px, py, pc) in enumerate(peers):
                copy(i, k, me, 4 * px + 2 * py + pc).wait_recv()
                copy(i, k, 4 * px + 2 * py + pc, me).wait_send()
            mine(i).wait()

    return start, finish


def _local_step(x, tgt, small, w_in_t, rest, exchange=False):
    g1, g2, gf = small["norm1_g"], small["norm2_g"], small["final_norm_g"].reshape(1, D_MODEL)
    ga, gg = small["attn_out_g"], small["gmlp_out_g"]
    ln_g = small["sgu_ln_g"].reshape(1, GMLP_W)
    ln_b = small["sgu_ln_b"].reshape(1, GMLP_W)
    sgu_w = small["sgu_w"][0]
    sgu_bt = small["sgu_b"][0].T

    hn1, q, k, v, u, z = _proj_fwd(x, g1, w_in_t)
    attn, lse, gathered = _attn_fwd(q, k, v, shards=rest if exchange else ())
    w_out, w_ff1_t, w_ff2 = [g.reshape(-1, D_MODEL) for g in gathered] if exchange else rest
    gm = _gmlp_fwd(u, z, ln_g, ln_b, sgu_w, sgu_bt)
    mixed, h1, hn2 = _out_fwd(attn, gm, ga, gg, w_out, x, g2)
    relu, dh2f, dh2b, loss8, dgf8 = _ffn_fwd(hn2, h1, w_ff1_t, w_ff2, gf, tgt)

    da, dh1f, dh1b, dg2 = _ffn_bwd(dh2b, dh2f, relu, h1, g2, w_ff2, w_ff1_t)
    wire = BF16 if exchange else F32
    dw_ff2 = _dw(relu, dh2b, "dw_ff2", DW_TILE, square_a=True, out_dtype=wire)
    dw_ff1_t = _dw(da, hn2, "dw_ff1", DW_TILE, out_dtype=wire)
    dattn, dgm, dga, dgg = _out_bwd(dh1b, w_out, attn, gm, ga, gg)
    dw_out = _dw(mixed, dh1b, "dw_out", DW_TILE, out_dtype=wire)
    early = [dw_out, dw_ff1_t, dw_ff2]
    if exchange:
        early = [g.reshape(N_DEV, -1, D_MODEL) for g in early]
    dproj, dlg, dlb, dsw, dsb = _gmlp_bwd(u, z, dgm, ln_g, ln_b, sgu_w, sgu_bt)
    dproj, arrived = _attn_bwd(q, k, v, dattn, attn, lse, dproj, owner_grads=early if exchange else ())
    dw_in_t = _dw(dproj, hn1, "dw_in", DW_TILE_IN, out_dtype=wire)
    late = (dw_in_t.reshape(N_DEV, -1, D_MODEL),) if exchange else ()
    dx, dg1, late = _proj_bwd(dproj, w_in_t, x, g1, dh1f, owner_grads=late)
    if exchange:
        dw_in_t, early = late[0], arrived

    small_grads = dict(
        norm1_g=dg1[0], sgu_ln_g=dlg[0], sgu_ln_b=dlb[0], sgu_w=dsw, sgu_b=dsb[:, :N_GROUPS].T,
        attn_out_g=dga[0], gmlp_out_g=dgg[0], norm2_g=dg2[0], final_norm_g=dgf8[0])
    return loss8[0, 0], dx, (dw_in_t, *early), small_grads


SMALL_NAMES = ("norm1_g", "sgu_ln_g", "sgu_ln_b", "sgu_w", "sgu_b", "attn_out_g", "gmlp_out_g", "norm2_g",
               "final_norm_g")
WEIGHT_ORDER = ("norm1_g", "w_in", "sgu_ln_g", "sgu_ln_b", "sgu_w", "sgu_b", "attn_out_g", "gmlp_out_g", "w_out",
                "norm2_g", "w_ff1", "w_ff2", "final_norm_g")


TINY_NAMES = tuple(n for n in SMALL_NAMES if n != "sgu_w")


def _as_rows(a):
    return a.reshape(-1, LANES)


def _pack_tiny_grads(d, loss):
    slots = [jnp.pad(_as_rows(d[n]), ((0, 8 - d[n].size // LANES), (0, 0))) for n in TINY_NAMES]
    return jnp.concatenate(slots + [jnp.full((8, LANES), loss, F32)], axis=0)


def kernel(x, norm1_g, w_in, sgu_ln_g, sgu_ln_b, sgu_w, sgu_b, attn_out_g, gmlp_out_g, w_out, norm2_g, w_ff1, w_ff2, final_norm_g, loss_target, m_norm1_g, m_w_in, m_sgu_ln_g, m_sgu_ln_b, m_sgu_w, m_sgu_b, m_attn_out_g, m_gmlp_out_g, m_w_out, m_norm2_g, m_w_ff1, m_w_ff2, m_final_norm_g, v_norm1_g, v_w_in, v_sgu_ln_g, v_sgu_ln_b, v_sgu_w, v_sgu_b, v_attn_out_g, v_gmlp_out_g, v_w_out, v_norm2_g, v_w_ff1, v_w_ff2, v_final_norm_g):
    w = dict(norm1_g=norm1_g, w_in=w_in, sgu_ln_g=sgu_ln_g, sgu_ln_b=sgu_ln_b, sgu_w=sgu_w, sgu_b=sgu_b,
             attn_out_g=attn_out_g, gmlp_out_g=gmlp_out_g, w_out=w_out, norm2_g=norm2_g, w_ff1=w_ff1, w_ff2=w_ff2,
             final_norm_g=final_norm_g)
    m = dict(norm1_g=m_norm1_g, w_in=m_w_in, sgu_ln_g=m_sgu_ln_g, sgu_ln_b=m_sgu_ln_b, sgu_w=m_sgu_w, sgu_b=m_sgu_b,
             attn_out_g=m_attn_out_g, gmlp_out_g=m_gmlp_out_g, w_out=m_w_out, norm2_g=m_norm2_g, w_ff1=m_w_ff1,
             w_ff2=m_w_ff2, final_norm_g=m_final_norm_g)
    v = dict(norm1_g=v_norm1_g, w_in=v_w_in, sgu_ln_g=v_sgu_ln_g, sgu_ln_b=v_sgu_ln_b, sgu_w=v_sgu_w, sgu_b=v_sgu_b,
             attn_out_g=v_attn_out_g, gmlp_out_g=v_gmlp_out_g, w_out=v_w_out, norm2_g=v_norm2_g, w_ff1=v_w_ff1,
             w_ff2=v_w_ff2, final_norm_g=v_final_norm_g)
    big = ("w_in", "w_out", "w_ff1", "w_ff2")

    w_in_t, = _all_gather([w_in[0].T.astype(BF16)], "w_in_all_gather")
    rest = (w_out[0].astype(BF16), w_ff1[0].T.astype(BF16), w_ff2[0].astype(BF16))
    loss, dx, parts, small_grads = _local_step(x[0], loss_target[0], {n: w[n] for n in SMALL_NAMES},
                                               w_in_t.reshape(IN_W, D_MODEL), rest, exchange=True)

    new = {}
    for n, p, transposed, tr in zip(big, parts, (True, False, True, False), (128, 128, 128, 256)):
        new[n] = [a[None] for a in _adamw(w[n][0], m[n][0], v[n][0], p, "adamw_" + n, tr, transposed)]

    tiny_parts, sgu_parts = _all_gather(
        [_pack_tiny_grads(small_grads, loss), _as_rows(small_grads["sgu_w"]).astype(BF16)], "small_grad_all_gather")
    tiny = _adamw_tiny(*[[_as_rows(src[n]) for n in TINY_NAMES] for src in (w, m, v)], tiny_parts)
    sgu = _adamw(_as_rows(sgu_w), _as_rows(m_sgu_w), _as_rows(v_sgu_w), sgu_parts, "adamw_sgu_w", 512)
    loss = tiny[-1][0, 0]

    outs = []
    for i in range(4):
        d = {n: new[n][i] for n in big}
        d.update({n: tiny[4 * k + i].reshape(w[n].shape) for k, n in enumerate(TINY_NAMES)})
        d["sgu_w"] = sgu[i].reshape(sgu_w.shape)
        outs.extend(d[n] for n in WEIGHT_ORDER)
    return (loss, dx[None], *outs)
```

```python
import math

import numpy as np
import jax
import jax.numpy as jnp
from jax import lax
from jax.experimental import pallas as pl
from jax.experimental.pallas import tpu as pltpu

F32 = jnp.float32
BF16 = jnp.bfloat16

D_MODEL = 1024
HEAD_DIM = 64
N_HEADS = 12
ATTN_W = N_HEADS * HEAD_DIM
N_GROUPS = 4
GMLP_W = N_GROUPS * HEAD_DIM
IN_W = 3 * ATTN_W + 2 * GMLP_W
D_FF = 4 * D_MODEL
CHUNK = 128
DILATIONS = (1, 4, 16)
EPS = 1e-6
Q_SCALE = HEAD_DIM ** -0.5
LOG2E = 1.4426950408889634
NEG = -1e30

ADAM_LR, ADAM_B1, ADAM_B2, ADAM_EPS, ADAM_WD, ADAM_STEP = 0.001, 0.9, 0.999, 1e-08, 0.01, 10

N_DEV = 8
LANES = 128
VMEM_LIMIT = 56 << 20

TM_PROJ = 512
TM_FFN = 512
FF_CHUNK = 512
TM_GMLP = 1024
DW_TILE = (512, 1024, 8192)
DW_TILE_IN = (IN_W // 2, 1024, 2048)

MESH = pl.DeviceIdType.MESH


def _alibi_slopes(n):
    def pow2(m):
        start = 2.0 ** (-8.0 / m)
        return [start ** (i + 1) for i in range(m)]
    c = 2 ** int(math.floor(math.log2(n)))
    s = pow2(n) if c == n else pow2(c) + pow2(2 * c)[0::2][: n - c]
    return np.asarray(s, dtype=np.float32)


SLOPES = _alibi_slopes(N_HEADS)


def _params(sem=None):
    kw = dict(vmem_limit_bytes=VMEM_LIMIT)
    if sem is not None:
        kw["dimension_semantics"] = sem
    return pltpu.CompilerParams(**kw)


def _rows(tm, n):
    return pl.BlockSpec((tm, n), lambda i: (i, 0))


def _resident(shape):
    return pl.BlockSpec(shape, lambda *_: (0,) * len(shape), pipeline_mode=pl.Buffered(1))


def _rms(x):
    r = lax.rsqrt(jnp.mean(x * x, axis=-1, keepdims=True) + EPS)
    return x * r, r


def _rms_bwd(n, r, g, dy):
    dn = dy * g
    return r * (dn - n * jnp.mean(dn * n, axis=-1, keepdims=True))


def _accum_rows(acc_ref, v):
    acc_ref[...] += jnp.broadcast_to(jnp.sum(v, axis=0, keepdims=True), acc_ref.shape)


_G0 = math.sqrt(2.0 / math.pi)
_G1 = 0.044715


def _gelu(x):
    t = jnp.tanh(_G0 * (x + _G1 * (x * x * x)))
    return x * (0.5 * (1.0 + t)), t


def _gelu_grad(x, t):
    return 0.5 * (1.0 + t) + 0.5 * x * (1.0 - t * t) * (_G0 * (1.0 + 3.0 * _G1 * x * x))


NT = (((1,), (1,)), ((), ()))
TN = (((0,), (0,)), ((), ()))


def _dot(a, b, dims=None):
    if dims is None:
        return jnp.dot(a, b, preferred_element_type=F32)
    return lax.dot_general(a, b, dims, preferred_element_type=F32)


def _proj_fwd(x, g1, w_in_t):
    T = x.shape[0]
    tm = TM_PROJ

    def body(x_ref, g_ref, w_ref, hn_ref, q_ref, k_ref, v_ref, u_ref, z_ref):
        n, _ = _rms(x_ref[...])
        hn = (n * g_ref[...]).astype(BF16)
        hn_ref[...] = hn
        a = ATTN_W
        q_ref[...] = _dot(hn, w_ref[0:a, :], NT) * Q_SCALE
        k_ref[...] = _dot(hn, w_ref[a:2 * a, :], NT) * LOG2E
        v_ref[...] = _dot(hn, w_ref[2 * a:3 * a, :], NT)
        u_ref[...] = _dot(hn, w_ref[3 * a:3 * a + GMLP_W, :], NT)
        z_ref[...] = _dot(hn, w_ref[3 * a + GMLP_W:, :], NT)

    sds = jax.ShapeDtypeStruct
    return pl.pallas_call(
        body, name="proj_fwd", grid=(T // tm,),
        in_specs=[_rows(tm, D_MODEL), _resident((1, D_MODEL)), _resident((IN_W, D_MODEL))],
        out_specs=[_rows(tm, D_MODEL), _rows(tm, ATTN_W), _rows(tm, ATTN_W), _rows(tm, ATTN_W),
                   _rows(tm, GMLP_W), _rows(tm, GMLP_W)],
        out_shape=[sds((T, D_MODEL), BF16), sds((T, ATTN_W), F32), sds((T, ATTN_W), F32),
                   sds((T, ATTN_W), F32), sds((T, GMLP_W), F32), sds((T, GMLP_W), F32)],
        compiler_params=_params(("parallel",)),
    )(x, g1, w_in_t)


ATT_TILE = 2048
ATT_BLOCKS = ATT_TILE // CHUNK
SM_BLOCKS = 4


def _slope_table():
    row = np.repeat(SLOPES, HEAD_DIM)
    return jnp.asarray(np.broadcast_to(row[None], (8, ATTN_W)), F32)


def _residue_view(a):
    return a.reshape(a.shape[0] // ATT_BLOCKS, ATT_BLOCKS, a.shape[1])


def _tile_copies(hbm, buf, sem, hp, t, to_hbm=False, lane0=0):
    rows = pl.ds(pl.multiple_of(t * CHUNK, CHUNK), CHUNK)
    lanes = pl.ds(pl.multiple_of(lane0 + hp * LANES, LANES), LANES)
    pairs = [(hbm.at[rows, r, lanes], buf.at[r]) for r in range(ATT_BLOCKS)]
    return [pltpu.make_async_copy(v, h, sem) if to_hbm else pltpu.make_async_copy(h, v, sem) for h, v in pairs]


def _wait_tile(buf, sem):
    pltpu.make_async_copy(buf, buf, sem).wait()


def _residue_rows(d, j):
    if d == 16:
        return [(j, 0, CHUNK)]
    if d == 4:
        return [(j % 4 + 4 * m, 32 * (j // 4), 32) for m in range(4)]
    return [(r, 8 * j, 8) for r in range(ATT_BLOCKS)]


def _block_order(p, d):
    if d == 16:
        return p
    if d == 4:
        return 4 * (p & 31) + (p >> 5)
    return 16 * (p & 7) + (p >> 3)


def _first_in_tile(d, j):
    return _residue_rows(d, j)[0][1] == 0


def _rm_block(buf, d, j):
    return jnp.concatenate([buf[r, lo:lo + n, :] for r, lo, n in _residue_rows(d, j)], axis=0)


def _rm_block_before(buf, buf_before, d, j):
    if _first_in_tile(d, j):
        return jnp.concatenate([buf_before[r, CHUNK - n:CHUNK, :] for r, _, n in _residue_rows(d, j)], axis=0)
    return jnp.concatenate([buf[r, lo - n:lo, :] for r, lo, n in _residue_rows(d, j)], axis=0)


def _rm_store(buf, d, j, val):
    at = 0
    for r, lo, n in _residue_rows(d, j):
        buf[r, lo:lo + n, :] = val[at:at + n, :]
        at += n


def _rm_add(buf, rows, val, first=False):
    at = 0
    for r, lo, n in rows:
        if first:
            buf[r, lo:lo + n, :] = val[at:at + n, :]
        else:
            buf[r, lo:lo + n, :] += val[at:at + n, :]
        at += n


def _residue_bias(sl_ref, d):
    shape = (2 * CHUNK, 2 * CHUNK)
    row = lax.broadcasted_iota(jnp.int32, shape, 0)
    col = lax.broadcasted_iota(jnp.int32, shape, 1)
    steps = _block_order(row & (CHUNK - 1), d) + CHUNK - (_block_order(col & (CHUNK - 1), d) + (col & CHUNK))
    band = (steps >= 0) & (steps <= CHUNK)
    sl = sl_ref[0:1, :]
    upper = lax.broadcasted_iota(jnp.int32, (2 * CHUNK, 1), 0) < CHUNK
    slope2 = jnp.where(upper, sl[:, 0:1], sl[:, HEAD_DIM:HEAD_DIM + 1])
    return jnp.where(band, -(float(d) * LOG2E * slope2 * steps.astype(F32)), NEG)


def _stack_heads(xb, head0):
    zero = jnp.zeros_like(xb)
    return jnp.concatenate([jnp.where(head0, xb, zero), jnp.where(head0, zero, xb)], axis=0).astype(BF16)


def _unstack_heads(x2, head0):
    return jnp.where(head0, x2[:CHUNK, :], x2[CHUNK:, :])


def _attn_fwd(q, k, v, shards=()):
    T = q.shape[0]
    nt = T // ATT_TILE
    ns = len(shards)
    steps = (ATTN_W // LANES) * nt

    def body(sl_ref, q_hbm, k_hbm, v_hbm, *rest):
        x_refs, rest = rest[:ns], rest[ns:]
        attn_hbm, lse_hbm = rest[:2]
        g_refs, rest = rest[2:2 + ns], rest[2 + ns:]
        qbuf, kbuf, vbuf, obuf, lbuf = rest[:5]
        o_acc, l_acc = rest[5:8], rest[8:11]
        sem_q, sem_k, sem_v, sem_o, sem_l = rest[11:16]
        hp, t = pl.program_id(0), pl.program_id(1)
        step = hp * nt + t
        two, three = step % 2, step % 3
        before, after = (step + 2) % 3, (step + 1) % 3
        if ns:
            start, forward, finish = _gather_phases(x_refs, g_refs, *rest[16:])
            pl.when(step == 0)(start)
            pl.when(step == (3 * steps) // 4)(forward)

        def fetch(hp_, t_, two_, three_):
            for cp in (_tile_copies(q_hbm, qbuf.at[two_], sem_q.at[two_], hp_, t_)
                       + _tile_copies(k_hbm, kbuf.at[three_], sem_k.at[three_], hp_, t_)
                       + _tile_copies(v_hbm, vbuf.at[three_], sem_v.at[three_], hp_, t_)):
                cp.start()

        @pl.when(step == 0)
        def _():
            kbuf[2] = jnp.zeros((ATT_BLOCKS, CHUNK, LANES), F32)
            vbuf[2] = jnp.zeros((ATT_BLOCKS, CHUNK, LANES), F32)
            fetch(0, 0, 0, 0)

        @pl.when(step + 1 < steps)
        def _():
            fetch((step + 1) // nt, (step + 1) % nt, 1 - two, after)

        _wait_tile(qbuf.at[two], sem_q.at[two])
        _wait_tile(kbuf.at[three], sem_k.at[three])
        _wait_tile(vbuf.at[three], sem_v.at[three])

        @pl.when(step >= 2)
        def _():
            _wait_tile(obuf.at[two], sem_o.at[two])
            _wait_tile(lbuf.at[two], sem_l.at[two])

        q_t, k_t, v_t = qbuf.at[two], kbuf.at[three], vbuf.at[three]
        k_b, v_b = kbuf.at[before], vbuf.at[before]
        head0 = lax.broadcasted_iota(jnp.int32, (CHUNK, LANES), 1) < HEAD_DIM
        no_key_before = jnp.where(lax.broadcasted_iota(jnp.int32, (2 * CHUNK, 2 * CHUNK), 1) < CHUNK, NEG, 0.0)
        for pi, d in enumerate(DILATIONS):
            bias = _residue_bias(sl_ref, d)

            def scores(j, d=d, bias=bias):
                kcat = jnp.concatenate([_rm_block_before(k_t, k_b, d, j), _rm_block(k_t, d, j)], axis=0).astype(BF16)
                vcat = jnp.concatenate([_rm_block_before(v_t, v_b, d, j), _rm_block(v_t, d, j)], axis=0).astype(BF16)
                s = _dot(_stack_heads(_rm_block(q_t, d, j), head0), kcat, NT)
                return s, vcat, bias_first if _first_in_tile(d, j) else bias

            bias_first = bias + jnp.where(t == 0, 1.0, 0.0) * no_key_before
            for j0 in range(0, ATT_BLOCKS, SM_BLOCKS):
                group = [scores(j) for j in range(j0, j0 + SM_BLOCKS)]
                s = jnp.concatenate([g[0] for g in group], axis=0) + jnp.concatenate([g[2] for g in group], axis=0)
                m = jnp.max(s, axis=-1, keepdims=True)
                p = jnp.exp2(s - m)
                l = jnp.sum(p, axis=-1, keepdims=True)
                p = p.astype(BF16)
                block = lambda a, i: a[i * 2 * CHUNK:(i + 1) * 2 * CHUNK, :]
                o = jnp.concatenate([_dot(block(p, i), g[1]) for i, g in enumerate(group)], axis=0) * (1.0 / l)
                lse = jnp.broadcast_to(m + jnp.log2(l), o.shape)
                for i in range(SM_BLOCKS):
                    _rm_store(o_acc[pi], d, j0 + i, _unstack_heads(block(o, i), head0))
                    _rm_store(l_acc[pi], d, j0 + i, _unstack_heads(block(lse, i), head0))

        for r in range(ATT_BLOCKS):
            a, b, c = l_acc[0][r], l_acc[1][r], l_acc[2][r]
            m = jnp.maximum(jnp.maximum(a, b), c)
            ea, eb, ec = jnp.exp2(a - m), jnp.exp2(b - m), jnp.exp2(c - m)
            tot = ea + eb + ec
            obuf[two, r] = (ea * o_acc[0][r] + eb * o_acc[1][r] + ec * o_acc[2][r]) / tot
            lbuf[two, r] = m + jnp.log2(tot)

        for cp in (_tile_copies(attn_hbm, obuf.at[two], sem_o.at[two], hp, t, to_hbm=True)
                   + _tile_copies(lse_hbm, lbuf.at[two], sem_l.at[two], hp, t, to_hbm=True)):
            cp.start()

        @pl.when(step == steps - 1)
        def _():
            for slot in (two, 1 - two)[:min(steps, 2)]:
                _wait_tile(obuf.at[slot], sem_o.at[slot])
                _wait_tile(lbuf.at[slot], sem_l.at[slot])

        if ns:
            pl.when(step == steps - 1)(finish)

    tile = lambda n: pltpu.VMEM((n, ATT_BLOCKS, CHUNK, LANES), F32)
    dma = lambda n: pltpu.SemaphoreType.DMA((n,))
    view = jax.ShapeDtypeStruct((T // ATT_BLOCKS, ATT_BLOCKS, ATTN_W), F32)
    outs = pl.pallas_call(
        body, name="attn_fwd", grid=(ATTN_W // LANES, nt),
        in_specs=[pl.BlockSpec((8, LANES), lambda c, t: (0, c))] + [_HBM] * (3 + ns),
        out_specs=[_HBM] * (2 + ns),
        out_shape=[view, view] + [_gathered_shape(s) for s in shards],
        scratch_shapes=[tile(2), tile(3), tile(3), tile(2), tile(2)] + [pltpu.VMEM((ATT_BLOCKS, CHUNK, LANES), F32)] * 6
        + [dma(2), dma(3), dma(3), dma(2), dma(2)] + (_gather_sems(ns) if ns else []),
        compiler_params=_params(("arbitrary", "arbitrary")),
    )(_slope_table(), _residue_view(q), _residue_view(k), _residue_view(v), *shards)
    return outs[0].reshape(T, ATTN_W), outs[1].reshape(T, ATTN_W), tuple(outs[2:])


def _group_mean(v, grp):
    halves = []
    for h in range(GMLP_W // LANES):
        x = v[:, h * LANES:(h + 1) * LANES]
        low = grp[:, h * LANES:(h + 1) * LANES] == 2 * h
        a = jnp.sum(jnp.where(low, x, 0.0), axis=-1, keepdims=True)
        b = jnp.sum(jnp.where(low, 0.0, x), axis=-1, keepdims=True)
        halves.append(jnp.where(low, a, b) * (1.0 / HEAD_DIM))
    return jnp.concatenate(halves, axis=1)


def _gmlp_core(uu, zz, lg, lb, ws, sb_ref, grp):
    ug, tu = _gelu(uu)
    zg, tz = _gelu(zz)
    zc = zg - _group_mean(zg, grp)
    rstd = lax.rsqrt(_group_mean(zc * zc, grp) + EPS)
    xhat = zc * rstd
    zn16 = (xhat * lg + lb).astype(BF16)
    low = grp[:CHUNK, :LANES] == 0
    mixed = []
    for ci in range(uu.shape[0] // CHUNK):
        rows = slice(ci * CHUNK, (ci + 1) * CHUNK)
        halves = []
        for h in range(GMLP_W // LANES):
            zh = zn16[rows, h * LANES:(h + 1) * LANES]
            halves.append(jnp.where(low, _dot(ws[2 * h], zh) + sb_ref[:, 2 * h:2 * h + 1],
                                    _dot(ws[2 * h + 1], zh) + sb_ref[:, 2 * h + 1:2 * h + 2]))
        mixed.append(jnp.concatenate(halves, axis=1))
    return ug, tu, tz, xhat, rstd, zn16, jnp.concatenate(mixed, axis=0)


def _causal_ws(w_ref):
    ti = lax.broadcasted_iota(jnp.int32, (CHUNK, CHUNK), 0)
    si = lax.broadcasted_iota(jnp.int32, (CHUNK, CHUNK), 1)
    causal = si <= ti
    return causal, [jnp.where(causal, w_ref[g], 0.0).astype(BF16) for g in range(N_GROUPS)]


def _gmlp_fwd(u, z, ln_g, ln_b, sgu_w, sgu_bt):
    T = u.shape[0]
    tg = TM_GMLP

    def body(u_ref, z_ref, g_ref, b_ref, w_ref, sb_ref, out_ref):
        grp = lax.broadcasted_iota(jnp.int32, (tg, GMLP_W), 1) // HEAD_DIM
        _, ws = _causal_ws(w_ref)
        ug, _, _, _, _, _, mixed = _gmlp_core(u_ref[...], z_ref[...], g_ref[...], b_ref[...], ws, sb_ref, grp)
        out_ref[...] = ug * mixed

    return pl.pallas_call(
        body, name="gmlp_fwd", grid=(T // tg,),
        in_specs=[_rows(tg, GMLP_W), _rows(tg, GMLP_W), _resident((1, GMLP_W)), _resident((1, GMLP_W)),
                  _resident((N_GROUPS, CHUNK, CHUNK)), _resident((CHUNK, N_GROUPS))],
        out_specs=_rows(tg, GMLP_W),
        out_shape=jax.ShapeDtypeStruct((T, GMLP_W), F32),
        compiler_params=_params(("parallel",)),
    )(u, z, ln_g, ln_b, sgu_w, sgu_bt)


def _out_fwd(attn, gm, ga, gg, w_out, x, g2):
    T = x.shape[0]
    tm = TM_PROJ

    def body(a_ref, m_ref, ga_ref, gg_ref, w_ref, x_ref, g2_ref, mix_ref, h1_ref, hn2_ref):
        an, _ = _rms(a_ref[...])
        gn, _ = _rms(m_ref[...])
        an = (an * ga_ref[...]).astype(BF16)
        gn = (gn * gg_ref[...]).astype(BF16)
        mix_ref[:, 0:ATTN_W] = an
        mix_ref[:, ATTN_W:] = gn
        h1 = x_ref[...] + _dot(an, w_ref[0:ATTN_W, :]) + _dot(gn, w_ref[ATTN_W:, :])
        h1_ref[...] = h1
        n2, _ = _rms(h1)
        hn2_ref[...] = (n2 * g2_ref[...]).astype(BF16)

    sds = jax.ShapeDtypeStruct
    return pl.pallas_call(
        body, name="out_fwd", grid=(T // tm,),
        in_specs=[_rows(tm, ATTN_W), _rows(tm, GMLP_W), _resident((1, ATTN_W)), _resident((1, GMLP_W)),
                  _resident((D_MODEL, D_MODEL)), _rows(tm, D_MODEL), _resident((1, D_MODEL))],
        out_specs=[_rows(tm, D_MODEL)] * 3,
        out_shape=[sds((T, D_MODEL), BF16), sds((T, D_MODEL), F32), sds((T, D_MODEL), BF16)],
        compiler_params=_params(("parallel",)),
    )(attn, gm, ga, gg, w_out, x, g2)


def _ffn_fwd(hn2, h1, w1t, w2, gf, tgt):
    T = h1.shape[0]
    tm = TM_FFN

    def body(hn_ref, h1_ref, w1_ref, w2_ref, gf_ref, t_ref, r_ref, dhf_ref, dhb_ref, loss_ref, dgf_ref):
        i = pl.program_id(0)

        @pl.when(i == 0)
        def _():
            loss_ref[...] = jnp.zeros_like(loss_ref)
            dgf_ref[...] = jnp.zeros_like(dgf_ref)

        hn = hn_ref[...]
        acc = h1_ref[...]
        for j in range(D_FF // FF_CHUNK):
            cols = slice(j * FF_CHUNK, (j + 1) * FF_CHUNK)
            r = jnp.maximum(_dot(hn, w1_ref[cols, :], NT), 0.0)
            r_ref[:, cols] = r.astype(BF16)
            act = jnp.square(r).astype(BF16)
            acc = acc + _dot(act, w2_ref[cols, :])
        n3, r3 = _rms(acc)
        gf_row = gf_ref[...]
        e = n3 * gf_row - t_ref[...]
        loss_ref[...] += 0.5 * jnp.sum(jnp.mean(e * e, axis=-1, keepdims=True))
        dy = e * (1.0 / D_MODEL)
        _accum_rows(dgf_ref, dy * n3)
        dh2 = _rms_bwd(n3, r3, gf_row, dy)
        dhf_ref[...] = dh2
        dhb_ref[...] = dh2.astype(BF16)

    sds = jax.ShapeDtypeStruct
    acc_spec = lambda n: pl.BlockSpec((8, n), lambda i: (0, 0))
    return pl.pallas_call(
        body, name="ffn_fwd", grid=(T // tm,),
        in_specs=[_rows(tm, D_MODEL), _rows(tm, D_MODEL), _resident((D_FF, D_MODEL)), _resident((D_FF, D_MODEL)),
                  _resident((1, D_MODEL)), _rows(tm, D_MODEL)],
        out_specs=[_rows(tm, D_FF), _rows(tm, D_MODEL), _rows(tm, D_MODEL), acc_spec(LANES), acc_spec(D_MODEL)],
        out_shape=[sds((T, D_FF), BF16), sds((T, D_MODEL), F32), sds((T, D_MODEL), BF16),
                   sds((8, LANES), F32), sds((8, D_MODEL), F32)],
        compiler_params=_params(("arbitrary",)),
    )(hn2, h1, w1t, w2, gf, tgt)


def _ffn_bwd(dh2b, dh2f, relu, h1, g2, w2, w1t):
    T = h1.shape[0]
    tm = TM_FFN

    def body(db_ref, df_ref, r_ref, h1_ref, g2_ref, w2_ref, w1t_ref, da_ref, d1f_ref, d1b_ref, dg_ref):
        @pl.when(pl.program_id(0) == 0)
        def _():
            dg_ref[...] = jnp.zeros_like(dg_ref)

        db = db_ref[...]
        acc = jnp.zeros((tm, D_MODEL), F32)
        for j in range(D_FF // FF_CHUNK):
            cols = slice(j * FF_CHUNK, (j + 1) * FF_CHUNK)
            da = (_dot(db, w2_ref[cols, :], NT) * (2.0 * r_ref[:, cols].astype(F32))).astype(BF16)
            da_ref[:, cols] = da
            acc = acc + _dot(da, w1t_ref[cols, :])
        n2, r2 = _rms(h1_ref[...])
        _accum_rows(dg_ref, acc * n2)
        dh1 = df_ref[...] + _rms_bwd(n2, r2, g2_ref[...], acc)
        d1f_ref[...] = dh1
        d1b_ref[...] = dh1.astype(BF16)

    sds = jax.ShapeDtypeStruct
    return pl.pallas_call(
        body, name="ffn_bwd", grid=(T // tm,),
        in_specs=[_rows(tm, D_MODEL), _rows(tm, D_MODEL), _rows(tm, D_FF), _rows(tm, D_MODEL),
                  _resident((1, D_MODEL)), _resident((D_FF, D_MODEL)), _resident((D_FF, D_MODEL))],
        out_specs=[_rows(tm, D_FF), _rows(tm, D_MODEL), _rows(tm, D_MODEL),
                   pl.BlockSpec((8, D_MODEL), lambda i: (0, 0))],
        out_shape=[sds((T, D_FF), BF16), sds((T, D_MODEL), F32), sds((T, D_MODEL), BF16), sds((8, D_MODEL), F32)],
        compiler_params=_params(("arbitrary",)),
    )(dh2b, dh2f, relu, h1, g2, w2, w1t)


def _out_bwd(dh1b, w_out, attn, gm, ga, gg):
    T = attn.shape[0]
    tm = TM_PROJ

    def body(d_ref, w_ref, a_ref, m_ref, ga_ref, gg_ref, da_ref, dm_ref, dga_ref, dgg_ref):
        @pl.when(pl.program_id(0) == 0)
        def _():
            dga_ref[...] = jnp.zeros_like(dga_ref)
            dgg_ref[...] = jnp.zeros_like(dgg_ref)

        d = d_ref[...]
        dan = _dot(d, w_ref[0:ATTN_W, :], NT)
        dgn = _dot(d, w_ref[ATTN_W:, :], NT)
        na, ra = _rms(a_ref[...])
        ng, rg = _rms(m_ref[...])
        _accum_rows(dga_ref, dan * na)
        _accum_rows(dgg_ref, dgn * ng)
        da_ref[...] = _rms_bwd(na, ra, ga_ref[...], dan)
        dm_ref[...] = _rms_bwd(ng, rg, gg_ref[...], dgn)

    sds = jax.ShapeDtypeStruct
    return pl.pallas_call(
        body, name="out_bwd", grid=(T // tm,),
        in_specs=[_rows(tm, D_MODEL), _resident((D_MODEL, D_MODEL)), _rows(tm, ATTN_W), _rows(tm, GMLP_W),
                  _resident((1, ATTN_W)), _resident((1, GMLP_W))],
        out_specs=[_rows(tm, ATTN_W), _rows(tm, GMLP_W), pl.BlockSpec((8, ATTN_W), lambda i: (0, 0)),
                   pl.BlockSpec((8, GMLP_W), lambda i: (0, 0))],
        out_shape=[sds((T, ATTN_W), F32), sds((T, GMLP_W), F32), sds((8, ATTN_W), F32), sds((8, GMLP_W), F32)],
        compiler_params=_params(("arbitrary",)),
    )(dh1b, w_out, attn, gm, ga, gg)


def _gmlp_bwd(u, z, dgm, ln_g, ln_b, sgu_w, sgu_bt):
    T = u.shape[0]
    tg = TM_GMLP
    nsteps = T // tg

    def body(u_ref, z_ref, d_ref, g_ref, b_ref, w_ref, sb_ref, dproj_hbm, dlg_ref, dlb_ref, dw_ref, dsb_ref,
             stage, sem):
        i = pl.program_id(0)
        slot = i % 2
        duz_ref = stage.at[slot]

        def to_dproj(step, buf):
            rows = pl.ds(pl.multiple_of(step * tg, tg), tg)
            return pltpu.make_async_copy(stage.at[buf], dproj_hbm.at[rows, pl.ds(3 * ATTN_W, 2 * GMLP_W)],
                                         sem.at[buf])

        @pl.when(i == 0)
        def _():
            for ref in (dlg_ref, dlb_ref, dw_ref, dsb_ref):
                ref[...] = jnp.zeros_like(ref)

        @pl.when(i >= 2)
        def _():
            to_dproj(i - 2, slot).wait()

        grp = lax.broadcasted_iota(jnp.int32, (tg, GMLP_W), 1) // HEAD_DIM
        lane = lax.broadcasted_iota(jnp.int32, (CHUNK, LANES), 1)
        causal, ws = _causal_ws(w_ref)
        lg = g_ref[...]
        uu, zz, dgm = u_ref[...], z_ref[...], d_ref[...]
        ug, tu, tz, xhat, rstd, zn16, mixed = _gmlp_core(uu, zz, lg, b_ref[...], ws, sb_ref, grp)
        dmx = dgm * ug
        duz_ref[:, 0:GMLP_W] = dgm * mixed * _gelu_grad(uu, tu)
        dmx16 = dmx.astype(BF16)
        low = grp[:CHUNK, :LANES] == 0
        dzn = []
        for ci in range(tg // CHUNK):
            rows = slice(ci * CHUNK, (ci + 1) * CHUNK)
            halves = []
            for h in range(GMLP_W // LANES):
                lanes = slice(h * LANES, (h + 1) * LANES)
                dmx_h, zn_h, zero = dmx16[rows, lanes], zn16[rows, lanes], jnp.zeros((CHUNK, LANES), BF16)
                halves.append(jnp.where(low, _dot(ws[2 * h], dmx_h, TN), _dot(ws[2 * h + 1], dmx_h, TN)))
                dw_ref[2 * h] += _dot(jnp.where(low, dmx_h, zero), zn_h, NT)
                dw_ref[2 * h + 1] += _dot(jnp.where(low, zero, dmx_h), zn_h, NT)
            dzn.append(jnp.concatenate(halves, axis=1))
        dzn = jnp.concatenate(dzn, axis=0)
        dsb = jnp.zeros((CHUNK, LANES), F32)
        for g in range(N_GROUPS):
            half = slice((g // 2) * LANES, (g // 2 + 1) * LANES)
            per_token = jnp.sum(jnp.where(grp[:, half] == g, dmx[:, half], 0.0), axis=-1, keepdims=True)
            by_position = sum(per_token[ci * CHUNK:(ci + 1) * CHUNK] for ci in range(tg // CHUNK))
            dsb = jnp.where(lane == g, by_position, dsb)
        dsb_ref[...] += dsb
        _accum_rows(dlg_ref, dzn * xhat)
        _accum_rows(dlb_ref, dzn)
        dxh = dzn * lg
        dzg = rstd * (dxh - _group_mean(dxh, grp) - xhat * _group_mean(dxh * xhat, grp))
        duz_ref[:, GMLP_W:] = dzg * _gelu_grad(zz, tz)
        to_dproj(i, slot).start()

        @pl.when(i == nsteps - 1)
        def _():
            for g in range(N_GROUPS):
                dw_ref[g] = jnp.where(causal, dw_ref[g], 0.0)
            to_dproj(i, slot).wait()
            if nsteps >= 2:
                to_dproj(i - 1, 1 - slot).wait()

    sds = jax.ShapeDtypeStruct
    return pl.pallas_call(
        body, name="gmlp_bwd", grid=(nsteps,),
        in_specs=[_rows(tg, GMLP_W)] * 3 + [_resident((1, GMLP_W)), _resident((1, GMLP_W)),
                                              _resident((N_GROUPS, CHUNK, CHUNK)), _resident((CHUNK, N_GROUPS))],
        out_specs=[_HBM, pl.BlockSpec((8, GMLP_W), lambda i: (0, 0)),
                   pl.BlockSpec((8, GMLP_W), lambda i: (0, 0)),
                   pl.BlockSpec((N_GROUPS, CHUNK, CHUNK), lambda i: (0, 0, 0)),
                   pl.BlockSpec((CHUNK, LANES), lambda i: (0, 0))],
        out_shape=[sds((T, IN_W), F32), sds((8, GMLP_W), F32), sds((8, GMLP_W), F32),
                   sds((N_GROUPS, CHUNK, CHUNK), F32), sds((CHUNK, LANES), F32)],
        scratch_shapes=[pltpu.VMEM((2, tg, 2 * GMLP_W), F32), pltpu.SemaphoreType.DMA((2,))],
        compiler_params=_params(("arbitrary",)),
    )(u, z, dgm, ln_g, ln_b, sgu_w, sgu_bt)


def _attn_bwd(q, k, v, dattn, attn, lse, dproj, owner_grads=()):
    T = q.shape[0]
    nt = T // ATT_TILE
    ns = len(owner_grads)
    steps = (ATTN_W // LANES) * nt

    def body(sl_ref, q_hbm, k_hbm, v_hbm, do_hbm, o_hbm, lse_hbm, _, *rest):
        p_refs, rest = rest[:ns], rest[ns:]
        dq_hbm = dk_hbm = dv_hbm = rest[0]
        r_refs, rest = rest[1:1 + ns], rest[1 + ns:]
        qbuf, dobuf, obuf, lbuf, kbuf, vbuf, dqbuf, dkbuf, dvbuf, delta_s = rest[:10]
        sem_q, sem_do, sem_o, sem_l, sem_k, sem_v, sem_dq, sem_dk, sem_dv = rest[10:19]
        hp, t = pl.program_id(0), pl.program_id(1)
        step = hp * nt + t
        two, three = step % 2, step % 3
        before, after = (step + 2) % 3, (step + 1) % 3
        if ns:
            start, finish = _owner_exchange_phases(p_refs, r_refs, *rest[19:])
            pl.when(step == 0)(start)

        def fetch(hp_, t_, two_, three_):
            for hbm, buf, sem, slot in ((q_hbm, qbuf, sem_q, two_), (do_hbm, dobuf, sem_do, two_),
                                        (o_hbm, obuf, sem_o, two_), (lse_hbm, lbuf, sem_l, two_),
                                        (k_hbm, kbuf, sem_k, three_), (v_hbm, vbuf, sem_v, three_)):
                for cp in _tile_copies(hbm, buf.at[slot], sem.at[slot], hp_, t_):
                    cp.start()

        @pl.when(step == 0)
        def _():
            kbuf[2] = jnp.zeros((ATT_BLOCKS, CHUNK, LANES), F32)
            vbuf[2] = jnp.zeros((ATT_BLOCKS, CHUNK, LANES), F32)
            dkbuf[3] = jnp.zeros((ATT_BLOCKS, CHUNK, LANES), F32)
            dvbuf[3] = jnp.zeros((ATT_BLOCKS, CHUNK, LANES), F32)
            fetch(0, 0, 0, 0)

        @pl.when(step + 1 < steps)
        def _():
            fetch((step + 1) // nt, (step + 1) % nt, 1 - two, after)

        for buf, sem in ((qbuf, sem_q), (dobuf, sem_do), (obuf, sem_o), (lbuf, sem_l)):
            _wait_tile(buf.at[two], sem.at[two])
        _wait_tile(kbuf.at[three], sem_k.at[three])
        _wait_tile(vbuf.at[three], sem_v.at[three])

        @pl.when(step >= 2)
        def _():
            _wait_tile(dqbuf.at[two], sem_dq.at[two])

        @pl.when(step >= 3)
        def _():
            _wait_tile(dkbuf.at[three], sem_dk.at[three])
            _wait_tile(dvbuf.at[three], sem_dv.at[three])

        q_t, do_t, l_t, k_t, v_t = qbuf.at[two], dobuf.at[two], lbuf.at[two], kbuf.at[three], vbuf.at[three]
        k_b, v_b = kbuf.at[before], vbuf.at[before]
        dq_t, dk_t, dv_t = dqbuf.at[two], dkbuf.at[three], dvbuf.at[three]
        dk_b, dv_b = dkbuf.at[before], dvbuf.at[before]
        sink = jnp.where(t > 0, before, 3)
        dk_sink, dv_sink = dkbuf.at[sink], dvbuf.at[sink]
        head0 = lax.broadcasted_iota(jnp.int32, (CHUNK, LANES), 1) < HEAD_DIM
        for r in range(ATT_BLOCKS):
            dd = dobuf[two, r] * obuf[two, r]
            d0 = jnp.sum(jnp.where(head0, dd, 0.0), axis=-1, keepdims=True)
            d1 = jnp.sum(jnp.where(head0, 0.0, dd), axis=-1, keepdims=True)
            delta_s[r] = jnp.where(head0, d0, d1)

        def column(xb):
            return jnp.concatenate([xb[:, 0:1], xb[:, HEAD_DIM:HEAD_DIM + 1]], axis=0)

        no_key_before = jnp.where(lax.broadcasted_iota(jnp.int32, (2 * CHUNK, 2 * CHUNK), 1) < CHUNK, NEG, 0.0)
        for d in DILATIONS:
            bias = _residue_bias(sl_ref, d)
            def scores(j, d=d, bias=bias):
                kcat = jnp.concatenate([_rm_block_before(k_t, k_b, d, j), _rm_block(k_t, d, j)], axis=0).astype(BF16)
                vcat = jnp.concatenate([_rm_block_before(v_t, v_b, d, j), _rm_block(v_t, d, j)], axis=0).astype(BF16)
                q2 = _stack_heads(_rm_block(q_t, d, j), head0)
                do2 = _stack_heads(_rm_block(do_t, d, j), head0)
                return (_dot(q2, kcat, NT), _dot(do2, vcat, NT), column(_rm_block(l_t, d, j)),
                        column(_rm_block(delta_s, d, j)), bias_first if _first_in_tile(d, j) else bias, kcat, q2, do2)

            bias_first = bias + jnp.where(t == 0, 1.0, 0.0) * no_key_before
            group = {}
            for j in range(ATT_BLOCKS):
                if j % SM_BLOCKS == 0:
                    group = {i: scores(i) for i in range(j, j + SM_BLOCKS)}
                    s_all, dp_all, lse_all, delta_all, bias_all = (
                        jnp.concatenate([g[i] for g in group.values()], axis=0) for i in range(5))
                    p_all = jnp.exp2(s_all + bias_all - lse_all)
                    ds_all = (p_all * (dp_all - delta_all)).astype(BF16)
                    p_all = p_all.astype(BF16)
                at = slice((j % SM_BLOCKS) * 2 * CHUNK, (j % SM_BLOCKS + 1) * 2 * CHUNK)
                ds, p16 = ds_all[at, :], p_all[at, :]
                kcat, q2, do2 = group[j][5:]
                first = d == DILATIONS[0]
                _rm_add(dq_t, _residue_rows(d, j), _unstack_heads(_dot(ds, kcat), head0), first)
                ck = _dot(ds, q2, TN)
                cv = _dot(p16, do2, TN)
                _rm_add(dk_t, _residue_rows(d, j), ck[CHUNK:, :], first)
                _rm_add(dv_t, _residue_rows(d, j), cv[CHUNK:, :], first)
                if _first_in_tile(d, j):
                    rows = [(r, CHUNK - n, n) for r, _, n in _residue_rows(d, j)]
                    _rm_add(dk_sink, rows, ck[:CHUNK, :])
                    _rm_add(dv_sink, rows, cv[:CHUNK, :])
                else:
                    rows = [(r, lo - n, n) for r, lo, n in _residue_rows(d, j)]
                    _rm_add(dk_t, rows, ck[:CHUNK, :])
                    _rm_add(dv_t, rows, cv[:CHUNK, :])

        for r in range(ATT_BLOCKS):
            dqbuf[two, r] = dqbuf[two, r] * (Q_SCALE / LOG2E)
        for cp in _tile_copies(dq_hbm, dq_t, sem_dq.at[two], hp, t, to_hbm=True):
            cp.start()

        @pl.when(t > 0)
        def _():
            for cp in (_tile_copies(dk_hbm, dk_b, sem_dk.at[before], hp, t - 1, to_hbm=True, lane0=ATTN_W)
                       + _tile_copies(dv_hbm, dv_b, sem_dv.at[before], hp, t - 1, to_hbm=True, lane0=2 * ATTN_W)):
                cp.start()

        @pl.when(t == nt - 1)
        def _():
            for cp in (_tile_copies(dk_hbm, dk_t, sem_dk.at[three], hp, t, to_hbm=True, lane0=ATTN_W)
                       + _tile_copies(dv_hbm, dv_t, sem_dv.at[three], hp, t, to_hbm=True, lane0=2 * ATTN_W)):
                cp.start()

        @pl.when(step == steps - 1)
        def _():
            for slot in range(2):
                _wait_tile(dqbuf.at[slot], sem_dq.at[slot])
            for slot in range(3):
                _wait_tile(dkbuf.at[slot], sem_dk.at[slot])
                _wait_tile(dvbuf.at[slot], sem_dv.at[slot])

        if ns:
            pl.when(step == steps - 1)(finish)

    tile = lambda n: pltpu.VMEM((n, ATT_BLOCKS, CHUNK, LANES), F32)
    dma = lambda n: pltpu.SemaphoreType.DMA((n,))
    view = jax.ShapeDtypeStruct((T // ATT_BLOCKS, ATT_BLOCKS, ATTN_W), F32)
    outs = pl.pallas_call(
        body, name="attn_bwd", grid=(ATTN_W // LANES, nt),
        in_specs=[pl.BlockSpec((8, LANES), lambda c, t: (0, c))] + [_HBM] * (7 + ns),
        out_specs=[_HBM] * (1 + ns),
        out_shape=[jax.ShapeDtypeStruct((T // ATT_BLOCKS, ATT_BLOCKS, IN_W), F32)]
        + [jax.ShapeDtypeStruct(p.shape, p.dtype) for p in owner_grads],
        scratch_shapes=[tile(2), tile(2), tile(2), tile(2), tile(3), tile(3), tile(2), tile(4), tile(4),
                        pltpu.VMEM((ATT_BLOCKS, CHUNK, LANES), F32)]
        + [dma(2), dma(2), dma(2), dma(2), dma(3), dma(3), dma(2), dma(3), dma(3)]
        + (_owner_exchange_sems(ns) if ns else []),
        input_output_aliases={7: 0},
        compiler_params=_params(("arbitrary", "arbitrary")),
    )(_slope_table(), *[_residue_view(a) for a in (q, k, v, dattn, attn, lse, dproj)], *owner_grads)
    return outs[0].reshape(T, IN_W), tuple(outs[1:])


def _proj_bwd(dproj, w_in_t, x, g1, dh1, owner_grads=()):
    T = x.shape[0]
    tm = TM_PROJ
    ns = len(owner_grads)
    steps = T // tm

    def body(d_ref, w_ref, x_ref, g_ref, r_ref, *rest):
        p_refs, rest = rest[:ns], rest[ns:]
        dx_ref, dg_ref = rest[:2]
        r_refs, sems = rest[2:2 + ns], rest[2 + ns:]
        step = pl.program_id(0)
        if ns:
            start, finish = _owner_exchange_phases(p_refs, r_refs, *sems)
            pl.when(step == 0)(start)

        @pl.when(step == 0)
        def _():
            dg_ref[...] = jnp.zeros_like(dg_ref)

        dhn = _dot(d_ref[...].astype(BF16), w_ref[...])
        n1, r1 = _rms(x_ref[...])
        _accum_rows(dg_ref, dhn * n1)
        dx_ref[...] = r_ref[...] + _rms_bwd(n1, r1, g_ref[...], dhn)
        if ns:
            pl.when(step == steps - 1)(finish)

    outs = pl.pallas_call(
        body, name="proj_bwd", grid=(steps,),
        in_specs=[_rows(tm, IN_W), _resident((IN_W, D_MODEL)), _rows(tm, D_MODEL), _resident((1, D_MODEL)),
                  _rows(tm, D_MODEL)] + [_HBM] * ns,
        out_specs=[_rows(tm, D_MODEL), pl.BlockSpec((8, D_MODEL), lambda i: (0, 0))] + [_HBM] * ns,
        out_shape=[jax.ShapeDtypeStruct((T, D_MODEL), F32), jax.ShapeDtypeStruct((8, D_MODEL), F32)]
        + [jax.ShapeDtypeStruct(p.shape, p.dtype) for p in owner_grads],
        scratch_shapes=_owner_exchange_sems(ns) if ns else [],
        compiler_params=_params(("arbitrary",)),
    )(dproj, w_in_t, x, g1, dh1, *owner_grads)
    return outs[0], outs[1], tuple(outs[2:])


def _dw(a, b, name, tile, square_a=False, out_dtype=F32):
    T, ka = a.shape
    nb = b.shape[1]
    tka, tnb, tt = tile
    tt = min(tt, T)
    last = T // tt - 1

    def body(a_ref, b_ref, *refs):
        o_ref = refs[0]
        acc_ref = refs[1] if len(refs) > 1 else o_ref
        s = pl.program_id(2)

        @pl.when(s == 0)
        def _():
            acc_ref[...] = jnp.zeros_like(acc_ref)

        a_tile = a_ref[...]
        if square_a:
            a_tile = jnp.square(a_tile.astype(F32))
        acc_ref[...] += _dot(a_tile.astype(BF16), b_ref[...], TN)
        if acc_ref is not o_ref:
            @pl.when(s == last)
            def _():
                o_ref[...] = acc_ref[...].astype(out_dtype)

    return pl.pallas_call(
        body, name=name, grid=(ka // tka, nb // tnb, T // tt),
        in_specs=[pl.BlockSpec((tt, tka), lambda i, j, s: (s, i)), pl.BlockSpec((tt, tnb), lambda i, j, s: (s, j))],
        out_specs=pl.BlockSpec((tka, tnb), lambda i, j, s: (i, j)),
        out_shape=jax.ShapeDtypeStruct((ka, nb), out_dtype),
        scratch_shapes=[] if out_dtype == F32 else [pltpu.VMEM((tka, tnb), F32)],
        compiler_params=_params(("parallel", "parallel", "arbitrary")),
    )(a, b)


def _adamw_update(w, m, v, g):
    m2 = ADAM_B1 * m + (1.0 - ADAM_B1) * g
    v2 = ADAM_B2 * v + (1.0 - ADAM_B2) * jnp.square(g)
    m_hat = m2 / (1.0 - ADAM_B1 ** ADAM_STEP)
    v_hat = v2 / (1.0 - ADAM_B2 ** ADAM_STEP)
    return -ADAM_LR * (m_hat / (jnp.sqrt(v_hat) + ADAM_EPS) + ADAM_WD * w), m2, v2


def _adamw_tiny(ws, ms, vs, parts):
    n = len(ws)
    P = parts.shape[0]

    def body(*refs):
        w_refs, m_refs, v_refs, p_ref = refs[:n], refs[n:2 * n], refs[2 * n:3 * n], refs[3 * n]
        outs = refs[3 * n + 1:]

        def total(slot, rows):
            g = p_ref[0, 8 * slot:8 * slot + rows, :]
            for i in range(1, P):
                g = g + p_ref[i, 8 * slot:8 * slot + rows, :]
            return g

        for k in range(n):
            g = total(k, ws[k].shape[0])
            outs[4 * k][...] = g
            outs[4 * k + 1][...], outs[4 * k + 2][...], outs[4 * k + 3][...] = _adamw_update(
                w_refs[k][...], m_refs[k][...], v_refs[k][...], g)
        outs[4 * n][...] = total(n, 8)

    sds = jax.ShapeDtypeStruct
    return pl.pallas_call(
        body, name="adamw_tiny",
        out_shape=[sds(w.shape, F32) for w in ws for _ in range(4)] + [sds((8, LANES), F32)],
    )(*ws, *ms, *vs, parts)


def _adamw(w, m, v, parts, name, tr, transposed=False, shards=()):
    R, C = w.shape
    P = parts.shape[0]
    ns = len(shards)
    steps = R // tr

    def body(w_ref, m_ref, v_ref, p_ref, *rest):
        x_refs, rest = rest[:ns], rest[ns:]
        g_ref, d_ref, m2_ref, v2_ref = rest[:4]
        if ns:
            start, forward, finish = _gather_phases(x_refs, rest[4:4 + ns], *rest[4 + ns:])
            pl.when(pl.program_id(0) == 0)(start)
            pl.when(pl.program_id(0) == steps // 2)(forward)
        g = p_ref[0].astype(F32)
        for i in range(1, P):
            g = g + p_ref[i].astype(F32)
        if transposed:
            g = g.T
        g_ref[...] = g
        d_ref[...], m2_ref[...], v2_ref[...] = _adamw_update(w_ref[...], m_ref[...], v_ref[...], g)
        if ns:
            pl.when(pl.program_id(0) == steps - 1)(finish)

    spec = _rows(tr, C)
    part_spec = (pl.BlockSpec((P, C, tr), lambda i: (0, 0, i)) if transposed
                 else pl.BlockSpec((P, tr, C), lambda i: (0, i, 0)))
    return pl.pallas_call(
        body, name=name, grid=(steps,),
        in_specs=[spec, spec, spec, part_spec] + [_HBM] * ns,
        out_specs=[spec] * 4 + [_HBM] * ns,
        out_shape=[jax.ShapeDtypeStruct((R, C), F32)] * 4 + [_gathered_shape(s) for s in shards],
        scratch_shapes=_gather_sems(ns) if ns else [],
        compiler_params=_params(("arbitrary",) if ns else ("parallel",)),
    )(w, m, v, parts, *shards)


_HBM = pl.BlockSpec(memory_space=pltpu.HBM)


def _place():
    return lax.axis_index("x"), lax.axis_index("y"), lax.axis_index("c")


def _gathered_shape(shard):
    return jax.ShapeDtypeStruct((N_DEV,) + shard.shape, shard.dtype)


def _gather_sems(n):
    return [pltpu.SemaphoreType.DMA((7, n)), pltpu.SemaphoreType.DMA((7, n)), pltpu.SemaphoreType.DMA((n,))]


def _gather_phases(x_refs, out_refs, send_sems, recv_sems, local_sems):
    x, y, c = _place()
    me, sibling = (x, y, c), (x, y, 1 - c)
    chips = [(1 - x, y), (x, 1 - y), (1 - x, 1 - y)]
    arrays = range(len(x_refs))

    def slot(i, px, py, pc):
        return out_refs[i].at[4 * px + 2 * py + pc]

    def copy(i, k, block, to, own=False):
        return pltpu.make_async_remote_copy(
            src_ref=x_refs[i] if own else slot(i, *block), dst_ref=slot(i, *block),
            send_sem=send_sems.at[k, i], recv_sem=recv_sems.at[k, i], device_id=to, device_id_type=MESH)

    def mine(i):
        return pltpu.make_async_copy(x_refs[i], slot(i, *me), local_sems.at[i])

    def start():
        for i in arrays:
            mine(i).start()
            copy(i, 0, me, sibling, own=True).start()
            for j, chip in enumerate(chips):
                copy(i, 1 + j, me, (*chip, c), own=True).start()

    def forward():
        for i in arrays:
            for j, chip in enumerate(chips):
                copy(i, 1 + j, (*chip, c), me).wait_recv()
                copy(i, 4 + j, (*chip, c), sibling).start()

    def finish():
        for i in arrays:
            copy(i, 0, sibling, me).wait_recv()
            copy(i, 0, me, sibling, own=True).wait_send()
            for j, chip in enumerate(chips):
                copy(i, 4 + j, (*chip, 1 - c), me).wait_recv()
                copy(i, 1 + j, me, (*chip, c), own=True).wait_send()
                copy(i, 4 + j, (*chip, c), sibling).wait_send()
            mine(i).wait()

    return start, forward, finish


def _all_gather(shards, name):
    n = len(shards)

    def body(*refs):
        start, forward, finish = _gather_phases(refs[:n], refs[n:2 * n], *refs[2 * n:])
        start()
        forward()
        finish()

    return pl.pallas_call(
        body, name=name,
        out_shape=[_gathered_shape(s) for s in shards],
        in_specs=[_HBM] * n, out_specs=[_HBM] * n,
        scratch_shapes=_gather_sems(n),
    )(*shards)


def _owner_exchange_sems(n):
    return [pltpu.SemaphoreType.DMA((7, n)), pltpu.SemaphoreType.DMA((7, n)), pltpu.SemaphoreType.DMA((n,))]


def _owner_exchange_phases(g_refs, r_refs, send_sems, recv_sems, local_sems):
    x, y, c = _place()
    me = 4 * x + 2 * y + c
    flip = lambda v, bit: 1 - v if bit else v
    peers = [(flip(x, k & 4), flip(y, k & 2), flip(c, k & 1)) for k in range(1, N_DEV)]
    arrays = range(len(g_refs))

    def mine(i):
        return pltpu.make_async_copy(g_refs[i].at[me], r_refs[i].at[me], local_sems.at[i])

    def copy(i, k, src_slot, dst_slot):
        return pltpu.make_async_remote_copy(
            src_ref=g_refs[i].at[src_slot], dst_ref=r_refs[i].at[dst_slot],
            send_sem=send_sems.at[k, i], recv_sem=recv_sems.at[k, i], device_id=peers[k], device_id_type=MESH)

    def start():
        for i in arrays:
            mine(i).start()
            for k, (px, py, pc) in enumerate(peers):
                copy(i, k, 4 * px + 2 * py + pc, me).start()

    def finish():
        for i in arrays:
            for k, (px, py, pc) in enumerate(peers):
                copy(i, k, me, 4 * px + 2 * py + pc).wait_recv()
                copy(i, k, 4 * px + 2 * py + pc, me).wait_send()
            mine(i).wait()

    return start, finish


def _local_step(x, tgt, small, w_in_t, rest, exchange=False):
    g1, g2, gf = small["norm1_g"], small["norm2_g"], small["final_norm_g"].reshape(1, D_MODEL)
    ga, gg = small["attn_out_g"], small["gmlp_out_g"]
    ln_g = small["sgu_ln_g"].reshape(1, GMLP_W)
    ln_b = small["sgu_ln_b"].reshape(1, GMLP_W)
    sgu_w = small["sgu_w"][0]
    sgu_bt = small["sgu_b"][0].T

    hn1, q, k, v, u, z = _proj_fwd(x, g1, w_in_t)
    attn, lse, gathered = _attn_fwd(q, k, v, shards=rest if exchange else ())
    w_out, w_ff1_t, w_ff2 = [g.reshape(-1, D_MODEL) for g in gathered] if exchange else rest
    gm = _gmlp_fwd(u, z, ln_g, ln_b, sgu_w, sgu_bt)
    mixed, h1, hn2 = _out_fwd(attn, gm, ga, gg, w_out, x, g2)
    relu, dh2f, dh2b, loss8, dgf8 = _ffn_fwd(hn2, h1, w_ff1_t, w_ff2, gf, tgt)

    da, dh1f, dh1b, dg2 = _ffn_bwd(dh2b, dh2f, relu, h1, g2, w_ff2, w_ff1_t)
    wire = BF16 if exchange else F32
    dw_ff2 = _dw(relu, dh2b, "dw_ff2", DW_TILE, square_a=True, out_dtype=wire)
    dw_ff1_t = _dw(da, hn2, "dw_ff1", DW_TILE, out_dtype=wire)
    dattn, dgm, dga, dgg = _out_bwd(dh1b, w_out, attn, gm, ga, gg)
    dw_out = _dw(mixed, dh1b, "dw_out", DW_TILE, out_dtype=wire)
    early = [dw_out, dw_ff1_t, dw_ff2]
    if exchange:
        early = [g.reshape(N_DEV, -1, D_MODEL) for g in early]
    dproj, dlg, dlb, dsw, dsb = _gmlp_bwd(u, z, dgm, ln_g, ln_b, sgu_w, sgu_bt)
    dproj, arrived = _attn_bwd(q, k, v, dattn, attn, lse, dproj, owner_grads=early if exchange else ())
    dw_in_t = _dw(dproj, hn1, "dw_in", DW_TILE_IN, out_dtype=wire)
    late = (dw_in_t.reshape(N_DEV, -1, D_MODEL),) if exchange else ()
    dx, dg1, late = _proj_bwd(dproj, w_in_t, x, g1, dh1f, owner_grads=late)
    if exchange:
        dw_in_t, early = late[0], arrived

    small_grads = dict(
        norm1_g=dg1[0], sgu_ln_g=dlg[0], sgu_ln_b=dlb[0], sgu_w=dsw, sgu_b=dsb[:, :N_GROUPS].T,
        attn_out_g=dga[0], gmlp_out_g=dgg[0], norm2_g=dg2[0], final_norm_g=dgf8[0])
    return loss8[0, 0], dx, (dw_in_t, *early), small_grads


SMALL_NAMES = ("norm1_g", "sgu_ln_g", "sgu_ln_b", "sgu_w", "sgu_b", "attn_out_g", "gmlp_out_g", "norm2_g",
               "final_norm_g")
WEIGHT_ORDER = ("norm1_g", "w_in", "sgu_ln_g", "sgu_ln_b", "sgu_w", "sgu_b", "attn_out_g", "gmlp_out_g", "w_out",
                "norm2_g", "w_ff1", "w_ff2", "final_norm_g")


TINY_NAMES = tuple(n for n in SMALL_NAMES if n != "sgu_w")


def _as_rows(a):
    return a.reshape(-1, LANES)


def _pack_tiny_grads(d, loss):
    slots = [jnp.pad(_as_rows(d[n]), ((0, 8 - d[n].size // LANES), (0, 0))) for n in TINY_NAMES]
    return jnp.concatenate(slots + [jnp.full((8, LANES), loss, F32)], axis=0)


def kernel(x, norm1_g, w_in, sgu_ln_g, sgu_ln_b, sgu_w, sgu_b, attn_out_g, gmlp_out_g, w_out, norm2_g, w_ff1, w_ff2, final_norm_g, loss_target, m_norm1_g, m_w_in, m_sgu_ln_g, m_sgu_ln_b, m_sgu_w, m_sgu_b, m_attn_out_g, m_gmlp_out_g, m_w_out, m_norm2_g, m_w_ff1, m_w_ff2, m_final_norm_g, v_norm1_g, v_w_in, v_sgu_ln_g, v_sgu_ln_b, v_sgu_w, v_sgu_b, v_attn_out_g, v_gmlp_out_g, v_w_out, v_norm2_g, v_w_ff1, v_w_ff2, v_final_norm_g):
    w = dict(norm1_g=norm1_g, w_in=w_in, sgu_ln_g=sgu_ln_g, sgu_ln_b=sgu_ln_b, sgu_w=sgu_w, sgu_b=sgu_b,
             attn_out_g=attn_out_g, gmlp_out_g=gmlp_out_g, w_out=w_out, norm2_g=norm2_g, w_ff1=w_ff1, w_ff2=w_ff2,
             final_norm_g=final_norm_g)
    m = dict(norm1_g=m_norm1_g, w_in=m_w_in, sgu_ln_g=m_sgu_ln_g, sgu_ln_b=m_sgu_ln_b, sgu_w=m_sgu_w, sgu_b=m_sgu_b,
             attn_out_g=m_attn_out_g, gmlp_out_g=m_gmlp_out_g, w_out=m_w_out, norm2_g=m_norm2_g, w_ff1=m_w_ff1,
             w_ff2=m_w_ff2, final_norm_g=m_final_norm_g)
    v = dict(norm1_g=v_norm1_g, w_in=v_w_in, sgu_ln_g=v_sgu_ln_g, sgu_ln_b=v_sgu_ln_b, sgu_w=v_sgu_w, sgu_b=v_sgu_b,
             attn_out_g=v_attn_out_g, gmlp_out_g=v_gmlp_out_g, w_out=v_w_out, norm2_g=v_norm2_g, w_ff1=v_w_ff1,
             w_ff2=v_w_ff2, final_norm_g=v_final_norm_g)
    big = ("w_in", "w_out", "w_ff1", "w_ff2")

    w_in_t, = _all_gather([w_in[0].T.astype(BF16)], "w_in_all_gather")
    rest = (w_out[0].astype(BF16), w_ff1[0].T.astype(BF16), w_ff2[0].astype(BF16))
    loss, dx, parts, small_grads = _local_step(x[0], loss_target[0], {n: w[n] for n in SMALL_NAMES},
                                               w_in_t.reshape(IN_W, D_MODEL), rest, exchange=True)

    small_parts = [_pack_tiny_grads(small_grads, loss), _as_rows(small_grads["sgu_w"]).astype(BF16)]
    new = {}
    for n, p, transposed, tr in zip(big, parts, (True, False, True, False), (128, 128, 128, 256)):
        res = _adamw(w[n][0], m[n][0], v[n][0], p, "adamw_" + n, tr, transposed,
                     shards=small_parts if n == "w_in" else ())
        new[n] = [a[None] for a in res[:4]]
        if n == "w_in":
            tiny_parts, sgu_parts = res[4:]
    tiny = _adamw_tiny(*[[_as_rows(src[n]) for n in TINY_NAMES] for src in (w, m, v)], tiny_parts)
    sgu = _adamw(_as_rows(sgu_w), _as_rows(m_sgu_w), _as_rows(v_sgu_w), sgu_parts, "adamw_sgu_w", 512)
    loss = tiny[-1][0, 0]

    outs = []
    for i in range(4):
        d = {n: new[n][i] for n in big}
        d.update({n: tiny[4 * k + i].reshape(w[n].shape) for k, n in enumerate(TINY_NAMES)})
        d["sgu_w"] = sgu[i].reshape(sgu_w.shape)
        outs.extend(d[n] for n in WEIGHT_ORDER)
    return (loss, dx[None], *outs)
```

```python
import math

import numpy as np
import jax
import jax.numpy as jnp
from jax import lax
from jax.experimental import pallas as pl
from jax.experimental.pallas import tpu as pltpu

F32 = jnp.float32
BF16 = jnp.bfloat16

D_MODEL = 1024
HEAD_DIM = 64
N_HEADS = 12
ATTN_W = N_HEADS * HEAD_DIM
N_GROUPS = 4
GMLP_W = N_GROUPS * HEAD_DIM
IN_W = 3 * ATTN_W + 2 * GMLP_W
D_FF = 4 * D_MODEL
CHUNK = 128
DILATIONS = (1, 4, 16)
EPS = 1e-6
Q_SCALE = HEAD_DIM ** -0.5
LOG2E = 1.4426950408889634
NEG = -1e30

ADAM_LR, ADAM_B1, ADAM_B2, ADAM_EPS, ADAM_WD, ADAM_STEP = 0.001, 0.9, 0.999, 1e-08, 0.01, 10

N_DEV = 8
LANES = 128
VMEM_LIMIT = 56 << 20

TM_PROJ = 512
TM_FFN = 512
FF_CHUNK = 512
TM_GMLP = 1024
DW_TILE = (512, 1024, 8192)
DW_TILE_IN = (IN_W // 2, 1024, 2048)

MESH = pl.DeviceIdType.MESH


def _alibi_slopes(n):
    def pow2(m):
        start = 2.0 ** (-8.0 / m)
        return [start ** (i + 1) for i in range(m)]
    c = 2 ** int(math.floor(math.log2(n)))
    s = pow2(n) if c == n else pow2(c) + pow2(2 * c)[0::2][: n - c]
    return np.asarray(s, dtype=np.float32)


SLOPES = _alibi_slopes(N_HEADS)


def _params(sem=None):
    kw = dict(vmem_limit_bytes=VMEM_LIMIT)
    if sem is not None:
        kw["dimension_semantics"] = sem
    return pltpu.CompilerParams(**kw)


def _rows(tm, n):
    return pl.BlockSpec((tm, n), lambda i: (i, 0))


def _resident(shape):
    return pl.BlockSpec(shape, lambda *_: (0,) * len(shape), pipeline_mode=pl.Buffered(1))


def _rms(x):
    r = lax.rsqrt(jnp.mean(x * x, axis=-1, keepdims=True) + EPS)
    return x * r, r


def _rms_bwd(n, r, g, dy):
    dn = dy * g
    return r * (dn - n * jnp.mean(dn * n, axis=-1, keepdims=True))


def _accum_rows(acc_ref, v):
    acc_ref[...] += jnp.broadcast_to(jnp.sum(v, axis=0, keepdims=True), acc_ref.shape)


_G0 = math.sqrt(2.0 / math.pi)
_G1 = 0.044715


def _gelu(x):
    t = jnp.tanh(_G0 * (x + _G1 * (x * x * x)))
    return x * (0.5 * (1.0 + t)), t


def _gelu_grad(x, t):
    return 0.5 * (1.0 + t) + 0.5 * x * (1.0 - t * t) * (_G0 * (1.0 + 3.0 * _G1 * x * x))


NT = (((1,), (1,)), ((), ()))
TN = (((0,), (0,)), ((), ()))


def _dot(a, b, dims=None):
    if dims is None:
        return jnp.dot(a, b, preferred_element_type=F32)
    return lax.dot_general(a, b, dims, preferred_element_type=F32)


def _proj_fwd(x, g1, w_in_t):
    T = x.shape[0]
    tm = TM_PROJ

    def body(x_ref, g_ref, w_ref, hn_ref, q_ref, k_ref, v_ref, u_ref, z_ref):
        n, _ = _rms(x_ref[...])
        hn = (n * g_ref[...]).astype(BF16)
        hn_ref[...] = hn
        a = ATTN_W
        q_ref[...] = _dot(hn, w_ref[0:a, :], NT) * Q_SCALE
        k_ref[...] = _dot(hn, w_ref[a:2 * a, :], NT) * LOG2E
        v_ref[...] = _dot(hn, w_ref[2 * a:3 * a, :], NT)
        u_ref[...] = _dot(hn, w_ref[3 * a:3 * a + GMLP_W, :], NT)
        z_ref[...] = _dot(hn, w_ref[3 * a + GMLP_W:, :], NT)

    sds = jax.ShapeDtypeStruct
    return pl.pallas_call(
        body, name="proj_fwd", grid=(T // tm,),
        in_specs=[_rows(tm, D_MODEL), _resident((1, D_MODEL)), _resident((IN_W, D_MODEL))],
        out_specs=[_rows(tm, D_MODEL), _rows(tm, ATTN_W), _rows(tm, ATTN_W), _rows(tm, ATTN_W),
                   _rows(tm, GMLP_W), _rows(tm, GMLP_W)],
        out_shape=[sds((T, D_MODEL), BF16), sds((T, ATTN_W), F32), sds((T, ATTN_W), F32),
                   sds((T, ATTN_W), F32), sds((T, GMLP_W), F32), sds((T, GMLP_W), F32)],
        compiler_params=_params(("parallel",)),
    )(x, g1, w_in_t)


ATT_TILE = 2048
ATT_BLOCKS = ATT_TILE // CHUNK
SM_BLOCKS = 4


def _slope_table():
    row = np.repeat(SLOPES, HEAD_DIM)
    return jnp.asarray(np.broadcast_to(row[None], (8, ATTN_W)), F32)


def _residue_view(a):
    return a.reshape(a.shape[0] // ATT_BLOCKS, ATT_BLOCKS, a.shape[1])


def _tile_copies(hbm, buf, sem, hp, t, to_hbm=False, lane0=0):
    rows = pl.ds(pl.multiple_of(t * CHUNK, CHUNK), CHUNK)
    lanes = pl.ds(pl.multiple_of(lane0 + hp * LANES, LANES), LANES)
    pairs = [(hbm.at[rows, r, lanes], buf.at[r]) for r in range(ATT_BLOCKS)]
    return [pltpu.make_async_copy(v, h, sem) if to_hbm else pltpu.make_async_copy(h, v, sem) for h, v in pairs]


def _wait_tile(buf, sem):
    pltpu.make_async_copy(buf, buf, sem).wait()


def _residue_rows(d, j):
    if d == 16:
        return [(j, 0, CHUNK)]
    if d == 4:
        return [(j % 4 + 4 * m, 32 * (j // 4), 32) for m in range(4)]
    return [(r, 8 * j, 8) for r in range(ATT_BLOCKS)]


def _block_order(p, d):
    if d == 16:
        return p
    if d == 4:
        return 4 * (p & 31) + (p >> 5)
    return 16 * (p & 7) + (p >> 3)


def _first_in_tile(d, j):
    return _residue_rows(d, j)[0][1] == 0


def _rm_block(buf, d, j):
    return jnp.concatenate([buf[r, lo:lo + n, :] for r, lo, n in _residue_rows(d, j)], axis=0)


def _rm_block_before(buf, buf_before, d, j):
    if _first_in_tile(d, j):
        return jnp.concatenate([buf_before[r, CHUNK - n:CHUNK, :] for r, _, n in _residue_rows(d, j)], axis=0)
    return jnp.concatenate([buf[r, lo - n:lo, :] for r, lo, n in _residue_rows(d, j)], axis=0)


def _rm_store(buf, d, j, val):
    at = 0
    for r, lo, n in _residue_rows(d, j):
        buf[r, lo:lo + n, :] = val[at:at + n, :]
        at += n


def _rm_add(buf, rows, val, first=False):
    at = 0
    for r, lo, n in rows:
        if first:
            buf[r, lo:lo + n, :] = val[at:at + n, :]
        else:
            buf[r, lo:lo + n, :] += val[at:at + n, :]
        at += n


def _residue_bias(sl_ref, d):
    shape = (2 * CHUNK, 2 * CHUNK)
    row = lax.broadcasted_iota(jnp.int32, shape, 0)
    col = lax.broadcasted_iota(jnp.int32, shape, 1)
    steps = _block_order(row & (CHUNK - 1), d) + CHUNK - (_block_order(col & (CHUNK - 1), d) + (col & CHUNK))
    band = (steps >= 0) & (steps <= CHUNK)
    sl = sl_ref[0:1, :]
    upper = lax.broadcasted_iota(jnp.int32, (2 * CHUNK, 1), 0) < CHUNK
    slope2 = jnp.where(upper, sl[:, 0:1], sl[:, HEAD_DIM:HEAD_DIM + 1])
    return jnp.where(band, -(float(d) * LOG2E * slope2 * steps.astype(F32)), NEG)


def _stack_heads(xb, head0):
    zero = jnp.zeros_like(xb)
    return jnp.concatenate([jnp.where(head0, xb, zero), jnp.where(head0, zero, xb)], axis=0).astype(BF16)


def _unstack_heads(x2, head0):
    return jnp.where(head0, x2[:CHUNK, :], x2[CHUNK:, :])


def _attn_fwd(q, k, v, shards=()):
    T = q.shape[0]
    nt = T // ATT_TILE
    ns = len(shards)
    steps = (ATTN_W // LANES) * nt

    def body(sl_ref, q_hbm, k_hbm, v_hbm, *rest):
        x_refs, rest = rest[:ns], rest[ns:]
        attn_hbm, lse_hbm = rest[:2]
        g_refs, rest = rest[2:2 + ns], rest[2 + ns:]
        qbuf, kbuf, vbuf, obuf, lbuf = rest[:5]
        o_acc, l_acc = rest[5:8], rest[8:11]
        sem_q, sem_k, sem_v, sem_o, sem_l = rest[11:16]
        hp, t = pl.program_id(0), pl.program_id(1)
        step = hp * nt + t
        two, three = step % 2, step % 3
        before, after = (step + 2) % 3, (step + 1) % 3
        if ns:
            start, forward, finish = _gather_phases(x_refs, g_refs, *rest[16:])
            pl.when(step == 0)(start)
            pl.when(step == (3 * steps) // 4)(forward)

        def fetch(hp_, t_, two_, three_):
            for cp in (_tile_copies(q_hbm, qbuf.at[two_], sem_q.at[two_], hp_, t_)
                       + _tile_copies(k_hbm, kbuf.at[three_], sem_k.at[three_], hp_, t_)
                       + _tile_copies(v_hbm, vbuf.at[three_], sem_v.at[three_], hp_, t_)):
                cp.start()

        @pl.when(step == 0)
        def _():
            kbuf[2] = jnp.zeros((ATT_BLOCKS, CHUNK, LANES), F32)
            vbuf[2] = jnp.zeros((ATT_BLOCKS, CHUNK, LANES), F32)
            fetch(0, 0, 0, 0)

        @pl.when(step + 1 < steps)
        def _():
            fetch((step + 1) // nt, (step + 1) % nt, 1 - two, after)

        _wait_tile(qbuf.at[two], sem_q.at[two])
        _wait_tile(kbuf.at[three], sem_k.at[three])
        _wait_tile(vbuf.at[three], sem_v.at[three])

        @pl.when(step >= 2)
        def _():
            _wait_tile(obuf.at[two], sem_o.at[two])
            _wait_tile(lbuf.at[two], sem_l.at[two])

        q_t, k_t, v_t = qbuf.at[two], kbuf.at[three], vbuf.at[three]
        k_b, v_b = kbuf.at[before], vbuf.at[before]
        head0 = lax.broadcasted_iota(jnp.int32, (CHUNK, LANES), 1) < HEAD_DIM
        no_key_before = jnp.where(lax.broadcasted_iota(jnp.int32, (2 * CHUNK, 2 * CHUNK), 1) < CHUNK, NEG, 0.0)
        for pi, d in enumerate(DILATIONS):
            bias = _residue_bias(sl_ref, d)

            def scores(j, d=d, bias=bias):
                kcat = jnp.concatenate([_rm_block_before(k_t, k_b, d, j), _rm_block(k_t, d, j)], axis=0).astype(BF16)
                vcat = jnp.concatenate([_rm_block_before(v_t, v_b, d, j), _rm_block(v_t, d, j)], axis=0).astype(BF16)
                s = _dot(_stack_heads(_rm_block(q_t, d, j), head0), kcat, NT)
                return s, vcat, bias_first if _first_in_tile(d, j) else bias

            bias_first = bias + jnp.where(t == 0, 1.0, 0.0) * no_key_before
            for j0 in range(0, ATT_BLOCKS, SM_BLOCKS):
                group = [scores(j) for j in range(j0, j0 + SM_BLOCKS)]
                s = jnp.concatenate([g[0] for g in group], axis=0) + jnp.concatenate([g[2] for g in group], axis=0)
                m = jnp.max(s, axis=-1, keepdims=True)
                p = jnp.exp2(s - m)
                l = jnp.sum(p, axis=-1, keepdims=True)
                p = p.astype(BF16)
                block = lambda a, i: a[i * 2 * CHUNK:(i + 1) * 2 * CHUNK, :]
                o = jnp.concatenate([_dot(block(p, i), g[1]) for i, g in enumerate(group)], axis=0) * (1.0 / l)
                lse = jnp.broadcast_to(m + jnp.log2(l), o.shape)
                for i in range(SM_BLOCKS):
                    _rm_store(o_acc[pi], d, j0 + i, _unstack_heads(block(o, i), head0))
                    _rm_store(l_acc[pi], d, j0 + i, _unstack_heads(block(lse, i), head0))

        for r in range(ATT_BLOCKS):
            a, b, c = l_acc[0][r], l_acc[1][r], l_acc[2][r]
            m = jnp.maximum(jnp.maximum(a, b), c)
            ea, eb, ec = jnp.exp2(a - m), jnp.exp2(b - m), jnp.exp2(c - m)
            tot = ea + eb + ec
            obuf[two, r] = (ea * o_acc[0][r] + eb * o_acc[1][r] + ec * o_acc[2][r]) / tot
            lbuf[two, r] = m + jnp.log2(tot)

        for cp in (_tile_copies(attn_hbm, obuf.at[two], sem_o.at[two], hp, t, to_hbm=True)
                   + _tile_copies(lse_hbm, lbuf.at[two], sem_l.at[two], hp, t, to_hbm=True)):
            cp.start()

        @pl.when(step == steps - 1)
        def _():
            for slot in (two, 1 - two)[:min(steps, 2)]:
                _wait_tile(obuf.at[slot], sem_o.at[slot])
                _wait_tile(lbuf.at[slot], sem_l.at[slot])

        if ns:
            pl.when(step == steps - 1)(finish)

    tile = lambda n: pltpu.VMEM((n, ATT_BLOCKS, CHUNK, LANES), F32)
    dma = lambda n: pltpu.SemaphoreType.DMA((n,))
    view = jax.ShapeDtypeStruct((T // ATT_BLOCKS, ATT_BLOCKS, ATTN_W), F32)
    outs = pl.pallas_call(
        body, name="attn_fwd", grid=(ATTN_W // LANES, nt),
        in_specs=[pl.BlockSpec((8, LANES), lambda c, t: (0, c))] + [_HBM] * (3 + ns),
        out_specs=[_HBM] * (2 + ns),
        out_shape=[view, view] + [_gathered_shape(s) for s in shards],
        scratch_shapes=[tile(2), tile(3), tile(3), tile(2), tile(2)] + [pltpu.VMEM((ATT_BLOCKS, CHUNK, LANES), F32)] * 6
        + [dma(2), dma(3), dma(3), dma(2), dma(2)] + (_gather_sems(ns) if ns else []),
        compiler_params=_params(("arbitrary", "arbitrary")),
    )(_slope_table(), _residue_view(q), _residue_view(k), _residue_view(v), *shards)
    return outs[0].reshape(T, ATTN_W), outs[1].reshape(T, ATTN_W), tuple(outs[2:])


def _group_mean(v, grp):
    halves = []
    for h in range(GMLP_W // LANES):
        x = v[:, h * LANES:(h + 1) * LANES]
        low = grp[:, h * LANES:(h + 1) * LANES] == 2 * h
        a = jnp.sum(jnp.where(low, x, 0.0), axis=-1, keepdims=True)
        b = jnp.sum(jnp.where(low, 0.0, x), axis=-1, keepdims=True)
        halves.append(jnp.where(low, a, b) * (1.0 / HEAD_DIM))
    return jnp.concatenate(halves, axis=1)


def _gmlp_core(uu, zz, lg, lb, ws, sb_ref, grp):
    ug, tu = _gelu(uu)
    zg, tz = _gelu(zz)
    zc = zg - _group_mean(zg, grp)
    rstd = lax.rsqrt(_group_mean(zc * zc, grp) + EPS)
    xhat = zc * rstd
    zn16 = (xhat * lg + lb).astype(BF16)
    low = grp[:CHUNK, :LANES] == 0
    mixed = []
    for ci in range(uu.shape[0] // CHUNK):
        rows = slice(ci * CHUNK, (ci + 1) * CHUNK)
        halves = []
        for h in range(GMLP_W // LANES):
            zh = zn16[rows, h * LANES:(h + 1) * LANES]
            halves.append(jnp.where(low, _dot(ws[2 * h], zh) + sb_ref[:, 2 * h:2 * h + 1],
                                    _dot(ws[2 * h + 1], zh) + sb_ref[:, 2 * h + 1:2 * h + 2]))
        mixed.append(jnp.concatenate(halves, axis=1))
    return ug, tu, tz, xhat, rstd, zn16, jnp.concatenate(mixed, axis=0)


def _causal_ws(w_ref):
    ti = lax.broadcasted_iota(jnp.int32, (CHUNK, CHUNK), 0)
    si = lax.broadcasted_iota(jnp.int32, (CHUNK, CHUNK), 1)
    causal = si <= ti
    return causal, [jnp.where(causal, w_ref[g], 0.0).astype(BF16) for g in range(N_GROUPS)]


def _gmlp_fwd(u, z, ln_g, ln_b, sgu_w, sgu_bt):
    T = u.shape[0]
    tg = TM_GMLP

    def body(u_ref, z_ref, g_ref, b_ref, w_ref, sb_ref, out_ref):
        grp = lax.broadcasted_iota(jnp.int32, (tg, GMLP_W), 1) // HEAD_DIM
        _, ws = _causal_ws(w_ref)
        ug, _, _, _, _, _, mixed = _gmlp_core(u_ref[...], z_ref[...], g_ref[...], b_ref[...], ws, sb_ref, grp)
        out_ref[...] = ug * mixed

    return pl.pallas_call(
        body, name="gmlp_fwd", grid=(T // tg,),
        in_specs=[_rows(tg, GMLP_W), _rows(tg, GMLP_W), _resident((1, GMLP_W)), _resident((1, GMLP_W)),
                  _resident((N_GROUPS, CHUNK, CHUNK)), _resident((CHUNK, N_GROUPS))],
        out_specs=_rows(tg, GMLP_W),
        out_shape=jax.ShapeDtypeStruct((T, GMLP_W), F32),
        compiler_params=_params(("parallel",)),
    )(u, z, ln_g, ln_b, sgu_w, sgu_bt)


def _out_fwd(attn, gm, ga, gg, w_out, x, g2):
    T = x.shape[0]
    tm = TM_PROJ

    def body(a_ref, m_ref, ga_ref, gg_ref, w_ref, x_ref, g2_ref, mix_ref, h1_ref, hn2_ref):
        an, _ = _rms(a_ref[...])
        gn, _ = _rms(m_ref[...])
        an = (an * ga_ref[...]).astype(BF16)
        gn = (gn * gg_ref[...]).astype(BF16)
        mix_ref[:, 0:ATTN_W] = an
        mix_ref[:, ATTN_W:] = gn
        h1 = x_ref[...] + _dot(an, w_ref[0:ATTN_W, :]) + _dot(gn, w_ref[ATTN_W:, :])
        h1_ref[...] = h1
        n2, _ = _rms(h1)
        hn2_ref[...] = (n2 * g2_ref[...]).astype(BF16)

    sds = jax.ShapeDtypeStruct
    return pl.pallas_call(
        body, name="out_fwd", grid=(T // tm,),
        in_specs=[_rows(tm, ATTN_W), _rows(tm, GMLP_W), _resident((1, ATTN_W)), _resident((1, GMLP_W)),
                  _resident((D_MODEL, D_MODEL)), _rows(tm, D_MODEL), _resident((1, D_MODEL))],
        out_specs=[_rows(tm, D_MODEL)] * 3,
        out_shape=[sds((T, D_MODEL), BF16), sds((T, D_MODEL), F32), sds((T, D_MODEL), BF16)],
        compiler_params=_params(("parallel",)),
    )(attn, gm, ga, gg, w_out, x, g2)


def _ffn_fwd(hn2, h1, w1t, w2, gf, tgt):
    T = h1.shape[0]
    tm = TM_FFN

    def body(hn_ref, h1_ref, w1_ref, w2_ref, gf_ref, t_ref, r_ref, dhf_ref, dhb_ref, loss_ref, dgf_ref):
        i = pl.program_id(0)

        @pl.when(i == 0)
        def _():
            loss_ref[...] = jnp.zeros_like(loss_ref)
            dgf_ref[...] = jnp.zeros_like(dgf_ref)

        hn = hn_ref[...]
        acc = h1_ref[...]
        for j in range(D_FF // FF_CHUNK):
            cols = slice(j * FF_CHUNK, (j + 1) * FF_CHUNK)
            r = jnp.maximum(_dot(hn, w1_ref[cols, :], NT), 0.0)
            r_ref[:, cols] = r.astype(BF16)
            act = jnp.square(r).astype(BF16)
            acc = acc + _dot(act, w2_ref[cols, :])
        n3, r3 = _rms(acc)
        gf_row = gf_ref[...]
        e = n3 * gf_row - t_ref[...]
        loss_ref[...] += 0.5 * jnp.sum(jnp.mean(e * e, axis=-1, keepdims=True))
        dy = e * (1.0 / D_MODEL)
        _accum_rows(dgf_ref, dy * n3)
        dh2 = _rms_bwd(n3, r3, gf_row, dy)
        dhf_ref[...] = dh2
        dhb_ref[...] = dh2.astype(BF16)

    sds = jax.ShapeDtypeStruct
    acc_spec = lambda n: pl.BlockSpec((8, n), lambda i: (0, 0))
    return pl.pallas_call(
        body, name="ffn_fwd", grid=(T // tm,),
        in_specs=[_rows(tm, D_MODEL), _rows(tm, D_MODEL), _resident((D_FF, D_MODEL)), _resident((D_FF, D_MODEL)),
                  _resident((1, D_MODEL)), _rows(tm, D_MODEL)],
        out_specs=[_rows(tm, D_FF), _rows(tm, D_MODEL), _rows(tm, D_MODEL), acc_spec(LANES), acc_spec(D_MODEL)],
        out_shape=[sds((T, D_FF), BF16), sds((T, D_MODEL), F32), sds((T, D_MODEL), BF16),
                   sds((8, LANES), F32), sds((8, D_MODEL), F32)],
        compiler_params=_params(("arbitrary",)),
    )(hn2, h1, w1t, w2, gf, tgt)


def _ffn_bwd(dh2b, dh2f, relu, h1, g2, w2, w1t):
    T = h1.shape[0]
    tm = TM_FFN

    def body(db_ref, df_ref, r_ref, h1_ref, g2_ref, w2_ref, w1t_ref, da_ref, d1f_ref, d1b_ref, dg_ref):
        @pl.when(pl.program_id(0) == 0)
        def _():
            dg_ref[...] = jnp.zeros_like(dg_ref)

        db = db_ref[...]
        acc = jnp.zeros((tm, D_MODEL), F32)
        for j in range(D_FF // FF_CHUNK):
            cols = slice(j * FF_CHUNK, (j + 1) * FF_CHUNK)
            da = (_dot(db, w2_ref[cols, :], NT) * (2.0 * r_ref[:, cols].astype(F32))).astype(BF16)
            da_ref[:, cols] = da
            acc = acc + _dot(da, w1t_ref[cols, :])
        n2, r2 = _rms(h1_ref[...])
        _accum_rows(dg_ref, acc * n2)
        dh1 = df_ref[...] + _rms_bwd(n2, r2, g2_ref[...], acc)
        d1f_ref[...] = dh1
        d1b_ref[...] = dh1.astype(BF16)

    sds = jax.ShapeDtypeStruct
    return pl.pallas_call(
        body, name="ffn_bwd", grid=(T // tm,),
        in_specs=[_rows(tm, D_MODEL), _rows(tm, D_MODEL), _rows(tm, D_FF), _rows(tm, D_MODEL),
                  _resident((1, D_MODEL)), _resident((D_FF, D_MODEL)), _resident((D_FF, D_MODEL))],
        out_specs=[_rows(tm, D_FF), _rows(tm, D_MODEL), _rows(tm, D_MODEL),
                   pl.BlockSpec((8, D_MODEL), lambda i: (0, 0))],
        out_shape=[sds((T, D_FF), BF16), sds((T, D_MODEL), F32), sds((T, D_MODEL), BF16), sds((8, D_MODEL), F32)],
        compiler_params=_params(("arbitrary",)),
    )(dh2b, dh2f, relu, h1, g2, w2, w1t)


def _out_bwd(dh1b, w_out, attn, gm, ga, gg):
    T = attn.shape[0]
    tm = TM_PROJ

    def body(d_ref, w_ref, a_ref, m_ref, ga_ref, gg_ref, da_ref, dm_ref, dga_ref, dgg_ref):
        @pl.when(pl.program_id(0) == 0)
        def _():
            dga_ref[...] = jnp.zeros_like(dga_ref)
            dgg_ref[...] = jnp.zeros_like(dgg_ref)

        d = d_ref[...]
        dan = _dot(d, w_ref[0:ATTN_W, :], NT)
        dgn = _dot(d, w_ref[ATTN_W:, :], NT)
        na, ra = _rms(a_ref[...])
        ng, rg = _rms(m_ref[...])
        _accum_rows(dga_ref, dan * na)
        _accum_rows(dgg_ref, dgn * ng)
        da_ref[...] = _rms_bwd(na, ra, ga_ref[...], dan)
        dm_ref[...] = _rms_bwd(ng, rg, gg_ref[...], dgn)

    sds = jax.ShapeDtypeStruct
    return pl.pallas_call(
        body, name="out_bwd", grid=(T // tm,),
        in_specs=[_rows(tm, D_MODEL), _resident((D_MODEL, D_MODEL)), _rows(tm, ATTN_W), _rows(tm, GMLP_W),
                  _resident((1, ATTN_W)), _resident((1, GMLP_W))],
        out_specs=[_rows(tm, ATTN_W), _rows(tm, GMLP_W), pl.BlockSpec((8, ATTN_W), lambda i: (0, 0)),
                   pl.BlockSpec((8, GMLP_W), lambda i: (0, 0))],
        out_shape=[sds((T, ATTN_W), F32), sds((T, GMLP_W), F32), sds((8, ATTN_W), F32), sds((8, GMLP_W), F32)],
        compiler_params=_params(("arbitrary",)),
    )(dh1b, w_out, attn, gm, ga, gg)


def _gmlp_bwd(u, z, dgm, ln_g, ln_b, sgu_w, sgu_bt):
    T = u.shape[0]
    tg = TM_GMLP
    nsteps = T // tg

    def body(u_ref, z_ref, d_ref, g_ref, b_ref, w_ref, sb_ref, dproj_hbm, dlg_ref, dlb_ref, dw_ref, dsb_ref,
             stage, sem):
        i = pl.program_id(0)
        slot = i % 2
        duz_ref = stage.at[slot]

        def to_dproj(step, buf):
            rows = pl.ds(pl.multiple_of(step * tg, tg), tg)
            return pltpu.make_async_copy(stage.at[buf], dproj_hbm.at[rows, pl.ds(3 * ATTN_W, 2 * GMLP_W)],
                                         sem.at[buf])

        @pl.when(i == 0)
        def _():
            for ref in (dlg_ref, dlb_ref, dw_ref, dsb_ref):
                ref[...] = jnp.zeros_like(ref)

        @pl.when(i >= 2)
        def _():
            to_dproj(i - 2, slot).wait()

        grp = lax.broadcasted_iota(jnp.int32, (tg, GMLP_W), 1) // HEAD_DIM
        lane = lax.broadcasted_iota(jnp.int32, (CHUNK, LANES), 1)
        causal, ws = _causal_ws(w_ref)
        lg = g_ref[...]
        uu, zz, dgm = u_ref[...], z_ref[...], d_ref[...]
        ug, tu, tz, xhat, rstd, zn16, mixed = _gmlp_core(uu, zz, lg, b_ref[...], ws, sb_ref, grp)
        dmx = dgm * ug
        duz_ref[:, 0:GMLP_W] = dgm * mixed * _gelu_grad(uu, tu)
        dmx16 = dmx.astype(BF16)
        low = grp[:CHUNK, :LANES] == 0
        dzn = []
        for ci in range(tg // CHUNK):
            rows = slice(ci * CHUNK, (ci + 1) * CHUNK)
            halves = []
            for h in range(GMLP_W // LANES):
                lanes = slice(h * LANES, (h + 1) * LANES)
                dmx_h, zn_h, zero = dmx16[rows, lanes], zn16[rows, lanes], jnp.zeros((CHUNK, LANES), BF16)
                halves.append(jnp.where(low, _dot(ws[2 * h], dmx_h, TN), _dot(ws[2 * h + 1], dmx_h, TN)))
                dw_ref[2 * h] += _dot(jnp.where(low, dmx_h, zero), zn_h, NT)
                dw_ref[2 * h + 1] += _dot(jnp.where(low, zero, dmx_h), zn_h, NT)
            dzn.append(jnp.concatenate(halves, axis=1))
        dzn = jnp.concatenate(dzn, axis=0)
        dsb = jnp.zeros((CHUNK, LANES), F32)
        for g in range(N_GROUPS):
            half = slice((g // 2) * LANES, (g // 2 + 1) * LANES)
            per_token = jnp.sum(jnp.where(grp[:, half] == g, dmx[:, half], 0.0), axis=-1, keepdims=True)
            by_position = sum(per_token[ci * CHUNK:(ci + 1) * CHUNK] for ci in range(tg // CHUNK))
            dsb = jnp.where(lane == g, by_position, dsb)
        dsb_ref[...] += dsb
        _accum_rows(dlg_ref, dzn * xhat)
        _accum_rows(dlb_ref, dzn)
        dxh = dzn * lg
        dzg = rstd * (dxh - _group_mean(dxh, grp) - xhat * _group_mean(dxh * xhat, grp))
        duz_ref[:, GMLP_W:] = dzg * _gelu_grad(zz, tz)
        to_dproj(i, slot).start()

        @pl.when(i == nsteps - 1)
        def _():
            for g in range(N_GROUPS):
                dw_ref[g] = jnp.where(causal, dw_ref[g], 0.0)
            to_dproj(i, slot).wait()
            if nsteps >= 2:
                to_dproj(i - 1, 1 - slot).wait()

    sds = jax.ShapeDtypeStruct
    return pl.pallas_call(
        body, name="gmlp_bwd", grid=(nsteps,),
        in_specs=[_rows(tg, GMLP_W)] * 3 + [_resident((1, GMLP_W)), _resident((1, GMLP_W)),
                                              _resident((N_GROUPS, CHUNK, CHUNK)), _resident((CHUNK, N_GROUPS))],
        out_specs=[_HBM, pl.BlockSpec((8, GMLP_W), lambda i: (0, 0)),
                   pl.BlockSpec((8, GMLP_W), lambda i: (0, 0)),
                   pl.BlockSpec((N_GROUPS, CHUNK, CHUNK), lambda i: (0, 0, 0)),
                   pl.BlockSpec((CHUNK, LANES), lambda i: (0, 0))],
        out_shape=[sds((T, IN_W), F32), sds((8, GMLP_W), F32), sds((8, GMLP_W), F32),
                   sds((N_GROUPS, CHUNK, CHUNK), F32), sds((CHUNK, LANES), F32)],
        scratch_shapes=[pltpu.VMEM((2, tg, 2 * GMLP_W), F32), pltpu.SemaphoreType.DMA((2,))],
        compiler_params=_params(("arbitrary",)),
    )(u, z, dgm, ln_g, ln_b, sgu_w, sgu_bt)


def _attn_bwd(q, k, v, dattn, attn, lse, dproj, owner_grads=()):
    T = q.shape[0]
    nt = T // ATT_TILE
    ns = len(owner_grads)
    steps = (ATTN_W // LANES) * nt

    def body(sl_ref, q_hbm, k_hbm, v_hbm, do_hbm, o_hbm, lse_hbm, _, *rest):
        p_refs, rest = rest[:ns], rest[ns:]
        dq_hbm = dk_hbm = dv_hbm = rest[0]
        r_refs, rest = rest[1:1 + ns], rest[1 + ns:]
        qbuf, dobuf, obuf, lbuf, kbuf, vbuf, dqbuf, dkbuf, dvbuf, delta_s = rest[:10]
        sem_q, sem_do, sem_o, sem_l, sem_k, sem_v, sem_dq, sem_dk, sem_dv = rest[10:19]
        hp, t = pl.program_id(0), pl.program_id(1)
        step = hp * nt + t
        two, three = step % 2, step % 3
        before, after = (step + 2) % 3, (step + 1) % 3
        if ns:
            start, finish = _owner_exchange_phases(p_refs, r_refs, *rest[19:])
            pl.when(step == 0)(start)

        def fetch(hp_, t_, two_, three_):
            for hbm, buf, sem, slot in ((q_hbm, qbuf, sem_q, two_), (do_hbm, dobuf, sem_do, two_),
                                        (o_hbm, obuf, sem_o, two_), (lse_hbm, lbuf, sem_l, two_),
                                        (k_hbm, kbuf, sem_k, three_), (v_hbm, vbuf, sem_v, three_)):
                for cp in _tile_copies(hbm, buf.at[slot], sem.at[slot], hp_, t_):
                    cp.start()

        @pl.when(step == 0)
        def _():
            kbuf[2] = jnp.zeros((ATT_BLOCKS, CHUNK, LANES), F32)
            vbuf[2] = jnp.zeros((ATT_BLOCKS, CHUNK, LANES), F32)
            dkbuf[3] = jnp.zeros((ATT_BLOCKS, CHUNK, LANES), F32)
            dvbuf[3] = jnp.zeros((ATT_BLOCKS, CHUNK, LANES), F32)
            fetch(0, 0, 0, 0)

        @pl.when(step + 1 < steps)
        def _():
            fetch((step + 1) // nt, (step + 1) % nt, 1 - two, after)

        for buf, sem in ((qbuf, sem_q), (dobuf, sem_do), (obuf, sem_o), (lbuf, sem_l)):
            _wait_tile(buf.at[two], sem.at[two])
        _wait_tile(kbuf.at[three], sem_k.at[three])
        _wait_tile(vbuf.at[three], sem_v.at[three])

        @pl.when(step >= 2)
        def _():
            _wait_tile(dqbuf.at[two], sem_dq.at[two])

        @pl.when(step >= 3)
        def _():
            _wait_tile(dkbuf.at[three], sem_dk.at[three])
            _wait_tile(dvbuf.at[three], sem_dv.at[three])

        q_t, do_t, l_t, k_t, v_t = qbuf.at[two], dobuf.at[two], lbuf.at[two], kbuf.at[three], vbuf.at[three]
        k_b, v_b = kbuf.at[before], vbuf.at[before]
        dq_t, dk_t, dv_t = dqbuf.at[two], dkbuf.at[three], dvbuf.at[three]
        dk_b, dv_b = dkbuf.at[before], dvbuf.at[before]
        sink = jnp.where(t > 0, before, 3)
        dk_sink, dv_sink = dkbuf.at[sink], dvbuf.at[sink]
        head0 = lax.broadcasted_iota(jnp.int32, (CHUNK, LANES), 1) < HEAD_DIM
        for r in range(ATT_BLOCKS):
            dd = dobuf[two, r] * obuf[two, r]
            d0 = jnp.sum(jnp.where(head0, dd, 0.0), axis=-1, keepdims=True)
            d1 = jnp.sum(jnp.where(head0, 0.0, dd), axis=-1, keepdims=True)
            delta_s[r] = jnp.where(head0, d0, d1)

        def column(xb):
            return jnp.concatenate([xb[:, 0:1], xb[:, HEAD_DIM:HEAD_DIM + 1]], axis=0)

        no_key_before = jnp.where(lax.broadcasted_iota(jnp.int32, (2 * CHUNK, 2 * CHUNK), 1) < CHUNK, NEG, 0.0)
        for d in DILATIONS:
            bias = _residue_bias(sl_ref, d)
            def scores(j, d=d, bias=bias):
                kcat = jnp.concatenate([_rm_block_before(k_t, k_b, d, j), _rm_block(k_t, d, j)], axis=0).astype(BF16)
                vcat = jnp.concatenate([_rm_block_before(v_t, v_b, d, j), _rm_block(v_t, d, j)], axis=0).astype(BF16)
                q2 = _stack_heads(_rm_block(q_t, d, j), head0)
                do2 = _stack_heads(_rm_block(do_t, d, j), head0)
                return (_dot(q2, kcat, NT), _dot(do2, vcat, NT), column(_rm_block(l_t, d, j)),
                        column(_rm_block(delta_s, d, j)), bias_first if _first_in_tile(d, j) else bias, kcat, q2, do2)

            bias_first = bias + jnp.where(t == 0, 1.0, 0.0) * no_key_before
            group = {}
            for j in range(ATT_BLOCKS):
                if j % SM_BLOCKS == 0:
                    group = {i: scores(i) for i in range(j, j + SM_BLOCKS)}
                    s_all, dp_all, lse_all, delta_all, bias_all = (
                        jnp.concatenate([g[i] for g in group.values()], axis=0) for i in range(5))
                    p_all = jnp.exp2(s_all + bias_all - lse_all)
                    ds_all = (p_all * (dp_all - delta_all)).astype(BF16)
                    p_all = p_all.astype(BF16)
                at = slice((j % SM_BLOCKS) * 2 * CHUNK, (j % SM_BLOCKS + 1) * 2 * CHUNK)
                ds, p16 = ds_all[at, :], p_all[at, :]
                kcat, q2, do2 = group[j][5:]
                first = d == DILATIONS[0]
                _rm_add(dq_t, _residue_rows(d, j), _unstack_heads(_dot(ds, kcat), head0), first)
                ck = _dot(ds, q2, TN)
                cv = _dot(p16, do2, TN)
                _rm_add(dk_t, _residue_rows(d, j), ck[CHUNK:, :], first)
                _rm_add(dv_t, _residue_rows(d, j), cv[CHUNK:, :], first)
                if _first_in_tile(d, j):
                    rows = [(r, CHUNK - n, n) for r, _, n in _residue_rows(d, j)]
                    _rm_add(dk_sink, rows, ck[:CHUNK, :])
                    _rm_add(dv_sink, rows, cv[:CHUNK, :])
                else:
                    rows = [(r, lo - n, n) for r, lo, n in _residue_rows(d, j)]
                    _rm_add(dk_t, rows, ck[:CHUNK, :])
                    _rm_add(dv_t, rows, cv[:CHUNK, :])

        for r in range(ATT_BLOCKS):
            dqbuf[two, r] = dqbuf[two, r] * (Q_SCALE / LOG2E)
        for cp in _tile_copies(dq_hbm, dq_t, sem_dq.at[two], hp, t, to_hbm=True):
            cp.start()

        @pl.when(t > 0)
        def _():
            for cp in (_tile_copies(dk_hbm, dk_b, sem_dk.at[before], hp, t - 1, to_hbm=True, lane0=ATTN_W)
                       + _tile_copies(dv_hbm, dv_b, sem_dv.at[before], hp, t - 1, to_hbm=True, lane0=2 * ATTN_W)):
                cp.start()

        @pl.when(t == nt - 1)
        def _():
            for cp in (_tile_copies(dk_hbm, dk_t, sem_dk.at[three], hp, t, to_hbm=True, lane0=ATTN_W)
                       + _tile_copies(dv_hbm, dv_t, sem_dv.at[three], hp, t, to_hbm=True, lane0=2 * ATTN_W)):
                cp.start()

        @pl.when(step == steps - 1)
        def _():
            for slot in range(2):
                _wait_tile(dqbuf.at[slot], sem_dq.at[slot])
            for slot in range(3):
                _wait_tile(dkbuf.at[slot], sem_dk.at[slot])
                _wait_tile(dvbuf.at[slot], sem_dv.at[slot])

        if ns:
            pl.when(step == steps - 1)(finish)

    tile = lambda n: pltpu.VMEM((n, ATT_BLOCKS, CHUNK, LANES), F32)
    dma = lambda n: pltpu.SemaphoreType.DMA((n,))
    view = jax.ShapeDtypeStruct((T // ATT_BLOCKS, ATT_BLOCKS, ATTN_W), F32)
    outs = pl.pallas_call(
        body, name="attn_bwd", grid=(ATTN_W // LANES, nt),
        in_specs=[pl.BlockSpec((8, LANES), lambda c, t: (0, c))] + [_HBM] * (7 + ns),
        out_specs=[_HBM] * (1 + ns),
        out_shape=[jax.ShapeDtypeStruct((T // ATT_BLOCKS, ATT_BLOCKS, IN_W), F32)]
        + [jax.ShapeDtypeStruct(p.shape, p.dtype) for p in owner_grads],
        scratch_shapes=[tile(2), tile(2), tile(2), tile(2), tile(3), tile(3), tile(2), tile(4), tile(4),
                        pltpu.VMEM((ATT_BLOCKS, CHUNK, LANES), F32)]
        + [dma(2), dma(2), dma(2), dma(2), dma(3), dma(3), dma(2), dma(3), dma(3)]
        + (_owner_exchange_sems(ns) if ns else []),
        input_output_aliases={7: 0},
        compiler_params=_params(("arbitrary", "arbitrary")),
    )(_slope_table(), *[_residue_view(a) for a in (q, k, v, dattn, attn, lse, dproj)], *owner_grads)
    return outs[0].reshape(T, IN_W), tuple(outs[1:])


def _proj_bwd(dproj, w_in_t, x, g1, dh1, owner_grads=()):
    T = x.shape[0]
    tm = TM_PROJ
    ns = len(owner_grads)
    steps = T // tm

    def body(d_ref, w_ref, x_ref, g_ref, r_ref, *rest):
        p_refs, rest = rest[:ns], rest[ns:]
        dx_ref, dg_ref = rest[:2]
        r_refs, sems = rest[2:2 + ns], rest[2 + ns:]
        step = pl.program_id(0)
        if ns:
            start, finish = _owner_exchange_phases(p_refs, r_refs, *sems)
            pl.when(step == 0)(start)

        @pl.when(step == 0)
        def _():
            dg_ref[...] = jnp.zeros_like(dg_ref)

        dhn = _dot(d_ref[...].astype(BF16), w_ref[...])
        n1, r1 = _rms(x_ref[...])
        _accum_rows(dg_ref, dhn * n1)
        dx_ref[...] = r_ref[...] + _rms_bwd(n1, r1, g_ref[...], dhn)
        if ns:
            pl.when(step == steps - 1)(finish)

    outs = pl.pallas_call(
        body, name="proj_bwd", grid=(steps,),
        in_specs=[_rows(tm, IN_W), _resident((IN_W, D_MODEL)), _rows(tm, D_MODEL), _resident((1, D_MODEL)),
                  _rows(tm, D_MODEL)] + [_HBM] * ns,
        out_specs=[_rows(tm, D_MODEL), pl.BlockSpec((8, D_MODEL), lambda i: (0, 0))] + [_HBM] * ns,
        out_shape=[jax.ShapeDtypeStruct((T, D_MODEL), F32), jax.ShapeDtypeStruct((8, D_MODEL), F32)]
        + [jax.ShapeDtypeStruct(p.shape, p.dtype) for p in owner_grads],
        scratch_shapes=_owner_exchange_sems(ns) if ns else [],
        compiler_params=_params(("arbitrary",)),
    )(dproj, w_in_t, x, g1, dh1, *owner_grads)
    return outs[0], outs[1], tuple(outs[2:])


def _dw(a, b, name, tile, square_a=False, out_dtype=F32, shards=()):
    T, ka = a.shape
    nb = b.shape[1]
    tka, tnb, tt = tile
    tt = min(tt, T)
    last = T // tt - 1
    ns = len(shards)
    grid = (ka // tka, nb // tnb, T // tt)
    steps = grid[0] * grid[1] * grid[2]
    own_acc = out_dtype != F32

    def body(a_ref, b_ref, *refs):
        x_refs, refs = refs[:ns], refs[ns:]
        o_ref = refs[0]
        g_refs, refs = refs[1:1 + ns], refs[1 + ns:]
        acc_ref = refs[0] if own_acc else o_ref
        s = pl.program_id(2)
        if ns:
            step = (pl.program_id(0) * grid[1] + pl.program_id(1)) * grid[2] + s
            start, forward, finish = _gather_phases(x_refs, g_refs, *refs[1 if own_acc else 0:])
            pl.when(step == 0)(start)
            pl.when(step == steps // 2)(forward)

        @pl.when(s == 0)
        def _():
            acc_ref[...] = jnp.zeros_like(acc_ref)

        a_tile = a_ref[...]
        if square_a:
            a_tile = jnp.square(a_tile.astype(F32))
        acc_ref[...] += _dot(a_tile.astype(BF16), b_ref[...], TN)
        if acc_ref is not o_ref:
            @pl.when(s == last)
            def _():
                o_ref[...] = acc_ref[...].astype(out_dtype)
        if ns:
            pl.when(step == steps - 1)(finish)

    outs = pl.pallas_call(
        body, name=name, grid=grid,
        in_specs=[pl.BlockSpec((tt, tka), lambda i, j, s: (s, i)), pl.BlockSpec((tt, tnb), lambda i, j, s: (s, j))]
        + [_HBM] * ns,
        out_specs=[pl.BlockSpec((tka, tnb), lambda i, j, s: (i, j))] + [_HBM] * ns,
        out_shape=[jax.ShapeDtypeStruct((ka, nb), out_dtype)] + [_gathered_shape(s) for s in shards],
        scratch_shapes=([pltpu.VMEM((tka, tnb), F32)] if own_acc else []) + (_gather_sems(ns) if ns else []),
        compiler_params=_params(("arbitrary",) * 3 if ns else ("parallel", "parallel", "arbitrary")),
    )(a, b, *shards)
    return outs if ns else outs[0]


def _adamw_update(w, m, v, g):
    m2 = ADAM_B1 * m + (1.0 - ADAM_B1) * g
    v2 = ADAM_B2 * v + (1.0 - ADAM_B2) * jnp.square(g)
    m_hat = m2 / (1.0 - ADAM_B1 ** ADAM_STEP)
    v_hat = v2 / (1.0 - ADAM_B2 ** ADAM_STEP)
    return -ADAM_LR * (m_hat / (jnp.sqrt(v_hat) + ADAM_EPS) + ADAM_WD * w), m2, v2


def _adamw_tiny(ws, ms, vs, parts):
    n = len(ws)
    P = parts.shape[0]

    def body(*refs):
        w_refs, m_refs, v_refs, p_ref = refs[:n], refs[n:2 * n], refs[2 * n:3 * n], refs[3 * n]
        outs = refs[3 * n + 1:]

        def total(slot, rows):
            g = p_ref[0, 8 * slot:8 * slot + rows, :]
            for i in range(1, P):
                g = g + p_ref[i, 8 * slot:8 * slot + rows, :]
            return g

        for k in range(n):
            g = total(k, ws[k].shape[0])
            outs[4 * k][...] = g
            outs[4 * k + 1][...], outs[4 * k + 2][...], outs[4 * k + 3][...] = _adamw_update(
                w_refs[k][...], m_refs[k][...], v_refs[k][...], g)
        outs[4 * n][...] = total(n, 8)

    sds = jax.ShapeDtypeStruct
    return pl.pallas_call(
        body, name="adamw_tiny",
        out_shape=[sds(w.shape, F32) for w in ws for _ in range(4)] + [sds((8, LANES), F32)],
    )(*ws, *ms, *vs, parts)


def _adamw(w, m, v, parts, name, tr, transposed=False, shards=()):
    R, C = w.shape
    P = parts.shape[0]
    ns = len(shards)
    steps = R // tr

    def body(w_ref, m_ref, v_ref, p_ref, *rest):
        x_refs, rest = rest[:ns], rest[ns:]
        g_ref, d_ref, m2_ref, v2_ref = rest[:4]
        if ns:
            start, forward, finish = _gather_phases(x_refs, rest[4:4 + ns], *rest[4 + ns:])
            pl.when(pl.program_id(0) == 0)(start)
            pl.when(pl.program_id(0) == steps // 2)(forward)
        g = p_ref[0].astype(F32)
        for i in range(1, P):
            g = g + p_ref[i].astype(F32)
        if transposed:
            g = g.T
        g_ref[...] = g
        d_ref[...], m2_ref[...], v2_ref[...] = _adamw_update(w_ref[...], m_ref[...], v_ref[...], g)
        if ns:
            pl.when(pl.program_id(0) == steps - 1)(finish)

    spec = _rows(tr, C)
    part_spec = (pl.BlockSpec((P, C, tr), lambda i: (0, 0, i)) if transposed
                 else pl.BlockSpec((P, tr, C), lambda i: (0, i, 0)))
    return pl.pallas_call(
        body, name=name, grid=(steps,),
        in_specs=[spec, spec, spec, part_spec] + [_HBM] * ns,
        out_specs=[spec] * 4 + [_HBM] * ns,
        out_shape=[jax.ShapeDtypeStruct((R, C), F32)] * 4 + [_gathered_shape(s) for s in shards],
        scratch_shapes=_gather_sems(ns) if ns else [],
        compiler_params=_params(("arbitrary",) if ns else ("parallel",)),
    )(w, m, v, parts, *shards)


_HBM = pl.BlockSpec(memory_space=pltpu.HBM)


def _place():
    return lax.axis_index("x"), lax.axis_index("y"), lax.axis_index("c")


def _gathered_shape(shard):
    return jax.ShapeDtypeStruct((N_DEV,) + shard.shape, shard.dtype)


def _gather_sems(n):
    return [pltpu.SemaphoreType.DMA((7, n)), pltpu.SemaphoreType.DMA((7, n)), pltpu.SemaphoreType.DMA((n,))]


def _gather_phases(x_refs, out_refs, send_sems, recv_sems, local_sems):
    x, y, c = _place()
    me, sibling = (x, y, c), (x, y, 1 - c)
    chips = [(1 - x, y), (x, 1 - y), (1 - x, 1 - y)]
    arrays = range(len(x_refs))

    def slot(i, px, py, pc):
        return out_refs[i].at[4 * px + 2 * py + pc]

    def copy(i, k, block, to, own=False):
        return pltpu.make_async_remote_copy(
            src_ref=x_refs[i] if own else slot(i, *block), dst_ref=slot(i, *block),
            send_sem=send_sems.at[k, i], recv_sem=recv_sems.at[k, i], device_id=to, device_id_type=MESH)

    def mine(i):
        return pltpu.make_async_copy(x_refs[i], slot(i, *me), local_sems.at[i])

    def start():
        for i in arrays:
            mine(i).start()
            copy(i, 0, me, sibling, own=True).start()
            for j, chip in enumerate(chips):
                copy(i, 1 + j, me, (*chip, c), own=True).start()

    def forward():
        for i in arrays:
            for j, chip in enumerate(chips):
                copy(i, 1 + j, (*chip, c), me).wait_recv()
                copy(i, 4 + j, (*chip, c), sibling).start()

    def finish():
        for i in arrays:
            copy(i, 0, sibling, me).wait_recv()
            copy(i, 0, me, sibling, own=True).wait_send()
            for j, chip in enumerate(chips):
                copy(i, 4 + j, (*chip, 1 - c), me).wait_recv()
                copy(i, 1 + j, me, (*chip, c), own=True).wait_send()
                copy(i, 4 + j, (*chip, c), sibling).wait_send()
            mine(i).wait()

    return start, forward, finish


def _all_gather(shards, name):
    n = len(shards)

    def body(*refs):
        start, forward, finish = _gather_phases(refs[:n], refs[n:2 * n], *refs[2 * n:])
        start()
        forward()
        finish()

    return pl.pallas_call(
        body, name=name,
        out_shape=[_gathered_shape(s) for s in shards],
        in_specs=[_HBM] * n, out_specs=[_HBM] * n,
        scratch_shapes=_gather_sems(n),
    )(*shards)


def _owner_exchange_sems(n):
    return [pltpu.SemaphoreType.DMA((7, n)), pltpu.SemaphoreType.DMA((7, n)), pltpu.SemaphoreType.DMA((n,))]


def _owner_exchange_phases(g_refs, r_refs, send_sems, recv_sems, local_sems):
    x, y, c = _place()
    me = 4 * x + 2 * y + c
    flip = lambda v, bit: 1 - v if bit else v
    peers = [(flip(x, k & 4), flip(y, k & 2), flip(c, k & 1)) for k in range(1, N_DEV)]
    arrays = range(len(g_refs))

    def mine(i):
        return pltpu.make_async_copy(g_refs[i].at[me], r_refs[i].at[me], local_sems.at[i])

    def copy(i, k, src_slot, dst_slot):
        return pltpu.make_async_remote_copy(
            src_ref=g_refs[i].at[src_slot], dst_ref=r_refs[i].at[dst_slot],
            send_sem=send_sems.at[k, i], recv_sem=recv_sems.at[k, i], device_id=peers[k], device_id_type=MESH)

    def start():
        for i in arrays:
            mine(i).start()
            for k, (px, py, pc) in enumerate(peers):
                copy(i, k, 4 * px + 2 * py + pc, me).start()

    def finish():
        for i in arrays:
            for k, (px, py, pc) in enumerate(peers):
                copy(i, k, me, 4 * px + 2 * py + pc).wait_recv()
                copy(i, k, 4 * px + 2 * py + pc, me).wait_send()
            mine(i).wait()

    return start, finish


def _local_step(x, tgt, small, w_in_t, rest, exchange=False):
    g1, g2, gf = small["norm1_g"], small["norm2_g"], small["final_norm_g"].reshape(1, D_MODEL)
    ga, gg = small["attn_out_g"], small["gmlp_out_g"]
    ln_g = small["sgu_ln_g"].reshape(1, GMLP_W)
    ln_b = small["sgu_ln_b"].reshape(1, GMLP_W)
    sgu_w = small["sgu_w"][0]
    sgu_bt = small["sgu_b"][0].T

    hn1, q, k, v, u, z = _proj_fwd(x, g1, w_in_t)
    attn, lse, gathered = _attn_fwd(q, k, v, shards=rest if exchange else ())
    w_out, w_ff1_t, w_ff2 = [g.reshape(-1, D_MODEL) for g in gathered] if exchange else rest
    gm = _gmlp_fwd(u, z, ln_g, ln_b, sgu_w, sgu_bt)
    mixed, h1, hn2 = _out_fwd(attn, gm, ga, gg, w_out, x, g2)
    relu, dh2f, dh2b, loss8, dgf8 = _ffn_fwd(hn2, h1, w_ff1_t, w_ff2, gf, tgt)

    da, dh1f, dh1b, dg2 = _ffn_bwd(dh2b, dh2f, relu, h1, g2, w_ff2, w_ff1_t)
    wire = BF16 if exchange else F32
    dw_ff2 = _dw(relu, dh2b, "dw_ff2", DW_TILE, square_a=True, out_dtype=wire)
    dw_ff1_t = _dw(da, hn2, "dw_ff1", DW_TILE, out_dtype=wire)
    dattn, dgm, dga, dgg = _out_bwd(dh1b, w_out, attn, gm, ga, gg)
    dw_out = _dw(mixed, dh1b, "dw_out", DW_TILE, out_dtype=wire)
    early = [dw_out, dw_ff1_t, dw_ff2]
    if exchange:
        early = [g.reshape(N_DEV, -1, D_MODEL) for g in early]
    dproj, dlg, dlb, dsw, dsb = _gmlp_bwd(u, z, dgm, ln_g, ln_b, sgu_w, sgu_bt)
    dproj, arrived = _attn_bwd(q, k, v, dattn, attn, lse, dproj, owner_grads=early if exchange else ())
    sgu_parts = None
    if exchange:
        dw_in_t, sgu_parts = _dw(dproj, hn1, "dw_in", DW_TILE_IN, out_dtype=wire,
                                 shards=[_as_rows(dsw).astype(BF16)])
    else:
        dw_in_t = _dw(dproj, hn1, "dw_in", DW_TILE_IN, out_dtype=wire)
    late =(dw_in_t.reshape(N_DEV, -1, D_MODEL),) if exchange else ()
    dx, dg1, late = _proj_bwd(dproj, w_in_t, x, g1, dh1f, owner_grads=late)
    if exchange:
        dw_in_t, early = late[0], arrived

    small_grads = dict(
        norm1_g=dg1[0], sgu_ln_g=dlg[0], sgu_ln_b=dlb[0], sgu_w=dsw, sgu_b=dsb[:, :N_GROUPS].T,
        attn_out_g=dga[0], gmlp_out_g=dgg[0], norm2_g=dg2[0], final_norm_g=dgf8[0])
    if exchange:
        small_grads["sgu_w_parts"] = sgu_parts
    return loss8[0, 0], dx, (dw_in_t, *early), small_grads


SMALL_NAMES = ("norm1_g", "sgu_ln_g", "sgu_ln_b", "sgu_w", "sgu_b", "attn_out_g", "gmlp_out_g", "norm2_g",
               "final_norm_g")
WEIGHT_ORDER = ("norm1_g", "w_in", "sgu_ln_g", "sgu_ln_b", "sgu_w", "sgu_b", "attn_out_g", "gmlp_out_g", "w_out",
                "norm2_g", "w_ff1", "w_ff2", "final_norm_g")


TINY_NAMES = tuple(n for n in SMALL_NAMES if n != "sgu_w")


def _as_rows(a):
    return a.reshape(-1, LANES)


def _pack_tiny_grads(d, loss):
    slots = [jnp.pad(_as_rows(d[n]), ((0, 8 - d[n].size // LANES), (0, 0))) for n in TINY_NAMES]
    return jnp.concatenate(slots + [jnp.full((8, LANES), loss, F32)], axis=0)


def kernel(x, norm1_g, w_in, sgu_ln_g, sgu_ln_b, sgu_w, sgu_b, attn_out_g, gmlp_out_g, w_out, norm2_g, w_ff1, w_ff2, final_norm_g, loss_target, m_norm1_g, m_w_in, m_sgu_ln_g, m_sgu_ln_b, m_sgu_w, m_sgu_b, m_attn_out_g, m_gmlp_out_g, m_w_out, m_norm2_g, m_w_ff1, m_w_ff2, m_final_norm_g, v_norm1_g, v_w_in, v_sgu_ln_g, v_sgu_ln_b, v_sgu_w, v_sgu_b, v_attn_out_g, v_gmlp_out_g, v_w_out, v_norm2_g, v_w_ff1, v_w_ff2, v_final_norm_g):
    w = dict(norm1_g=norm1_g, w_in=w_in, sgu_ln_g=sgu_ln_g, sgu_ln_b=sgu_ln_b, sgu_w=sgu_w, sgu_b=sgu_b,
             attn_out_g=attn_out_g, gmlp_out_g=gmlp_out_g, w_out=w_out, norm2_g=norm2_g, w_ff1=w_ff1, w_ff2=w_ff2,
             final_norm_g=final_norm_g)
    m = dict(norm1_g=m_norm1_g, w_in=m_w_in, sgu_ln_g=m_sgu_ln_g, sgu_ln_b=m_sgu_ln_b, sgu_w=m_sgu_w, sgu_b=m_sgu_b,
             attn_out_g=m_attn_out_g, gmlp_out_g=m_gmlp_out_g, w_out=m_w_out, norm2_g=m_norm2_g, w_ff1=m_w_ff1,
             w_ff2=m_w_ff2, final_norm_g=m_final_norm_g)
    v = dict(norm1_g=v_norm1_g, w_in=v_w_in, sgu_ln_g=v_sgu_ln_g, sgu_ln_b=v_sgu_ln_b, sgu_w=v_sgu_w, sgu_b=v_sgu_b,
             attn_out_g=v_attn_out_g, gmlp_out_g=v_gmlp_out_g, w_out=v_w_out, norm2_g=v_norm2_g, w_ff1=v_w_ff1,
             w_ff2=v_w_ff2, final_norm_g=v_final_norm_g)
    big = ("w_in", "w_out", "w_ff1", "w_ff2")

    w_in_t, = _all_gather([w_in[0].T.astype(BF16)], "w_in_all_gather")
    rest = (w_out[0].astype(BF16), w_ff1[0].T.astype(BF16), w_ff2[0].astype(BF16))
    loss, dx, parts, small_grads = _local_step(x[0], loss_target[0], {n: w[n] for n in SMALL_NAMES},
                                               w_in_t.reshape(IN_W, D_MODEL), rest, exchange=True)

    small_parts = [_pack_tiny_grads(small_grads, loss)]
    sgu_parts = small_grads["sgu_w_parts"]
    new = {}
    for n, p, transposed, tr in zip(big, parts, (True, False, True, False), (128, 128, 128, 256)):
        res = _adamw(w[n][0], m[n][0], v[n][0], p, "adamw_" + n, tr, transposed,
                     shards=small_parts if n == "w_in" else ())
        new[n] = [a[None] for a in res[:4]]
        if n == "w_in":
            tiny_parts, = res[4:]
    tiny = _adamw_tiny(*[[_as_rows(src[n]) for n in TINY_NAMES] for src in (w, m, v)], tiny_parts)
    sgu = _adamw(_as_rows(sgu_w), _as_rows(m_sgu_w), _as_rows(v_sgu_w), sgu_parts, "adamw_sgu_w", 512)
    loss = tiny[-1][0, 0]

    outs = []
    for i in range(4):
        d = {n: new[n][i] for n in big}
        d.update({n: tiny[4 * k + i].reshape(w[n].shape) for k, n in enumerate(TINY_NAMES)})
        d["sgu_w"] = sgu[i].reshape(sgu_w.shape)
        outs.extend(d[n] for n in WEIGHT_ORDER)
    return (loss, dx[None], *outs)
```

```python
import math

import numpy as np
import jax
import jax.numpy as jnp
from jax import lax
from jax.experimental import pallas as pl
from jax.experimental.pallas import tpu as pltpu

F32 = jnp.float32
BF16 = jnp.bfloat16

D_MODEL = 1024
HEAD_DIM = 64
N_HEADS = 12
ATTN_W = N_HEADS * HEAD_DIM
N_GROUPS = 4
GMLP_W = N_GROUPS * HEAD_DIM
IN_W = 3 * ATTN_W + 2 * GMLP_W
D_FF = 4 * D_MODEL
CHUNK = 128
DILATIONS = (1, 4, 16)
EPS = 1e-6
Q_SCALE = HEAD_DIM ** -0.5
LOG2E = 1.4426950408889634
NEG = -1e30

ADAM_LR, ADAM_B1, ADAM_B2, ADAM_EPS, ADAM_WD, ADAM_STEP = 0.001, 0.9, 0.999, 1e-08, 0.01, 10

N_DEV = 8
LANES = 128
VMEM_LIMIT = 56 << 20

TM_PROJ = 512
TM_FFN = 512
FF_CHUNK = 512
TM_GMLP = 1024
DW_TILE = (512, 1024, 8192)
DW_TILE_IN = (IN_W // 2, 1024, 2048)

MESH = pl.DeviceIdType.MESH


def _alibi_slopes(n):
    def pow2(m):
        start = 2.0 ** (-8.0 / m)
        return [start ** (i + 1) for i in range(m)]
    c = 2 ** int(math.floor(math.log2(n)))
    s = pow2(n) if c == n else pow2(c) + pow2(2 * c)[0::2][: n - c]
    return np.asarray(s, dtype=np.float32)


SLOPES = _alibi_slopes(N_HEADS)


def _params(sem=None):
    kw = dict(vmem_limit_bytes=VMEM_LIMIT)
    if sem is not None:
        kw["dimension_semantics"] = sem
    return pltpu.CompilerParams(**kw)


def _rows(tm, n):
    return pl.BlockSpec((tm, n), lambda i: (i, 0))


def _resident(shape):
    return pl.BlockSpec(shape, lambda *_: (0,) * len(shape), pipeline_mode=pl.Buffered(1))


def _rms(x):
    r = lax.rsqrt(jnp.mean(x * x, axis=-1, keepdims=True) + EPS)
    return x * r, r


def _rms_bwd(n, r, g, dy):
    dn = dy * g
    return r * (dn - n * jnp.mean(dn * n, axis=-1, keepdims=True))


def _accum_rows(acc_ref, v):
    acc_ref[...] += jnp.broadcast_to(jnp.sum(v, axis=0, keepdims=True), acc_ref.shape)


_G0 = math.sqrt(2.0 / math.pi)
_G1 = 0.044715


def _gelu(x):
    t = jnp.tanh(_G0 * (x + _G1 * (x * x * x)))
    return x * (0.5 * (1.0 + t)), t


def _gelu_grad(x, t):
    return 0.5 * (1.0 + t) + 0.5 * x * (1.0 - t * t) * (_G0 * (1.0 + 3.0 * _G1 * x * x))


NT = (((1,), (1,)), ((), ()))
TN = (((0,), (0,)), ((), ()))


def _dot(a, b, dims=None):
    if dims is None:
        return jnp.dot(a, b, preferred_element_type=F32)
    return lax.dot_general(a, b, dims, preferred_element_type=F32)


def _proj_fwd(x, g1, w_in_t):
    T = x.shape[0]
    tm = TM_PROJ

    def body(x_ref, g_ref, w_ref, hn_ref, q_ref, k_ref, v_ref, u_ref, z_ref):
        n, _ = _rms(x_ref[...])
        hn = (n * g_ref[...]).astype(BF16)
        hn_ref[...] = hn
        a = ATTN_W
        q_ref[...] = _dot(hn, w_ref[0:a, :], NT) * Q_SCALE
        k_ref[...] = _dot(hn, w_ref[a:2 * a, :], NT) * LOG2E
        v_ref[...] = _dot(hn, w_ref[2 * a:3 * a, :], NT)
        u_ref[...] = _dot(hn, w_ref[3 * a:3 * a + GMLP_W, :], NT)
        z_ref[...] = _dot(hn, w_ref[3 * a + GMLP_W:, :], NT)

    sds = jax.ShapeDtypeStruct
    return pl.pallas_call(
        body, name="proj_fwd", grid=(T // tm,),
        in_specs=[_rows(tm, D_MODEL), _resident((1, D_MODEL)), _resident((IN_W, D_MODEL))],
        out_specs=[_rows(tm, D_MODEL), _rows(tm, ATTN_W), _rows(tm, ATTN_W), _rows(tm, ATTN_W),
                   _rows(tm, GMLP_W), _rows(tm, GMLP_W)],
        out_shape=[sds((T, D_MODEL), BF16), sds((T, ATTN_W), F32), sds((T, ATTN_W), F32),
                   sds((T, ATTN_W), F32), sds((T, GMLP_W), F32), sds((T, GMLP_W), F32)],
        compiler_params=_params(("parallel",)),
    )(x, g1, w_in_t)


ATT_TILE = 2048
ATT_BLOCKS = ATT_TILE // CHUNK
SM_BLOCKS = 4


def _slope_table():
    row = np.repeat(SLOPES, HEAD_DIM)
    return jnp.asarray(np.broadcast_to(row[None], (8, ATTN_W)), F32)


def _residue_view(a):
    return a.reshape(a.shape[0] // ATT_BLOCKS, ATT_BLOCKS, a.shape[1])


def _tile_copies(hbm, buf, sem, hp, t, to_hbm=False, lane0=0):
    rows = pl.ds(pl.multiple_of(t * CHUNK, CHUNK), CHUNK)
    lanes = pl.ds(pl.multiple_of(lane0 + hp * LANES, LANES), LANES)
    pairs = [(hbm.at[rows, r, lanes], buf.at[r]) for r in range(ATT_BLOCKS)]
    return [pltpu.make_async_copy(v, h, sem) if to_hbm else pltpu.make_async_copy(h, v, sem) for h, v in pairs]


def _wait_tile(buf, sem):
    pltpu.make_async_copy(buf, buf, sem).wait()


def _residue_rows(d, j):
    if d == 16:
        return [(j, 0, CHUNK)]
    if d == 4:
        return [(j % 4 + 4 * m, 32 * (j // 4), 32) for m in range(4)]
    return [(r, 8 * j, 8) for r in range(ATT_BLOCKS)]


def _block_order(p, d):
    if d == 16:
        return p
    if d == 4:
        return 4 * (p & 31) + (p >> 5)
    return 16 * (p & 7) + (p >> 3)


def _first_in_tile(d, j):
    return _residue_rows(d, j)[0][1] == 0


def _rm_block(buf, d, j):
    return jnp.concatenate([buf[r, lo:lo + n, :] for r, lo, n in _residue_rows(d, j)], axis=0)


def _rm_block_before(buf, buf_before, d, j):
    if _first_in_tile(d, j):
        return jnp.concatenate([buf_before[r, CHUNK - n:CHUNK, :] for r, _, n in _residue_rows(d, j)], axis=0)
    return jnp.concatenate([buf[r, lo - n:lo, :] for r, lo, n in _residue_rows(d, j)], axis=0)


def _rm_store(buf, d, j, val):
    at = 0
    for r, lo, n in _residue_rows(d, j):
        buf[r, lo:lo + n, :] = val[at:at + n, :]
        at += n


def _rm_add(buf, rows, val, first=False):
    at = 0
    for r, lo, n in rows:
        if first:
            buf[r, lo:lo + n, :] = val[at:at + n, :]
        else:
            buf[r, lo:lo + n, :] += val[at:at + n, :]
        at += n


def _residue_bias(sl_ref, d):
    shape = (2 * CHUNK, 2 * CHUNK)
    row = lax.broadcasted_iota(jnp.int32, shape, 0)
    col = lax.broadcasted_iota(jnp.int32, shape, 1)
    steps = _block_order(row & (CHUNK - 1), d) + CHUNK - (_block_order(col & (CHUNK - 1), d) + (col & CHUNK))
    band = (steps >= 0) & (steps <= CHUNK)
    sl = sl_ref[0:1, :]
    upper = lax.broadcasted_iota(jnp.int32, (2 * CHUNK, 1), 0) < CHUNK
    slope2 = jnp.where(upper, sl[:, 0:1], sl[:, HEAD_DIM:HEAD_DIM + 1])
    return jnp.where(band, -(float(d) * LOG2E * slope2 * steps.astype(F32)), NEG)


def _stack_heads(xb, head0):
    zero = jnp.zeros_like(xb)
    return jnp.concatenate([jnp.where(head0, xb, zero), jnp.where(head0, zero, xb)], axis=0).astype(BF16)


def _unstack_heads(x2, head0):
    return jnp.where(head0, x2[:CHUNK, :], x2[CHUNK:, :])


def _attn_fwd(q, k, v, shards=()):
    T = q.shape[0]
    nt = T // ATT_TILE
    ns = len(shards)
    steps = (ATTN_W // LANES) * nt

    def body(sl_ref, q_hbm, k_hbm, v_hbm, *rest):
        x_refs, rest = rest[:ns], rest[ns:]
        attn_hbm, lse_hbm = rest[:2]
        g_refs, rest = rest[2:2 + ns], rest[2 + ns:]
        qbuf, kbuf, vbuf, obuf, lbuf = rest[:5]
        o_acc, l_acc = rest[5:8], rest[8:11]
        sem_q, sem_k, sem_v, sem_o, sem_l = rest[11:16]
        hp, t = pl.program_id(0), pl.program_id(1)
        step = hp * nt + t
        two, three = step % 2, step % 3
        before, after = (step + 2) % 3, (step + 1) % 3
        if ns:
            start, forward, finish = _gather_phases(x_refs, g_refs, *rest[16:])
            pl.when(step == 0)(start)
            pl.when(step == (3 * steps) // 4)(forward)

        def fetch(hp_, t_, two_, three_):
            for cp in (_tile_copies(q_hbm, qbuf.at[two_], sem_q.at[two_], hp_, t_)
                       + _tile_copies(k_hbm, kbuf.at[three_], sem_k.at[three_], hp_, t_)
                       + _tile_copies(v_hbm, vbuf.at[three_], sem_v.at[three_], hp_, t_)):
                cp.start()

        @pl.when(step == 0)
        def _():
            kbuf[2] = jnp.zeros((ATT_BLOCKS, CHUNK, LANES), F32)
            vbuf[2] = jnp.zeros((ATT_BLOCKS, CHUNK, LANES), F32)
            fetch(0, 0, 0, 0)

        @pl.when(step + 1 < steps)
        def _():
            fetch((step + 1) // nt, (step + 1) % nt, 1 - two, after)

        _wait_tile(qbuf.at[two], sem_q.at[two])
        _wait_tile(kbuf.at[three], sem_k.at[three])
        _wait_tile(vbuf.at[three], sem_v.at[three])

        @pl.when(step >= 2)
        def _():
            _wait_tile(obuf.at[two], sem_o.at[two])
            _wait_tile(lbuf.at[two], sem_l.at[two])

        q_t, k_t, v_t = qbuf.at[two], kbuf.at[three], vbuf.at[three]
        k_b, v_b = kbuf.at[before], vbuf.at[before]
        head0 = lax.broadcasted_iota(jnp.int32, (CHUNK, LANES), 1) < HEAD_DIM
        no_key_before = jnp.where(lax.broadcasted_iota(jnp.int32, (2 * CHUNK, 2 * CHUNK), 1) < CHUNK, NEG, 0.0)
        for pi, d in enumerate(DILATIONS):
            bias = _residue_bias(sl_ref, d)

            def scores(j, d=d, bias=bias):
                kcat = jnp.concatenate([_rm_block_before(k_t, k_b, d, j), _rm_block(k_t, d, j)], axis=0).astype(BF16)
                vcat = jnp.concatenate([_rm_block_before(v_t, v_b, d, j), _rm_block(v_t, d, j)], axis=0).astype(BF16)
                s = _dot(_stack_heads(_rm_block(q_t, d, j), head0), kcat, NT)
                return s, vcat, bias_first if _first_in_tile(d, j) else bias

            bias_first = bias + jnp.where(t == 0, 1.0, 0.0) * no_key_before
            for j0 in range(0, ATT_BLOCKS, SM_BLOCKS):
                group = [scores(j) for j in range(j0, j0 + SM_BLOCKS)]
                s = jnp.concatenate([g[0] for g in group], axis=0) + jnp.concatenate([g[2] for g in group], axis=0)
                m = jnp.max(s, axis=-1, keepdims=True)
                p = jnp.exp2(s - m)
                l = jnp.sum(p, axis=-1, keepdims=True)
                p = p.astype(BF16)
                block = lambda a, i: a[i * 2 * CHUNK:(i + 1) * 2 * CHUNK, :]
                o = jnp.concatenate([_dot(block(p, i), g[1]) for i, g in enumerate(group)], axis=0) * (1.0 / l)
                lse = jnp.broadcast_to(m + jnp.log2(l), o.shape)
                for i in range(SM_BLOCKS):
                    _rm_store(o_acc[pi], d, j0 + i, _unstack_heads(block(o, i), head0))
                    _rm_store(l_acc[pi], d, j0 + i, _unstack_heads(block(lse, i), head0))

        for r in range(ATT_BLOCKS):
            a, b, c = l_acc[0][r], l_acc[1][r], l_acc[2][r]
            m = jnp.maximum(jnp.maximum(a, b), c)
            ea, eb, ec = jnp.exp2(a - m), jnp.exp2(b - m), jnp.exp2(c - m)
            tot = ea + eb + ec
            obuf[two, r] = (ea * o_acc[0][r] + eb * o_acc[1][r] + ec * o_acc[2][r]) / tot
            lbuf[two, r] = m + jnp.log2(tot)

        for cp in (_tile_copies(attn_hbm, obuf.at[two], sem_o.at[two], hp, t, to_hbm=True)
                   + _tile_copies(lse_hbm, lbuf.at[two], sem_l.at[two], hp, t, to_hbm=True)):
            cp.start()

        @pl.when(step == steps - 1)
        def _():
            for slot in (two, 1 - two)[:min(steps, 2)]:
                _wait_tile(obuf.at[slot], sem_o.at[slot])
                _wait_tile(lbuf.at[slot], sem_l.at[slot])

        if ns:
            pl.when(step == steps - 1)(finish)

    tile = lambda n: pltpu.VMEM((n, ATT_BLOCKS, CHUNK, LANES), F32)
    dma = lambda n: pltpu.SemaphoreType.DMA((n,))
    view = jax.ShapeDtypeStruct((T // ATT_BLOCKS, ATT_BLOCKS, ATTN_W), F32)
    outs = pl.pallas_call(
        body, name="attn_fwd", grid=(ATTN_W // LANES, nt),
        in_specs=[pl.BlockSpec((8, LANES), lambda c, t: (0, c))] + [_HBM] * (3 + ns),
        out_specs=[_HBM] * (2 + ns),
        out_shape=[view, view] + [_gathered_shape(s) for s in shards],
        scratch_shapes=[tile(2), tile(3), tile(3), tile(2), tile(2)] + [pltpu.VMEM((ATT_BLOCKS, CHUNK, LANES), F32)] * 6
        + [dma(2), dma(3), dma(3), dma(2), dma(2)] + (_gather_sems(ns) if ns else []),
        compiler_params=_params(("arbitrary", "arbitrary")),
    )(_slope_table(), _residue_view(q), _residue_view(k), _residue_view(v), *shards)
    return outs[0].reshape(T, ATTN_W), outs[1].reshape(T, ATTN_W), tuple(outs[2:])


def _group_mean(v, grp):
    halves = []
    for h in range(GMLP_W // LANES):
        x = v[:, h * LANES:(h + 1) * LANES]
        low = grp[:, h * LANES:(h + 1) * LANES] == 2 * h
        a = jnp.sum(jnp.where(low, x, 0.0), axis=-1, keepdims=True)
        b = jnp.sum(jnp.where(low, 0.0, x), axis=-1, keepdims=True)
        halves.append(jnp.where(low, a, b) * (1.0 / HEAD_DIM))
    return jnp.concatenate(halves, axis=1)


def _gmlp_core(uu, zz, lg, lb, ws, sb_ref, grp):
    ug, tu = _gelu(uu)
    zg, tz = _gelu(zz)
    zc = zg - _group_mean(zg, grp)
    rstd = lax.rsqrt(_group_mean(zc * zc, grp) + EPS)
    xhat = zc * rstd
    zn16 = (xhat * lg + lb).astype(BF16)
    low = grp[:CHUNK, :LANES] == 0
    mixed = []
    for ci in range(uu.shape[0] // CHUNK):
        rows = slice(ci * CHUNK, (ci + 1) * CHUNK)
        halves = []
        for h in range(GMLP_W // LANES):
            zh = zn16[rows, h * LANES:(h + 1) * LANES]
            halves.append(jnp.where(low, _dot(ws[2 * h], zh) + sb_ref[:, 2 * h:2 * h + 1],
                                    _dot(ws[2 * h + 1], zh) + sb_ref[:, 2 * h + 1:2 * h + 2]))
        mixed.append(jnp.concatenate(halves, axis=1))
    return ug, tu, tz, xhat, rstd, zn16, jnp.concatenate(mixed, axis=0)


def _causal_ws(w_ref):
    ti = lax.broadcasted_iota(jnp.int32, (CHUNK, CHUNK), 0)
    si = lax.broadcasted_iota(jnp.int32, (CHUNK, CHUNK), 1)
    causal = si <= ti
    return causal, [jnp.where(causal, w_ref[g], 0.0).astype(BF16) for g in range(N_GROUPS)]


def _gmlp_fwd(u, z, ln_g, ln_b, sgu_w, sgu_bt):
    T = u.shape[0]
    tg = TM_GMLP

    def body(u_ref, z_ref, g_ref, b_ref, w_ref, sb_ref, out_ref):
        grp = lax.broadcasted_iota(jnp.int32, (tg, GMLP_W), 1) // HEAD_DIM
        _, ws = _causal_ws(w_ref)
        ug, _, _, _, _, _, mixed = _gmlp_core(u_ref[...], z_ref[...], g_ref[...], b_ref[...], ws, sb_ref, grp)
        out_ref[...] = ug * mixed

    return pl.pallas_call(
        body, name="gmlp_fwd", grid=(T // tg,),
        in_specs=[_rows(tg, GMLP_W), _rows(tg, GMLP_W), _resident((1, GMLP_W)), _resident((1, GMLP_W)),
                  _resident((N_GROUPS, CHUNK, CHUNK)), _resident((CHUNK, N_GROUPS))],
        out_specs=_rows(tg, GMLP_W),
        out_shape=jax.ShapeDtypeStruct((T, GMLP_W), F32),
        compiler_params=_params(("parallel",)),
    )(u, z, ln_g, ln_b, sgu_w, sgu_bt)


def _out_fwd(attn, gm, ga, gg, w_out, x, g2):
    T = x.shape[0]
    tm = TM_PROJ

    def body(a_ref, m_ref, ga_ref, gg_ref, w_ref, x_ref, g2_ref, mix_ref, h1_ref, hn2_ref):
        an, _ = _rms(a_ref[...])
        gn, _ = _rms(m_ref[...])
        an = (an * ga_ref[...]).astype(BF16)
        gn = (gn * gg_ref[...]).astype(BF16)
        mix_ref[:, 0:ATTN_W] = an
        mix_ref[:, ATTN_W:] = gn
        h1 = x_ref[...] + _dot(an, w_ref[0:ATTN_W, :]) + _dot(gn, w_ref[ATTN_W:, :])
        h1_ref[...] = h1
        n2, _ = _rms(h1)
        hn2_ref[...] = (n2 * g2_ref[...]).astype(BF16)

    sds = jax.ShapeDtypeStruct
    return pl.pallas_call(
        body, name="out_fwd", grid=(T // tm,),
        in_specs=[_rows(tm, ATTN_W), _rows(tm, GMLP_W), _resident((1, ATTN_W)), _resident((1, GMLP_W)),
                  _resident((D_MODEL, D_MODEL)), _rows(tm, D_MODEL), _resident((1, D_MODEL))],
        out_specs=[_rows(tm, D_MODEL)] * 3,
        out_shape=[sds((T, D_MODEL), BF16), sds((T, D_MODEL), F32), sds((T, D_MODEL), BF16)],
        compiler_params=_params(("parallel",)),
    )(attn, gm, ga, gg, w_out, x, g2)


def _ffn_fwd(hn2, h1, w1t, w2, gf, tgt):
    T = h1.shape[0]
    tm = TM_FFN

    def body(hn_ref, h1_ref, w1_ref, w2_ref, gf_ref, t_ref, r_ref, dhf_ref, dhb_ref, loss_ref, dgf_ref):
        i = pl.program_id(0)

        @pl.when(i == 0)
        def _():
            loss_ref[...] = jnp.zeros_like(loss_ref)
            dgf_ref[...] = jnp.zeros_like(dgf_ref)

        hn = hn_ref[...]
        acc = h1_ref[...]
        for j in range(D_FF // FF_CHUNK):
            cols = slice(j * FF_CHUNK, (j + 1) * FF_CHUNK)
            r = jnp.maximum(_dot(hn, w1_ref[cols, :], NT), 0.0)
            r_ref[:, cols] = r.astype(BF16)
            act = jnp.square(r).astype(BF16)
            acc = acc + _dot(act, w2_ref[cols, :])
        n3, r3 = _rms(acc)
        gf_row = gf_ref[...]
        e = n3 * gf_row - t_ref[...]
        loss_ref[...] += 0.5 * jnp.sum(jnp.mean(e * e, axis=-1, keepdims=True))
        dy = e * (1.0 / D_MODEL)
        _accum_rows(dgf_ref, dy * n3)
        dh2 = _rms_bwd(n3, r3, gf_row, dy)
        dhf_ref[...] = dh2
        dhb_ref[...] = dh2.astype(BF16)

    sds = jax.ShapeDtypeStruct
    acc_spec = lambda n: pl.BlockSpec((8, n), lambda i: (0, 0))
    return pl.pallas_call(
        body, name="ffn_fwd", grid=(T // tm,),
        in_specs=[_rows(tm, D_MODEL), _rows(tm, D_MODEL), _resident((D_FF, D_MODEL)), _resident((D_FF, D_MODEL)),
                  _resident((1, D_MODEL)), _rows(tm, D_MODEL)],
        out_specs=[_rows(tm, D_FF), _rows(tm, D_MODEL), _rows(tm, D_MODEL), acc_spec(LANES), acc_spec(D_MODEL)],
        out_shape=[sds((T, D_FF), BF16), sds((T, D_MODEL), F32), sds((T, D_MODEL), BF16),
                   sds((8, LANES), F32), sds((8, D_MODEL), F32)],
        compiler_params=_params(("arbitrary",)),
    )(hn2, h1, w1t, w2, gf, tgt)


def _ffn_bwd(dh2b, dh2f, relu, h1, g2, w2, w1t):
    T = h1.shape[0]
    tm = TM_FFN

    def body(db_ref, df_ref, r_ref, h1_ref, g2_ref, w2_ref, w1t_ref, da_ref, d1f_ref, d1b_ref, dg_ref):
        @pl.when(pl.program_id(0) == 0)
        def _():
            dg_ref[...] = jnp.zeros_like(dg_ref)

        db = db_ref[...]
        acc = jnp.zeros((tm, D_MODEL), F32)
        for j in range(D_FF // FF_CHUNK):
            cols = slice(j * FF_CHUNK, (j + 1) * FF_CHUNK)
            da = (_dot(db, w2_ref[cols, :], NT) * (2.0 * r_ref[:, cols].astype(F32))).astype(BF16)
            da_ref[:, cols] = da
            acc = acc + _dot(da, w1t_ref[cols, :])
        n2, r2 = _rms(h1_ref[...])
        _accum_rows(dg_ref, acc * n2)
        dh1 = df_ref[...] + _rms_bwd(n2, r2, g2_ref[...], acc)
        d1f_ref[...] = dh1
        d1b_ref[...] = dh1.astype(BF16)

    sds = jax.ShapeDtypeStruct
    return pl.pallas_call(
        body, name="ffn_bwd", grid=(T // tm,),
        in_specs=[_rows(tm, D_MODEL), _rows(tm, D_MODEL), _rows(tm, D_FF), _rows(tm, D_MODEL),
                  _resident((1, D_MODEL)), _resident((D_FF, D_MODEL)), _resident((D_FF, D_MODEL))],
        out_specs=[_rows(tm, D_FF), _rows(tm, D_MODEL), _rows(tm, D_MODEL),
                   pl.BlockSpec((8, D_MODEL), lambda i: (0, 0))],
        out_shape=[sds((T, D_FF), BF16), sds((T, D_MODEL), F32), sds((T, D_MODEL), BF16), sds((8, D_MODEL), F32)],
        compiler_params=_params(("arbitrary",)),
    )(dh2b, dh2f, relu, h1, g2, w2, w1t)


def _out_bwd(dh1b, w_out, attn, gm, ga, gg):
    T = attn.shape[0]
    tm = TM_PROJ

    def body(d_ref, w_ref, a_ref, m_ref, ga_ref, gg_ref, da_ref, dm_ref, dga_ref, dgg_ref):
        @pl.when(pl.program_id(0) == 0)
        def _():
            dga_ref[...] = jnp.zeros_like(dga_ref)
            dgg_ref[...] = jnp.zeros_like(dgg_ref)

        d = d_ref[...]
        dan = _dot(d, w_ref[0:ATTN_W, :], NT)
        dgn = _dot(d, w_ref[ATTN_W:, :], NT)
        na, ra = _rms(a_ref[...])
        ng, rg = _rms(m_ref[...])
        _accum_rows(dga_ref, dan * na)
        _accum_rows(dgg_ref, dgn * ng)
        da_ref[...] = _rms_bwd(na, ra, ga_ref[...], dan)
        dm_ref[...] = _rms_bwd(ng, rg, gg_ref[...], dgn)

    sds = jax.ShapeDtypeStruct
    return pl.pallas_call(
        body, name="out_bwd", grid=(T // tm,),
        in_specs=[_rows(tm, D_MODEL), _resident((D_MODEL, D_MODEL)), _rows(tm, ATTN_W), _rows(tm, GMLP_W),
                  _resident((1, ATTN_W)), _resident((1, GMLP_W))],
        out_specs=[_rows(tm, ATTN_W), _rows(tm, GMLP_W), pl.BlockSpec((8, ATTN_W), lambda i: (0, 0)),
                   pl.BlockSpec((8, GMLP_W), lambda i: (0, 0))],
        out_shape=[sds((T, ATTN_W), F32), sds((T, GMLP_W), F32), sds((8, ATTN_W), F32), sds((8, GMLP_W), F32)],
        compiler_params=_params(("arbitrary",)),
    )(dh1b, w_out, attn, gm, ga, gg)


def _gmlp_bwd(u, z, dgm, ln_g, ln_b, sgu_w, sgu_bt):
    T = u.shape[0]
    tg = TM_GMLP
    nsteps = T // tg

    def body(u_ref, z_ref, d_ref, g_ref, b_ref, w_ref, sb_ref, dproj_hbm, dlg_ref, dlb_ref, dw_ref, dsb_ref,
             stage, sem):
        i = pl.program_id(0)
        slot = i % 2
        duz_ref = stage.at[slot]

        def to_dproj(step, buf):
            rows = pl.ds(pl.multiple_of(step * tg, tg), tg)
            return pltpu.make_async_copy(stage.at[buf], dproj_hbm.at[rows, pl.ds(3 * ATTN_W, 2 * GMLP_W)],
                                         sem.at[buf])

        @pl.when(i == 0)
        def _():
            for ref in (dlg_ref, dlb_ref, dw_ref, dsb_ref):
                ref[...] = jnp.zeros_like(ref)

        @pl.when(i >= 2)
        def _():
            to_dproj(i - 2, slot).wait()

        grp = lax.broadcasted_iota(jnp.int32, (tg, GMLP_W), 1) // HEAD_DIM
        lane = lax.broadcasted_iota(jnp.int32, (CHUNK, LANES), 1)
        causal, ws = _causal_ws(w_ref)
        lg = g_ref[...]
        uu, zz, dgm = u_ref[...], z_ref[...], d_ref[...]
        ug, tu, tz, xhat, rstd, zn16, mixed = _gmlp_core(uu, zz, lg, b_ref[...], ws, sb_ref, grp)
        dmx = dgm * ug
        duz_ref[:, 0:GMLP_W] = dgm * mixed * _gelu_grad(uu, tu)
        dmx16 = dmx.astype(BF16)
        low = grp[:CHUNK, :LANES] == 0
        dzn = []
        for ci in range(tg // CHUNK):
            rows = slice(ci * CHUNK, (ci + 1) * CHUNK)
            halves = []
            for h in range(GMLP_W // LANES):
                lanes = slice(h * LANES, (h + 1) * LANES)
                dmx_h, zn_h, zero = dmx16[rows, lanes], zn16[rows, lanes], jnp.zeros((CHUNK, LANES), BF16)
                halves.append(jnp.where(low, _dot(ws[2 * h], dmx_h, TN), _dot(ws[2 * h + 1], dmx_h, TN)))
                dw_ref[2 * h] += _dot(jnp.where(low, dmx_h, zero), zn_h, NT)
                dw_ref[2 * h + 1] += _dot(jnp.where(low, zero, dmx_h), zn_h, NT)
            dzn.append(jnp.concatenate(halves, axis=1))
        dzn = jnp.concatenate(dzn, axis=0)
        dsb = jnp.zeros((CHUNK, LANES), F32)
        for g in range(N_GROUPS):
            half = slice((g // 2) * LANES, (g // 2 + 1) * LANES)
            per_token = jnp.sum(jnp.where(grp[:, half] == g, dmx[:, half], 0.0), axis=-1, keepdims=True)
            by_position = sum(per_token[ci * CHUNK:(ci + 1) * CHUNK] for ci in range(tg // CHUNK))
            dsb = jnp.where(lane == g, by_position, dsb)
        dsb_ref[...] += dsb
        _accum_rows(dlg_ref, dzn * xhat)
        _accum_rows(dlb_ref, dzn)
        dxh = dzn * lg
        dzg = rstd * (dxh - _group_mean(dxh, grp) - xhat * _group_mean(dxh * xhat, grp))
        duz_ref[:, GMLP_W:] = dzg * _gelu_grad(zz, tz)
        to_dproj(i, slot).start()

        @pl.when(i == nsteps - 1)
        def _():
            for g in range(N_GROUPS):
                dw_ref[g] = jnp.where(causal, dw_ref[g], 0.0)
            to_dproj(i, slot).wait()
            if nsteps >= 2:
                to_dproj(i - 1, 1 - slot).wait()

    sds = jax.ShapeDtypeStruct
    return pl.pallas_call(
        body, name="gmlp_bwd", grid=(nsteps,),
        in_specs=[_rows(tg, GMLP_W)] * 3 + [_resident((1, GMLP_W)), _resident((1, GMLP_W)),
                                              _resident((N_GROUPS, CHUNK, CHUNK)), _resident((CHUNK, N_GROUPS))],
        out_specs=[_HBM, pl.BlockSpec((8, GMLP_W), lambda i: (0, 0)),
                   pl.BlockSpec((8, GMLP_W), lambda i: (0, 0)),
                   pl.BlockSpec((N_GROUPS, CHUNK, CHUNK), lambda i: (0, 0, 0)),
                   pl.BlockSpec((CHUNK, LANES), lambda i: (0, 0))],
        out_shape=[sds((T, IN_W), F32), sds((8, GMLP_W), F32), sds((8, GMLP_W), F32),
                   sds((N_GROUPS, CHUNK, CHUNK), F32), sds((CHUNK, LANES), F32)],
        scratch_shapes=[pltpu.VMEM((2, tg, 2 * GMLP_W), F32), pltpu.SemaphoreType.DMA((2,))],
        compiler_params=_params(("arbitrary",)),
    )(u, z, dgm, ln_g, ln_b, sgu_w, sgu_bt)


def _attn_bwd(q, k, v, dattn, attn, lse, dproj, owner_grads=()):
    T = q.shape[0]
    nt = T // ATT_TILE
    ns = len(owner_grads)
    steps = (ATTN_W // LANES) * nt

    def body(sl_ref, q_hbm, k_hbm, v_hbm, do_hbm, o_hbm, lse_hbm, _, *rest):
        p_refs, rest = rest[:ns], rest[ns:]
        dq_hbm = dk_hbm = dv_hbm = rest[0]
        r_refs, rest = rest[1:1 + ns], rest[1 + ns:]
        qbuf, dobuf, obuf, lbuf, kbuf, vbuf, dqbuf, dkbuf, dvbuf, delta_s = rest[:10]
        sem_q, sem_do, sem_o, sem_l, sem_k, sem_v, sem_dq, sem_dk, sem_dv = rest[10:19]
        hp, t = pl.program_id(0), pl.program_id(1)
        step = hp * nt + t
        two, three = step % 2, step % 3
        before, after = (step + 2) % 3, (step + 1) % 3
        if ns:
            start, finish = _owner_exchange_phases(p_refs, r_refs, *rest[19:])
            pl.when(step == 0)(start)

        def fetch(hp_, t_, two_, three_):
            for hbm, buf, sem, slot in ((q_hbm, qbuf, sem_q, two_), (do_hbm, dobuf, sem_do, two_),
                                        (o_hbm, obuf, sem_o, two_), (lse_hbm, lbuf, sem_l, two_),
                                        (k_hbm, kbuf, sem_k, three_), (v_hbm, vbuf, sem_v, three_)):
                for cp in _tile_copies(hbm, buf.at[slot], sem.at[slot], hp_, t_):
                    cp.start()

        @pl.when(step == 0)
        def _():
            kbuf[2] = jnp.zeros((ATT_BLOCKS, CHUNK, LANES), F32)
            vbuf[2] = jnp.zeros((ATT_BLOCKS, CHUNK, LANES), F32)
            dkbuf[3] = jnp.zeros((ATT_BLOCKS, CHUNK, LANES), F32)
            dvbuf[3] = jnp.zeros((ATT_BLOCKS, CHUNK, LANES), F32)
            fetch(0, 0, 0, 0)

        @pl.when(step + 1 < steps)
        def _():
            fetch((step + 1) // nt, (step + 1) % nt, 1 - two, after)

        for buf, sem in ((qbuf, sem_q), (dobuf, sem_do), (obuf, sem_o), (lbuf, sem_l)):
            _wait_tile(buf.at[two], sem.at[two])
        _wait_tile(kbuf.at[three], sem_k.at[three])
        _wait_tile(vbuf.at[three], sem_v.at[three])

        @pl.when(step >= 2)
        def _():
            _wait_tile(dqbuf.at[two], sem_dq.at[two])

        @pl.when(step >= 3)
        def _():
            _wait_tile(dkbuf.at[three], sem_dk.at[three])
            _wait_tile(dvbuf.at[three], sem_dv.at[three])

        q_t, do_t, l_t, k_t, v_t = qbuf.at[two], dobuf.at[two], lbuf.at[two], kbuf.at[three], vbuf.at[three]
        k_b, v_b = kbuf.at[before], vbuf.at[before]
        dq_t, dk_t, dv_t = dqbuf.at[two], dkbuf.at[three], dvbuf.at[three]
        dk_b, dv_b = dkbuf.at[before], dvbuf.at[before]
        sink = jnp.where(t > 0, before, 3)
        dk_sink, dv_sink = dkbuf.at[sink], dvbuf.at[sink]
        head0 = lax.broadcasted_iota(jnp.int32, (CHUNK, LANES), 1) < HEAD_DIM
        for r in range(ATT_BLOCKS):
            dd = dobuf[two, r] * obuf[two, r]
            d0 = jnp.sum(jnp.where(head0, dd, 0.0), axis=-1, keepdims=True)
            d1 = jnp.sum(jnp.where(head0, 0.0, dd), axis=-1, keepdims=True)
            delta_s[r] = jnp.where(head0, d0, d1)

        def column(xb):
            return jnp.concatenate([xb[:, 0:1], xb[:, HEAD_DIM:HEAD_DIM + 1]], axis=0)

        no_key_before = jnp.where(lax.broadcasted_iota(jnp.int32, (2 * CHUNK, 2 * CHUNK), 1) < CHUNK, NEG, 0.0)
        for d in DILATIONS:
            bias = _residue_bias(sl_ref, d)
            def scores(j, d=d, bias=bias):
                kcat = jnp.concatenate([_rm_block_before(k_t, k_b, d, j), _rm_block(k_t, d, j)], axis=0).astype(BF16)
                vcat = jnp.concatenate([_rm_block_before(v_t, v_b, d, j), _rm_block(v_t, d, j)], axis=0).astype(BF16)
                q2 = _stack_heads(_rm_block(q_t, d, j), head0)
                do2 = _stack_heads(_rm_block(do_t, d, j), head0)
                return (_dot(q2, kcat, NT), _dot(do2, vcat, NT), column(_rm_block(l_t, d, j)),
                        column(_rm_block(delta_s, d, j)), bias_first if _first_in_tile(d, j) else bias, kcat, q2, do2)

            bias_first = bias + jnp.where(t == 0, 1.0, 0.0) * no_key_before
            group = {}
            for j in range(ATT_BLOCKS):
                if j % SM_BLOCKS == 0:
                    group = {i: scores(i) for i in range(j, j + SM_BLOCKS)}
                    s_all, dp_all, lse_all, delta_all, bias_all = (
                        jnp.concatenate([g[i] for g in group.values()], axis=0) for i in range(5))
                    p_all = jnp.exp2(s_all + bias_all - lse_all)
                    ds_all = (p_all * (dp_all - delta_all)).astype(BF16)
                    p_all = p_all.astype(BF16)
                at = slice((j % SM_BLOCKS) * 2 * CHUNK, (j % SM_BLOCKS + 1) * 2 * CHUNK)
                ds, p16 = ds_all[at, :], p_all[at, :]
                kcat, q2, do2 = group[j][5:]
                first = d == DILATIONS[0]
                _rm_add(dq_t, _residue_rows(d, j), _unstack_heads(_dot(ds, kcat), head0), first)
                ck = _dot(ds, q2, TN)
                cv = _dot(p16, do2, TN)
                _rm_add(dk_t, _residue_rows(d, j), ck[CHUNK:, :], first)
                _rm_add(dv_t, _residue_rows(d, j), cv[CHUNK:, :], first)
                if _first_in_tile(d, j):
                    rows = [(r, CHUNK - n, n) for r, _, n in _residue_rows(d, j)]
                    _rm_add(dk_sink, rows, ck[:CHUNK, :])
                    _rm_add(dv_sink, rows, cv[:CHUNK, :])
                else:
                    rows = [(r, lo - n, n) for r, lo, n in _residue_rows(d, j)]
                    _rm_add(dk_t, rows, ck[:CHUNK, :])
                    _rm_add(dv_t, rows, cv[:CHUNK, :])

        for r in range(ATT_BLOCKS):
            dqbuf[two, r] = dqbuf[two, r] * (Q_SCALE / LOG2E)
        for cp in _tile_copies(dq_hbm, dq_t, sem_dq.at[two], hp, t, to_hbm=True):
            cp.start()

        @pl.when(t > 0)
        def _():
            for cp in (_tile_copies(dk_hbm, dk_b, sem_dk.at[before], hp, t - 1, to_hbm=True, lane0=ATTN_W)
                       + _tile_copies(dv_hbm, dv_b, sem_dv.at[before], hp, t - 1, to_hbm=True, lane0=2 * ATTN_W)):
                cp.start()

        @pl.when(t == nt - 1)
        def _():
            for cp in (_tile_copies(dk_hbm, dk_t, sem_dk.at[three], hp, t, to_hbm=True, lane0=ATTN_W)
                       + _tile_copies(dv_hbm, dv_t, sem_dv.at[three], hp, t, to_hbm=True, lane0=2 * ATTN_W)):
                cp.start()

        @pl.when(step == steps - 1)
        def _():
            for slot in range(2):
                _wait_tile(dqbuf.at[slot], sem_dq.at[slot])
            for slot in range(3):
                _wait_tile(dkbuf.at[slot], sem_dk.at[slot])
                _wait_tile(dvbuf.at[slot], sem_dv.at[slot])

        if ns:
            pl.when(step == steps - 1)(finish)

    tile = lambda n: pltpu.VMEM((n, ATT_BLOCKS, CHUNK, LANES), F32)
    dma = lambda n: pltpu.SemaphoreType.DMA((n,))
    view = jax.ShapeDtypeStruct((T // ATT_BLOCKS, ATT_BLOCKS, ATTN_W), F32)
    outs = pl.pallas_call(
        body, name="attn_bwd", grid=(ATTN_W // LANES, nt),
        in_specs=[pl.BlockSpec((8, LANES), lambda c, t: (0, c))] + [_HBM] * (7 + ns),
        out_specs=[_HBM] * (1 + ns),
        out_shape=[jax.ShapeDtypeStruct((T // ATT_BLOCKS, ATT_BLOCKS, IN_W), F32)]
        + [jax.ShapeDtypeStruct(p.shape, p.dtype) for p in owner_grads],
        scratch_shapes=[tile(2), tile(2), tile(2), tile(2), tile(3), tile(3), tile(2), tile(4), tile(4),
                        pltpu.VMEM((ATT_BLOCKS, CHUNK, LANES), F32)]
        + [dma(2), dma(2), dma(2), dma(2), dma(3), dma(3), dma(2), dma(3), dma(3)]
        + (_owner_exchange_sems(ns) if ns else []),
        input_output_aliases={7: 0},
        compiler_params=_params(("arbitrary", "arbitrary")),
    )(_slope_table(), *[_residue_view(a) for a in (q, k, v, dattn, attn, lse, dproj)], *owner_grads)
    return outs[0].reshape(T, IN_W), tuple(outs[1:])


def _proj_bwd(dproj, w_in_t, x, g1, dh1, owner_grads=()):
    T = x.shape[0]
    tm = TM_PROJ
    ns = len(owner_grads)
    steps = T // tm

    def body(d_ref, w_ref, x_ref, g_ref, r_ref, *rest):
        p_refs, rest = rest[:ns], rest[ns:]
        dx_ref, dg_ref = rest[:2]
        r_refs, sems = rest[2:2 + ns], rest[2 + ns:]
        step = pl.program_id(0)
        if ns:
            start, finish = _owner_exchange_phases(p_refs, r_refs, *sems)
            pl.when(step == 0)(start)

        @pl.when(step == 0)
        def _():
            dg_ref[...] = jnp.zeros_like(dg_ref)

        dhn = _dot(d_ref[...].astype(BF16), w_ref[...])
        n1, r1 = _rms(x_ref[...])
        _accum_rows(dg_ref, dhn * n1)
        dx_ref[...] = r_ref[...] + _rms_bwd(n1, r1, g_ref[...], dhn)
        if ns:
            pl.when(step == steps - 1)(finish)

    outs = pl.pallas_call(
        body, name="proj_bwd", grid=(steps,),
        in_specs=[_rows(tm, IN_W), _resident((IN_W, D_MODEL)), _rows(tm, D_MODEL), _resident((1, D_MODEL)),
                  _rows(tm, D_MODEL)] + [_HBM] * ns,
        out_specs=[_rows(tm, D_MODEL), pl.BlockSpec((8, D_MODEL), lambda i: (0, 0))] + [_HBM] * ns,
        out_shape=[jax.ShapeDtypeStruct((T, D_MODEL), F32), jax.ShapeDtypeStruct((8, D_MODEL), F32)]
        + [jax.ShapeDtypeStruct(p.shape, p.dtype) for p in owner_grads],
        scratch_shapes=_owner_exchange_sems(ns) if ns else [],
        compiler_params=_params(("arbitrary",)),
    )(dproj, w_in_t, x, g1, dh1, *owner_grads)
    return outs[0], outs[1], tuple(outs[2:])


def _dw(a, b, name, tile, square_a=False, out_dtype=F32, shards=()):
    T, ka = a.shape
    nb = b.shape[1]
    tka, tnb, tt = tile
    tt = min(tt, T)
    last = T // tt - 1
    ns = len(shards)
    grid = (ka // tka, nb // tnb, T // tt)
    steps = grid[0] * grid[1] * grid[2]
    own_acc = out_dtype != F32

    def body(a_ref, b_ref, *refs):
        x_refs, refs = refs[:ns], refs[ns:]
        o_ref = refs[0]
        g_refs, refs = refs[1:1 + ns], refs[1 + ns:]
        acc_ref = refs[0] if own_acc else o_ref
        s = pl.program_id(2)
        if ns:
            step = (pl.program_id(0) * grid[1] + pl.program_id(1)) * grid[2] + s
            start, forward, finish = _gather_phases(x_refs, g_refs, *refs[1 if own_acc else 0:])
            pl.when(step == 0)(start)
            pl.when(step == steps // 2)(forward)

        @pl.when(s == 0)
        def _():
            acc_ref[...] = jnp.zeros_like(acc_ref)

        a_tile = a_ref[...]
        if square_a:
            a_tile = jnp.square(a_tile.astype(F32))
        acc_ref[...] += _dot(a_tile.astype(BF16), b_ref[...], TN)
        if acc_ref is not o_ref:
            @pl.when(s == last)
            def _():
                o_ref[...] = acc_ref[...].astype(out_dtype)
        if ns:
            pl.when(step == steps - 1)(finish)

    outs = pl.pallas_call(
        body, name=name, grid=grid,
        in_specs=[pl.BlockSpec((tt, tka), lambda i, j, s: (s, i)), pl.BlockSpec((tt, tnb), lambda i, j, s: (s, j))]
        + [_HBM] * ns,
        out_specs=[pl.BlockSpec((tka, tnb), lambda i, j, s: (i, j))] + [_HBM] * ns,
        out_shape=[jax.ShapeDtypeStruct((ka, nb), out_dtype)] + [_gathered_shape(s) for s in shards],
        scratch_shapes=([pltpu.VMEM((tka, tnb), F32)] if own_acc else []) + (_gather_sems(ns) if ns else []),
        compiler_params=_params(("arbitrary",) * 3 if ns else ("parallel", "parallel", "arbitrary")),
    )(a, b, *shards)
    return outs if ns else outs[0]


def _adamw_update(w, m, v, g):
    m2 = ADAM_B1 * m + (1.0 - ADAM_B1) * g
    v2 = ADAM_B2 * v + (1.0 - ADAM_B2) * jnp.square(g)
    m_hat = m2 / (1.0 - ADAM_B1 ** ADAM_STEP)
    v_hat = v2 / (1.0 - ADAM_B2 ** ADAM_STEP)
    return -ADAM_LR * (m_hat / (jnp.sqrt(v_hat) + ADAM_EPS) + ADAM_WD * w), m2, v2


def _adamw_tiny(ws, ms, vs, parts):
    n = len(ws)
    P = parts.shape[0]

    def body(*refs):
        w_refs, m_refs, v_refs, p_ref = refs[:n], refs[n:2 * n], refs[2 * n:3 * n], refs[3 * n]
        outs = refs[3 * n + 1:]

        def total(slot, rows):
            g = p_ref[0, 8 * slot:8 * slot + rows, :]
            for i in range(1, P):
                g = g + p_ref[i, 8 * slot:8 * slot + rows, :]
            return g

        for k in range(n):
            g = total(k, ws[k].shape[0])
            outs[4 * k][...] = g
            outs[4 * k + 1][...], outs[4 * k + 2][...], outs[4 * k + 3][...] = _adamw_update(
                w_refs[k][...], m_refs[k][...], v_refs[k][...], g)
        outs[4 * n][...] = total(n, 8)

    sds = jax.ShapeDtypeStruct
    return pl.pallas_call(
        body, name="adamw_tiny",
        out_shape=[sds(w.shape, F32) for w in ws for _ in range(4)] + [sds((8, LANES), F32)],
    )(*ws, *ms, *vs, parts)


def _adamw(w, m, v, parts, name, tr, transposed=False, shards=()):
    R, C = w.shape
    P = parts.shape[0]
    ns = len(shards)
    steps = R // tr

    def body(w_ref, m_ref, v_ref, p_ref, *rest):
        x_refs, rest = rest[:ns], rest[ns:]
        g_ref, d_ref, m2_ref, v2_ref = rest[:4]
        if ns:
            start, finish = _direct_gather_phases(x_refs, rest[4:4 + ns], *rest[4 + ns:])
            pl.when(pl.program_id(0) == 0)(start)
        g = p_ref[0].astype(F32)
        for i in range(1, P):
            g = g + p_ref[i].astype(F32)
        if transposed:
            g = g.T
        g_ref[...] = g
        d_ref[...], m2_ref[...], v2_ref[...] = _adamw_update(w_ref[...], m_ref[...], v_ref[...], g)
        if ns:
            pl.when(pl.program_id(0) == steps - 1)(finish)

    spec = _rows(tr, C)
    part_spec = (pl.BlockSpec((P, C, tr), lambda i: (0, 0, i)) if transposed
                 else pl.BlockSpec((P, tr, C), lambda i: (0, i, 0)))
    return pl.pallas_call(
        body, name=name, grid=(steps,),
        in_specs=[spec, spec, spec, part_spec] + [_HBM] * ns,
        out_specs=[spec] * 4 + [_HBM] * ns,
        out_shape=[jax.ShapeDtypeStruct((R, C), F32)] * 4 + [_gathered_shape(s) for s in shards],
        scratch_shapes=_gather_sems(ns) if ns else [],
        compiler_params=_params(("arbitrary",) if ns else ("parallel",)),
    )(w, m, v, parts, *shards)


_HBM = pl.BlockSpec(memory_space=pltpu.HBM)


def _place():
    return lax.axis_index("x"), lax.axis_index("y"), lax.axis_index("c")


def _gathered_shape(shard):
    return jax.ShapeDtypeStruct((N_DEV,) + shard.shape, shard.dtype)


def _gather_sems(n):
    return [pltpu.SemaphoreType.DMA((7, n)), pltpu.SemaphoreType.DMA((7, n)), pltpu.SemaphoreType.DMA((n,))]


def _gather_phases(x_refs, out_refs, send_sems, recv_sems, local_sems):
    x, y, c = _place()
    me, sibling = (x, y, c), (x, y, 1 - c)
    chips = [(1 - x, y), (x, 1 - y), (1 - x, 1 - y)]
    arrays = range(len(x_refs))

    def slot(i, px, py, pc):
        return out_refs[i].at[4 * px + 2 * py + pc]

    def copy(i, k, block, to, own=False):
        return pltpu.make_async_remote_copy(
            src_ref=x_refs[i] if own else slot(i, *block), dst_ref=slot(i, *block),
            send_sem=send_sems.at[k, i], recv_sem=recv_sems.at[k, i], device_id=to, device_id_type=MESH)

    def mine(i):
        return pltpu.make_async_copy(x_refs[i], slot(i, *me), local_sems.at[i])

    def start():
        for i in arrays:
            mine(i).start()
            copy(i, 0, me, sibling, own=True).start()
            for j, chip in enumerate(chips):
                copy(i, 1 + j, me, (*chip, c), own=True).start()

    def forward():
        for i in arrays:
            for j, chip in enumerate(chips):
                copy(i, 1 + j, (*chip, c), me).wait_recv()
                copy(i, 4 + j, (*chip, c), sibling).start()

    def finish():
        for i in arrays:
            copy(i, 0, sibling, me).wait_recv()
            copy(i, 0, me, sibling, own=True).wait_send()
            for j, chip in enumerate(chips):
                copy(i, 4 + j, (*chip, 1 - c), me).wait_recv()
                copy(i, 1 + j, me, (*chip, c), own=True).wait_send()
                copy(i, 4 + j, (*chip, c), sibling).wait_send()
            mine(i).wait()

    return start, forward, finish


def _direct_gather_phases(x_refs, out_refs, send_sems, recv_sems, local_sems):
    x, y, c = _place()
    me = 4 * x + 2 * y + c
    flip = lambda v, bit: 1 - v if bit else v
    peers = [(flip(x, k & 4), flip(y, k & 2), flip(c, k & 1)) for k in range(1, N_DEV)]
    arrays = range(len(x_refs))

    def mine(i):
        return pltpu.make_async_copy(x_refs[i], out_refs[i].at[me], local_sems.at[i])

    def copy(i, k, slot):
        return pltpu.make_async_remote_copy(
            src_ref=x_refs[i], dst_ref=out_refs[i].at[slot],
            send_sem=send_sems.at[k, i], recv_sem=recv_sems.at[k, i], device_id=peers[k], device_id_type=MESH)

    def start():
        for i in arrays:
            mine(i).start()
            for k in range(N_DEV - 1):
                copy(i, k, me).start()

    def finish():
        for i in arrays:
            for k, (px, py, pc) in enumerate(peers):
                copy(i, k, 4 * px + 2 * py + pc).wait_recv()
                copy(i, k, me).wait_send()
            mine(i).wait()

    return start, finish


def _all_gather(shards, name):
    n = len(shards)

    def body(*refs):
        start, forward, finish = _gather_phases(refs[:n], refs[n:2 * n], *refs[2 * n:])
        start()
        forward()
        finish()

    return pl.pallas_call(
        body, name=name,
        out_shape=[_gathered_shape(s) for s in shards],
        in_specs=[_HBM] * n, out_specs=[_HBM] * n,
        scratch_shapes=_gather_sems(n),
    )(*shards)


def _owner_exchange_sems(n):
    return [pltpu.SemaphoreType.DMA((7, n)), pltpu.SemaphoreType.DMA((7, n)), pltpu.SemaphoreType.DMA((n,))]


def _owner_exchange_phases(g_refs, r_refs, send_sems, recv_sems, local_sems):
    x, y, c = _place()
    me = 4 * x + 2 * y + c
    flip = lambda v, bit: 1 - v if bit else v
    peers = [(flip(x, k & 4), flip(y, k & 2), flip(c, k & 1)) for k in range(1, N_DEV)]
    arrays = range(len(g_refs))

    def mine(i):
        return pltpu.make_async_copy(g_refs[i].at[me], r_refs[i].at[me], local_sems.at[i])

    def copy(i, k, src_slot, dst_slot):
        return pltpu.make_async_remote_copy(
            src_ref=g_refs[i].at[src_slot], dst_ref=r_refs[i].at[dst_slot],
            send_sem=send_sems.at[k, i], recv_sem=recv_sems.at[k, i], device_id=peers[k], device_id_type=MESH)

    def start():
        for i in arrays:
            mine(i).start()
            for k, (px, py, pc) in enumerate(peers):
                copy(i, k, 4 * px + 2 * py + pc, me).start()

    def finish():
        for i in arrays:
            for k, (px, py, pc) in enumerate(peers):
                copy(i, k, me, 4 * px + 2 * py + pc).wait_recv()
                copy(i, k, 4 * px + 2 * py + pc, me).wait_send()
            mine(i).wait()

    return start, finish


def _local_step(x, tgt, small, w_in_t, rest, exchange=False):
    g1, g2, gf = small["norm1_g"], small["norm2_g"], small["final_norm_g"].reshape(1, D_MODEL)
    ga, gg = small["attn_out_g"], small["gmlp_out_g"]
    ln_g = small["sgu_ln_g"].reshape(1, GMLP_W)
    ln_b = small["sgu_ln_b"].reshape(1, GMLP_W)
    sgu_w = small["sgu_w"][0]
    sgu_bt = small["sgu_b"][0].T

    hn1, q, k, v, u, z = _proj_fwd(x, g1, w_in_t)
    attn, lse, gathered = _attn_fwd(q, k, v, shards=rest if exchange else ())
    w_out, w_ff1_t, w_ff2 = [g.reshape(-1, D_MODEL) for g in gathered] if exchange else rest
    gm = _gmlp_fwd(u, z, ln_g, ln_b, sgu_w, sgu_bt)
    mixed, h1, hn2 = _out_fwd(attn, gm, ga, gg, w_out, x, g2)
    relu, dh2f, dh2b, loss8, dgf8 = _ffn_fwd(hn2, h1, w_ff1_t, w_ff2, gf, tgt)

    da, dh1f, dh1b, dg2 = _ffn_bwd(dh2b, dh2f, relu, h1, g2, w_ff2, w_ff1_t)
    wire = BF16 if exchange else F32
    dw_ff2 = _dw(relu, dh2b, "dw_ff2", DW_TILE, square_a=True, out_dtype=wire)
    dw_ff1_t = _dw(da, hn2, "dw_ff1", DW_TILE, out_dtype=wire)
    dattn, dgm, dga, dgg = _out_bwd(dh1b, w_out, attn, gm, ga, gg)
    dw_out = _dw(mixed, dh1b, "dw_out", DW_TILE, out_dtype=wire)
    early = [dw_out, dw_ff1_t, dw_ff2]
    if exchange:
        early = [g.reshape(N_DEV, -1, D_MODEL) for g in early]
    dproj, dlg, dlb, dsw, dsb = _gmlp_bwd(u, z, dgm, ln_g, ln_b, sgu_w, sgu_bt)
    dproj, arrived = _attn_bwd(q, k, v, dattn, attn, lse, dproj, owner_grads=early if exchange else ())
    sgu_parts = None
    if exchange:
        dw_in_t, sgu_parts = _dw(dproj, hn1, "dw_in", DW_TILE_IN, out_dtype=wire,
                                 shards=[_as_rows(dsw).astype(BF16)])
    else:
        dw_in_t = _dw(dproj, hn1, "dw_in", DW_TILE_IN, out_dtype=wire)
    late =(dw_in_t.reshape(N_DEV, -1, D_MODEL),) if exchange else ()
    dx, dg1, late = _proj_bwd(dproj, w_in_t, x, g1, dh1f, owner_grads=late)
    if exchange:
        dw_in_t, early = late[0], arrived

    small_grads = dict(
        norm1_g=dg1[0], sgu_ln_g=dlg[0], sgu_ln_b=dlb[0], sgu_w=dsw, sgu_b=dsb[:, :N_GROUPS].T,
        attn_out_g=dga[0], gmlp_out_g=dgg[0], norm2_g=dg2[0], final_norm_g=dgf8[0])
    if exchange:
        small_grads["sgu_w_parts"] = sgu_parts
    return loss8[0, 0], dx, (dw_in_t, *early), small_grads


SMALL_NAMES = ("norm1_g", "sgu_ln_g", "sgu_ln_b", "sgu_w", "sgu_b", "attn_out_g", "gmlp_out_g", "norm2_g",
               "final_norm_g")
WEIGHT_ORDER = ("norm1_g", "w_in", "sgu_ln_g", "sgu_ln_b", "sgu_w", "sgu_b", "attn_out_g", "gmlp_out_g", "w_out",
                "norm2_g", "w_ff1", "w_ff2", "final_norm_g")


TINY_NAMES = tuple(n for n in SMALL_NAMES if n != "sgu_w")


def _as_rows(a):
    return a.reshape(-1, LANES)


def _pack_tiny_grads(d, loss):
    slots = [jnp.pad(_as_rows(d[n]), ((0, 8 - d[n].size // LANES), (0, 0))) for n in TINY_NAMES]
    return jnp.concatenate(slots + [jnp.full((8, LANES), loss, F32)], axis=0)


def kernel(x, norm1_g, w_in, sgu_ln_g, sgu_ln_b, sgu_w, sgu_b, attn_out_g, gmlp_out_g, w_out, norm2_g, w_ff1, w_ff2, final_norm_g, loss_target, m_norm1_g, m_w_in, m_sgu_ln_g, m_sgu_ln_b, m_sgu_w, m_sgu_b, m_attn_out_g, m_gmlp_out_g, m_w_out, m_norm2_g, m_w_ff1, m_w_ff2, m_final_norm_g, v_norm1_g, v_w_in, v_sgu_ln_g, v_sgu_ln_b, v_sgu_w, v_sgu_b, v_attn_out_g, v_gmlp_out_g, v_w_out, v_norm2_g, v_w_ff1, v_w_ff2, v_final_norm_g):
    w = dict(norm1_g=norm1_g, w_in=w_in, sgu_ln_g=sgu_ln_g, sgu_ln_b=sgu_ln_b, sgu_w=sgu_w, sgu_b=sgu_b,
             attn_out_g=attn_out_g, gmlp_out_g=gmlp_out_g, w_out=w_out, norm2_g=norm2_g, w_ff1=w_ff1, w_ff2=w_ff2,
             final_norm_g=final_norm_g)
    m = dict(norm1_g=m_norm1_g, w_in=m_w_in, sgu_ln_g=m_sgu_ln_g, sgu_ln_b=m_sgu_ln_b, sgu_w=m_sgu_w, sgu_b=m_sgu_b,
             attn_out_g=m_attn_out_g, gmlp_out_g=m_gmlp_out_g, w_out=m_w_out, norm2_g=m_norm2_g, w_ff1=m_w_ff1,
             w_ff2=m_w_ff2, final_norm_g=m_final_norm_g)
    v = dict(norm1_g=v_norm1_g, w_in=v_w_in, sgu_ln_g=v_sgu_ln_g, sgu_ln_b=v_sgu_ln_b, sgu_w=v_sgu_w, sgu_b=v_sgu_b,
             attn_out_g=v_attn_out_g, gmlp_out_g=v_gmlp_out_g, w_out=v_w_out, norm2_g=v_norm2_g, w_ff1=v_w_ff1,
             w_ff2=v_w_ff2, final_norm_g=v_final_norm_g)
    big = ("w_in", "w_out", "w_ff1", "w_ff2")

    w_in_t, = _all_gather([w_in[0].T.astype(BF16)], "w_in_all_gather")
    rest = (w_out[0].astype(BF16), w_ff1[0].T.astype(BF16), w_ff2[0].astype(BF16))
    loss, dx, parts, small_grads = _local_step(x[0], loss_target[0], {n: w[n] for n in SMALL_NAMES},
                                               w_in_t.reshape(IN_W, D_MODEL), rest, exchange=True)

    small_parts = [_pack_tiny_grads(small_grads, loss)]
    sgu_parts = small_grads["sgu_w_parts"]
    new = {}
    for n, p, transposed, tr in zip(big, parts, (True, False, True, False), (128, 128, 128, 256)):
        res = _adamw(w[n][0], m[n][0], v[n][0], p, "adamw_" + n, tr, transposed,
                     shards=small_parts if n == "w_in" else ())
        new[n] = [a[None] for a in res[:4]]
        if n == "w_in":
            tiny_parts, = res[4:]
    tiny = _adamw_tiny(*[[_as_rows(src[n]) for n in TINY_NAMES] for src in (w, m, v)], tiny_parts)
    sgu = _adamw(_as_rows(sgu_w), _as_rows(m_sgu_w), _as_rows(v_sgu_w), sgu_parts, "adamw_sgu_w", 512)
    loss = tiny[-1][0, 0]

    outs = []
    for i in range(4):
        d = {n: new[n][i] for n in big}
        d.update({n: tiny[4 * k + i].reshape(w[n].shape) for k, n in enumerate(TINY_NAMES)})
        d["sgu_w"] = sgu[i].reshape(sgu_w.shape)
        outs.extend(d[n] for n in WEIGHT_ORDER)
    return (loss, dx[None], *outs)
```

```python
import math

import numpy as np
import jax
import jax.numpy as jnp
from jax import lax
from jax.experimental import pallas as pl
from jax.experimental.pallas import tpu as pltpu

F32 = jnp.float32
BF16 = jnp.bfloat16

D_MODEL = 1024
HEAD_DIM = 64
N_HEADS = 12
ATTN_W = N_HEADS * HEAD_DIM
N_GROUPS = 4
GMLP_W = N_GROUPS * HEAD_DIM
IN_W = 3 * ATTN_W + 2 * GMLP_W
D_FF = 4 * D_MODEL
CHUNK = 128
DILATIONS = (1, 4, 16)
EPS = 1e-6
Q_SCALE = HEAD_DIM ** -0.5
LOG2E = 1.4426950408889634
NEG = -1e30

ADAM_LR, ADAM_B1, ADAM_B2, ADAM_EPS, ADAM_WD, ADAM_STEP = 0.001, 0.9, 0.999, 1e-08, 0.01, 10

N_DEV = 8
LANES = 128
VMEM_LIMIT = 56 << 20

TM_PROJ = 512
TM_FFN = 512
FF_CHUNK = 512
TM_GMLP = 1024
DW_TILE = (512, 1024, 8192)
DW_TILE_IN = (IN_W // 2, 1024, 2048)

MESH = pl.DeviceIdType.MESH


def _alibi_slopes(n):
    def pow2(m):
        start = 2.0 ** (-8.0 / m)
        return [start ** (i + 1) for i in range(m)]
    c = 2 ** int(math.floor(math.log2(n)))
    s = pow2(n) if c == n else pow2(c) + pow2(2 * c)[0::2][: n - c]
    return np.asarray(s, dtype=np.float32)


SLOPES = _alibi_slopes(N_HEADS)


def _params(sem=None):
    kw = dict(vmem_limit_bytes=VMEM_LIMIT)
    if sem is not None:
        kw["dimension_semantics"] = sem
    return pltpu.CompilerParams(**kw)


def _rows(tm, n):
    return pl.BlockSpec((tm, n), lambda i: (i, 0))


def _resident(shape):
    return pl.BlockSpec(shape, lambda *_: (0,) * len(shape), pipeline_mode=pl.Buffered(1))


def _rms(x):
    r = lax.rsqrt(jnp.mean(x * x, axis=-1, keepdims=True) + EPS)
    return x * r, r


def _rms_bwd(n, r, g, dy):
    dn = dy * g
    return r * (dn - n * jnp.mean(dn * n, axis=-1, keepdims=True))


def _accum_rows(acc_ref, v):
    acc_ref[...] += jnp.broadcast_to(jnp.sum(v, axis=0, keepdims=True), acc_ref.shape)


_G0 = math.sqrt(2.0 / math.pi)
_G1 = 0.044715


def _gelu(x):
    t = jnp.tanh(_G0 * (x + _G1 * (x * x * x)))
    return x * (0.5 * (1.0 + t)), t


def _gelu_grad(x, t):
    return 0.5 * (1.0 + t) + 0.5 * x * (1.0 - t * t) * (_G0 * (1.0 + 3.0 * _G1 * x * x))


NT = (((1,), (1,)), ((), ()))
TN = (((0,), (0,)), ((), ()))


def _dot(a, b, dims=None):
    if dims is None:
        return jnp.dot(a, b, preferred_element_type=F32)
    return lax.dot_general(a, b, dims, preferred_element_type=F32)


def _proj_fwd(x, g1, w_in_t):
    T = x.shape[0]
    tm = TM_PROJ

    def body(x_ref, g_ref, w_ref, hn_ref, q_ref, k_ref, v_ref, u_ref, z_ref):
        n, _ = _rms(x_ref[...])
        hn = (n * g_ref[...]).astype(BF16)
        hn_ref[...] = hn
        a = ATTN_W
        q_ref[...] = _dot(hn, w_ref[0:a, :], NT) * Q_SCALE
        k_ref[...] = _dot(hn, w_ref[a:2 * a, :], NT) * LOG2E
        v_ref[...] = _dot(hn, w_ref[2 * a:3 * a, :], NT)
        u_ref[...] = _dot(hn, w_ref[3 * a:3 * a + GMLP_W, :], NT)
        z_ref[...] = _dot(hn, w_ref[3 * a + GMLP_W:, :], NT)

    sds = jax.ShapeDtypeStruct
    return pl.pallas_call(
        body, name="proj_fwd", grid=(T // tm,),
        in_specs=[_rows(tm, D_MODEL), _resident((1, D_MODEL)), _resident((IN_W, D_MODEL))],
        out_specs=[_rows(tm, D_MODEL), _rows(tm, ATTN_W), _rows(tm, ATTN_W), _rows(tm, ATTN_W),
                   _rows(tm, GMLP_W), _rows(tm, GMLP_W)],
        out_shape=[sds((T, D_MODEL), BF16), sds((T, ATTN_W), F32), sds((T, ATTN_W), F32),
                   sds((T, ATTN_W), F32), sds((T, GMLP_W), F32), sds((T, GMLP_W), F32)],
        compiler_params=_params(("parallel",)),
    )(x, g1, w_in_t)


ATT_TILE = 2048
ATT_BLOCKS = ATT_TILE // CHUNK
SM_BLOCKS = 4


def _slope_table():
    row = np.repeat(SLOPES, HEAD_DIM)
    return jnp.asarray(np.broadcast_to(row[None], (8, ATTN_W)), F32)


def _residue_view(a):
    return a.reshape(a.shape[0] // ATT_BLOCKS, ATT_BLOCKS, a.shape[1])


def _tile_copies(hbm, buf, sem, hp, t, to_hbm=False, lane0=0):
    rows = pl.ds(pl.multiple_of(t * CHUNK, CHUNK), CHUNK)
    lanes = pl.ds(pl.multiple_of(lane0 + hp * LANES, LANES), LANES)
    pairs = [(hbm.at[rows, r, lanes], buf.at[r]) for r in range(ATT_BLOCKS)]
    return [pltpu.make_async_copy(v, h, sem) if to_hbm else pltpu.make_async_copy(h, v, sem) for h, v in pairs]


def _wait_tile(buf, sem):
    pltpu.make_async_copy(buf, buf, sem).wait()


def _residue_rows(d, j):
    if d == 16:
        return [(j, 0, CHUNK)]
    if d == 4:
        return [(j % 4 + 4 * m, 32 * (j // 4), 32) for m in range(4)]
    return [(r, 8 * j, 8) for r in range(ATT_BLOCKS)]


def _block_order(p, d):
    if d == 16:
        return p
    if d == 4:
        return 4 * (p & 31) + (p >> 5)
    return 16 * (p & 7) + (p >> 3)


def _first_in_tile(d, j):
    return _residue_rows(d, j)[0][1] == 0


def _rm_block(buf, d, j):
    return jnp.concatenate([buf[r, lo:lo + n, :] for r, lo, n in _residue_rows(d, j)], axis=0)


def _rm_block_before(buf, buf_before, d, j):
    if _first_in_tile(d, j):
        return jnp.concatenate([buf_before[r, CHUNK - n:CHUNK, :] for r, _, n in _residue_rows(d, j)], axis=0)
    return jnp.concatenate([buf[r, lo - n:lo, :] for r, lo, n in _residue_rows(d, j)], axis=0)


def _rm_store(buf, d, j, val):
    at = 0
    for r, lo, n in _residue_rows(d, j):
        buf[r, lo:lo + n, :] = val[at:at + n, :]
        at += n


def _rm_add(buf, rows, val, first=False):
    at = 0
    for r, lo, n in rows:
        if first:
            buf[r, lo:lo + n, :] = val[at:at + n, :]
        else:
            buf[r, lo:lo + n, :] += val[at:at + n, :]
        at += n


def _residue_bias(sl_ref, d):
    shape = (2 * CHUNK, 2 * CHUNK)
    row = lax.broadcasted_iota(jnp.int32, shape, 0)
    col = lax.broadcasted_iota(jnp.int32, shape, 1)
    steps = _block_order(row & (CHUNK - 1), d) + CHUNK - (_block_order(col & (CHUNK - 1), d) + (col & CHUNK))
    band = (steps >= 0) & (steps <= CHUNK)
    sl = sl_ref[0:1, :]
    upper = lax.broadcasted_iota(jnp.int32, (2 * CHUNK, 1), 0) < CHUNK
    slope2 = jnp.where(upper, sl[:, 0:1], sl[:, HEAD_DIM:HEAD_DIM + 1])
    return jnp.where(band, -(float(d) * LOG2E * slope2 * steps.astype(F32)), NEG)


def _stack_heads(xb, head0):
    zero = jnp.zeros_like(xb)
    return jnp.concatenate([jnp.where(head0, xb, zero), jnp.where(head0, zero, xb)], axis=0).astype(BF16)


def _unstack_heads(x2, head0):
    return jnp.where(head0, x2[:CHUNK, :], x2[CHUNK:, :])


def _attn_fwd(q, k, v, shards=()):
    T = q.shape[0]
    nt = T // ATT_TILE
    ns = len(shards)
    steps = (ATTN_W // LANES) * nt

    def body(sl_ref, q_hbm, k_hbm, v_hbm, *rest):
        x_refs, rest = rest[:ns], rest[ns:]
        attn_hbm, lse_hbm = rest[:2]
        g_refs, rest = rest[2:2 + ns], rest[2 + ns:]
        qbuf, kbuf, vbuf, obuf, lbuf = rest[:5]
        o_acc, l_acc = rest[5:8], rest[8:11]
        sem_q, sem_k, sem_v, sem_o, sem_l = rest[11:16]
        hp, t = pl.program_id(0), pl.program_id(1)
        step = hp * nt + t
        two, three = step % 2, step % 3
        before, after = (step + 2) % 3, (step + 1) % 3
        if ns:
            start, forward, finish = _gather_phases(x_refs, g_refs, *rest[16:])
            pl.when(step == 0)(start)
            pl.when(step == (3 * steps) // 4)(forward)

        def fetch(hp_, t_, two_, three_):
            for cp in (_tile_copies(q_hbm, qbuf.at[two_], sem_q.at[two_], hp_, t_)
                       + _tile_copies(k_hbm, kbuf.at[three_], sem_k.at[three_], hp_, t_)
                       + _tile_copies(v_hbm, vbuf.at[three_], sem_v.at[three_], hp_, t_)):
                cp.start()

        @pl.when(step == 0)
        def _():
            kbuf[2] = jnp.zeros((ATT_BLOCKS, CHUNK, LANES), F32)
            vbuf[2] = jnp.zeros((ATT_BLOCKS, CHUNK, LANES), F32)
            fetch(0, 0, 0, 0)

        @pl.when(step + 1 < steps)
        def _():
            fetch((step + 1) // nt, (step + 1) % nt, 1 - two, after)

        _wait_tile(qbuf.at[two], sem_q.at[two])
        _wait_tile(kbuf.at[three], sem_k.at[three])
        _wait_tile(vbuf.at[three], sem_v.at[three])

        @pl.when(step >= 2)
        def _():
            _wait_tile(obuf.at[two], sem_o.at[two])
            _wait_tile(lbuf.at[two], sem_l.at[two])

        q_t, k_t, v_t = qbuf.at[two], kbuf.at[three], vbuf.at[three]
        k_b, v_b = kbuf.at[before], vbuf.at[before]
        head0 = lax.broadcasted_iota(jnp.int32, (CHUNK, LANES), 1) < HEAD_DIM
        no_key_before = jnp.where(lax.broadcasted_iota(jnp.int32, (2 * CHUNK, 2 * CHUNK), 1) < CHUNK, NEG, 0.0)
        for pi, d in enumerate(DILATIONS):
            bias = _residue_bias(sl_ref, d)

            def scores(j, d=d, bias=bias):
                kcat = jnp.concatenate([_rm_block_before(k_t, k_b, d, j), _rm_block(k_t, d, j)], axis=0).astype(BF16)
                vcat = jnp.concatenate([_rm_block_before(v_t, v_b, d, j), _rm_block(v_t, d, j)], axis=0).astype(BF16)
                s = _dot(_stack_heads(_rm_block(q_t, d, j), head0), kcat, NT)
                return s, vcat, bias_first if _first_in_tile(d, j) else bias

            bias_first = bias + jnp.where(t == 0, 1.0, 0.0) * no_key_before
            for j0 in range(0, ATT_BLOCKS, SM_BLOCKS):
                group = [scores(j) for j in range(j0, j0 + SM_BLOCKS)]
                s = jnp.concatenate([g[0] for g in group], axis=0) + jnp.concatenate([g[2] for g in group], axis=0)
                m = jnp.max(s, axis=-1, keepdims=True)
                p = jnp.exp2(s - m)
                l = jnp.sum(p, axis=-1, keepdims=True)
                p = p.astype(BF16)
                block = lambda a, i: a[i * 2 * CHUNK:(i + 1) * 2 * CHUNK, :]
                o = jnp.concatenate([_dot(block(p, i), g[1]) for i, g in enumerate(group)], axis=0) * (1.0 / l)
                lse = jnp.broadcast_to(m + jnp.log2(l), o.shape)
                for i in range(SM_BLOCKS):
                    _rm_store(o_acc[pi], d, j0 + i, _unstack_heads(block(o, i), head0))
                    _rm_store(l_acc[pi], d, j0 + i, _unstack_heads(block(lse, i), head0))

        for r in range(ATT_BLOCKS):
            a, b, c = l_acc[0][r], l_acc[1][r], l_acc[2][r]
            m = jnp.maximum(jnp.maximum(a, b), c)
            ea, eb, ec = jnp.exp2(a - m), jnp.exp2(b - m), jnp.exp2(c - m)
            tot = ea + eb + ec
            obuf[two, r] = (ea * o_acc[0][r] + eb * o_acc[1][r] + ec * o_acc[2][r]) / tot
            lbuf[two, r] = m + jnp.log2(tot)

        for cp in (_tile_copies(attn_hbm, obuf.at[two], sem_o.at[two], hp, t, to_hbm=True)
                   + _tile_copies(lse_hbm, lbuf.at[two], sem_l.at[two], hp, t, to_hbm=True)):
            cp.start()

        @pl.when(step == steps - 1)
        def _():
            for slot in (two, 1 - two)[:min(steps, 2)]:
                _wait_tile(obuf.at[slot], sem_o.at[slot])
                _wait_tile(lbuf.at[slot], sem_l.at[slot])

        if ns:
            pl.when(step == steps - 1)(finish)

    tile = lambda n: pltpu.VMEM((n, ATT_BLOCKS, CHUNK, LANES), F32)
    dma = lambda n: pltpu.SemaphoreType.DMA((n,))
    view = jax.ShapeDtypeStruct((T // ATT_BLOCKS, ATT_BLOCKS, ATTN_W), F32)
    outs = pl.pallas_call(
        body, name="attn_fwd", grid=(ATTN_W // LANES, nt),
        in_specs=[pl.BlockSpec((8, LANES), lambda c, t: (0, c))] + [_HBM] * (3 + ns),
        out_specs=[_HBM] * (2 + ns),
        out_shape=[view, view] + [_gathered_shape(s) for s in shards],
        scratch_shapes=[tile(2), tile(3), tile(3), tile(2), tile(2)] + [pltpu.VMEM((ATT_BLOCKS, CHUNK, LANES), F32)] * 6
        + [dma(2), dma(3), dma(3), dma(2), dma(2)] + (_gather_sems(ns) if ns else []),
        compiler_params=_params(("arbitrary", "arbitrary")),
    )(_slope_table(), _residue_view(q), _residue_view(k), _residue_view(v), *shards)
    return outs[0].reshape(T, ATTN_W), outs[1].reshape(T, ATTN_W), tuple(outs[2:])


def _group_mean(v, grp):
    halves = []
    for h in range(GMLP_W // LANES):
        x = v[:, h * LANES:(h + 1) * LANES]
        low = grp[:, h * LANES:(h + 1) * LANES] == 2 * h
        a = jnp.sum(jnp.where(low, x, 0.0), axis=-1, keepdims=True)
        b = jnp.sum(jnp.where(low, 0.0, x), axis=-1, keepdims=True)
        halves.append(jnp.where(low, a, b) * (1.0 / HEAD_DIM))
    return jnp.concatenate(halves, axis=1)


def _gmlp_core(uu, zz, lg, lb, ws, sb_ref, grp):
    ug, tu = _gelu(uu)
    zg, tz = _gelu(zz)
    zc = zg - _group_mean(zg, grp)
    rstd = lax.rsqrt(_group_mean(zc * zc, grp) + EPS)
    xhat = zc * rstd
    zn16 = (xhat * lg + lb).astype(BF16)
    low = grp[:CHUNK, :LANES] == 0
    mixed = []
    for ci in range(uu.shape[0] // CHUNK):
        rows = slice(ci * CHUNK, (ci + 1) * CHUNK)
        halves = []
        for h in range(GMLP_W // LANES):
            zh = zn16[rows, h * LANES:(h + 1) * LANES]
            halves.append(jnp.where(low, _dot(ws[2 * h], zh) + sb_ref[:, 2 * h:2 * h + 1],
                                    _dot(ws[2 * h + 1], zh) + sb_ref[:, 2 * h + 1:2 * h + 2]))
        mixed.append(jnp.concatenate(halves, axis=1))
    return ug, tu, tz, xhat, rstd, zn16, jnp.concatenate(mixed, axis=0)


def _causal_ws(w_ref):
    ti = lax.broadcasted_iota(jnp.int32, (CHUNK, CHUNK), 0)
    si = lax.broadcasted_iota(jnp.int32, (CHUNK, CHUNK), 1)
    causal = si <= ti
    return causal, [jnp.where(causal, w_ref[g], 0.0).astype(BF16) for g in range(N_GROUPS)]


def _gmlp_fwd(u, z, ln_g, ln_b, sgu_w, sgu_bt):
    T = u.shape[0]
    tg = TM_GMLP

    def body(u_ref, z_ref, g_ref, b_ref, w_ref, sb_ref, out_ref):
        grp = lax.broadcasted_iota(jnp.int32, (tg, GMLP_W), 1) // HEAD_DIM
        _, ws = _causal_ws(w_ref)
        ug, _, _, _, _, _, mixed = _gmlp_core(u_ref[...], z_ref[...], g_ref[...], b_ref[...], ws, sb_ref, grp)
        out_ref[...] = ug * mixed

    return pl.pallas_call(
        body, name="gmlp_fwd", grid=(T // tg,),
        in_specs=[_rows(tg, GMLP_W), _rows(tg, GMLP_W), _resident((1, GMLP_W)), _resident((1, GMLP_W)),
                  _resident((N_GROUPS, CHUNK, CHUNK)), _resident((CHUNK, N_GROUPS))],
        out_specs=_rows(tg, GMLP_W),
        out_shape=jax.ShapeDtypeStruct((T, GMLP_W), F32),
        compiler_params=_params(("parallel",)),
    )(u, z, ln_g, ln_b, sgu_w, sgu_bt)


def _out_fwd(attn, gm, ga, gg, w_out, x, g2):
    T = x.shape[0]
    tm = TM_PROJ

    def body(a_ref, m_ref, ga_ref, gg_ref, w_ref, x_ref, g2_ref, mix_ref, h1_ref, hn2_ref):
        an, _ = _rms(a_ref[...])
        gn, _ = _rms(m_ref[...])
        an = (an * ga_ref[...]).astype(BF16)
        gn = (gn * gg_ref[...]).astype(BF16)
        mix_ref[:, 0:ATTN_W] = an
        mix_ref[:, ATTN_W:] = gn
        h1 = x_ref[...] + _dot(an, w_ref[0:ATTN_W, :]) + _dot(gn, w_ref[ATTN_W:, :])
        h1_ref[...] = h1
        n2, _ = _rms(h1)
        hn2_ref[...] = (n2 * g2_ref[...]).astype(BF16)

    sds = jax.ShapeDtypeStruct
    return pl.pallas_call(
        body, name="out_fwd", grid=(T // tm,),
        in_specs=[_rows(tm, ATTN_W), _rows(tm, GMLP_W), _resident((1, ATTN_W)), _resident((1, GMLP_W)),
                  _resident((D_MODEL, D_MODEL)), _rows(tm, D_MODEL), _resident((1, D_MODEL))],
        out_specs=[_rows(tm, D_MODEL)] * 3,
        out_shape=[sds((T, D_MODEL), BF16), sds((T, D_MODEL), F32), sds((T, D_MODEL), BF16)],
        compiler_params=_params(("parallel",)),
    )(attn, gm, ga, gg, w_out, x, g2)


def _ffn_fwd(hn2, h1, w1t, w2, gf, tgt):
    T = h1.shape[0]
    tm = TM_FFN

    def body(hn_ref, h1_ref, w1_ref, w2_ref, gf_ref, t_ref, r_ref, dhf_ref, dhb_ref, loss_ref, dgf_ref):
        i = pl.program_id(0)

        @pl.when(i == 0)
        def _():
            loss_ref[...] = jnp.zeros_like(loss_ref)
            dgf_ref[...] = jnp.zeros_like(dgf_ref)

        hn = hn_ref[...]
        acc = h1_ref[...]
        for j in range(D_FF // FF_CHUNK):
            cols = slice(j * FF_CHUNK, (j + 1) * FF_CHUNK)
            r = jnp.maximum(_dot(hn, w1_ref[cols, :], NT), 0.0)
            r_ref[:, cols] = r.astype(BF16)
            act = jnp.square(r).astype(BF16)
            acc = acc + _dot(act, w2_ref[cols, :])
        n3, r3 = _rms(acc)
        gf_row = gf_ref[...]
        e = n3 * gf_row - t_ref[...]
        loss_ref[...] += 0.5 * jnp.sum(jnp.mean(e * e, axis=-1, keepdims=True))
        dy = e * (1.0 / D_MODEL)
        _accum_rows(dgf_ref, dy * n3)
        dh2 = _rms_bwd(n3, r3, gf_row, dy)
        dhf_ref[...] = dh2
        dhb_ref[...] = dh2.astype(BF16)

    sds = jax.ShapeDtypeStruct
    acc_spec = lambda n: pl.BlockSpec((8, n), lambda i: (0, 0))
    return pl.pallas_call(
        body, name="ffn_fwd", grid=(T // tm,),
        in_specs=[_rows(tm, D_MODEL), _rows(tm, D_MODEL), _resident((D_FF, D_MODEL)), _resident((D_FF, D_MODEL)),
                  _resident((1, D_MODEL)), _rows(tm, D_MODEL)],
        out_specs=[_rows(tm, D_FF), _rows(tm, D_MODEL), _rows(tm, D_MODEL), acc_spec(LANES), acc_spec(D_MODEL)],
        out_shape=[sds((T, D_FF), BF16), sds((T, D_MODEL), F32), sds((T, D_MODEL), BF16),
                   sds((8, LANES), F32), sds((8, D_MODEL), F32)],
        compiler_params=_params(("arbitrary",)),
    )(hn2, h1, w1t, w2, gf, tgt)


def _ffn_bwd(dh2b, dh2f, relu, h1, g2, w2, w1t):
    T = h1.shape[0]
    tm = TM_FFN

    def body(db_ref, df_ref, r_ref, h1_ref, g2_ref, w2_ref, w1t_ref, da_ref, d1f_ref, d1b_ref, dg_ref):
        @pl.when(pl.program_id(0) == 0)
        def _():
            dg_ref[...] = jnp.zeros_like(dg_ref)

        db = db_ref[...]
        acc = jnp.zeros((tm, D_MODEL), F32)
        for j in range(D_FF // FF_CHUNK):
            cols = slice(j * FF_CHUNK, (j + 1) * FF_CHUNK)
            da = (_dot(db, w2_ref[cols, :], NT) * (2.0 * r_ref[:, cols].astype(F32))).astype(BF16)
            da_ref[:, cols] = da
            acc = acc + _dot(da, w1t_ref[cols, :])
        n2, r2 = _rms(h1_ref[...])
        _accum_rows(dg_ref, acc * n2)
        dh1 = df_ref[...] + _rms_bwd(n2, r2, g2_ref[...], acc)
        d1f_ref[...] = dh1
        d1b_ref[...] = dh1.astype(BF16)

    sds = jax.ShapeDtypeStruct
    return pl.pallas_call(
        body, name="ffn_bwd", grid=(T // tm,),
        in_specs=[_rows(tm, D_MODEL), _rows(tm, D_MODEL), _rows(tm, D_FF), _rows(tm, D_MODEL),
                  _resident((1, D_MODEL)), _resident((D_FF, D_MODEL)), _resident((D_FF, D_MODEL))],
        out_specs=[_rows(tm, D_FF), _rows(tm, D_MODEL), _rows(tm, D_MODEL),
                   pl.BlockSpec((8, D_MODEL), lambda i: (0, 0))],
        out_shape=[sds((T, D_FF), BF16), sds((T, D_MODEL), F32), sds((T, D_MODEL), BF16), sds((8, D_MODEL), F32)],
        compiler_params=_params(("arbitrary",)),
    )(dh2b, dh2f, relu, h1, g2, w2, w1t)


def _out_bwd(dh1b, w_out, attn, gm, ga, gg):
    T = attn.shape[0]
    tm = TM_PROJ

    def body(d_ref, w_ref, a_ref, m_ref, ga_ref, gg_ref, da_ref, dm_ref, dga_ref, dgg_ref):
        @pl.when(pl.program_id(0) == 0)
        def _():
            dga_ref[...] = jnp.zeros_like(dga_ref)
            dgg_ref[...] = jnp.zeros_like(dgg_ref)

        d = d_ref[...]
        dan = _dot(d, w_ref[0:ATTN_W, :], NT)
        dgn = _dot(d, w_ref[ATTN_W:, :], NT)
        na, ra = _rms(a_ref[...])
        ng, rg = _rms(m_ref[...])
        _accum_rows(dga_ref, dan * na)
        _accum_rows(dgg_ref, dgn * ng)
        da_ref[...] = _rms_bwd(na, ra, ga_ref[...], dan)
        dm_ref[...] = _rms_bwd(ng, rg, gg_ref[...], dgn)

    sds = jax.ShapeDtypeStruct
    return pl.pallas_call(
        body, name="out_bwd", grid=(T // tm,),
        in_specs=[_rows(tm, D_MODEL), _resident((D_MODEL, D_MODEL)), _rows(tm, ATTN_W), _rows(tm, GMLP_W),
                  _resident((1, ATTN_W)), _resident((1, GMLP_W))],
        out_specs=[_rows(tm, ATTN_W), _rows(tm, GMLP_W), pl.BlockSpec((8, ATTN_W), lambda i: (0, 0)),
                   pl.BlockSpec((8, GMLP_W), lambda i: (0, 0))],
        out_shape=[sds((T, ATTN_W), F32), sds((T, GMLP_W), F32), sds((8, ATTN_W), F32), sds((8, GMLP_W), F32)],
        compiler_params=_params(("arbitrary",)),
    )(dh1b, w_out, attn, gm, ga, gg)


def _gmlp_bwd(u, z, dgm, ln_g, ln_b, sgu_w, sgu_bt):
    T = u.shape[0]
    tg = TM_GMLP
    nsteps = T // tg

    def body(u_ref, z_ref, d_ref, g_ref, b_ref, w_ref, sb_ref, dproj_hbm, dlg_ref, dlb_ref, dw_ref, dsb_ref,
             stage, sem):
        i = pl.program_id(0)
        slot = i % 2
        duz_ref = stage.at[slot]

        def to_dproj(step, buf):
            rows = pl.ds(pl.multiple_of(step * tg, tg), tg)
            return pltpu.make_async_copy(stage.at[buf], dproj_hbm.at[rows, pl.ds(3 * ATTN_W, 2 * GMLP_W)],
                                         sem.at[buf])

        @pl.when(i == 0)
        def _():
            for ref in (dlg_ref, dlb_ref, dw_ref, dsb_ref):
                ref[...] = jnp.zeros_like(ref)

        @pl.when(i >= 2)
        def _():
            to_dproj(i - 2, slot).wait()

        grp = lax.broadcasted_iota(jnp.int32, (tg, GMLP_W), 1) // HEAD_DIM
        lane = lax.broadcasted_iota(jnp.int32, (CHUNK, LANES), 1)
        causal, ws = _causal_ws(w_ref)
        lg = g_ref[...]
        uu, zz, dgm = u_ref[...], z_ref[...], d_ref[...]
        ug, tu, tz, xhat, rstd, zn16, mixed = _gmlp_core(uu, zz, lg, b_ref[...], ws, sb_ref, grp)
        dmx = dgm * ug
        duz_ref[:, 0:GMLP_W] = dgm * mixed * _gelu_grad(uu, tu)
        dmx16 = dmx.astype(BF16)
        low = grp[:CHUNK, :LANES] == 0
        dzn = []
        for ci in range(tg // CHUNK):
            rows = slice(ci * CHUNK, (ci + 1) * CHUNK)
            halves = []
            for h in range(GMLP_W // LANES):
                lanes = slice(h * LANES, (h + 1) * LANES)
                dmx_h, zn_h, zero = dmx16[rows, lanes], zn16[rows, lanes], jnp.zeros((CHUNK, LANES), BF16)
                halves.append(jnp.where(low, _dot(ws[2 * h], dmx_h, TN), _dot(ws[2 * h + 1], dmx_h, TN)))
                dw_ref[2 * h] += _dot(jnp.where(low, dmx_h, zero), zn_h, NT)
                dw_ref[2 * h + 1] += _dot(jnp.where(low, zero, dmx_h), zn_h, NT)
            dzn.append(jnp.concatenate(halves, axis=1))
        dzn = jnp.concatenate(dzn, axis=0)
        dsb = jnp.zeros((CHUNK, LANES), F32)
        for g in range(N_GROUPS):
            half = slice((g // 2) * LANES, (g // 2 + 1) * LANES)
            per_token = jnp.sum(jnp.where(grp[:, half] == g, dmx[:, half], 0.0), axis=-1, keepdims=True)
            by_position = sum(per_token[ci * CHUNK:(ci + 1) * CHUNK] for ci in range(tg // CHUNK))
            dsb = jnp.where(lane == g, by_position, dsb)
        dsb_ref[...] += dsb
        _accum_rows(dlg_ref, dzn * xhat)
        _accum_rows(dlb_ref, dzn)
        dxh = dzn * lg
        dzg = rstd * (dxh - _group_mean(dxh, grp) - xhat * _group_mean(dxh * xhat, grp))
        duz_ref[:, GMLP_W:] = dzg * _gelu_grad(zz, tz)
        to_dproj(i, slot).start()

        @pl.when(i == nsteps - 1)
        def _():
            for g in range(N_GROUPS):
                dw_ref[g] = jnp.where(causal, dw_ref[g], 0.0)
            to_dproj(i, slot).wait()
            if nsteps >= 2:
                to_dproj(i - 1, 1 - slot).wait()

    sds = jax.ShapeDtypeStruct
    return pl.pallas_call(
        body, name="gmlp_bwd", grid=(nsteps,),
        in_specs=[_rows(tg, GMLP_W)] * 3 + [_resident((1, GMLP_W)), _resident((1, GMLP_W)),
                                              _resident((N_GROUPS, CHUNK, CHUNK)), _resident((CHUNK, N_GROUPS))],
        out_specs=[_HBM, pl.BlockSpec((8, GMLP_W), lambda i: (0, 0)),
                   pl.BlockSpec((8, GMLP_W), lambda i: (0, 0)),
                   pl.BlockSpec((N_GROUPS, CHUNK, CHUNK), lambda i: (0, 0, 0)),
                   pl.BlockSpec((CHUNK, LANES), lambda i: (0, 0))],
        out_shape=[sds((T, IN_W), F32), sds((8, GMLP_W), F32), sds((8, GMLP_W), F32),
                   sds((N_GROUPS, CHUNK, CHUNK), F32), sds((CHUNK, LANES), F32)],
        scratch_shapes=[pltpu.VMEM((2, tg, 2 * GMLP_W), F32), pltpu.SemaphoreType.DMA((2,))],
        compiler_params=_params(("arbitrary",)),
    )(u, z, dgm, ln_g, ln_b, sgu_w, sgu_bt)


def _attn_bwd(q, k, v, dattn, attn, lse, dproj, owner_grads=()):
    T = q.shape[0]
    nt = T // ATT_TILE
    ns = len(owner_grads)
    steps = (ATTN_W // LANES) * nt

    def body(sl_ref, q_hbm, k_hbm, v_hbm, do_hbm, o_hbm, lse_hbm, _, *rest):
        p_refs, rest = rest[:ns], rest[ns:]
        dq_hbm = dk_hbm = dv_hbm = rest[0]
        r_refs, rest = rest[1:1 + ns], rest[1 + ns:]
        qbuf, dobuf, obuf, lbuf, kbuf, vbuf, dqbuf, dkbuf, dvbuf, delta_s = rest[:10]
        sem_q, sem_do, sem_o, sem_l, sem_k, sem_v, sem_dq, sem_dk, sem_dv = rest[10:19]
        hp, t = pl.program_id(0), pl.program_id(1)
        step = hp * nt + t
        two, three = step % 2, step % 3
        before, after = (step + 2) % 3, (step + 1) % 3
        if ns:
            start, finish = _owner_exchange_phases(p_refs, r_refs, *rest[19:])
            pl.when(step == 0)(start)

        def fetch(hp_, t_, two_, three_):
            for hbm, buf, sem, slot in ((q_hbm, qbuf, sem_q, two_), (do_hbm, dobuf, sem_do, two_),
                                        (o_hbm, obuf, sem_o, two_), (lse_hbm, lbuf, sem_l, two_),
                                        (k_hbm, kbuf, sem_k, three_), (v_hbm, vbuf, sem_v, three_)):
                for cp in _tile_copies(hbm, buf.at[slot], sem.at[slot], hp_, t_):
                    cp.start()

        @pl.when(step == 0)
        def _():
            kbuf[2] = jnp.zeros((ATT_BLOCKS, CHUNK, LANES), F32)
            vbuf[2] = jnp.zeros((ATT_BLOCKS, CHUNK, LANES), F32)
            dkbuf[3] = jnp.zeros((ATT_BLOCKS, CHUNK, LANES), F32)
            dvbuf[3] = jnp.zeros((ATT_BLOCKS, CHUNK, LANES), F32)
            fetch(0, 0, 0, 0)

        @pl.when(step + 1 < steps)
        def _():
            fetch((step + 1) // nt, (step + 1) % nt, 1 - two, after)

        for buf, sem in ((qbuf, sem_q), (dobuf, sem_do), (obuf, sem_o), (lbuf, sem_l)):
            _wait_tile(buf.at[two], sem.at[two])
        _wait_tile(kbuf.at[three], sem_k.at[three])
        _wait_tile(vbuf.at[three], sem_v.at[three])

        @pl.when(step >= 2)
        def _():
            _wait_tile(dqbuf.at[two], sem_dq.at[two])

        @pl.when(step >= 3)
        def _():
            _wait_tile(dkbuf.at[three], sem_dk.at[three])
            _wait_tile(dvbuf.at[three], sem_dv.at[three])

        q_t, do_t, l_t, k_t, v_t = qbuf.at[two], dobuf.at[two], lbuf.at[two], kbuf.at[three], vbuf.at[three]
        k_b, v_b = kbuf.at[before], vbuf.at[before]
        dq_t, dk_t, dv_t = dqbuf.at[two], dkbuf.at[three], dvbuf.at[three]
        dk_b, dv_b = dkbuf.at[before], dvbuf.at[before]
        sink = jnp.where(t > 0, before, 3)
        dk_sink, dv_sink = dkbuf.at[sink], dvbuf.at[sink]
        head0 = lax.broadcasted_iota(jnp.int32, (CHUNK, LANES), 1) < HEAD_DIM
        for r in range(ATT_BLOCKS):
            dd = dobuf[two, r] * obuf[two, r]
            d0 = jnp.sum(jnp.where(head0, dd, 0.0), axis=-1, keepdims=True)
            d1 = jnp.sum(jnp.where(head0, 0.0, dd), axis=-1, keepdims=True)
            delta_s[r] = jnp.where(head0, d0, d1)

        def column(xb):
            return jnp.concatenate([xb[:, 0:1], xb[:, HEAD_DIM:HEAD_DIM + 1]], axis=0)

        no_key_before = jnp.where(lax.broadcasted_iota(jnp.int32, (2 * CHUNK, 2 * CHUNK), 1) < CHUNK, NEG, 0.0)
        for d in DILATIONS:
            bias = _residue_bias(sl_ref, d)
            def scores(j, d=d, bias=bias):
                kcat = jnp.concatenate([_rm_block_before(k_t, k_b, d, j), _rm_block(k_t, d, j)], axis=0).astype(BF16)
                vcat = jnp.concatenate([_rm_block_before(v_t, v_b, d, j), _rm_block(v_t, d, j)], axis=0).astype(BF16)
                q2 = _stack_heads(_rm_block(q_t, d, j), head0)
                do2 = _stack_heads(_rm_block(do_t, d, j), head0)
                return (_dot(q2, kcat, NT), _dot(do2, vcat, NT), column(_rm_block(l_t, d, j)),
                        column(_rm_block(delta_s, d, j)), bias_first if _first_in_tile(d, j) else bias, kcat, q2, do2)

            bias_first = bias + jnp.where(t == 0, 1.0, 0.0) * no_key_before
            group = {}
            for j in range(ATT_BLOCKS):
                if j % SM_BLOCKS == 0:
                    group = {i: scores(i) for i in range(j, j + SM_BLOCKS)}
                    s_all, dp_all, lse_all, delta_all, bias_all = (
                        jnp.concatenate([g[i] for g in group.values()], axis=0) for i in range(5))
                    p_all = jnp.exp2(s_all + bias_all - lse_all)
                    ds_all = (p_all * (dp_all - delta_all)).astype(BF16)
                    p_all = p_all.astype(BF16)
                at = slice((j % SM_BLOCKS) * 2 * CHUNK, (j % SM_BLOCKS + 1) * 2 * CHUNK)
                ds, p16 = ds_all[at, :], p_all[at, :]
                kcat, q2, do2 = group[j][5:]
                first = d == DILATIONS[0]
                _rm_add(dq_t, _residue_rows(d, j), _unstack_heads(_dot(ds, kcat), head0), first)
                ck = _dot(ds, q2, TN)
                cv = _dot(p16, do2, TN)
                _rm_add(dk_t, _residue_rows(d, j), ck[CHUNK:, :], first)
                _rm_add(dv_t, _residue_rows(d, j), cv[CHUNK:, :], first)
                if _first_in_tile(d, j):
                    rows = [(r, CHUNK - n, n) for r, _, n in _residue_rows(d, j)]
                    _rm_add(dk_sink, rows, ck[:CHUNK, :])
                    _rm_add(dv_sink, rows, cv[:CHUNK, :])
                else:
                    rows = [(r, lo - n, n) for r, lo, n in _residue_rows(d, j)]
                    _rm_add(dk_t, rows, ck[:CHUNK, :])
                    _rm_add(dv_t, rows, cv[:CHUNK, :])

        for r in range(ATT_BLOCKS):
            dqbuf[two, r] = dqbuf[two, r] * (Q_SCALE / LOG2E)
        for cp in _tile_copies(dq_hbm, dq_t, sem_dq.at[two], hp, t, to_hbm=True):
            cp.start()

        @pl.when(t > 0)
        def _():
            for cp in (_tile_copies(dk_hbm, dk_b, sem_dk.at[before], hp, t - 1, to_hbm=True, lane0=ATTN_W)
                       + _tile_copies(dv_hbm, dv_b, sem_dv.at[before], hp, t - 1, to_hbm=True, lane0=2 * ATTN_W)):
                cp.start()

        @pl.when(t == nt - 1)
        def _():
            for cp in (_tile_copies(dk_hbm, dk_t, sem_dk.at[three], hp, t, to_hbm=True, lane0=ATTN_W)
                       + _tile_copies(dv_hbm, dv_t, sem_dv.at[three], hp, t, to_hbm=True, lane0=2 * ATTN_W)):
                cp.start()

        @pl.when(step == steps - 1)
        def _():
            for slot in range(2):
                _wait_tile(dqbuf.at[slot], sem_dq.at[slot])
            for slot in range(3):
                _wait_tile(dkbuf.at[slot], sem_dk.at[slot])
                _wait_tile(dvbuf.at[slot], sem_dv.at[slot])

        if ns:
            pl.when(step == steps - 1)(finish)

    tile = lambda n: pltpu.VMEM((n, ATT_BLOCKS, CHUNK, LANES), F32)
    dma = lambda n: pltpu.SemaphoreType.DMA((n,))
    view = jax.ShapeDtypeStruct((T // ATT_BLOCKS, ATT_BLOCKS, ATTN_W), F32)
    outs = pl.pallas_call(
        body, name="attn_bwd", grid=(ATTN_W // LANES, nt),
        in_specs=[pl.BlockSpec((8, LANES), lambda c, t: (0, c))] + [_HBM] * (7 + ns),
        out_specs=[_HBM] * (1 + ns),
        out_shape=[jax.ShapeDtypeStruct((T // ATT_BLOCKS, ATT_BLOCKS, IN_W), F32)]
        + [jax.ShapeDtypeStruct(p.shape, p.dtype) for p in owner_grads],
        scratch_shapes=[tile(2), tile(2), tile(2), tile(2), tile(3), tile(3), tile(2), tile(4), tile(4),
                        pltpu.VMEM((ATT_BLOCKS, CHUNK, LANES), F32)]
        + [dma(2), dma(2), dma(2), dma(2), dma(3), dma(3), dma(2), dma(3), dma(3)]
        + (_owner_exchange_sems(ns) if ns else []),
        input_output_aliases={7: 0},
        compiler_params=_params(("arbitrary", "arbitrary")),
    )(_slope_table(), *[_residue_view(a) for a in (q, k, v, dattn, attn, lse, dproj)], *owner_grads)
    return outs[0].reshape(T, IN_W), tuple(outs[1:])


def _proj_bwd(dproj, w_in_t, x, g1, dh1, owner_grads=()):
    T = x.shape[0]
    tm = TM_PROJ
    ns = len(owner_grads)
    steps = T // tm

    def body(d_ref, w_ref, x_ref, g_ref, r_ref, *rest):
        p_refs, rest = rest[:ns], rest[ns:]
        dx_ref, dg_ref = rest[:2]
        r_refs, sems = rest[2:2 + ns], rest[2 + ns:]
        step = pl.program_id(0)
        if ns:
            start, finish = _owner_exchange_phases(p_refs, r_refs, *sems)
            pl.when(step == 0)(start)

        @pl.when(step == 0)
        def _():
            dg_ref[...] = jnp.zeros_like(dg_ref)

        dhn = _dot(d_ref[...].astype(BF16), w_ref[...])
        n1, r1 = _rms(x_ref[...])
        _accum_rows(dg_ref, dhn * n1)
        dx_ref[...] = r_ref[...] + _rms_bwd(n1, r1, g_ref[...], dhn)
        if ns:
            pl.when(step == steps - 1)(finish)

    outs = pl.pallas_call(
        body, name="proj_bwd", grid=(steps,),
        in_specs=[_rows(tm, IN_W), _resident((IN_W, D_MODEL)), _rows(tm, D_MODEL), _resident((1, D_MODEL)),
                  _rows(tm, D_MODEL)] + [_HBM] * ns,
        out_specs=[_rows(tm, D_MODEL), pl.BlockSpec((8, D_MODEL), lambda i: (0, 0))] + [_HBM] * ns,
        out_shape=[jax.ShapeDtypeStruct((T, D_MODEL), F32), jax.ShapeDtypeStruct((8, D_MODEL), F32)]
        + [jax.ShapeDtypeStruct(p.shape, p.dtype) for p in owner_grads],
        scratch_shapes=_owner_exchange_sems(ns) if ns else [],
        compiler_params=_params(("arbitrary",)),
    )(dproj, w_in_t, x, g1, dh1, *owner_grads)
    return outs[0], outs[1], tuple(outs[2:])


def _dw(a, b, name, tile, square_a=False, out_dtype=F32, shards=()):
    T, ka = a.shape
    nb = b.shape[1]
    tka, tnb, tt = tile
    tt = min(tt, T)
    last = T // tt - 1
    ns = len(shards)
    grid = (ka // tka, nb // tnb, T // tt)
    steps = grid[0] * grid[1] * grid[2]
    own_acc = out_dtype != F32

    def body(a_ref, b_ref, *refs):
        x_refs, refs = refs[:ns], refs[ns:]
        o_ref = refs[0]
        g_refs, refs = refs[1:1 + ns], refs[1 + ns:]
        acc_ref = refs[0] if own_acc else o_ref
        s = pl.program_id(2)
        if ns:
            step = (pl.program_id(0) * grid[1] + pl.program_id(1)) * grid[2] + s
            start, forward, finish = _gather_phases(x_refs, g_refs, *refs[1 if own_acc else 0:])
            pl.when(step == 0)(start)
            pl.when(step == (3 * steps) // 4)(forward)

        @pl.when(s == 0)
        def _():
            acc_ref[...] = jnp.zeros_like(acc_ref)

        a_tile = a_ref[...]
        if square_a:
            a_tile = jnp.square(a_tile.astype(F32))
        acc_ref[...] += _dot(a_tile.astype(BF16), b_ref[...], TN)
        if acc_ref is not o_ref:
            @pl.when(s == last)
            def _():
                o_ref[...] = acc_ref[...].astype(out_dtype)
        if ns:
            pl.when(step == steps - 1)(finish)

    outs = pl.pallas_call(
        body, name=name, grid=grid,
        in_specs=[pl.BlockSpec((tt, tka), lambda i, j, s: (s, i)), pl.BlockSpec((tt, tnb), lambda i, j, s: (s, j))]
        + [_HBM] * ns,
        out_specs=[pl.BlockSpec((tka, tnb), lambda i, j, s: (i, j))] + [_HBM] * ns,
        out_shape=[jax.ShapeDtypeStruct((ka, nb), out_dtype)] + [_gathered_shape(s) for s in shards],
        scratch_shapes=([pltpu.VMEM((tka, tnb), F32)] if own_acc else []) + (_gather_sems(ns) if ns else []),
        compiler_params=_params(("arbitrary",) * 3 if ns else ("parallel", "parallel", "arbitrary")),
    )(a, b, *shards)
    return outs if ns else outs[0]


def _adamw_update(w, m, v, g):
    m2 = ADAM_B1 * m + (1.0 - ADAM_B1) * g
    v2 = ADAM_B2 * v + (1.0 - ADAM_B2) * jnp.square(g)
    m_hat = m2 / (1.0 - ADAM_B1 ** ADAM_STEP)
    v_hat = v2 / (1.0 - ADAM_B2 ** ADAM_STEP)
    return -ADAM_LR * (m_hat / (jnp.sqrt(v_hat) + ADAM_EPS) + ADAM_WD * w), m2, v2


def _adamw_tiny(ws, ms, vs, parts):
    n = len(ws)
    P = parts.shape[0]

    def body(*refs):
        w_refs, m_refs, v_refs, p_ref = refs[:n], refs[n:2 * n], refs[2 * n:3 * n], refs[3 * n]
        outs = refs[3 * n + 1:]

        def total(slot, rows):
            g = p_ref[0, 8 * slot:8 * slot + rows, :]
            for i in range(1, P):
                g = g + p_ref[i, 8 * slot:8 * slot + rows, :]
            return g

        for k in range(n):
            g = total(k, ws[k].shape[0])
            outs[4 * k][...] = g
            outs[4 * k + 1][...], outs[4 * k + 2][...], outs[4 * k + 3][...] = _adamw_update(
                w_refs[k][...], m_refs[k][...], v_refs[k][...], g)
        outs[4 * n][...] = total(n, 8)

    sds = jax.ShapeDtypeStruct
    return pl.pallas_call(
        body, name="adamw_tiny",
        out_shape=[sds(w.shape, F32) for w in ws for _ in range(4)] + [sds((8, LANES), F32)],
    )(*ws, *ms, *vs, parts)


def _adamw(w, m, v, parts, name, tr, transposed=False, shards=()):
    R, C = w.shape
    P = parts.shape[0]
    ns = len(shards)
    steps = R // tr

    def body(w_ref, m_ref, v_ref, p_ref, *rest):
        x_refs, rest = rest[:ns], rest[ns:]
        g_ref, d_ref, m2_ref, v2_ref = rest[:4]
        if ns:
            start, finish = _direct_gather_phases(x_refs, rest[4:4 + ns], *rest[4 + ns:])
            pl.when(pl.program_id(0) == 0)(start)
        g = p_ref[0].astype(F32)
        for i in range(1, P):
            g = g + p_ref[i].astype(F32)
        if transposed:
            g = g.T
        g_ref[...] = g
        d_ref[...], m2_ref[...], v2_ref[...] = _adamw_update(w_ref[...], m_ref[...], v_ref[...], g)
        if ns:
            pl.when(pl.program_id(0) == steps - 1)(finish)

    spec = _rows(tr, C)
    part_spec = (pl.BlockSpec((P, C, tr), lambda i: (0, 0, i)) if transposed
                 else pl.BlockSpec((P, tr, C), lambda i: (0, i, 0)))
    return pl.pallas_call(
        body, name=name, grid=(steps,),
        in_specs=[spec, spec, spec, part_spec] + [_HBM] * ns,
        out_specs=[spec] * 4 + [_HBM] * ns,
        out_shape=[jax.ShapeDtypeStruct((R, C), F32)] * 4 + [_gathered_shape(s) for s in shards],
        scratch_shapes=_gather_sems(ns) if ns else [],
        compiler_params=_params(("arbitrary",) if ns else ("parallel",)),
    )(w, m, v, parts, *shards)


_HBM = pl.BlockSpec(memory_space=pltpu.HBM)


def _place():
    return lax.axis_index("x"), lax.axis_index("y"), lax.axis_index("c")


def _gathered_shape(shard):
    return jax.ShapeDtypeStruct((N_DEV,) + shard.shape, shard.dtype)


def _gather_sems(n):
    return [pltpu.SemaphoreType.DMA((7, n)), pltpu.SemaphoreType.DMA((7, n)), pltpu.SemaphoreType.DMA((n,))]


def _gather_phases(x_refs, out_refs, send_sems, recv_sems, local_sems):
    x, y, c = _place()
    me, sibling = (x, y, c), (x, y, 1 - c)
    chips = [(1 - x, y), (x, 1 - y), (1 - x, 1 - y)]
    arrays = range(len(x_refs))

    def slot(i, px, py, pc):
        return out_refs[i].at[4 * px + 2 * py + pc]

    def copy(i, k, block, to, own=False):
        return pltpu.make_async_remote_copy(
            src_ref=x_refs[i] if own else slot(i, *block), dst_ref=slot(i, *block),
            send_sem=send_sems.at[k, i], recv_sem=recv_sems.at[k, i], device_id=to, device_id_type=MESH)

    def mine(i):
        return pltpu.make_async_copy(x_refs[i], slot(i, *me), local_sems.at[i])

    def start():
        for i in arrays:
            mine(i).start()
            copy(i, 0, me, sibling, own=True).start()
            for j, chip in enumerate(chips):
                copy(i, 1 + j, me, (*chip, c), own=True).start()

    def forward():
        for i in arrays:
            for j, chip in enumerate(chips):
                copy(i, 1 + j, (*chip, c), me).wait_recv()
                copy(i, 4 + j, (*chip, c), sibling).start()

    def finish():
        for i in arrays:
            copy(i, 0, sibling, me).wait_recv()
            copy(i, 0, me, sibling, own=True).wait_send()
            for j, chip in enumerate(chips):
                copy(i, 4 + j, (*chip, 1 - c), me).wait_recv()
                copy(i, 1 + j, me, (*chip, c), own=True).wait_send()
                copy(i, 4 + j, (*chip, c), sibling).wait_send()
            mine(i).wait()

    return start, forward, finish


def _direct_gather_phases(x_refs, out_refs, send_sems, recv_sems, local_sems):
    x, y, c = _place()
    me = 4 * x + 2 * y + c
    flip = lambda v, bit: 1 - v if bit else v
    peers = [(flip(x, k & 4), flip(y, k & 2), flip(c, k & 1)) for k in range(1, N_DEV)]
    arrays = range(len(x_refs))

    def mine(i):
        return pltpu.make_async_copy(x_refs[i], out_refs[i].at[me], local_sems.at[i])

    def copy(i, k, slot):
        return pltpu.make_async_remote_copy(
            src_ref=x_refs[i], dst_ref=out_refs[i].at[slot],
            send_sem=send_sems.at[k, i], recv_sem=recv_sems.at[k, i], device_id=peers[k], device_id_type=MESH)

    def start():
        for i in arrays:
            mine(i).start()
            for k in range(N_DEV - 1):
                copy(i, k, me).start()

    def finish():
        for i in arrays:
            for k, (px, py, pc) in enumerate(peers):
                copy(i, k, 4 * px + 2 * py + pc).wait_recv()
                copy(i, k, me).wait_send()
            mine(i).wait()

    return start, finish


def _all_gather(shards, name):
    n = len(shards)

    def body(*refs):
        start, forward, finish = _gather_phases(refs[:n], refs[n:2 * n], *refs[2 * n:])
        start()
        forward()
        finish()

    return pl.pallas_call(
        body, name=name,
        out_shape=[_gathered_shape(s) for s in shards],
        in_specs=[_HBM] * n, out_specs=[_HBM] * n,
        scratch_shapes=_gather_sems(n),
    )(*shards)


def _owner_exchange_sems(n):
    return [pltpu.SemaphoreType.DMA((7, n)), pltpu.SemaphoreType.DMA((7, n)), pltpu.SemaphoreType.DMA((n,))]


def _owner_exchange_phases(g_refs, r_refs, send_sems, recv_sems, local_sems):
    x, y, c = _place()
    me = 4 * x + 2 * y + c
    flip = lambda v, bit: 1 - v if bit else v
    peers = [(flip(x, k & 4), flip(y, k & 2), flip(c, k & 1)) for k in range(1, N_DEV)]
    arrays = range(len(g_refs))

    def mine(i):
        return pltpu.make_async_copy(g_refs[i].at[me], r_refs[i].at[me], local_sems.at[i])

    def copy(i, k, src_slot, dst_slot):
        return pltpu.make_async_remote_copy(
            src_ref=g_refs[i].at[src_slot], dst_ref=r_refs[i].at[dst_slot],
            send_sem=send_sems.at[k, i], recv_sem=recv_sems.at[k, i], device_id=peers[k], device_id_type=MESH)

    def start():
        for i in arrays:
            mine(i).start()
            for k, (px, py, pc) in enumerate(peers):
                copy(i, k, 4 * px + 2 * py + pc, me).start()

    def finish():
        for i in arrays:
            for k, (px, py, pc) in enumerate(peers):
                copy(i, k, me, 4 * px + 2 * py + pc).wait_recv()
                copy(i, k, 4 * px + 2 * py + pc, me).wait_send()
            mine(i).wait()

    return start, finish


def _local_step(x, tgt, small, w_in_t, rest, exchange=False):
    g1, g2, gf = small["norm1_g"], small["norm2_g"], small["final_norm_g"].reshape(1, D_MODEL)
    ga, gg = small["attn_out_g"], small["gmlp_out_g"]
    ln_g = small["sgu_ln_g"].reshape(1, GMLP_W)
    ln_b = small["sgu_ln_b"].reshape(1, GMLP_W)
    sgu_w = small["sgu_w"][0]
    sgu_bt = small["sgu_b"][0].T

    hn1, q, k, v, u, z = _proj_fwd(x, g1, w_in_t)
    attn, lse, gathered = _attn_fwd(q, k, v, shards=rest if exchange else ())
    w_out, w_ff1_t, w_ff2 = [g.reshape(-1, D_MODEL) for g in gathered] if exchange else rest
    gm = _gmlp_fwd(u, z, ln_g, ln_b, sgu_w, sgu_bt)
    mixed, h1, hn2 = _out_fwd(attn, gm, ga, gg, w_out, x, g2)
    relu, dh2f, dh2b, loss8, dgf8 = _ffn_fwd(hn2, h1, w_ff1_t, w_ff2, gf, tgt)

    da, dh1f, dh1b, dg2 = _ffn_bwd(dh2b, dh2f, relu, h1, g2, w_ff2, w_ff1_t)
    wire = BF16 if exchange else F32
    dw_ff2 = _dw(relu, dh2b, "dw_ff2", DW_TILE, square_a=True, out_dtype=wire)
    dw_ff1_t = _dw(da, hn2, "dw_ff1", DW_TILE, out_dtype=wire)
    dattn, dgm, dga, dgg = _out_bwd(dh1b, w_out, attn, gm, ga, gg)
    dw_out = _dw(mixed, dh1b, "dw_out", DW_TILE, out_dtype=wire)
    early = [dw_out, dw_ff1_t, dw_ff2]
    if exchange:
        early = [g.reshape(N_DEV, -1, D_MODEL) for g in early]
    dproj, dlg, dlb, dsw, dsb = _gmlp_bwd(u, z, dgm, ln_g, ln_b, sgu_w, sgu_bt)
    dproj, arrived = _attn_bwd(q, k, v, dattn, attn, lse, dproj, owner_grads=early if exchange else ())
    sgu_parts = None
    if exchange:
        dw_in_t, sgu_parts = _dw(dproj, hn1, "dw_in", DW_TILE_IN, out_dtype=wire,
                                 shards=[_as_rows(dsw).astype(BF16)])
    else:
        dw_in_t = _dw(dproj, hn1, "dw_in", DW_TILE_IN, out_dtype=wire)
    late =(dw_in_t.reshape(N_DEV, -1, D_MODEL),) if exchange else ()
    dx, dg1, late = _proj_bwd(dproj, w_in_t, x, g1, dh1f, owner_grads=late)
    if exchange:
        dw_in_t, early = late[0], arrived

    small_grads = dict(
        norm1_g=dg1[0], sgu_ln_g=dlg[0], sgu_ln_b=dlb[0], sgu_w=dsw, sgu_b=dsb[:, :N_GROUPS].T,
        attn_out_g=dga[0], gmlp_out_g=dgg[0], norm2_g=dg2[0], final_norm_g=dgf8[0])
    if exchange:
        small_grads["sgu_w_parts"] = sgu_parts
    return loss8[0, 0], dx, (dw_in_t, *early), small_grads


SMALL_NAMES = ("norm1_g", "sgu_ln_g", "sgu_ln_b", "sgu_w", "sgu_b", "attn_out_g", "gmlp_out_g", "norm2_g",
               "final_norm_g")
WEIGHT_ORDER = ("norm1_g", "w_in", "sgu_ln_g", "sgu_ln_b", "sgu_w", "sgu_b", "attn_out_g", "gmlp_out_g", "w_out",
                "norm2_g", "w_ff1", "w_ff2", "final_norm_g")


TINY_NAMES = tuple(n for n in SMALL_NAMES if n != "sgu_w")


def _as_rows(a):
    return a.reshape(-1, LANES)


def _pack_tiny_grads(d, loss):
    slots = [jnp.pad(_as_rows(d[n]), ((0, 8 - d[n].size // LANES), (0, 0))) for n in TINY_NAMES]
    return jnp.concatenate(slots + [jnp.full((8, LANES), loss, F32)], axis=0)


def kernel(x, norm1_g, w_in, sgu_ln_g, sgu_ln_b, sgu_w, sgu_b, attn_out_g, gmlp_out_g, w_out, norm2_g, w_ff1, w_ff2, final_norm_g, loss_target, m_norm1_g, m_w_in, m_sgu_ln_g, m_sgu_ln_b, m_sgu_w, m_sgu_b, m_attn_out_g, m_gmlp_out_g, m_w_out, m_norm2_g, m_w_ff1, m_w_ff2, m_final_norm_g, v_norm1_g, v_w_in, v_sgu_ln_g, v_sgu_ln_b, v_sgu_w, v_sgu_b, v_attn_out_g, v_gmlp_out_g, v_w_out, v_norm2_g, v_w_ff1, v_w_ff2, v_final_norm_g):
    w = dict(norm1_g=norm1_g, w_in=w_in, sgu_ln_g=sgu_ln_g, sgu_ln_b=sgu_ln_b, sgu_w=sgu_w, sgu_b=sgu_b,
             attn_out_g=attn_out_g, gmlp_out_g=gmlp_out_g, w_out=w_out, norm2_g=norm2_g, w_ff1=w_ff1, w_ff2=w_ff2,
             final_norm_g=final_norm_g)
    m = dict(norm1_g=m_norm1_g, w_in=m_w_in, sgu_ln_g=m_sgu_ln_g, sgu_ln_b=m_sgu_ln_b, sgu_w=m_sgu_w, sgu_b=m_sgu_b,
             attn_out_g=m_attn_out_g, gmlp_out_g=m_gmlp_out_g, w_out=m_w_out, norm2_g=m_norm2_g, w_ff1=m_w_ff1,
             w_ff2=m_w_ff2, final_norm_g=m_final_norm_g)
    v = dict(norm1_g=v_norm1_g, w_in=v_w_in, sgu_ln_g=v_sgu_ln_g, sgu_ln_b=v_sgu_ln_b, sgu_w=v_sgu_w, sgu_b=v_sgu_b,
             attn_out_g=v_attn_out_g, gmlp_out_g=v_gmlp_out_g, w_out=v_w_out, norm2_g=v_norm2_g, w_ff1=v_w_ff1,
             w_ff2=v_w_ff2, final_norm_g=v_final_norm_g)
    big = ("w_in", "w_out", "w_ff1", "w_ff2")

    w_in_t, = _all_gather([w_in[0].T.astype(BF16)], "w_in_all_gather")
    rest = (w_out[0].astype(BF16), w_ff1[0].T.astype(BF16), w_ff2[0].astype(BF16))
    loss, dx, parts, small_grads = _local_step(x[0], loss_target[0], {n: w[n] for n in SMALL_NAMES},
                                               w_in_t.reshape(IN_W, D_MODEL), rest, exchange=True)

    small_parts = [_pack_tiny_grads(small_grads, loss)]
    sgu_parts = small_grads["sgu_w_parts"]
    new = {}
    for n, p, transposed, tr in zip(big, parts, (True, False, True, False), (128, 128, 128, 256)):
        res = _adamw(w[n][0], m[n][0], v[n][0], p, "adamw_" + n, tr, transposed,
                     shards=small_parts if n == "w_in" else ())
        new[n] = [a[None] for a in res[:4]]
        if n == "w_in":
            tiny_parts, = res[4:]
    tiny = _adamw_tiny(*[[_as_rows(src[n]) for n in TINY_NAMES] for src in (w, m, v)], tiny_parts)
    sgu = _adamw(_as_rows(sgu_w), _as_rows(m_sgu_w), _as_rows(v_sgu_w), sgu_parts, "adamw_sgu_w", 512)
    loss = tiny[-1][0, 0]

    outs = []
    for i in range(4):
        d = {n: new[n][i] for n in big}
        d.update({n: tiny[4 * k + i].reshape(w[n].shape) for k, n in enumerate(TINY_NAMES)})
        d["sgu_w"] = sgu[i].reshape(sgu_w.shape)
        outs.extend(d[n] for n in WEIGHT_ORDER)
    return (loss, dx[None], *outs)
```

```python
import math

import numpy as np
import jax
import jax.numpy as jnp
from jax import lax
from jax.experimental import pallas as pl
from jax.experimental.pallas import tpu as pltpu

F32 = jnp.float32
BF16 = jnp.bfloat16

D_MODEL = 1024
HEAD_DIM = 64
N_HEADS = 12
ATTN_W = N_HEADS * HEAD_DIM
N_GROUPS = 4
GMLP_W = N_GROUPS * HEAD_DIM
IN_W = 3 * ATTN_W + 2 * GMLP_W
D_FF = 4 * D_MODEL
CHUNK = 128
DILATIONS = (1, 4, 16)
EPS = 1e-6
Q_SCALE = HEAD_DIM ** -0.5
LOG2E = 1.4426950408889634
NEG = -1e30

ADAM_LR, ADAM_B1, ADAM_B2, ADAM_EPS, ADAM_WD, ADAM_STEP = 0.001, 0.9, 0.999, 1e-08, 0.01, 10

N_DEV = 8
LANES = 128
VMEM_LIMIT = 56 << 20

TM_PROJ = 512
TM_FFN = 512
FF_CHUNK = 512
TM_GMLP = 1024
DW_TILE = (512, 1024, 8192)
DW_TILE_IN = (IN_W // 2, 1024, 2048)

MESH = pl.DeviceIdType.MESH


def _alibi_slopes(n):
    def pow2(m):
        start = 2.0 ** (-8.0 / m)
        return [start ** (i + 1) for i in range(m)]
    c = 2 ** int(math.floor(math.log2(n)))
    s = pow2(n) if c == n else pow2(c) + pow2(2 * c)[0::2][: n - c]
    return np.asarray(s, dtype=np.float32)


SLOPES = _alibi_slopes(N_HEADS)


def _params(sem=None):
    kw = dict(vmem_limit_bytes=VMEM_LIMIT)
    if sem is not None:
        kw["dimension_semantics"] = sem
    return pltpu.CompilerParams(**kw)


def _rows(tm, n):
    return pl.BlockSpec((tm, n), lambda i: (i, 0))


def _resident(shape):
    return pl.BlockSpec(shape, lambda *_: (0,) * len(shape), pipeline_mode=pl.Buffered(1))


def _rms(x):
    r = lax.rsqrt(jnp.mean(x * x, axis=-1, keepdims=True) + EPS)
    return x * r, r


def _rms_bwd(n, r, g, dy):
    dn = dy * g
    return r * (dn - n * jnp.mean(dn * n, axis=-1, keepdims=True))


def _accum_rows(acc_ref, v):
    acc_ref[...] += jnp.broadcast_to(jnp.sum(v, axis=0, keepdims=True), acc_ref.shape)


_G0 = math.sqrt(2.0 / math.pi)
_G1 = 0.044715


def _gelu(x):
    t = jnp.tanh(_G0 * (x + _G1 * (x * x * x)))
    return x * (0.5 * (1.0 + t)), t


def _gelu_grad(x, t):
    return 0.5 * (1.0 + t) + 0.5 * x * (1.0 - t * t) * (_G0 * (1.0 + 3.0 * _G1 * x * x))


NT = (((1,), (1,)), ((), ()))
TN = (((0,), (0,)), ((), ()))


def _dot(a, b, dims=None):
    if dims is None:
        return jnp.dot(a, b, preferred_element_type=F32)
    return lax.dot_general(a, b, dims, preferred_element_type=F32)


def _proj_fwd(x, g1, w_in_t):
    T = x.shape[0]
    tm = TM_PROJ

    def body(x_ref, g_ref, w_ref, hn_ref, q_ref, k_ref, v_ref, u_ref, z_ref):
        n, _ = _rms(x_ref[...])
        hn = (n * g_ref[...]).astype(BF16)
        hn_ref[...] = hn
        a = ATTN_W
        q_ref[...] = _dot(hn, w_ref[0:a, :], NT) * Q_SCALE
        k_ref[...] = _dot(hn, w_ref[a:2 * a, :], NT) * LOG2E
        v_ref[...] = _dot(hn, w_ref[2 * a:3 * a, :], NT)
        u_ref[...] = _dot(hn, w_ref[3 * a:3 * a + GMLP_W, :], NT)
        z_ref[...] = _dot(hn, w_ref[3 * a + GMLP_W:, :], NT)

    sds = jax.ShapeDtypeStruct
    return pl.pallas_call(
        body, name="proj_fwd", grid=(T // tm,),
        in_specs=[_rows(tm, D_MODEL), _resident((1, D_MODEL)), _resident((IN_W, D_MODEL))],
        out_specs=[_rows(tm, D_MODEL), _rows(tm, ATTN_W), _rows(tm, ATTN_W), _rows(tm, ATTN_W),
                   _rows(tm, GMLP_W), _rows(tm, GMLP_W)],
        out_shape=[sds((T, D_MODEL), BF16), sds((T, ATTN_W), F32), sds((T, ATTN_W), F32),
                   sds((T, ATTN_W), F32), sds((T, GMLP_W), F32), sds((T, GMLP_W), F32)],
        compiler_params=_params(("parallel",)),
    )(x, g1, w_in_t)


ATT_TILE = 2048
ATT_BLOCKS = ATT_TILE // CHUNK
SM_BLOCKS = 4


def _slope_table():
    row = np.repeat(SLOPES, HEAD_DIM)
    return jnp.asarray(np.broadcast_to(row[None], (8, ATTN_W)), F32)


def _residue_view(a):
    return a.reshape(a.shape[0] // ATT_BLOCKS, ATT_BLOCKS, a.shape[1])


def _tile_copies(hbm, buf, sem, hp, t, to_hbm=False, lane0=0):
    rows = pl.ds(pl.multiple_of(t * CHUNK, CHUNK), CHUNK)
    lanes = pl.ds(pl.multiple_of(lane0 + hp * LANES, LANES), LANES)
    pairs = [(hbm.at[rows, r, lanes], buf.at[r]) for r in range(ATT_BLOCKS)]
    return [pltpu.make_async_copy(v, h, sem) if to_hbm else pltpu.make_async_copy(h, v, sem) for h, v in pairs]


def _wait_tile(buf, sem):
    pltpu.make_async_copy(buf, buf, sem).wait()


def _residue_rows(d, j):
    if d == 16:
        return [(j, 0, CHUNK)]
    if d == 4:
        return [(j % 4 + 4 * m, 32 * (j // 4), 32) for m in range(4)]
    return [(r, 8 * j, 8) for r in range(ATT_BLOCKS)]


def _block_order(p, d):
    if d == 16:
        return p
    if d == 4:
        return 4 * (p & 31) + (p >> 5)
    return 16 * (p & 7) + (p >> 3)


def _first_in_tile(d, j):
    return _residue_rows(d, j)[0][1] == 0


def _rm_block(buf, d, j):
    return jnp.concatenate([buf[r, lo:lo + n, :] for r, lo, n in _residue_rows(d, j)], axis=0)


def _rm_block_before(buf, buf_before, d, j):
    if _first_in_tile(d, j):
        return jnp.concatenate([buf_before[r, CHUNK - n:CHUNK, :] for r, _, n in _residue_rows(d, j)], axis=0)
    return jnp.concatenate([buf[r, lo - n:lo, :] for r, lo, n in _residue_rows(d, j)], axis=0)


def _rm_store(buf, d, j, val):
    at = 0
    for r, lo, n in _residue_rows(d, j):
        buf[r, lo:lo + n, :] = val[at:at + n, :]
        at += n


def _rm_add(buf, rows, val, first=False):
    at = 0
    for r, lo, n in rows:
        if first:
            buf[r, lo:lo + n, :] = val[at:at + n, :]
        else:
            buf[r, lo:lo + n, :] += val[at:at + n, :]
        at += n


def _residue_bias(sl_ref, d):
    shape = (2 * CHUNK, 2 * CHUNK)
    row = lax.broadcasted_iota(jnp.int32, shape, 0)
    col = lax.broadcasted_iota(jnp.int32, shape, 1)
    steps = _block_order(row & (CHUNK - 1), d) + CHUNK - (_block_order(col & (CHUNK - 1), d) + (col & CHUNK))
    band = (steps >= 0) & (steps <= CHUNK)
    sl = sl_ref[0:1, :]
    upper = lax.broadcasted_iota(jnp.int32, (2 * CHUNK, 1), 0) < CHUNK
    slope2 = jnp.where(upper, sl[:, 0:1], sl[:, HEAD_DIM:HEAD_DIM + 1])
    return jnp.where(band, -(float(d) * LOG2E * slope2 * steps.astype(F32)), NEG)


def _stack_heads(xb, head0):
    zero = jnp.zeros_like(xb)
    return jnp.concatenate([jnp.where(head0, xb, zero), jnp.where(head0, zero, xb)], axis=0).astype(BF16)


def _unstack_heads(x2, head0):
    return jnp.where(head0, x2[:CHUNK, :], x2[CHUNK:, :])


def _attn_fwd(q, k, v, shards=()):
    T = q.shape[0]
    nt = T // ATT_TILE
    ns = len(shards)
    steps = (ATTN_W // LANES) * nt

    def body(sl_ref, q_hbm, k_hbm, v_hbm, *rest):
        x_refs, rest = rest[:ns], rest[ns:]
        attn_hbm, lse_hbm = rest[:2]
        g_refs, rest = rest[2:2 + ns], rest[2 + ns:]
        qbuf, kbuf, vbuf, obuf, lbuf = rest[:5]
        o_acc, l_acc = rest[5:8], rest[8:11]
        sem_q, sem_k, sem_v, sem_o, sem_l = rest[11:16]
        hp, t = pl.program_id(0), pl.program_id(1)
        step = hp * nt + t
        two, three = step % 2, step % 3
        before, after = (step + 2) % 3, (step + 1) % 3
        if ns:
            start, forward, finish = _gather_phases(x_refs, g_refs, *rest[16:])
            pl.when(step == 0)(start)
            pl.when(step == (3 * steps) // 4)(forward)

        def fetch(hp_, t_, two_, three_):
            for cp in (_tile_copies(q_hbm, qbuf.at[two_], sem_q.at[two_], hp_, t_)
                       + _tile_copies(k_hbm, kbuf.at[three_], sem_k.at[three_], hp_, t_)
                       + _tile_copies(v_hbm, vbuf.at[three_], sem_v.at[three_], hp_, t_)):
                cp.start()

        @pl.when(step == 0)
        def _():
            kbuf[2] = jnp.zeros((ATT_BLOCKS, CHUNK, LANES), F32)
            vbuf[2] = jnp.zeros((ATT_BLOCKS, CHUNK, LANES), F32)
            fetch(0, 0, 0, 0)

        @pl.when(step + 1 < steps)
        def _():
            fetch((step + 1) // nt, (step + 1) % nt, 1 - two, after)

        _wait_tile(qbuf.at[two], sem_q.at[two])
        _wait_tile(kbuf.at[three], sem_k.at[three])
        _wait_tile(vbuf.at[three], sem_v.at[three])

        @pl.when(step >= 2)
        def _():
            _wait_tile(obuf.at[two], sem_o.at[two])
            _wait_tile(lbuf.at[two], sem_l.at[two])

        q_t, k_t, v_t = qbuf.at[two], kbuf.at[three], vbuf.at[three]
        k_b, v_b = kbuf.at[before], vbuf.at[before]
        head0 = lax.broadcasted_iota(jnp.int32, (CHUNK, LANES), 1) < HEAD_DIM
        no_key_before = jnp.where(lax.broadcasted_iota(jnp.int32, (2 * CHUNK, 2 * CHUNK), 1) < CHUNK, NEG, 0.0)
        for pi, d in enumerate(DILATIONS):
            bias = _residue_bias(sl_ref, d)

            def scores(j, d=d, bias=bias):
                kcat = jnp.concatenate([_rm_block_before(k_t, k_b, d, j), _rm_block(k_t, d, j)], axis=0).astype(BF16)
                vcat = jnp.concatenate([_rm_block_before(v_t, v_b, d, j), _rm_block(v_t, d, j)], axis=0).astype(BF16)
                s = _dot(_stack_heads(_rm_block(q_t, d, j), head0), kcat, NT)
                return s, vcat, bias_first if _first_in_tile(d, j) else bias

            bias_first = bias + jnp.where(t == 0, 1.0, 0.0) * no_key_before
            for j0 in range(0, ATT_BLOCKS, SM_BLOCKS):
                group = [scores(j) for j in range(j0, j0 + SM_BLOCKS)]
                s = jnp.concatenate([g[0] for g in group], axis=0) + jnp.concatenate([g[2] for g in group], axis=0)
                m = jnp.max(s, axis=-1, keepdims=True)
                p = jnp.exp2(s - m)
                l = jnp.sum(p, axis=-1, keepdims=True)
                p = p.astype(BF16)
                block = lambda a, i: a[i * 2 * CHUNK:(i + 1) * 2 * CHUNK, :]
                o = jnp.concatenate([_dot(block(p, i), g[1]) for i, g in enumerate(group)], axis=0) * (1.0 / l)
                lse = jnp.broadcast_to(m + jnp.log2(l), o.shape)
                for i in range(SM_BLOCKS):
                    _rm_store(o_acc[pi], d, j0 + i, _unstack_heads(block(o, i), head0))
                    _rm_store(l_acc[pi], d, j0 + i, _unstack_heads(block(lse, i), head0))

        for r in range(ATT_BLOCKS):
            a, b, c = l_acc[0][r], l_acc[1][r], l_acc[2][r]
            m = jnp.maximum(jnp.maximum(a, b), c)
            ea, eb, ec = jnp.exp2(a - m), jnp.exp2(b - m), jnp.exp2(c - m)
            tot = ea + eb + ec
            obuf[two, r] = (ea * o_acc[0][r] + eb * o_acc[1][r] + ec * o_acc[2][r]) / tot
            lbuf[two, r] = m + jnp.log2(tot)

        for cp in (_tile_copies(attn_hbm, obuf.at[two], sem_o.at[two], hp, t, to_hbm=True)
                   + _tile_copies(lse_hbm, lbuf.at[two], sem_l.at[two], hp, t, to_hbm=True)):
            cp.start()

        @pl.when(step == steps - 1)
        def _():
            for slot in (two, 1 - two)[:min(steps, 2)]:
                _wait_tile(obuf.at[slot], sem_o.at[slot])
                _wait_tile(lbuf.at[slot], sem_l.at[slot])

        if ns:
            pl.when(step == steps - 1)(finish)

    tile = lambda n: pltpu.VMEM((n, ATT_BLOCKS, CHUNK, LANES), F32)
    dma = lambda n: pltpu.SemaphoreType.DMA((n,))
    view = jax.ShapeDtypeStruct((T // ATT_BLOCKS, ATT_BLOCKS, ATTN_W), F32)
    outs = pl.pallas_call(
        body, name="attn_fwd", grid=(ATTN_W // LANES, nt),
        in_specs=[pl.BlockSpec((8, LANES), lambda c, t: (0, c))] + [_HBM] * (3 + ns),
        out_specs=[_HBM] * (2 + ns),
        out_shape=[view, view] + [_gathered_shape(s) for s in shards],
        scratch_shapes=[tile(2), tile(3), tile(3), tile(2), tile(2)] + [pltpu.VMEM((ATT_BLOCKS, CHUNK, LANES), F32)] * 6
        + [dma(2), dma(3), dma(3), dma(2), dma(2)] + (_gather_sems(ns) if ns else []),
        compiler_params=_params(("arbitrary", "arbitrary")),
    )(_slope_table(), _residue_view(q), _residue_view(k), _residue_view(v), *shards)
    return outs[0].reshape(T, ATTN_W), outs[1].reshape(T, ATTN_W), tuple(outs[2:])


def _group_mean(v, grp):
    halves = []
    for h in range(GMLP_W // LANES):
        x = v[:, h * LANES:(h + 1) * LANES]
        low = grp[:, h * LANES:(h + 1) * LANES] == 2 * h
        a = jnp.sum(jnp.where(low, x, 0.0), axis=-1, keepdims=True)
        b = jnp.sum(jnp.where(low, 0.0, x), axis=-1, keepdims=True)
        halves.append(jnp.where(low, a, b) * (1.0 / HEAD_DIM))
    return jnp.concatenate(halves, axis=1)


def _gmlp_core(uu, zz, lg, lb, ws, sb_ref, grp):
    ug, tu = _gelu(uu)
    zg, tz = _gelu(zz)
    zc = zg - _group_mean(zg, grp)
    rstd = lax.rsqrt(_group_mean(zc * zc, grp) + EPS)
    xhat = zc * rstd
    zn16 = (xhat * lg + lb).astype(BF16)
    low = grp[:CHUNK, :LANES] == 0
    mixed = []
    for ci in range(uu.shape[0] // CHUNK):
        rows = slice(ci * CHUNK, (ci + 1) * CHUNK)
        halves = []
        for h in range(GMLP_W // LANES):
            zh = zn16[rows, h * LANES:(h + 1) * LANES]
            halves.append(jnp.where(low, _dot(ws[2 * h], zh) + sb_ref[:, 2 * h:2 * h + 1],
                                    _dot(ws[2 * h + 1], zh) + sb_ref[:, 2 * h + 1:2 * h + 2]))
        mixed.append(jnp.concatenate(halves, axis=1))
    return ug, tu, tz, xhat, rstd, zn16, jnp.concatenate(mixed, axis=0)


def _causal_ws(w_ref):
    ti = lax.broadcasted_iota(jnp.int32, (CHUNK, CHUNK), 0)
    si = lax.broadcasted_iota(jnp.int32, (CHUNK, CHUNK), 1)
    causal = si <= ti
    return causal, [jnp.where(causal, w_ref[g], 0.0).astype(BF16) for g in range(N_GROUPS)]


def _gmlp_fwd(u, z, ln_g, ln_b, sgu_w, sgu_bt):
    T = u.shape[0]
    tg = TM_GMLP

    def body(u_ref, z_ref, g_ref, b_ref, w_ref, sb_ref, out_ref):
        grp = lax.broadcasted_iota(jnp.int32, (tg, GMLP_W), 1) // HEAD_DIM
        _, ws = _causal_ws(w_ref)
        ug, _, _, _, _, _, mixed = _gmlp_core(u_ref[...], z_ref[...], g_ref[...], b_ref[...], ws, sb_ref, grp)
        out_ref[...] = ug * mixed

    return pl.pallas_call(
        body, name="gmlp_fwd", grid=(T // tg,),
        in_specs=[_rows(tg, GMLP_W), _rows(tg, GMLP_W), _resident((1, GMLP_W)), _resident((1, GMLP_W)),
                  _resident((N_GROUPS, CHUNK, CHUNK)), _resident((CHUNK, N_GROUPS))],
        out_specs=_rows(tg, GMLP_W),
        out_shape=jax.ShapeDtypeStruct((T, GMLP_W), F32),
        compiler_params=_params(("parallel",)),
    )(u, z, ln_g, ln_b, sgu_w, sgu_bt)


def _out_fwd(attn, gm, ga, gg, w_out, x, g2):
    T = x.shape[0]
    tm = TM_PROJ

    def body(a_ref, m_ref, ga_ref, gg_ref, w_ref, x_ref, g2_ref, mix_ref, h1_ref, hn2_ref):
        an, _ = _rms(a_ref[...])
        gn, _ = _rms(m_ref[...])
        an = (an * ga_ref[...]).astype(BF16)
        gn = (gn * gg_ref[...]).astype(BF16)
        mix_ref[:, 0:ATTN_W] = an
        mix_ref[:, ATTN_W:] = gn
        h1 = x_ref[...] + _dot(an, w_ref[0:ATTN_W, :]) + _dot(gn, w_ref[ATTN_W:, :])
        h1_ref[...] = h1
        n2, _ = _rms(h1)
        hn2_ref[...] = (n2 * g2_ref[...]).astype(BF16)

    sds = jax.ShapeDtypeStruct
    return pl.pallas_call(
        body, name="out_fwd", grid=(T // tm,),
        in_specs=[_rows(tm, ATTN_W), _rows(tm, GMLP_W), _resident((1, ATTN_W)), _resident((1, GMLP_W)),
                  _resident((D_MODEL, D_MODEL)), _rows(tm, D_MODEL), _resident((1, D_MODEL))],
        out_specs=[_rows(tm, D_MODEL)] * 3,
        out_shape=[sds((T, D_MODEL), BF16), sds((T, D_MODEL), F32), sds((T, D_MODEL), BF16)],
        compiler_params=_params(("parallel",)),
    )(attn, gm, ga, gg, w_out, x, g2)


def _ffn_fwd(hn2, h1, w1t, w2, gf, tgt):
    T = h1.shape[0]
    tm = TM_FFN

    def body(hn_ref, h1_ref, w1_ref, w2_ref, gf_ref, t_ref, r_ref, dhf_ref, dhb_ref, loss_ref, dgf_ref):
        i = pl.program_id(0)

        @pl.when(i == 0)
        def _():
            loss_ref[...] = jnp.zeros_like(loss_ref)
            dgf_ref[...] = jnp.zeros_like(dgf_ref)

        hn = hn_ref[...]
        acc = h1_ref[...]
        for j in range(D_FF // FF_CHUNK):
            cols = slice(j * FF_CHUNK, (j + 1) * FF_CHUNK)
            r = jnp.maximum(_dot(hn, w1_ref[cols, :], NT), 0.0)
            r_ref[:, cols] = r.astype(BF16)
            act = jnp.square(r).astype(BF16)
            acc = acc + _dot(act, w2_ref[cols, :])
        n3, r3 = _rms(acc)
        gf_row = gf_ref[...]
        e = n3 * gf_row - t_ref[...]
        loss_ref[...] += 0.5 * jnp.sum(jnp.mean(e * e, axis=-1, keepdims=True))
        dy = e * (1.0 / D_MODEL)
        _accum_rows(dgf_ref, dy * n3)
        dh2 = _rms_bwd(n3, r3, gf_row, dy)
        dhf_ref[...] = dh2
        dhb_ref[...] = dh2.astype(BF16)

    sds = jax.ShapeDtypeStruct
    acc_spec = lambda n: pl.BlockSpec((8, n), lambda i: (0, 0))
    return pl.pallas_call(
        body, name="ffn_fwd", grid=(T // tm,),
        in_specs=[_rows(tm, D_MODEL), _rows(tm, D_MODEL), _resident((D_FF, D_MODEL)), _resident((D_FF, D_MODEL)),
                  _resident((1, D_MODEL)), _rows(tm, D_MODEL)],
        out_specs=[_rows(tm, D_FF), _rows(tm, D_MODEL), _rows(tm, D_MODEL), acc_spec(LANES), acc_spec(D_MODEL)],
        out_shape=[sds((T, D_FF), BF16), sds((T, D_MODEL), F32), sds((T, D_MODEL), BF16),
                   sds((8, LANES), F32), sds((8, D_MODEL), F32)],
        compiler_params=_params(("arbitrary",)),
    )(hn2, h1, w1t, w2, gf, tgt)


def _ffn_bwd(dh2b, dh2f, relu, h1, g2, w2, w1t):
    T = h1.shape[0]
    tm = TM_FFN

    def body(db_ref, df_ref, r_ref, h1_ref, g2_ref, w2_ref, w1t_ref, da_ref, d1f_ref, d1b_ref, dg_ref):
        @pl.when(pl.program_id(0) == 0)
        def _():
            dg_ref[...] = jnp.zeros_like(dg_ref)

        db = db_ref[...]
        acc = jnp.zeros((tm, D_MODEL), F32)
        for j in range(D_FF // FF_CHUNK):
            cols = slice(j * FF_CHUNK, (j + 1) * FF_CHUNK)
            da = (_dot(db, w2_ref[cols, :], NT) * (2.0 * r_ref[:, cols].astype(F32))).astype(BF16)
            da_ref[:, cols] = da
            acc = acc + _dot(da, w1t_ref[cols, :])
        n2, r2 = _rms(h1_ref[...])
        _accum_rows(dg_ref, acc * n2)
        dh1 = df_ref[...] + _rms_bwd(n2, r2, g2_ref[...], acc)
        d1f_ref[...] = dh1
        d1b_ref[...] = dh1.astype(BF16)

    sds = jax.ShapeDtypeStruct
    return pl.pallas_call(
        body, name="ffn_bwd", grid=(T // tm,),
        in_specs=[_rows(tm, D_MODEL), _rows(tm, D_MODEL), _rows(tm, D_FF), _rows(tm, D_MODEL),
                  _resident((1, D_MODEL)), _resident((D_FF, D_MODEL)), _resident((D_FF, D_MODEL))],
        out_specs=[_rows(tm, D_FF), _rows(tm, D_MODEL), _rows(tm, D_MODEL),
                   pl.BlockSpec((8, D_MODEL), lambda i: (0, 0))],
        out_shape=[sds((T, D_FF), BF16), sds((T, D_MODEL), F32), sds((T, D_MODEL), BF16), sds((8, D_MODEL), F32)],
        compiler_params=_params(("arbitrary",)),
    )(dh2b, dh2f, relu, h1, g2, w2, w1t)


def _out_bwd(dh1b, w_out, attn, gm, ga, gg):
    T = attn.shape[0]
    tm = TM_PROJ

    def body(d_ref, w_ref, a_ref, m_ref, ga_ref, gg_ref, da_ref, dm_ref, dga_ref, dgg_ref):
        @pl.when(pl.program_id(0) == 0)
        def _():
            dga_ref[...] = jnp.zeros_like(dga_ref)
            dgg_ref[...] = jnp.zeros_like(dgg_ref)

        d = d_ref[...]
        dan = _dot(d, w_ref[0:ATTN_W, :], NT)
        dgn = _dot(d, w_ref[ATTN_W:, :], NT)
        na, ra = _rms(a_ref[...])
        ng, rg = _rms(m_ref[...])
        _accum_rows(dga_ref, dan * na)
        _accum_rows(dgg_ref, dgn * ng)
        da_ref[...] = _rms_bwd(na, ra, ga_ref[...], dan)
        dm_ref[...] = _rms_bwd(ng, rg, gg_ref[...], dgn)

    sds = jax.ShapeDtypeStruct
    return pl.pallas_call(
        body, name="out_bwd", grid=(T // tm,),
        in_specs=[_rows(tm, D_MODEL), _resident((D_MODEL, D_MODEL)), _rows(tm, ATTN_W), _rows(tm, GMLP_W),
                  _resident((1, ATTN_W)), _resident((1, GMLP_W))],
        out_specs=[_rows(tm, ATTN_W), _rows(tm, GMLP_W), pl.BlockSpec((8, ATTN_W), lambda i: (0, 0)),
                   pl.BlockSpec((8, GMLP_W), lambda i: (0, 0))],
        out_shape=[sds((T, ATTN_W), F32), sds((T, GMLP_W), F32), sds((8, ATTN_W), F32), sds((8, GMLP_W), F32)],
        compiler_params=_params(("arbitrary",)),
    )(dh1b, w_out, attn, gm, ga, gg)


def _gmlp_bwd(u, z, dgm, ln_g, ln_b, sgu_w, sgu_bt):
    T = u.shape[0]
    tg = TM_GMLP
    nsteps = T // tg

    def body(u_ref, z_ref, d_ref, g_ref, b_ref, w_ref, sb_ref, dproj_hbm, dlg_ref, dlb_ref, dw_ref, dsb_ref,
             stage, sem):
        i = pl.program_id(0)
        slot = i % 2
        duz_ref = stage.at[slot]

        def to_dproj(step, buf):
            rows = pl.ds(pl.multiple_of(step * tg, tg), tg)
            return pltpu.make_async_copy(stage.at[buf], dproj_hbm.at[rows, pl.ds(3 * ATTN_W, 2 * GMLP_W)],
                                         sem.at[buf])

        @pl.when(i == 0)
        def _():
            for ref in (dlg_ref, dlb_ref, dw_ref, dsb_ref):
                ref[...] = jnp.zeros_like(ref)

        @pl.when(i >= 2)
        def _():
            to_dproj(i - 2, slot).wait()

        grp = lax.broadcasted_iota(jnp.int32, (tg, GMLP_W), 1) // HEAD_DIM
        lane = lax.broadcasted_iota(jnp.int32, (CHUNK, LANES), 1)
        causal, ws = _causal_ws(w_ref)
        lg = g_ref[...]
        uu, zz, dgm = u_ref[...], z_ref[...], d_ref[...]
        ug, tu, tz, xhat, rstd, zn16, mixed = _gmlp_core(uu, zz, lg, b_ref[...], ws, sb_ref, grp)
        dmx = dgm * ug
        duz_ref[:, 0:GMLP_W] = dgm * mixed * _gelu_grad(uu, tu)
        dmx16 = dmx.astype(BF16)
        low = grp[:CHUNK, :LANES] == 0
        dzn = []
        for ci in range(tg // CHUNK):
            rows = slice(ci * CHUNK, (ci + 1) * CHUNK)
            halves = []
            for h in range(GMLP_W // LANES):
                lanes = slice(h * LANES, (h + 1) * LANES)
                dmx_h, zn_h, zero = dmx16[rows, lanes], zn16[rows, lanes], jnp.zeros((CHUNK, LANES), BF16)
                halves.append(jnp.where(low, _dot(ws[2 * h], dmx_h, TN), _dot(ws[2 * h + 1], dmx_h, TN)))
                dw_ref[2 * h] += _dot(jnp.where(low, dmx_h, zero), zn_h, NT)
                dw_ref[2 * h + 1] += _dot(jnp.where(low, zero, dmx_h), zn_h, NT)
            dzn.append(jnp.concatenate(halves, axis=1))
        dzn = jnp.concatenate(dzn, axis=0)
        dsb = jnp.zeros((CHUNK, LANES), F32)
        for g in range(N_GROUPS):
            half = slice((g // 2) * LANES, (g // 2 + 1) * LANES)
            per_token = jnp.sum(jnp.where(grp[:, half] == g, dmx[:, half], 0.0), axis=-1, keepdims=True)
            by_position = sum(per_token[ci * CHUNK:(ci + 1) * CHUNK] for ci in range(tg // CHUNK))
            dsb = jnp.where(lane == g, by_position, dsb)
        dsb_ref[...] += dsb
        _accum_rows(dlg_ref, dzn * xhat)
        _accum_rows(dlb_ref, dzn)
        dxh = dzn * lg
        dzg = rstd * (dxh - _group_mean(dxh, grp) - xhat * _group_mean(dxh * xhat, grp))
        duz_ref[:, GMLP_W:] = dzg * _gelu_grad(zz, tz)
        to_dproj(i, slot).start()

        @pl.when(i == nsteps - 1)
        def _():
            for g in range(N_GROUPS):
                dw_ref[g] = jnp.where(causal, dw_ref[g], 0.0)
            to_dproj(i, slot).wait()
            if nsteps >= 2:
                to_dproj(i - 1, 1 - slot).wait()

    sds = jax.ShapeDtypeStruct
    return pl.pallas_call(
        body, name="gmlp_bwd", grid=(nsteps,),
        in_specs=[_rows(tg, GMLP_W)] * 3 + [_resident((1, GMLP_W)), _resident((1, GMLP_W)),
                                              _resident((N_GROUPS, CHUNK, CHUNK)), _resident((CHUNK, N_GROUPS))],
        out_specs=[_HBM, pl.BlockSpec((8, GMLP_W), lambda i: (0, 0)),
                   pl.BlockSpec((8, GMLP_W), lambda i: (0, 0)),
                   pl.BlockSpec((N_GROUPS, CHUNK, CHUNK), lambda i: (0, 0, 0)),
                   pl.BlockSpec((CHUNK, LANES), lambda i: (0, 0))],
        out_shape=[sds((T, IN_W), F32), sds((8, GMLP_W), F32), sds((8, GMLP_W), F32),
                   sds((N_GROUPS, CHUNK, CHUNK), F32), sds((CHUNK, LANES), F32)],
        scratch_shapes=[pltpu.VMEM((2, tg, 2 * GMLP_W), F32), pltpu.SemaphoreType.DMA((2,))],
        compiler_params=_params(("arbitrary",)),
    )(u, z, dgm, ln_g, ln_b, sgu_w, sgu_bt)


def _attn_bwd(q, k, v, dattn, attn, lse, dproj, owner_grads=()):
    T = q.shape[0]
    nt = T // ATT_TILE
    ns = len(owner_grads)
    steps = (ATTN_W // LANES) * nt

    def body(sl_ref, q_hbm, k_hbm, v_hbm, do_hbm, o_hbm, lse_hbm, _, *rest):
        p_refs, rest = rest[:ns], rest[ns:]
        dq_hbm = dk_hbm = dv_hbm = rest[0]
        r_refs, rest = rest[1:1 + ns], rest[1 + ns:]
        qbuf, dobuf, obuf, lbuf, kbuf, vbuf, dqbuf, dkbuf, dvbuf, delta_s = rest[:10]
        sem_q, sem_do, sem_o, sem_l, sem_k, sem_v, sem_dq, sem_dk, sem_dv = rest[10:19]
        hp, t = pl.program_id(0), pl.program_id(1)
        step = hp * nt + t
        two, three = step % 2, step % 3
        before, after = (step + 2) % 3, (step + 1) % 3
        if ns:
            start, finish = _owner_exchange_phases(p_refs, r_refs, *rest[19:])
            pl.when(step == 0)(start)

        def fetch(hp_, t_, two_, three_):
            for hbm, buf, sem, slot in ((q_hbm, qbuf, sem_q, two_), (do_hbm, dobuf, sem_do, two_),
                                        (o_hbm, obuf, sem_o, two_), (lse_hbm, lbuf, sem_l, two_),
                                        (k_hbm, kbuf, sem_k, three_), (v_hbm, vbuf, sem_v, three_)):
                for cp in _tile_copies(hbm, buf.at[slot], sem.at[slot], hp_, t_):
                    cp.start()

        @pl.when(step == 0)
        def _():
            kbuf[2] = jnp.zeros((ATT_BLOCKS, CHUNK, LANES), F32)
            vbuf[2] = jnp.zeros((ATT_BLOCKS, CHUNK, LANES), F32)
            dkbuf[3] = jnp.zeros((ATT_BLOCKS, CHUNK, LANES), F32)
            dvbuf[3] = jnp.zeros((ATT_BLOCKS, CHUNK, LANES), F32)
            fetch(0, 0, 0, 0)

        @pl.when(step + 1 < steps)
        def _():
            fetch((step + 1) // nt, (step + 1) % nt, 1 - two, after)

        for buf, sem in ((qbuf, sem_q), (dobuf, sem_do), (obuf, sem_o), (lbuf, sem_l)):
            _wait_tile(buf.at[two], sem.at[two])
        _wait_tile(kbuf.at[three], sem_k.at[three])
        _wait_tile(vbuf.at[three], sem_v.at[three])

        @pl.when(step >= 2)
        def _():
            _wait_tile(dqbuf.at[two], sem_dq.at[two])

        @pl.when(step >= 3)
        def _():
            _wait_tile(dkbuf.at[three], sem_dk.at[three])
            _wait_tile(dvbuf.at[three], sem_dv.at[three])

        q_t, do_t, l_t, k_t, v_t = qbuf.at[two], dobuf.at[two], lbuf.at[two], kbuf.at[three], vbuf.at[three]
        k_b, v_b = kbuf.at[before], vbuf.at[before]
        dq_t, dk_t, dv_t = dqbuf.at[two], dkbuf.at[three], dvbuf.at[three]
        dk_b, dv_b = dkbuf.at[before], dvbuf.at[before]
        sink = jnp.where(t > 0, before, 3)
        dk_sink, dv_sink = dkbuf.at[sink], dvbuf.at[sink]
        head0 = lax.broadcasted_iota(jnp.int32, (CHUNK, LANES), 1) < HEAD_DIM
        for r in range(ATT_BLOCKS):
            dd = dobuf[two, r] * obuf[two, r]
            d0 = jnp.sum(jnp.where(head0, dd, 0.0), axis=-1, keepdims=True)
            d1 = jnp.sum(jnp.where(head0, 0.0, dd), axis=-1, keepdims=True)
            delta_s[r] = jnp.where(head0, d0, d1)

        def column(xb):
            return jnp.concatenate([xb[:, 0:1], xb[:, HEAD_DIM:HEAD_DIM + 1]], axis=0)

        no_key_before = jnp.where(lax.broadcasted_iota(jnp.int32, (2 * CHUNK, 2 * CHUNK), 1) < CHUNK, NEG, 0.0)
        for d in DILATIONS:
            bias = _residue_bias(sl_ref, d)
            def scores(j, d=d, bias=bias):
                kcat = jnp.concatenate([_rm_block_before(k_t, k_b, d, j), _rm_block(k_t, d, j)], axis=0).astype(BF16)
                vcat = jnp.concatenate([_rm_block_before(v_t, v_b, d, j), _rm_block(v_t, d, j)], axis=0).astype(BF16)
                q2 = _stack_heads(_rm_block(q_t, d, j), head0)
                do2 = _stack_heads(_rm_block(do_t, d, j), head0)
                return (_dot(q2, kcat, NT), _dot(do2, vcat, NT), column(_rm_block(l_t, d, j)),
                        column(_rm_block(delta_s, d, j)), bias_first if _first_in_tile(d, j) else bias, kcat, q2, do2)

            bias_first = bias + jnp.where(t == 0, 1.0, 0.0) * no_key_before
            group = {}
            for j in range(ATT_BLOCKS):
                if j % SM_BLOCKS == 0:
                    group = {i: scores(i) for i in range(j, j + SM_BLOCKS)}
                    s_all, dp_all, lse_all, delta_all, bias_all = (
                        jnp.concatenate([g[i] for g in group.values()], axis=0) for i in range(5))
                    p_all = jnp.exp2(s_all + bias_all - lse_all)
                    ds_all = (p_all * (dp_all - delta_all)).astype(BF16)
                    p_all = p_all.astype(BF16)
                at = slice((j % SM_BLOCKS) * 2 * CHUNK, (j % SM_BLOCKS + 1) * 2 * CHUNK)
                ds, p16 = ds_all[at, :], p_all[at, :]
                kcat, q2, do2 = group[j][5:]
                first = d == DILATIONS[0]
                _rm_add(dq_t, _residue_rows(d, j), _unstack_heads(_dot(ds, kcat), head0), first)
                ck = _dot(ds, q2, TN)
                cv = _dot(p16, do2, TN)
                _rm_add(dk_t, _residue_rows(d, j), ck[CHUNK:, :], first)
                _rm_add(dv_t, _residue_rows(d, j), cv[CHUNK:, :], first)
                if _first_in_tile(d, j):
                    rows = [(r, CHUNK - n, n) for r, _, n in _residue_rows(d, j)]
                    _rm_add(dk_sink, rows, ck[:CHUNK, :])
                    _rm_add(dv_sink, rows, cv[:CHUNK, :])
                else:
                    rows = [(r, lo - n, n) for r, lo, n in _residue_rows(d, j)]
                    _rm_add(dk_t, rows, ck[:CHUNK, :])
                    _rm_add(dv_t, rows, cv[:CHUNK, :])

        for r in range(ATT_BLOCKS):
            dqbuf[two, r] = dqbuf[two, r] * (Q_SCALE / LOG2E)
        for cp in _tile_copies(dq_hbm, dq_t, sem_dq.at[two], hp, t, to_hbm=True):
            cp.start()

        @pl.when(t > 0)
        def _():
            for cp in (_tile_copies(dk_hbm, dk_b, sem_dk.at[before], hp, t - 1, to_hbm=True, lane0=ATTN_W)
                       + _tile_copies(dv_hbm, dv_b, sem_dv.at[before], hp, t - 1, to_hbm=True, lane0=2 * ATTN_W)):
                cp.start()

        @pl.when(t == nt - 1)
        def _():
            for cp in (_tile_copies(dk_hbm, dk_t, sem_dk.at[three], hp, t, to_hbm=True, lane0=ATTN_W)
                       + _tile_copies(dv_hbm, dv_t, sem_dv.at[three], hp, t, to_hbm=True, lane0=2 * ATTN_W)):
                cp.start()

        @pl.when(step == steps - 1)
        def _():
            for slot in range(2):
                _wait_tile(dqbuf.at[slot], sem_dq.at[slot])
            for slot in range(3):
                _wait_tile(dkbuf.at[slot], sem_dk.at[slot])
                _wait_tile(dvbuf.at[slot], sem_dv.at[slot])

        if ns:
            pl.when(step == steps - 1)(finish)

    tile = lambda n: pltpu.VMEM((n, ATT_BLOCKS, CHUNK, LANES), F32)
    dma = lambda n: pltpu.SemaphoreType.DMA((n,))
    view = jax.ShapeDtypeStruct((T // ATT_BLOCKS, ATT_BLOCKS, ATTN_W), F32)
    outs = pl.pallas_call(
        body, name="attn_bwd", grid=(ATTN_W // LANES, nt),
        in_specs=[pl.BlockSpec((8, LANES), lambda c, t: (0, c))] + [_HBM] * (7 + ns),
        out_specs=[_HBM] * (1 + ns),
        out_shape=[jax.ShapeDtypeStruct((T // ATT_BLOCKS, ATT_BLOCKS, IN_W), F32)]
        + [jax.ShapeDtypeStruct(p.shape, p.dtype) for p in owner_grads],
        scratch_shapes=[tile(2), tile(2), tile(2), tile(2), tile(3), tile(3), tile(2), tile(4), tile(4),
                        pltpu.VMEM((ATT_BLOCKS, CHUNK, LANES), F32)]
        + [dma(2), dma(2), dma(2), dma(2), dma(3), dma(3), dma(2), dma(3), dma(3)]
        + (_owner_exchange_sems(ns) if ns else []),
        input_output_aliases={7: 0},
        compiler_params=_params(("arbitrary", "arbitrary")),
    )(_slope_table(), *[_residue_view(a) for a in (q, k, v, dattn, attn, lse, dproj)], *owner_grads)
    return outs[0].reshape(T, IN_W), tuple(outs[1:])


def _proj_bwd(dproj, w_in_t, x, g1, dh1, owner_grads=(), shards=()):
    T = x.shape[0]
    tm = TM_PROJ
    ns = len(owner_grads)
    ng = len(shards)
    steps = T // tm

    def body(d_ref, w_ref, x_ref, g_ref, r_ref, *rest):
        p_refs, rest = rest[:ns], rest[ns:]
        x_refs, rest = rest[:ng], rest[ng:]
        dx_ref, dg_ref = rest[:2]
        r_refs, rest = rest[2:2 + ns], rest[2 + ns:]
        g_refs, sems = rest[:ng], rest[ng:]
        step = pl.program_id(0)
        if ns:
            start, finish = _owner_exchange_phases(p_refs, r_refs, *sems[:3])
            pl.when(step == 0)(start)
        if ng:
            g_start, g_forward, g_finish = _gather_phases(x_refs, g_refs, *sems[3 if ns else 0:])
            pl.when(step == 0)(g_start)
            pl.when(step == steps // 2)(g_forward)

        @pl.when(step == 0)
        def _():
            dg_ref[...] = jnp.zeros_like(dg_ref)

        dhn = _dot(d_ref[...].astype(BF16), w_ref[...])
        n1, r1 = _rms(x_ref[...])
        _accum_rows(dg_ref, dhn * n1)
        dx_ref[...] = r_ref[...] + _rms_bwd(n1, r1, g_ref[...], dhn)
        if ns:
            pl.when(step == steps - 1)(finish)
        if ng:
            pl.when(step == steps - 1)(g_finish)

    outs = pl.pallas_call(
        body, name="proj_bwd", grid=(steps,),
        in_specs=[_rows(tm, IN_W), _resident((IN_W, D_MODEL)), _rows(tm, D_MODEL), _resident((1, D_MODEL)),
                  _rows(tm, D_MODEL)] + [_HBM] * (ns + ng),
        out_specs=[_rows(tm, D_MODEL), pl.BlockSpec((8, D_MODEL), lambda i: (0, 0))] + [_HBM] * (ns + ng),
        out_shape=[jax.ShapeDtypeStruct((T, D_MODEL), F32), jax.ShapeDtypeStruct((8, D_MODEL), F32)]
        + [jax.ShapeDtypeStruct(p.shape, p.dtype) for p in owner_grads] + [_gathered_shape(s) for s in shards],
        scratch_shapes=(_owner_exchange_sems(ns) if ns else []) + (_gather_sems(ng) if ng else []),
        compiler_params=_params(("arbitrary",)),
    )(dproj, w_in_t, x, g1, dh1, *owner_grads, *shards)
    return outs[0], outs[1], tuple(outs[2:2 + ns]), tuple(outs[2 + ns:])


def _dw(a, b, name, tile, square_a=False, out_dtype=F32, shards=()):
    T, ka = a.shape
    nb = b.shape[1]
    tka, tnb, tt = tile
    tt = min(tt, T)
    last = T // tt - 1
    ns = len(shards)
    grid = (ka // tka, nb // tnb, T // tt)
    steps = grid[0] * grid[1] * grid[2]
    own_acc = out_dtype != F32

    def body(a_ref, b_ref, *refs):
        x_refs, refs = refs[:ns], refs[ns:]
        o_ref = refs[0]
        g_refs, refs = refs[1:1 + ns], refs[1 + ns:]
        acc_ref = refs[0] if own_acc else o_ref
        s = pl.program_id(2)
        if ns:
            step = (pl.program_id(0) * grid[1] + pl.program_id(1)) * grid[2] + s
            start, forward, finish = _gather_phases(x_refs, g_refs, *refs[1 if own_acc else 0:])
            pl.when(step == 0)(start)
            pl.when(step == steps // 2)(forward)

        @pl.when(s == 0)
        def _():
            acc_ref[...] = jnp.zeros_like(acc_ref)

        a_tile = a_ref[...]
        if square_a:
            a_tile = jnp.square(a_tile.astype(F32))
        acc_ref[...] += _dot(a_tile.astype(BF16), b_ref[...], TN)
        if acc_ref is not o_ref:
            @pl.when(s == last)
            def _():
                o_ref[...] = acc_ref[...].astype(out_dtype)
        if ns:
            pl.when(step == steps - 1)(finish)

    outs = pl.pallas_call(
        body, name=name, grid=grid,
        in_specs=[pl.BlockSpec((tt, tka), lambda i, j, s: (s, i)), pl.BlockSpec((tt, tnb), lambda i, j, s: (s, j))]
        + [_HBM] * ns,
        out_specs=[pl.BlockSpec((tka, tnb), lambda i, j, s: (i, j))] + [_HBM] * ns,
        out_shape=[jax.ShapeDtypeStruct((ka, nb), out_dtype)] + [_gathered_shape(s) for s in shards],
        scratch_shapes=([pltpu.VMEM((tka, tnb), F32)] if own_acc else []) + (_gather_sems(ns) if ns else []),
        compiler_params=_params(("arbitrary",) * 3 if ns else ("parallel", "parallel", "arbitrary")),
    )(a, b, *shards)
    return outs if ns else outs[0]


def _adamw_update(w, m, v, g):
    m2 = ADAM_B1 * m + (1.0 - ADAM_B1) * g
    v2 = ADAM_B2 * v + (1.0 - ADAM_B2) * jnp.square(g)
    m_hat = m2 / (1.0 - ADAM_B1 ** ADAM_STEP)
    v_hat = v2 / (1.0 - ADAM_B2 ** ADAM_STEP)
    return -ADAM_LR * (m_hat / (jnp.sqrt(v_hat) + ADAM_EPS) + ADAM_WD * w), m2, v2


def _adamw_tiny(ws, ms, vs, parts):
    n = len(ws)
    P = parts.shape[0]

    def body(*refs):
        w_refs, m_refs, v_refs, p_ref = refs[:n], refs[n:2 * n], refs[2 * n:3 * n], refs[3 * n]
        outs = refs[3 * n + 1:]

        def total(slot, rows):
            g = p_ref[0, 8 * slot:8 * slot + rows, :]
            for i in range(1, P):
                g = g + p_ref[i, 8 * slot:8 * slot + rows, :]
            return g

        for k in range(n):
            g = total(k, ws[k].shape[0])
            outs[4 * k][...] = g
            outs[4 * k + 1][...], outs[4 * k + 2][...], outs[4 * k + 3][...] = _adamw_update(
                w_refs[k][...], m_refs[k][...], v_refs[k][...], g)
        outs[4 * n][...] = total(n, 8)

    sds = jax.ShapeDtypeStruct
    return pl.pallas_call(
        body, name="adamw_tiny",
        out_shape=[sds(w.shape, F32) for w in ws for _ in range(4)] + [sds((8, LANES), F32)],
    )(*ws, *ms, *vs, parts)


def _adamw(w, m, v, parts, name, tr, transposed=False, shards=()):
    R, C = w.shape
    P = parts.shape[0]
    ns = len(shards)
    steps = R // tr

    def body(w_ref, m_ref, v_ref, p_ref, *rest):
        x_refs, rest = rest[:ns], rest[ns:]
        g_ref, d_ref, m2_ref, v2_ref = rest[:4]
        if ns:
            start, finish = _direct_gather_phases(x_refs, rest[4:4 + ns], *rest[4 + ns:])
            pl.when(pl.program_id(0) == 0)(start)
        g = p_ref[0].astype(F32)
        for i in range(1, P):
            g = g + p_ref[i].astype(F32)
        if transposed:
            g = g.T
        g_ref[...] = g
        d_ref[...], m2_ref[...], v2_ref[...] = _adamw_update(w_ref[...], m_ref[...], v_ref[...], g)
        if ns:
            pl.when(pl.program_id(0) == steps - 1)(finish)

    spec = _rows(tr, C)
    part_spec = (pl.BlockSpec((P, C, tr), lambda i: (0, 0, i)) if transposed
                 else pl.BlockSpec((P, tr, C), lambda i: (0, i, 0)))
    return pl.pallas_call(
        body, name=name, grid=(steps,),
        in_specs=[spec, spec, spec, part_spec] + [_HBM] * ns,
        out_specs=[spec] * 4 + [_HBM] * ns,
        out_shape=[jax.ShapeDtypeStruct((R, C), F32)] * 4 + [_gathered_shape(s) for s in shards],
        scratch_shapes=_gather_sems(ns) if ns else [],
        compiler_params=_params(("arbitrary",) if ns else ("parallel",)),
    )(w, m, v, parts, *shards)


_HBM = pl.BlockSpec(memory_space=pltpu.HBM)


def _place():
    return lax.axis_index("x"), lax.axis_index("y"), lax.axis_index("c")


def _gathered_shape(shard):
    return jax.ShapeDtypeStruct((N_DEV,) + shard.shape, shard.dtype)


def _gather_sems(n):
    return [pltpu.SemaphoreType.DMA((7, n)), pltpu.SemaphoreType.DMA((7, n)), pltpu.SemaphoreType.DMA((n,))]


def _gather_phases(x_refs, out_refs, send_sems, recv_sems, local_sems):
    x, y, c = _place()
    me, sibling = (x, y, c), (x, y, 1 - c)
    chips = [(1 - x, y), (x, 1 - y), (1 - x, 1 - y)]
    arrays = range(len(x_refs))

    def slot(i, px, py, pc):
        return out_refs[i].at[4 * px + 2 * py + pc]

    def copy(i, k, block, to, own=False):
        return pltpu.make_async_remote_copy(
            src_ref=x_refs[i] if own else slot(i, *block), dst_ref=slot(i, *block),
            send_sem=send_sems.at[k, i], recv_sem=recv_sems.at[k, i], device_id=to, device_id_type=MESH)

    def mine(i):
        return pltpu.make_async_copy(x_refs[i], slot(i, *me), local_sems.at[i])

    def start():
        for i in arrays:
            mine(i).start()
            copy(i, 0, me, sibling, own=True).start()
            for j, chip in enumerate(chips):
                copy(i, 1 + j, me, (*chip, c), own=True).start()

    def forward():
        for i in arrays:
            for j, chip in enumerate(chips):
                copy(i, 1 + j, (*chip, c), me).wait_recv()
                copy(i, 4 + j, (*chip, c), sibling).start()

    def finish():
        for i in arrays:
            copy(i, 0, sibling, me).wait_recv()
            copy(i, 0, me, sibling, own=True).wait_send()
            for j, chip in enumerate(chips):
                copy(i, 4 + j, (*chip, 1 - c), me).wait_recv()
                copy(i, 1 + j, me, (*chip, c), own=True).wait_send()
                copy(i, 4 + j, (*chip, c), sibling).wait_send()
            mine(i).wait()

    return start, forward, finish


def _direct_gather_phases(x_refs, out_refs, send_sems, recv_sems, local_sems):
    x, y, c = _place()
    me = 4 * x + 2 * y + c
    flip = lambda v, bit: 1 - v if bit else v
    peers = [(flip(x, k & 4), flip(y, k & 2), flip(c, k & 1)) for k in range(1, N_DEV)]
    arrays = range(len(x_refs))

    def mine(i):
        return pltpu.make_async_copy(x_refs[i], out_refs[i].at[me], local_sems.at[i])

    def copy(i, k, slot):
        return pltpu.make_async_remote_copy(
            src_ref=x_refs[i], dst_ref=out_refs[i].at[slot],
            send_sem=send_sems.at[k, i], recv_sem=recv_sems.at[k, i], device_id=peers[k], device_id_type=MESH)

    def start():
        for i in arrays:
            mine(i).start()
            for k in range(N_DEV - 1):
                copy(i, k, me).start()

    def finish():
        for i in arrays:
            for k, (px, py, pc) in enumerate(peers):
                copy(i, k, 4 * px + 2 * py + pc).wait_recv()
                copy(i, k, me).wait_send()
            mine(i).wait()

    return start, finish


def _all_gather(shards, name):
    n = len(shards)

    def body(*refs):
        start, forward, finish = _gather_phases(refs[:n], refs[n:2 * n], *refs[2 * n:])
        start()
        forward()
        finish()

    return pl.pallas_call(
        body, name=name,
        out_shape=[_gathered_shape(s) for s in shards],
        in_specs=[_HBM] * n, out_specs=[_HBM] * n,
        scratch_shapes=_gather_sems(n),
    )(*shards)


def _owner_exchange_sems(n):
    return [pltpu.SemaphoreType.DMA((7, n)), pltpu.SemaphoreType.DMA((7, n)), pltpu.SemaphoreType.DMA((n,))]


def _owner_exchange_phases(g_refs, r_refs, send_sems, recv_sems, local_sems):
    x, y, c = _place()
    me = 4 * x + 2 * y + c
    flip = lambda v, bit: 1 - v if bit else v
    peers = [(flip(x, k & 4), flip(y, k & 2), flip(c, k & 1)) for k in range(1, N_DEV)]
    arrays = range(len(g_refs))

    def mine(i):
        return pltpu.make_async_copy(g_refs[i].at[me], r_refs[i].at[me], local_sems.at[i])

    def copy(i, k, src_slot, dst_slot):
        return pltpu.make_async_remote_copy(
            src_ref=g_refs[i].at[src_slot], dst_ref=r_refs[i].at[dst_slot],
            send_sem=send_sems.at[k, i], recv_sem=recv_sems.at[k, i], device_id=peers[k], device_id_type=MESH)

    def start():
        for i in arrays:
            mine(i).start()
            for k, (px, py, pc) in enumerate(peers):
                copy(i, k, 4 * px + 2 * py + pc, me).start()

    def finish():
        for i in arrays:
            for k, (px, py, pc) in enumerate(peers):
                copy(i, k, me, 4 * px + 2 * py + pc).wait_recv()
                copy(i, k, 4 * px + 2 * py + pc, me).wait_send()
            mine(i).wait()

    return start, finish


def _local_step(x, tgt, small, w_in_t, rest, exchange=False):
    g1, g2, gf = small["norm1_g"], small["norm2_g"], small["final_norm_g"].reshape(1, D_MODEL)
    ga, gg = small["attn_out_g"], small["gmlp_out_g"]
    ln_g = small["sgu_ln_g"].reshape(1, GMLP_W)
    ln_b = small["sgu_ln_b"].reshape(1, GMLP_W)
    sgu_w = small["sgu_w"][0]
    sgu_bt = small["sgu_b"][0].T

    hn1, q, k, v, u, z = _proj_fwd(x, g1, w_in_t)
    attn, lse, gathered = _attn_fwd(q, k, v, shards=rest if exchange else ())
    w_out, w_ff1_t, w_ff2 = [g.reshape(-1, D_MODEL) for g in gathered] if exchange else rest
    gm = _gmlp_fwd(u, z, ln_g, ln_b, sgu_w, sgu_bt)
    mixed, h1, hn2 = _out_fwd(attn, gm, ga, gg, w_out, x, g2)
    relu, dh2f, dh2b, loss8, dgf8 = _ffn_fwd(hn2, h1, w_ff1_t, w_ff2, gf, tgt)

    da, dh1f, dh1b, dg2 = _ffn_bwd(dh2b, dh2f, relu, h1, g2, w_ff2, w_ff1_t)
    wire = BF16 if exchange else F32
    dw_ff2 = _dw(relu, dh2b, "dw_ff2", DW_TILE, square_a=True, out_dtype=wire)
    dw_ff1_t = _dw(da, hn2, "dw_ff1", DW_TILE, out_dtype=wire)
    dattn, dgm, dga, dgg = _out_bwd(dh1b, w_out, attn, gm, ga, gg)
    dw_out = _dw(mixed, dh1b, "dw_out", DW_TILE, out_dtype=wire)
    early = [dw_out, dw_ff1_t, dw_ff2]
    if exchange:
        early = [g.reshape(N_DEV, -1, D_MODEL) for g in early]
    dproj, dlg, dlb, dsw, dsb = _gmlp_bwd(u, z, dgm, ln_g, ln_b, sgu_w, sgu_bt)
    dproj, arrived = _attn_bwd(q, k, v, dattn, attn, lse, dproj, owner_grads=early if exchange else ())
    dw_in_t = _dw(dproj, hn1, "dw_in", DW_TILE_IN, out_dtype=wire)
    late = (dw_in_t.reshape(N_DEV, -1, D_MODEL),) if exchange else ()
    dx, dg1, late, sgu_parts = _proj_bwd(dproj, w_in_t, x, g1, dh1f, owner_grads=late,
                                         shards=[_as_rows(dsw).astype(BF16)] if exchange else ())
    if exchange:
        dw_in_t, early = late[0], arrived
        sgu_parts, = sgu_parts

    small_grads = dict(
        norm1_g=dg1[0], sgu_ln_g=dlg[0], sgu_ln_b=dlb[0], sgu_w=dsw, sgu_b=dsb[:, :N_GROUPS].T,
        attn_out_g=dga[0], gmlp_out_g=dgg[0], norm2_g=dg2[0], final_norm_g=dgf8[0])
    if exchange:
        small_grads["sgu_w_parts"] = sgu_parts
    return loss8[0, 0], dx, (dw_in_t, *early), small_grads


SMALL_NAMES = ("norm1_g", "sgu_ln_g", "sgu_ln_b", "sgu_w", "sgu_b", "attn_out_g", "gmlp_out_g", "norm2_g",
               "final_norm_g")
WEIGHT_ORDER = ("norm1_g", "w_in", "sgu_ln_g", "sgu_ln_b", "sgu_w", "sgu_b", "attn_out_g", "gmlp_out_g", "w_out",
                "norm2_g", "w_ff1", "w_ff2", "final_norm_g")


TINY_NAMES = tuple(n for n in SMALL_NAMES if n != "sgu_w")


def _as_rows(a):
    return a.reshape(-1, LANES)


def _pack_tiny_grads(d, loss):
    slots = [jnp.pad(_as_rows(d[n]), ((0, 8 - d[n].size // LANES), (0, 0))) for n in TINY_NAMES]
    return jnp.concatenate(slots + [jnp.full((8, LANES), loss, F32)], axis=0)


def kernel(x, norm1_g, w_in, sgu_ln_g, sgu_ln_b, sgu_w, sgu_b, attn_out_g, gmlp_out_g, w_out, norm2_g, w_ff1, w_ff2, final_norm_g, loss_target, m_norm1_g, m_w_in, m_sgu_ln_g, m_sgu_ln_b, m_sgu_w, m_sgu_b, m_attn_out_g, m_gmlp_out_g, m_w_out, m_norm2_g, m_w_ff1, m_w_ff2, m_final_norm_g, v_norm1_g, v_w_in, v_sgu_ln_g, v_sgu_ln_b, v_sgu_w, v_sgu_b, v_attn_out_g, v_gmlp_out_g, v_w_out, v_norm2_g, v_w_ff1, v_w_ff2, v_final_norm_g):
    w = dict(norm1_g=norm1_g, w_in=w_in, sgu_ln_g=sgu_ln_g, sgu_ln_b=sgu_ln_b, sgu_w=sgu_w, sgu_b=sgu_b,
             attn_out_g=attn_out_g, gmlp_out_g=gmlp_out_g, w_out=w_out, norm2_g=norm2_g, w_ff1=w_ff1, w_ff2=w_ff2,
             final_norm_g=final_norm_g)
    m = dict(norm1_g=m_norm1_g, w_in=m_w_in, sgu_ln_g=m_sgu_ln_g, sgu_ln_b=m_sgu_ln_b, sgu_w=m_sgu_w, sgu_b=m_sgu_b,
             attn_out_g=m_attn_out_g, gmlp_out_g=m_gmlp_out_g, w_out=m_w_out, norm2_g=m_norm2_g, w_ff1=m_w_ff1,
             w_ff2=m_w_ff2, final_norm_g=m_final_norm_g)
    v = dict(norm1_g=v_norm1_g, w_in=v_w_in, sgu_ln_g=v_sgu_ln_g, sgu_ln_b=v_sgu_ln_b, sgu_w=v_sgu_w, sgu_b=v_sgu_b,
             attn_out_g=v_attn_out_g, gmlp_out_g=v_gmlp_out_g, w_out=v_w_out, norm2_g=v_norm2_g, w_ff1=v_w_ff1,
             w_ff2=v_w_ff2, final_norm_g=v_final_norm_g)
    big = ("w_in", "w_out", "w_ff1", "w_ff2")

    w_in_t, = _all_gather([w_in[0].T.astype(BF16)], "w_in_all_gather")
    rest = (w_out[0].astype(BF16), w_ff1[0].T.astype(BF16), w_ff2[0].astype(BF16))
    loss, dx, parts, small_grads = _local_step(x[0], loss_target[0], {n: w[n] for n in SMALL_NAMES},
                                               w_in_t.reshape(IN_W, D_MODEL), rest, exchange=True)

    small_parts = [_pack_tiny_grads(small_grads, loss)]
    sgu_parts = small_grads["sgu_w_parts"]
    new = {}
    for n, p, transposed, tr in zip(big, parts, (True, False, True, False), (128, 128, 128, 256)):
        res = _adamw(w[n][0], m[n][0], v[n][0], p, "adamw_" + n, tr, transposed,
                     shards=small_parts if n == "w_in" else ())
        new[n] = [a[None] for a in res[:4]]
        if n == "w_in":
            tiny_parts, = res[4:]
    tiny = _adamw_tiny(*[[_as_rows(src[n]) for n in TINY_NAMES] for src in (w, m, v)], tiny_parts)
    sgu = _adamw(_as_rows(sgu_w), _as_rows(m_sgu_w), _as_rows(v_sgu_w), sgu_parts, "adamw_sgu_w", 512)
    loss = tiny[-1][0, 0]

    outs = []
    for i in range(4):
        d = {n: new[n][i] for n in big}
        d.update({n: tiny[4 * k + i].reshape(w[n].shape) for k, n in enumerate(TINY_NAMES)})
        d["sgu_w"] = sgu[i].reshape(sgu_w.shape)
        outs.extend(d[n] for n in WEIGHT_ORDER)
    return (loss, dx[None], *outs)
```
